```python
import math
import jax, jax.numpy as jnp
from jax import lax
import numpy as np

D_MODEL = 1024
BATCH = 2
SEQ = 16384
DEPTH = 1

D_MIX = D_MODEL
HGRN_WIDTH = (3 * D_MIX) // 4
HGRN_DK = 128
HGRN_HEADS = HGRN_WIDTH // HGRN_DK
HGRN_DV = HGRN_WIDTH // HGRN_HEADS
HGRN_CHUNK = 64
ATTN_WIDTH = D_MIX - HGRN_WIDTH
ATTN_HEAD_DIM = 64
ATTN_HEADS = ATTN_WIDTH // ATTN_HEAD_DIM
ROT_DIM = ATTN_HEAD_DIM // 4
ROPE_THETA = 500000.0
MOBA_BLOCK = 256
MOBA_TOPK = 3
Q_BLOCK = 128
N_EXPERTS = 32
TOP_K = 4
D_FF = D_MODEL
SWIGLU_ALPHA = 1.702
SWIGLU_LIMIT = 7.0
MOE_BLOCK = 256
N_MOD = 6
RMS_EPS = 1e-6
D_PROJ = 4 * HGRN_WIDTH + 3 * ATTN_WIDTH

kernel_name = "hymba_hgrn2_moba_moe_block"


def rms_norm(x, g, eps=RMS_EPS):
    xf = x.astype(jnp.float32)
    y = xf * lax.rsqrt(jnp.mean(xf * xf, axis=-1, keepdims=True) + eps)
    return (y * g.astype(jnp.float32)).astype(x.dtype)


def split_heads(t, n_heads):
    B, S, W = t.shape
    return t.reshape(B, S, n_heads, W // n_heads).transpose(0, 2, 1, 3)


def merge_heads(t):
    B, H, S, d = t.shape
    return t.transpose(0, 2, 1, 3).reshape(B, S, H * d)


def partial_rotary(t, cos, sin):
    half = ROT_DIM // 2
    t1 = t[..., :half].astype(jnp.float32)
    t2 = t[..., half:ROT_DIM].astype(jnp.float32)
    rot = jnp.concatenate([t1 * cos - t2 * sin, t2 * cos + t1 * sin], axis=-1).astype(t.dtype)
    return jnp.concatenate([rot, t[..., ROT_DIM:]], axis=-1)


def hgrn2_recurrence(q, k, v, log_f):
    B, H, S, dk = q.shape
    dv = v.shape[-1]
    n_chunks = S // HGRN_CHUNK

    def chunked(t):
        return jnp.moveaxis(t.reshape(B, H, n_chunks, HGRN_CHUNK, t.shape[-1]), 2, 0)

    qc, kc, vc = chunked(q), chunked(k), chunked(v)
    bc = jnp.cumsum(chunked(log_f), axis=3)
    causal = jnp.tril(jnp.ones((HGRN_CHUNK, HGRN_CHUNK), dtype=bool))[:, :, None]

    def step(state, inp):
        q_c, k_c, v_c, b_c = inp
        b_last = b_c[:, :, -1:, :]
        o_inter = jnp.einsum('bhtk,bhkv->bhtv', q_c * jnp.exp(b_c), state)
        diff = b_c[:, :, :, None, :] - b_c[:, :, None, :, :]
        decay = jnp.exp(jnp.where(causal, diff, -jnp.inf))
        scores = jnp.einsum('bhtk,bhtsk,bhsk->bhts', q_c, decay, k_c)
        o_intra = jnp.einsum('bhts,bhsv->bhtv', scores, v_c)
        new_state = (jnp.exp(b_last[:, :, 0, :])[..., None] * state
                     + jnp.einsum('bhsk,bhsv->bhkv', k_c * jnp.exp(b_last - b_c), v_c))
        return new_state, o_inter + o_intra

    state0 = jnp.zeros((B, H, dk, dv), jnp.float32)
    _, o = lax.scan(step, state0, (qc, kc, vc, bc))
    return jnp.moveaxis(o, 0, 2).reshape(B, H, S, dv)


def moba_attention(q, k, v):
    B, H, S, hd = q.shape
    n_blocks = -(-S // MOBA_BLOCK)
    pad = n_blocks * MOBA_BLOCK - S
    kb = jnp.pad(k, ((0, 0), (0, 0), (0, pad), (0, 0))).reshape(B, H, n_blocks, MOBA_BLOCK, hd)
    vb = jnp.pad(v, ((0, 0), (0, 0), (0, pad), (0, 0))).reshape(B, H, n_blocks, MOBA_BLOCK, hd)
    k_mean = jnp.mean(kb.astype(jnp.float32), axis=3)
    topk = min(MOBA_TOPK, n_blocks)
    n_q = S // Q_BLOCK
    q_blocks = jnp.moveaxis(q.reshape(B, H, n_q, Q_BLOCK, hd), 2, 0)
    b_idx = jnp.arange(B)[:, None, None, None]
    h_idx = jnp.arange(H)[None, :, None, None]
    scale = hd ** -0.5

    def one_block(args):
        qb, n = args
        q_pos = n * Q_BLOCK + jnp.arange(Q_BLOCK)
        own = (n * Q_BLOCK) // MOBA_BLOCK
        gate = jnp.einsum('bhqd,bhnd->bhqn', qb.astype(jnp.float32), k_mean)
        gate = jnp.where(jnp.arange(n_blocks) < own, gate, -jnp.inf)
        _, sel = lax.top_k(gate, topk)
        valid = jnp.arange(topk) < own
        k_sel = kb[b_idx, h_idx, sel]
        v_sel = vb[b_idx, h_idx, sel]
        s_sel = jnp.einsum('bhqd,bhqnkd->bhqnk', qb, k_sel).astype(jnp.float32) * scale
        s_sel = jnp.where(valid[:, None], s_sel, -jnp.inf).reshape(B, H, Q_BLOCK, topk * MOBA_BLOCK)
        k_own = lax.dynamic_index_in_dim(kb, own, axis=2, keepdims=False)
        v_own = lax.dynamic_index_in_dim(vb, own, axis=2, keepdims=False)
        k_pos = own * MOBA_BLOCK + jnp.arange(MOBA_BLOCK)
        s_own = jnp.einsum('bhqd,bhkd->bhqk', qb, k_own).astype(jnp.float32) * scale
        s_own = jnp.where(k_pos[None, :] <= q_pos[:, None], s_own, -jnp.inf)
        p = jax.nn.softmax(jnp.concatenate([s_sel, s_own], axis=-1), axis=-1).astype(q.dtype)
        p_sel = p[..., :topk * MOBA_BLOCK].reshape(B, H, Q_BLOCK, topk, MOBA_BLOCK)
        p_own = p[..., topk * MOBA_BLOCK:]
        return (jnp.einsum('bhqnk,bhqnkd->bhqd', p_sel, v_sel)
                + jnp.einsum('bhqk,bhkd->bhqd', p_own, v_own))

    out = lax.map(one_block, (q_blocks, jnp.arange(n_q)))
    return jnp.moveaxis(out, 0, 2).reshape(B, H, S, hd)


def moe_ffn(h, w_router, b_router, w_gate_up, b_gate_up, w_down, b_down):
    T, D = h.shape
    n_assign = T * TOP_K
    logits = (h @ w_router + b_router).astype(jnp.float32)
    top_val, top_idx = lax.top_k(logits, TOP_K)
    gates = jax.nn.softmax(top_val, axis=-1).astype(h.dtype)
    flat_e = top_idx.reshape(-1)
    flat_tok = jnp.arange(n_assign, dtype=jnp.int32) // TOP_K
    order = jnp.argsort(flat_e)
    sorted_e = flat_e[order]
    sorted_tok = flat_tok[order]
    sorted_gate = gates.reshape(-1)[order]
    counts = jax.ops.segment_sum(jnp.ones_like(flat_e), flat_e, num_segments=N_EXPERTS)
    group_start = jnp.cumsum(counts) - counts
    padded = (counts + MOE_BLOCK - 1) // MOE_BLOCK * MOE_BLOCK
    pad_end = jnp.cumsum(padded)
    pad_start = pad_end - padded
    dest = pad_start[sorted_e] + jnp.arange(n_assign, dtype=jnp.int32) - group_start[sorted_e]
    n_blk = -(-n_assign // MOE_BLOCK) + N_EXPERTS
    tok_buf = jnp.full((n_blk * MOE_BLOCK,), T, jnp.int32).at[dest].set(sorted_tok)
    gate_buf = jnp.zeros((n_blk * MOE_BLOCK,), h.dtype).at[dest].set(sorted_gate)
    blk_expert = jnp.minimum(
        jnp.searchsorted(pad_end, jnp.arange(n_blk, dtype=jnp.int32) * MOE_BLOCK, side='right'),
        N_EXPERTS - 1)
    h_pad = jnp.concatenate([h, jnp.zeros((1, D), h.dtype)], axis=0)

    def expert_block(args):
        toks, g, e = args
        xb = h_pad[toks]
        gu = xb @ w_gate_up[e] + b_gate_up[e]
        gate = jnp.minimum(gu[:, :D_FF], SWIGLU_LIMIT)
        up = jnp.clip(gu[:, D_FF:], -SWIGLU_LIMIT, SWIGLU_LIMIT)
        act = (up + 1.0) * gate * jax.nn.sigmoid(SWIGLU_ALPHA * gate)
        return (act @ w_down[e] + b_down[e]) * g[:, None]

    y = lax.map(expert_block, (tok_buf.reshape(n_blk, MOE_BLOCK),
                               gate_buf.reshape(n_blk, MOE_BLOCK), blk_expert))
    return jax.ops.segment_sum(y.reshape(-1, D), tok_buf, num_segments=T + 1)[:T]


def setup_inputs(seed: int = 0) -> dict:
    key = jax.random.key(seed)
    ks = jax.random.split(key, 20)
    f32 = jnp.float32
    nrm = lambda k, shape, s: jax.random.normal(k, shape, f32) * s
    x = jax.random.normal(ks[0], (BATCH, SEQ, D_MODEL), f32)
    c = jax.random.normal(ks[1], (BATCH, D_MODEL), f32)
    offset = jax.random.randint(ks[2], (BATCH, 1), 0, 4096, dtype=jnp.int32)
    positions = offset + jnp.arange(SEQ, dtype=jnp.int32)[None, :]
    return {
        "x": x,
        "c": c,
        "positions": positions,
        "w_ada": nrm(ks[3], (DEPTH, D_MODEL, N_MOD * D_MODEL), 0.5 * D_MODEL ** -0.5),
        "b_ada": nrm(ks[4], (DEPTH, N_MOD * D_MODEL), 0.02),
        "norm1_g": 1.0 + nrm(ks[5], (DEPTH, D_MODEL), 0.02),
        "w_in": nrm(ks[6], (DEPTH, D_MODEL, D_PROJ), D_MODEL ** -0.5),
        "hgrn_lb_logits": nrm(ks[7], (DEPTH + 1, HGRN_WIDTH), 0.5),
        "hgrn_norm_g": 1.0 + nrm(ks[8], (DEPTH, HGRN_DV), 0.02),
        "attn_norm_g": 1.0 + nrm(ks[9], (DEPTH, ATTN_HEAD_DIM), 0.02),
        "w_out": nrm(ks[10], (DEPTH, D_MIX, D_MODEL), D_MIX ** -0.5),
        "norm2_g": 1.0 + nrm(ks[11], (DEPTH, D_MODEL), 0.02),
        "w_router": nrm(ks[12], (DEPTH, D_MODEL, N_EXPERTS), D_MODEL ** -0.5),
        "b_router": nrm(ks[13], (DEPTH, N_EXPERTS), 0.01),
        "w_gate_up": nrm(ks[14], (DEPTH, N_EXPERTS, D_MODEL, 2 * D_FF), D_MODEL ** -0.5),
        "b_gate_up": nrm(ks[15], (DEPTH, N_EXPERTS, 2 * D_FF), 0.02),
        "w_down": nrm(ks[16], (DEPTH, N_EXPERTS, D_FF, D_MODEL), D_FF ** -0.5),
        "b_down": nrm(ks[17], (DEPTH, N_EXPERTS, D_MODEL), 0.02),
        "final_norm_g": 1.0 + nrm(ks[18], (D_MODEL,), 0.02),
    }


def reference(x, c, positions, w_ada, b_ada, norm1_g, w_in, hgrn_lb_logits, hgrn_norm_g,
              attn_norm_g, w_out, norm2_g, w_router, b_router, w_gate_up, b_gate_up,
              w_down, b_down, final_norm_g):
    B, S, D = x.shape
    inv_freq = jnp.exp(-math.log(ROPE_THETA) * jnp.arange(0, ROT_DIM, 2, dtype=jnp.float32) / ROT_DIM)
    ang = positions.astype(jnp.float32)[:, None, :, None] * inv_freq
    cos, sin = jnp.cos(ang), jnp.sin(ang)
    lower_bounds = jnp.cumsum(jax.nn.softmax(hgrn_lb_logits.astype(jnp.float32), axis=0), axis=0)
    splits = [HGRN_WIDTH, 2 * HGRN_WIDTH, 3 * HGRN_WIDTH, 4 * HGRN_WIDTH,
              4 * HGRN_WIDTH + ATTN_WIDTH, 4 * HGRN_WIDTH + 2 * ATTN_WIDTH]
    for l in range(DEPTH):
        mod = jax.nn.silu(c) @ w_ada[l] + b_ada[l]
        shift1, scale1, gate1, shift2, scale2, gate2 = jnp.split(mod[:, None, :], N_MOD, axis=-1)

        h = rms_norm(x, norm1_g[l]) * (1.0 + scale1) + shift1
        proj = h @ w_in[l]
        hq, hf, hi, hg, aq, ak, av = jnp.split(proj, splits, axis=-1)

        lb = lower_bounds[l]
        f = lb + (1.0 - lb) * jax.nn.sigmoid(hf.astype(jnp.float32))
        q_h = split_heads(jax.nn.silu(hq.astype(jnp.float32)) * HGRN_DK ** -0.5, HGRN_HEADS)
        k_h = split_heads(1.0 - f, HGRN_HEADS)
        v_h = split_heads(hi.astype(jnp.float32), HGRN_HEADS)
        o_h = hgrn2_recurrence(q_h, k_h, v_h, split_heads(jnp.log(f), HGRN_HEADS)).astype(x.dtype)
        o_h = rms_norm(o_h, hgrn_norm_g[l]) * jax.nn.silu(split_heads(hg, HGRN_HEADS))

        q_a = partial_rotary(split_heads(aq, ATTN_HEADS), cos, sin)
        k_a = partial_rotary(split_heads(ak, ATTN_HEADS), cos, sin)
        v_a = split_heads(av, ATTN_HEADS)
        o_a = rms_norm(moba_attention(q_a, k_a, v_a), attn_norm_g[l])

        mix = jnp.concatenate([merge_heads(o_h), merge_heads(o_a)], axis=-1) @ w_out[l]
        x = x + gate1 * mix

        h2 = rms_norm(x, norm2_g[l]) * (1.0 + scale2) + shift2
        y = moe_ffn(h2.reshape(B * S, D), w_router[l], b_router[l], w_gate_up[l],
                    b_gate_up[l], w_down[l], b_down[l]).reshape(B, S, D)
        x = x + gate2 * y
    return rms_norm(x, final_norm_g)
```

```python
import functools
import math

import jax
import jax.numpy as jnp
from jax import lax
from jax.experimental import pallas as pl
from jax.experimental.pallas import tpu as pltpu

F32 = jnp.float32
BF16 = jnp.bfloat16
HIGHEST = lax.Precision.HIGHEST

HGRN_DK = 128
HGRN_CHUNK = 64
ATTN_HEADS = 4
ATTN_HEAD_DIM = 64
ROT_DIM = ATTN_HEAD_DIM // 4
ROPE_THETA = 500000.0
MOBA_BLOCK = 256
MOBA_TOPK = 3
N_EXPERTS = 32
TOP_K = 4
SWIGLU_ALPHA = 1.702
SWIGLU_LIMIT = 7.0
N_MOD = 6
RMS_EPS = 1e-6

MASK_BIAS = -1e30
EXP_CLAMP = 80.0
MOE_ROWS = 256
COMBINE_TOKENS = 128
VMEM_LIMIT = 56 * 1024 * 1024


def _sigmoid(x):
    return 1.0 / (1.0 + jnp.exp(-x))


def _dot(a, b, **kw):
    return jnp.dot(a, b, preferred_element_type=F32, **kw)


def _dot_nt(a, b, **kw):
    return lax.dot_general(a, b, (((1,), (1,)), ((), ())), preferred_element_type=F32, **kw)


def _ada_kernel(c_ref, w_ref, b_ref, o_ref):
    c = c_ref[...]
    o_ref[...] = _dot(c * _sigmoid(c), w_ref[...], precision=HIGHEST) + b_ref[...]


def _ada(c, w_ada, b_ada):
    B, D = c.shape
    N = w_ada.shape[1]
    tn = N // 4
    c8 = jnp.zeros((8, D), F32).at[:B].set(c)
    out = pl.pallas_call(
        _ada_kernel,
        out_shape=jax.ShapeDtypeStruct((8, N), F32),
        grid=(N // tn,),
        in_specs=[pl.BlockSpec((8, D), lambda j: (0, 0)),
                  pl.BlockSpec((D, tn), lambda j: (0, j)),
                  pl.BlockSpec((1, tn), lambda j: (0, j))],
        out_specs=pl.BlockSpec((8, tn), lambda j: (0, j)),
        compiler_params=pltpu.CompilerParams(vmem_limit_bytes=VMEM_LIMIT),
        name="ada",
    )(c8, w_ada, b_ada.reshape(1, N))
    return out[:B]


def _proj_kernel(x_ref, sc_ref, sh_ref, g_ref, w_ref, lb_ref, ct_ref, st_ref,
                 q_ref, k_ref, lf_ref, v_ref, gt_ref, aq_ref, ka_ref, av_ref, km_ref,
                 *, hw, aw, n_blocks):
    x = x_ref[...]
    ms = jnp.mean(x * x, axis=-1, keepdims=True)
    h = x * lax.rsqrt(ms + RMS_EPS) * g_ref[...]
    h = h * (1.0 + sc_ref[...]) + sh_ref[...]
    proj = _dot(h.astype(BF16), w_ref[...])

    hq = proj[:, 0:hw]
    hf = proj[:, hw:2 * hw]
    hg = proj[:, 3 * hw:4 * hw]
    q_ref[...] = hq * _sigmoid(hq) * (HGRN_DK ** -0.5)
    lb = lb_ref[...]
    f = lb + (1.0 - lb) * _sigmoid(hf)
    k_ref[...] = 1.0 - f
    lf_ref[...] = jnp.log(f)
    v_ref[...] = proj[:, 2 * hw:3 * hw]
    gt_ref[...] = hg * _sigmoid(hg)

    ct = jnp.concatenate([ct_ref[...]] * (aw // 128), axis=1)
    st = jnp.concatenate([st_ref[...]] * (aw // 128), axis=1)
    lane = lax.broadcasted_iota(jnp.int32, ct.shape, 1) % ATTN_HEAD_DIM
    first_half = lane < (ROT_DIM // 2)

    def rot(t):
        partner = jnp.where(first_half, pltpu.roll(t, aw - ROT_DIM // 2, 1), pltpu.roll(t, ROT_DIM // 2, 1))
        return t * ct + partner * st

    base = 4 * hw
    aq = rot(proj[:, base:base + aw])
    ak = rot(proj[:, base + aw:base + 2 * aw])
    av = proj[:, base + 2 * aw:base + 3 * aw]
    km_ref[...] = jnp.mean(ak, axis=0, keepdims=True)
    tm = x.shape[0]
    blk = pl.program_id(1)
    onehot = (lax.broadcasted_iota(jnp.int32, (tm, n_blocks), 1) == blk).astype(BF16)
    for hd in range(ATTN_HEADS):
        sl = slice(hd * ATTN_HEAD_DIM, (hd + 1) * ATTN_HEAD_DIM)
        aq_ref[hd] = aq[:, sl]
        ka_ref[hd] = jnp.concatenate([ak[:, sl].astype(BF16), onehot], axis=1)
        av_ref[hd] = av[:, sl].astype(BF16)


def _proj(x, scale1, shift1, norm_g, w_in_bf16, lb, ctab, stab):
    B, S, D = x.shape
    hw = lb.shape[-1]
    aw = ATTN_HEADS * ATTN_HEAD_DIM
    tm = MOBA_BLOCK
    nb = S // MOBA_BLOCK
    n_proj = w_in_bf16.shape[1]
    row = lambda b, i: (b, i, 0)
    vec = lambda b, i: (b, 0, 0)
    head = lambda b, i: (b, 0, i, 0)
    out_shapes = (
        jax.ShapeDtypeStruct((B, S, hw), F32),
        jax.ShapeDtypeStruct((B, S, hw), F32),
        jax.ShapeDtypeStruct((B, S, hw), F32),
        jax.ShapeDtypeStruct((B, S, hw), F32),
        jax.ShapeDtypeStruct((B, S, hw), F32),
        jax.ShapeDtypeStruct((B, ATTN_HEADS, S, ATTN_HEAD_DIM), F32),
        jax.ShapeDtypeStruct((B, ATTN_HEADS, S, ATTN_HEAD_DIM + nb), BF16),
        jax.ShapeDtypeStruct((B, ATTN_HEADS, S, ATTN_HEAD_DIM), BF16),
        jax.ShapeDtypeStruct((B, nb, 1, aw), F32),
    )
    hspec = pl.BlockSpec((None, tm, hw), row)
    return pl.pallas_call(
        functools.partial(_proj_kernel, hw=hw, aw=aw, n_blocks=nb),
        out_shape=out_shapes,
        grid=(B, S // tm),
        in_specs=[pl.BlockSpec((None, tm, D), row),
                  pl.BlockSpec((None, 1, D), vec),
                  pl.BlockSpec((None, 1, D), vec),
                  pl.BlockSpec((1, D), lambda b, i: (0, 0)),
                  pl.BlockSpec((D, n_proj), lambda b, i: (0, 0)),
                  pl.BlockSpec((1, hw), lambda b, i: (0, 0)),
                  pl.BlockSpec((None, tm, 128), row),
                  pl.BlockSpec((None, tm, 128), row)],
        out_specs=(hspec, hspec, hspec, hspec, hspec,
                   pl.BlockSpec((None, ATTN_HEADS, tm, ATTN_HEAD_DIM), head),
                   pl.BlockSpec((None, ATTN_HEADS, tm, ATTN_HEAD_DIM + nb), head),
                   pl.BlockSpec((None, ATTN_HEADS, tm, ATTN_HEAD_DIM), head),
                   pl.BlockSpec((None, None, 1, aw), lambda b, i: (b, i, 0, 0))),
        compiler_params=pltpu.CompilerParams(
            dimension_semantics=("arbitrary", "arbitrary"), vmem_limit_bytes=VMEM_LIMIT),
        name="proj",
    )(x, scale1, shift1, norm_g, w_in_bf16, lb, ctab, stab)


def _hgrn_kernel(q_ref, k_ref, lf_ref, v_ref, gt_ref, gn_ref, o_ref, st_ref, *, n_heads, n_chunks):
    @pl.when(pl.program_id(1) == 0)
    def _():
        st_ref[...] = jnp.zeros_like(st_ref)

    C = HGRN_CHUNK
    r = lax.broadcasted_iota(jnp.int32, (C, C), 0)
    c = lax.broadcasted_iota(jnp.int32, (C, C), 1)
    tril = c <= r
    ltri = tril.astype(F32)
    gn = gn_ref[...]

    def chunk(ci, carry):
        r0 = pl.multiple_of(ci * C, C)
        rows = pl.ds(r0, C)
        b_all = _dot(ltri, lf_ref[rows, :], precision=HIGHEST)
        for hd in range(n_heads):
            sl = slice(hd * HGRN_DK, (hd + 1) * HGRN_DK)
            b = b_all[:, sl]
            b_last = b[C - 1:C, :]
            q = q_ref[rows, sl]
            k = k_ref[rows, sl]
            v = v_ref[rows, sl]
            state_t = st_ref[hd]
            o_inter = _dot_nt((q * jnp.exp(b)).astype(BF16), state_t.astype(BF16))
            rho = 0.5 * b_last
            qa = q * jnp.exp(jnp.minimum(b - rho, EXP_CLAMP))
            kb = k * jnp.exp(jnp.minimum(rho - b, EXP_CLAMP))
            scores = jnp.where(tril, _dot_nt(qa.astype(BF16), kb.astype(BF16)), 0.0)
            v16 = v.astype(BF16)
            o = o_inter + _dot(scores.astype(BF16), v16)
            kd = k * jnp.exp(b_last - b)
            st_ref[hd] = state_t * jnp.exp(b_last) + _dot(v.T.astype(BF16), kd.astype(BF16))
            ms = jnp.mean(o * o, axis=-1, keepdims=True)
            o_ref[rows, sl] = o * lax.rsqrt(ms + RMS_EPS) * gn * gt_ref[rows, sl]
        return carry

    lax.fori_loop(0, n_chunks, chunk, 0)


def _hgrn(q, k, lf, v, gt, norm_g):
    B, S, hw = q.shape
    n_heads = hw // HGRN_DK
    tc = 512
    spec = pl.BlockSpec((None, tc, hw), lambda b, i: (b, i, 0))
    return pl.pallas_call(
        functools.partial(_hgrn_kernel, n_heads=n_heads, n_chunks=tc // HGRN_CHUNK),
        out_shape=jax.ShapeDtypeStruct((B, S, hw), F32),
        grid=(B, S // tc),
        in_specs=[spec, spec, spec, spec, spec, pl.BlockSpec((1, HGRN_DK), lambda b, i: (0, 0))],
        out_specs=spec,
        scratch_shapes=[pltpu.VMEM((n_heads, HGRN_DK, HGRN_DK), F32)],
        compiler_params=pltpu.CompilerParams(
            dimension_semantics=("arbitrary", "arbitrary"), vmem_limit_bytes=VMEM_LIMIT),
        name="hgrn",
    )(q, k, lf, v, gt, norm_g)


def _moba_kernel(q_ref, km_ref, ka_ref, v_ref, g_ref, o_ref, *, n_blocks):
    j = pl.program_id(2)
    T = MOBA_BLOCK
    q = q_ref[...]
    gate = _dot_nt(q, km_ref[...], precision=HIGHEST)
    lane = lax.broadcasted_iota(jnp.int32, gate.shape, 1)
    neg_inf = jnp.float32(-jnp.inf)
    gate = jnp.where(lane < j, gate, neg_inf)
    sel = lane == j
    for _ in range(MOBA_TOPK):
        m = jnp.max(gate, axis=1, keepdims=True)
        first = jnp.min(jnp.where(gate == m, lane, n_blocks), axis=1, keepdims=True)
        hit = lane == first
        sel = sel | (hit & (m > neg_inf))
        gate = jnp.where(hit, neg_inf, gate)
    bias = jnp.where(sel, 0.0, MASK_BIAS)
    q_aug = jnp.concatenate([(q * (ATTN_HEAD_DIM ** -0.5)).astype(BF16), bias.astype(BF16)], axis=1)

    own = pl.ds(pl.multiple_of(j * T, T), T)
    s = _dot_nt(q_aug, ka_ref[own, :])
    rr = lax.broadcasted_iota(jnp.int32, (T, T), 0)
    cc = lax.broadcasted_iota(jnp.int32, (T, T), 1)
    s = jnp.where(cc <= rr, s, MASK_BIAS)
    m0 = jnp.max(s, axis=1, keepdims=True)
    p = jnp.exp(s - m0)
    l0 = jnp.sum(p, axis=1, keepdims=True)
    acc0 = _dot(p.astype(BF16), v_ref[own, :])

    def past(kb, carry):
        m_i, l_i, acc = carry
        rows = pl.ds(pl.multiple_of(kb * T, T), T)
        s = _dot_nt(q_aug, ka_ref[rows, :])
        m_new = jnp.maximum(m_i, jnp.max(s, axis=1, keepdims=True))
        alpha = jnp.exp(m_i - m_new)
        p = jnp.exp(s - m_new)
        l_new = alpha * l_i + jnp.sum(p, axis=1, keepdims=True)
        acc_new = alpha * acc + _dot(p.astype(BF16), v_ref[rows, :])
        return m_new, l_new, acc_new

    _, l_f, acc = lax.fori_loop(0, j, past, (m0, l0, acc0))
    o = acc / l_f
    ms = jnp.mean(o * o, axis=-1, keepdims=True)
    o_ref[...] = o * lax.rsqrt(ms + RMS_EPS) * g_ref[...]


def _moba(aq, kmean, ka, av, norm_g):
    B, H, S, hd = aq.shape
    nb = S // MOBA_BLOCK
    T = MOBA_BLOCK
    return pl.pallas_call(
        functools.partial(_moba_kernel, n_blocks=nb),
        out_shape=jax.ShapeDtypeStruct((B, H, S, hd), F32),
        grid=(B, H, nb),
        in_specs=[pl.BlockSpec((None, None, T, hd), lambda b, h, j: (b, h, j, 0)),
                  pl.BlockSpec((None, None, nb, hd), lambda b, h, j: (b, h, 0, 0)),
                  pl.BlockSpec((None, None, S, hd + nb), lambda b, h, j: (b, h, 0, 0)),
                  pl.BlockSpec((None, None, S, hd), lambda b, h, j: (b, h, 0, 0)),
                  pl.BlockSpec((1, hd), lambda b, h, j: (0, 0))],
        out_specs=pl.BlockSpec((None, None, T, hd), lambda b, h, j: (b, h, j, 0)),
        compiler_params=pltpu.CompilerParams(
            dimension_semantics=("arbitrary", "arbitrary", "arbitrary"), vmem_limit_bytes=VMEM_LIMIT),
        name="moba",
    )(aq, kmean, ka, av, norm_g)


def _mix_kernel(oh_ref, oa_ref, x_ref, g1_ref, sc2_ref, sh2_ref, n2_ref, wo_ref, wr_ref, br_ref,
                x1_ref, h2_ref, idx_ref, gw_ref):
    cat = jnp.concatenate([oh_ref[...]] + [oa_ref[hd] for hd in range(ATTN_HEADS)], axis=1)
    mix = _dot(cat.astype(BF16), wo_ref[...])
    x1 = x_ref[...] + g1_ref[...] * mix
    x1_ref[...] = x1
    ms = jnp.mean(x1 * x1, axis=-1, keepdims=True)
    h2 = x1 * lax.rsqrt(ms + RMS_EPS) * n2_ref[...]
    h2 = h2 * (1.0 + sc2_ref[...]) + sh2_ref[...]
    h2_ref[...] = h2
    logits = _dot(h2, wr_ref[...], precision=HIGHEST) + br_ref[...]
    lane = lax.broadcasted_iota(jnp.int32, logits.shape, 1)
    neg_inf = jnp.float32(-jnp.inf)
    vals, idxs = [], []
    for _ in range(TOP_K):
        m = jnp.max(logits, axis=1, keepdims=True)
        first = jnp.min(jnp.where(logits == m, lane, N_EXPERTS), axis=1, keepdims=True)
        vals.append(m)
        idxs.append(first)
        logits = jnp.where(lane == first, neg_inf, logits)
    e = [jnp.exp(v - vals[0]) for v in vals]
    denom = e[0] + e[1] + e[2] + e[3]
    idx_ref[...] = jnp.concatenate(idxs, axis=1)
    gw_ref[...] = jnp.concatenate([ei / denom for ei in e], axis=1)


def _mix(oh, oa, x, gate1, scale2, shift2, norm2_g, w_out_bf16, w_router, b_router):
    B, S, D = x.shape
    hw = oh.shape[-1]
    tm = 256
    row = lambda b, i: (b, i, 0)
    vec = lambda b, i: (b, 0, 0)
    const = lambda b, i: (0, 0)
    return pl.pallas_call(
        _mix_kernel,
        out_shape=(jax.ShapeDtypeStruct((B, S, D), F32),
                   jax.ShapeDtypeStruct((B, S, D), F32),
                   jax.ShapeDtypeStruct((B, S, TOP_K), jnp.int32),
                   jax.ShapeDtypeStruct((B, S, TOP_K), F32)),
        grid=(B, S // tm),
        in_specs=[pl.BlockSpec((None, tm, hw), row),
                  pl.BlockSpec((None, ATTN_HEADS, tm, ATTN_HEAD_DIM), lambda b, i: (b, 0, i, 0)),
                  pl.BlockSpec((None, tm, D), row),
                  pl.BlockSpec((None, 1, D), vec),
                  pl.BlockSpec((None, 1, D), vec),
                  pl.BlockSpec((None, 1, D), vec),
                  pl.BlockSpec((1, D), const),
                  pl.BlockSpec((D, D), const),
                  pl.BlockSpec((D, N_EXPERTS), const),
                  pl.BlockSpec((1, N_EXPERTS), const)],
        out_specs=(pl.BlockSpec((None, tm, D), row),
                   pl.BlockSpec((None, tm, D), row),
                   pl.BlockSpec((None, tm, TOP_K), row),
                   pl.BlockSpec((None, tm, TOP_K), row)),
        compiler_params=pltpu.CompilerParams(
            dimension_semantics=("arbitrary", "arbitrary"), vmem_limit_bytes=VMEM_LIMIT),
        name="mix",
    )(oh, oa, x, gate1, scale2, shift2, norm2_g, w_out_bf16, w_router, b_router)


def _route(top_idx, gates, bm):
    T = top_idx.shape[0]
    n_assign = T * TOP_K
    flat_e = top_idx.reshape(-1)
    order = jnp.argsort(flat_e).astype(jnp.int32)
    sorted_e = flat_e[order]
    counts = jnp.zeros((N_EXPERTS,), jnp.int32).at[flat_e].add(1)
    group_start = jnp.cumsum(counts) - counts
    padded = (counts + bm - 1) // bm * bm
    pad_end = jnp.cumsum(padded)
    pad_start = pad_end - padded
    dest = pad_start[sorted_e] + jnp.arange(n_assign, dtype=jnp.int32) - group_start[sorted_e]
    n_blk = n_assign // bm + N_EXPERTS
    tok_buf = jnp.zeros((n_blk * bm,), jnp.int32).at[dest].set(order // TOP_K)
    gate_buf = jnp.zeros((n_blk * bm,), F32).at[dest].set(gates.reshape(-1)[order])
    blk_expert = jnp.minimum(
        jnp.searchsorted(pad_end, jnp.arange(n_blk, dtype=jnp.int32) * bm, side='right'),
        N_EXPERTS - 1).astype(jnp.int32)
    pos = jnp.zeros((n_assign,), jnp.int32).at[order].set(dest)
    n_used = (pad_end[-1] // bm).astype(jnp.int32).reshape(1)
    return tok_buf, gate_buf, blk_expert, pos, n_used, n_blk


def _gather_rows(ids_smem, slot, src_hbm, dst, sem, n_rows):
    def body(r, carry):
        t = ids_smem[slot, r]
        pltpu.make_async_copy(src_hbm.at[pl.ds(t, 1), :], dst.at[slot, pl.ds(r, 1), :], sem.at[slot]).start()
        return carry
    lax.fori_loop(0, n_rows, body, 0)


def _gather_step(ids_hbm, ids_smem, src_hbm, buf, ids_sem, row_sem, n_steps, n_rows):
    i = pl.program_id(0)
    slot = i % 2
    nxt = 1 - slot

    def ids_copy(step, s):
        return pltpu.make_async_copy(ids_hbm.at[step], ids_smem.at[s], ids_sem.at[s])

    @pl.when(i == 0)
    def _():
        first = ids_copy(0, 0)
        first.start()
        first.wait()
        _gather_rows(ids_smem, 0, src_hbm, buf, row_sem, n_rows)
        if n_steps > 1:
            ids_copy(1, 1).start()

    @pl.when(i + 1 < n_steps)
    def _():
        ids_copy(i + 1, nxt).wait()
        _gather_rows(ids_smem, nxt, src_hbm, buf, row_sem, n_rows)

    @pl.when(i + 2 < n_steps)
    def _():
        ids_copy(i + 2, slot).start()

    pltpu.make_async_copy(src_hbm.at[pl.ds(0, n_rows), :], buf.at[slot], row_sem.at[slot]).wait()
    return slot


def _moe_kernel(be_ref, nu_ref, ids_hbm, h2_hbm, gate_ref, wgu_ref, bgu_ref, wd_ref, bd_ref,
                y_ref, ids_smem, xbuf, ids_sem, row_sem, *, n_blk, bm, d_ff):
    slot = _gather_step(ids_hbm, ids_smem, h2_hbm, xbuf, ids_sem, row_sem, n_blk, bm)
    i = pl.program_id(0)

    @pl.when(i < nu_ref[0])
    def _():
        xb = xbuf[slot].astype(BF16)
        gu = _dot(xb, wgu_ref[...]) + bgu_ref[...]
        gate = jnp.minimum(gu[:, :d_ff], SWIGLU_LIMIT)
        up = jnp.clip(gu[:, d_ff:], -SWIGLU_LIMIT, SWIGLU_LIMIT)
        act = (up + 1.0) * gate * _sigmoid(SWIGLU_ALPHA * gate)
        y_ref[...] = (_dot(act.astype(BF16), wd_ref[...]) + bd_ref[...]) * gate_ref[...]

    @pl.when(i >= nu_ref[0])
    def _():
        y_ref[...] = jnp.zeros_like(y_ref)


def _moe(h2, tok_buf, gate_buf, blk_expert, n_used, n_blk, wgu, bgu, wd, bd):
    T, D = h2.shape
    bm = MOE_ROWS
    d_ff = wd.shape[1]
    grid_spec = pltpu.PrefetchScalarGridSpec(
        num_scalar_prefetch=2,
        grid=(n_blk,),
        in_specs=[pl.BlockSpec(memory_space=pl.ANY),
                  pl.BlockSpec(memory_space=pl.ANY),
                  pl.BlockSpec((bm, 1), lambda i, be, nu: (i, 0)),
                  pl.BlockSpec((None, D, 2 * d_ff), lambda i, be, nu: (be[i], 0, 0)),
                  pl.BlockSpec((None, 1, 2 * d_ff), lambda i, be, nu: (be[i], 0, 0)),
                  pl.BlockSpec((None, d_ff, D), lambda i, be, nu: (be[i], 0, 0)),
                  pl.BlockSpec((None, 1, D), lambda i, be, nu: (be[i], 0, 0))],
        out_specs=pl.BlockSpec((bm, D), lambda i, be, nu: (i, 0)),
        scratch_shapes=[pltpu.SMEM((2, bm), jnp.int32),
                        pltpu.VMEM((2, bm, D), F32),
                        pltpu.SemaphoreType.DMA((2,)),
                        pltpu.SemaphoreType.DMA((2,))],
    )
    return pl.pallas_call(
        functools.partial(_moe_kernel, n_blk=n_blk, bm=bm, d_ff=d_ff),
        out_shape=jax.ShapeDtypeStruct((n_blk * bm, D), F32),
        grid_spec=grid_spec,
        compiler_params=pltpu.CompilerParams(
            dimension_semantics=("arbitrary",), vmem_limit_bytes=VMEM_LIMIT),
        name="moe",
    )(blk_expert, n_used, tok_buf.reshape(n_blk, bm), h2, gate_buf.reshape(n_blk * bm, 1),
      wgu, bgu.reshape(N_EXPERTS, 1, 2 * d_ff), wd, bd.reshape(N_EXPERTS, 1, D))


def _combine_kernel(ids_hbm, y_hbm, x1_ref, g2_ref, fg_ref, o_ref, ids_smem, ybuf, ids_sem, row_sem,
                    *, n_steps, tm):
    slot = _gather_step(ids_hbm, ids_smem, y_hbm, ybuf, ids_sem, row_sem, n_steps, TOP_K * tm)
    y = ybuf[slot, 0:tm, :]
    for kk in range(1, TOP_K):
        y = y + ybuf[slot, kk * tm:(kk + 1) * tm, :]
    x2 = x1_ref[...] + g2_ref[...] * y
    ms = jnp.mean(x2 * x2, axis=-1, keepdims=True)
    o_ref[...] = x2 * lax.rsqrt(ms + RMS_EPS) * fg_ref[...]


def _combine(y_sorted, pos, x1, gate2, final_g):
    B, S, D = x1.shape
    T = B * S
    tm = COMBINE_TOKENS
    n_steps = T // tm
    steps_per_batch = S // tm
    ids = pos.reshape(n_steps, tm, TOP_K).transpose(0, 2, 1).reshape(n_steps, TOP_K * tm)
    return pl.pallas_call(
        functools.partial(_combine_kernel, n_steps=n_steps, tm=tm),
        out_shape=jax.ShapeDtypeStruct((T, D), F32),
        grid=(n_steps,),
        in_specs=[pl.BlockSpec(memory_space=pl.ANY),
                  pl.BlockSpec(memory_space=pl.ANY),
                  pl.BlockSpec((tm, D), lambda i: (i, 0)),
                  pl.BlockSpec((None, 1, D), lambda i: (i // steps_per_batch, 0, 0)),
                  pl.BlockSpec((1, D), lambda i: (0, 0))],
        out_specs=pl.BlockSpec((tm, D), lambda i: (i, 0)),
        scratch_shapes=[pltpu.SMEM((2, TOP_K * tm), jnp.int32),
                        pltpu.VMEM((2, TOP_K * tm, D), F32),
                        pltpu.SemaphoreType.DMA((2,)),
                        pltpu.SemaphoreType.DMA((2,))],
        compiler_params=pltpu.CompilerParams(
            dimension_semantics=("arbitrary",), vmem_limit_bytes=VMEM_LIMIT),
        name="combine",
    )(ids, y_sorted, x1.reshape(T, D), gate2, final_g).reshape(B, S, D)


def _rotary_tables(positions):
    half = ROT_DIM // 2
    inv_freq = jnp.exp(-math.log(ROPE_THETA) * jnp.arange(0, ROT_DIM, 2, dtype=F32) / ROT_DIM)
    ang = positions.astype(F32)[:, :, None] * inv_freq
    cos, sin = jnp.cos(ang), jnp.sin(ang)
    B, S = positions.shape
    pad = ATTN_HEAD_DIM - ROT_DIM
    ct = jnp.concatenate([cos, cos, jnp.ones((B, S, pad), F32)], axis=-1)
    st = jnp.concatenate([-sin, sin, jnp.zeros((B, S, pad), F32)], axis=-1)
    return jnp.concatenate([ct, ct], axis=-1), jnp.concatenate([st, st], axis=-1)


def kernel(x, c, positions, w_ada, b_ada, norm1_g, w_in, hgrn_lb_logits, hgrn_norm_g, attn_norm_g,
           w_out, norm2_g, w_router, b_router, w_gate_up, b_gate_up, w_down, b_down, final_norm_g):
    B, S, D = x.shape
    assert w_in.shape[0] == 1, "single-layer block: the final norm is fused into the combine step"
    l = 0
    ctab, stab = _rotary_tables(positions)
    lower_bounds = jnp.cumsum(jax.nn.softmax(hgrn_lb_logits.astype(F32), axis=0), axis=0)
    mod = _ada(c, w_ada[l], b_ada[l])
    shift1, scale1, gate1, shift2, scale2, gate2 = jnp.split(mod[:, None, :], N_MOD, axis=-1)
    q, k, lf, v, gt, aq, ka, av, km = _proj(
        x, scale1, shift1, norm1_g[l][None], w_in[l].astype(BF16), lower_bounds[l][None], ctab, stab)
    o_h = _hgrn(q, k, lf, v, gt, hgrn_norm_g[l][None])
    nb = S // MOBA_BLOCK
    kmean = km.reshape(B, nb, ATTN_HEADS, ATTN_HEAD_DIM).transpose(0, 2, 1, 3)
    o_a = _moba(aq, kmean, ka, av, attn_norm_g[l][None])
    x1, h2, top_idx, gates = _mix(o_h, o_a, x, gate1, scale2, shift2, norm2_g[l][None],
                                  w_out[l].astype(BF16), w_router[l], b_router[l][None])
    tok_buf, gate_buf, blk_expert, pos, n_used, n_blk = _route(
        top_idx.reshape(B * S, TOP_K), gates.reshape(B * S, TOP_K), MOE_ROWS)
    y_sorted = _moe(h2.reshape(B * S, D), tok_buf, gate_buf, blk_expert, n_used, n_blk,
                    w_gate_up[l].astype(BF16), b_gate_up[l], w_down[l].astype(BF16), b_down[l])
    return _combine(y_sorted, pos, x1, gate2, final_norm_g[None])
```

```python
import functools
import math

import jax
import jax.numpy as jnp
from jax import lax
from jax.experimental import pallas as pl
from jax.experimental.pallas import tpu as pltpu
from jax.experimental.pallas import tpu_sc as plsc

F32 = jnp.float32
BF16 = jnp.bfloat16
HIGHEST = lax.Precision.HIGHEST

HGRN_DK = 128
HGRN_CHUNK = 64
ATTN_HEADS = 4
ATTN_HEAD_DIM = 64
ROT_DIM = ATTN_HEAD_DIM // 4
ROPE_THETA = 500000.0
MOBA_BLOCK = 256
MOBA_TOPK = 3
N_EXPERTS = 32
TOP_K = 4
SWIGLU_ALPHA = 1.702
SWIGLU_LIMIT = 7.0
N_MOD = 6
RMS_EPS = 1e-6

MASK_BIAS = -1e30
EXP_CLAMP = 80.0
MOE_ROWS = 256
COMBINE_TOKENS = 128
VMEM_LIMIT = 56 * 1024 * 1024


def _sigmoid(x):
    return 1.0 / (1.0 + jnp.exp(-x))


def _dot(a, b, **kw):
    return jnp.dot(a, b, preferred_element_type=F32, **kw)


def _dot_nt(a, b, **kw):
    return lax.dot_general(a, b, (((1,), (1,)), ((), ())), preferred_element_type=F32, **kw)


def _ada_kernel(c_ref, w_ref, b_ref, o_ref):
    c = c_ref[...]
    o_ref[...] = _dot(c * _sigmoid(c), w_ref[...], precision=HIGHEST) + b_ref[...]


def _ada(c, w_ada, b_ada):
    B, D = c.shape
    N = w_ada.shape[1]
    tn = N // 4
    c8 = jnp.zeros((8, D), F32).at[:B].set(c)
    out = pl.pallas_call(
        _ada_kernel,
        out_shape=jax.ShapeDtypeStruct((8, N), F32),
        grid=(N // tn,),
        in_specs=[pl.BlockSpec((8, D), lambda j: (0, 0)),
                  pl.BlockSpec((D, tn), lambda j: (0, j)),
                  pl.BlockSpec((1, tn), lambda j: (0, j))],
        out_specs=pl.BlockSpec((8, tn), lambda j: (0, j)),
        compiler_params=pltpu.CompilerParams(vmem_limit_bytes=VMEM_LIMIT),
        name="ada",
    )(c8, w_ada, b_ada.reshape(1, N))
    return out[:B]


def _proj_kernel(x_ref, sc_ref, sh_ref, g_ref, w_ref, lb_ref, ct_ref, st_ref,
                 q_ref, k_ref, lf_ref, v_ref, gt_ref, aq_ref, ka_ref, av_ref, km_ref,
                 *, hw, aw, n_blocks):
    x = x_ref[...]
    ms = jnp.mean(x * x, axis=-1, keepdims=True)
    h = x * lax.rsqrt(ms + RMS_EPS) * g_ref[...]
    h = h * (1.0 + sc_ref[...]) + sh_ref[...]
    proj = _dot(h.astype(BF16), w_ref[...])

    hq = proj[:, 0:hw]
    hf = proj[:, hw:2 * hw]
    hg = proj[:, 3 * hw:4 * hw]
    q_ref[...] = hq * _sigmoid(hq) * (HGRN_DK ** -0.5)
    lb = lb_ref[...]
    f = lb + (1.0 - lb) * _sigmoid(hf)
    k_ref[...] = 1.0 - f
    lf_ref[...] = jnp.log(f)
    v_ref[...] = proj[:, 2 * hw:3 * hw]
    gt_ref[...] = hg * _sigmoid(hg)

    ct = jnp.concatenate([ct_ref[...]] * (aw // 128), axis=1)
    st = jnp.concatenate([st_ref[...]] * (aw // 128), axis=1)
    lane = lax.broadcasted_iota(jnp.int32, ct.shape, 1) % ATTN_HEAD_DIM
    first_half = lane < (ROT_DIM // 2)

    def rot(t):
        partner = jnp.where(first_half, pltpu.roll(t, aw - ROT_DIM // 2, 1), pltpu.roll(t, ROT_DIM // 2, 1))
        return t * ct + partner * st

    base = 4 * hw
    aq = rot(proj[:, base:base + aw])
    ak = rot(proj[:, base + aw:base + 2 * aw])
    av = proj[:, base + 2 * aw:base + 3 * aw]
    km_ref[...] = jnp.mean(ak, axis=0, keepdims=True)
    tm = x.shape[0]
    blk = pl.program_id(1)
    onehot = (lax.broadcasted_iota(jnp.int32, (tm, n_blocks), 1) == blk).astype(BF16)
    for hd in range(ATTN_HEADS):
        sl = slice(hd * ATTN_HEAD_DIM, (hd + 1) * ATTN_HEAD_DIM)
        aq_ref[hd] = aq[:, sl]
        ka_ref[hd] = jnp.concatenate([ak[:, sl].astype(BF16), onehot], axis=1)
        av_ref[hd] = av[:, sl].astype(BF16)


def _proj(x, scale1, shift1, norm_g, w_in_bf16, lb, ctab, stab):
    B, S, D = x.shape
    hw = lb.shape[-1]
    aw = ATTN_HEADS * ATTN_HEAD_DIM
    tm = MOBA_BLOCK
    nb = S // MOBA_BLOCK
    n_proj = w_in_bf16.shape[1]
    row = lambda b, i: (b, i, 0)
    vec = lambda b, i: (b, 0, 0)
    head = lambda b, i: (b, 0, i, 0)
    out_shapes = (
        jax.ShapeDtypeStruct((B, S, hw), F32),
        jax.ShapeDtypeStruct((B, S, hw), F32),
        jax.ShapeDtypeStruct((B, S, hw), F32),
        jax.ShapeDtypeStruct((B, S, hw), F32),
        jax.ShapeDtypeStruct((B, S, hw), F32),
        jax.ShapeDtypeStruct((B, ATTN_HEADS, S, ATTN_HEAD_DIM), F32),
        jax.ShapeDtypeStruct((B, ATTN_HEADS, S, ATTN_HEAD_DIM + nb), BF16),
        jax.ShapeDtypeStruct((B, ATTN_HEADS, S, ATTN_HEAD_DIM), BF16),
        jax.ShapeDtypeStruct((B, nb, 1, aw), F32),
    )
    hspec = pl.BlockSpec((None, tm, hw), row)
    return pl.pallas_call(
        functools.partial(_proj_kernel, hw=hw, aw=aw, n_blocks=nb),
        out_shape=out_shapes,
        grid=(B, S // tm),
        in_specs=[pl.BlockSpec((None, tm, D), row),
                  pl.BlockSpec((None, 1, D), vec),
                  pl.BlockSpec((None, 1, D), vec),
                  pl.BlockSpec((1, D), lambda b, i: (0, 0)),
                  pl.BlockSpec((D, n_proj), lambda b, i: (0, 0)),
                  pl.BlockSpec((1, hw), lambda b, i: (0, 0)),
                  pl.BlockSpec((None, tm, 128), row),
                  pl.BlockSpec((None, tm, 128), row)],
        out_specs=(hspec, hspec, hspec, hspec, hspec,
                   pl.BlockSpec((None, ATTN_HEADS, tm, ATTN_HEAD_DIM), head),
                   pl.BlockSpec((None, ATTN_HEADS, tm, ATTN_HEAD_DIM + nb), head),
                   pl.BlockSpec((None, ATTN_HEADS, tm, ATTN_HEAD_DIM), head),
                   pl.BlockSpec((None, None, 1, aw), lambda b, i: (b, i, 0, 0))),
        compiler_params=pltpu.CompilerParams(
            dimension_semantics=("arbitrary", "arbitrary"), vmem_limit_bytes=VMEM_LIMIT),
        name="proj",
    )(x, scale1, shift1, norm_g, w_in_bf16, lb, ctab, stab)


def _hgrn_kernel(q_ref, k_ref, lf_ref, v_ref, gt_ref, gn_ref, o_ref, st_ref, *, n_heads, n_chunks):
    @pl.when(pl.program_id(1) == 0)
    def _():
        st_ref[...] = jnp.zeros_like(st_ref)

    C = HGRN_CHUNK
    r = lax.broadcasted_iota(jnp.int32, (C, C), 0)
    c = lax.broadcasted_iota(jnp.int32, (C, C), 1)
    tril = c <= r
    ltri = tril.astype(F32)
    gn = gn_ref[...]

    def chunk(ci, carry):
        r0 = pl.multiple_of(ci * C, C)
        rows = pl.ds(r0, C)
        b_all = _dot(ltri, lf_ref[rows, :], precision=HIGHEST)
        for hd in range(n_heads):
            sl = slice(hd * HGRN_DK, (hd + 1) * HGRN_DK)
            b = b_all[:, sl]
            b_last = b[C - 1:C, :]
            q = q_ref[rows, sl]
            k = k_ref[rows, sl]
            v = v_ref[rows, sl]
            state_t = st_ref[hd]
            o_inter = _dot_nt((q * jnp.exp(b)).astype(BF16), state_t.astype(BF16))
            rho = 0.5 * b_last
            qa = q * jnp.exp(jnp.minimum(b - rho, EXP_CLAMP))
            kb = k * jnp.exp(jnp.minimum(rho - b, EXP_CLAMP))
            scores = jnp.where(tril, _dot_nt(qa.astype(BF16), kb.astype(BF16)), 0.0)
            v16 = v.astype(BF16)
            o = o_inter + _dot(scores.astype(BF16), v16)
            kd = k * jnp.exp(b_last - b)
            st_ref[hd] = state_t * jnp.exp(b_last) + _dot(v.T.astype(BF16), kd.astype(BF16))
            ms = jnp.mean(o * o, axis=-1, keepdims=True)
            o_ref[rows, sl] = o * lax.rsqrt(ms + RMS_EPS) * gn * gt_ref[rows, sl]
        return carry

    lax.fori_loop(0, n_chunks, chunk, 0)


def _hgrn(q, k, lf, v, gt, norm_g):
    B, S, hw = q.shape
    n_heads = hw // HGRN_DK
    tc = 512
    spec = pl.BlockSpec((None, tc, hw), lambda b, i: (b, i, 0))
    return pl.pallas_call(
        functools.partial(_hgrn_kernel, n_heads=n_heads, n_chunks=tc // HGRN_CHUNK),
        out_shape=jax.ShapeDtypeStruct((B, S, hw), F32),
        grid=(B, S // tc),
        in_specs=[spec, spec, spec, spec, spec, pl.BlockSpec((1, HGRN_DK), lambda b, i: (0, 0))],
        out_specs=spec,
        scratch_shapes=[pltpu.VMEM((n_heads, HGRN_DK, HGRN_DK), F32)],
        compiler_params=pltpu.CompilerParams(
            dimension_semantics=("arbitrary", "arbitrary"), vmem_limit_bytes=VMEM_LIMIT),
        name="hgrn",
    )(q, k, lf, v, gt, norm_g)


def _moba_kernel(q_ref, km_ref, ka_ref, v_ref, g_ref, o_ref, *, n_blocks):
    j = pl.program_id(2)
    T = MOBA_BLOCK
    q = q_ref[...]
    gate = _dot_nt(q, km_ref[...], precision=HIGHEST)
    lane = lax.broadcasted_iota(jnp.int32, gate.shape, 1)
    neg_inf = jnp.float32(-jnp.inf)
    gate = jnp.where(lane < j, gate, neg_inf)
    sel = lane == j
    for _ in range(MOBA_TOPK):
        m = jnp.max(gate, axis=1, keepdims=True)
        first = jnp.min(jnp.where(gate == m, lane, n_blocks), axis=1, keepdims=True)
        hit = lane == first
        sel = sel | (hit & (m > neg_inf))
        gate = jnp.where(hit, neg_inf, gate)
    bias = jnp.where(sel, 0.0, MASK_BIAS)
    q_aug = jnp.concatenate([(q * (ATTN_HEAD_DIM ** -0.5)).astype(BF16), bias.astype(BF16)], axis=1)

    own = pl.ds(pl.multiple_of(j * T, T), T)
    s = _dot_nt(q_aug, ka_ref[own, :])
    rr = lax.broadcasted_iota(jnp.int32, (T, T), 0)
    cc = lax.broadcasted_iota(jnp.int32, (T, T), 1)
    s = jnp.where(cc <= rr, s, MASK_BIAS)
    m0 = jnp.max(s, axis=1, keepdims=True)
    p = jnp.exp(s - m0)
    l0 = jnp.sum(p, axis=1, keepdims=True)
    acc0 = _dot(p.astype(BF16), v_ref[own, :])

    def past(kb, carry):
        m_i, l_i, acc = carry
        rows = pl.ds(pl.multiple_of(kb * T, T), T)
        s = _dot_nt(q_aug, ka_ref[rows, :])
        m_new = jnp.maximum(m_i, jnp.max(s, axis=1, keepdims=True))
        alpha = jnp.exp(m_i - m_new)
        p = jnp.exp(s - m_new)
        l_new = alpha * l_i + jnp.sum(p, axis=1, keepdims=True)
        acc_new = alpha * acc + _dot(p.astype(BF16), v_ref[rows, :])
        return m_new, l_new, acc_new

    _, l_f, acc = lax.fori_loop(0, j, past, (m0, l0, acc0))
    o = acc / l_f
    ms = jnp.mean(o * o, axis=-1, keepdims=True)
    o_ref[...] = o * lax.rsqrt(ms + RMS_EPS) * g_ref[...]


def _moba(aq, kmean, ka, av, norm_g):
    B, H, S, hd = aq.shape
    nb = S // MOBA_BLOCK
    T = MOBA_BLOCK
    return pl.pallas_call(
        functools.partial(_moba_kernel, n_blocks=nb),
        out_shape=jax.ShapeDtypeStruct((B, H, S, hd), F32),
        grid=(B, H, nb),
        in_specs=[pl.BlockSpec((None, None, T, hd), lambda b, h, j: (b, h, j, 0)),
                  pl.BlockSpec((None, None, nb, hd), lambda b, h, j: (b, h, 0, 0)),
                  pl.BlockSpec((None, None, S, hd + nb), lambda b, h, j: (b, h, 0, 0)),
                  pl.BlockSpec((None, None, S, hd), lambda b, h, j: (b, h, 0, 0)),
                  pl.BlockSpec((1, hd), lambda b, h, j: (0, 0))],
        out_specs=pl.BlockSpec((None, None, T, hd), lambda b, h, j: (b, h, j, 0)),
        compiler_params=pltpu.CompilerParams(
            dimension_semantics=("arbitrary", "arbitrary", "arbitrary"), vmem_limit_bytes=VMEM_LIMIT),
        name="moba",
    )(aq, kmean, ka, av, norm_g)


def _mix_kernel(oh_ref, oa_ref, x_ref, g1_ref, sc2_ref, sh2_ref, n2_ref, wo_ref, wr_ref, br_ref,
                x1_ref, h2_ref, idx_ref, gw_ref):
    cat = jnp.concatenate([oh_ref[...]] + [oa_ref[hd] for hd in range(ATTN_HEADS)], axis=1)
    mix = _dot(cat.astype(BF16), wo_ref[...])
    x1 = x_ref[...] + g1_ref[...] * mix
    x1_ref[...] = x1
    ms = jnp.mean(x1 * x1, axis=-1, keepdims=True)
    h2 = x1 * lax.rsqrt(ms + RMS_EPS) * n2_ref[...]
    h2 = h2 * (1.0 + sc2_ref[...]) + sh2_ref[...]
    h2_ref[...] = h2
    logits = _dot(h2, wr_ref[...], precision=HIGHEST) + br_ref[...]
    lane = lax.broadcasted_iota(jnp.int32, logits.shape, 1)
    neg_inf = jnp.float32(-jnp.inf)
    vals, idxs = [], []
    for _ in range(TOP_K):
        m = jnp.max(logits, axis=1, keepdims=True)
        first = jnp.min(jnp.where(logits == m, lane, N_EXPERTS), axis=1, keepdims=True)
        vals.append(m)
        idxs.append(first)
        logits = jnp.where(lane == first, neg_inf, logits)
    e = [jnp.exp(v - vals[0]) for v in vals]
    denom = e[0] + e[1] + e[2] + e[3]
    idx_ref[...] = jnp.concatenate(idxs, axis=1)
    gw_ref[...] = jnp.concatenate([ei / denom for ei in e], axis=1)


def _mix(oh, oa, x, gate1, scale2, shift2, norm2_g, w_out_bf16, w_router, b_router):
    B, S, D = x.shape
    hw = oh.shape[-1]
    tm = 256
    row = lambda b, i: (b, i, 0)
    vec = lambda b, i: (b, 0, 0)
    const = lambda b, i: (0, 0)
    return pl.pallas_call(
        _mix_kernel,
        out_shape=(jax.ShapeDtypeStruct((B, S, D), F32),
                   jax.ShapeDtypeStruct((B, S, D), F32),
                   jax.ShapeDtypeStruct((B, S, TOP_K), jnp.int32),
                   jax.ShapeDtypeStruct((B, S, TOP_K), F32)),
        grid=(B, S // tm),
        in_specs=[pl.BlockSpec((None, tm, hw), row),
                  pl.BlockSpec((None, ATTN_HEADS, tm, ATTN_HEAD_DIM), lambda b, i: (b, 0, i, 0)),
                  pl.BlockSpec((None, tm, D), row),
                  pl.BlockSpec((None, 1, D), vec),
                  pl.BlockSpec((None, 1, D), vec),
                  pl.BlockSpec((None, 1, D), vec),
                  pl.BlockSpec((1, D), const),
                  pl.BlockSpec((D, D), const),
                  pl.BlockSpec((D, N_EXPERTS), const),
                  pl.BlockSpec((1, N_EXPERTS), const)],
        out_specs=(pl.BlockSpec((None, tm, D), row),
                   pl.BlockSpec((None, tm, D), row),
                   pl.BlockSpec((None, tm, TOP_K), row),
                   pl.BlockSpec((None, tm, TOP_K), row)),
        compiler_params=pltpu.CompilerParams(
            dimension_semantics=("arbitrary", "arbitrary"), vmem_limit_bytes=VMEM_LIMIT),
        name="mix",
    )(oh, oa, x, gate1, scale2, shift2, norm2_g, w_out_bf16, w_router, b_router)


def _route(top_idx, bm):
    T = top_idx.shape[0]
    n_assign = T * TOP_K
    flat_e = top_idx.reshape(-1)
    a_ids = jnp.arange(n_assign, dtype=jnp.int32)
    experts = jnp.arange(N_EXPERTS, dtype=jnp.int32)
    sorted_e, order = lax.sort((flat_e, a_ids), num_keys=1, is_stable=True)
    counts = jnp.sum((flat_e[:, None] == experts[None, :]).astype(jnp.int32), axis=0)
    group_end = jnp.cumsum(counts)
    group_start = group_end - counts
    padded = (counts + bm - 1) // bm * bm
    pad_end = jnp.cumsum(padded)
    pad_start = pad_end - padded
    n_blk = n_assign // bm + N_EXPERTS
    blk_start = jnp.arange(n_blk, dtype=jnp.int32) * bm
    blk_expert = jnp.minimum(
        jnp.sum((pad_end[None, :] <= blk_start[:, None]).astype(jnp.int32), axis=1), N_EXPERTS - 1)
    shift = pad_start - group_start
    src = (blk_start - shift[blk_expert])[:, None] + jnp.arange(bm, dtype=jnp.int32)[None, :]
    valid = src < group_end[blk_expert][:, None]
    tok_buf = jnp.where(valid, order[jnp.clip(src, 0, n_assign - 1)] // TOP_K, 0)
    dest_sorted = a_ids + jnp.sum(jnp.where(sorted_e[:, None] == experts[None, :], shift[None, :], 0), axis=1)
    _, pos = lax.sort((order, dest_sorted), num_keys=1)
    return tok_buf, blk_expert.astype(jnp.int32), pos, n_blk


SC_CORES = 2
SC_SUBCORES = 16


def _sc_gather(table, idx, chunk):
    M = idx.shape[0]
    D = table.shape[1]
    n_workers = SC_CORES * SC_SUBCORES
    per_worker = M // n_workers
    assert per_worker * n_workers == M and per_worker % chunk == 0 and chunk % 8 == 0
    mesh = plsc.VectorSubcoreMesh(core_axis_name="c", subcore_axis_name="s")

    @functools.partial(
        pl.kernel, mesh=mesh,
        out_type=jax.ShapeDtypeStruct((M, D), table.dtype),
        scratch_types=[pltpu.VMEM((chunk,), jnp.int32),
                       pltpu.VMEM((chunk, D), table.dtype),
                       pltpu.SemaphoreType.DMA],
    )
    def gather_kernel(table_hbm, idx_hbm, out_hbm, idx_v, rows_v, sem):
        wid = lax.axis_index("s") * SC_CORES + lax.axis_index("c")
        base = wid * per_worker

        @pl.loop(0, per_worker // chunk)
        def _(j):
            off = pl.multiple_of(base + j * chunk, 8)
            pltpu.sync_copy(idx_hbm.at[pl.ds(off, chunk)], idx_v)
            pltpu.async_copy(table_hbm.at[idx_v], rows_v, sem).wait()
            pltpu.sync_copy(rows_v, out_hbm.at[pl.ds(off, chunk)])

    return gather_kernel(table, idx)


class _RowGather:
    def __init__(self, ids_hbm, ids_smem, src_hbm, buf, ids_sem, row_sem, n_steps, n_rows):
        self.ids_hbm, self.ids_smem, self.src, self.buf = ids_hbm, ids_smem, src_hbm, buf
        self.ids_sem, self.row_sem, self.n_steps, self.n_rows = ids_sem, row_sem, n_steps, n_rows
        i = pl.program_id(0)
        self.i = i
        self.slot = i % 2
        self.nxt = 1 - self.slot
        self.step1 = jnp.where(i + 1 < n_steps, i + 1, i + 1 - n_steps)
        self.step2 = jnp.where(i + 2 < n_steps, i + 2, i + 2 - n_steps)

    def _ids_copy(self, step, s):
        return pltpu.make_async_copy(self.ids_hbm.at[step], self.ids_smem.at[s], self.ids_sem.at[s])

    def _row_copy(self, s, r, t):
        return pltpu.make_async_copy(self.src.at[pl.ds(t, 1), :], self.buf.at[s, pl.ds(r, 1), :], self.row_sem.at[s])

    def _wait_rows(self, s):
        pltpu.make_async_copy(self.src.at[pl.ds(0, self.n_rows), :], self.buf.at[s], self.row_sem.at[s]).wait()

    def begin(self):
        @pl.when(self.i == 0)
        def _():
            first = self._ids_copy(0, 0)
            first.start()
            first.wait()

            def body(r, carry):
                self._row_copy(0, r, self.ids_smem[0, r]).start()
                return carry
            lax.fori_loop(0, self.n_rows, body, 0)
            self._ids_copy(1 % self.n_steps, 1).start()

        self._wait_rows(self.slot)
        self._ids_copy(self.step1, self.nxt).wait()
        return self.slot

    def issue_next(self):
        for r in range(self.n_rows):
            self._row_copy(self.nxt, r, self.ids_smem[self.nxt, r]).start()
        self._ids_copy(self.step2, self.slot).start()

    def finish(self):
        @pl.when(self.i == self.n_steps - 1)
        def _():
            self._wait_rows(self.nxt)
            self._ids_copy(self.step2, self.slot).wait()


def _moe_kernel(be_ref, ids_hbm, h2_hbm, wgu_ref, bgu_ref, wd_ref, bd_ref,
                y_ref, ids_smem, xbuf, ids_sem, row_sem, *, n_blk, bm, d_ff):
    g = _RowGather(ids_hbm, ids_smem, h2_hbm, xbuf, ids_sem, row_sem, n_blk, bm)
    slot = g.begin()
    xb = xbuf[slot].astype(BF16)
    g.issue_next()
    gu = _dot(xb, wgu_ref[...]) + bgu_ref[...]
    gate = jnp.minimum(gu[:, :d_ff], SWIGLU_LIMIT)
    up = jnp.clip(gu[:, d_ff:], -SWIGLU_LIMIT, SWIGLU_LIMIT)
    act = (up + 1.0) * gate * _sigmoid(SWIGLU_ALPHA * gate)
    y_ref[...] = _dot(act.astype(BF16), wd_ref[...]) + bd_ref[...]
    g.finish()


def _moe(h2, tok_buf, blk_expert, n_blk, wgu, bgu, wd, bd):
    T, D = h2.shape
    bm = MOE_ROWS
    d_ff = wd.shape[1]
    grid_spec = pltpu.PrefetchScalarGridSpec(
        num_scalar_prefetch=1,
        grid=(n_blk,),
        in_specs=[pl.BlockSpec(memory_space=pl.ANY),
                  pl.BlockSpec(memory_space=pl.ANY),
                  pl.BlockSpec((None, D, 2 * d_ff), lambda i, be: (be[i], 0, 0)),
                  pl.BlockSpec((None, 1, 2 * d_ff), lambda i, be: (be[i], 0, 0)),
                  pl.BlockSpec((None, d_ff, D), lambda i, be: (be[i], 0, 0)),
                  pl.BlockSpec((None, 1, D), lambda i, be: (be[i], 0, 0))],
        out_specs=pl.BlockSpec((bm, D), lambda i, be: (i, 0)),
        scratch_shapes=[pltpu.SMEM((2, bm), jnp.int32),
                        pltpu.VMEM((2, bm, D), F32),
                        pltpu.SemaphoreType.DMA((2,)),
                        pltpu.SemaphoreType.DMA((2,))],
    )
    return pl.pallas_call(
        functools.partial(_moe_kernel, n_blk=n_blk, bm=bm, d_ff=d_ff),
        out_shape=jax.ShapeDtypeStruct((n_blk * bm, D), F32),
        grid_spec=grid_spec,
        compiler_params=pltpu.CompilerParams(
            dimension_semantics=("arbitrary",), vmem_limit_bytes=VMEM_LIMIT),
        name="moe",
    )(blk_expert, tok_buf, h2, wgu, bgu.reshape(N_EXPERTS, 1, 2 * d_ff), wd, bd.reshape(N_EXPERTS, 1, D))


def _moe_rows_kernel(be_ref, x_ref, wgu_ref, bgu_ref, wd_ref, bd_ref, y_ref, *, d_ff):
    gu = _dot(x_ref[...].astype(BF16), wgu_ref[...]) + bgu_ref[...]
    gate = jnp.minimum(gu[:, :d_ff], SWIGLU_LIMIT)
    up = jnp.clip(gu[:, d_ff:], -SWIGLU_LIMIT, SWIGLU_LIMIT)
    act = (up + 1.0) * gate * _sigmoid(SWIGLU_ALPHA * gate)
    y_ref[...] = _dot(act.astype(BF16), wd_ref[...]) + bd_ref[...]


def _moe_rows(xs, blk_expert, n_blk, wgu, bgu, wd, bd):
    D = xs.shape[1]
    bm = MOE_ROWS
    d_ff = wd.shape[1]
    grid_spec = pltpu.PrefetchScalarGridSpec(
        num_scalar_prefetch=1,
        grid=(n_blk,),
        in_specs=[pl.BlockSpec((bm, D), lambda i, be: (i, 0)),
                  pl.BlockSpec((None, D, 2 * d_ff), lambda i, be: (be[i], 0, 0)),
                  pl.BlockSpec((None, 1, 2 * d_ff), lambda i, be: (be[i], 0, 0)),
                  pl.BlockSpec((None, d_ff, D), lambda i, be: (be[i], 0, 0)),
                  pl.BlockSpec((None, 1, D), lambda i, be: (be[i], 0, 0))],
        out_specs=pl.BlockSpec((bm, D), lambda i, be: (i, 0)),
    )
    return pl.pallas_call(
        functools.partial(_moe_rows_kernel, d_ff=d_ff),
        out_shape=jax.ShapeDtypeStruct((n_blk * bm, D), F32),
        grid_spec=grid_spec,
        compiler_params=pltpu.CompilerParams(
            dimension_semantics=("arbitrary",), vmem_limit_bytes=VMEM_LIMIT),
        name="moe_rows",
    )(blk_expert, xs, wgu, bgu.reshape(N_EXPERTS, 1, 2 * d_ff), wd, bd.reshape(N_EXPERTS, 1, D))


def _combine_kernel(ids_hbm, y_hbm, gw_ref, x1_ref, g2_ref, fg_ref, o_ref, ids_smem, ybuf, ids_sem, row_sem,
                    *, n_steps, tm):
    g = _RowGather(ids_hbm, ids_smem, y_hbm, ybuf, ids_sem, row_sem, n_steps, TOP_K * tm)
    slot = g.begin()
    gw = gw_ref[...]
    y = gw[:, 0:1] * ybuf[slot, 0:tm, :]
    for kk in range(1, TOP_K):
        y = y + gw[:, kk:kk + 1] * ybuf[slot, kk * tm:(kk + 1) * tm, :]
    g.issue_next()
    x2 = x1_ref[...] + g2_ref[...] * y
    ms = jnp.mean(x2 * x2, axis=-1, keepdims=True)
    o_ref[...] = x2 * lax.rsqrt(ms + RMS_EPS) * fg_ref[...]
    g.finish()


def _combine(y_sorted, pos, gates, x1, gate2, final_g):
    B, S, D = x1.shape
    T = B * S
    tm = COMBINE_TOKENS
    n_steps = T // tm
    steps_per_batch = S // tm
    ids = pos.reshape(n_steps, tm, TOP_K).transpose(0, 2, 1).reshape(n_steps, TOP_K * tm)
    return pl.pallas_call(
        functools.partial(_combine_kernel, n_steps=n_steps, tm=tm),
        out_shape=jax.ShapeDtypeStruct((T, D), F32),
        grid=(n_steps,),
        in_specs=[pl.BlockSpec(memory_space=pl.ANY),
                  pl.BlockSpec(memory_space=pl.ANY),
                  pl.BlockSpec((tm, TOP_K), lambda i: (i, 0)),
                  pl.BlockSpec((tm, D), lambda i: (i, 0)),
                  pl.BlockSpec((None, 1, D), lambda i: (i // steps_per_batch, 0, 0)),
                  pl.BlockSpec((1, D), lambda i: (0, 0))],
        out_specs=pl.BlockSpec((tm, D), lambda i: (i, 0)),
        scratch_shapes=[pltpu.SMEM((2, TOP_K * tm), jnp.int32),
                        pltpu.VMEM((2, TOP_K * tm, D), F32),
                        pltpu.SemaphoreType.DMA((2,)),
                        pltpu.SemaphoreType.DMA((2,))],
        compiler_params=pltpu.CompilerParams(
            dimension_semantics=("arbitrary",), vmem_limit_bytes=VMEM_LIMIT),
        name="combine",
    )(ids, y_sorted, gates, x1.reshape(T, D), gate2, final_g).reshape(B, S, D)


def _combine_rows_kernel(yg_ref, gw_ref, x1_ref, g2_ref, fg_ref, o_ref):
    gw = gw_ref[...]
    y = gw[:, 0:1] * yg_ref[0]
    for kk in range(1, TOP_K):
        y = y + gw[:, kk:kk + 1] * yg_ref[kk]
    x2 = x1_ref[...] + g2_ref[...] * y
    ms = jnp.mean(x2 * x2, axis=-1, keepdims=True)
    o_ref[...] = x2 * lax.rsqrt(ms + RMS_EPS) * fg_ref[...]


def _combine_rows(yg, gates, x1, gate2, final_g):
    B, S, D = x1.shape
    T = B * S
    tm = 256
    steps_per_batch = S // tm
    return pl.pallas_call(
        _combine_rows_kernel,
        out_shape=jax.ShapeDtypeStruct((T, D), F32),
        grid=(T // tm,),
        in_specs=[pl.BlockSpec((TOP_K, tm, D), lambda i: (0, i, 0)),
                  pl.BlockSpec((tm, TOP_K), lambda i: (i, 0)),
                  pl.BlockSpec((tm, D), lambda i: (i, 0)),
                  pl.BlockSpec((None, 1, D), lambda i: (i // steps_per_batch, 0, 0)),
                  pl.BlockSpec((1, D), lambda i: (0, 0))],
        out_specs=pl.BlockSpec((tm, D), lambda i: (i, 0)),
        compiler_params=pltpu.CompilerParams(
            dimension_semantics=("arbitrary",), vmem_limit_bytes=VMEM_LIMIT),
        name="combine_rows",
    )(yg, gates, x1.reshape(T, D), gate2, final_g).reshape(B, S, D)


def _rotary_tables(positions):
    half = ROT_DIM // 2
    inv_freq = jnp.exp(-math.log(ROPE_THETA) * jnp.arange(0, ROT_DIM, 2, dtype=F32) / ROT_DIM)
    ang = positions.astype(F32)[:, :, None] * inv_freq
    cos, sin = jnp.cos(ang), jnp.sin(ang)
    B, S = positions.shape
    pad = ATTN_HEAD_DIM - ROT_DIM
    ct = jnp.concatenate([cos, cos, jnp.ones((B, S, pad), F32)], axis=-1)
    st = jnp.concatenate([-sin, sin, jnp.zeros((B, S, pad), F32)], axis=-1)
    return jnp.concatenate([ct, ct], axis=-1), jnp.concatenate([st, st], axis=-1)


def kernel(x, c, positions, w_ada, b_ada, norm1_g, w_in, hgrn_lb_logits, hgrn_norm_g, attn_norm_g,
           w_out, norm2_g, w_router, b_router, w_gate_up, b_gate_up, w_down, b_down, final_norm_g):
    B, S, D = x.shape
    assert w_in.shape[0] == 1, "single-layer block: the final norm is fused into the combine step"
    l = 0
    ctab, stab = _rotary_tables(positions)
    lower_bounds = jnp.cumsum(jax.nn.softmax(hgrn_lb_logits.astype(F32), axis=0), axis=0)
    mod = _ada(c, w_ada[l], b_ada[l])
    shift1, scale1, gate1, shift2, scale2, gate2 = jnp.split(mod[:, None, :], N_MOD, axis=-1)
    q, k, lf, v, gt, aq, ka, av, km = _proj(
        x, scale1, shift1, norm1_g[l][None], w_in[l].astype(BF16), lower_bounds[l][None], ctab, stab)
    o_h = _hgrn(q, k, lf, v, gt, hgrn_norm_g[l][None])
    nb = S // MOBA_BLOCK
    kmean = km.reshape(B, nb, ATTN_HEADS, ATTN_HEAD_DIM).transpose(0, 2, 1, 3)
    o_a = _moba(aq, kmean, ka, av, attn_norm_g[l][None])
    x1, h2, top_idx, gates = _mix(o_h, o_a, x, gate1, scale2, shift2, norm2_g[l][None],
                                  w_out[l].astype(BF16), w_router[l], b_router[l][None])
    tok_buf, blk_expert, pos, n_blk = _route(top_idx.reshape(B * S, TOP_K), MOE_ROWS)
    T = B * S
    xs = _sc_gather(h2.reshape(T, D), tok_buf.reshape(-1), 64)
    y_sorted = _moe_rows(xs, blk_expert, n_blk,
                         w_gate_up[l].astype(BF16), b_gate_up[l], w_down[l].astype(BF16), b_down[l])
    yg = _sc_gather(y_sorted, pos.reshape(T, TOP_K).T.reshape(-1), 64).reshape(TOP_K, T, D)
    return _combine_rows(yg, gates.reshape(T, TOP_K), x1, gate2, final_norm_g[None])
```

```python
import functools
import math

import jax
import jax.numpy as jnp
from jax import lax
from jax.experimental import pallas as pl
from jax.experimental.pallas import tpu as pltpu
from jax.experimental.pallas import tpu_sc as plsc

F32 = jnp.float32
BF16 = jnp.bfloat16
HIGHEST = lax.Precision.HIGHEST

HGRN_DK = 128
HGRN_CHUNK = 64
ATTN_HEADS = 4
ATTN_HEAD_DIM = 64
ROT_DIM = ATTN_HEAD_DIM // 4
ROPE_THETA = 500000.0
MOBA_BLOCK = 256
MOBA_TOPK = 3
N_EXPERTS = 32
TOP_K = 4
SWIGLU_ALPHA = 1.702
SWIGLU_LIMIT = 7.0
N_MOD = 6
RMS_EPS = 1e-6

EXP_CLAMP = 80.0
MOE_ROWS = 256
MOBA_ROWS = 256
PART_W = 128
VMEM_LIMIT = 56 * 1024 * 1024
SC_CORES = 2
SC_SUBCORES = 16


def _sigmoid(x):
    return 1.0 / (1.0 + jnp.exp(-x))


def _dot(a, b, **kw):
    return jnp.dot(a, b, preferred_element_type=F32, **kw)


def _dot_nt(a, b, **kw):
    return lax.dot_general(a, b, (((1,), (1,)), ((), ())), preferred_element_type=F32, **kw)


def _ada_kernel(c_ref, w_ref, b_ref, o_ref):
    c = c_ref[...]
    o_ref[...] = _dot(c * _sigmoid(c), w_ref[...], precision=HIGHEST) + b_ref[...]


def _ada(c, w_ada, b_ada):
    B, D = c.shape
    N = w_ada.shape[1]
    tn = N // 4
    c8 = jnp.zeros((8, D), F32).at[:B].set(c)
    out = pl.pallas_call(
        _ada_kernel,
        out_shape=jax.ShapeDtypeStruct((8, N), F32),
        grid=(N // tn,),
        in_specs=[pl.BlockSpec((8, D), lambda j: (0, 0)),
                  pl.BlockSpec((D, tn), lambda j: (0, j)),
                  pl.BlockSpec((1, tn), lambda j: (0, j))],
        out_specs=pl.BlockSpec((8, tn), lambda j: (0, j)),
        compiler_params=pltpu.CompilerParams(vmem_limit_bytes=VMEM_LIMIT),
        name="ada",
    )(c8, w_ada, b_ada.reshape(1, N))
    return out[:B]


def _proj_kernel(x_ref, sc_ref, sh_ref, g_ref, w_ref, lb_ref, ct_ref, st_ref,
                 q_ref, k_ref, lf_ref, v_ref, gt_ref, aq_ref, ak_ref, av_ref, km_ref,
                 *, hw, aw):
    x = x_ref[...]
    ms = jnp.mean(x * x, axis=-1, keepdims=True)
    h = x * lax.rsqrt(ms + RMS_EPS) * g_ref[...]
    h = h * (1.0 + sc_ref[...]) + sh_ref[...]
    proj = _dot(h.astype(BF16), w_ref[...])

    hq = proj[:, 0:hw]
    hf = proj[:, hw:2 * hw]
    hg = proj[:, 3 * hw:4 * hw]
    q_ref[...] = hq * _sigmoid(hq) * (HGRN_DK ** -0.5)
    lb = lb_ref[...]
    f = lb + (1.0 - lb) * _sigmoid(hf)
    k_ref[...] = 1.0 - f
    lf_ref[...] = jnp.log(f)
    v_ref[...] = proj[:, 2 * hw:3 * hw]
    gt_ref[...] = hg * _sigmoid(hg)

    ct = jnp.concatenate([ct_ref[...]] * (aw // 128), axis=1)
    st = jnp.concatenate([st_ref[...]] * (aw // 128), axis=1)
    lane = lax.broadcasted_iota(jnp.int32, ct.shape, 1) % ATTN_HEAD_DIM
    first_half = lane < (ROT_DIM // 2)

    def rot(t):
        partner = jnp.where(first_half, pltpu.roll(t, aw - ROT_DIM // 2, 1), pltpu.roll(t, ROT_DIM // 2, 1))
        return t * ct + partner * st

    base = 4 * hw
    aq = rot(proj[:, base:base + aw])
    ak = rot(proj[:, base + aw:base + 2 * aw])
    av = proj[:, base + 2 * aw:base + 3 * aw]
    km_ref[...] = jnp.mean(ak, axis=0, keepdims=True)
    lane128 = lax.broadcasted_iota(jnp.int32, (x.shape[0], 128), 1)
    for pair in range(ATTN_HEADS // 2):
        aq_ref[pair] = aq[:, pair * 128:(pair + 1) * 128]
    for hd in range(ATTN_HEADS):
        pair, half = divmod(hd, 2)
        in_head = (lane128 // ATTN_HEAD_DIM) == half
        ak_ref[hd] = jnp.where(in_head, ak[:, pair * 128:(pair + 1) * 128], 0.0).astype(BF16)
        av_ref[hd] = av[:, hd * ATTN_HEAD_DIM:(hd + 1) * ATTN_HEAD_DIM].astype(BF16)


def _proj(x, scale1, shift1, norm_g, w_in_bf16, lb, ctab, stab):
    B, S, D = x.shape
    hw = lb.shape[-1]
    aw = ATTN_HEADS * ATTN_HEAD_DIM
    tm = MOBA_BLOCK
    nb = S // MOBA_BLOCK
    n_proj = w_in_bf16.shape[1]
    row = lambda b, i: (b, i, 0)
    vec = lambda b, i: (b, 0, 0)
    head = lambda b, i: (b, 0, i, 0)
    out_shapes = (
        jax.ShapeDtypeStruct((B, S, hw), F32),
        jax.ShapeDtypeStruct((B, S, hw), F32),
        jax.ShapeDtypeStruct((B, S, hw), F32),
        jax.ShapeDtypeStruct((B, S, hw), F32),
        jax.ShapeDtypeStruct((B, S, hw), F32),
        jax.ShapeDtypeStruct((B, ATTN_HEADS // 2, S, 128), F32),
        jax.ShapeDtypeStruct((B, ATTN_HEADS, S, 128), BF16),
        jax.ShapeDtypeStruct((B, ATTN_HEADS, S, ATTN_HEAD_DIM), BF16),
        jax.ShapeDtypeStruct((B, nb, 1, aw), F32),
    )
    hspec = pl.BlockSpec((None, tm, hw), row)
    aspec = pl.BlockSpec((None, ATTN_HEADS, tm, ATTN_HEAD_DIM), head)
    return pl.pallas_call(
        functools.partial(_proj_kernel, hw=hw, aw=aw),
        out_shape=out_shapes,
        grid=(B, S // tm),
        in_specs=[pl.BlockSpec((None, tm, D), row),
                  pl.BlockSpec((None, 1, D), vec),
                  pl.BlockSpec((None, 1, D), vec),
                  pl.BlockSpec((1, D), lambda b, i: (0, 0)),
                  pl.BlockSpec((D, n_proj), lambda b, i: (0, 0)),
                  pl.BlockSpec((1, hw), lambda b, i: (0, 0)),
                  pl.BlockSpec((None, tm, 128), row),
                  pl.BlockSpec((None, tm, 128), row)],
        out_specs=(hspec, hspec, hspec, hspec, hspec,
                   pl.BlockSpec((None, ATTN_HEADS // 2, tm, 128), head),
                   pl.BlockSpec((None, ATTN_HEADS, tm, 128), head), aspec,
                   pl.BlockSpec((None, None, 1, aw), lambda b, i: (b, i, 0, 0))),
        compiler_params=pltpu.CompilerParams(
            dimension_semantics=("arbitrary", "arbitrary"), vmem_limit_bytes=VMEM_LIMIT),
        name="proj",
    )(x, scale1, shift1, norm_g, w_in_bf16, lb, ctab, stab)


def _hgrn_kernel(q_ref, k_ref, lf_ref, v_ref, gt_ref, gn_ref, o_ref, st_ref, *, n_heads, n_chunks):
    @pl.when(pl.program_id(1) == 0)
    def _():
        st_ref[...] = jnp.zeros_like(st_ref)

    C = HGRN_CHUNK
    r = lax.broadcasted_iota(jnp.int32, (C, C), 0)
    c = lax.broadcasted_iota(jnp.int32, (C, C), 1)
    tril = c <= r
    ltri = tril.astype(F32)
    gn = gn_ref[...]

    def chunk(ci, carry):
        r0 = pl.multiple_of(ci * C, C)
        rows = pl.ds(r0, C)
        b_all = _dot(ltri, lf_ref[rows, :], precision=HIGHEST)
        for hd in range(n_heads):
            sl = slice(hd * HGRN_DK, (hd + 1) * HGRN_DK)
            b = b_all[:, sl]
            b_last = b[C - 1:C, :]
            q = q_ref[rows, sl]
            k = k_ref[rows, sl]
            v = v_ref[rows, sl]
            state_t = st_ref[hd]
            o_inter = _dot_nt((q * jnp.exp(b)).astype(BF16), state_t.astype(BF16))
            rho = 0.5 * b_last
            qa = q * jnp.exp(jnp.minimum(b - rho, EXP_CLAMP))
            kb = k * jnp.exp(jnp.minimum(rho - b, EXP_CLAMP))
            scores = jnp.where(tril, _dot_nt(qa.astype(BF16), kb.astype(BF16)), 0.0)
            v16 = v.astype(BF16)
            o = o_inter + _dot(scores.astype(BF16), v16)
            kd = k * jnp.exp(b_last - b)
            st_ref[hd] = state_t * jnp.exp(b_last) + _dot(v.T.astype(BF16), kd.astype(BF16))
            ms = jnp.mean(o * o, axis=-1, keepdims=True)
            o_ref[rows, sl] = o * lax.rsqrt(ms + RMS_EPS) * gn * gt_ref[rows, sl]
        return carry

    lax.fori_loop(0, n_chunks, chunk, 0)


def _hgrn(q, k, lf, v, gt, norm_g):
    B, S, hw = q.shape
    n_heads = hw // HGRN_DK
    tc = 512
    spec = pl.BlockSpec((None, tc, hw), lambda b, i: (b, i, 0))
    return pl.pallas_call(
        functools.partial(_hgrn_kernel, n_heads=n_heads, n_chunks=tc // HGRN_CHUNK),
        out_shape=jax.ShapeDtypeStruct((B, S, hw), F32),
        grid=(B, S // tc),
        in_specs=[spec, spec, spec, spec, spec, pl.BlockSpec((1, HGRN_DK), lambda b, i: (0, 0))],
        out_specs=spec,
        scratch_shapes=[pltpu.VMEM((n_heads, HGRN_DK, HGRN_DK), F32)],
        compiler_params=pltpu.CompilerParams(
            dimension_semantics=("arbitrary", "arbitrary"), vmem_limit_bytes=VMEM_LIMIT),
        name="hgrn",
    )(q, k, lf, v, gt, norm_g)


def _sc_gather(table, idx, chunk):
    M = idx.shape[0]
    D = table.shape[1]
    n_workers = SC_CORES * SC_SUBCORES
    per_worker = M // n_workers
    assert per_worker * n_workers == M and per_worker % chunk == 0 and chunk % 8 == 0
    mesh = plsc.VectorSubcoreMesh(core_axis_name="c", subcore_axis_name="s")

    @functools.partial(
        pl.kernel, mesh=mesh,
        out_type=jax.ShapeDtypeStruct((M, D), table.dtype),
        scratch_types=[pltpu.VMEM((chunk,), jnp.int32),
                       pltpu.VMEM((chunk, D), table.dtype),
                       pltpu.SemaphoreType.DMA],
    )
    def gather_kernel(table_hbm, idx_hbm, out_hbm, idx_v, rows_v, sem):
        wid = lax.axis_index("s") * SC_CORES + lax.axis_index("c")
        base = wid * per_worker

        @pl.loop(0, per_worker // chunk)
        def _(j):
            off = pl.multiple_of(base + j * chunk, 8)
            pltpu.sync_copy(idx_hbm.at[pl.ds(off, chunk)], idx_v)
            pltpu.async_copy(table_hbm.at[idx_v], rows_v, sem).wait()
            pltpu.sync_copy(rows_v, out_hbm.at[pl.ds(off, chunk)])

    return gather_kernel(table, idx)


def _group_layout(keys, n_groups, bm, n_tiles):
    n = keys.shape[0]
    a_ids = jnp.arange(n, dtype=jnp.int32)
    sorted_k, order = lax.sort((keys, a_ids), num_keys=1, is_stable=True)
    n_hi = n_groups // 32 + 1
    oh_hi = ((keys // 32)[:, None] == jnp.arange(n_hi, dtype=jnp.int32)[None, :]).astype(BF16)
    oh_lo = ((keys % 32)[:, None] == jnp.arange(32, dtype=jnp.int32)[None, :]).astype(BF16)
    counts = jnp.einsum('ah,al->hl', oh_hi, oh_lo, preferred_element_type=F32)
    counts = counts.reshape(-1)[:n_groups].astype(jnp.int32)
    group_end = jnp.cumsum(counts)
    group_start = group_end - counts
    padded = (counts + bm - 1) // bm * bm
    pad_end = jnp.cumsum(padded)
    pad_start = pad_end - padded
    tile_start = jnp.arange(n_tiles, dtype=jnp.int32) * bm
    tile_group = jnp.minimum(
        jnp.sum((pad_end[None, :] <= tile_start[:, None]).astype(jnp.int32), axis=1), n_groups - 1)
    shift = pad_start - group_start
    src = (tile_start - shift[tile_group])[:, None] + jnp.arange(bm, dtype=jnp.int32)[None, :]
    valid = src < group_end[tile_group][:, None]
    row_src = jnp.where(valid, order[jnp.clip(src, 0, n - 1)], 0)
    shift_ext = jnp.concatenate([shift, jnp.zeros((1,), jnp.int32)])
    dest_sorted = a_ids + shift_ext[sorted_k]
    _, pos = lax.sort((order, dest_sorted), num_keys=1)
    pos = jnp.where(keys < n_groups, pos, 0)
    n_used = (pad_end[-1] // bm).astype(jnp.int32).reshape(1)
    return row_src, tile_group.astype(jnp.int32), pos, n_used


def _partial_row(acc, m, l):
    pad = jnp.zeros((acc.shape[0], PART_W - ATTN_HEAD_DIM - 2), F32)
    return jnp.concatenate([acc, m, l, pad], axis=1)


def _attend(q, k, v, mask=None):
    s = _dot_nt((q * (ATTN_HEAD_DIM ** -0.5)).astype(BF16), k)
    if mask is not None:
        s = jnp.where(mask, s, -jnp.inf)
    m = jnp.max(s, axis=1, keepdims=True)
    p = jnp.exp(s - m)
    l = jnp.sum(p, axis=1, keepdims=True)
    return _partial_row(_dot(p.astype(BF16), v), m, l)


def _moba_sel_kernel(q_ref, km_ref, k_ref, v_ref, sel_ref, own_ref, *, n_blocks):
    j = pl.program_id(2)
    T = MOBA_BLOCK
    q = q_ref[...]
    gate = _dot_nt(q, km_ref[...], precision=HIGHEST)
    lane = lax.broadcasted_iota(jnp.int32, gate.shape, 1)
    neg_inf = jnp.float32(-jnp.inf)
    gate = jnp.where(lane < j, gate, neg_inf)
    picks = []
    for _ in range(MOBA_TOPK):
        m = jnp.max(gate, axis=1, keepdims=True)
        first = jnp.min(jnp.where(gate == m, lane, n_blocks), axis=1, keepdims=True)
        picks.append(jnp.where(m > neg_inf, first, -1))
        gate = jnp.where(lane == first, neg_inf, gate)
    sel_ref[...] = jnp.concatenate(picks, axis=1)
    rr = lax.broadcasted_iota(jnp.int32, (T, T), 0)
    cc = lax.broadcasted_iota(jnp.int32, (T, T), 1)
    own_ref[...] = _attend(q, k_ref[...], v_ref[...], mask=cc <= rr)


def _moba_sel(aq, kmean, ak, av):
    B, H, S, hd = av.shape
    nb = S // MOBA_BLOCK
    T = MOBA_BLOCK
    blk = lambda b, h, j: (b, h, j, 0)
    return pl.pallas_call(
        functools.partial(_moba_sel_kernel, n_blocks=nb),
        out_shape=(jax.ShapeDtypeStruct((B, H, S, MOBA_TOPK), jnp.int32),
                   jax.ShapeDtypeStruct((B, H, S, PART_W), F32)),
        grid=(B, H, nb),
        in_specs=[pl.BlockSpec((None, None, T, 128), lambda b, h, j: (b, h // 2, j, 0)),
                  pl.BlockSpec((None, None, nb, 128), lambda b, h, j: (b, h, 0, 0)),
                  pl.BlockSpec((None, None, T, 128), blk),
                  pl.BlockSpec((None, None, T, hd), blk)],
        out_specs=(pl.BlockSpec((None, None, T, MOBA_TOPK), blk),
                   pl.BlockSpec((None, None, T, PART_W), blk)),
        compiler_params=pltpu.CompilerParams(
            dimension_semantics=("arbitrary", "arbitrary", "arbitrary"), vmem_limit_bytes=VMEM_LIMIT),
        name="moba_sel",
    )(aq, kmean, ak, av)


def _moba_blk_kernel(tg_ref, nu_ref, q_ref, k_ref, v_ref, o_ref):
    i = pl.program_id(0)

    @pl.when(i < nu_ref[0])
    def _():
        o_ref[...] = _attend(q_ref[...], k_ref[...], v_ref[...])

    @pl.when(i >= nu_ref[0])
    def _():
        o_ref[...] = jnp.zeros_like(o_ref)


def _moba_blk(qs, tile_group, n_used, ak, av):
    B, H, S, hd = av.shape
    nb = S // MOBA_BLOCK
    R = MOBA_ROWS
    n_tiles = qs.shape[0] // R
    kv = lambda i, tg, nu: (tg[i] // nb, tg[i] % nb, 0, 0)
    grid_spec = pltpu.PrefetchScalarGridSpec(
        num_scalar_prefetch=2,
        grid=(n_tiles,),
        in_specs=[pl.BlockSpec((R, 128), lambda i, tg, nu: (i, 0)),
                  pl.BlockSpec((None, None, MOBA_BLOCK, 128), kv),
                  pl.BlockSpec((None, None, MOBA_BLOCK, hd), kv)],
        out_specs=pl.BlockSpec((R, PART_W), lambda i, tg, nu: (i, 0)),
    )
    return pl.pallas_call(
        _moba_blk_kernel,
        out_shape=jax.ShapeDtypeStruct((n_tiles * R, PART_W), F32),
        grid_spec=grid_spec,
        compiler_params=pltpu.CompilerParams(
            dimension_semantics=("arbitrary",), vmem_limit_bytes=VMEM_LIMIT),
        name="moba_blk",
    )(tile_group, n_used, qs, ak.reshape(B * H, nb, MOBA_BLOCK, 128), av.reshape(B * H, nb, MOBA_BLOCK, hd))


def _moba_merge_kernel(sel_ref, own_ref, pg_ref, g_ref, o_ref):
    hd = ATTN_HEAD_DIM
    own = own_ref[...]
    sel = sel_ref[...]
    neg_inf = jnp.float32(-jnp.inf)
    parts = [(own[:, :hd], own[:, hd:hd + 1], own[:, hd + 1:hd + 2])]
    for s in range(MOBA_TOPK):
        valid = sel[:, s:s + 1] >= 0
        p = pg_ref[s]
        parts.append((jnp.where(valid, p[:, :hd], 0.0),
                      jnp.where(valid, p[:, hd:hd + 1], neg_inf),
                      jnp.where(valid, p[:, hd + 1:hd + 2], 0.0)))
    m_all = parts[0][1]
    for _, m, _ in parts[1:]:
        m_all = jnp.maximum(m_all, m)
    num = jnp.zeros_like(parts[0][0])
    den = jnp.zeros_like(m_all)
    for acc, m, l in parts:
        w = jnp.exp(m - m_all)
        num = num + w * acc
        den = den + w * l
    o = num / den
    ms = jnp.mean(o * o, axis=-1, keepdims=True)
    o_ref[...] = o * lax.rsqrt(ms + RMS_EPS) * g_ref[...]


def _moba_merge(sel, own, pg, norm_g):
    n = sel.shape[0]
    T = MOBA_BLOCK
    row = lambda i: (i, 0)
    return pl.pallas_call(
        _moba_merge_kernel,
        out_shape=jax.ShapeDtypeStruct((n, ATTN_HEAD_DIM), F32),
        grid=(n // T,),
        in_specs=[pl.BlockSpec((T, MOBA_TOPK), row),
                  pl.BlockSpec((T, PART_W), row),
                  pl.BlockSpec((MOBA_TOPK, T, PART_W), lambda i: (0, i, 0)),
                  pl.BlockSpec((1, ATTN_HEAD_DIM), lambda i: (0, 0))],
        out_specs=pl.BlockSpec((T, ATTN_HEAD_DIM), row),
        compiler_params=pltpu.CompilerParams(
            dimension_semantics=("arbitrary",), vmem_limit_bytes=VMEM_LIMIT),
        name="moba_merge",
    )(sel, own, pg, norm_g)


def _moba(aq, km, ak, av, norm_g):
    B, H, S, hd = av.shape
    nb = S // MOBA_BLOCK
    n_q = B * H * S
    kmp = km.reshape(B, nb, H // 2, 128)
    half = jnp.arange(128, dtype=jnp.int32) // hd
    kmean = jnp.stack([jnp.where(half == h % 2, kmp[:, :, h // 2, :], 0.0) for h in range(H)], axis=1)
    sel, own = _moba_sel(aq, kmean, ak, av)
    bh = jnp.arange(B * H, dtype=jnp.int32)[:, None, None]
    n_groups = B * H * nb
    keys = jnp.where(sel.reshape(B * H, S, MOBA_TOPK) >= 0, bh * nb + sel.reshape(B * H, S, MOBA_TOPK), n_groups)
    n_tiles = (n_q * MOBA_TOPK) // MOBA_ROWS + n_groups
    row_src, tile_group, pos, n_used = _group_layout(keys.reshape(-1), n_groups, MOBA_ROWS, n_tiles)
    q_id = (row_src // MOBA_TOPK).reshape(-1)
    pair_row = (q_id // (H * S) * (H // 2) + (q_id // S) % H // 2) * S + q_id % S
    qs = _sc_gather(aq.reshape(B * (H // 2) * S, 128), pair_row, 256)
    parts = _moba_blk(qs, tile_group, n_used, ak, av)
    pg = _sc_gather(parts, pos.reshape(n_q, MOBA_TOPK).T.reshape(-1), 256).reshape(MOBA_TOPK, n_q, PART_W)
    o = _moba_merge(sel.reshape(n_q, MOBA_TOPK), own.reshape(n_q, PART_W), pg, norm_g)
    return o.reshape(B, H, S, hd)


def _mix_kernel(oh_ref, oa_ref, x_ref, g1_ref, sc2_ref, sh2_ref, n2_ref, wo_ref, wr_ref, br_ref,
                x1_ref, h2_ref, idx_ref, gw_ref):
    cat = jnp.concatenate([oh_ref[...]] + [oa_ref[hd] for hd in range(ATTN_HEADS)], axis=1)
    mix = _dot(cat.astype(BF16), wo_ref[...])
    x1 = x_ref[...] + g1_ref[...] * mix
    x1_ref[...] = x1
    ms = jnp.mean(x1 * x1, axis=-1, keepdims=True)
    h2 = x1 * lax.rsqrt(ms + RMS_EPS) * n2_ref[...]
    h2 = h2 * (1.0 + sc2_ref[...]) + sh2_ref[...]
    h2_ref[...] = h2
    logits = _dot(h2, wr_ref[...], precision=HIGHEST) + br_ref[...]
    lane = lax.broadcasted_iota(jnp.int32, logits.shape, 1)
    neg_inf = jnp.float32(-jnp.inf)
    vals, idxs = [], []
    for _ in range(TOP_K):
        m = jnp.max(logits, axis=1, keepdims=True)
        first = jnp.min(jnp.where(logits == m, lane, N_EXPERTS), axis=1, keepdims=True)
        vals.append(m)
        idxs.append(first)
        logits = jnp.where(lane == first, neg_inf, logits)
    e = [jnp.exp(v - vals[0]) for v in vals]
    denom = e[0] + e[1] + e[2] + e[3]
    idx_ref[...] = jnp.concatenate(idxs, axis=1)
    gw_ref[...] = jnp.concatenate([ei / denom for ei in e], axis=1)


def _mix(oh, oa, x, gate1, scale2, shift2, norm2_g, w_out_bf16, w_router, b_router):
    B, S, D = x.shape
    hw = oh.shape[-1]
    tm = 256
    row = lambda b, i: (b, i, 0)
    vec = lambda b, i: (b, 0, 0)
    const = lambda b, i: (0, 0)
    return pl.pallas_call(
        _mix_kernel,
        out_shape=(jax.ShapeDtypeStruct((B, S, D), F32),
                   jax.ShapeDtypeStruct((B, S, D), F32),
                   jax.ShapeDtypeStruct((B, S, TOP_K), jnp.int32),
                   jax.ShapeDtypeStruct((B, S, TOP_K), F32)),
        grid=(B, S // tm),
        in_specs=[pl.BlockSpec((None, tm, hw), row),
                  pl.BlockSpec((None, ATTN_HEADS, tm, ATTN_HEAD_DIM), lambda b, i: (b, 0, i, 0)),
                  pl.BlockSpec((None, tm, D), row),
                  pl.BlockSpec((None, 1, D), vec),
                  pl.BlockSpec((None, 1, D), vec),
                  pl.BlockSpec((None, 1, D), vec),
                  pl.BlockSpec((1, D), const),
                  pl.BlockSpec((D, D), const),
                  pl.BlockSpec((D, N_EXPERTS), const),
                  pl.BlockSpec((1, N_EXPERTS), const)],
        out_specs=(pl.BlockSpec((None, tm, D), row),
                   pl.BlockSpec((None, tm, D), row),
                   pl.BlockSpec((None, tm, TOP_K), row),
                   pl.BlockSpec((None, tm, TOP_K), row)),
        compiler_params=pltpu.CompilerParams(
            dimension_semantics=("arbitrary", "arbitrary"), vmem_limit_bytes=VMEM_LIMIT),
        name="mix",
    )(oh, oa, x, gate1, scale2, shift2, norm2_g, w_out_bf16, w_router, b_router)


def _moe_rows_kernel(be_ref, nu_ref, x_ref, wgu_ref, bgu_ref, wd_ref, bd_ref, y_ref, *, d_ff):
    i = pl.program_id(0)

    @pl.when(i < nu_ref[0])
    def _():
        gu = _dot(x_ref[...].astype(BF16), wgu_ref[...]) + bgu_ref[...]
        gate = jnp.minimum(gu[:, :d_ff], SWIGLU_LIMIT)
        up = jnp.clip(gu[:, d_ff:], -SWIGLU_LIMIT, SWIGLU_LIMIT)
        act = (up + 1.0) * gate * _sigmoid(SWIGLU_ALPHA * gate)
        y_ref[...] = _dot(act.astype(BF16), wd_ref[...]) + bd_ref[...]

    @pl.when(i >= nu_ref[0])
    def _():
        y_ref[...] = jnp.zeros_like(y_ref)


def _moe_rows(xs, blk_expert, n_used, wgu, bgu, wd, bd):
    D = xs.shape[1]
    bm = MOE_ROWS
    n_blk = xs.shape[0] // bm
    d_ff = wd.shape[1]
    wsel = lambda i, be, nu: (be[i], 0, 0)
    grid_spec = pltpu.PrefetchScalarGridSpec(
        num_scalar_prefetch=2,
        grid=(n_blk,),
        in_specs=[pl.BlockSpec((bm, D), lambda i, be, nu: (i, 0)),
                  pl.BlockSpec((None, D, 2 * d_ff), wsel),
                  pl.BlockSpec((None, 1, 2 * d_ff), wsel),
                  pl.BlockSpec((None, d_ff, D), wsel),
                  pl.BlockSpec((None, 1, D), wsel)],
        out_specs=pl.BlockSpec((bm, D), lambda i, be, nu: (i, 0)),
    )
    return pl.pallas_call(
        functools.partial(_moe_rows_kernel, d_ff=d_ff),
        out_shape=jax.ShapeDtypeStruct((n_blk * bm, D), F32),
        grid_spec=grid_spec,
        compiler_params=pltpu.CompilerParams(
            dimension_semantics=("arbitrary",), vmem_limit_bytes=VMEM_LIMIT),
        name="moe_rows",
    )(blk_expert, n_used, xs, wgu, bgu.reshape(N_EXPERTS, 1, 2 * d_ff), wd, bd.reshape(N_EXPERTS, 1, D))


def _combine_rows_kernel(yg_ref, gw_ref, x1_ref, g2_ref, fg_ref, o_ref):
    gw = gw_ref[...]
    y = gw[:, 0:1] * yg_ref[0]
    for kk in range(1, TOP_K):
        y = y + gw[:, kk:kk + 1] * yg_ref[kk]
    x2 = x1_ref[...] + g2_ref[...] * y
    ms = jnp.mean(x2 * x2, axis=-1, keepdims=True)
    o_ref[...] = x2 * lax.rsqrt(ms + RMS_EPS) * fg_ref[...]


def _combine_rows(yg, gates, x1, gate2, final_g):
    B, S, D = x1.shape
    T = B * S
    tm = 256
    steps_per_batch = S // tm
    return pl.pallas_call(
        _combine_rows_kernel,
        out_shape=jax.ShapeDtypeStruct((T, D), F32),
        grid=(T // tm,),
        in_specs=[pl.BlockSpec((TOP_K, tm, D), lambda i: (0, i, 0)),
                  pl.BlockSpec((tm, TOP_K), lambda i: (i, 0)),
                  pl.BlockSpec((tm, D), lambda i: (i, 0)),
                  pl.BlockSpec((None, 1, D), lambda i: (i // steps_per_batch, 0, 0)),
                  pl.BlockSpec((1, D), lambda i: (0, 0))],
        out_specs=pl.BlockSpec((tm, D), lambda i: (i, 0)),
        compiler_params=pltpu.CompilerParams(
            dimension_semantics=("arbitrary",), vmem_limit_bytes=VMEM_LIMIT),
        name="combine_rows",
    )(yg, gates, x1.reshape(T, D), gate2, final_g).reshape(B, S, D)


def _rotary_tables(positions):
    inv_freq = jnp.exp(-math.log(ROPE_THETA) * jnp.arange(0, ROT_DIM, 2, dtype=F32) / ROT_DIM)
    ang = positions.astype(F32)[:, :, None] * inv_freq
    cos, sin = jnp.cos(ang), jnp.sin(ang)
    B, S = positions.shape
    pad = ATTN_HEAD_DIM - ROT_DIM
    ct = jnp.concatenate([cos, cos, jnp.ones((B, S, pad), F32)], axis=-1)
    st = jnp.concatenate([-sin, sin, jnp.zeros((B, S, pad), F32)], axis=-1)
    return jnp.concatenate([ct, ct], axis=-1), jnp.concatenate([st, st], axis=-1)


def kernel(x, c, positions, w_ada, b_ada, norm1_g, w_in, hgrn_lb_logits, hgrn_norm_g, attn_norm_g,
           w_out, norm2_g, w_router, b_router, w_gate_up, b_gate_up, w_down, b_down, final_norm_g):
    B, S, D = x.shape
    T = B * S
    assert w_in.shape[0] == 1, "single-layer block: the final norm is fused into the combine step"
    l = 0
    ctab, stab = _rotary_tables(positions)
    lower_bounds = jnp.cumsum(jax.nn.softmax(hgrn_lb_logits.astype(F32), axis=0), axis=0)
    mod = _ada(c, w_ada[l], b_ada[l])
    shift1, scale1, gate1, shift2, scale2, gate2 = jnp.split(mod[:, None, :], N_MOD, axis=-1)
    q, k, lf, v, gt, aq, ak, av, km = _proj(
        x, scale1, shift1, norm1_g[l][None], w_in[l].astype(BF16), lower_bounds[l][None], ctab, stab)
    o_a = _moba(aq, km, ak, av, attn_norm_g[l][None])
    o_h = _hgrn(q, k, lf, v, gt, hgrn_norm_g[l][None])
    x1, h2, top_idx, gates = _mix(o_h, o_a, x, gate1, scale2, shift2, norm2_g[l][None],
                                  w_out[l].astype(BF16), w_router[l], b_router[l][None])
    n_blk = (T * TOP_K) // MOE_ROWS + N_EXPERTS
    row_src, blk_expert, pos, n_used = _group_layout(top_idx.reshape(-1), N_EXPERTS, MOE_ROWS, n_blk)
    xs = _sc_gather(h2.reshape(T, D), (row_src // TOP_K).reshape(-1), 64)
    y_sorted = _moe_rows(xs, blk_expert, n_used,
                         w_gate_up[l].astype(BF16), b_gate_up[l], w_down[l].astype(BF16), b_down[l])
    yg = _sc_gather(y_sorted, pos.reshape(T, TOP_K).T.reshape(-1), 64).reshape(TOP_K, T, D)
    return _combine_rows(yg, gates.reshape(T, TOP_K), x1, gate2, final_norm_g[None])
```

```python
import functools
import math

import jax
import jax.numpy as jnp
from jax import lax
from jax.experimental import pallas as pl
from jax.experimental.pallas import tpu as pltpu
from jax.experimental.pallas import tpu_sc as plsc

F32 = jnp.float32
BF16 = jnp.bfloat16
HIGHEST = lax.Precision.HIGHEST

HGRN_DK = 128
HGRN_CHUNK = 64
ATTN_HEADS = 4
ATTN_HEAD_DIM = 64
ROT_DIM = ATTN_HEAD_DIM // 4
ROPE_THETA = 500000.0
MOBA_BLOCK = 256
MOBA_TOPK = 3
N_EXPERTS = 32
TOP_K = 4
SWIGLU_ALPHA = 1.702
SWIGLU_LIMIT = 7.0
N_MOD = 6
RMS_EPS = 1e-6

EXP_CLAMP = 80.0
MOE_ROWS = 256
MOBA_ROWS = 256
PART_W = 128
VMEM_LIMIT = 56 * 1024 * 1024
SC_CORES = 2
SC_SUBCORES = 16


def _sigmoid(x):
    return 1.0 / (1.0 + jnp.exp(-x))


def _dot(a, b, **kw):
    return jnp.dot(a, b, preferred_element_type=F32, **kw)


def _dot_nt(a, b, **kw):
    return lax.dot_general(a, b, (((1,), (1,)), ((), ())), preferred_element_type=F32, **kw)


def _ada_kernel(c_ref, w_ref, b_ref, o_ref):
    c = c_ref[...]
    o_ref[...] = _dot(c * _sigmoid(c), w_ref[...], precision=HIGHEST) + b_ref[...]


def _ada(c, w_ada, b_ada):
    B, D = c.shape
    N = w_ada.shape[1]
    tn = N // 4
    c8 = jnp.zeros((8, D), F32).at[:B].set(c)
    out = pl.pallas_call(
        _ada_kernel,
        out_shape=jax.ShapeDtypeStruct((8, N), F32),
        grid=(N // tn,),
        in_specs=[pl.BlockSpec((8, D), lambda j: (0, 0)),
                  pl.BlockSpec((D, tn), lambda j: (0, j)),
                  pl.BlockSpec((1, tn), lambda j: (0, j))],
        out_specs=pl.BlockSpec((8, tn), lambda j: (0, j)),
        compiler_params=pltpu.CompilerParams(vmem_limit_bytes=VMEM_LIMIT),
        name="ada",
    )(c8, w_ada, b_ada.reshape(1, N))
    return out[:B]


def _proj_kernel(x_ref, sc_ref, sh_ref, g_ref, w_ref, lb_ref, ct_ref, st_ref,
                 q_ref, k_ref, lf_ref, v_ref, gt_ref, aq_ref, ak_ref, av_ref, km_ref,
                 *, hw, aw):
    x = x_ref[...]
    ms = jnp.mean(x * x, axis=-1, keepdims=True)
    h = x * lax.rsqrt(ms + RMS_EPS) * g_ref[...]
    h = h * (1.0 + sc_ref[...]) + sh_ref[...]
    proj = _dot(h.astype(BF16), w_ref[...])

    hq = proj[:, 0:hw]
    hf = proj[:, hw:2 * hw]
    hg = proj[:, 3 * hw:4 * hw]
    q_ref[...] = hq * _sigmoid(hq) * (HGRN_DK ** -0.5)
    lb = lb_ref[...]
    f = lb + (1.0 - lb) * _sigmoid(hf)
    k_ref[...] = 1.0 - f
    lf_ref[...] = jnp.log(f)
    v_ref[...] = proj[:, 2 * hw:3 * hw]
    gt_ref[...] = hg * _sigmoid(hg)

    ct = jnp.concatenate([ct_ref[...]] * (aw // 128), axis=1)
    st = jnp.concatenate([st_ref[...]] * (aw // 128), axis=1)
    lane = lax.broadcasted_iota(jnp.int32, ct.shape, 1) % ATTN_HEAD_DIM
    first_half = lane < (ROT_DIM // 2)

    def rot(t):
        partner = jnp.where(first_half, pltpu.roll(t, aw - ROT_DIM // 2, 1), pltpu.roll(t, ROT_DIM // 2, 1))
        return t * ct + partner * st

    base = 4 * hw
    aq = rot(proj[:, base:base + aw])
    ak = rot(proj[:, base + aw:base + 2 * aw])
    av = proj[:, base + 2 * aw:base + 3 * aw]
    km_ref[...] = jnp.mean(ak, axis=0, keepdims=True)
    lane128 = lax.broadcasted_iota(jnp.int32, (x.shape[0], 128), 1)
    for pair in range(ATTN_HEADS // 2):
        aq_ref[pair] = aq[:, pair * 128:(pair + 1) * 128]
    for hd in range(ATTN_HEADS):
        pair, half = divmod(hd, 2)
        in_head = (lane128 // ATTN_HEAD_DIM) == half
        ak_ref[hd] = jnp.where(in_head, ak[:, pair * 128:(pair + 1) * 128], 0.0).astype(BF16)
        av_ref[hd] = av[:, hd * ATTN_HEAD_DIM:(hd + 1) * ATTN_HEAD_DIM].astype(BF16)


def _proj(x, scale1, shift1, norm_g, w_in_bf16, lb, ctab, stab):
    B, S, D = x.shape
    hw = lb.shape[-1]
    aw = ATTN_HEADS * ATTN_HEAD_DIM
    tm = MOBA_BLOCK
    nb = S // MOBA_BLOCK
    n_proj = w_in_bf16.shape[1]
    row = lambda b, i: (b, i, 0)
    vec = lambda b, i: (b, 0, 0)
    head = lambda b, i: (b, 0, i, 0)
    out_shapes = (
        jax.ShapeDtypeStruct((B, S, hw), F32),
        jax.ShapeDtypeStruct((B, S, hw), F32),
        jax.ShapeDtypeStruct((B, S, hw), F32),
        jax.ShapeDtypeStruct((B, S, hw), F32),
        jax.ShapeDtypeStruct((B, S, hw), F32),
        jax.ShapeDtypeStruct((B, ATTN_HEADS // 2, S, 128), F32),
        jax.ShapeDtypeStruct((B, ATTN_HEADS, S, 128), BF16),
        jax.ShapeDtypeStruct((B, ATTN_HEADS, S, ATTN_HEAD_DIM), BF16),
        jax.ShapeDtypeStruct((B, nb, 1, aw), F32),
    )
    hspec = pl.BlockSpec((None, tm, hw), row)
    aspec = pl.BlockSpec((None, ATTN_HEADS, tm, ATTN_HEAD_DIM), head)
    return pl.pallas_call(
        functools.partial(_proj_kernel, hw=hw, aw=aw),
        out_shape=out_shapes,
        grid=(B, S // tm),
        in_specs=[pl.BlockSpec((None, tm, D), row),
                  pl.BlockSpec((None, 1, D), vec),
                  pl.BlockSpec((None, 1, D), vec),
                  pl.BlockSpec((1, D), lambda b, i: (0, 0)),
                  pl.BlockSpec((D, n_proj), lambda b, i: (0, 0)),
                  pl.BlockSpec((1, hw), lambda b, i: (0, 0)),
                  pl.BlockSpec((None, tm, 128), row),
                  pl.BlockSpec((None, tm, 128), row)],
        out_specs=(hspec, hspec, hspec, hspec, hspec,
                   pl.BlockSpec((None, ATTN_HEADS // 2, tm, 128), head),
                   pl.BlockSpec((None, ATTN_HEADS, tm, 128), head), aspec,
                   pl.BlockSpec((None, None, 1, aw), lambda b, i: (b, i, 0, 0))),
        compiler_params=pltpu.CompilerParams(
            dimension_semantics=("arbitrary", "arbitrary"), vmem_limit_bytes=VMEM_LIMIT),
        name="proj",
    )(x, scale1, shift1, norm_g, w_in_bf16, lb, ctab, stab)


def _hgrn_kernel(q_ref, k_ref, lf_ref, v_ref, gt_ref, gn_ref, o_ref, st_ref, *, n_heads, n_chunks):
    @pl.when(pl.program_id(1) == 0)
    def _():
        st_ref[...] = jnp.zeros_like(st_ref)

    C = HGRN_CHUNK
    r = lax.broadcasted_iota(jnp.int32, (C, C), 0)
    c = lax.broadcasted_iota(jnp.int32, (C, C), 1)
    tril = c <= r
    ltri = tril.astype(F32)
    gn = gn_ref[...]

    def chunk(ci, carry):
        r0 = pl.multiple_of(ci * C, C)
        rows = pl.ds(r0, C)
        b_all = _dot(ltri, lf_ref[rows, :], precision=HIGHEST)
        for hd in range(n_heads):
            sl = slice(hd * HGRN_DK, (hd + 1) * HGRN_DK)
            b = b_all[:, sl]
            b_last = b[C - 1:C, :]
            q = q_ref[rows, sl]
            k = k_ref[rows, sl]
            v = v_ref[rows, sl]
            state_t = st_ref[hd]
            o_inter = _dot_nt((q * jnp.exp(b)).astype(BF16), state_t.astype(BF16))
            rho = 0.5 * b_last
            qa = q * jnp.exp(jnp.minimum(b - rho, EXP_CLAMP))
            kb = k * jnp.exp(jnp.minimum(rho - b, EXP_CLAMP))
            scores = jnp.where(tril, _dot_nt(qa.astype(BF16), kb.astype(BF16)), 0.0)
            v16 = v.astype(BF16)
            o = o_inter + _dot(scores.astype(BF16), v16)
            kd = k * jnp.exp(b_last - b)
            st_ref[hd] = state_t * jnp.exp(b_last) + _dot(v.T.astype(BF16), kd.astype(BF16))
            ms = jnp.mean(o * o, axis=-1, keepdims=True)
            o_ref[rows, sl] = o * lax.rsqrt(ms + RMS_EPS) * gn * gt_ref[rows, sl]
        return carry

    lax.fori_loop(0, n_chunks, chunk, 0)


def _hgrn(q, k, lf, v, gt, norm_g):
    B, S, hw = q.shape
    n_heads = hw // HGRN_DK
    tc = 512
    spec = pl.BlockSpec((None, tc, hw), lambda b, i: (b, i, 0))
    return pl.pallas_call(
        functools.partial(_hgrn_kernel, n_heads=n_heads, n_chunks=tc // HGRN_CHUNK),
        out_shape=jax.ShapeDtypeStruct((B, S, hw), F32),
        grid=(B, S // tc),
        in_specs=[spec, spec, spec, spec, spec, pl.BlockSpec((1, HGRN_DK), lambda b, i: (0, 0))],
        out_specs=spec,
        scratch_shapes=[pltpu.VMEM((n_heads, HGRN_DK, HGRN_DK), F32)],
        compiler_params=pltpu.CompilerParams(
            dimension_semantics=("arbitrary", "arbitrary"), vmem_limit_bytes=VMEM_LIMIT),
        name="hgrn",
    )(q, k, lf, v, gt, norm_g)


def _sc_gather(table, idx, chunk):
    M = idx.shape[0]
    D = table.shape[1]
    n_workers = SC_CORES * SC_SUBCORES
    per_worker = M // n_workers
    assert per_worker * n_workers == M and per_worker % chunk == 0 and chunk % 8 == 0
    mesh = plsc.VectorSubcoreMesh(core_axis_name="c", subcore_axis_name="s")

    @functools.partial(
        pl.kernel, mesh=mesh,
        out_type=jax.ShapeDtypeStruct((M, D), table.dtype),
        scratch_types=[pltpu.VMEM((chunk,), jnp.int32),
                       pltpu.VMEM((chunk, D), table.dtype),
                       pltpu.SemaphoreType.DMA],
    )
    def gather_kernel(table_hbm, idx_hbm, out_hbm, idx_v, rows_v, sem):
        wid = lax.axis_index("s") * SC_CORES + lax.axis_index("c")
        base = wid * per_worker

        @pl.loop(0, per_worker // chunk)
        def _(j):
            off = pl.multiple_of(base + j * chunk, 8)
            pltpu.sync_copy(idx_hbm.at[pl.ds(off, chunk)], idx_v)
            pltpu.async_copy(table_hbm.at[idx_v], rows_v, sem).wait()
            pltpu.sync_copy(rows_v, out_hbm.at[pl.ds(off, chunk)])

    return gather_kernel(table, idx)


def _group_layout(keys, n_groups, bm, n_tiles):
    n = keys.shape[0]
    n_rows = n_tiles * bm
    n_fill = n_rows - n
    n_hi = n_groups // 32 + 1
    oh_hi = ((keys // 32)[:, None] == jnp.arange(n_hi, dtype=jnp.int32)[None, :]).astype(BF16)
    oh_lo = ((keys % 32)[:, None] == jnp.arange(32, dtype=jnp.int32)[None, :]).astype(BF16)
    counts = jnp.einsum('ah,al->hl', oh_hi, oh_lo, preferred_element_type=F32)
    counts = counts.reshape(-1)[:n_groups].astype(jnp.int32)
    padded = (counts + bm - 1) // bm * bm
    fill_end = jnp.cumsum(padded - counts)
    f_ids = jnp.arange(n_fill, dtype=jnp.int32)
    fill_key = jnp.sum((fill_end[None, :] <= f_ids[:, None]).astype(jnp.int32), axis=1)
    all_keys = jnp.concatenate([keys, fill_key])
    all_src = jnp.concatenate([jnp.arange(n, dtype=jnp.int32), (f_ids * 61) % n])
    all_ids = jnp.arange(n_rows, dtype=jnp.int32)
    sorted_k, row_src, order = lax.sort((all_keys, all_src, all_ids), num_keys=1, is_stable=True)
    tile_group = jnp.minimum(sorted_k[::bm], n_groups - 1)
    _, pos = lax.sort((order, all_ids), num_keys=1)
    n_used = (jnp.sum(padded) // bm).astype(jnp.int32).reshape(1)
    return row_src, tile_group, pos[:n], n_used


def _attend(q, k, v, mask=None):
    s = _dot_nt((q * (ATTN_HEAD_DIM ** -0.5)).astype(BF16), k)
    if mask is not None:
        s = jnp.where(mask, s, -jnp.inf)
    m = jnp.max(s, axis=1, keepdims=True)
    p = jnp.exp(s - m)
    l = jnp.sum(p, axis=1, keepdims=True)
    o = _dot(p.astype(BF16), v) / l
    lse = jnp.broadcast_to(m + jnp.log(l), (o.shape[0], PART_W - ATTN_HEAD_DIM))
    return jnp.concatenate([o, lse], axis=1)


def _null_partial(rows):
    lane = lax.broadcasted_iota(jnp.int32, (rows, PART_W), 1)
    return jnp.where(lane < ATTN_HEAD_DIM, 0.0, -jnp.inf).astype(F32)


def _moba_sel_kernel(q_ref, km_ref, k_ref, v_ref, sel_ref, own_ref, *, n_blocks):
    j = pl.program_id(2)
    T = MOBA_BLOCK
    q = q_ref[...]
    gate = _dot_nt(q, km_ref[...], precision=HIGHEST)
    lane = lax.broadcasted_iota(jnp.int32, gate.shape, 1)
    neg_inf = jnp.float32(-jnp.inf)
    gate = jnp.where(lane < j, gate, neg_inf)
    picks = []
    for _ in range(MOBA_TOPK):
        m = jnp.max(gate, axis=1, keepdims=True)
        first = jnp.min(jnp.where(gate == m, lane, n_blocks), axis=1, keepdims=True)
        picks.append(jnp.where(m > neg_inf, first, -1))
        gate = jnp.where(lane == first, neg_inf, gate)
    sel_ref[...] = jnp.concatenate(picks, axis=1)
    rr = lax.broadcasted_iota(jnp.int32, (T, T), 0)
    cc = lax.broadcasted_iota(jnp.int32, (T, T), 1)
    own_ref[...] = _attend(q, k_ref[...], v_ref[...], mask=cc <= rr)


def _moba_sel(aq, kmean, ak, av):
    B, H, S, hd = av.shape
    nb = S // MOBA_BLOCK
    T = MOBA_BLOCK
    blk = lambda b, h, j: (b, h, j, 0)
    return pl.pallas_call(
        functools.partial(_moba_sel_kernel, n_blocks=nb),
        out_shape=(jax.ShapeDtypeStruct((B, H, S, MOBA_TOPK), jnp.int32),
                   jax.ShapeDtypeStruct((B, H, S, PART_W), F32)),
        grid=(B, H, nb),
        in_specs=[pl.BlockSpec((None, None, T, 128), lambda b, h, j: (b, h // 2, j, 0)),
                  pl.BlockSpec((None, None, nb, 128), lambda b, h, j: (b, h, 0, 0)),
                  pl.BlockSpec((None, None, T, 128), blk),
                  pl.BlockSpec((None, None, T, hd), blk)],
        out_specs=(pl.BlockSpec((None, None, T, MOBA_TOPK), blk),
                   pl.BlockSpec((None, None, T, PART_W), blk)),
        compiler_params=pltpu.CompilerParams(
            dimension_semantics=("arbitrary", "arbitrary", "arbitrary"), vmem_limit_bytes=VMEM_LIMIT),
        name="moba_sel",
    )(aq, kmean, ak, av)


def _moba_blk_kernel(tg_ref, nu_ref, q_ref, k_ref, v_ref, o_ref):
    i = pl.program_id(0)

    @pl.when(i < nu_ref[0])
    def _():
        o_ref[...] = _attend(q_ref[...], k_ref[...], v_ref[...])

    @pl.when(i >= nu_ref[0])
    def _():
        o_ref[...] = _null_partial(MOBA_ROWS)


def _moba_blk(qs, tile_group, n_used, ak, av):
    B, H, S, hd = av.shape
    nb = S // MOBA_BLOCK
    R = MOBA_ROWS
    n_tiles = qs.shape[0] // R
    kv = lambda i, tg, nu: (tg[i] // nb, tg[i] % nb, 0, 0)
    grid_spec = pltpu.PrefetchScalarGridSpec(
        num_scalar_prefetch=2,
        grid=(n_tiles,),
        in_specs=[pl.BlockSpec((R, 128), lambda i, tg, nu: (i, 0)),
                  pl.BlockSpec((None, None, MOBA_BLOCK, 128), kv),
                  pl.BlockSpec((None, None, MOBA_BLOCK, hd), kv)],
        out_specs=pl.BlockSpec((R, PART_W), lambda i, tg, nu: (i, 0)),
    )
    return pl.pallas_call(
        _moba_blk_kernel,
        out_shape=jax.ShapeDtypeStruct((n_tiles * R, PART_W), F32),
        grid_spec=grid_spec,
        compiler_params=pltpu.CompilerParams(
            dimension_semantics=("arbitrary",), vmem_limit_bytes=VMEM_LIMIT),
        name="moba_blk",
    )(tile_group, n_used, qs, ak.reshape(B * H, nb, MOBA_BLOCK, 128), av.reshape(B * H, nb, MOBA_BLOCK, hd))


def _moba_merge_kernel(own_ref, pg_ref, g_ref, o_ref):
    hd = ATTN_HEAD_DIM
    rows = [own_ref[...]] + [pg_ref[s] for s in range(MOBA_TOPK)]
    lses = [pltpu.roll(r, hd, 1) for r in rows]
    top = lses[0]
    for z in lses[1:]:
        top = jnp.maximum(top, z)
    num = jnp.zeros_like(top)
    den = jnp.zeros_like(top)
    for r, z in zip(rows, lses):
        w = jnp.exp(z - top)
        num = num + w * r
        den = den + w
    o = (num / den)[:, :hd]
    ms = jnp.mean(o * o, axis=-1, keepdims=True)
    o_ref[...] = o * lax.rsqrt(ms + RMS_EPS) * g_ref[...]


def _moba_merge(own, pg, norm_g):
    n = own.shape[0]
    T = 512
    row = lambda i: (i, 0)
    return pl.pallas_call(
        _moba_merge_kernel,
        out_shape=jax.ShapeDtypeStruct((n, ATTN_HEAD_DIM), F32),
        grid=(n // T,),
        in_specs=[pl.BlockSpec((T, PART_W), row),
                  pl.BlockSpec((MOBA_TOPK, T, PART_W), lambda i: (0, i, 0)),
                  pl.BlockSpec((1, ATTN_HEAD_DIM), lambda i: (0, 0))],
        out_specs=pl.BlockSpec((T, ATTN_HEAD_DIM), row),
        compiler_params=pltpu.CompilerParams(
            dimension_semantics=("arbitrary",), vmem_limit_bytes=VMEM_LIMIT),
        name="moba_merge",
    )(own, pg, norm_g)


def _moba(aq, km, ak, av, norm_g):
    B, H, S, hd = av.shape
    nb = S // MOBA_BLOCK
    n_q = B * H * S
    kmp = km.reshape(B, nb, H // 2, 128)
    half = jnp.arange(128, dtype=jnp.int32) // hd
    kmean = jnp.stack([jnp.where(half == h % 2, kmp[:, :, h // 2, :], 0.0) for h in range(H)], axis=1)
    sel, own = _moba_sel(aq, kmean, ak, av)
    bh = jnp.arange(B * H, dtype=jnp.int32)[:, None, None]
    n_groups = B * H * nb
    keys = jnp.where(sel.reshape(B * H, S, MOBA_TOPK) >= 0, bh * nb + sel.reshape(B * H, S, MOBA_TOPK), n_groups)
    n_tiles = (n_q * MOBA_TOPK) // MOBA_ROWS + n_groups
    row_src, tile_group, pos, n_used = _group_layout(keys.reshape(-1), n_groups, MOBA_ROWS, n_tiles)
    q_id = (row_src // MOBA_TOPK).reshape(-1)
    pair_row = (q_id // (H * S) * (H // 2) + (q_id // S) % H // 2) * S + q_id % S
    qs = _sc_gather(aq.reshape(B * (H // 2) * S, 128), pair_row, 256)
    parts = _moba_blk(qs, tile_group, n_used, ak, av)
    pg = _sc_gather(parts, pos.reshape(n_q, MOBA_TOPK).T.reshape(-1), 256).reshape(MOBA_TOPK, n_q, PART_W)
    o = _moba_merge(own.reshape(n_q, PART_W), pg, norm_g)
    return o.reshape(B, H, S, hd)


def _mix_kernel(oh_ref, oa_ref, x_ref, g1_ref, sc2_ref, sh2_ref, n2_ref, wo_ref, wr_ref, br_ref,
                x1_ref, h2_ref, idx_ref, gw_ref):
    cat = jnp.concatenate([oh_ref[...]] + [oa_ref[hd] for hd in range(ATTN_HEADS)], axis=1)
    mix = _dot(cat.astype(BF16), wo_ref[...])
    x1 = x_ref[...] + g1_ref[...] * mix
    x1_ref[...] = x1
    ms = jnp.mean(x1 * x1, axis=-1, keepdims=True)
    h2 = x1 * lax.rsqrt(ms + RMS_EPS) * n2_ref[...]
    h2 = h2 * (1.0 + sc2_ref[...]) + sh2_ref[...]
    h2_ref[...] = h2
    logits = _dot(h2, wr_ref[...], precision=HIGHEST) + br_ref[...]
    lane = lax.broadcasted_iota(jnp.int32, logits.shape, 1)
    neg_inf = jnp.float32(-jnp.inf)
    vals, idxs = [], []
    for _ in range(TOP_K):
        m = jnp.max(logits, axis=1, keepdims=True)
        first = jnp.min(jnp.where(logits == m, lane, N_EXPERTS), axis=1, keepdims=True)
        vals.append(m)
        idxs.append(first)
        logits = jnp.where(lane == first, neg_inf, logits)
    e = [jnp.exp(v - vals[0]) for v in vals]
    denom = e[0] + e[1] + e[2] + e[3]
    idx_ref[...] = jnp.concatenate(idxs, axis=1)
    gw_ref[...] = jnp.concatenate([ei / denom for ei in e], axis=1)


def _mix(oh, oa, x, gate1, scale2, shift2, norm2_g, w_out_bf16, w_router, b_router):
    B, S, D = x.shape
    hw = oh.shape[-1]
    tm = 256
    row = lambda b, i: (b, i, 0)
    vec = lambda b, i: (b, 0, 0)
    const = lambda b, i: (0, 0)
    return pl.pallas_call(
        _mix_kernel,
        out_shape=(jax.ShapeDtypeStruct((B, S, D), F32),
                   jax.ShapeDtypeStruct((B, S, D), F32),
                   jax.ShapeDtypeStruct((B, S, TOP_K), jnp.int32),
                   jax.ShapeDtypeStruct((B, S, TOP_K), F32)),
        grid=(B, S // tm),
        in_specs=[pl.BlockSpec((None, tm, hw), row),
                  pl.BlockSpec((None, ATTN_HEADS, tm, ATTN_HEAD_DIM), lambda b, i: (b, 0, i, 0)),
                  pl.BlockSpec((None, tm, D), row),
                  pl.BlockSpec((None, 1, D), vec),
                  pl.BlockSpec((None, 1, D), vec),
                  pl.BlockSpec((None, 1, D), vec),
                  pl.BlockSpec((1, D), const),
                  pl.BlockSpec((D, D), const),
                  pl.BlockSpec((D, N_EXPERTS), const),
                  pl.BlockSpec((1, N_EXPERTS), const)],
        out_specs=(pl.BlockSpec((None, tm, D), row),
                   pl.BlockSpec((None, tm, D), row),
                   pl.BlockSpec((None, tm, TOP_K), row),
                   pl.BlockSpec((None, tm, TOP_K), row)),
        compiler_params=pltpu.CompilerParams(
            dimension_semantics=("arbitrary", "arbitrary"), vmem_limit_bytes=VMEM_LIMIT),
        name="mix",
    )(oh, oa, x, gate1, scale2, shift2, norm2_g, w_out_bf16, w_router, b_router)


def _moe_rows_kernel(be_ref, nu_ref, x_ref, wgu_ref, bgu_ref, wd_ref, bd_ref, y_ref, *, d_ff):
    i = pl.program_id(0)

    @pl.when(i < nu_ref[0])
    def _():
        gu = _dot(x_ref[...].astype(BF16), wgu_ref[...]) + bgu_ref[...]
        gate = jnp.minimum(gu[:, :d_ff], SWIGLU_LIMIT)
        up = jnp.clip(gu[:, d_ff:], -SWIGLU_LIMIT, SWIGLU_LIMIT)
        act = (up + 1.0) * gate * _sigmoid(SWIGLU_ALPHA * gate)
        y_ref[...] = _dot(act.astype(BF16), wd_ref[...]) + bd_ref[...]

    @pl.when(i >= nu_ref[0])
    def _():
        y_ref[...] = jnp.zeros_like(y_ref)


def _moe_rows(xs, blk_expert, n_used, wgu, bgu, wd, bd):
    D = xs.shape[1]
    bm = MOE_ROWS
    n_blk = xs.shape[0] // bm
    d_ff = wd.shape[1]
    wsel = lambda i, be, nu: (be[i], 0, 0)
    grid_spec = pltpu.PrefetchScalarGridSpec(
        num_scalar_prefetch=2,
        grid=(n_blk,),
        in_specs=[pl.BlockSpec((bm, D), lambda i, be, nu: (i, 0)),
                  pl.BlockSpec((None, D, 2 * d_ff), wsel),
                  pl.BlockSpec((None, 1, 2 * d_ff), wsel),
                  pl.BlockSpec((None, d_ff, D), wsel),
                  pl.BlockSpec((None, 1, D), wsel)],
        out_specs=pl.BlockSpec((bm, D), lambda i, be, nu: (i, 0)),
    )
    return pl.pallas_call(
        functools.partial(_moe_rows_kernel, d_ff=d_ff),
        out_shape=jax.ShapeDtypeStruct((n_blk * bm, D), F32),
        grid_spec=grid_spec,
        compiler_params=pltpu.CompilerParams(
            dimension_semantics=("arbitrary",), vmem_limit_bytes=VMEM_LIMIT),
        name="moe_rows",
    )(blk_expert, n_used, xs, wgu, bgu.reshape(N_EXPERTS, 1, 2 * d_ff), wd, bd.reshape(N_EXPERTS, 1, D))


def _combine_rows_kernel(yg_ref, gw_ref, x1_ref, g2_ref, fg_ref, o_ref):
    gw = gw_ref[...]
    y = gw[:, 0:1] * yg_ref[0]
    for kk in range(1, TOP_K):
        y = y + gw[:, kk:kk + 1] * yg_ref[kk]
    x2 = x1_ref[...] + g2_ref[...] * y
    ms = jnp.mean(x2 * x2, axis=-1, keepdims=True)
    o_ref[...] = x2 * lax.rsqrt(ms + RMS_EPS) * fg_ref[...]


def _combine_rows(yg, gates, x1, gate2, final_g):
    B, S, D = x1.shape
    T = B * S
    tm = 256
    steps_per_batch = S // tm
    return pl.pallas_call(
        _combine_rows_kernel,
        out_shape=jax.ShapeDtypeStruct((T, D), F32),
        grid=(T // tm,),
        in_specs=[pl.BlockSpec((TOP_K, tm, D), lambda i: (0, i, 0)),
                  pl.BlockSpec((tm, TOP_K), lambda i: (i, 0)),
                  pl.BlockSpec((tm, D), lambda i: (i, 0)),
                  pl.BlockSpec((None, 1, D), lambda i: (i // steps_per_batch, 0, 0)),
                  pl.BlockSpec((1, D), lambda i: (0, 0))],
        out_specs=pl.BlockSpec((tm, D), lambda i: (i, 0)),
        compiler_params=pltpu.CompilerParams(
            dimension_semantics=("arbitrary",), vmem_limit_bytes=VMEM_LIMIT),
        name="combine_rows",
    )(yg, gates, x1.reshape(T, D), gate2, final_g).reshape(B, S, D)


def _rotary_tables(positions):
    inv_freq = jnp.exp(-math.log(ROPE_THETA) * jnp.arange(0, ROT_DIM, 2, dtype=F32) / ROT_DIM)
    ang = positions.astype(F32)[:, :, None] * inv_freq
    cos, sin = jnp.cos(ang), jnp.sin(ang)
    B, S = positions.shape
    pad = ATTN_HEAD_DIM - ROT_DIM
    ct = jnp.concatenate([cos, cos, jnp.ones((B, S, pad), F32)], axis=-1)
    st = jnp.concatenate([-sin, sin, jnp.zeros((B, S, pad), F32)], axis=-1)
    return jnp.concatenate([ct, ct], axis=-1), jnp.concatenate([st, st], axis=-1)


def kernel(x, c, positions, w_ada, b_ada, norm1_g, w_in, hgrn_lb_logits, hgrn_norm_g, attn_norm_g,
           w_out, norm2_g, w_router, b_router, w_gate_up, b_gate_up, w_down, b_down, final_norm_g):
    B, S, D = x.shape
    T = B * S
    assert w_in.shape[0] == 1, "single-layer block: the final norm is fused into the combine step"
    l = 0
    ctab, stab = _rotary_tables(positions)
    lower_bounds = jnp.cumsum(jax.nn.softmax(hgrn_lb_logits.astype(F32), axis=0), axis=0)
    mod = _ada(c, w_ada[l], b_ada[l])
    shift1, scale1, gate1, shift2, scale2, gate2 = jnp.split(mod[:, None, :], N_MOD, axis=-1)
    q, k, lf, v, gt, aq, ak, av, km = _proj(
        x, scale1, shift1, norm1_g[l][None], w_in[l].astype(BF16), lower_bounds[l][None], ctab, stab)
    o_a = _moba(aq, km, ak, av, attn_norm_g[l][None])
    o_h = _hgrn(q, k, lf, v, gt, hgrn_norm_g[l][None])
    x1, h2, top_idx, gates = _mix(o_h, o_a, x, gate1, scale2, shift2, norm2_g[l][None],
                                  w_out[l].astype(BF16), w_router[l], b_router[l][None])
    n_blk = (T * TOP_K) // MOE_ROWS + N_EXPERTS
    row_src, blk_expert, pos, n_used = _group_layout(top_idx.reshape(-1), N_EXPERTS, MOE_ROWS, n_blk)
    xs = _sc_gather(h2.reshape(T, D), (row_src // TOP_K).reshape(-1), 64)
    y_sorted = _moe_rows(xs, blk_expert, n_used,
                         w_gate_up[l].astype(BF16), b_gate_up[l], w_down[l].astype(BF16), b_down[l])
    yg = _sc_gather(y_sorted, pos.reshape(T, TOP_K).T.reshape(-1), 64).reshape(TOP_K, T, D)
    return _combine_rows(yg, gates.reshape(T, TOP_K), x1, gate2, final_norm_g[None])
```

```python
import functools
import math

import jax
import jax.numpy as jnp
from jax import lax
from jax.experimental import pallas as pl
from jax.experimental.pallas import tpu as pltpu
from jax.experimental.pallas import tpu_sc as plsc

F32 = jnp.float32
BF16 = jnp.bfloat16
HIGHEST = lax.Precision.HIGHEST

HGRN_DK = 128
HGRN_CHUNK = 64
ATTN_HEADS = 4
ATTN_HEAD_DIM = 64
ROT_DIM = ATTN_HEAD_DIM // 4
ROPE_THETA = 500000.0
MOBA_BLOCK = 256
MOBA_TOPK = 3
N_EXPERTS = 32
TOP_K = 4
SWIGLU_ALPHA = 1.702
SWIGLU_LIMIT = 7.0
N_MOD = 6
RMS_EPS = 1e-6

EXP_CLAMP = 80.0
MOE_ROWS = 256
MOBA_ROWS = 256
PART_W = 128
VMEM_LIMIT = 56 * 1024 * 1024
SC_CORES = 2
SC_SUBCORES = 16


def _sigmoid(x):
    return 1.0 / (1.0 + jnp.exp(-x))


def _dot(a, b, **kw):
    return jnp.dot(a, b, preferred_element_type=F32, **kw)


def _dot_nt(a, b, **kw):
    return lax.dot_general(a, b, (((1,), (1,)), ((), ())), preferred_element_type=F32, **kw)


def _ada_kernel(c_ref, w_ref, b_ref, o_ref):
    c = c_ref[...]
    o_ref[...] = _dot(c * _sigmoid(c), w_ref[...], precision=HIGHEST) + b_ref[...]


def _ada(c, w_ada, b_ada):
    B, D = c.shape
    N = w_ada.shape[1]
    tn = N // 4
    c8 = jnp.zeros((8, D), F32).at[:B].set(c)
    out = pl.pallas_call(
        _ada_kernel,
        out_shape=jax.ShapeDtypeStruct((8, N), F32),
        grid=(N // tn,),
        in_specs=[pl.BlockSpec((8, D), lambda j: (0, 0)),
                  pl.BlockSpec((D, tn), lambda j: (0, j)),
                  pl.BlockSpec((1, tn), lambda j: (0, j))],
        out_specs=pl.BlockSpec((8, tn), lambda j: (0, j)),
        compiler_params=pltpu.CompilerParams(vmem_limit_bytes=VMEM_LIMIT),
        name="ada",
    )(c8, w_ada, b_ada.reshape(1, N))
    return out[:B]


def _proj_kernel(x_ref, sc_ref, sh_ref, g_ref, w_ref, lb_ref, ct_ref, st_ref,
                 q_ref, k_ref, lf_ref, v_ref, gt_ref, aq_ref, ak_ref, av_ref, km_ref,
                 *, hw, aw):
    x = x_ref[...]
    ms = jnp.mean(x * x, axis=-1, keepdims=True)
    h = x * lax.rsqrt(ms + RMS_EPS) * g_ref[...]
    h = h * (1.0 + sc_ref[...]) + sh_ref[...]
    proj = _dot(h.astype(BF16), w_ref[...])

    hq = proj[:, 0:hw]
    hf = proj[:, hw:2 * hw]
    hg = proj[:, 3 * hw:4 * hw]
    q_ref[...] = hq * _sigmoid(hq) * (HGRN_DK ** -0.5)
    lb = lb_ref[...]
    f = lb + (1.0 - lb) * _sigmoid(hf)
    k_ref[...] = 1.0 - f
    lf_ref[...] = jnp.log(f)
    v_ref[...] = proj[:, 2 * hw:3 * hw]
    gt_ref[...] = hg * _sigmoid(hg)

    ct = jnp.concatenate([ct_ref[...]] * (aw // 128), axis=1)
    st = jnp.concatenate([st_ref[...]] * (aw // 128), axis=1)
    lane = lax.broadcasted_iota(jnp.int32, ct.shape, 1) % ATTN_HEAD_DIM
    first_half = lane < (ROT_DIM // 2)

    def rot(t):
        partner = jnp.where(first_half, pltpu.roll(t, aw - ROT_DIM // 2, 1), pltpu.roll(t, ROT_DIM // 2, 1))
        return t * ct + partner * st

    base = 4 * hw
    aq = rot(proj[:, base:base + aw])
    ak = rot(proj[:, base + aw:base + 2 * aw])
    av = proj[:, base + 2 * aw:base + 3 * aw]
    km_ref[...] = jnp.mean(ak, axis=0, keepdims=True)
    lane128 = lax.broadcasted_iota(jnp.int32, (x.shape[0], 128), 1)
    for pair in range(ATTN_HEADS // 2):
        aq_ref[pair] = aq[:, pair * 128:(pair + 1) * 128]
    for hd in range(ATTN_HEADS):
        pair, half = divmod(hd, 2)
        in_head = (lane128 // ATTN_HEAD_DIM) == half
        ak_ref[hd] = jnp.where(in_head, ak[:, pair * 128:(pair + 1) * 128], 0.0).astype(BF16)
        av_ref[hd] = av[:, hd * ATTN_HEAD_DIM:(hd + 1) * ATTN_HEAD_DIM].astype(BF16)


def _proj(x, scale1, shift1, norm_g, w_in_bf16, lb, ctab, stab):
    B, S, D = x.shape
    hw = lb.shape[-1]
    aw = ATTN_HEADS * ATTN_HEAD_DIM
    tm = MOBA_BLOCK
    nb = S // MOBA_BLOCK
    n_proj = w_in_bf16.shape[1]
    row = lambda b, i: (b, i, 0)
    vec = lambda b, i: (b, 0, 0)
    head = lambda b, i: (b, 0, i, 0)
    out_shapes = (
        jax.ShapeDtypeStruct((B, S, hw), F32),
        jax.ShapeDtypeStruct((B, S, hw), F32),
        jax.ShapeDtypeStruct((B, S, hw), F32),
        jax.ShapeDtypeStruct((B, S, hw), F32),
        jax.ShapeDtypeStruct((B, S, hw), F32),
        jax.ShapeDtypeStruct((B, ATTN_HEADS // 2, S, 128), F32),
        jax.ShapeDtypeStruct((B, ATTN_HEADS, S, 128), BF16),
        jax.ShapeDtypeStruct((B, ATTN_HEADS, S, ATTN_HEAD_DIM), BF16),
        jax.ShapeDtypeStruct((B, nb, 1, aw), F32),
    )
    hspec = pl.BlockSpec((None, tm, hw), row)
    aspec = pl.BlockSpec((None, ATTN_HEADS, tm, ATTN_HEAD_DIM), head)
    return pl.pallas_call(
        functools.partial(_proj_kernel, hw=hw, aw=aw),
        out_shape=out_shapes,
        grid=(B, S // tm),
        in_specs=[pl.BlockSpec((None, tm, D), row),
                  pl.BlockSpec((None, 1, D), vec),
                  pl.BlockSpec((None, 1, D), vec),
                  pl.BlockSpec((1, D), lambda b, i: (0, 0)),
                  pl.BlockSpec((D, n_proj), lambda b, i: (0, 0)),
                  pl.BlockSpec((1, hw), lambda b, i: (0, 0)),
                  pl.BlockSpec((None, tm, 128), row),
                  pl.BlockSpec((None, tm, 128), row)],
        out_specs=(hspec, hspec, hspec, hspec, hspec,
                   pl.BlockSpec((None, ATTN_HEADS // 2, tm, 128), head),
                   pl.BlockSpec((None, ATTN_HEADS, tm, 128), head), aspec,
                   pl.BlockSpec((None, None, 1, aw), lambda b, i: (b, i, 0, 0))),
        compiler_params=pltpu.CompilerParams(
            dimension_semantics=("arbitrary", "arbitrary"), vmem_limit_bytes=VMEM_LIMIT),
        name="proj",
    )(x, scale1, shift1, norm_g, w_in_bf16, lb, ctab, stab)


def _hgrn_kernel(q_ref, k_ref, lf_ref, v_ref, gt_ref, gn_ref, o_ref, st_ref, *, n_heads, n_chunks):
    @pl.when(pl.program_id(1) == 0)
    def _():
        st_ref[...] = jnp.zeros_like(st_ref)

    C = HGRN_CHUNK
    r = lax.broadcasted_iota(jnp.int32, (C, C), 0)
    c = lax.broadcasted_iota(jnp.int32, (C, C), 1)
    tril = c <= r
    ltri = tril.astype(F32)
    gn = gn_ref[...]

    def chunk(ci, carry):
        r0 = pl.multiple_of(ci * C, C)
        rows = pl.ds(r0, C)
        b_all = _dot(ltri, lf_ref[rows, :], precision=HIGHEST)
        for hd in range(n_heads):
            sl = slice(hd * HGRN_DK, (hd + 1) * HGRN_DK)
            b = b_all[:, sl]
            b_last = b[C - 1:C, :]
            q = q_ref[rows, sl]
            k = k_ref[rows, sl]
            v = v_ref[rows, sl]
            state_t = st_ref[hd]
            o_inter = _dot_nt((q * jnp.exp(b)).astype(BF16), state_t.astype(BF16))
            rho = 0.5 * b_last
            qa = q * jnp.exp(jnp.minimum(b - rho, EXP_CLAMP))
            kb = k * jnp.exp(jnp.minimum(rho - b, EXP_CLAMP))
            scores = jnp.where(tril, _dot_nt(qa.astype(BF16), kb.astype(BF16)), 0.0)
            v16 = v.astype(BF16)
            o = o_inter + _dot(scores.astype(BF16), v16)
            kd = k * jnp.exp(b_last - b)
            st_ref[hd] = state_t * jnp.exp(b_last) + _dot(v.T.astype(BF16), kd.astype(BF16))
            ms = jnp.mean(o * o, axis=-1, keepdims=True)
            o_ref[rows, sl] = o * lax.rsqrt(ms + RMS_EPS) * gn * gt_ref[rows, sl]
        return carry

    lax.fori_loop(0, n_chunks, chunk, 0)


def _hgrn(q, k, lf, v, gt, norm_g):
    B, S, hw = q.shape
    n_heads = hw // HGRN_DK
    tc = 512
    spec = pl.BlockSpec((None, tc, hw), lambda b, i: (b, i, 0))
    return pl.pallas_call(
        functools.partial(_hgrn_kernel, n_heads=n_heads, n_chunks=tc // HGRN_CHUNK),
        out_shape=jax.ShapeDtypeStruct((B, S, hw), F32),
        grid=(B, S // tc),
        in_specs=[spec, spec, spec, spec, spec, pl.BlockSpec((1, HGRN_DK), lambda b, i: (0, 0))],
        out_specs=spec,
        scratch_shapes=[pltpu.VMEM((n_heads, HGRN_DK, HGRN_DK), F32)],
        compiler_params=pltpu.CompilerParams(
            dimension_semantics=("arbitrary", "arbitrary"), vmem_limit_bytes=VMEM_LIMIT),
        name="hgrn",
    )(q, k, lf, v, gt, norm_g)


def _sc_gather(table, idx, chunk):
    M = idx.shape[0]
    D = table.shape[1]
    n_workers = SC_CORES * SC_SUBCORES
    per_worker = M // n_workers
    assert per_worker * n_workers == M and per_worker % chunk == 0 and chunk % 8 == 0
    mesh = plsc.VectorSubcoreMesh(core_axis_name="c", subcore_axis_name="s")

    @functools.partial(
        pl.kernel, mesh=mesh,
        out_type=jax.ShapeDtypeStruct((M, D), table.dtype),
        scratch_types=[pltpu.VMEM((chunk,), jnp.int32),
                       pltpu.VMEM((chunk, D), table.dtype),
                       pltpu.SemaphoreType.DMA],
    )
    def gather_kernel(table_hbm, idx_hbm, out_hbm, idx_v, rows_v, sem):
        wid = lax.axis_index("s") * SC_CORES + lax.axis_index("c")
        base = wid * per_worker

        @pl.loop(0, per_worker // chunk)
        def _(j):
            off = pl.multiple_of(base + j * chunk, 8)
            pltpu.sync_copy(idx_hbm.at[pl.ds(off, chunk)], idx_v)
            pltpu.async_copy(table_hbm.at[idx_v], rows_v, sem).wait()
            pltpu.sync_copy(rows_v, out_hbm.at[pl.ds(off, chunk)])

    return gather_kernel(table, idx)


def _sc_scatter(rows, dst, chunk):
    M, D = rows.shape
    n_workers = SC_CORES * SC_SUBCORES
    per_worker = M // n_workers
    assert per_worker * n_workers == M and per_worker % chunk == 0 and chunk % 8 == 0
    mesh = plsc.VectorSubcoreMesh(core_axis_name="c", subcore_axis_name="s")

    @functools.partial(
        pl.kernel, mesh=mesh,
        out_type=jax.ShapeDtypeStruct((M, D), rows.dtype),
        scratch_types=[pltpu.VMEM((chunk,), jnp.int32),
                       pltpu.VMEM((chunk, D), rows.dtype),
                       pltpu.SemaphoreType.DMA],
    )
    def scatter_kernel(rows_hbm, dst_hbm, out_hbm, idx_v, rows_v, sem):
        wid = lax.axis_index("s") * SC_CORES + lax.axis_index("c")
        base = wid * per_worker

        @pl.loop(0, per_worker // chunk)
        def _(j):
            off = pl.multiple_of(base + j * chunk, 8)
            pltpu.sync_copy(dst_hbm.at[pl.ds(off, chunk)], idx_v)
            pltpu.sync_copy(rows_hbm.at[pl.ds(off, chunk)], rows_v)
            pltpu.async_copy(rows_v, out_hbm.at[idx_v], sem).wait()

    return scatter_kernel(rows, dst)


def _group_layout(keys, n_groups, bm, n_tiles, fan):
    n = keys.shape[0]
    n_rows = n_tiles * bm
    n_fill = n_rows - n
    n_hi = n_groups // 32 + 1
    oh_hi = ((keys // 32)[:, None] == jnp.arange(n_hi, dtype=jnp.int32)[None, :]).astype(BF16)
    oh_lo = ((keys % 32)[:, None] == jnp.arange(32, dtype=jnp.int32)[None, :]).astype(BF16)
    counts = jnp.einsum('ah,al->hl', oh_hi, oh_lo, preferred_element_type=F32)
    counts = counts.reshape(-1)[:n_groups].astype(jnp.int32)
    padded = (counts + bm - 1) // bm * bm
    fill_end = jnp.cumsum(padded - counts)
    f_ids = jnp.arange(n_fill, dtype=jnp.int32)
    fill_key = jnp.sum((fill_end[None, :] <= f_ids[:, None]).astype(jnp.int32), axis=1)
    all_keys = jnp.concatenate([keys, fill_key])
    sorted_k, a = lax.sort((all_keys, jnp.arange(n_rows, dtype=jnp.int32)), num_keys=1, is_stable=True)
    tile_group = jnp.minimum(sorted_k[::bm], n_groups - 1)
    real = a < n
    n_items = n // fan
    item = jnp.where(real, a, ((a - n) * 61) % n) // fan
    dest = jnp.where(real, (a % fan) * n_items + a // fan, a)
    n_used = (jnp.sum(padded) // bm).astype(jnp.int32).reshape(1)
    return item, dest, tile_group, n_used


def _attend(q, k, v, mask=None):
    s = _dot_nt((q * (ATTN_HEAD_DIM ** -0.5)).astype(BF16), k)
    if mask is not None:
        s = jnp.where(mask, s, -jnp.inf)
    m = jnp.max(s, axis=1, keepdims=True)
    p = jnp.exp(s - m)
    l = jnp.sum(p, axis=1, keepdims=True)
    o = _dot(p.astype(BF16), v) / l
    lse = jnp.broadcast_to(m + jnp.log(l), (o.shape[0], PART_W - ATTN_HEAD_DIM))
    return jnp.concatenate([o, lse], axis=1)


def _null_partial(rows):
    lane = lax.broadcasted_iota(jnp.int32, (rows, PART_W), 1)
    return jnp.where(lane < ATTN_HEAD_DIM, 0.0, -jnp.inf).astype(F32)


def _moba_sel_kernel(q_ref, km_ref, k_ref, v_ref, sel_ref, own_ref, *, n_blocks):
    j = pl.program_id(2)
    T = MOBA_BLOCK
    q = q_ref[...]
    gate = _dot_nt(q, km_ref[...], precision=HIGHEST)
    lane = lax.broadcasted_iota(jnp.int32, gate.shape, 1)
    neg_inf = jnp.float32(-jnp.inf)
    gate = jnp.where(lane < j, gate, neg_inf)
    picks = []
    for _ in range(MOBA_TOPK):
        m = jnp.max(gate, axis=1, keepdims=True)
        first = jnp.min(jnp.where(gate == m, lane, n_blocks), axis=1, keepdims=True)
        picks.append(jnp.where(m > neg_inf, first, -1))
        gate = jnp.where(lane == first, neg_inf, gate)
    sel_ref[...] = jnp.concatenate(picks, axis=1)
    rr = lax.broadcasted_iota(jnp.int32, (T, T), 0)
    cc = lax.broadcasted_iota(jnp.int32, (T, T), 1)
    own_ref[...] = _attend(q, k_ref[...], v_ref[...], mask=cc <= rr)


def _moba_sel(aq, kmean, ak, av):
    B, H, S, hd = av.shape
    nb = S // MOBA_BLOCK
    T = MOBA_BLOCK
    blk = lambda b, h, j: (b, h, j, 0)
    return pl.pallas_call(
        functools.partial(_moba_sel_kernel, n_blocks=nb),
        out_shape=(jax.ShapeDtypeStruct((B, H, S, MOBA_TOPK), jnp.int32),
                   jax.ShapeDtypeStruct((B, H, S, PART_W), F32)),
        grid=(B, H, nb),
        in_specs=[pl.BlockSpec((None, None, T, 128), lambda b, h, j: (b, h // 2, j, 0)),
                  pl.BlockSpec((None, None, nb, 128), lambda b, h, j: (b, h, 0, 0)),
                  pl.BlockSpec((None, None, T, 128), blk),
                  pl.BlockSpec((None, None, T, hd), blk)],
        out_specs=(pl.BlockSpec((None, None, T, MOBA_TOPK), blk),
                   pl.BlockSpec((None, None, T, PART_W), blk)),
        compiler_params=pltpu.CompilerParams(
            dimension_semantics=("arbitrary", "arbitrary", "arbitrary"), vmem_limit_bytes=VMEM_LIMIT),
        name="moba_sel",
    )(aq, kmean, ak, av)


def _moba_blk_kernel(tg_ref, nu_ref, q_ref, k_ref, v_ref, o_ref):
    i = pl.program_id(0)

    @pl.when(i < nu_ref[0])
    def _():
        o_ref[...] = _attend(q_ref[...], k_ref[...], v_ref[...])

    @pl.when(i >= nu_ref[0])
    def _():
        o_ref[...] = _null_partial(MOBA_ROWS)


def _moba_blk(qs, tile_group, n_used, ak, av):
    B, H, S, hd = av.shape
    nb = S // MOBA_BLOCK
    R = MOBA_ROWS
    n_tiles = qs.shape[0] // R
    kv = lambda i, tg, nu: (tg[i] // nb, tg[i] % nb, 0, 0)
    grid_spec = pltpu.PrefetchScalarGridSpec(
        num_scalar_prefetch=2,
        grid=(n_tiles,),
        in_specs=[pl.BlockSpec((R, 128), lambda i, tg, nu: (i, 0)),
                  pl.BlockSpec((None, None, MOBA_BLOCK, 128), kv),
                  pl.BlockSpec((None, None, MOBA_BLOCK, hd), kv)],
        out_specs=pl.BlockSpec((R, PART_W), lambda i, tg, nu: (i, 0)),
    )
    return pl.pallas_call(
        _moba_blk_kernel,
        out_shape=jax.ShapeDtypeStruct((n_tiles * R, PART_W), F32),
        grid_spec=grid_spec,
        compiler_params=pltpu.CompilerParams(
            dimension_semantics=("arbitrary",), vmem_limit_bytes=VMEM_LIMIT),
        name="moba_blk",
    )(tile_group, n_used, qs, ak.reshape(B * H, nb, MOBA_BLOCK, 128), av.reshape(B * H, nb, MOBA_BLOCK, hd))


def _moba_merge_kernel(own_ref, pg_ref, g_ref, o_ref):
    hd = ATTN_HEAD_DIM
    rows = [own_ref[...]] + [pg_ref[s] for s in range(MOBA_TOPK)]
    lses = [pltpu.roll(r, hd, 1) for r in rows]
    top = lses[0]
    for z in lses[1:]:
        top = jnp.maximum(top, z)
    num = jnp.zeros_like(top)
    den = jnp.zeros_like(top)
    for r, z in zip(rows, lses):
        w = jnp.exp(z - top)
        num = num + w * r
        den = den + w
    o = (num / den)[:, :hd]
    ms = jnp.mean(o * o, axis=-1, keepdims=True)
    o_ref[...] = o * lax.rsqrt(ms + RMS_EPS) * g_ref[...]


def _moba_merge(own, pg, norm_g):
    n = own.shape[0]
    T = 512
    row = lambda i: (i, 0)
    return pl.pallas_call(
        _moba_merge_kernel,
        out_shape=jax.ShapeDtypeStruct((n, ATTN_HEAD_DIM), F32),
        grid=(n // T,),
        in_specs=[pl.BlockSpec((T, PART_W), row),
                  pl.BlockSpec((MOBA_TOPK, T, PART_W), lambda i: (0, i, 0)),
                  pl.BlockSpec((1, ATTN_HEAD_DIM), lambda i: (0, 0))],
        out_specs=pl.BlockSpec((T, ATTN_HEAD_DIM), row),
        compiler_params=pltpu.CompilerParams(
            dimension_semantics=("arbitrary",), vmem_limit_bytes=VMEM_LIMIT),
        name="moba_merge",
    )(own, pg, norm_g)


def _moba(aq, km, ak, av, norm_g):
    B, H, S, hd = av.shape
    nb = S // MOBA_BLOCK
    n_q = B * H * S
    kmp = km.reshape(B, nb, H // 2, 128)
    half = jnp.arange(128, dtype=jnp.int32) // hd
    kmean = jnp.stack([jnp.where(half == h % 2, kmp[:, :, h // 2, :], 0.0) for h in range(H)], axis=1)
    sel, own = _moba_sel(aq, kmean, ak, av)
    bh = jnp.arange(B * H, dtype=jnp.int32)[:, None, None]
    n_groups = B * H * nb
    keys = jnp.where(sel.reshape(B * H, S, MOBA_TOPK) >= 0, bh * nb + sel.reshape(B * H, S, MOBA_TOPK), n_groups)
    n_tiles = (n_q * MOBA_TOPK) // MOBA_ROWS + n_groups
    q_id, dest, tile_group, n_used = _group_layout(keys.reshape(-1), n_groups, MOBA_ROWS, n_tiles, MOBA_TOPK)
    pair_row = (q_id // (H * S) * (H // 2) + (q_id // S) % H // 2) * S + q_id % S
    qs = _sc_gather(aq.reshape(B * (H // 2) * S, 128), pair_row, 256)
    parts = _moba_blk(qs, tile_group, n_used, ak, av)
    pg = _sc_scatter(parts, dest, 256)
    assert pg.shape[0] % n_q == 0
    o = _moba_merge(own.reshape(n_q, PART_W), pg.reshape(pg.shape[0] // n_q, n_q, PART_W), norm_g)
    return o.reshape(B, H, S, hd)


def _mix_kernel(oh_ref, oa_ref, x_ref, g1_ref, sc2_ref, sh2_ref, n2_ref, wo_ref, wr_ref, br_ref,
                x1_ref, h2_ref, idx_ref, gw_ref):
    cat = jnp.concatenate([oh_ref[...]] + [oa_ref[hd] for hd in range(ATTN_HEADS)], axis=1)
    mix = _dot(cat.astype(BF16), wo_ref[...])
    x1 = x_ref[...] + g1_ref[...] * mix
    x1_ref[...] = x1
    ms = jnp.mean(x1 * x1, axis=-1, keepdims=True)
    h2 = x1 * lax.rsqrt(ms + RMS_EPS) * n2_ref[...]
    h2 = h2 * (1.0 + sc2_ref[...]) + sh2_ref[...]
    h2_ref[...] = h2
    logits = _dot(h2, wr_ref[...], precision=HIGHEST) + br_ref[...]
    lane = lax.broadcasted_iota(jnp.int32, logits.shape, 1)
    neg_inf = jnp.float32(-jnp.inf)
    vals, idxs = [], []
    for _ in range(TOP_K):
        m = jnp.max(logits, axis=1, keepdims=True)
        first = jnp.min(jnp.where(logits == m, lane, N_EXPERTS), axis=1, keepdims=True)
        vals.append(m)
        idxs.append(first)
        logits = jnp.where(lane == first, neg_inf, logits)
    e = [jnp.exp(v - vals[0]) for v in vals]
    denom = e[0] + e[1] + e[2] + e[3]
    idx_ref[...] = jnp.concatenate(idxs, axis=1)
    gw_ref[...] = jnp.concatenate([ei / denom for ei in e], axis=1)


def _mix(oh, oa, x, gate1, scale2, shift2, norm2_g, w_out_bf16, w_router, b_router):
    B, S, D = x.shape
    hw = oh.shape[-1]
    tm = 256
    row = lambda b, i: (b, i, 0)
    vec = lambda b, i: (b, 0, 0)
    const = lambda b, i: (0, 0)
    return pl.pallas_call(
        _mix_kernel,
        out_shape=(jax.ShapeDtypeStruct((B, S, D), F32),
                   jax.ShapeDtypeStruct((B, S, D), F32),
                   jax.ShapeDtypeStruct((B, S, TOP_K), jnp.int32),
                   jax.ShapeDtypeStruct((B, S, TOP_K), F32)),
        grid=(B, S // tm),
        in_specs=[pl.BlockSpec((None, tm, hw), row),
                  pl.BlockSpec((None, ATTN_HEADS, tm, ATTN_HEAD_DIM), lambda b, i: (b, 0, i, 0)),
                  pl.BlockSpec((None, tm, D), row),
                  pl.BlockSpec((None, 1, D), vec),
                  pl.BlockSpec((None, 1, D), vec),
                  pl.BlockSpec((None, 1, D), vec),
                  pl.BlockSpec((1, D), const),
                  pl.BlockSpec((D, D), const),
                  pl.BlockSpec((D, N_EXPERTS), const),
                  pl.BlockSpec((1, N_EXPERTS), const)],
        out_specs=(pl.BlockSpec((None, tm, D), row),
                   pl.BlockSpec((None, tm, D), row),
                   pl.BlockSpec((None, tm, TOP_K), row),
                   pl.BlockSpec((None, tm, TOP_K), row)),
        compiler_params=pltpu.CompilerParams(
            dimension_semantics=("arbitrary", "arbitrary"), vmem_limit_bytes=VMEM_LIMIT),
        name="mix",
    )(oh, oa, x, gate1, scale2, shift2, norm2_g, w_out_bf16, w_router, b_router)


def _moe_rows_kernel(be_ref, nu_ref, x_ref, wgu_ref, bgu_ref, wd_ref, bd_ref, y_ref, wgu16, wd16, *, d_ff):
    i = pl.program_id(0)

    @pl.when((i == 0) | (be_ref[i] != be_ref[jnp.maximum(i - 1, 0)]))
    def _():
        wgu16[...] = wgu_ref[...].astype(BF16)
        wd16[...] = wd_ref[...].astype(BF16)

    @pl.when(i < nu_ref[0])
    def _():
        gu = _dot(x_ref[...].astype(BF16), wgu16[...]) + bgu_ref[...]
        gate = jnp.minimum(gu[:, :d_ff], SWIGLU_LIMIT)
        up = jnp.clip(gu[:, d_ff:], -SWIGLU_LIMIT, SWIGLU_LIMIT)
        act = (up + 1.0) * gate * _sigmoid(SWIGLU_ALPHA * gate)
        y_ref[...] = _dot(act.astype(BF16), wd16[...]) + bd_ref[...]

    @pl.when(i >= nu_ref[0])
    def _():
        y_ref[...] = jnp.zeros_like(y_ref)


def _moe_rows(xs, blk_expert, n_used, wgu, bgu, wd, bd):
    D = xs.shape[1]
    bm = MOE_ROWS
    n_blk = xs.shape[0] // bm
    d_ff = wd.shape[1]
    wsel = lambda i, be, nu: (be[i], 0, 0)
    grid_spec = pltpu.PrefetchScalarGridSpec(
        num_scalar_prefetch=2,
        grid=(n_blk,),
        in_specs=[pl.BlockSpec((bm, D), lambda i, be, nu: (i, 0)),
                  pl.BlockSpec((None, D, 2 * d_ff), wsel),
                  pl.BlockSpec((None, 1, 2 * d_ff), wsel),
                  pl.BlockSpec((None, d_ff, D), wsel),
                  pl.BlockSpec((None, 1, D), wsel)],
        out_specs=pl.BlockSpec((bm, D), lambda i, be, nu: (i, 0)),
        scratch_shapes=[pltpu.VMEM((D, 2 * d_ff), BF16), pltpu.VMEM((d_ff, D), BF16)],
    )
    return pl.pallas_call(
        functools.partial(_moe_rows_kernel, d_ff=d_ff),
        out_shape=jax.ShapeDtypeStruct((n_blk * bm, D), F32),
        grid_spec=grid_spec,
        compiler_params=pltpu.CompilerParams(
            dimension_semantics=("arbitrary",), vmem_limit_bytes=VMEM_LIMIT),
        name="moe_rows",
    )(blk_expert, n_used, xs, wgu, bgu.reshape(N_EXPERTS, 1, 2 * d_ff), wd, bd.reshape(N_EXPERTS, 1, D))


def _combine_rows_kernel(*refs):
    y_refs = refs[:TOP_K]
    gw_ref, x1_ref, g2_ref, fg_ref, o_ref = refs[TOP_K:]
    gw = gw_ref[...]
    y = gw[:, 0:1] * y_refs[0][...]
    for kk in range(1, TOP_K):
        y = y + gw[:, kk:kk + 1] * y_refs[kk][...]
    x2 = x1_ref[...] + g2_ref[...] * y
    ms = jnp.mean(x2 * x2, axis=-1, keepdims=True)
    o_ref[...] = x2 * lax.rsqrt(ms + RMS_EPS) * fg_ref[...]


def _combine_rows(yg, gates, x1, gate2, final_g):
    B, S, D = x1.shape
    T = B * S
    tm = 256
    steps = T // tm
    steps_per_batch = S // tm
    slot_spec = lambda kk: pl.BlockSpec((tm, D), lambda i: (kk * steps + i, 0))
    return pl.pallas_call(
        _combine_rows_kernel,
        out_shape=jax.ShapeDtypeStruct((T, D), F32),
        grid=(steps,),
        in_specs=[slot_spec(kk) for kk in range(TOP_K)] + [
            pl.BlockSpec((tm, TOP_K), lambda i: (i, 0)),
            pl.BlockSpec((tm, D), lambda i: (i, 0)),
            pl.BlockSpec((None, 1, D), lambda i: (i // steps_per_batch, 0, 0)),
            pl.BlockSpec((1, D), lambda i: (0, 0))],
        out_specs=pl.BlockSpec((tm, D), lambda i: (i, 0)),
        compiler_params=pltpu.CompilerParams(
            dimension_semantics=("arbitrary",), vmem_limit_bytes=VMEM_LIMIT),
        name="combine_rows",
    )(*([yg] * TOP_K), gates, x1.reshape(T, D), gate2, final_g).reshape(B, S, D)


def _rotary_tables(positions):
    inv_freq = jnp.exp(-math.log(ROPE_THETA) * jnp.arange(0, ROT_DIM, 2, dtype=F32) / ROT_DIM)
    ang = positions.astype(F32)[:, :, None] * inv_freq
    cos, sin = jnp.cos(ang), jnp.sin(ang)
    B, S = positions.shape
    pad = ATTN_HEAD_DIM - ROT_DIM
    ct = jnp.concatenate([cos, cos, jnp.ones((B, S, pad), F32)], axis=-1)
    st = jnp.concatenate([-sin, sin, jnp.zeros((B, S, pad), F32)], axis=-1)
    return jnp.concatenate([ct, ct], axis=-1), jnp.concatenate([st, st], axis=-1)


def kernel(x, c, positions, w_ada, b_ada, norm1_g, w_in, hgrn_lb_logits, hgrn_norm_g, attn_norm_g,
           w_out, norm2_g, w_router, b_router, w_gate_up, b_gate_up, w_down, b_down, final_norm_g):
    B, S, D = x.shape
    T = B * S
    assert w_in.shape[0] == 1, "single-layer block: the final norm is fused into the combine step"
    l = 0
    ctab, stab = _rotary_tables(positions)
    lower_bounds = jnp.cumsum(jax.nn.softmax(hgrn_lb_logits.astype(F32), axis=0), axis=0)
    mod = _ada(c, w_ada[l], b_ada[l])
    shift1, scale1, gate1, shift2, scale2, gate2 = jnp.split(mod[:, None, :], N_MOD, axis=-1)
    q, k, lf, v, gt, aq, ak, av, km = _proj(
        x, scale1, shift1, norm1_g[l][None], w_in[l].astype(BF16), lower_bounds[l][None], ctab, stab)
    o_a = _moba(aq, km, ak, av, attn_norm_g[l][None])
    o_h = _hgrn(q, k, lf, v, gt, hgrn_norm_g[l][None])
    x1, h2, top_idx, gates = _mix(o_h, o_a, x, gate1, scale2, shift2, norm2_g[l][None],
                                  w_out[l].astype(BF16), w_router[l], b_router[l][None])
    n_blk = (T * TOP_K) // MOE_ROWS + N_EXPERTS
    tok, dest, blk_expert, n_used = _group_layout(top_idx.reshape(-1), N_EXPERTS, MOE_ROWS, n_blk, TOP_K)
    xs = _sc_gather(h2.reshape(T, D), tok, 64)
    y_sorted = _moe_rows(xs, blk_expert, n_used, w_gate_up[l], b_gate_up[l], w_down[l], b_down[l])
    yg = _sc_scatter(y_sorted, dest, 64)
    return _combine_rows(yg, gates.reshape(T, TOP_K), x1, gate2, final_norm_g[None])
```

```python
import functools
import math

import jax
import jax.numpy as jnp
from jax import lax
from jax.experimental import pallas as pl
from jax.experimental.pallas import tpu as pltpu
from jax.experimental.pallas import tpu_sc as plsc

F32 = jnp.float32
BF16 = jnp.bfloat16
HIGHEST = lax.Precision.HIGHEST

HGRN_DK = 128
HGRN_CHUNK = 64
ATTN_HEADS = 4
ATTN_HEAD_DIM = 64
ROT_DIM = ATTN_HEAD_DIM // 4
ROPE_THETA = 500000.0
MOBA_BLOCK = 256
MOBA_TOPK = 3
N_EXPERTS = 32
TOP_K = 4
SWIGLU_ALPHA = 1.702
SWIGLU_LIMIT = 7.0
N_MOD = 6
RMS_EPS = 1e-6

EXP_CLAMP = 80.0
MOE_ROWS = 256
MOBA_ROWS = 256
MOBA_TILES_PER_STEP = 4
PART_W = 128
VMEM_LIMIT = 56 * 1024 * 1024
SC_CORES = 2
SC_SUBCORES = 16


def _sigmoid(x):
    return 1.0 / (1.0 + jnp.exp(-x))


def _dot(a, b, **kw):
    return jnp.dot(a, b, preferred_element_type=F32, **kw)


def _dot_nt(a, b, **kw):
    return lax.dot_general(a, b, (((1,), (1,)), ((), ())), preferred_element_type=F32, **kw)


def _ada_kernel(c_ref, w_ref, b_ref, o_ref):
    c = c_ref[...]
    o_ref[...] = _dot(c * _sigmoid(c), w_ref[...], precision=HIGHEST) + b_ref[...]


def _ada(c, w_ada, b_ada):
    B, D = c.shape
    N = w_ada.shape[1]
    tn = N // 4
    c8 = jnp.zeros((8, D), F32).at[:B].set(c)
    out = pl.pallas_call(
        _ada_kernel,
        out_shape=jax.ShapeDtypeStruct((8, N), F32),
        grid=(N // tn,),
        in_specs=[pl.BlockSpec((8, D), lambda j: (0, 0)),
                  pl.BlockSpec((D, tn), lambda j: (0, j)),
                  pl.BlockSpec((1, tn), lambda j: (0, j))],
        out_specs=pl.BlockSpec((8, tn), lambda j: (0, j)),
        compiler_params=pltpu.CompilerParams(vmem_limit_bytes=VMEM_LIMIT),
        name="ada",
    )(c8, w_ada, b_ada.reshape(1, N))
    return out[:B]


def _proj_kernel(x_ref, sc_ref, sh_ref, g_ref, w_ref, lb_ref, ct_ref, st_ref,
                 q_ref, k_ref, lf_ref, v_ref, gt_ref, aq_ref, ak_ref, av_ref, km_ref,
                 *, hw, aw):
    x = x_ref[...]
    ms = jnp.mean(x * x, axis=-1, keepdims=True)
    h = x * lax.rsqrt(ms + RMS_EPS) * g_ref[...]
    h = h * (1.0 + sc_ref[...]) + sh_ref[...]
    proj = _dot(h.astype(BF16), w_ref[...])

    hq = proj[:, 0:hw]
    hf = proj[:, hw:2 * hw]
    hg = proj[:, 3 * hw:4 * hw]
    q_ref[...] = hq * _sigmoid(hq) * (HGRN_DK ** -0.5)
    lb = lb_ref[...]
    f = lb + (1.0 - lb) * _sigmoid(hf)
    k_ref[...] = 1.0 - f
    lf_ref[...] = jnp.log(f)
    v_ref[...] = proj[:, 2 * hw:3 * hw]
    gt_ref[...] = hg * _sigmoid(hg)

    ct = jnp.concatenate([ct_ref[...]] * (aw // 128), axis=1)
    st = jnp.concatenate([st_ref[...]] * (aw // 128), axis=1)
    lane = lax.broadcasted_iota(jnp.int32, ct.shape, 1) % ATTN_HEAD_DIM
    first_half = lane < (ROT_DIM // 2)

    def rot(t):
        partner = jnp.where(first_half, pltpu.roll(t, aw - ROT_DIM // 2, 1), pltpu.roll(t, ROT_DIM // 2, 1))
        return t * ct + partner * st

    base = 4 * hw
    aq = rot(proj[:, base:base + aw])
    ak = rot(proj[:, base + aw:base + 2 * aw])
    av = proj[:, base + 2 * aw:base + 3 * aw]
    km_ref[...] = jnp.mean(ak, axis=0, keepdims=True)
    lane128 = lax.broadcasted_iota(jnp.int32, (x.shape[0], 128), 1)
    for pair in range(ATTN_HEADS // 2):
        aq_ref[pair] = aq[:, pair * 128:(pair + 1) * 128]
    for hd in range(ATTN_HEADS):
        pair, half = divmod(hd, 2)
        in_head = (lane128 // ATTN_HEAD_DIM) == half
        ak_ref[hd] = jnp.where(in_head, ak[:, pair * 128:(pair + 1) * 128], 0.0).astype(BF16)
        av_ref[hd] = av[:, hd * ATTN_HEAD_DIM:(hd + 1) * ATTN_HEAD_DIM].astype(BF16)


def _proj(x, scale1, shift1, norm_g, w_in_bf16, lb, ctab, stab):
    B, S, D = x.shape
    hw = lb.shape[-1]
    aw = ATTN_HEADS * ATTN_HEAD_DIM
    tm = MOBA_BLOCK
    nb = S // MOBA_BLOCK
    n_proj = w_in_bf16.shape[1]
    row = lambda b, i: (b, i, 0)
    vec = lambda b, i: (b, 0, 0)
    head = lambda b, i: (b, 0, i, 0)
    out_shapes = (
        jax.ShapeDtypeStruct((B, S, hw), F32),
        jax.ShapeDtypeStruct((B, S, hw), F32),
        jax.ShapeDtypeStruct((B, S, hw), F32),
        jax.ShapeDtypeStruct((B, S, hw), F32),
        jax.ShapeDtypeStruct((B, S, hw), F32),
        jax.ShapeDtypeStruct((B, ATTN_HEADS // 2, S, 128), F32),
        jax.ShapeDtypeStruct((B, ATTN_HEADS, S, 128), BF16),
        jax.ShapeDtypeStruct((B, ATTN_HEADS, S, ATTN_HEAD_DIM), BF16),
        jax.ShapeDtypeStruct((B, nb, 1, aw), F32),
    )
    hspec = pl.BlockSpec((None, tm, hw), row)
    aspec = pl.BlockSpec((None, ATTN_HEADS, tm, ATTN_HEAD_DIM), head)
    return pl.pallas_call(
        functools.partial(_proj_kernel, hw=hw, aw=aw),
        out_shape=out_shapes,
        grid=(B, S // tm),
        in_specs=[pl.BlockSpec((None, tm, D), row),
                  pl.BlockSpec((None, 1, D), vec),
                  pl.BlockSpec((None, 1, D), vec),
                  pl.BlockSpec((1, D), lambda b, i: (0, 0)),
                  pl.BlockSpec((D, n_proj), lambda b, i: (0, 0)),
                  pl.BlockSpec((1, hw), lambda b, i: (0, 0)),
                  pl.BlockSpec((None, tm, 128), row),
                  pl.BlockSpec((None, tm, 128), row)],
        out_specs=(hspec, hspec, hspec, hspec, hspec,
                   pl.BlockSpec((None, ATTN_HEADS // 2, tm, 128), head),
                   pl.BlockSpec((None, ATTN_HEADS, tm, 128), head), aspec,
                   pl.BlockSpec((None, None, 1, aw), lambda b, i: (b, i, 0, 0))),
        compiler_params=pltpu.CompilerParams(
            dimension_semantics=("arbitrary", "arbitrary"), vmem_limit_bytes=VMEM_LIMIT),
        name="proj",
    )(x, scale1, shift1, norm_g, w_in_bf16, lb, ctab, stab)


def _hgrn_kernel(q_ref, k_ref, lf_ref, v_ref, gt_ref, gn_ref, o_ref, st_ref, *, n_heads, n_chunks):
    @pl.when(pl.program_id(1) == 0)
    def _():
        st_ref[...] = jnp.zeros_like(st_ref)

    C = HGRN_CHUNK
    r = lax.broadcasted_iota(jnp.int32, (C, C), 0)
    c = lax.broadcasted_iota(jnp.int32, (C, C), 1)
    tril = c <= r
    ltri = tril.astype(F32)
    gn = gn_ref[...]

    def chunk(ci, carry):
        r0 = pl.multiple_of(ci * C, C)
        rows = pl.ds(r0, C)
        b_all = _dot(ltri, lf_ref[rows, :], precision=HIGHEST)
        for hd in range(n_heads):
            sl = slice(hd * HGRN_DK, (hd + 1) * HGRN_DK)
            b = b_all[:, sl]
            b_last = b[C - 1:C, :]
            q = q_ref[rows, sl]
            k = k_ref[rows, sl]
            v = v_ref[rows, sl]
            state_t = st_ref[hd]
            o_inter = _dot_nt((q * jnp.exp(b)).astype(BF16), state_t.astype(BF16))
            rho = 0.5 * b_last
            qa = q * jnp.exp(jnp.minimum(b - rho, EXP_CLAMP))
            kb = k * jnp.exp(jnp.minimum(rho - b, EXP_CLAMP))
            scores = jnp.where(tril, _dot_nt(qa.astype(BF16), kb.astype(BF16)), 0.0)
            v16 = v.astype(BF16)
            o = o_inter + _dot(scores.astype(BF16), v16)
            kd = k * jnp.exp(b_last - b)
            st_ref[hd] = state_t * jnp.exp(b_last) + _dot(v.T.astype(BF16), kd.astype(BF16))
            ms = jnp.mean(o * o, axis=-1, keepdims=True)
            o_ref[rows, sl] = o * lax.rsqrt(ms + RMS_EPS) * gn * gt_ref[rows, sl]
        return carry

    lax.fori_loop(0, n_chunks, chunk, 0)


def _hgrn(q, k, lf, v, gt, norm_g):
    B, S, hw = q.shape
    n_heads = hw // HGRN_DK
    tc = 512
    spec = pl.BlockSpec((None, tc, hw), lambda b, i: (b, i, 0))
    return pl.pallas_call(
        functools.partial(_hgrn_kernel, n_heads=n_heads, n_chunks=tc // HGRN_CHUNK),
        out_shape=jax.ShapeDtypeStruct((B, S, hw), F32),
        grid=(B, S // tc),
        in_specs=[spec, spec, spec, spec, spec, pl.BlockSpec((1, HGRN_DK), lambda b, i: (0, 0))],
        out_specs=spec,
        scratch_shapes=[pltpu.VMEM((n_heads, HGRN_DK, HGRN_DK), F32)],
        compiler_params=pltpu.CompilerParams(
            dimension_semantics=("arbitrary", "arbitrary"), vmem_limit_bytes=VMEM_LIMIT),
        name="hgrn",
    )(q, k, lf, v, gt, norm_g)


def _sc_gather(table, idx, chunk):
    M = idx.shape[0]
    D = table.shape[1]
    n_workers = SC_CORES * SC_SUBCORES
    per_worker = M // n_workers
    assert per_worker * n_workers == M and per_worker % chunk == 0 and chunk % 8 == 0
    mesh = plsc.VectorSubcoreMesh(core_axis_name="c", subcore_axis_name="s")

    @functools.partial(
        pl.kernel, mesh=mesh,
        out_type=jax.ShapeDtypeStruct((M, D), table.dtype),
        scratch_types=[pltpu.VMEM((chunk,), jnp.int32),
                       pltpu.VMEM((chunk, D), table.dtype),
                       pltpu.SemaphoreType.DMA],
    )
    def gather_kernel(table_hbm, idx_hbm, out_hbm, idx_v, rows_v, sem):
        wid = lax.axis_index("s") * SC_CORES + lax.axis_index("c")
        base = wid * per_worker

        @pl.loop(0, per_worker // chunk)
        def _(j):
            off = pl.multiple_of(base + j * chunk, 8)
            pltpu.sync_copy(idx_hbm.at[pl.ds(off, chunk)], idx_v)
            pltpu.async_copy(table_hbm.at[idx_v], rows_v, sem).wait()
            pltpu.sync_copy(rows_v, out_hbm.at[pl.ds(off, chunk)])

    return gather_kernel(table, idx)


def _sc_scatter(rows, dst, chunk):
    M, D = rows.shape
    n_workers = SC_CORES * SC_SUBCORES
    per_worker = M // n_workers
    assert per_worker * n_workers == M and per_worker % chunk == 0 and chunk % 8 == 0
    mesh = plsc.VectorSubcoreMesh(core_axis_name="c", subcore_axis_name="s")

    @functools.partial(
        pl.kernel, mesh=mesh,
        out_type=jax.ShapeDtypeStruct((M, D), rows.dtype),
        scratch_types=[pltpu.VMEM((chunk,), jnp.int32),
                       pltpu.VMEM((chunk, D), rows.dtype),
                       pltpu.SemaphoreType.DMA],
    )
    def scatter_kernel(rows_hbm, dst_hbm, out_hbm, idx_v, rows_v, sem):
        wid = lax.axis_index("s") * SC_CORES + lax.axis_index("c")
        base = wid * per_worker

        @pl.loop(0, per_worker // chunk)
        def _(j):
            off = pl.multiple_of(base + j * chunk, 8)
            pltpu.sync_copy(dst_hbm.at[pl.ds(off, chunk)], idx_v)
            pltpu.sync_copy(rows_hbm.at[pl.ds(off, chunk)], rows_v)
            pltpu.async_copy(rows_v, out_hbm.at[idx_v], sem).wait()

    return scatter_kernel(rows, dst)


def _group_layout(keys, n_groups, bm, n_tiles, fan):
    n = keys.shape[0]
    n_rows = n_tiles * bm
    n_fill = n_rows - n
    n_hi = n_groups // 32 + 1
    oh_hi = ((keys // 32)[:, None] == jnp.arange(n_hi, dtype=jnp.int32)[None, :]).astype(BF16)
    oh_lo = ((keys % 32)[:, None] == jnp.arange(32, dtype=jnp.int32)[None, :]).astype(BF16)
    counts = jnp.einsum('ah,al->hl', oh_hi, oh_lo, preferred_element_type=F32)
    counts = counts.reshape(-1)[:n_groups].astype(jnp.int32)
    padded = (counts + bm - 1) // bm * bm
    fill_end = jnp.cumsum(padded - counts)
    f_ids = jnp.arange(n_fill, dtype=jnp.int32)
    fill_key = jnp.sum((fill_end[None, :] <= f_ids[:, None]).astype(jnp.int32), axis=1)
    all_keys = jnp.concatenate([keys, fill_key])
    sorted_k, a = lax.sort((all_keys, jnp.arange(n_rows, dtype=jnp.int32)), num_keys=1, is_stable=True)
    tile_group = jnp.minimum(sorted_k[::bm], n_groups - 1)
    real = a < n
    n_items = n // fan
    item = jnp.where(real, a, ((a - n) * 61) % n) // fan
    dest = jnp.where(real, (a % fan) * n_items + a // fan, a)
    n_used = (jnp.sum(padded) // bm).astype(jnp.int32).reshape(1)
    return item, dest, tile_group, n_used


def _null_partial(rows):
    lane = lax.broadcasted_iota(jnp.int32, (rows, PART_W), 1)
    return jnp.where(lane < ATTN_HEAD_DIM, 0.0, -jnp.inf).astype(F32)


def _moba_sel_kernel(q_ref, km_ref, k_ref, v_ref, sel_ref, own_ref, *, n_blocks):
    j = pl.program_id(1)
    T = MOBA_BLOCK
    heads = range(ATTN_HEADS)
    qs = [q_ref[hd // 2] for hd in heads]
    gates = [_dot_nt(qs[hd], km_ref[hd], precision=HIGHEST) for hd in heads]
    lane = lax.broadcasted_iota(jnp.int32, gates[0].shape, 1)
    neg_inf = jnp.float32(-jnp.inf)
    gates = [jnp.where(lane < j, g, neg_inf) for g in gates]
    picks = [[] for _ in heads]
    for _ in range(MOBA_TOPK):
        ms = [jnp.max(g, axis=1, keepdims=True) for g in gates]
        firsts = [jnp.min(jnp.where(g == m, lane, n_blocks), axis=1, keepdims=True) for g, m in zip(gates, ms)]
        for hd in heads:
            picks[hd].append(jnp.where(ms[hd] > neg_inf, firsts[hd], -1))
        gates = [jnp.where(lane == f, neg_inf, g) for g, f in zip(gates, firsts)]
    for hd in heads:
        sel_ref[hd] = jnp.concatenate(picks[hd], axis=1)
    causal = lax.broadcasted_iota(jnp.int32, (T, T), 1) <= lax.broadcasted_iota(jnp.int32, (T, T), 0)
    scale = ATTN_HEAD_DIM ** -0.5
    ss = [jnp.where(causal, _dot_nt((qs[hd] * scale).astype(BF16), k_ref[hd]), neg_inf) for hd in heads]
    mx = [jnp.max(s, axis=1, keepdims=True) for s in ss]
    ps = [jnp.exp(s - m) for s, m in zip(ss, mx)]
    ls = [jnp.sum(p, axis=1, keepdims=True) for p in ps]
    accs = [_dot(ps[hd].astype(BF16), v_ref[hd]) for hd in heads]
    for hd in heads:
        lse = jnp.broadcast_to(mx[hd] + jnp.log(ls[hd]), (T, PART_W - ATTN_HEAD_DIM))
        own_ref[hd] = jnp.concatenate([accs[hd] / ls[hd], lse], axis=1)


def _moba_sel(aq, kmean, ak, av):
    B, H, S, hd = av.shape
    nb = S // MOBA_BLOCK
    T = MOBA_BLOCK
    blk = lambda b, j: (b, 0, j, 0)
    return pl.pallas_call(
        functools.partial(_moba_sel_kernel, n_blocks=nb),
        out_shape=(jax.ShapeDtypeStruct((B, H, S, MOBA_TOPK), jnp.int32),
                   jax.ShapeDtypeStruct((B, H, S, PART_W), F32)),
        grid=(B, nb),
        in_specs=[pl.BlockSpec((None, H // 2, T, 128), blk),
                  pl.BlockSpec((None, H, nb, 128), lambda b, j: (b, 0, 0, 0)),
                  pl.BlockSpec((None, H, T, 128), blk),
                  pl.BlockSpec((None, H, T, hd), blk)],
        out_specs=(pl.BlockSpec((None, H, T, MOBA_TOPK), blk),
                   pl.BlockSpec((None, H, T, PART_W), blk)),
        compiler_params=pltpu.CompilerParams(
            dimension_semantics=("arbitrary", "arbitrary"), vmem_limit_bytes=VMEM_LIMIT),
        name="moba_sel",
    )(aq, kmean, ak, av)


def _moba_blk_kernel(tg_ref, nu_ref, q_ref, *refs):
    n = MOBA_TILES_PER_STEP
    k_refs, v_refs, o_ref = refs[:n], refs[n:2 * n], refs[2 * n]
    R = MOBA_ROWS
    t0 = pl.program_id(0) * n

    @pl.when(t0 < nu_ref[0])
    def _():
        scale = ATTN_HEAD_DIM ** -0.5
        ss = [_dot_nt((q_ref[j * R:(j + 1) * R, :] * scale).astype(BF16), k_refs[j][...]) for j in range(n)]
        ms = [jnp.max(s, axis=1, keepdims=True) for s in ss]
        ps = [jnp.exp(s - m) for s, m in zip(ss, ms)]
        ls = [jnp.sum(p, axis=1, keepdims=True) for p in ps]
        accs = [_dot(p.astype(BF16), v_refs[j][...]) for j, p in enumerate(ps)]
        null = _null_partial(R)
        for j in range(n):
            lse = jnp.broadcast_to(ms[j] + jnp.log(ls[j]), (R, PART_W - ATTN_HEAD_DIM))
            row = jnp.concatenate([accs[j] / ls[j], lse], axis=1)
            o_ref[j * R:(j + 1) * R, :] = jnp.where(t0 + j < nu_ref[0], row, null)

    @pl.when(t0 >= nu_ref[0])
    def _():
        o_ref[...] = _null_partial(n * R)


def _moba_blk(qs, tile_group, n_used, ak, av):
    B, H, S, hd = av.shape
    nb = S // MOBA_BLOCK
    R = MOBA_ROWS
    n = MOBA_TILES_PER_STEP
    n_tiles = qs.shape[0] // R
    assert n_tiles % n == 0
    kv = lambda j: (lambda i, tg, nu: (tg[i * n + j] // nb, tg[i * n + j] % nb, 0, 0))
    grid_spec = pltpu.PrefetchScalarGridSpec(
        num_scalar_prefetch=2,
        grid=(n_tiles // n,),
        in_specs=[pl.BlockSpec((n * R, 128), lambda i, tg, nu: (i, 0))]
        + [pl.BlockSpec((None, None, MOBA_BLOCK, 128), kv(j)) for j in range(n)]
        + [pl.BlockSpec((None, None, MOBA_BLOCK, hd), kv(j)) for j in range(n)],
        out_specs=pl.BlockSpec((n * R, PART_W), lambda i, tg, nu: (i, 0)),
    )
    k4 = ak.reshape(B * H, nb, MOBA_BLOCK, 128)
    v4 = av.reshape(B * H, nb, MOBA_BLOCK, hd)
    return pl.pallas_call(
        _moba_blk_kernel,
        out_shape=jax.ShapeDtypeStruct((n_tiles * R, PART_W), F32),
        grid_spec=grid_spec,
        compiler_params=pltpu.CompilerParams(
            dimension_semantics=("arbitrary",), vmem_limit_bytes=VMEM_LIMIT),
        name="moba_blk",
    )(tile_group, n_used, qs, *([k4] * n), *([v4] * n))


def _moba_merge_kernel(own_ref, pg_ref, g_ref, o_ref):
    hd = ATTN_HEAD_DIM
    rows = [own_ref[...]] + [pg_ref[s] for s in range(MOBA_TOPK)]
    lses = [pltpu.roll(r, hd, 1) for r in rows]
    top = lses[0]
    for z in lses[1:]:
        top = jnp.maximum(top, z)
    num = jnp.zeros_like(top)
    den = jnp.zeros_like(top)
    for r, z in zip(rows, lses):
        w = jnp.exp(z - top)
        num = num + w * r
        den = den + w
    o = (num / den)[:, :hd]
    ms = jnp.mean(o * o, axis=-1, keepdims=True)
    o_ref[...] = o * lax.rsqrt(ms + RMS_EPS) * g_ref[...]


def _moba_merge(own, pg, norm_g):
    n = own.shape[0]
    T = 512
    row = lambda i: (i, 0)
    return pl.pallas_call(
        _moba_merge_kernel,
        out_shape=jax.ShapeDtypeStruct((n, ATTN_HEAD_DIM), F32),
        grid=(n // T,),
        in_specs=[pl.BlockSpec((T, PART_W), row),
                  pl.BlockSpec((MOBA_TOPK, T, PART_W), lambda i: (0, i, 0)),
                  pl.BlockSpec((1, ATTN_HEAD_DIM), lambda i: (0, 0))],
        out_specs=pl.BlockSpec((T, ATTN_HEAD_DIM), row),
        compiler_params=pltpu.CompilerParams(
            dimension_semantics=("arbitrary",), vmem_limit_bytes=VMEM_LIMIT),
        name="moba_merge",
    )(own, pg, norm_g)


def _moba(aq, km, ak, av, norm_g):
    B, H, S, hd = av.shape
    nb = S // MOBA_BLOCK
    n_q = B * H * S
    kmp = km.reshape(B, nb, H // 2, 128)
    half = jnp.arange(128, dtype=jnp.int32) // hd
    kmean = jnp.stack([jnp.where(half == h % 2, kmp[:, :, h // 2, :], 0.0) for h in range(H)], axis=1)
    sel, own = _moba_sel(aq, kmean, ak, av)
    bh = jnp.arange(B * H, dtype=jnp.int32)[:, None, None]
    n_groups = B * H * nb
    keys = jnp.where(sel.reshape(B * H, S, MOBA_TOPK) >= 0, bh * nb + sel.reshape(B * H, S, MOBA_TOPK), n_groups)
    n_tiles = (n_q * MOBA_TOPK) // MOBA_ROWS + n_groups
    q_id, dest, tile_group, n_used = _group_layout(keys.reshape(-1), n_groups, MOBA_ROWS, n_tiles, MOBA_TOPK)
    pair_row = (q_id // (H * S) * (H // 2) + (q_id // S) % H // 2) * S + q_id % S
    qs = _sc_gather(aq.reshape(B * (H // 2) * S, 128), pair_row, 256)
    parts = _moba_blk(qs, tile_group, n_used, ak, av)
    pg = _sc_scatter(parts, dest, 256)
    assert pg.shape[0] % n_q == 0
    o = _moba_merge(own.reshape(n_q, PART_W), pg.reshape(pg.shape[0] // n_q, n_q, PART_W), norm_g)
    return o.reshape(B, H, S, hd)


def _mix_kernel(oh_ref, oa_ref, x_ref, g1_ref, sc2_ref, sh2_ref, n2_ref, wo_ref, wr_ref, br_ref,
                x1_ref, h2_ref, idx_ref, gw_ref):
    cat = jnp.concatenate([oh_ref[...]] + [oa_ref[hd] for hd in range(ATTN_HEADS)], axis=1)
    mix = _dot(cat.astype(BF16), wo_ref[...])
    x1 = x_ref[...] + g1_ref[...] * mix
    x1_ref[...] = x1
    ms = jnp.mean(x1 * x1, axis=-1, keepdims=True)
    h2 = x1 * lax.rsqrt(ms + RMS_EPS) * n2_ref[...]
    h2 = h2 * (1.0 + sc2_ref[...]) + sh2_ref[...]
    h2_ref[...] = h2
    logits = _dot(h2, wr_ref[...], precision=HIGHEST) + br_ref[...]
    lane = lax.broadcasted_iota(jnp.int32, logits.shape, 1)
    neg_inf = jnp.float32(-jnp.inf)
    vals, idxs = [], []
    for _ in range(TOP_K):
        m = jnp.max(logits, axis=1, keepdims=True)
        first = jnp.min(jnp.where(logits == m, lane, N_EXPERTS), axis=1, keepdims=True)
        vals.append(m)
        idxs.append(first)
        logits = jnp.where(lane == first, neg_inf, logits)
    e = [jnp.exp(v - vals[0]) for v in vals]
    denom = e[0] + e[1] + e[2] + e[3]
    idx_ref[...] = jnp.concatenate(idxs, axis=1)
    gw_ref[...] = jnp.concatenate([ei / denom for ei in e], axis=1)


def _mix(oh, oa, x, gate1, scale2, shift2, norm2_g, w_out_bf16, w_router, b_router):
    B, S, D = x.shape
    hw = oh.shape[-1]
    tm = 256
    row = lambda b, i: (b, i, 0)
    vec = lambda b, i: (b, 0, 0)
    const = lambda b, i: (0, 0)
    return pl.pallas_call(
        _mix_kernel,
        out_shape=(jax.ShapeDtypeStruct((B, S, D), F32),
                   jax.ShapeDtypeStruct((B, S, D), F32),
                   jax.ShapeDtypeStruct((B, S, TOP_K), jnp.int32),
                   jax.ShapeDtypeStruct((B, S, TOP_K), F32)),
        grid=(B, S // tm),
        in_specs=[pl.BlockSpec((None, tm, hw), row),
                  pl.BlockSpec((None, ATTN_HEADS, tm, ATTN_HEAD_DIM), lambda b, i: (b, 0, i, 0)),
                  pl.BlockSpec((None, tm, D), row),
                  pl.BlockSpec((None, 1, D), vec),
                  pl.BlockSpec((None, 1, D), vec),
                  pl.BlockSpec((None, 1, D), vec),
                  pl.BlockSpec((1, D), const),
                  pl.BlockSpec((D, D), const),
                  pl.BlockSpec((D, N_EXPERTS), const),
                  pl.BlockSpec((1, N_EXPERTS), const)],
        out_specs=(pl.BlockSpec((None, tm, D), row),
                   pl.BlockSpec((None, tm, D), row),
                   pl.BlockSpec((None, tm, TOP_K), row),
                   pl.BlockSpec((None, tm, TOP_K), row)),
        compiler_params=pltpu.CompilerParams(
            dimension_semantics=("arbitrary", "arbitrary"), vmem_limit_bytes=VMEM_LIMIT),
        name="mix",
    )(oh, oa, x, gate1, scale2, shift2, norm2_g, w_out_bf16, w_router, b_router)


def _moe_rows_kernel(be_ref, nu_ref, x_ref, wgu_ref, bgu_ref, wd_ref, bd_ref, y_ref, wgu16, wd16, *, d_ff):
    i = pl.program_id(0)

    @pl.when((i == 0) | (be_ref[i] != be_ref[jnp.maximum(i - 1, 0)]))
    def _():
        wgu16[...] = wgu_ref[...].astype(BF16)
        wd16[...] = wd_ref[...].astype(BF16)

    @pl.when(i < nu_ref[0])
    def _():
        gu = _dot(x_ref[...].astype(BF16), wgu16[...]) + bgu_ref[...]
        gate = jnp.minimum(gu[:, :d_ff], SWIGLU_LIMIT)
        up = jnp.clip(gu[:, d_ff:], -SWIGLU_LIMIT, SWIGLU_LIMIT)
        act = (up + 1.0) * gate * _sigmoid(SWIGLU_ALPHA * gate)
        y_ref[...] = _dot(act.astype(BF16), wd16[...]) + bd_ref[...]

    @pl.when(i >= nu_ref[0])
    def _():
        y_ref[...] = jnp.zeros_like(y_ref)


def _moe_rows(xs, blk_expert, n_used, wgu, bgu, wd, bd):
    D = xs.shape[1]
    bm = MOE_ROWS
    n_blk = xs.shape[0] // bm
    d_ff = wd.shape[1]
    wsel = lambda i, be, nu: (be[i], 0, 0)
    grid_spec = pltpu.PrefetchScalarGridSpec(
        num_scalar_prefetch=2,
        grid=(n_blk,),
        in_specs=[pl.BlockSpec((bm, D), lambda i, be, nu: (i, 0)),
                  pl.BlockSpec((None, D, 2 * d_ff), wsel),
                  pl.BlockSpec((None, 1, 2 * d_ff), wsel),
                  pl.BlockSpec((None, d_ff, D), wsel),
                  pl.BlockSpec((None, 1, D), wsel)],
        out_specs=pl.BlockSpec((bm, D), lambda i, be, nu: (i, 0)),
        scratch_shapes=[pltpu.VMEM((D, 2 * d_ff), BF16), pltpu.VMEM((d_ff, D), BF16)],
    )
    return pl.pallas_call(
        functools.partial(_moe_rows_kernel, d_ff=d_ff),
        out_shape=jax.ShapeDtypeStruct((n_blk * bm, D), F32),
        grid_spec=grid_spec,
        compiler_params=pltpu.CompilerParams(
            dimension_semantics=("arbitrary",), vmem_limit_bytes=VMEM_LIMIT),
        name="moe_rows",
    )(blk_expert, n_used, xs, wgu, bgu.reshape(N_EXPERTS, 1, 2 * d_ff), wd, bd.reshape(N_EXPERTS, 1, D))


def _combine_rows_kernel(*refs):
    y_refs = refs[:TOP_K]
    gw_ref, x1_ref, g2_ref, fg_ref, o_ref = refs[TOP_K:]
    gw = gw_ref[...]
    y = gw[:, 0:1] * y_refs[0][...]
    for kk in range(1, TOP_K):
        y = y + gw[:, kk:kk + 1] * y_refs[kk][...]
    x2 = x1_ref[...] + g2_ref[...] * y
    ms = jnp.mean(x2 * x2, axis=-1, keepdims=True)
    o_ref[...] = x2 * lax.rsqrt(ms + RMS_EPS) * fg_ref[...]


def _combine_rows(yg, gates, x1, gate2, final_g):
    B, S, D = x1.shape
    T = B * S
    tm = 256
    steps = T // tm
    steps_per_batch = S // tm
    slot_spec = lambda kk: pl.BlockSpec((tm, D), lambda i: (kk * steps + i, 0))
    return pl.pallas_call(
        _combine_rows_kernel,
        out_shape=jax.ShapeDtypeStruct((T, D), F32),
        grid=(steps,),
        in_specs=[slot_spec(kk) for kk in range(TOP_K)] + [
            pl.BlockSpec((tm, TOP_K), lambda i: (i, 0)),
            pl.BlockSpec((tm, D), lambda i: (i, 0)),
            pl.BlockSpec((None, 1, D), lambda i: (i // steps_per_batch, 0, 0)),
            pl.BlockSpec((1, D), lambda i: (0, 0))],
        out_specs=pl.BlockSpec((tm, D), lambda i: (i, 0)),
        compiler_params=pltpu.CompilerParams(
            dimension_semantics=("arbitrary",), vmem_limit_bytes=VMEM_LIMIT),
        name="combine_rows",
    )(*([yg] * TOP_K), gates, x1.reshape(T, D), gate2, final_g).reshape(B, S, D)


def _rotary_tables(positions):
    inv_freq = jnp.exp(-math.log(ROPE_THETA) * jnp.arange(0, ROT_DIM, 2, dtype=F32) / ROT_DIM)
    ang = positions.astype(F32)[:, :, None] * inv_freq
    cos, sin = jnp.cos(ang), jnp.sin(ang)
    B, S = positions.shape
    pad = ATTN_HEAD_DIM - ROT_DIM
    ct = jnp.concatenate([cos, cos, jnp.ones((B, S, pad), F32)], axis=-1)
    st = jnp.concatenate([-sin, sin, jnp.zeros((B, S, pad), F32)], axis=-1)
    return jnp.concatenate([ct, ct], axis=-1), jnp.concatenate([st, st], axis=-1)


def kernel(x, c, positions, w_ada, b_ada, norm1_g, w_in, hgrn_lb_logits, hgrn_norm_g, attn_norm_g,
           w_out, norm2_g, w_router, b_router, w_gate_up, b_gate_up, w_down, b_down, final_norm_g):
    B, S, D = x.shape
    T = B * S
    assert w_in.shape[0] == 1, "single-layer block: the final norm is fused into the combine step"
    l = 0
    ctab, stab = _rotary_tables(positions)
    lower_bounds = jnp.cumsum(jax.nn.softmax(hgrn_lb_logits.astype(F32), axis=0), axis=0)
    mod = _ada(c, w_ada[l], b_ada[l])
    shift1, scale1, gate1, shift2, scale2, gate2 = jnp.split(mod[:, None, :], N_MOD, axis=-1)
    q, k, lf, v, gt, aq, ak, av, km = _proj(
        x, scale1, shift1, norm1_g[l][None], w_in[l].astype(BF16), lower_bounds[l][None], ctab, stab)
    o_a = _moba(aq, km, ak, av, attn_norm_g[l][None])
    o_h = _hgrn(q, k, lf, v, gt, hgrn_norm_g[l][None])
    x1, h2, top_idx, gates = _mix(o_h, o_a, x, gate1, scale2, shift2, norm2_g[l][None],
                                  w_out[l].astype(BF16), w_router[l], b_router[l][None])
    n_blk = (T * TOP_K) // MOE_ROWS + N_EXPERTS
    tok, dest, blk_expert, n_used = _group_layout(top_idx.reshape(-1), N_EXPERTS, MOE_ROWS, n_blk, TOP_K)
    xs = _sc_gather(h2.reshape(T, D), tok, 64)
    y_sorted = _moe_rows(xs, blk_expert, n_used, w_gate_up[l], b_gate_up[l], w_down[l], b_down[l])
    yg = _sc_scatter(y_sorted, dest, 64)
    return _combine_rows(yg, gates.reshape(T, TOP_K), x1, gate2, final_norm_g[None])
```

```python
import functools
import math

import jax
import jax.numpy as jnp
from jax import lax
from jax.experimental import pallas as pl
from jax.experimental.pallas import tpu as pltpu
from jax.experimental.pallas import tpu_sc as plsc

F32 = jnp.float32
BF16 = jnp.bfloat16
HIGHEST = lax.Precision.HIGHEST

HGRN_DK = 128
HGRN_CHUNK = 64
ATTN_HEADS = 4
ATTN_HEAD_DIM = 64
ROT_DIM = ATTN_HEAD_DIM // 4
ROPE_THETA = 500000.0
MOBA_BLOCK = 256
MOBA_TOPK = 3
N_EXPERTS = 32
TOP_K = 4
SWIGLU_ALPHA = 1.702
SWIGLU_LIMIT = 7.0
N_MOD = 6
RMS_EPS = 1e-6

EXP_CLAMP = 80.0
MOE_ROWS = 256
MOBA_ROWS = 256
MOBA_TILES_PER_STEP = 4
PART_W = 128
VMEM_LIMIT = 56 * 1024 * 1024
SC_CORES = 2
SC_SUBCORES = 16


def _sigmoid(x):
    return 1.0 / (1.0 + jnp.exp(-x))


def _dot(a, b, **kw):
    return jnp.dot(a, b, preferred_element_type=F32, **kw)


def _dot_nt(a, b, **kw):
    return lax.dot_general(a, b, (((1,), (1,)), ((), ())), preferred_element_type=F32, **kw)


def _pack_bf16_pairs(x):
    w = x.shape[1] // 2
    bits = lax.bitcast_convert_type(x.astype(BF16).astype(F32), jnp.int32)
    return bits[:, w:] | lax.shift_right_logical(bits[:, :w], 16)


def _unpack_bf16_pairs(p):
    lo = lax.bitcast_convert_type(lax.shift_left(p, 16), F32)
    hi = lax.bitcast_convert_type(p & jnp.int32(-65536), F32)
    return jnp.concatenate([lo, hi], axis=1)


def _ada_kernel(c_ref, w_ref, b_ref, o_ref):
    c = c_ref[...]
    o_ref[...] = _dot(c * _sigmoid(c), w_ref[...], precision=HIGHEST) + b_ref[...]


def _ada(c, w_ada, b_ada):
    B, D = c.shape
    N = w_ada.shape[1]
    tn = N // 4
    c8 = jnp.zeros((8, D), F32).at[:B].set(c)
    out = pl.pallas_call(
        _ada_kernel,
        out_shape=jax.ShapeDtypeStruct((8, N), F32),
        grid=(N // tn,),
        in_specs=[pl.BlockSpec((8, D), lambda j: (0, 0)),
                  pl.BlockSpec((D, tn), lambda j: (0, j)),
                  pl.BlockSpec((1, tn), lambda j: (0, j))],
        out_specs=pl.BlockSpec((8, tn), lambda j: (0, j)),
        compiler_params=pltpu.CompilerParams(vmem_limit_bytes=VMEM_LIMIT),
        name="ada",
    )(c8, w_ada, b_ada.reshape(1, N))
    return out[:B]


def _proj_kernel(x_ref, sc_ref, sh_ref, g_ref, w_ref, lb_ref, ct_ref, st_ref,
                 q_ref, k_ref, lf_ref, v_ref, gt_ref, aq_ref, ak_ref, av_ref, km_ref,
                 *, hw, aw):
    x = x_ref[...]
    ms = jnp.mean(x * x, axis=-1, keepdims=True)
    h = x * lax.rsqrt(ms + RMS_EPS) * g_ref[...]
    h = h * (1.0 + sc_ref[...]) + sh_ref[...]
    proj = _dot(h.astype(BF16), w_ref[...])

    hq = proj[:, 0:hw]
    hf = proj[:, hw:2 * hw]
    hg = proj[:, 3 * hw:4 * hw]
    q_ref[...] = hq * _sigmoid(hq) * (HGRN_DK ** -0.5)
    lb = lb_ref[...]
    f = lb + (1.0 - lb) * _sigmoid(hf)
    k_ref[...] = 1.0 - f
    lf_ref[...] = jnp.log(f)
    v_ref[...] = proj[:, 2 * hw:3 * hw]
    gt_ref[...] = hg * _sigmoid(hg)

    ct = jnp.concatenate([ct_ref[...]] * (aw // 128), axis=1)
    st = jnp.concatenate([st_ref[...]] * (aw // 128), axis=1)
    lane = lax.broadcasted_iota(jnp.int32, ct.shape, 1) % ATTN_HEAD_DIM
    first_half = lane < (ROT_DIM // 2)

    def rot(t):
        partner = jnp.where(first_half, pltpu.roll(t, aw - ROT_DIM // 2, 1), pltpu.roll(t, ROT_DIM // 2, 1))
        return t * ct + partner * st

    base = 4 * hw
    aq = rot(proj[:, base:base + aw])
    ak = rot(proj[:, base + aw:base + 2 * aw])
    av = proj[:, base + 2 * aw:base + 3 * aw]
    km_ref[...] = jnp.mean(ak, axis=0, keepdims=True)
    lane128 = lax.broadcasted_iota(jnp.int32, (x.shape[0], 128), 1)
    for pair in range(ATTN_HEADS // 2):
        aq_ref[pair] = aq[:, pair * 128:(pair + 1) * 128]
    for hd in range(ATTN_HEADS):
        pair, half = divmod(hd, 2)
        in_head = (lane128 // ATTN_HEAD_DIM) == half
        ak_ref[hd] = jnp.where(in_head, ak[:, pair * 128:(pair + 1) * 128], 0.0).astype(BF16)
        av_ref[hd] = av[:, hd * ATTN_HEAD_DIM:(hd + 1) * ATTN_HEAD_DIM].astype(BF16)


def _proj(x, scale1, shift1, norm_g, w_in_bf16, lb, ctab, stab):
    B, S, D = x.shape
    hw = lb.shape[-1]
    aw = ATTN_HEADS * ATTN_HEAD_DIM
    tm = MOBA_BLOCK
    nb = S // MOBA_BLOCK
    n_proj = w_in_bf16.shape[1]
    row = lambda b, i: (b, i, 0)
    vec = lambda b, i: (b, 0, 0)
    head = lambda b, i: (b, 0, i, 0)
    out_shapes = (
        jax.ShapeDtypeStruct((B, S, hw), F32),
        jax.ShapeDtypeStruct((B, S, hw), F32),
        jax.ShapeDtypeStruct((B, S, hw), F32),
        jax.ShapeDtypeStruct((B, S, hw), F32),
        jax.ShapeDtypeStruct((B, S, hw), F32),
        jax.ShapeDtypeStruct((B, ATTN_HEADS // 2, S, 128), F32),
        jax.ShapeDtypeStruct((B, ATTN_HEADS, S, 128), BF16),
        jax.ShapeDtypeStruct((B, ATTN_HEADS, S, ATTN_HEAD_DIM), BF16),
        jax.ShapeDtypeStruct((B, nb, 1, aw), F32),
    )
    hspec = pl.BlockSpec((None, tm, hw), row)
    aspec = pl.BlockSpec((None, ATTN_HEADS, tm, ATTN_HEAD_DIM), head)
    return pl.pallas_call(
        functools.partial(_proj_kernel, hw=hw, aw=aw),
        out_shape=out_shapes,
        grid=(B, S // tm),
        in_specs=[pl.BlockSpec((None, tm, D), row),
                  pl.BlockSpec((None, 1, D), vec),
                  pl.BlockSpec((None, 1, D), vec),
                  pl.BlockSpec((1, D), lambda b, i: (0, 0)),
                  pl.BlockSpec((D, n_proj), lambda b, i: (0, 0)),
                  pl.BlockSpec((1, hw), lambda b, i: (0, 0)),
                  pl.BlockSpec((None, tm, 128), row),
                  pl.BlockSpec((None, tm, 128), row)],
        out_specs=(hspec, hspec, hspec, hspec, hspec,
                   pl.BlockSpec((None, ATTN_HEADS // 2, tm, 128), head),
                   pl.BlockSpec((None, ATTN_HEADS, tm, 128), head), aspec,
                   pl.BlockSpec((None, None, 1, aw), lambda b, i: (b, i, 0, 0))),
        compiler_params=pltpu.CompilerParams(
            dimension_semantics=("arbitrary", "arbitrary"), vmem_limit_bytes=VMEM_LIMIT),
        name="proj",
    )(x, scale1, shift1, norm_g, w_in_bf16, lb, ctab, stab)


def _hgrn_kernel(q_ref, k_ref, lf_ref, v_ref, gt_ref, gn_ref, o_ref, st_ref, *, n_heads, n_chunks):
    @pl.when(pl.program_id(1) == 0)
    def _():
        st_ref[...] = jnp.zeros_like(st_ref)

    C = HGRN_CHUNK
    r = lax.broadcasted_iota(jnp.int32, (C, C), 0)
    c = lax.broadcasted_iota(jnp.int32, (C, C), 1)
    tril = c <= r
    ltri = tril.astype(F32)
    gn = gn_ref[...]

    def chunk(ci, carry):
        r0 = pl.multiple_of(ci * C, C)
        rows = pl.ds(r0, C)
        b_all = _dot(ltri, lf_ref[rows, :], precision=HIGHEST)
        for hd in range(n_heads):
            sl = slice(hd * HGRN_DK, (hd + 1) * HGRN_DK)
            b = b_all[:, sl]
            b_last = b[C - 1:C, :]
            q = q_ref[rows, sl]
            k = k_ref[rows, sl]
            v = v_ref[rows, sl]
            state_t = st_ref[hd]
            o_inter = _dot_nt((q * jnp.exp(b)).astype(BF16), state_t.astype(BF16))
            rho = 0.5 * b_last
            qa = q * jnp.exp(jnp.minimum(b - rho, EXP_CLAMP))
            kb = k * jnp.exp(jnp.minimum(rho - b, EXP_CLAMP))
            scores = jnp.where(tril, _dot_nt(qa.astype(BF16), kb.astype(BF16)), 0.0)
            v16 = v.astype(BF16)
            o = o_inter + _dot(scores.astype(BF16), v16)
            kd = k * jnp.exp(b_last - b)
            st_ref[hd] = state_t * jnp.exp(b_last) + _dot(v.T.astype(BF16), kd.astype(BF16))
            ms = jnp.mean(o * o, axis=-1, keepdims=True)
            o_ref[rows, sl] = o * lax.rsqrt(ms + RMS_EPS) * gn * gt_ref[rows, sl]
        return carry

    lax.fori_loop(0, n_chunks, chunk, 0)


def _hgrn(q, k, lf, v, gt, norm_g):
    B, S, hw = q.shape
    n_heads = hw // HGRN_DK
    tc = 512
    spec = pl.BlockSpec((None, tc, hw), lambda b, i: (b, i, 0))
    return pl.pallas_call(
        functools.partial(_hgrn_kernel, n_heads=n_heads, n_chunks=tc // HGRN_CHUNK),
        out_shape=jax.ShapeDtypeStruct((B, S, hw), F32),
        grid=(B, S // tc),
        in_specs=[spec, spec, spec, spec, spec, pl.BlockSpec((1, HGRN_DK), lambda b, i: (0, 0))],
        out_specs=spec,
        scratch_shapes=[pltpu.VMEM((n_heads, HGRN_DK, HGRN_DK), F32)],
        compiler_params=pltpu.CompilerParams(
            dimension_semantics=("arbitrary", "arbitrary"), vmem_limit_bytes=VMEM_LIMIT),
        name="hgrn",
    )(q, k, lf, v, gt, norm_g)


def _sc_gather(table, idx, chunk):
    M = idx.shape[0]
    D = table.shape[1]
    n_workers = SC_CORES * SC_SUBCORES
    per_worker = M // n_workers
    assert per_worker * n_workers == M and per_worker % chunk == 0 and chunk % 8 == 0
    mesh = plsc.VectorSubcoreMesh(core_axis_name="c", subcore_axis_name="s")

    @functools.partial(
        pl.kernel, mesh=mesh,
        out_type=jax.ShapeDtypeStruct((M, D), table.dtype),
        scratch_types=[pltpu.VMEM((chunk,), jnp.int32),
                       pltpu.VMEM((chunk, D), table.dtype),
                       pltpu.SemaphoreType.DMA],
    )
    def gather_kernel(table_hbm, idx_hbm, out_hbm, idx_v, rows_v, sem):
        wid = lax.axis_index("s") * SC_CORES + lax.axis_index("c")
        base = wid * per_worker

        @pl.loop(0, per_worker // chunk)
        def _(j):
            off = pl.multiple_of(base + j * chunk, 8)
            pltpu.sync_copy(idx_hbm.at[pl.ds(off, chunk)], idx_v)
            pltpu.async_copy(table_hbm.at[idx_v], rows_v, sem).wait()
            pltpu.sync_copy(rows_v, out_hbm.at[pl.ds(off, chunk)])

    return gather_kernel(table, idx)


def _sc_scatter(rows, dst, chunk):
    M, D = rows.shape
    n_workers = SC_CORES * SC_SUBCORES
    per_worker = M // n_workers
    assert per_worker * n_workers == M and per_worker % chunk == 0 and chunk % 8 == 0
    mesh = plsc.VectorSubcoreMesh(core_axis_name="c", subcore_axis_name="s")

    @functools.partial(
        pl.kernel, mesh=mesh,
        out_type=jax.ShapeDtypeStruct((M, D), rows.dtype),
        scratch_types=[pltpu.VMEM((chunk,), jnp.int32),
                       pltpu.VMEM((chunk, D), rows.dtype),
                       pltpu.SemaphoreType.DMA],
    )
    def scatter_kernel(rows_hbm, dst_hbm, out_hbm, idx_v, rows_v, sem):
        wid = lax.axis_index("s") * SC_CORES + lax.axis_index("c")
        base = wid * per_worker

        @pl.loop(0, per_worker // chunk)
        def _(j):
            off = pl.multiple_of(base + j * chunk, 8)
            pltpu.sync_copy(dst_hbm.at[pl.ds(off, chunk)], idx_v)
            pltpu.sync_copy(rows_hbm.at[pl.ds(off, chunk)], rows_v)
            pltpu.async_copy(rows_v, out_hbm.at[idx_v], sem).wait()

    return scatter_kernel(rows, dst)


def _group_layout(keys, n_groups, bm, n_tiles, fan):
    n = keys.shape[0]
    n_rows = n_tiles * bm
    n_fill = n_rows - n
    n_hi = n_groups // 32 + 1
    oh_hi = ((keys // 32)[:, None] == jnp.arange(n_hi, dtype=jnp.int32)[None, :]).astype(BF16)
    oh_lo = ((keys % 32)[:, None] == jnp.arange(32, dtype=jnp.int32)[None, :]).astype(BF16)
    counts = jnp.einsum('ah,al->hl', oh_hi, oh_lo, preferred_element_type=F32)
    counts = counts.reshape(-1)[:n_groups].astype(jnp.int32)
    padded = (counts + bm - 1) // bm * bm
    fill_end = jnp.cumsum(padded - counts)
    f_ids = jnp.arange(n_fill, dtype=jnp.int32)
    fill_key = jnp.sum((fill_end[None, :] <= f_ids[:, None]).astype(jnp.int32), axis=1)
    all_keys = jnp.concatenate([keys, fill_key])
    sorted_k, a = lax.sort((all_keys, jnp.arange(n_rows, dtype=jnp.int32)), num_keys=1, is_stable=True)
    tile_group = jnp.minimum(sorted_k[::bm], n_groups - 1)
    real = a < n
    n_items = n // fan
    item = jnp.where(real, a, ((a - n) * 61) % n) // fan
    dest = jnp.where(real, (a % fan) * n_items + a // fan, a)
    n_used = (jnp.sum(padded) // bm).astype(jnp.int32).reshape(1)
    return item, dest, tile_group, n_used


def _null_partial(rows):
    lane = lax.broadcasted_iota(jnp.int32, (rows, PART_W), 1)
    return jnp.where(lane < ATTN_HEAD_DIM, 0.0, -jnp.inf).astype(F32)


def _moba_sel_kernel(q_ref, km_ref, k_ref, v_ref, sel_ref, own_ref, *, n_blocks):
    j = pl.program_id(1)
    T = MOBA_BLOCK
    heads = range(ATTN_HEADS)
    qs = [q_ref[hd // 2] for hd in heads]
    gates = [_dot_nt(qs[hd], km_ref[hd], precision=HIGHEST) for hd in heads]
    lane = lax.broadcasted_iota(jnp.int32, gates[0].shape, 1)
    neg_inf = jnp.float32(-jnp.inf)
    gates = [jnp.where(lane < j, g, neg_inf) for g in gates]
    picks = [[] for _ in heads]
    for _ in range(MOBA_TOPK):
        ms = [jnp.max(g, axis=1, keepdims=True) for g in gates]
        firsts = [jnp.min(jnp.where(g == m, lane, n_blocks), axis=1, keepdims=True) for g, m in zip(gates, ms)]
        for hd in heads:
            picks[hd].append(jnp.where(ms[hd] > neg_inf, firsts[hd], -1))
        gates = [jnp.where(lane == f, neg_inf, g) for g, f in zip(gates, firsts)]
    for hd in heads:
        sel_ref[hd] = jnp.concatenate(picks[hd], axis=1)
    causal = lax.broadcasted_iota(jnp.int32, (T, T), 1) <= lax.broadcasted_iota(jnp.int32, (T, T), 0)
    scale = ATTN_HEAD_DIM ** -0.5
    ss = [jnp.where(causal, _dot_nt((qs[hd] * scale).astype(BF16), k_ref[hd]), neg_inf) for hd in heads]
    mx = [jnp.max(s, axis=1, keepdims=True) for s in ss]
    ps = [jnp.exp(s - m) for s, m in zip(ss, mx)]
    ls = [jnp.sum(p, axis=1, keepdims=True) for p in ps]
    accs = [_dot(ps[hd].astype(BF16), v_ref[hd]) for hd in heads]
    for hd in heads:
        lse = jnp.broadcast_to(mx[hd] + jnp.log(ls[hd]), (T, PART_W - ATTN_HEAD_DIM))
        own_ref[hd] = jnp.concatenate([accs[hd] / ls[hd], lse], axis=1)


def _moba_sel(aq, kmean, ak, av):
    B, H, S, hd = av.shape
    nb = S // MOBA_BLOCK
    T = MOBA_BLOCK
    blk = lambda b, j: (b, 0, j, 0)
    return pl.pallas_call(
        functools.partial(_moba_sel_kernel, n_blocks=nb),
        out_shape=(jax.ShapeDtypeStruct((B, H, S, MOBA_TOPK), jnp.int32),
                   jax.ShapeDtypeStruct((B, H, S, PART_W), F32)),
        grid=(B, nb),
        in_specs=[pl.BlockSpec((None, H // 2, T, 128), blk),
                  pl.BlockSpec((None, H, nb, 128), lambda b, j: (b, 0, 0, 0)),
                  pl.BlockSpec((None, H, T, 128), blk),
                  pl.BlockSpec((None, H, T, hd), blk)],
        out_specs=(pl.BlockSpec((None, H, T, MOBA_TOPK), blk),
                   pl.BlockSpec((None, H, T, PART_W), blk)),
        compiler_params=pltpu.CompilerParams(
            dimension_semantics=("arbitrary", "arbitrary"), vmem_limit_bytes=VMEM_LIMIT),
        name="moba_sel",
    )(aq, kmean, ak, av)


def _moba_blk_kernel(tg_ref, nu_ref, q_ref, *refs):
    n = MOBA_TILES_PER_STEP
    k_refs, v_refs, o_ref = refs[:n], refs[n:2 * n], refs[2 * n]
    R = MOBA_ROWS
    t0 = pl.program_id(0) * n

    @pl.when(t0 < nu_ref[0])
    def _():
        scale = ATTN_HEAD_DIM ** -0.5
        ss = [_dot_nt((q_ref[j * R:(j + 1) * R, :] * scale).astype(BF16), k_refs[j][...]) for j in range(n)]
        ms = [jnp.max(s, axis=1, keepdims=True) for s in ss]
        ps = [jnp.exp(s - m) for s, m in zip(ss, ms)]
        ls = [jnp.sum(p, axis=1, keepdims=True) for p in ps]
        accs = [_dot(p.astype(BF16), v_refs[j][...]) for j, p in enumerate(ps)]
        null = _null_partial(R)
        for j in range(n):
            lse = jnp.broadcast_to(ms[j] + jnp.log(ls[j]), (R, PART_W - ATTN_HEAD_DIM))
            row = jnp.concatenate([accs[j] / ls[j], lse], axis=1)
            o_ref[j * R:(j + 1) * R, :] = jnp.where(t0 + j < nu_ref[0], row, null)

    @pl.when(t0 >= nu_ref[0])
    def _():
        o_ref[...] = _null_partial(n * R)


def _moba_blk(qs, tile_group, n_used, ak, av):
    B, H, S, hd = av.shape
    nb = S // MOBA_BLOCK
    R = MOBA_ROWS
    n = MOBA_TILES_PER_STEP
    n_tiles = qs.shape[0] // R
    assert n_tiles % n == 0
    kv = lambda j: (lambda i, tg, nu: (tg[i * n + j] // nb, tg[i * n + j] % nb, 0, 0))
    grid_spec = pltpu.PrefetchScalarGridSpec(
        num_scalar_prefetch=2,
        grid=(n_tiles // n,),
        in_specs=[pl.BlockSpec((n * R, 128), lambda i, tg, nu: (i, 0))]
        + [pl.BlockSpec((None, None, MOBA_BLOCK, 128), kv(j)) for j in range(n)]
        + [pl.BlockSpec((None, None, MOBA_BLOCK, hd), kv(j)) for j in range(n)],
        out_specs=pl.BlockSpec((n * R, PART_W), lambda i, tg, nu: (i, 0)),
    )
    k4 = ak.reshape(B * H, nb, MOBA_BLOCK, 128)
    v4 = av.reshape(B * H, nb, MOBA_BLOCK, hd)
    return pl.pallas_call(
        _moba_blk_kernel,
        out_shape=jax.ShapeDtypeStruct((n_tiles * R, PART_W), F32),
        grid_spec=grid_spec,
        compiler_params=pltpu.CompilerParams(
            dimension_semantics=("arbitrary",), vmem_limit_bytes=VMEM_LIMIT),
        name="moba_blk",
    )(tile_group, n_used, qs, *([k4] * n), *([v4] * n))


def _moba_merge_kernel(own_ref, pg_ref, g_ref, o_ref):
    hd = ATTN_HEAD_DIM
    rows = [own_ref[...]] + [pg_ref[s] for s in range(MOBA_TOPK)]
    lses = [pltpu.roll(r, hd, 1) for r in rows]
    top = lses[0]
    for z in lses[1:]:
        top = jnp.maximum(top, z)
    num = jnp.zeros_like(top)
    den = jnp.zeros_like(top)
    for r, z in zip(rows, lses):
        w = jnp.exp(z - top)
        num = num + w * r
        den = den + w
    o = (num / den)[:, :hd]
    ms = jnp.mean(o * o, axis=-1, keepdims=True)
    o_ref[...] = o * lax.rsqrt(ms + RMS_EPS) * g_ref[...]


def _moba_merge(own, pg, norm_g):
    n = own.shape[0]
    T = 512
    row = lambda i: (i, 0)
    return pl.pallas_call(
        _moba_merge_kernel,
        out_shape=jax.ShapeDtypeStruct((n, ATTN_HEAD_DIM), F32),
        grid=(n // T,),
        in_specs=[pl.BlockSpec((T, PART_W), row),
                  pl.BlockSpec((MOBA_TOPK, T, PART_W), lambda i: (0, i, 0)),
                  pl.BlockSpec((1, ATTN_HEAD_DIM), lambda i: (0, 0))],
        out_specs=pl.BlockSpec((T, ATTN_HEAD_DIM), row),
        compiler_params=pltpu.CompilerParams(
            dimension_semantics=("arbitrary",), vmem_limit_bytes=VMEM_LIMIT),
        name="moba_merge",
    )(own, pg, norm_g)


def _moba(aq, km, ak, av, norm_g):
    B, H, S, hd = av.shape
    nb = S // MOBA_BLOCK
    n_q = B * H * S
    kmp = km.reshape(B, nb, H // 2, 128)
    half = jnp.arange(128, dtype=jnp.int32) // hd
    kmean = jnp.stack([jnp.where(half == h % 2, kmp[:, :, h // 2, :], 0.0) for h in range(H)], axis=1)
    sel, own = _moba_sel(aq, kmean, ak, av)
    bh = jnp.arange(B * H, dtype=jnp.int32)[:, None, None]
    n_groups = B * H * nb
    keys = jnp.where(sel.reshape(B * H, S, MOBA_TOPK) >= 0, bh * nb + sel.reshape(B * H, S, MOBA_TOPK), n_groups)
    n_tiles = (n_q * MOBA_TOPK) // MOBA_ROWS + n_groups
    q_id, dest, tile_group, n_used = _group_layout(keys.reshape(-1), n_groups, MOBA_ROWS, n_tiles, MOBA_TOPK)
    pair_row = (q_id // (H * S) * (H // 2) + (q_id // S) % H // 2) * S + q_id % S
    qs = _sc_gather(aq.reshape(B * (H // 2) * S, 128), pair_row, 256)
    parts = _moba_blk(qs, tile_group, n_used, ak, av)
    pg = _sc_scatter(parts, dest, 256)
    assert pg.shape[0] % n_q == 0
    o = _moba_merge(own.reshape(n_q, PART_W), pg.reshape(pg.shape[0] // n_q, n_q, PART_W), norm_g)
    return o.reshape(B, H, S, hd)


def _mix_kernel(oh_ref, oa_ref, x_ref, g1_ref, sc2_ref, sh2_ref, n2_ref, wo_ref, wr_ref, br_ref,
                x1_ref, h2_ref, idx_ref, gw_ref):
    cat = jnp.concatenate([oh_ref[...]] + [oa_ref[hd] for hd in range(ATTN_HEADS)], axis=1)
    mix = _dot(cat.astype(BF16), wo_ref[...])
    x1 = x_ref[...] + g1_ref[...] * mix
    x1_ref[...] = x1
    ms = jnp.mean(x1 * x1, axis=-1, keepdims=True)
    h2 = x1 * lax.rsqrt(ms + RMS_EPS) * n2_ref[...]
    h2 = h2 * (1.0 + sc2_ref[...]) + sh2_ref[...]
    h2_ref[...] = _pack_bf16_pairs(h2)
    logits = _dot(h2, wr_ref[...], precision=HIGHEST) + br_ref[...]
    lane = lax.broadcasted_iota(jnp.int32, logits.shape, 1)
    neg_inf = jnp.float32(-jnp.inf)
    vals, idxs = [], []
    for _ in range(TOP_K):
        m = jnp.max(logits, axis=1, keepdims=True)
        first = jnp.min(jnp.where(logits == m, lane, N_EXPERTS), axis=1, keepdims=True)
        vals.append(m)
        idxs.append(first)
        logits = jnp.where(lane == first, neg_inf, logits)
    e = [jnp.exp(v - vals[0]) for v in vals]
    denom = e[0] + e[1] + e[2] + e[3]
    idx_ref[...] = jnp.concatenate(idxs, axis=1)
    gw_ref[...] = jnp.concatenate([ei / denom for ei in e], axis=1)


def _mix(oh, oa, x, gate1, scale2, shift2, norm2_g, w_out_bf16, w_router, b_router):
    B, S, D = x.shape
    hw = oh.shape[-1]
    tm = 256
    row = lambda b, i: (b, i, 0)
    vec = lambda b, i: (b, 0, 0)
    const = lambda b, i: (0, 0)
    return pl.pallas_call(
        _mix_kernel,
        out_shape=(jax.ShapeDtypeStruct((B, S, D), F32),
                   jax.ShapeDtypeStruct((B, S, D // 2), jnp.int32),
                   jax.ShapeDtypeStruct((B, S, TOP_K), jnp.int32),
                   jax.ShapeDtypeStruct((B, S, TOP_K), F32)),
        grid=(B, S // tm),
        in_specs=[pl.BlockSpec((None, tm, hw), row),
                  pl.BlockSpec((None, ATTN_HEADS, tm, ATTN_HEAD_DIM), lambda b, i: (b, 0, i, 0)),
                  pl.BlockSpec((None, tm, D), row),
                  pl.BlockSpec((None, 1, D), vec),
                  pl.BlockSpec((None, 1, D), vec),
                  pl.BlockSpec((None, 1, D), vec),
                  pl.BlockSpec((1, D), const),
                  pl.BlockSpec((D, D), const),
                  pl.BlockSpec((D, N_EXPERTS), const),
                  pl.BlockSpec((1, N_EXPERTS), const)],
        out_specs=(pl.BlockSpec((None, tm, D), row),
                   pl.BlockSpec((None, tm, D // 2), row),
                   pl.BlockSpec((None, tm, TOP_K), row),
                   pl.BlockSpec((None, tm, TOP_K), row)),
        compiler_params=pltpu.CompilerParams(
            dimension_semantics=("arbitrary", "arbitrary"), vmem_limit_bytes=VMEM_LIMIT),
        name="mix",
    )(oh, oa, x, gate1, scale2, shift2, norm2_g, w_out_bf16, w_router, b_router)


def _moe_rows_kernel(be_ref, nu_ref, x_ref, wgu_ref, bgu_ref, wd_ref, bd_ref, y_ref, wgu16, wd16, *, d_ff):
    i = pl.program_id(0)

    @pl.when((i == 0) | (be_ref[i] != be_ref[jnp.maximum(i - 1, 0)]))
    def _():
        wgu16[...] = wgu_ref[...].astype(BF16)
        wd16[...] = wd_ref[...].astype(BF16)

    @pl.when(i < nu_ref[0])
    def _():
        gu = _dot(_unpack_bf16_pairs(x_ref[...]).astype(BF16), wgu16[...]) + bgu_ref[...]
        gate = jnp.minimum(gu[:, :d_ff], SWIGLU_LIMIT)
        up = jnp.clip(gu[:, d_ff:], -SWIGLU_LIMIT, SWIGLU_LIMIT)
        act = (up + 1.0) * gate * _sigmoid(SWIGLU_ALPHA * gate)
        y_ref[...] = _pack_bf16_pairs(_dot(act.astype(BF16), wd16[...]) + bd_ref[...])

    @pl.when(i >= nu_ref[0])
    def _():
        y_ref[...] = jnp.zeros_like(y_ref)


def _moe_rows(xs, blk_expert, n_used, wgu, bgu, wd, bd):
    D = 2 * xs.shape[1]
    bm = MOE_ROWS
    n_blk = xs.shape[0] // bm
    d_ff = wd.shape[1]
    wsel = lambda i, be, nu: (be[i], 0, 0)
    grid_spec = pltpu.PrefetchScalarGridSpec(
        num_scalar_prefetch=2,
        grid=(n_blk,),
        in_specs=[pl.BlockSpec((bm, D // 2), lambda i, be, nu: (i, 0)),
                  pl.BlockSpec((None, D, 2 * d_ff), wsel),
                  pl.BlockSpec((None, 1, 2 * d_ff), wsel),
                  pl.BlockSpec((None, d_ff, D), wsel),
                  pl.BlockSpec((None, 1, D), wsel)],
        out_specs=pl.BlockSpec((bm, D // 2), lambda i, be, nu: (i, 0)),
        scratch_shapes=[pltpu.VMEM((D, 2 * d_ff), BF16), pltpu.VMEM((d_ff, D), BF16)],
    )
    return pl.pallas_call(
        functools.partial(_moe_rows_kernel, d_ff=d_ff),
        out_shape=jax.ShapeDtypeStruct((n_blk * bm, D // 2), jnp.int32),
        grid_spec=grid_spec,
        compiler_params=pltpu.CompilerParams(
            dimension_semantics=("arbitrary",), vmem_limit_bytes=VMEM_LIMIT),
        name="moe_rows",
    )(blk_expert, n_used, xs, wgu, bgu.reshape(N_EXPERTS, 1, 2 * d_ff), wd, bd.reshape(N_EXPERTS, 1, D))


def _combine_rows_kernel(*refs):
    y_refs = refs[:TOP_K]
    gw_ref, x1_ref, g2_ref, fg_ref, o_ref = refs[TOP_K:]
    gw = gw_ref[...]
    y = gw[:, 0:1] * _unpack_bf16_pairs(y_refs[0][...])
    for kk in range(1, TOP_K):
        y = y + gw[:, kk:kk + 1] * _unpack_bf16_pairs(y_refs[kk][...])
    x2 = x1_ref[...] + g2_ref[...] * y
    ms = jnp.mean(x2 * x2, axis=-1, keepdims=True)
    o_ref[...] = x2 * lax.rsqrt(ms + RMS_EPS) * fg_ref[...]


def _combine_rows(yg, gates, x1, gate2, final_g):
    B, S, D = x1.shape
    T = B * S
    tm = 256
    steps = T // tm
    steps_per_batch = S // tm
    slot_spec = lambda kk: pl.BlockSpec((tm, D // 2), lambda i: (kk * steps + i, 0))
    return pl.pallas_call(
        _combine_rows_kernel,
        out_shape=jax.ShapeDtypeStruct((T, D), F32),
        grid=(steps,),
        in_specs=[slot_spec(kk) for kk in range(TOP_K)] + [
            pl.BlockSpec((tm, TOP_K), lambda i: (i, 0)),
            pl.BlockSpec((tm, D), lambda i: (i, 0)),
            pl.BlockSpec((None, 1, D), lambda i: (i // steps_per_batch, 0, 0)),
            pl.BlockSpec((1, D), lambda i: (0, 0))],
        out_specs=pl.BlockSpec((tm, D), lambda i: (i, 0)),
        compiler_params=pltpu.CompilerParams(
            dimension_semantics=("arbitrary",), vmem_limit_bytes=VMEM_LIMIT),
        name="combine_rows",
    )(*([yg] * TOP_K), gates, x1.reshape(T, D), gate2, final_g).reshape(B, S, D)


def _rotary_tables(positions):
    inv_freq = jnp.exp(-math.log(ROPE_THETA) * jnp.arange(0, ROT_DIM, 2, dtype=F32) / ROT_DIM)
    ang = positions.astype(F32)[:, :, None] * inv_freq
    cos, sin = jnp.cos(ang), jnp.sin(ang)
    B, S = positions.shape
    pad = ATTN_HEAD_DIM - ROT_DIM
    ct = jnp.concatenate([cos, cos, jnp.ones((B, S, pad), F32)], axis=-1)
    st = jnp.concatenate([-sin, sin, jnp.zeros((B, S, pad), F32)], axis=-1)
    return jnp.concatenate([ct, ct], axis=-1), jnp.concatenate([st, st], axis=-1)


def kernel(x, c, positions, w_ada, b_ada, norm1_g, w_in, hgrn_lb_logits, hgrn_norm_g, attn_norm_g,
           w_out, norm2_g, w_router, b_router, w_gate_up, b_gate_up, w_down, b_down, final_norm_g):
    B, S, D = x.shape
    T = B * S
    assert w_in.shape[0] == 1, "single-layer block: the final norm is fused into the combine step"
    l = 0
    ctab, stab = _rotary_tables(positions)
    lower_bounds = jnp.cumsum(jax.nn.softmax(hgrn_lb_logits.astype(F32), axis=0), axis=0)
    mod = _ada(c, w_ada[l], b_ada[l])
    shift1, scale1, gate1, shift2, scale2, gate2 = jnp.split(mod[:, None, :], N_MOD, axis=-1)
    q, k, lf, v, gt, aq, ak, av, km = _proj(
        x, scale1, shift1, norm1_g[l][None], w_in[l].astype(BF16), lower_bounds[l][None], ctab, stab)
    o_a = _moba(aq, km, ak, av, attn_norm_g[l][None])
    o_h = _hgrn(q, k, lf, v, gt, hgrn_norm_g[l][None])
    x1, h2, top_idx, gates = _mix(o_h, o_a, x, gate1, scale2, shift2, norm2_g[l][None],
                                  w_out[l].astype(BF16), w_router[l], b_router[l][None])
    n_blk = (T * TOP_K) // MOE_ROWS + N_EXPERTS
    tok, dest, blk_expert, n_used = _group_layout(top_idx.reshape(-1), N_EXPERTS, MOE_ROWS, n_blk, TOP_K)
    xs = _sc_gather(h2.reshape(T, D // 2), tok, 128)
    y_sorted = _moe_rows(xs, blk_expert, n_used, w_gate_up[l], b_gate_up[l], w_down[l], b_down[l])
    yg = _sc_scatter(y_sorted, dest, 128)
    return _combine_rows(yg, gates.reshape(T, TOP_K), x1, gate2, final_norm_g[None])
```

```python
import functools
import math

import jax
import jax.numpy as jnp
from jax import lax
from jax.experimental import pallas as pl
from jax.experimental.pallas import tpu as pltpu
from jax.experimental.pallas import tpu_sc as plsc

F32 = jnp.float32
BF16 = jnp.bfloat16
HIGHEST = lax.Precision.HIGHEST

HGRN_DK = 128
HGRN_CHUNK = 64
ATTN_HEADS = 4
ATTN_HEAD_DIM = 64
ROT_DIM = ATTN_HEAD_DIM // 4
ROPE_THETA = 500000.0
MOBA_BLOCK = 256
MOBA_TOPK = 3
N_EXPERTS = 32
TOP_K = 4
SWIGLU_ALPHA = 1.702
SWIGLU_LIMIT = 7.0
N_MOD = 6
RMS_EPS = 1e-6

EXP_CLAMP = 80.0
MOE_ROWS = 256
MOBA_ROWS = 256
MOBA_TILES_PER_STEP = 4
PART_W = 128
VMEM_LIMIT = 56 * 1024 * 1024
SC_CORES = 2
SC_SUBCORES = 16


def _sigmoid(x):
    return 1.0 / (1.0 + jnp.exp(-x))


def _dot(a, b, **kw):
    return jnp.dot(a, b, preferred_element_type=F32, **kw)


def _dot_nt(a, b, **kw):
    return lax.dot_general(a, b, (((1,), (1,)), ((), ())), preferred_element_type=F32, **kw)


def _pack_bf16_pairs(x):
    w = x.shape[1] // 2
    bits = lax.bitcast_convert_type(x.astype(BF16).astype(F32), jnp.int32)
    return bits[:, w:] | lax.shift_right_logical(bits[:, :w], 16)


def _unpack_bf16_pairs(p):
    lo = lax.bitcast_convert_type(lax.shift_left(p, 16), F32)
    hi = lax.bitcast_convert_type(p & jnp.int32(-65536), F32)
    return jnp.concatenate([lo, hi], axis=1)


def _ada_kernel(c_ref, w_ref, b_ref, o_ref):
    c = c_ref[...]
    o_ref[...] = _dot(c * _sigmoid(c), w_ref[...], precision=HIGHEST) + b_ref[...]


def _ada(c, w_ada, b_ada):
    B, D = c.shape
    N = w_ada.shape[1]
    tn = N // 4
    c8 = jnp.zeros((8, D), F32).at[:B].set(c)
    out = pl.pallas_call(
        _ada_kernel,
        out_shape=jax.ShapeDtypeStruct((8, N), F32),
        grid=(N // tn,),
        in_specs=[pl.BlockSpec((8, D), lambda j: (0, 0)),
                  pl.BlockSpec((D, tn), lambda j: (0, j)),
                  pl.BlockSpec((1, tn), lambda j: (0, j))],
        out_specs=pl.BlockSpec((8, tn), lambda j: (0, j)),
        compiler_params=pltpu.CompilerParams(vmem_limit_bytes=VMEM_LIMIT),
        name="ada",
    )(c8, w_ada, b_ada.reshape(1, N))
    return out[:B]


def _proj_kernel(x_ref, sc_ref, sh_ref, g_ref, w_ref, lb_ref, ct_ref, st_ref,
                 q_ref, k_ref, lf_ref, v_ref, gt_ref, aq_ref, ak_ref, av_ref, km_ref,
                 *, hw, aw):
    x = x_ref[...]
    ms = jnp.mean(x * x, axis=-1, keepdims=True)
    h = x * lax.rsqrt(ms + RMS_EPS) * g_ref[...]
    h = h * (1.0 + sc_ref[...]) + sh_ref[...]
    proj = _dot(h.astype(BF16), w_ref[...])

    hq = proj[:, 0:hw]
    hf = proj[:, hw:2 * hw]
    hg = proj[:, 3 * hw:4 * hw]
    q_ref[...] = hq * _sigmoid(hq) * (HGRN_DK ** -0.5)
    lb = lb_ref[...]
    f = lb + (1.0 - lb) * _sigmoid(hf)
    k_ref[...] = 1.0 - f
    lf_ref[...] = jnp.log(f)
    v_ref[...] = proj[:, 2 * hw:3 * hw]
    gt_ref[...] = hg * _sigmoid(hg)

    ct = jnp.concatenate([ct_ref[...]] * (aw // 128), axis=1)
    st = jnp.concatenate([st_ref[...]] * (aw // 128), axis=1)
    lane = lax.broadcasted_iota(jnp.int32, ct.shape, 1) % ATTN_HEAD_DIM
    first_half = lane < (ROT_DIM // 2)

    def rot(t):
        partner = jnp.where(first_half, pltpu.roll(t, aw - ROT_DIM // 2, 1), pltpu.roll(t, ROT_DIM // 2, 1))
        return t * ct + partner * st

    base = 4 * hw
    aq = rot(proj[:, base:base + aw])
    ak = rot(proj[:, base + aw:base + 2 * aw])
    av = proj[:, base + 2 * aw:base + 3 * aw]
    km_ref[...] = jnp.mean(ak, axis=0, keepdims=True)
    lane128 = lax.broadcasted_iota(jnp.int32, (x.shape[0], 128), 1)
    for pair in range(ATTN_HEADS // 2):
        aq_ref[pair] = aq[:, pair * 128:(pair + 1) * 128]
    for hd in range(ATTN_HEADS):
        pair, half = divmod(hd, 2)
        in_head = (lane128 // ATTN_HEAD_DIM) == half
        ak_ref[hd] = jnp.where(in_head, ak[:, pair * 128:(pair + 1) * 128], 0.0).astype(BF16)
        av_ref[hd] = av[:, hd * ATTN_HEAD_DIM:(hd + 1) * ATTN_HEAD_DIM].astype(BF16)


def _proj(x, scale1, shift1, norm_g, w_in_bf16, lb, ctab, stab):
    B, S, D = x.shape
    hw = lb.shape[-1]
    aw = ATTN_HEADS * ATTN_HEAD_DIM
    tm = MOBA_BLOCK
    nb = S // MOBA_BLOCK
    n_proj = w_in_bf16.shape[1]
    row = lambda b, i: (b, i, 0)
    vec = lambda b, i: (b, 0, 0)
    head = lambda b, i: (b, 0, i, 0)
    out_shapes = (
        jax.ShapeDtypeStruct((B, S, hw), F32),
        jax.ShapeDtypeStruct((B, S, hw), F32),
        jax.ShapeDtypeStruct((B, S, hw), F32),
        jax.ShapeDtypeStruct((B, S, hw), F32),
        jax.ShapeDtypeStruct((B, S, hw), F32),
        jax.ShapeDtypeStruct((B, ATTN_HEADS // 2, S, 128), F32),
        jax.ShapeDtypeStruct((B, ATTN_HEADS, S, 128), BF16),
        jax.ShapeDtypeStruct((B, ATTN_HEADS, S, ATTN_HEAD_DIM), BF16),
        jax.ShapeDtypeStruct((B, nb, 1, aw), F32),
    )
    hspec = pl.BlockSpec((None, tm, hw), row)
    aspec = pl.BlockSpec((None, ATTN_HEADS, tm, ATTN_HEAD_DIM), head)
    return pl.pallas_call(
        functools.partial(_proj_kernel, hw=hw, aw=aw),
        out_shape=out_shapes,
        grid=(B, S // tm),
        in_specs=[pl.BlockSpec((None, tm, D), row),
                  pl.BlockSpec((None, 1, D), vec),
                  pl.BlockSpec((None, 1, D), vec),
                  pl.BlockSpec((1, D), lambda b, i: (0, 0)),
                  pl.BlockSpec((D, n_proj), lambda b, i: (0, 0)),
                  pl.BlockSpec((1, hw), lambda b, i: (0, 0)),
                  pl.BlockSpec((None, tm, 128), row),
                  pl.BlockSpec((None, tm, 128), row)],
        out_specs=(hspec, hspec, hspec, hspec, hspec,
                   pl.BlockSpec((None, ATTN_HEADS // 2, tm, 128), head),
                   pl.BlockSpec((None, ATTN_HEADS, tm, 128), head), aspec,
                   pl.BlockSpec((None, None, 1, aw), lambda b, i: (b, i, 0, 0))),
        compiler_params=pltpu.CompilerParams(
            dimension_semantics=("arbitrary", "arbitrary"), vmem_limit_bytes=VMEM_LIMIT),
        name="proj",
    )(x, scale1, shift1, norm_g, w_in_bf16, lb, ctab, stab)


def _hgrn_kernel(q_ref, k_ref, lf_ref, v_ref, gt_ref, gn_ref, o_ref, st_ref, *, n_heads, n_chunks):
    @pl.when(pl.program_id(1) == 0)
    def _():
        st_ref[...] = jnp.zeros_like(st_ref)

    C = HGRN_CHUNK
    r = lax.broadcasted_iota(jnp.int32, (C, C), 0)
    c = lax.broadcasted_iota(jnp.int32, (C, C), 1)
    tril = c <= r
    ltri = tril.astype(F32)
    gn = gn_ref[...]

    def chunk(ci, carry):
        r0 = pl.multiple_of(ci * C, C)
        rows = pl.ds(r0, C)
        b_all = _dot(ltri, lf_ref[rows, :], precision=HIGHEST)
        heads = range(n_heads)
        sls = [slice(hd * HGRN_DK, (hd + 1) * HGRN_DK) for hd in heads]
        bs = [b_all[:, sl] for sl in sls]
        b_lasts = [b[C - 1:C, :] for b in bs]
        qs = [q_ref[rows, sl] for sl in sls]
        ks = [k_ref[rows, sl] for sl in sls]
        vs = [v_ref[rows, sl] for sl in sls]
        states = [st_ref[hd] for hd in heads]
        o_inter = [_dot_nt((qs[hd] * jnp.exp(bs[hd])).astype(BF16), states[hd].astype(BF16)) for hd in heads]
        rhos = [0.5 * bl for bl in b_lasts]
        qas = [(qs[hd] * jnp.exp(jnp.minimum(bs[hd] - rhos[hd], EXP_CLAMP))).astype(BF16) for hd in heads]
        kbs = [(ks[hd] * jnp.exp(jnp.minimum(rhos[hd] - bs[hd], EXP_CLAMP))).astype(BF16) for hd in heads]
        scores = [jnp.where(tril, _dot_nt(qas[hd], kbs[hd]), 0.0).astype(BF16) for hd in heads]
        outs = [o_inter[hd] + _dot(scores[hd], vs[hd].astype(BF16)) for hd in heads]
        kds = [(ks[hd] * jnp.exp(b_lasts[hd] - bs[hd])).astype(BF16) for hd in heads]
        upds = [_dot(vs[hd].T.astype(BF16), kds[hd]) for hd in heads]
        for hd in heads:
            st_ref[hd] = states[hd] * jnp.exp(b_lasts[hd]) + upds[hd]
            o = outs[hd]
            ms = jnp.mean(o * o, axis=-1, keepdims=True)
            o_ref[rows, sls[hd]] = o * lax.rsqrt(ms + RMS_EPS) * gn * gt_ref[rows, sls[hd]]
        return carry

    lax.fori_loop(0, n_chunks, chunk, 0)


def _hgrn(q, k, lf, v, gt, norm_g):
    B, S, hw = q.shape
    n_heads = hw // HGRN_DK
    tc = 512
    spec = pl.BlockSpec((None, tc, hw), lambda b, i: (b, i, 0))
    return pl.pallas_call(
        functools.partial(_hgrn_kernel, n_heads=n_heads, n_chunks=tc // HGRN_CHUNK),
        out_shape=jax.ShapeDtypeStruct((B, S, hw), F32),
        grid=(B, S // tc),
        in_specs=[spec, spec, spec, spec, spec, pl.BlockSpec((1, HGRN_DK), lambda b, i: (0, 0))],
        out_specs=spec,
        scratch_shapes=[pltpu.VMEM((n_heads, HGRN_DK, HGRN_DK), F32)],
        compiler_params=pltpu.CompilerParams(
            dimension_semantics=("arbitrary", "arbitrary"), vmem_limit_bytes=VMEM_LIMIT),
        name="hgrn",
    )(q, k, lf, v, gt, norm_g)


def _sc_gather(table, idx, chunk):
    M = idx.shape[0]
    D = table.shape[1]
    n_workers = SC_CORES * SC_SUBCORES
    per_worker = M // n_workers
    assert per_worker * n_workers == M and per_worker % chunk == 0 and chunk % 8 == 0
    mesh = plsc.VectorSubcoreMesh(core_axis_name="c", subcore_axis_name="s")

    @functools.partial(
        pl.kernel, mesh=mesh,
        out_type=jax.ShapeDtypeStruct((M, D), table.dtype),
        scratch_types=[pltpu.VMEM((chunk,), jnp.int32),
                       pltpu.VMEM((chunk, D), table.dtype),
                       pltpu.SemaphoreType.DMA],
    )
    def gather_kernel(table_hbm, idx_hbm, out_hbm, idx_v, rows_v, sem):
        wid = lax.axis_index("s") * SC_CORES + lax.axis_index("c")
        base = wid * per_worker

        @pl.loop(0, per_worker // chunk)
        def _(j):
            off = pl.multiple_of(base + j * chunk, 8)
            pltpu.sync_copy(idx_hbm.at[pl.ds(off, chunk)], idx_v)
            pltpu.async_copy(table_hbm.at[idx_v], rows_v, sem).wait()
            pltpu.sync_copy(rows_v, out_hbm.at[pl.ds(off, chunk)])

    return gather_kernel(table, idx)


def _sc_scatter(rows, dst, chunk):
    M, D = rows.shape
    n_workers = SC_CORES * SC_SUBCORES
    per_worker = M // n_workers
    assert per_worker * n_workers == M and per_worker % chunk == 0 and chunk % 8 == 0
    mesh = plsc.VectorSubcoreMesh(core_axis_name="c", subcore_axis_name="s")

    @functools.partial(
        pl.kernel, mesh=mesh,
        out_type=jax.ShapeDtypeStruct((M, D), rows.dtype),
        scratch_types=[pltpu.VMEM((chunk,), jnp.int32),
                       pltpu.VMEM((chunk, D), rows.dtype),
                       pltpu.SemaphoreType.DMA],
    )
    def scatter_kernel(rows_hbm, dst_hbm, out_hbm, idx_v, rows_v, sem):
        wid = lax.axis_index("s") * SC_CORES + lax.axis_index("c")
        base = wid * per_worker

        @pl.loop(0, per_worker // chunk)
        def _(j):
            off = pl.multiple_of(base + j * chunk, 8)
            pltpu.sync_copy(dst_hbm.at[pl.ds(off, chunk)], idx_v)
            pltpu.sync_copy(rows_hbm.at[pl.ds(off, chunk)], rows_v)
            pltpu.async_copy(rows_v, out_hbm.at[idx_v], sem).wait()

    return scatter_kernel(rows, dst)


def _group_layout(keys, n_groups, bm, n_tiles, fan):
    n = keys.shape[0]
    n_rows = n_tiles * bm
    n_fill = n_rows - n
    n_hi = n_groups // 32 + 1
    oh_hi = ((keys // 32)[:, None] == jnp.arange(n_hi, dtype=jnp.int32)[None, :]).astype(BF16)
    oh_lo = ((keys % 32)[:, None] == jnp.arange(32, dtype=jnp.int32)[None, :]).astype(BF16)
    counts = jnp.einsum('ah,al->hl', oh_hi, oh_lo, preferred_element_type=F32)
    counts = counts.reshape(-1)[:n_groups].astype(jnp.int32)
    padded = (counts + bm - 1) // bm * bm
    fill_end = jnp.cumsum(padded - counts)
    f_ids = jnp.arange(n_fill, dtype=jnp.int32)
    fill_key = jnp.sum((fill_end[None, :] <= f_ids[:, None]).astype(jnp.int32), axis=1)
    all_keys = jnp.concatenate([keys, fill_key])
    sorted_k, a = lax.sort((all_keys, jnp.arange(n_rows, dtype=jnp.int32)), num_keys=1, is_stable=True)
    tile_group = jnp.minimum(sorted_k[::bm], n_groups - 1)
    real = a < n
    n_items = n // fan
    item = jnp.where(real, a, ((a - n) * 61) % n) // fan
    dest = jnp.where(real, (a % fan) * n_items + a // fan, a)
    n_used = (jnp.sum(padded) // bm).astype(jnp.int32).reshape(1)
    return item, dest, tile_group, n_used


def _null_partial(rows):
    lane = lax.broadcasted_iota(jnp.int32, (rows, PART_W), 1)
    return jnp.where(lane < ATTN_HEAD_DIM, 0.0, -jnp.inf).astype(F32)


def _moba_sel_kernel(q_ref, km_ref, k_ref, v_ref, sel_ref, own_ref, *, n_blocks):
    j = pl.program_id(1)
    T = MOBA_BLOCK
    heads = range(ATTN_HEADS)
    qs = [q_ref[hd // 2] for hd in heads]
    gates = [_dot_nt(qs[hd], km_ref[hd], precision=HIGHEST) for hd in heads]
    lane = lax.broadcasted_iota(jnp.int32, gates[0].shape, 1)
    neg_inf = jnp.float32(-jnp.inf)
    gates = [jnp.where(lane < j, g, neg_inf) for g in gates]
    picks = [[] for _ in heads]
    for _ in range(MOBA_TOPK):
        ms = [jnp.max(g, axis=1, keepdims=True) for g in gates]
        firsts = [jnp.min(jnp.where(g == m, lane, n_blocks), axis=1, keepdims=True) for g, m in zip(gates, ms)]
        for hd in heads:
            picks[hd].append(jnp.where(ms[hd] > neg_inf, firsts[hd], -1))
        gates = [jnp.where(lane == f, neg_inf, g) for g, f in zip(gates, firsts)]
    for hd in heads:
        sel_ref[hd] = jnp.concatenate(picks[hd], axis=1)
    causal = lax.broadcasted_iota(jnp.int32, (T, T), 1) <= lax.broadcasted_iota(jnp.int32, (T, T), 0)
    scale = ATTN_HEAD_DIM ** -0.5
    ss = [jnp.where(causal, _dot_nt((qs[hd] * scale).astype(BF16), k_ref[hd]), neg_inf) for hd in heads]
    mx = [jnp.max(s, axis=1, keepdims=True) for s in ss]
    ps = [jnp.exp(s - m) for s, m in zip(ss, mx)]
    ls = [jnp.sum(p, axis=1, keepdims=True) for p in ps]
    accs = [_dot(ps[hd].astype(BF16), v_ref[hd]) for hd in heads]
    for hd in heads:
        lse = jnp.broadcast_to(mx[hd] + jnp.log(ls[hd]), (T, PART_W - ATTN_HEAD_DIM))
        own_ref[hd] = jnp.concatenate([accs[hd] / ls[hd], lse], axis=1)


def _moba_sel(aq, kmean, ak, av):
    B, H, S, hd = av.shape
    nb = S // MOBA_BLOCK
    T = MOBA_BLOCK
    blk = lambda b, j: (b, 0, j, 0)
    return pl.pallas_call(
        functools.partial(_moba_sel_kernel, n_blocks=nb),
        out_shape=(jax.ShapeDtypeStruct((B, H, S, MOBA_TOPK), jnp.int32),
                   jax.ShapeDtypeStruct((B, H, S, PART_W), F32)),
        grid=(B, nb),
        in_specs=[pl.BlockSpec((None, H // 2, T, 128), blk),
                  pl.BlockSpec((None, H, nb, 128), lambda b, j: (b, 0, 0, 0)),
                  pl.BlockSpec((None, H, T, 128), blk),
                  pl.BlockSpec((None, H, T, hd), blk)],
        out_specs=(pl.BlockSpec((None, H, T, MOBA_TOPK), blk),
                   pl.BlockSpec((None, H, T, PART_W), blk)),
        compiler_params=pltpu.CompilerParams(
            dimension_semantics=("arbitrary", "arbitrary"), vmem_limit_bytes=VMEM_LIMIT),
        name="moba_sel",
    )(aq, kmean, ak, av)


def _moba_blk_kernel(tg_ref, nu_ref, q_ref, *refs):
    n = MOBA_TILES_PER_STEP
    k_refs, v_refs, o_ref = refs[:n], refs[n:2 * n], refs[2 * n]
    R = MOBA_ROWS
    t0 = pl.program_id(0) * n

    @pl.when(t0 < nu_ref[0])
    def _():
        scale = ATTN_HEAD_DIM ** -0.5
        ss = [_dot_nt((q_ref[j * R:(j + 1) * R, :] * scale).astype(BF16), k_refs[j][...]) for j in range(n)]
        ms = [jnp.max(s, axis=1, keepdims=True) for s in ss]
        ps = [jnp.exp(s - m) for s, m in zip(ss, ms)]
        ls = [jnp.sum(p, axis=1, keepdims=True) for p in ps]
        accs = [_dot(p.astype(BF16), v_refs[j][...]) for j, p in enumerate(ps)]
        null = _null_partial(R)
        for j in range(n):
            lse = jnp.broadcast_to(ms[j] + jnp.log(ls[j]), (R, PART_W - ATTN_HEAD_DIM))
            row = jnp.concatenate([accs[j] / ls[j], lse], axis=1)
            o_ref[j * R:(j + 1) * R, :] = jnp.where(t0 + j < nu_ref[0], row, null)

    @pl.when(t0 >= nu_ref[0])
    def _():
        o_ref[...] = _null_partial(n * R)


def _moba_blk(qs, tile_group, n_used, ak, av):
    B, H, S, hd = av.shape
    nb = S // MOBA_BLOCK
    R = MOBA_ROWS
    n = MOBA_TILES_PER_STEP
    n_tiles = qs.shape[0] // R
    assert n_tiles % n == 0
    kv = lambda j: (lambda i, tg, nu: (tg[i * n + j] // nb, tg[i * n + j] % nb, 0, 0))
    grid_spec = pltpu.PrefetchScalarGridSpec(
        num_scalar_prefetch=2,
        grid=(n_tiles // n,),
        in_specs=[pl.BlockSpec((n * R, 128), lambda i, tg, nu: (i, 0))]
        + [pl.BlockSpec((None, None, MOBA_BLOCK, 128), kv(j)) for j in range(n)]
        + [pl.BlockSpec((None, None, MOBA_BLOCK, hd), kv(j)) for j in range(n)],
        out_specs=pl.BlockSpec((n * R, PART_W), lambda i, tg, nu: (i, 0)),
    )
    k4 = ak.reshape(B * H, nb, MOBA_BLOCK, 128)
    v4 = av.reshape(B * H, nb, MOBA_BLOCK, hd)
    return pl.pallas_call(
        _moba_blk_kernel,
        out_shape=jax.ShapeDtypeStruct((n_tiles * R, PART_W), F32),
        grid_spec=grid_spec,
        compiler_params=pltpu.CompilerParams(
            dimension_semantics=("arbitrary",), vmem_limit_bytes=VMEM_LIMIT),
        name="moba_blk",
    )(tile_group, n_used, qs, *([k4] * n), *([v4] * n))


def _moba_merge_kernel(own_ref, pg_ref, g_ref, o_ref):
    hd = ATTN_HEAD_DIM
    rows = [own_ref[...]] + [pg_ref[s] for s in range(MOBA_TOPK)]
    lses = [pltpu.roll(r, hd, 1) for r in rows]
    top = lses[0]
    for z in lses[1:]:
        top = jnp.maximum(top, z)
    num = jnp.zeros_like(top)
    den = jnp.zeros_like(top)
    for r, z in zip(rows, lses):
        w = jnp.exp(z - top)
        num = num + w * r
        den = den + w
    o = (num / den)[:, :hd]
    ms = jnp.mean(o * o, axis=-1, keepdims=True)
    o_ref[...] = o * lax.rsqrt(ms + RMS_EPS) * g_ref[...]


def _moba_merge(own, pg, norm_g):
    n = own.shape[0]
    T = 512
    row = lambda i: (i, 0)
    return pl.pallas_call(
        _moba_merge_kernel,
        out_shape=jax.ShapeDtypeStruct((n, ATTN_HEAD_DIM), F32),
        grid=(n // T,),
        in_specs=[pl.BlockSpec((T, PART_W), row),
                  pl.BlockSpec((MOBA_TOPK, T, PART_W), lambda i: (0, i, 0)),
                  pl.BlockSpec((1, ATTN_HEAD_DIM), lambda i: (0, 0))],
        out_specs=pl.BlockSpec((T, ATTN_HEAD_DIM), row),
        compiler_params=pltpu.CompilerParams(
            dimension_semantics=("arbitrary",), vmem_limit_bytes=VMEM_LIMIT),
        name="moba_merge",
    )(own, pg, norm_g)


def _moba(aq, km, ak, av, norm_g):
    B, H, S, hd = av.shape
    nb = S // MOBA_BLOCK
    n_q = B * H * S
    kmp = km.reshape(B, nb, H // 2, 128)
    half = jnp.arange(128, dtype=jnp.int32) // hd
    kmean = jnp.stack([jnp.where(half == h % 2, kmp[:, :, h // 2, :], 0.0) for h in range(H)], axis=1)
    sel, own = _moba_sel(aq, kmean, ak, av)
    bh = jnp.arange(B * H, dtype=jnp.int32)[:, None, None]
    n_groups = B * H * nb
    keys = jnp.where(sel.reshape(B * H, S, MOBA_TOPK) >= 0, bh * nb + sel.reshape(B * H, S, MOBA_TOPK), n_groups)
    n_tiles = (n_q * MOBA_TOPK) // MOBA_ROWS + n_groups
    q_id, dest, tile_group, n_used = _group_layout(keys.reshape(-1), n_groups, MOBA_ROWS, n_tiles, MOBA_TOPK)
    pair_row = (q_id // (H * S) * (H // 2) + (q_id // S) % H // 2) * S + q_id % S
    qs = _sc_gather(aq.reshape(B * (H // 2) * S, 128), pair_row, 256)
    parts = _moba_blk(qs, tile_group, n_used, ak, av)
    pg = _sc_scatter(parts, dest, 256)
    assert pg.shape[0] % n_q == 0
    o = _moba_merge(own.reshape(n_q, PART_W), pg.reshape(pg.shape[0] // n_q, n_q, PART_W), norm_g)
    return o.reshape(B, H, S, hd)


def _mix_kernel(oh_ref, oa_ref, x_ref, g1_ref, sc2_ref, sh2_ref, n2_ref, wo_ref, wr_ref, br_ref,
                x1_ref, h2_ref, idx_ref, gw_ref):
    cat = jnp.concatenate([oh_ref[...]] + [oa_ref[hd] for hd in range(ATTN_HEADS)], axis=1)
    mix = _dot(cat.astype(BF16), wo_ref[...])
    x1 = x_ref[...] + g1_ref[...] * mix
    x1_ref[...] = x1
    ms = jnp.mean(x1 * x1, axis=-1, keepdims=True)
    h2 = x1 * lax.rsqrt(ms + RMS_EPS) * n2_ref[...]
    h2 = h2 * (1.0 + sc2_ref[...]) + sh2_ref[...]
    h2_ref[...] = _pack_bf16_pairs(h2)
    E = N_EXPERTS
    h_0 = h2.astype(BF16)
    r_1 = h2 - h_0.astype(F32)
    h_1 = r_1.astype(BF16)
    h_2 = (r_1 - h_1.astype(F32)).astype(BF16)
    wr = wr_ref[...]
    p_0 = _dot(h_0, wr)
    p_1 = _dot(h_1, wr[:, :2 * E])
    p_2 = _dot(h_2, wr[:, :E])
    logits = (p_0[:, :E] + (p_0[:, E:2 * E] + p_1[:, :E])
              + (p_0[:, 2 * E:] + p_1[:, E:] + p_2)) + br_ref[...]
    lane = lax.broadcasted_iota(jnp.int32, logits.shape, 1)
    neg_inf = jnp.float32(-jnp.inf)
    vals, idxs = [], []
    for _ in range(TOP_K):
        m = jnp.max(logits, axis=1, keepdims=True)
        first = jnp.min(jnp.where(logits == m, lane, N_EXPERTS), axis=1, keepdims=True)
        vals.append(m)
        idxs.append(first)
        logits = jnp.where(lane == first, neg_inf, logits)
    e = [jnp.exp(v - vals[0]) for v in vals]
    denom = e[0] + e[1] + e[2] + e[3]
    idx_ref[...] = jnp.concatenate(idxs, axis=1)
    gw_ref[...] = jnp.concatenate([ei / denom for ei in e], axis=1)


def _mix(oh, oa, x, gate1, scale2, shift2, norm2_g, w_out_bf16, w_router, b_router):
    B, S, D = x.shape
    hw = oh.shape[-1]
    tm = 256
    row = lambda b, i: (b, i, 0)
    vec = lambda b, i: (b, 0, 0)
    const = lambda b, i: (0, 0)
    return pl.pallas_call(
        _mix_kernel,
        out_shape=(jax.ShapeDtypeStruct((B, S, D), F32),
                   jax.ShapeDtypeStruct((B, S, D // 2), jnp.int32),
                   jax.ShapeDtypeStruct((B, S, TOP_K), jnp.int32),
                   jax.ShapeDtypeStruct((B, S, TOP_K), F32)),
        grid=(B, S // tm),
        in_specs=[pl.BlockSpec((None, tm, hw), row),
                  pl.BlockSpec((None, ATTN_HEADS, tm, ATTN_HEAD_DIM), lambda b, i: (b, 0, i, 0)),
                  pl.BlockSpec((None, tm, D), row),
                  pl.BlockSpec((None, 1, D), vec),
                  pl.BlockSpec((None, 1, D), vec),
                  pl.BlockSpec((None, 1, D), vec),
                  pl.BlockSpec((1, D), const),
                  pl.BlockSpec((D, D), const),
                  pl.BlockSpec((D, 3 * N_EXPERTS), const),
                  pl.BlockSpec((1, N_EXPERTS), const)],
        out_specs=(pl.BlockSpec((None, tm, D), row),
                   pl.BlockSpec((None, tm, D // 2), row),
                   pl.BlockSpec((None, tm, TOP_K), row),
                   pl.BlockSpec((None, tm, TOP_K), row)),
        compiler_params=pltpu.CompilerParams(
            dimension_semantics=("arbitrary", "arbitrary"), vmem_limit_bytes=VMEM_LIMIT),
        name="mix",
    )(oh, oa, x, gate1, scale2, shift2, norm2_g, w_out_bf16, w_router, b_router)


def _moe_rows_kernel(be_ref, nu_ref, x_ref, wgu_ref, bgu_ref, wd_ref, bd_ref, y_ref, wgu16, wd16, *, d_ff):
    i = pl.program_id(0)

    @pl.when((i == 0) | (be_ref[i] != be_ref[jnp.maximum(i - 1, 0)]))
    def _():
        wgu16[...] = wgu_ref[...].astype(BF16)
        wd16[...] = wd_ref[...].astype(BF16)

    @pl.when(i < nu_ref[0])
    def _():
        gu = _dot(_unpack_bf16_pairs(x_ref[...]).astype(BF16), wgu16[...]) + bgu_ref[...]
        gate = jnp.minimum(gu[:, :d_ff], SWIGLU_LIMIT)
        up = jnp.clip(gu[:, d_ff:], -SWIGLU_LIMIT, SWIGLU_LIMIT)
        act = (up + 1.0) * gate * _sigmoid(SWIGLU_ALPHA * gate)
        y_ref[...] = _pack_bf16_pairs(_dot(act.astype(BF16), wd16[...]) + bd_ref[...])

    @pl.when(i >= nu_ref[0])
    def _():
        y_ref[...] = jnp.zeros_like(y_ref)


def _moe_rows(xs, blk_expert, n_used, wgu, bgu, wd, bd):
    D = 2 * xs.shape[1]
    bm = MOE_ROWS
    n_blk = xs.shape[0] // bm
    d_ff = wd.shape[1]
    wsel = lambda i, be, nu: (be[i], 0, 0)
    grid_spec = pltpu.PrefetchScalarGridSpec(
        num_scalar_prefetch=2,
        grid=(n_blk,),
        in_specs=[pl.BlockSpec((bm, D // 2), lambda i, be, nu: (i, 0)),
                  pl.BlockSpec((None, D, 2 * d_ff), wsel),
                  pl.BlockSpec((None, 1, 2 * d_ff), wsel),
                  pl.BlockSpec((None, d_ff, D), wsel),
                  pl.BlockSpec((None, 1, D), wsel)],
        out_specs=pl.BlockSpec((bm, D // 2), lambda i, be, nu: (i, 0)),
        scratch_shapes=[pltpu.VMEM((D, 2 * d_ff), BF16), pltpu.VMEM((d_ff, D), BF16)],
    )
    return pl.pallas_call(
        functools.partial(_moe_rows_kernel, d_ff=d_ff),
        out_shape=jax.ShapeDtypeStruct((n_blk * bm, D // 2), jnp.int32),
        grid_spec=grid_spec,
        compiler_params=pltpu.CompilerParams(
            dimension_semantics=("arbitrary",), vmem_limit_bytes=VMEM_LIMIT),
        name="moe_rows",
    )(blk_expert, n_used, xs, wgu, bgu.reshape(N_EXPERTS, 1, 2 * d_ff), wd, bd.reshape(N_EXPERTS, 1, D))


def _combine_rows_kernel(*refs):
    y_refs = refs[:TOP_K]
    gw_ref, x1_ref, g2_ref, fg_ref, o_ref = refs[TOP_K:]
    gw = gw_ref[...]
    y = gw[:, 0:1] * _unpack_bf16_pairs(y_refs[0][...])
    for kk in range(1, TOP_K):
        y = y + gw[:, kk:kk + 1] * _unpack_bf16_pairs(y_refs[kk][...])
    x2 = x1_ref[...] + g2_ref[...] * y
    ms = jnp.mean(x2 * x2, axis=-1, keepdims=True)
    o_ref[...] = x2 * lax.rsqrt(ms + RMS_EPS) * fg_ref[...]


def _combine_rows(yg, gates, x1, gate2, final_g):
    B, S, D = x1.shape
    T = B * S
    tm = 256
    steps = T // tm
    steps_per_batch = S // tm
    slot_spec = lambda kk: pl.BlockSpec((tm, D // 2), lambda i: (kk * steps + i, 0))
    return pl.pallas_call(
        _combine_rows_kernel,
        out_shape=jax.ShapeDtypeStruct((T, D), F32),
        grid=(steps,),
        in_specs=[slot_spec(kk) for kk in range(TOP_K)] + [
            pl.BlockSpec((tm, TOP_K), lambda i: (i, 0)),
            pl.BlockSpec((tm, D), lambda i: (i, 0)),
            pl.BlockSpec((None, 1, D), lambda i: (i // steps_per_batch, 0, 0)),
            pl.BlockSpec((1, D), lambda i: (0, 0))],
        out_specs=pl.BlockSpec((tm, D), lambda i: (i, 0)),
        compiler_params=pltpu.CompilerParams(
            dimension_semantics=("arbitrary",), vmem_limit_bytes=VMEM_LIMIT),
        name="combine_rows",
    )(*([yg] * TOP_K), gates, x1.reshape(T, D), gate2, final_g).reshape(B, S, D)


def _split_bf16x3(w):
    def top(v):
        return lax.bitcast_convert_type(lax.bitcast_convert_type(v, jnp.int32) & jnp.int32(-65536), F32)
    w0 = top(w)
    w1 = top(w - w0)
    w2 = w - w0 - w1
    return jnp.concatenate([w0, w1, w2], axis=1).astype(BF16)


def _rotary_tables(positions):
    half = ROT_DIM // 2
    inv_freq = jnp.exp(-math.log(ROPE_THETA) * jnp.arange(0, ROT_DIM, 2, dtype=F32) / ROT_DIM)
    d = jnp.arange(128, dtype=jnp.int32) % ATTN_HEAD_DIM
    freq = jnp.where(d < ROT_DIM, inv_freq[d % half], 0.0)
    sign = jnp.where(d < half, -1.0, 1.0)
    ang = positions.astype(F32)[:, :, None] * freq
    return jnp.cos(ang), jnp.sin(ang) * sign


def kernel(x, c, positions, w_ada, b_ada, norm1_g, w_in, hgrn_lb_logits, hgrn_norm_g, attn_norm_g,
           w_out, norm2_g, w_router, b_router, w_gate_up, b_gate_up, w_down, b_down, final_norm_g):
    B, S, D = x.shape
    T = B * S
    assert w_in.shape[0] == 1, "single-layer block: the final norm is fused into the combine step"
    l = 0
    ctab, stab = _rotary_tables(positions)
    lower_bounds = jnp.cumsum(jax.nn.softmax(hgrn_lb_logits.astype(F32), axis=0), axis=0)
    mod = _ada(c, w_ada[l], b_ada[l])
    shift1, scale1, gate1, shift2, scale2, gate2 = jnp.split(mod[:, None, :], N_MOD, axis=-1)
    q, k, lf, v, gt, aq, ak, av, km = _proj(
        x, scale1, shift1, norm1_g[l][None], w_in[l].astype(BF16), lower_bounds[l][None], ctab, stab)
    o_a = _moba(aq, km, ak, av, attn_norm_g[l][None])
    o_h = _hgrn(q, k, lf, v, gt, hgrn_norm_g[l][None])
    x1, h2, top_idx, gates = _mix(o_h, o_a, x, gate1, scale2, shift2, norm2_g[l][None],
                                  w_out[l].astype(BF16), _split_bf16x3(w_router[l]), b_router[l][None])
    n_blk = (T * TOP_K) // MOE_ROWS + N_EXPERTS
    tok, dest, blk_expert, n_used = _group_layout(top_idx.reshape(-1), N_EXPERTS, MOE_ROWS, n_blk, TOP_K)
    xs = _sc_gather(h2.reshape(T, D // 2), tok, 128)
    y_sorted = _moe_rows(xs, blk_expert, n_used, w_gate_up[l], b_gate_up[l], w_down[l], b_down[l])
    yg = _sc_scatter(y_sorted, dest, 128)
    return _combine_rows(yg, gates.reshape(T, TOP_K), x1, gate2, final_norm_g[None])
```

```python
import functools
import math

import jax
import jax.numpy as jnp
from jax import lax
from jax.experimental import pallas as pl
from jax.experimental.pallas import tpu as pltpu
from jax.experimental.pallas import tpu_sc as plsc

F32 = jnp.float32
BF16 = jnp.bfloat16
HIGHEST = lax.Precision.HIGHEST

HGRN_DK = 128
HGRN_CHUNK = 64
ATTN_HEADS = 4
ATTN_HEAD_DIM = 64
ROT_DIM = ATTN_HEAD_DIM // 4
ROPE_THETA = 500000.0
MOBA_BLOCK = 256
MOBA_TOPK = 3
N_EXPERTS = 32
TOP_K = 4
SWIGLU_ALPHA = 1.702
SWIGLU_LIMIT = 7.0
N_MOD = 6
RMS_EPS = 1e-6

EXP_CLAMP = 80.0
MOE_ROWS = 256
MOBA_ROWS = 256
MOBA_TILES_PER_STEP = 8
PART_W = 128
VMEM_LIMIT = 56 * 1024 * 1024
SC_CORES = 2
SC_SUBCORES = 16


def _sigmoid(x):
    return 1.0 / (1.0 + jnp.exp(-x))


def _dot(a, b, **kw):
    return jnp.dot(a, b, preferred_element_type=F32, **kw)


def _dot_nt(a, b, **kw):
    return lax.dot_general(a, b, (((1,), (1,)), ((), ())), preferred_element_type=F32, **kw)


def _pack_bf16_pairs(x):
    w = x.shape[1] // 2
    bits = lax.bitcast_convert_type(x.astype(BF16).astype(F32), jnp.int32)
    return bits[:, w:] | lax.shift_right_logical(bits[:, :w], 16)


def _unpack_bf16_pairs(p):
    lo = lax.bitcast_convert_type(lax.shift_left(p, 16), F32)
    hi = lax.bitcast_convert_type(p & jnp.int32(-65536), F32)
    return jnp.concatenate([lo, hi], axis=1)


def _group_ranks(onehots, carry):
    T = onehots[0].shape[0]
    member = onehots[0]
    for oh in onehots[1:]:
        member = member + oh
    earlier = (lax.broadcasted_iota(jnp.int32, (T, T), 1) < lax.broadcasted_iota(jnp.int32, (T, T), 0))
    base = carry + _dot(earlier.astype(BF16), member.astype(BF16))
    ranks = [jnp.sum(oh * base, axis=1, keepdims=True) for oh in onehots]
    return ranks, carry + jnp.sum(member, axis=0, keepdims=True)


def _ada_kernel(c_ref, w_ref, b_ref, o_ref):
    c = c_ref[...]
    o_ref[...] = _dot(c * _sigmoid(c), w_ref[...], precision=HIGHEST) + b_ref[...]


def _ada(c, w_ada, b_ada):
    B, D = c.shape
    N = w_ada.shape[1]
    tn = N // 4
    c8 = jnp.zeros((8, D), F32).at[:B].set(c)
    out = pl.pallas_call(
        _ada_kernel,
        out_shape=jax.ShapeDtypeStruct((8, N), F32),
        grid=(N // tn,),
        in_specs=[pl.BlockSpec((8, D), lambda j: (0, 0)),
                  pl.BlockSpec((D, tn), lambda j: (0, j)),
                  pl.BlockSpec((1, tn), lambda j: (0, j))],
        out_specs=pl.BlockSpec((8, tn), lambda j: (0, j)),
        compiler_params=pltpu.CompilerParams(vmem_limit_bytes=VMEM_LIMIT),
        name="ada",
    )(c8, w_ada, b_ada.reshape(1, N))
    return out[:B]


def _proj_kernel(x_ref, sc_ref, sh_ref, g_ref, w_ref, lb_ref, ct_ref, st_ref,
                 q_ref, k_ref, lf_ref, v_ref, gt_ref, aq_ref, ak_ref, av_ref, km_ref,
                 *, hw, aw):
    x = x_ref[...]
    ms = jnp.mean(x * x, axis=-1, keepdims=True)
    h = x * lax.rsqrt(ms + RMS_EPS) * g_ref[...]
    h = h * (1.0 + sc_ref[...]) + sh_ref[...]
    proj = _dot(h.astype(BF16), w_ref[...])

    hq = proj[:, 0:hw]
    hf = proj[:, hw:2 * hw]
    hg = proj[:, 3 * hw:4 * hw]
    q_ref[...] = hq * _sigmoid(hq) * (HGRN_DK ** -0.5)
    lb = lb_ref[...]
    f = lb + (1.0 - lb) * _sigmoid(hf)
    k_ref[...] = 1.0 - f
    lf_ref[...] = jnp.log(f)
    v_ref[...] = proj[:, 2 * hw:3 * hw]
    gt_ref[...] = hg * _sigmoid(hg)

    ct = jnp.concatenate([ct_ref[...]] * (aw // 128), axis=1)
    st = jnp.concatenate([st_ref[...]] * (aw // 128), axis=1)
    lane = lax.broadcasted_iota(jnp.int32, ct.shape, 1) % ATTN_HEAD_DIM
    first_half = lane < (ROT_DIM // 2)

    def rot(t):
        partner = jnp.where(first_half, pltpu.roll(t, aw - ROT_DIM // 2, 1), pltpu.roll(t, ROT_DIM // 2, 1))
        return t * ct + partner * st

    base = 4 * hw
    aq = rot(proj[:, base:base + aw])
    ak = rot(proj[:, base + aw:base + 2 * aw])
    av = proj[:, base + 2 * aw:base + 3 * aw]
    km_ref[...] = jnp.mean(ak, axis=0, keepdims=True)
    lane128 = lax.broadcasted_iota(jnp.int32, (x.shape[0], 128), 1)
    for pair in range(ATTN_HEADS // 2):
        aq_ref[pair] = aq[:, pair * 128:(pair + 1) * 128]
    for hd in range(ATTN_HEADS):
        pair, half = divmod(hd, 2)
        in_head = (lane128 // ATTN_HEAD_DIM) == half
        ak_ref[hd] = jnp.where(in_head, ak[:, pair * 128:(pair + 1) * 128], 0.0).astype(BF16)
        av_ref[hd] = av[:, hd * ATTN_HEAD_DIM:(hd + 1) * ATTN_HEAD_DIM].astype(BF16)


def _proj(x, scale1, shift1, norm_g, w_in_bf16, lb, ctab, stab):
    B, S, D = x.shape
    hw = lb.shape[-1]
    aw = ATTN_HEADS * ATTN_HEAD_DIM
    tm = MOBA_BLOCK
    nb = S // MOBA_BLOCK
    n_proj = w_in_bf16.shape[1]
    row = lambda b, i: (b, i, 0)
    vec = lambda b, i: (b, 0, 0)
    head = lambda b, i: (b, 0, i, 0)
    out_shapes = (
        jax.ShapeDtypeStruct((B, S, hw), F32),
        jax.ShapeDtypeStruct((B, S, hw), F32),
        jax.ShapeDtypeStruct((B, S, hw), F32),
        jax.ShapeDtypeStruct((B, S, hw), F32),
        jax.ShapeDtypeStruct((B, S, hw), F32),
        jax.ShapeDtypeStruct((B, ATTN_HEADS // 2, S, 128), F32),
        jax.ShapeDtypeStruct((B, ATTN_HEADS, S, 128), BF16),
        jax.ShapeDtypeStruct((B, ATTN_HEADS, S, ATTN_HEAD_DIM), BF16),
        jax.ShapeDtypeStruct((B, nb, 1, aw), F32),
    )
    hspec = pl.BlockSpec((None, tm, hw), row)
    aspec = pl.BlockSpec((None, ATTN_HEADS, tm, ATTN_HEAD_DIM), head)
    return pl.pallas_call(
        functools.partial(_proj_kernel, hw=hw, aw=aw),
        out_shape=out_shapes,
        grid=(B, S // tm),
        in_specs=[pl.BlockSpec((None, tm, D), row),
                  pl.BlockSpec((None, 1, D), vec),
                  pl.BlockSpec((None, 1, D), vec),
                  pl.BlockSpec((1, D), lambda b, i: (0, 0)),
                  pl.BlockSpec((D, n_proj), lambda b, i: (0, 0)),
                  pl.BlockSpec((1, hw), lambda b, i: (0, 0)),
                  pl.BlockSpec((None, tm, 128), row),
                  pl.BlockSpec((None, tm, 128), row)],
        out_specs=(hspec, hspec, hspec, hspec, hspec,
                   pl.BlockSpec((None, ATTN_HEADS // 2, tm, 128), head),
                   pl.BlockSpec((None, ATTN_HEADS, tm, 128), head), aspec,
                   pl.BlockSpec((None, None, 1, aw), lambda b, i: (b, i, 0, 0))),
        compiler_params=pltpu.CompilerParams(
            dimension_semantics=("arbitrary", "arbitrary"), vmem_limit_bytes=VMEM_LIMIT),
        name="proj",
    )(x, scale1, shift1, norm_g, w_in_bf16, lb, ctab, stab)


def _hgrn_kernel(q_ref, k_ref, lf_ref, v_ref, gt_ref, gn_ref, o_ref, st_ref, *, n_heads, n_chunks):
    @pl.when(pl.program_id(1) == 0)
    def _():
        st_ref[...] = jnp.zeros_like(st_ref)

    C = HGRN_CHUNK
    r = lax.broadcasted_iota(jnp.int32, (C, C), 0)
    c = lax.broadcasted_iota(jnp.int32, (C, C), 1)
    tril = c <= r
    ltri = tril.astype(F32)
    gn = gn_ref[...]

    def chunk(ci, carry):
        r0 = pl.multiple_of(ci * C, C)
        rows = pl.ds(r0, C)
        b_all = _dot(ltri, lf_ref[rows, :], precision=HIGHEST)
        heads = range(n_heads)
        sls = [slice(hd * HGRN_DK, (hd + 1) * HGRN_DK) for hd in heads]
        bs = [b_all[:, sl] for sl in sls]
        b_lasts = [b[C - 1:C, :] for b in bs]
        qs = [q_ref[rows, sl] for sl in sls]
        ks = [k_ref[rows, sl] for sl in sls]
        vs = [v_ref[rows, sl] for sl in sls]
        states = [st_ref[hd] for hd in heads]
        o_inter = [_dot_nt((qs[hd] * jnp.exp(bs[hd])).astype(BF16), states[hd].astype(BF16)) for hd in heads]
        rhos = [0.5 * bl for bl in b_lasts]
        qas = [(qs[hd] * jnp.exp(jnp.minimum(bs[hd] - rhos[hd], EXP_CLAMP))).astype(BF16) for hd in heads]
        kbs = [(ks[hd] * jnp.exp(jnp.minimum(rhos[hd] - bs[hd], EXP_CLAMP))).astype(BF16) for hd in heads]
        scores = [jnp.where(tril, _dot_nt(qas[hd], kbs[hd]), 0.0).astype(BF16) for hd in heads]
        outs = [o_inter[hd] + _dot(scores[hd], vs[hd].astype(BF16)) for hd in heads]
        kds = [(ks[hd] * jnp.exp(b_lasts[hd] - bs[hd])).astype(BF16) for hd in heads]
        upds = [_dot(vs[hd].T.astype(BF16), kds[hd]) for hd in heads]
        for hd in heads:
            st_ref[hd] = states[hd] * jnp.exp(b_lasts[hd]) + upds[hd]
            o = outs[hd]
            ms = jnp.mean(o * o, axis=-1, keepdims=True)
            o_ref[rows, sls[hd]] = o * lax.rsqrt(ms + RMS_EPS) * gn * gt_ref[rows, sls[hd]]
        return carry

    lax.fori_loop(0, n_chunks, chunk, 0)


def _hgrn(q, k, lf, v, gt, norm_g):
    B, S, hw = q.shape
    n_heads = hw // HGRN_DK
    tc = 512
    spec = pl.BlockSpec((None, tc, hw), lambda b, i: (b, i, 0))
    return pl.pallas_call(
        functools.partial(_hgrn_kernel, n_heads=n_heads, n_chunks=tc // HGRN_CHUNK),
        out_shape=jax.ShapeDtypeStruct((B, S, hw), F32),
        grid=(B, S // tc),
        in_specs=[spec, spec, spec, spec, spec, pl.BlockSpec((1, HGRN_DK), lambda b, i: (0, 0))],
        out_specs=spec,
        scratch_shapes=[pltpu.VMEM((n_heads, HGRN_DK, HGRN_DK), F32)],
        compiler_params=pltpu.CompilerParams(
            dimension_semantics=("arbitrary", "arbitrary"), vmem_limit_bytes=VMEM_LIMIT),
        name="hgrn",
    )(q, k, lf, v, gt, norm_g)


def _sc_gather(table, idx, chunk):
    M = idx.shape[0]
    D = table.shape[1]
    n_workers = SC_CORES * SC_SUBCORES
    per_worker = M // n_workers
    assert per_worker * n_workers == M and per_worker % chunk == 0 and chunk % 8 == 0
    mesh = plsc.VectorSubcoreMesh(core_axis_name="c", subcore_axis_name="s")

    @functools.partial(
        pl.kernel, mesh=mesh,
        out_type=jax.ShapeDtypeStruct((M, D), table.dtype),
        scratch_types=[pltpu.VMEM((chunk,), jnp.int32),
                       pltpu.VMEM((chunk, D), table.dtype),
                       pltpu.SemaphoreType.DMA],
    )
    def gather_kernel(table_hbm, idx_hbm, out_hbm, idx_v, rows_v, sem):
        wid = lax.axis_index("s") * SC_CORES + lax.axis_index("c")
        base = wid * per_worker

        @pl.loop(0, per_worker // chunk)
        def _(j):
            off = pl.multiple_of(base + j * chunk, 8)
            pltpu.sync_copy(idx_hbm.at[pl.ds(off, chunk)], idx_v)
            pltpu.async_copy(table_hbm.at[idx_v], rows_v, sem).wait()
            pltpu.sync_copy(rows_v, out_hbm.at[pl.ds(off, chunk)])

    return gather_kernel(table, idx)


def _sc_permute(table, src, dst, n_out, chunk):
    M = src.shape[0]
    D = table.shape[1]
    n_workers = SC_CORES * SC_SUBCORES
    per_worker = M // n_workers
    assert per_worker * n_workers == M and per_worker % chunk == 0 and chunk % 8 == 0
    mesh = plsc.VectorSubcoreMesh(core_axis_name="c", subcore_axis_name="s")

    @functools.partial(
        pl.kernel, mesh=mesh,
        out_type=jax.ShapeDtypeStruct((n_out, D), table.dtype),
        scratch_types=[pltpu.VMEM((chunk,), jnp.int32),
                       pltpu.VMEM((chunk,), jnp.int32),
                       pltpu.VMEM((chunk, D), table.dtype),
                       pltpu.SemaphoreType.DMA],
    )
    def permute_kernel(table_hbm, src_hbm, dst_hbm, out_hbm, src_v, dst_v, rows_v, sem):
        wid = lax.axis_index("s") * SC_CORES + lax.axis_index("c")
        base = wid * per_worker

        @pl.loop(0, per_worker // chunk)
        def _(j):
            off = pl.multiple_of(base + j * chunk, 8)
            pltpu.sync_copy(src_hbm.at[pl.ds(off, chunk)], src_v)
            pltpu.sync_copy(dst_hbm.at[pl.ds(off, chunk)], dst_v)
            pltpu.async_copy(table_hbm.at[src_v], rows_v, sem).wait()
            pltpu.async_copy(rows_v, out_hbm.at[dst_v], sem).wait()

    return permute_kernel(table, src, dst)


def _tile_layout(counts, bm, n_tiles):
    n_groups = counts.shape[0]
    padded = (counts + bm - 1) // bm * bm
    pad_end = jnp.cumsum(padded)
    tile_start = jnp.arange(n_tiles, dtype=jnp.int32) * bm
    tile_group = jnp.minimum(
        jnp.sum((pad_end[None, :] <= tile_start[:, None]).astype(jnp.int32), axis=1), n_groups - 1)
    n_used = (pad_end[-1] // bm).astype(jnp.int32).reshape(1)
    return pad_end - padded, tile_group.astype(jnp.int32), n_used


def _null_partial(rows):
    lane = lax.broadcasted_iota(jnp.int32, (rows, PART_W), 1)
    return jnp.where(lane < ATTN_HEAD_DIM, 0.0, -jnp.inf).astype(F32)


def _moba_sel_kernel(q_ref, km_ref, k_ref, v_ref, sel_ref, rank_ref, cnt_ref, own_ref, cnt_acc, *, n_blocks):
    j = pl.program_id(1)
    T = MOBA_BLOCK
    heads = range(ATTN_HEADS)
    qs = [q_ref[hd // 2] for hd in heads]
    gates = [_dot_nt(qs[hd], km_ref[hd], precision=HIGHEST) for hd in heads]
    lane = lax.broadcasted_iota(jnp.int32, gates[0].shape, 1)
    neg_inf = jnp.float32(-jnp.inf)
    gates = [jnp.where(lane < j, g, neg_inf) for g in gates]
    picks = [[] for _ in heads]
    for _ in range(MOBA_TOPK):
        ms = [jnp.max(g, axis=1, keepdims=True) for g in gates]
        firsts = [jnp.min(jnp.where(g == m, lane, n_blocks), axis=1, keepdims=True) for g, m in zip(gates, ms)]
        for hd in heads:
            picks[hd].append(jnp.where(ms[hd] > neg_inf, firsts[hd], -1))
        gates = [jnp.where(lane == f, neg_inf, g) for g, f in zip(gates, firsts)]

    @pl.when(j == 0)
    def _():
        cnt_acc[...] = jnp.zeros_like(cnt_acc)

    for hd in heads:
        sel_ref[hd] = jnp.concatenate(picks[hd], axis=1)
        ranks, total = _group_ranks([(lane == p).astype(F32) for p in picks[hd]], cnt_acc[hd])
        rank_ref[hd] = jnp.concatenate(ranks, axis=1).astype(jnp.int32)
        cnt_acc[hd] = total
        cnt_ref[hd] = total.astype(jnp.int32)
    causal = lax.broadcasted_iota(jnp.int32, (T, T), 1) <= lax.broadcasted_iota(jnp.int32, (T, T), 0)
    scale = ATTN_HEAD_DIM ** -0.5
    ss = [jnp.where(causal, _dot_nt((qs[hd] * scale).astype(BF16), k_ref[hd]), neg_inf) for hd in heads]
    mx = [jnp.max(s, axis=1, keepdims=True) for s in ss]
    ps = [jnp.exp(s - m) for s, m in zip(ss, mx)]
    ls = [jnp.sum(p, axis=1, keepdims=True) for p in ps]
    accs = [_dot(ps[hd].astype(BF16), v_ref[hd]) for hd in heads]
    for hd in heads:
        lse = jnp.broadcast_to(mx[hd] + jnp.log(ls[hd]), (T, PART_W - ATTN_HEAD_DIM))
        own_ref[hd] = jnp.concatenate([accs[hd] / ls[hd], lse], axis=1)


def _moba_sel(aq, kmean, ak, av):
    B, H, S, hd = av.shape
    nb = S // MOBA_BLOCK
    T = MOBA_BLOCK
    blk = lambda b, j: (b, 0, j, 0)
    return pl.pallas_call(
        functools.partial(_moba_sel_kernel, n_blocks=nb),
        out_shape=(jax.ShapeDtypeStruct((B, H, S, MOBA_TOPK), jnp.int32),
                   jax.ShapeDtypeStruct((B, H, S, MOBA_TOPK), jnp.int32),
                   jax.ShapeDtypeStruct((B, H, 1, nb), jnp.int32),
                   jax.ShapeDtypeStruct((B, H, S, PART_W), F32)),
        grid=(B, nb),
        in_specs=[pl.BlockSpec((None, H // 2, T, 128), blk),
                  pl.BlockSpec((None, H, nb, 128), lambda b, j: (b, 0, 0, 0)),
                  pl.BlockSpec((None, H, T, 128), blk),
                  pl.BlockSpec((None, H, T, hd), blk)],
        out_specs=(pl.BlockSpec((None, H, T, MOBA_TOPK), blk),
                   pl.BlockSpec((None, H, T, MOBA_TOPK), blk),
                   pl.BlockSpec((None, H, 1, nb), lambda b, j: (b, 0, 0, 0)),
                   pl.BlockSpec((None, H, T, PART_W), blk)),
        scratch_shapes=[pltpu.VMEM((H, 1, nb), F32)],
        compiler_params=pltpu.CompilerParams(
            dimension_semantics=("arbitrary", "arbitrary"), vmem_limit_bytes=VMEM_LIMIT),
        name="moba_sel",
    )(aq, kmean, ak, av)


def _moba_blk_kernel(tg_ref, nu_ref, q_ref, *refs):
    n = MOBA_TILES_PER_STEP
    k_refs, v_refs, o_ref = refs[:n], refs[n:2 * n], refs[2 * n]
    R = MOBA_ROWS
    t0 = pl.program_id(0) * n

    @pl.when(t0 < nu_ref[0])
    def _():
        scale = ATTN_HEAD_DIM ** -0.5
        ss = [_dot_nt((q_ref[j * R:(j + 1) * R, :] * scale).astype(BF16), k_refs[j][...]) for j in range(n)]
        ms = [jnp.max(s, axis=1, keepdims=True) for s in ss]
        ps = [jnp.exp(s - m) for s, m in zip(ss, ms)]
        ls = [jnp.sum(p, axis=1, keepdims=True) for p in ps]
        accs = [_dot(p.astype(BF16), v_refs[j][...]) for j, p in enumerate(ps)]
        null = _null_partial(R)
        for j in range(n):
            lse = jnp.broadcast_to(ms[j] + jnp.log(ls[j]), (R, PART_W - ATTN_HEAD_DIM))
            row = jnp.concatenate([accs[j] / ls[j], lse], axis=1)
            o_ref[j * R:(j + 1) * R, :] = jnp.where(t0 + j < nu_ref[0], row, null)

    @pl.when(t0 >= nu_ref[0])
    def _():
        o_ref[...] = _null_partial(n * R)


def _moba_blk(qs, tile_group, n_used, ak, av):
    B, H, S, hd = av.shape
    nb = S // MOBA_BLOCK
    R = MOBA_ROWS
    n = MOBA_TILES_PER_STEP
    n_tiles = qs.shape[0] // R
    assert n_tiles % n == 0
    kv = lambda j: (lambda i, tg, nu: (tg[i * n + j] // nb, tg[i * n + j] % nb, 0, 0))
    grid_spec = pltpu.PrefetchScalarGridSpec(
        num_scalar_prefetch=2,
        grid=(n_tiles // n,),
        in_specs=[pl.BlockSpec((n * R, 128), lambda i, tg, nu: (i, 0))]
        + [pl.BlockSpec((None, None, MOBA_BLOCK, 128), kv(j)) for j in range(n)]
        + [pl.BlockSpec((None, None, MOBA_BLOCK, hd), kv(j)) for j in range(n)],
        out_specs=pl.BlockSpec((n * R, PART_W), lambda i, tg, nu: (i, 0)),
    )
    k4 = ak.reshape(B * H, nb, MOBA_BLOCK, 128)
    v4 = av.reshape(B * H, nb, MOBA_BLOCK, hd)
    return pl.pallas_call(
        _moba_blk_kernel,
        out_shape=jax.ShapeDtypeStruct((n_tiles * R, PART_W), F32),
        grid_spec=grid_spec,
        compiler_params=pltpu.CompilerParams(
            dimension_semantics=("arbitrary",), vmem_limit_bytes=VMEM_LIMIT),
        name="moba_blk",
    )(tile_group, n_used, qs, *([k4] * n), *([v4] * n))


def _moba_merge_kernel(own_ref, pg_ref, g_ref, o_ref):
    hd = ATTN_HEAD_DIM
    rows = [own_ref[...]] + [pg_ref[s] for s in range(MOBA_TOPK)]
    lses = [pltpu.roll(r, hd, 1) for r in rows]
    top = lses[0]
    for z in lses[1:]:
        top = jnp.maximum(top, z)
    num = jnp.zeros_like(top)
    den = jnp.zeros_like(top)
    for r, z in zip(rows, lses):
        w = jnp.exp(z - top)
        num = num + w * r
        den = den + w
    o = (num / den)[:, :hd]
    ms = jnp.mean(o * o, axis=-1, keepdims=True)
    o_ref[...] = o * lax.rsqrt(ms + RMS_EPS) * g_ref[...]


def _moba_merge(own, pg, norm_g):
    n = own.shape[0]
    T = 512
    row = lambda i: (i, 0)
    return pl.pallas_call(
        _moba_merge_kernel,
        out_shape=jax.ShapeDtypeStruct((n, ATTN_HEAD_DIM), F32),
        grid=(n // T,),
        in_specs=[pl.BlockSpec((T, PART_W), row),
                  pl.BlockSpec((MOBA_TOPK, T, PART_W), lambda i: (0, i, 0)),
                  pl.BlockSpec((1, ATTN_HEAD_DIM), lambda i: (0, 0))],
        out_specs=pl.BlockSpec((T, ATTN_HEAD_DIM), row),
        compiler_params=pltpu.CompilerParams(
            dimension_semantics=("arbitrary",), vmem_limit_bytes=VMEM_LIMIT),
        name="moba_merge",
    )(own, pg, norm_g)


def _moba(aq, km, ak, av, norm_g):
    B, H, S, hd = av.shape
    nb = S // MOBA_BLOCK
    n_q = B * H * S
    kmp = km.reshape(B, nb, H // 2, 128)
    half = jnp.arange(128, dtype=jnp.int32) // hd
    kmean = jnp.stack([jnp.where(half == h % 2, kmp[:, :, h // 2, :], 0.0) for h in range(H)], axis=1)
    sel, rank, counts, own = _moba_sel(aq, kmean, ak, av)
    n_groups = B * H * nb
    n_tiles = (n_q * MOBA_TOPK) // MOBA_ROWS + n_groups
    pad_start, tile_group, n_used = _tile_layout(counts.reshape(-1), MOBA_ROWS, n_tiles)
    sel3 = sel.reshape(B * H, S, MOBA_TOPK)
    blocks = jnp.arange(nb, dtype=jnp.int32)
    start = jnp.sum(jnp.where(sel3[..., None] == blocks, pad_start.reshape(B * H, 1, 1, nb), 0), axis=-1)
    a_ids = jnp.arange(n_q * MOBA_TOPK, dtype=jnp.int32).reshape(B * H, S, MOBA_TOPK)
    assert n_tiles * MOBA_ROWS >= n_q * MOBA_TOPK + MOBA_ROWS
    pos = jnp.where(sel3 >= 0, start + rank.reshape(B * H, S, MOBA_TOPK), n_used[0] * MOBA_ROWS + a_ids % MOBA_ROWS)
    q_id = a_ids.reshape(-1) // MOBA_TOPK
    pair_row = (q_id // (H * S) * (H // 2) + (q_id // S) % H // 2) * S + q_id % S
    qs = _sc_permute(aq.reshape(B * (H // 2) * S, 128), pair_row, pos.reshape(-1), n_tiles * MOBA_ROWS, 256)
    parts = _moba_blk(qs, tile_group, n_used, ak, av)
    pg = _sc_gather(parts, pos.reshape(n_q, MOBA_TOPK).T.reshape(-1), 256)
    o = _moba_merge(own.reshape(n_q, PART_W), pg.reshape(MOBA_TOPK, n_q, PART_W), norm_g)
    return o.reshape(B, H, S, hd)


def _mix_kernel(oh_ref, oa_ref, x_ref, g1_ref, sc2_ref, sh2_ref, n2_ref, wo_ref, wr_ref, br_ref,
                x1_ref, h2_ref, idx_ref, gw_ref, rank_ref, cnt_ref, cnt_acc):
    cat = jnp.concatenate([oh_ref[...]] + [oa_ref[hd] for hd in range(ATTN_HEADS)], axis=1)
    mix = _dot(cat.astype(BF16), wo_ref[...])
    x1 = x_ref[...] + g1_ref[...] * mix
    x1_ref[...] = x1
    ms = jnp.mean(x1 * x1, axis=-1, keepdims=True)
    h2 = x1 * lax.rsqrt(ms + RMS_EPS) * n2_ref[...]
    h2 = h2 * (1.0 + sc2_ref[...]) + sh2_ref[...]
    h2_ref[...] = _pack_bf16_pairs(h2)
    E = N_EXPERTS
    h_0 = h2.astype(BF16)
    r_1 = h2 - h_0.astype(F32)
    h_1 = r_1.astype(BF16)
    h_2 = (r_1 - h_1.astype(F32)).astype(BF16)
    wr = wr_ref[...]
    p_0 = _dot(h_0, wr)
    p_1 = _dot(h_1, wr[:, :2 * E])
    p_2 = _dot(h_2, wr[:, :E])
    logits = (p_0[:, :E] + (p_0[:, E:2 * E] + p_1[:, :E])
              + (p_0[:, 2 * E:] + p_1[:, E:] + p_2)) + br_ref[...]
    lane = lax.broadcasted_iota(jnp.int32, logits.shape, 1)
    neg_inf = jnp.float32(-jnp.inf)
    vals, idxs = [], []
    for _ in range(TOP_K):
        m = jnp.max(logits, axis=1, keepdims=True)
        first = jnp.min(jnp.where(logits == m, lane, N_EXPERTS), axis=1, keepdims=True)
        vals.append(m)
        idxs.append(first)
        logits = jnp.where(lane == first, neg_inf, logits)
    e = [jnp.exp(v - vals[0]) for v in vals]
    denom = e[0] + e[1] + e[2] + e[3]
    idx_ref[...] = jnp.concatenate(idxs, axis=1)
    gw_ref[...] = jnp.concatenate([ei / denom for ei in e], axis=1)

    @pl.when((pl.program_id(0) == 0) & (pl.program_id(1) == 0))
    def _():
        cnt_acc[...] = jnp.zeros_like(cnt_acc)

    ranks, total = _group_ranks([(lane == ix).astype(F32) for ix in idxs], cnt_acc[...])
    rank_ref[...] = jnp.concatenate(ranks, axis=1).astype(jnp.int32)
    cnt_acc[...] = total
    cnt_ref[...] = total.astype(jnp.int32)


def _mix(oh, oa, x, gate1, scale2, shift2, norm2_g, w_out_bf16, w_router, b_router):
    B, S, D = x.shape
    hw = oh.shape[-1]
    tm = 256
    row = lambda b, i: (b, i, 0)
    vec = lambda b, i: (b, 0, 0)
    const = lambda b, i: (0, 0)
    return pl.pallas_call(
        _mix_kernel,
        out_shape=(jax.ShapeDtypeStruct((B, S, D), F32),
                   jax.ShapeDtypeStruct((B, S, D // 2), jnp.int32),
                   jax.ShapeDtypeStruct((B, S, TOP_K), jnp.int32),
                   jax.ShapeDtypeStruct((B, S, TOP_K), F32),
                   jax.ShapeDtypeStruct((B, S, TOP_K), jnp.int32),
                   jax.ShapeDtypeStruct((1, N_EXPERTS), jnp.int32)),
        grid=(B, S // tm),
        in_specs=[pl.BlockSpec((None, tm, hw), row),
                  pl.BlockSpec((None, ATTN_HEADS, tm, ATTN_HEAD_DIM), lambda b, i: (b, 0, i, 0)),
                  pl.BlockSpec((None, tm, D), row),
                  pl.BlockSpec((None, 1, D), vec),
                  pl.BlockSpec((None, 1, D), vec),
                  pl.BlockSpec((None, 1, D), vec),
                  pl.BlockSpec((1, D), const),
                  pl.BlockSpec((D, D), const),
                  pl.BlockSpec((D, 3 * N_EXPERTS), const),
                  pl.BlockSpec((1, N_EXPERTS), const)],
        out_specs=(pl.BlockSpec((None, tm, D), row),
                   pl.BlockSpec((None, tm, D // 2), row),
                   pl.BlockSpec((None, tm, TOP_K), row),
                   pl.BlockSpec((None, tm, TOP_K), row),
                   pl.BlockSpec((None, tm, TOP_K), row),
                   pl.BlockSpec((1, N_EXPERTS), const)),
        scratch_shapes=[pltpu.VMEM((1, N_EXPERTS), F32)],
        compiler_params=pltpu.CompilerParams(
            dimension_semantics=("arbitrary", "arbitrary"), vmem_limit_bytes=VMEM_LIMIT),
        name="mix",
    )(oh, oa, x, gate1, scale2, shift2, norm2_g, w_out_bf16, w_router, b_router)


def _moe_rows_kernel(be_ref, nu_ref, x_ref, wgu_ref, bgu_ref, wd_ref, bd_ref, y_ref, wgu16, wd16, *, d_ff):
    i = pl.program_id(0)

    @pl.when((i == 0) | (be_ref[i] != be_ref[jnp.maximum(i - 1, 0)]))
    def _():
        wgu16[...] = wgu_ref[...].astype(BF16)
        wd16[...] = wd_ref[...].astype(BF16)

    @pl.when(i < nu_ref[0])
    def _():
        gu = _dot(_unpack_bf16_pairs(x_ref[...]).astype(BF16), wgu16[...]) + bgu_ref[...]
        gate = jnp.minimum(gu[:, :d_ff], SWIGLU_LIMIT)
        up = jnp.clip(gu[:, d_ff:], -SWIGLU_LIMIT, SWIGLU_LIMIT)
        act = (up + 1.0) * gate * _sigmoid(SWIGLU_ALPHA * gate)
        y_ref[...] = _pack_bf16_pairs(_dot(act.astype(BF16), wd16[...]) + bd_ref[...])

    @pl.when(i >= nu_ref[0])
    def _():
        y_ref[...] = jnp.zeros_like(y_ref)


def _moe_rows(xs, blk_expert, n_used, wgu, bgu, wd, bd):
    D = 2 * xs.shape[1]
    bm = MOE_ROWS
    n_blk = xs.shape[0] // bm
    d_ff = wd.shape[1]
    wsel = lambda i, be, nu: (be[i], 0, 0)
    grid_spec = pltpu.PrefetchScalarGridSpec(
        num_scalar_prefetch=2,
        grid=(n_blk,),
        in_specs=[pl.BlockSpec((bm, D // 2), lambda i, be, nu: (i, 0)),
                  pl.BlockSpec((None, D, 2 * d_ff), wsel),
                  pl.BlockSpec((None, 1, 2 * d_ff), wsel),
                  pl.BlockSpec((None, d_ff, D), wsel),
                  pl.BlockSpec((None, 1, D), wsel)],
        out_specs=pl.BlockSpec((bm, D // 2), lambda i, be, nu: (i, 0)),
        scratch_shapes=[pltpu.VMEM((D, 2 * d_ff), BF16), pltpu.VMEM((d_ff, D), BF16)],
    )
    return pl.pallas_call(
        functools.partial(_moe_rows_kernel, d_ff=d_ff),
        out_shape=jax.ShapeDtypeStruct((n_blk * bm, D // 2), jnp.int32),
        grid_spec=grid_spec,
        compiler_params=pltpu.CompilerParams(
            dimension_semantics=("arbitrary",), vmem_limit_bytes=VMEM_LIMIT),
        name="moe_rows",
    )(blk_expert, n_used, xs, wgu, bgu.reshape(N_EXPERTS, 1, 2 * d_ff), wd, bd.reshape(N_EXPERTS, 1, D))


def _combine_rows_kernel(*refs):
    y_refs = refs[:TOP_K]
    gw_ref, x1_ref, g2_ref, fg_ref, o_ref = refs[TOP_K:]
    gw = gw_ref[...]
    y = gw[:, 0:1] * _unpack_bf16_pairs(y_refs[0][...])
    for kk in range(1, TOP_K):
        y = y + gw[:, kk:kk + 1] * _unpack_bf16_pairs(y_refs[kk][...])
    x2 = x1_ref[...] + g2_ref[...] * y
    ms = jnp.mean(x2 * x2, axis=-1, keepdims=True)
    o_ref[...] = x2 * lax.rsqrt(ms + RMS_EPS) * fg_ref[...]


def _combine_rows(yg, gates, x1, gate2, final_g):
    B, S, D = x1.shape
    T = B * S
    tm = 256
    steps = T // tm
    steps_per_batch = S // tm
    slot_spec = lambda kk: pl.BlockSpec((tm, D // 2), lambda i: (kk * steps + i, 0))
    return pl.pallas_call(
        _combine_rows_kernel,
        out_shape=jax.ShapeDtypeStruct((T, D), F32),
        grid=(steps,),
        in_specs=[slot_spec(kk) for kk in range(TOP_K)] + [
            pl.BlockSpec((tm, TOP_K), lambda i: (i, 0)),
            pl.BlockSpec((tm, D), lambda i: (i, 0)),
            pl.BlockSpec((None, 1, D), lambda i: (i // steps_per_batch, 0, 0)),
            pl.BlockSpec((1, D), lambda i: (0, 0))],
        out_specs=pl.BlockSpec((tm, D), lambda i: (i, 0)),
        compiler_params=pltpu.CompilerParams(
            dimension_semantics=("arbitrary",), vmem_limit_bytes=VMEM_LIMIT),
        name="combine_rows",
    )(*([yg] * TOP_K), gates, x1.reshape(T, D), gate2, final_g).reshape(B, S, D)


def _split_bf16x3(w):
    def top(v):
        return lax.bitcast_convert_type(lax.bitcast_convert_type(v, jnp.int32) & jnp.int32(-65536), F32)
    w0 = top(w)
    w1 = top(w - w0)
    w2 = w - w0 - w1
    return jnp.concatenate([w0, w1, w2], axis=1).astype(BF16)


def _rotary_tables(positions):
    half = ROT_DIM // 2
    inv_freq = jnp.exp(-math.log(ROPE_THETA) * jnp.arange(0, ROT_DIM, 2, dtype=F32) / ROT_DIM)
    d = jnp.arange(128, dtype=jnp.int32) % ATTN_HEAD_DIM
    freq = jnp.where(d < ROT_DIM, inv_freq[d % half], 0.0)
    sign = jnp.where(d < half, -1.0, 1.0)
    ang = positions.astype(F32)[:, :, None] * freq
    return jnp.cos(ang), jnp.sin(ang) * sign


def kernel(x, c, positions, w_ada, b_ada, norm1_g, w_in, hgrn_lb_logits, hgrn_norm_g, attn_norm_g,
           w_out, norm2_g, w_router, b_router, w_gate_up, b_gate_up, w_down, b_down, final_norm_g):
    B, S, D = x.shape
    T = B * S
    assert w_in.shape[0] == 1, "single-layer block: the final norm is fused into the combine step"
    l = 0
    ctab, stab = _rotary_tables(positions)
    lower_bounds = jnp.cumsum(jax.nn.softmax(hgrn_lb_logits.astype(F32), axis=0), axis=0)
    mod = _ada(c, w_ada[l], b_ada[l])
    shift1, scale1, gate1, shift2, scale2, gate2 = jnp.split(mod[:, None, :], N_MOD, axis=-1)
    q, k, lf, v, gt, aq, ak, av, km = _proj(
        x, scale1, shift1, norm1_g[l][None], w_in[l].astype(BF16), lower_bounds[l][None], ctab, stab)
    o_a = _moba(aq, km, ak, av, attn_norm_g[l][None])
    o_h = _hgrn(q, k, lf, v, gt, hgrn_norm_g[l][None])
    x1, h2, top_idx, gates, rank, counts = _mix(o_h, o_a, x, gate1, scale2, shift2, norm2_g[l][None],
                                                w_out[l].astype(BF16), _split_bf16x3(w_router[l]), b_router[l][None])
    n_blk = (T * TOP_K) // MOE_ROWS + N_EXPERTS
    pad_start, blk_expert, n_used = _tile_layout(counts.reshape(-1), MOE_ROWS, n_blk)
    idx = top_idx.reshape(T, TOP_K)
    experts = jnp.arange(N_EXPERTS, dtype=jnp.int32)
    pos = jnp.sum(jnp.where(idx[..., None] == experts, pad_start, 0), axis=-1) + rank.reshape(T, TOP_K)
    tok = jnp.arange(T * TOP_K, dtype=jnp.int32) // TOP_K
    xs = _sc_permute(h2.reshape(T, D // 2), tok, pos.reshape(-1), n_blk * MOE_ROWS, 128)
    y_sorted = _moe_rows(xs, blk_expert, n_used, w_gate_up[l], b_gate_up[l], w_down[l], b_down[l])
    yg = _sc_gather(y_sorted, pos.T.reshape(-1), 128)
    return _combine_rows(yg, gates.reshape(T, TOP_K), x1, gate2, final_norm_g[None])
```

```python
import functools
import math

import jax
import jax.numpy as jnp
from jax import lax
from jax.experimental import pallas as pl
from jax.experimental.pallas import tpu as pltpu
from jax.experimental.pallas import tpu_sc as plsc

F32 = jnp.float32
BF16 = jnp.bfloat16
HIGHEST = lax.Precision.HIGHEST

HGRN_DK = 128
HGRN_CHUNK = 64
ATTN_HEADS = 4
ATTN_HEAD_DIM = 64
ROT_DIM = ATTN_HEAD_DIM // 4
ROPE_THETA = 500000.0
MOBA_BLOCK = 256
MOBA_TOPK = 3
N_EXPERTS = 32
TOP_K = 4
SWIGLU_ALPHA = 1.702
SWIGLU_LIMIT = 7.0
N_MOD = 6
RMS_EPS = 1e-6

EXP_CLAMP = 80.0
MOE_ROWS = 256
MOBA_ROWS = 256
MOBA_TILES_PER_STEP = 8
PART_W = 128
VMEM_LIMIT = 56 * 1024 * 1024
SC_CORES = 2
SC_SUBCORES = 16


def _sigmoid(x):
    return 1.0 / (1.0 + jnp.exp(-x))


def _dot(a, b, **kw):
    return jnp.dot(a, b, preferred_element_type=F32, **kw)


def _dot_nt(a, b, **kw):
    return lax.dot_general(a, b, (((1,), (1,)), ((), ())), preferred_element_type=F32, **kw)


def _pack_bf16_pairs(x):
    w = x.shape[1] // 2
    bits = lax.bitcast_convert_type(x.astype(BF16).astype(F32), jnp.int32)
    return bits[:, w:] | lax.shift_right_logical(bits[:, :w], 16)


def _unpack_bf16_pairs(p):
    lo = lax.bitcast_convert_type(lax.shift_left(p, 16), F32)
    hi = lax.bitcast_convert_type(p & jnp.int32(-65536), F32)
    return jnp.concatenate([lo, hi], axis=1)


def _group_ranks(onehots, carry):
    T = onehots[0].shape[0]
    member = onehots[0]
    for oh in onehots[1:]:
        member = member + oh
    earlier = (lax.broadcasted_iota(jnp.int32, (T, T), 1) < lax.broadcasted_iota(jnp.int32, (T, T), 0))
    base = carry + _dot(earlier.astype(BF16), member.astype(BF16))
    ranks = [jnp.sum(oh * base, axis=1, keepdims=True) for oh in onehots]
    return ranks, carry + jnp.sum(member, axis=0, keepdims=True)


def _columns_to_rows(cols):
    T = cols[0].shape[0]
    slab = jnp.concatenate([c.astype(F32) for c in cols] + [jnp.zeros((T, 128 - len(cols)), F32)], axis=1)
    return slab.T[0:8, :].astype(jnp.int32)


def _ada_kernel(c_ref, w_ref, b_ref, o_ref):
    c = c_ref[...]
    o_ref[...] = _dot(c * _sigmoid(c), w_ref[...], precision=HIGHEST) + b_ref[...]


def _ada(c, w_ada, b_ada):
    B, D = c.shape
    N = w_ada.shape[1]
    tn = N // 4
    c8 = jnp.zeros((8, D), F32).at[:B].set(c)
    out = pl.pallas_call(
        _ada_kernel,
        out_shape=jax.ShapeDtypeStruct((8, N), F32),
        grid=(N // tn,),
        in_specs=[pl.BlockSpec((8, D), lambda j: (0, 0)),
                  pl.BlockSpec((D, tn), lambda j: (0, j)),
                  pl.BlockSpec((1, tn), lambda j: (0, j))],
        out_specs=pl.BlockSpec((8, tn), lambda j: (0, j)),
        compiler_params=pltpu.CompilerParams(vmem_limit_bytes=VMEM_LIMIT),
        name="ada",
    )(c8, w_ada, b_ada.reshape(1, N))
    return out[:B]


def _proj_kernel(x_ref, sc_ref, sh_ref, g_ref, w_ref, lb_ref, ct_ref, st_ref,
                 q_ref, k_ref, lf_ref, v_ref, gt_ref, aq_ref, ak_ref, av_ref, km_ref,
                 *, hw, aw):
    x = x_ref[...]
    ms = jnp.mean(x * x, axis=-1, keepdims=True)
    h = x * lax.rsqrt(ms + RMS_EPS) * g_ref[...]
    h = h * (1.0 + sc_ref[...]) + sh_ref[...]
    proj = _dot(h.astype(BF16), w_ref[...])

    hq = proj[:, 0:hw]
    hf = proj[:, hw:2 * hw]
    hg = proj[:, 3 * hw:4 * hw]
    q_ref[...] = hq * _sigmoid(hq) * (HGRN_DK ** -0.5)
    lb = lb_ref[...]
    f = lb + (1.0 - lb) * _sigmoid(hf)
    k_ref[...] = 1.0 - f
    lf_ref[...] = jnp.log(f)
    v_ref[...] = proj[:, 2 * hw:3 * hw]
    gt_ref[...] = hg * _sigmoid(hg)

    ct = jnp.concatenate([ct_ref[...]] * (aw // 128), axis=1)
    st = jnp.concatenate([st_ref[...]] * (aw // 128), axis=1)
    lane = lax.broadcasted_iota(jnp.int32, ct.shape, 1) % ATTN_HEAD_DIM
    first_half = lane < (ROT_DIM // 2)

    def rot(t):
        partner = jnp.where(first_half, pltpu.roll(t, aw - ROT_DIM // 2, 1), pltpu.roll(t, ROT_DIM // 2, 1))
        return t * ct + partner * st

    base = 4 * hw
    aq = rot(proj[:, base:base + aw])
    ak = rot(proj[:, base + aw:base + 2 * aw])
    av = proj[:, base + 2 * aw:base + 3 * aw]
    km_ref[...] = jnp.mean(ak, axis=0, keepdims=True)
    lane128 = lax.broadcasted_iota(jnp.int32, (x.shape[0], 128), 1)
    for pair in range(ATTN_HEADS // 2):
        aq_ref[pair] = aq[:, pair * 128:(pair + 1) * 128]
    for hd in range(ATTN_HEADS):
        pair, half = divmod(hd, 2)
        in_head = (lane128 // ATTN_HEAD_DIM) == half
        ak_ref[hd] = jnp.where(in_head, ak[:, pair * 128:(pair + 1) * 128], 0.0).astype(BF16)
        av_ref[hd] = av[:, hd * ATTN_HEAD_DIM:(hd + 1) * ATTN_HEAD_DIM].astype(BF16)


def _proj(x, scale1, shift1, norm_g, w_in_bf16, lb, ctab, stab):
    B, S, D = x.shape
    hw = lb.shape[-1]
    aw = ATTN_HEADS * ATTN_HEAD_DIM
    tm = MOBA_BLOCK
    nb = S // MOBA_BLOCK
    n_proj = w_in_bf16.shape[1]
    row = lambda b, i: (b, i, 0)
    vec = lambda b, i: (b, 0, 0)
    head = lambda b, i: (b, 0, i, 0)
    out_shapes = (
        jax.ShapeDtypeStruct((B, S, hw), F32),
        jax.ShapeDtypeStruct((B, S, hw), F32),
        jax.ShapeDtypeStruct((B, S, hw), F32),
        jax.ShapeDtypeStruct((B, S, hw), F32),
        jax.ShapeDtypeStruct((B, S, hw), F32),
        jax.ShapeDtypeStruct((B, ATTN_HEADS // 2, S, 128), F32),
        jax.ShapeDtypeStruct((B, ATTN_HEADS, S, 128), BF16),
        jax.ShapeDtypeStruct((B, ATTN_HEADS, S, ATTN_HEAD_DIM), BF16),
        jax.ShapeDtypeStruct((B, nb, 1, aw), F32),
    )
    hspec = pl.BlockSpec((None, tm, hw), row)
    aspec = pl.BlockSpec((None, ATTN_HEADS, tm, ATTN_HEAD_DIM), head)
    return pl.pallas_call(
        functools.partial(_proj_kernel, hw=hw, aw=aw),
        out_shape=out_shapes,
        grid=(B, S // tm),
        in_specs=[pl.BlockSpec((None, tm, D), row),
                  pl.BlockSpec((None, 1, D), vec),
                  pl.BlockSpec((None, 1, D), vec),
                  pl.BlockSpec((1, D), lambda b, i: (0, 0)),
                  pl.BlockSpec((D, n_proj), lambda b, i: (0, 0)),
                  pl.BlockSpec((1, hw), lambda b, i: (0, 0)),
                  pl.BlockSpec((None, tm, 128), row),
                  pl.BlockSpec((None, tm, 128), row)],
        out_specs=(hspec, hspec, hspec, hspec, hspec,
                   pl.BlockSpec((None, ATTN_HEADS // 2, tm, 128), head),
                   pl.BlockSpec((None, ATTN_HEADS, tm, 128), head), aspec,
                   pl.BlockSpec((None, None, 1, aw), lambda b, i: (b, i, 0, 0))),
        compiler_params=pltpu.CompilerParams(
            dimension_semantics=("arbitrary", "arbitrary"), vmem_limit_bytes=VMEM_LIMIT),
        name="proj",
    )(x, scale1, shift1, norm_g, w_in_bf16, lb, ctab, stab)


def _hgrn_kernel(q_ref, k_ref, lf_ref, v_ref, gt_ref, gn_ref, o_ref, st_ref, *, n_heads, n_chunks):
    @pl.when(pl.program_id(1) == 0)
    def _():
        st_ref[...] = jnp.zeros_like(st_ref)

    C = HGRN_CHUNK
    r = lax.broadcasted_iota(jnp.int32, (C, C), 0)
    c = lax.broadcasted_iota(jnp.int32, (C, C), 1)
    tril = c <= r
    ltri = tril.astype(F32)
    gn = gn_ref[...]

    def chunk(ci, carry):
        r0 = pl.multiple_of(ci * C, C)
        rows = pl.ds(r0, C)
        b_all = _dot(ltri, lf_ref[rows, :], precision=HIGHEST)
        heads = range(n_heads)
        sls = [slice(hd * HGRN_DK, (hd + 1) * HGRN_DK) for hd in heads]
        bs = [b_all[:, sl] for sl in sls]
        b_lasts = [b[C - 1:C, :] for b in bs]
        qs = [q_ref[rows, sl] for sl in sls]
        ks = [k_ref[rows, sl] for sl in sls]
        vs = [v_ref[rows, sl] for sl in sls]
        states = [st_ref[hd] for hd in heads]
        o_inter = [_dot_nt((qs[hd] * jnp.exp(bs[hd])).astype(BF16), states[hd].astype(BF16)) for hd in heads]
        rhos = [0.5 * bl for bl in b_lasts]
        qas = [(qs[hd] * jnp.exp(jnp.minimum(bs[hd] - rhos[hd], EXP_CLAMP))).astype(BF16) for hd in heads]
        kbs = [(ks[hd] * jnp.exp(jnp.minimum(rhos[hd] - bs[hd], EXP_CLAMP))).astype(BF16) for hd in heads]
        scores = [jnp.where(tril, _dot_nt(qas[hd], kbs[hd]), 0.0).astype(BF16) for hd in heads]
        outs = [o_inter[hd] + _dot(scores[hd], vs[hd].astype(BF16)) for hd in heads]
        kds = [(ks[hd] * jnp.exp(b_lasts[hd] - bs[hd])).astype(BF16) for hd in heads]
        upds = [_dot(vs[hd].T.astype(BF16), kds[hd]) for hd in heads]
        for hd in heads:
            st_ref[hd] = states[hd] * jnp.exp(b_lasts[hd]) + upds[hd]
            o = outs[hd]
            ms = jnp.mean(o * o, axis=-1, keepdims=True)
            o_ref[rows, sls[hd]] = o * lax.rsqrt(ms + RMS_EPS) * gn * gt_ref[rows, sls[hd]]
        return carry

    lax.fori_loop(0, n_chunks, chunk, 0)


def _hgrn(q, k, lf, v, gt, norm_g):
    B, S, hw = q.shape
    n_heads = hw // HGRN_DK
    tc = 512
    spec = pl.BlockSpec((None, tc, hw), lambda b, i: (b, i, 0))
    return pl.pallas_call(
        functools.partial(_hgrn_kernel, n_heads=n_heads, n_chunks=tc // HGRN_CHUNK),
        out_shape=jax.ShapeDtypeStruct((B, S, hw), F32),
        grid=(B, S // tc),
        in_specs=[spec, spec, spec, spec, spec, pl.BlockSpec((1, HGRN_DK), lambda b, i: (0, 0))],
        out_specs=spec,
        scratch_shapes=[pltpu.VMEM((n_heads, HGRN_DK, HGRN_DK), F32)],
        compiler_params=pltpu.CompilerParams(
            dimension_semantics=("arbitrary", "arbitrary"), vmem_limit_bytes=VMEM_LIMIT),
        name="hgrn",
    )(q, k, lf, v, gt, norm_g)


def _sc_gather(table, idx, chunk):
    M = idx.shape[0]
    D = table.shape[1]
    n_workers = SC_CORES * SC_SUBCORES
    per_worker = M // n_workers
    assert per_worker * n_workers == M and per_worker % chunk == 0 and chunk % 8 == 0
    mesh = plsc.VectorSubcoreMesh(core_axis_name="c", subcore_axis_name="s")

    @functools.partial(
        pl.kernel, mesh=mesh,
        out_type=jax.ShapeDtypeStruct((M, D), table.dtype),
        scratch_types=[pltpu.VMEM((chunk,), jnp.int32),
                       pltpu.VMEM((chunk, D), table.dtype),
                       pltpu.SemaphoreType.DMA],
    )
    def gather_kernel(table_hbm, idx_hbm, out_hbm, idx_v, rows_v, sem):
        wid = lax.axis_index("s") * SC_CORES + lax.axis_index("c")
        base = wid * per_worker

        @pl.loop(0, per_worker // chunk)
        def _(j):
            off = pl.multiple_of(base + j * chunk, 8)
            pltpu.sync_copy(idx_hbm.at[pl.ds(off, chunk)], idx_v)
            pltpu.async_copy(table_hbm.at[idx_v], rows_v, sem).wait()
            pltpu.sync_copy(rows_v, out_hbm.at[pl.ds(off, chunk)])

    return gather_kernel(table, idx)


def _sc_permute(table, src, dst, n_out, chunk):
    M = src.shape[0]
    D = table.shape[1]
    n_workers = SC_CORES * SC_SUBCORES
    per_worker = M // n_workers
    assert per_worker * n_workers == M and per_worker % chunk == 0 and chunk % 8 == 0
    mesh = plsc.VectorSubcoreMesh(core_axis_name="c", subcore_axis_name="s")

    @functools.partial(
        pl.kernel, mesh=mesh,
        out_type=jax.ShapeDtypeStruct((n_out, D), table.dtype),
        scratch_types=[pltpu.VMEM((chunk,), jnp.int32),
                       pltpu.VMEM((chunk,), jnp.int32),
                       pltpu.VMEM((chunk, D), table.dtype),
                       pltpu.SemaphoreType.DMA],
    )
    def permute_kernel(table_hbm, src_hbm, dst_hbm, out_hbm, src_v, dst_v, rows_v, sem):
        wid = lax.axis_index("s") * SC_CORES + lax.axis_index("c")
        base = wid * per_worker

        @pl.loop(0, per_worker // chunk)
        def _(j):
            off = pl.multiple_of(base + j * chunk, 8)
            pltpu.sync_copy(src_hbm.at[pl.ds(off, chunk)], src_v)
            pltpu.sync_copy(dst_hbm.at[pl.ds(off, chunk)], dst_v)
            pltpu.async_copy(table_hbm.at[src_v], rows_v, sem).wait()
            pltpu.async_copy(rows_v, out_hbm.at[dst_v], sem).wait()

    return permute_kernel(table, src, dst)


def _tile_layout(counts, bm, n_tiles):
    n_groups = counts.shape[0]
    padded = (counts + bm - 1) // bm * bm
    pad_end = jnp.cumsum(padded)
    tile_start = jnp.arange(n_tiles, dtype=jnp.int32) * bm
    tile_group = jnp.minimum(
        jnp.sum((pad_end[None, :] <= tile_start[:, None]).astype(jnp.int32), axis=1), n_groups - 1)
    n_used = (pad_end[-1] // bm).astype(jnp.int32).reshape(1)
    return pad_end - padded, tile_group.astype(jnp.int32), n_used


def _null_partial(rows):
    lane = lax.broadcasted_iota(jnp.int32, (rows, PART_W), 1)
    return jnp.where(lane < ATTN_HEAD_DIM, 0.0, -jnp.inf).astype(F32)


def _moba_sel_kernel(q_ref, km_ref, k_ref, v_ref, idx_ref, cnt_ref, own_ref, cnt_acc, *, n_blocks):
    j = pl.program_id(1)
    T = MOBA_BLOCK
    heads = range(ATTN_HEADS)
    qs = [q_ref[hd // 2] for hd in heads]
    gates = [_dot_nt(qs[hd], km_ref[hd], precision=HIGHEST) for hd in heads]
    lane = lax.broadcasted_iota(jnp.int32, gates[0].shape, 1)
    neg_inf = jnp.float32(-jnp.inf)
    gates = [jnp.where(lane < j, g, neg_inf) for g in gates]
    picks = [[] for _ in heads]
    for _ in range(MOBA_TOPK):
        ms = [jnp.max(g, axis=1, keepdims=True) for g in gates]
        firsts = [jnp.min(jnp.where(g == m, lane, n_blocks), axis=1, keepdims=True) for g, m in zip(gates, ms)]
        for hd in heads:
            picks[hd].append(jnp.where(ms[hd] > neg_inf, firsts[hd], -1))
        gates = [jnp.where(lane == f, neg_inf, g) for g, f in zip(gates, firsts)]

    @pl.when(j == 0)
    def _():
        cnt_acc[...] = jnp.zeros_like(cnt_acc)

    for hd in heads:
        ranks, total = _group_ranks([(lane == p).astype(F32) for p in picks[hd]], cnt_acc[hd])
        idx_ref[hd] = _columns_to_rows(picks[hd] + ranks)
        cnt_acc[hd] = total
        cnt_ref[hd] = total.astype(jnp.int32)
    causal = lax.broadcasted_iota(jnp.int32, (T, T), 1) <= lax.broadcasted_iota(jnp.int32, (T, T), 0)
    scale = ATTN_HEAD_DIM ** -0.5
    ss = [jnp.where(causal, _dot_nt((qs[hd] * scale).astype(BF16), k_ref[hd]), neg_inf) for hd in heads]
    mx = [jnp.max(s, axis=1, keepdims=True) for s in ss]
    ps = [jnp.exp(s - m) for s, m in zip(ss, mx)]
    ls = [jnp.sum(p, axis=1, keepdims=True) for p in ps]
    accs = [_dot(ps[hd].astype(BF16), v_ref[hd]) for hd in heads]
    for hd in heads:
        lse = jnp.broadcast_to(mx[hd] + jnp.log(ls[hd]), (T, PART_W - ATTN_HEAD_DIM))
        own_ref[hd] = jnp.concatenate([accs[hd] / ls[hd], lse], axis=1)


def _moba_sel(aq, kmean, ak, av):
    B, H, S, hd = av.shape
    nb = S // MOBA_BLOCK
    T = MOBA_BLOCK
    blk = lambda b, j: (b, 0, j, 0)
    return pl.pallas_call(
        functools.partial(_moba_sel_kernel, n_blocks=nb),
        out_shape=(jax.ShapeDtypeStruct((B, H, 8, S), jnp.int32),
                   jax.ShapeDtypeStruct((B, H, 1, nb), jnp.int32),
                   jax.ShapeDtypeStruct((B, H, S, PART_W), F32)),
        grid=(B, nb),
        in_specs=[pl.BlockSpec((None, H // 2, T, 128), blk),
                  pl.BlockSpec((None, H, nb, 128), lambda b, j: (b, 0, 0, 0)),
                  pl.BlockSpec((None, H, T, 128), blk),
                  pl.BlockSpec((None, H, T, hd), blk)],
        out_specs=(pl.BlockSpec((None, H, 8, T), lambda b, j: (b, 0, 0, j)),
                   pl.BlockSpec((None, H, 1, nb), lambda b, j: (b, 0, 0, 0)),
                   pl.BlockSpec((None, H, T, PART_W), blk)),
        scratch_shapes=[pltpu.VMEM((H, 1, nb), F32)],
        compiler_params=pltpu.CompilerParams(
            dimension_semantics=("arbitrary", "arbitrary"), vmem_limit_bytes=VMEM_LIMIT),
        name="moba_sel",
    )(aq, kmean, ak, av)


def _moba_blk_kernel(tg_ref, nu_ref, q_ref, *refs):
    n = MOBA_TILES_PER_STEP
    k_refs, v_refs, o_ref = refs[:n], refs[n:2 * n], refs[2 * n]
    R = MOBA_ROWS
    t0 = pl.program_id(0) * n

    @pl.when(t0 < nu_ref[0])
    def _():
        scale = ATTN_HEAD_DIM ** -0.5
        ss = [_dot_nt((q_ref[j * R:(j + 1) * R, :] * scale).astype(BF16), k_refs[j][...]) for j in range(n)]
        ms = [jnp.max(s, axis=1, keepdims=True) for s in ss]
        ps = [jnp.exp(s - m) for s, m in zip(ss, ms)]
        ls = [jnp.sum(p, axis=1, keepdims=True) for p in ps]
        accs = [_dot(p.astype(BF16), v_refs[j][...]) for j, p in enumerate(ps)]
        null = _null_partial(R)
        for j in range(n):
            lse = jnp.broadcast_to(ms[j] + jnp.log(ls[j]), (R, PART_W - ATTN_HEAD_DIM))
            row = jnp.concatenate([accs[j] / ls[j], lse], axis=1)
            o_ref[j * R:(j + 1) * R, :] = jnp.where(t0 + j < nu_ref[0], row, null)

    @pl.when(t0 >= nu_ref[0])
    def _():
        o_ref[...] = _null_partial(n * R)


def _moba_blk(qs, tile_group, n_used, ak, av):
    B, H, S, hd = av.shape
    nb = S // MOBA_BLOCK
    R = MOBA_ROWS
    n = MOBA_TILES_PER_STEP
    n_tiles = qs.shape[0] // R
    assert n_tiles % n == 0
    kv = lambda j: (lambda i, tg, nu: (tg[i * n + j] // nb, tg[i * n + j] % nb, 0, 0))
    grid_spec = pltpu.PrefetchScalarGridSpec(
        num_scalar_prefetch=2,
        grid=(n_tiles // n,),
        in_specs=[pl.BlockSpec((n * R, 128), lambda i, tg, nu: (i, 0))]
        + [pl.BlockSpec((None, None, MOBA_BLOCK, 128), kv(j)) for j in range(n)]
        + [pl.BlockSpec((None, None, MOBA_BLOCK, hd), kv(j)) for j in range(n)],
        out_specs=pl.BlockSpec((n * R, PART_W), lambda i, tg, nu: (i, 0)),
    )
    k4 = ak.reshape(B * H, nb, MOBA_BLOCK, 128)
    v4 = av.reshape(B * H, nb, MOBA_BLOCK, hd)
    return pl.pallas_call(
        _moba_blk_kernel,
        out_shape=jax.ShapeDtypeStruct((n_tiles * R, PART_W), F32),
        grid_spec=grid_spec,
        compiler_params=pltpu.CompilerParams(
            dimension_semantics=("arbitrary",), vmem_limit_bytes=VMEM_LIMIT),
        name="moba_blk",
    )(tile_group, n_used, qs, *([k4] * n), *([v4] * n))


def _moba_merge_kernel(own_ref, pg_ref, g_ref, o_ref):
    hd = ATTN_HEAD_DIM
    rows = [own_ref[...]] + [pg_ref[s] for s in range(MOBA_TOPK)]
    lses = [pltpu.roll(r, hd, 1) for r in rows]
    top = lses[0]
    for z in lses[1:]:
        top = jnp.maximum(top, z)
    num = jnp.zeros_like(top)
    den = jnp.zeros_like(top)
    for r, z in zip(rows, lses):
        w = jnp.exp(z - top)
        num = num + w * r
        den = den + w
    o = (num / den)[:, :hd]
    ms = jnp.mean(o * o, axis=-1, keepdims=True)
    o_ref[...] = o * lax.rsqrt(ms + RMS_EPS) * g_ref[...]


def _moba_merge(own, pg, norm_g):
    n = own.shape[0]
    T = 512
    row = lambda i: (i, 0)
    return pl.pallas_call(
        _moba_merge_kernel,
        out_shape=jax.ShapeDtypeStruct((n, ATTN_HEAD_DIM), F32),
        grid=(n // T,),
        in_specs=[pl.BlockSpec((T, PART_W), row),
                  pl.BlockSpec((MOBA_TOPK, T, PART_W), lambda i: (0, i, 0)),
                  pl.BlockSpec((1, ATTN_HEAD_DIM), lambda i: (0, 0))],
        out_specs=pl.BlockSpec((T, ATTN_HEAD_DIM), row),
        compiler_params=pltpu.CompilerParams(
            dimension_semantics=("arbitrary",), vmem_limit_bytes=VMEM_LIMIT),
        name="moba_merge",
    )(own, pg, norm_g)


def _moba(aq, km, ak, av, norm_g):
    B, H, S, hd = av.shape
    nb = S // MOBA_BLOCK
    n_q = B * H * S
    kmp = km.reshape(B, nb, H // 2, 128)
    half = jnp.arange(128, dtype=jnp.int32) // hd
    kmean = jnp.stack([jnp.where(half == h % 2, kmp[:, :, h // 2, :], 0.0) for h in range(H)], axis=1)
    idx8, counts, own = _moba_sel(aq, kmean, ak, av)
    sel = idx8[:, :, 0:MOBA_TOPK, :].reshape(B * H, MOBA_TOPK, S)
    rank = idx8[:, :, MOBA_TOPK:2 * MOBA_TOPK, :].reshape(B * H, MOBA_TOPK, S)
    n_groups = B * H * nb
    n_tiles = (n_q * MOBA_TOPK) // MOBA_ROWS + n_groups
    pad_start, tile_group, n_used = _tile_layout(counts.reshape(-1), MOBA_ROWS, n_tiles)
    blocks = jnp.arange(nb, dtype=jnp.int32)[:, None]
    start = jnp.sum(jnp.where(sel[:, :, None, :] == blocks, pad_start.reshape(B * H, 1, nb, 1), 0), axis=2)
    a_ids = jnp.arange(n_q * MOBA_TOPK, dtype=jnp.int32).reshape(B * H, MOBA_TOPK, S)
    assert n_tiles * MOBA_ROWS >= n_q * MOBA_TOPK + MOBA_ROWS
    pos = jnp.where(sel >= 0, start + rank, n_used[0] * MOBA_ROWS + a_ids % MOBA_ROWS)
    bh = jnp.arange(B * H, dtype=jnp.int32)[:, None, None]
    t = jnp.arange(S, dtype=jnp.int32)[None, None, :]
    pair_row = jnp.broadcast_to((bh // H * (H // 2) + bh % H // 2) * S + t, pos.shape)
    qs = _sc_permute(aq.reshape(B * (H // 2) * S, 128), pair_row.reshape(-1), pos.reshape(-1),
                     n_tiles * MOBA_ROWS, 256)
    parts = _moba_blk(qs, tile_group, n_used, ak, av)
    pg = _sc_gather(parts, pos.transpose(1, 0, 2).reshape(-1), 256)
    o = _moba_merge(own.reshape(n_q, PART_W), pg.reshape(MOBA_TOPK, n_q, PART_W), norm_g)
    return o.reshape(B, H, S, hd)


def _mix_kernel(oh_ref, oa_ref, x_ref, g1_ref, sc2_ref, sh2_ref, n2_ref, wo_ref, wr_ref, br_ref,
                x1_ref, h2_ref, gw_ref, idx_ref, cnt_ref, cnt_acc):
    cat = jnp.concatenate([oh_ref[...]] + [oa_ref[hd] for hd in range(ATTN_HEADS)], axis=1)
    mix = _dot(cat.astype(BF16), wo_ref[...])
    x1 = x_ref[...] + g1_ref[...] * mix
    x1_ref[...] = x1
    ms = jnp.mean(x1 * x1, axis=-1, keepdims=True)
    h2 = x1 * lax.rsqrt(ms + RMS_EPS) * n2_ref[...]
    h2 = h2 * (1.0 + sc2_ref[...]) + sh2_ref[...]
    h2_ref[...] = _pack_bf16_pairs(h2)
    E = N_EXPERTS
    h_0 = h2.astype(BF16)
    r_1 = h2 - h_0.astype(F32)
    h_1 = r_1.astype(BF16)
    h_2 = (r_1 - h_1.astype(F32)).astype(BF16)
    wr = wr_ref[...]
    p_0 = _dot(h_0, wr)
    p_1 = _dot(h_1, wr[:, :2 * E])
    p_2 = _dot(h_2, wr[:, :E])
    logits = (p_0[:, :E] + (p_0[:, E:2 * E] + p_1[:, :E])
              + (p_0[:, 2 * E:] + p_1[:, E:] + p_2)) + br_ref[...]
    lane = lax.broadcasted_iota(jnp.int32, logits.shape, 1)
    neg_inf = jnp.float32(-jnp.inf)
    vals, idxs = [], []
    for _ in range(TOP_K):
        m = jnp.max(logits, axis=1, keepdims=True)
        first = jnp.min(jnp.where(logits == m, lane, N_EXPERTS), axis=1, keepdims=True)
        vals.append(m)
        idxs.append(first)
        logits = jnp.where(lane == first, neg_inf, logits)
    e = [jnp.exp(v - vals[0]) for v in vals]
    denom = e[0] + e[1] + e[2] + e[3]
    gw_ref[...] = jnp.concatenate([ei / denom for ei in e], axis=1)

    @pl.when((pl.program_id(0) == 0) & (pl.program_id(1) == 0))
    def _():
        cnt_acc[...] = jnp.zeros_like(cnt_acc)

    ranks, total = _group_ranks([(lane == ix).astype(F32) for ix in idxs], cnt_acc[...])
    idx_ref[...] = _columns_to_rows(idxs + ranks)
    cnt_acc[...] = total
    cnt_ref[...] = total.astype(jnp.int32)


def _mix(oh, oa, x, gate1, scale2, shift2, norm2_g, w_out_bf16, w_router, b_router):
    B, S, D = x.shape
    hw = oh.shape[-1]
    tm = 256
    row = lambda b, i: (b, i, 0)
    vec = lambda b, i: (b, 0, 0)
    const = lambda b, i: (0, 0)
    return pl.pallas_call(
        _mix_kernel,
        out_shape=(jax.ShapeDtypeStruct((B, S, D), F32),
                   jax.ShapeDtypeStruct((B, S, D // 2), jnp.int32),
                   jax.ShapeDtypeStruct((B, S, TOP_K), F32),
                   jax.ShapeDtypeStruct((B, 2 * TOP_K, S), jnp.int32),
                   jax.ShapeDtypeStruct((1, N_EXPERTS), jnp.int32)),
        grid=(B, S // tm),
        in_specs=[pl.BlockSpec((None, tm, hw), row),
                  pl.BlockSpec((None, ATTN_HEADS, tm, ATTN_HEAD_DIM), lambda b, i: (b, 0, i, 0)),
                  pl.BlockSpec((None, tm, D), row),
                  pl.BlockSpec((None, 1, D), vec),
                  pl.BlockSpec((None, 1, D), vec),
                  pl.BlockSpec((None, 1, D), vec),
                  pl.BlockSpec((1, D), const),
                  pl.BlockSpec((D, D), const),
                  pl.BlockSpec((D, 3 * N_EXPERTS), const),
                  pl.BlockSpec((1, N_EXPERTS), const)],
        out_specs=(pl.BlockSpec((None, tm, D), row),
                   pl.BlockSpec((None, tm, D // 2), row),
                   pl.BlockSpec((None, tm, TOP_K), row),
                   pl.BlockSpec((None, 2 * TOP_K, tm), lambda b, i: (b, 0, i)),
                   pl.BlockSpec((1, N_EXPERTS), const)),
        scratch_shapes=[pltpu.VMEM((1, N_EXPERTS), F32)],
        compiler_params=pltpu.CompilerParams(
            dimension_semantics=("arbitrary", "arbitrary"), vmem_limit_bytes=VMEM_LIMIT),
        name="mix",
    )(oh, oa, x, gate1, scale2, shift2, norm2_g, w_out_bf16, w_router, b_router)


def _moe_rows_kernel(be_ref, nu_ref, x_ref, wgu_ref, bgu_ref, wd_ref, bd_ref, y_ref, wgu16, wd16, *, d_ff):
    i = pl.program_id(0)

    @pl.when((i == 0) | (be_ref[i] != be_ref[jnp.maximum(i - 1, 0)]))
    def _():
        wgu16[...] = wgu_ref[...].astype(BF16)
        wd16[...] = wd_ref[...].astype(BF16)

    @pl.when(i < nu_ref[0])
    def _():
        gu = _dot(_unpack_bf16_pairs(x_ref[...]).astype(BF16), wgu16[...]) + bgu_ref[...]
        gate = jnp.minimum(gu[:, :d_ff], SWIGLU_LIMIT)
        up = jnp.clip(gu[:, d_ff:], -SWIGLU_LIMIT, SWIGLU_LIMIT)
        act = (up + 1.0) * gate * _sigmoid(SWIGLU_ALPHA * gate)
        y_ref[...] = _pack_bf16_pairs(_dot(act.astype(BF16), wd16[...]) + bd_ref[...])

    @pl.when(i >= nu_ref[0])
    def _():
        y_ref[...] = jnp.zeros_like(y_ref)


def _moe_rows(xs, blk_expert, n_used, wgu, bgu, wd, bd):
    D = 2 * xs.shape[1]
    bm = MOE_ROWS
    n_blk = xs.shape[0] // bm
    d_ff = wd.shape[1]
    wsel = lambda i, be, nu: (be[i], 0, 0)
    grid_spec = pltpu.PrefetchScalarGridSpec(
        num_scalar_prefetch=2,
        grid=(n_blk,),
        in_specs=[pl.BlockSpec((bm, D // 2), lambda i, be, nu: (i, 0)),
                  pl.BlockSpec((None, D, 2 * d_ff), wsel),
                  pl.BlockSpec((None, 1, 2 * d_ff), wsel),
                  pl.BlockSpec((None, d_ff, D), wsel),
                  pl.BlockSpec((None, 1, D), wsel)],
        out_specs=pl.BlockSpec((bm, D // 2), lambda i, be, nu: (i, 0)),
        scratch_shapes=[pltpu.VMEM((D, 2 * d_ff), BF16), pltpu.VMEM((d_ff, D), BF16)],
    )
    return pl.pallas_call(
        functools.partial(_moe_rows_kernel, d_ff=d_ff),
        out_shape=jax.ShapeDtypeStruct((n_blk * bm, D // 2), jnp.int32),
        grid_spec=grid_spec,
        compiler_params=pltpu.CompilerParams(
            dimension_semantics=("arbitrary",), vmem_limit_bytes=VMEM_LIMIT),
        name="moe_rows",
    )(blk_expert, n_used, xs, wgu, bgu.reshape(N_EXPERTS, 1, 2 * d_ff), wd, bd.reshape(N_EXPERTS, 1, D))


def _combine_rows_kernel(*refs):
    y_refs = refs[:TOP_K]
    gw_ref, x1_ref, g2_ref, fg_ref, o_ref = refs[TOP_K:]
    gw = gw_ref[...]
    y = gw[:, 0:1] * _unpack_bf16_pairs(y_refs[0][...])
    for kk in range(1, TOP_K):
        y = y + gw[:, kk:kk + 1] * _unpack_bf16_pairs(y_refs[kk][...])
    x2 = x1_ref[...] + g2_ref[...] * y
    ms = jnp.mean(x2 * x2, axis=-1, keepdims=True)
    o_ref[...] = x2 * lax.rsqrt(ms + RMS_EPS) * fg_ref[...]


def _combine_rows(yg, gates, x1, gate2, final_g):
    B, S, D = x1.shape
    T = B * S
    tm = 256
    steps = T // tm
    steps_per_batch = S // tm
    slot_spec = lambda kk: pl.BlockSpec((tm, D // 2), lambda i: (kk * steps + i, 0))
    return pl.pallas_call(
        _combine_rows_kernel,
        out_shape=jax.ShapeDtypeStruct((T, D), F32),
        grid=(steps,),
        in_specs=[slot_spec(kk) for kk in range(TOP_K)] + [
            pl.BlockSpec((tm, TOP_K), lambda i: (i, 0)),
            pl.BlockSpec((tm, D), lambda i: (i, 0)),
            pl.BlockSpec((None, 1, D), lambda i: (i // steps_per_batch, 0, 0)),
            pl.BlockSpec((1, D), lambda i: (0, 0))],
        out_specs=pl.BlockSpec((tm, D), lambda i: (i, 0)),
        compiler_params=pltpu.CompilerParams(
            dimension_semantics=("arbitrary",), vmem_limit_bytes=VMEM_LIMIT),
        name="combine_rows",
    )(*([yg] * TOP_K), gates, x1.reshape(T, D), gate2, final_g).reshape(B, S, D)


def _split_bf16x3(w):
    def top(v):
        return lax.bitcast_convert_type(lax.bitcast_convert_type(v, jnp.int32) & jnp.int32(-65536), F32)
    w0 = top(w)
    w1 = top(w - w0)
    w2 = w - w0 - w1
    return jnp.concatenate([w0, w1, w2], axis=1).astype(BF16)


def _rotary_tables(positions):
    half = ROT_DIM // 2
    inv_freq = jnp.exp(-math.log(ROPE_THETA) * jnp.arange(0, ROT_DIM, 2, dtype=F32) / ROT_DIM)
    d = jnp.arange(128, dtype=jnp.int32) % ATTN_HEAD_DIM
    freq = jnp.where(d < ROT_DIM, inv_freq[d % half], 0.0)
    sign = jnp.where(d < half, -1.0, 1.0)
    ang = positions.astype(F32)[:, :, None] * freq
    return jnp.cos(ang), jnp.sin(ang) * sign


def kernel(x, c, positions, w_ada, b_ada, norm1_g, w_in, hgrn_lb_logits, hgrn_norm_g, attn_norm_g,
           w_out, norm2_g, w_router, b_router, w_gate_up, b_gate_up, w_down, b_down, final_norm_g):
    B, S, D = x.shape
    T = B * S
    assert w_in.shape[0] == 1, "single-layer block: the final norm is fused into the combine step"
    l = 0
    ctab, stab = _rotary_tables(positions)
    lower_bounds = jnp.cumsum(jax.nn.softmax(hgrn_lb_logits.astype(F32), axis=0), axis=0)
    mod = _ada(c, w_ada[l], b_ada[l])
    shift1, scale1, gate1, shift2, scale2, gate2 = jnp.split(mod[:, None, :], N_MOD, axis=-1)
    q, k, lf, v, gt, aq, ak, av, km = _proj(
        x, scale1, shift1, norm1_g[l][None], w_in[l].astype(BF16), lower_bounds[l][None], ctab, stab)
    o_a = _moba(aq, km, ak, av, attn_norm_g[l][None])
    o_h = _hgrn(q, k, lf, v, gt, hgrn_norm_g[l][None])
    x1, h2, gates, idx8, counts = _mix(o_h, o_a, x, gate1, scale2, shift2, norm2_g[l][None],
                                       w_out[l].astype(BF16), _split_bf16x3(w_router[l]), b_router[l][None])
    n_blk = (T * TOP_K) // MOE_ROWS + N_EXPERTS
    pad_start, blk_expert, n_used = _tile_layout(counts.reshape(-1), MOE_ROWS, n_blk)
    experts = jnp.arange(N_EXPERTS, dtype=jnp.int32)[:, None]
    chosen = idx8[:, 0:TOP_K, :]
    pos = jnp.sum(jnp.where(chosen[:, :, None, :] == experts, pad_start[:, None], 0), axis=2) + idx8[:, TOP_K:, :]
    tok = jnp.broadcast_to(jnp.arange(B, dtype=jnp.int32)[:, None, None] * S
                           + jnp.arange(S, dtype=jnp.int32)[None, None, :], pos.shape)
    xs = _sc_permute(h2.reshape(T, D // 2), tok.reshape(-1), pos.reshape(-1), n_blk * MOE_ROWS, 128)
    y_sorted = _moe_rows(xs, blk_expert, n_used, w_gate_up[l], b_gate_up[l], w_down[l], b_down[l])
    yg = _sc_gather(y_sorted, pos.transpose(1, 0, 2).reshape(-1), 128)
    return _combine_rows(yg, gates.reshape(T, TOP_K), x1, gate2, final_norm_g[None])
```

```python
import functools
import math

import jax
import jax.numpy as jnp
from jax import lax
from jax.experimental import pallas as pl
from jax.experimental.pallas import tpu as pltpu
from jax.experimental.pallas import tpu_sc as plsc

F32 = jnp.float32
BF16 = jnp.bfloat16
HIGHEST = lax.Precision.HIGHEST

HGRN_DK = 128
HGRN_CHUNK = 64
ATTN_HEADS = 4
ATTN_HEAD_DIM = 64
ROT_DIM = ATTN_HEAD_DIM // 4
ROPE_THETA = 500000.0
MOBA_BLOCK = 256
MOBA_TOPK = 3
N_EXPERTS = 32
TOP_K = 4
SWIGLU_ALPHA = 1.702
SWIGLU_LIMIT = 7.0
N_MOD = 6
RMS_EPS = 1e-6

EXP_CLAMP = 80.0
MOE_ROWS = 256
MOBA_ROWS = 256
MOBA_TILES_PER_STEP = 8
PART_W = 128
VMEM_LIMIT = 56 * 1024 * 1024
SC_CORES = 2
SC_SUBCORES = 16


def _sigmoid(x):
    return 1.0 / (1.0 + jnp.exp(-x))


def _dot(a, b, **kw):
    return jnp.dot(a, b, preferred_element_type=F32, **kw)


def _dot_nt(a, b, **kw):
    return lax.dot_general(a, b, (((1,), (1,)), ((), ())), preferred_element_type=F32, **kw)


def _pack_bf16_pairs(x):
    w = x.shape[1] // 2
    bits = lax.bitcast_convert_type(x.astype(BF16).astype(F32), jnp.int32)
    return bits[:, w:] | lax.shift_right_logical(bits[:, :w], 16)


def _unpack_bf16_pairs(p):
    lo = lax.bitcast_convert_type(lax.shift_left(p, 16), F32)
    hi = lax.bitcast_convert_type(p & jnp.int32(-65536), F32)
    return jnp.concatenate([lo, hi], axis=1)


def _group_ranks(onehots, carry):
    T = onehots[0].shape[0]
    member = onehots[0]
    for oh in onehots[1:]:
        member = member + oh
    earlier = (lax.broadcasted_iota(jnp.int32, (T, T), 1) < lax.broadcasted_iota(jnp.int32, (T, T), 0))
    base = carry + _dot(earlier.astype(BF16), member.astype(BF16))
    ranks = [jnp.sum(oh * base, axis=1, keepdims=True) for oh in onehots]
    return ranks, carry + jnp.sum(member, axis=0, keepdims=True)


def _columns_to_rows(cols):
    T = cols[0].shape[0]
    slab = jnp.concatenate([c.astype(F32) for c in cols] + [jnp.zeros((T, 128 - len(cols)), F32)], axis=1)
    return slab.T[0:8, :].astype(jnp.int32)


def _ada_kernel(c_ref, w_ref, b_ref, o_ref):
    c = c_ref[...]
    o_ref[...] = _dot(c * _sigmoid(c), w_ref[...], precision=HIGHEST) + b_ref[...]


def _ada(c, w_ada, b_ada):
    B, D = c.shape
    N = w_ada.shape[1]
    tn = N // 4
    c8 = jnp.zeros((8, D), F32).at[:B].set(c)
    out = pl.pallas_call(
        _ada_kernel,
        out_shape=jax.ShapeDtypeStruct((8, N), F32),
        grid=(N // tn,),
        in_specs=[pl.BlockSpec((8, D), lambda j: (0, 0)),
                  pl.BlockSpec((D, tn), lambda j: (0, j)),
                  pl.BlockSpec((1, tn), lambda j: (0, j))],
        out_specs=pl.BlockSpec((8, tn), lambda j: (0, j)),
        compiler_params=pltpu.CompilerParams(vmem_limit_bytes=VMEM_LIMIT),
        name="ada",
    )(c8, w_ada, b_ada.reshape(1, N))
    return out[:B]


def _proj_kernel(x_ref, sc_ref, sh_ref, g_ref, w_ref, lb_ref, ct_ref, st_ref,
                 q_ref, k_ref, lf_ref, v_ref, gt_ref, aq_ref, ak_ref, av_ref, km_ref,
                 *, hw, aw):
    x = x_ref[...]
    ms = jnp.mean(x * x, axis=-1, keepdims=True)
    h = x * lax.rsqrt(ms + RMS_EPS) * g_ref[...]
    h = h * (1.0 + sc_ref[...]) + sh_ref[...]
    proj = _dot(h.astype(BF16), w_ref[...])

    hq = proj[:, 0:hw]
    hf = proj[:, hw:2 * hw]
    hg = proj[:, 3 * hw:4 * hw]
    q_ref[...] = hq * _sigmoid(hq) * (HGRN_DK ** -0.5)
    lb = lb_ref[...]
    f = lb + (1.0 - lb) * _sigmoid(hf)
    k_ref[...] = 1.0 - f
    lf_ref[...] = jnp.log(f)
    v_ref[...] = proj[:, 2 * hw:3 * hw]
    gt_ref[...] = hg * _sigmoid(hg)

    ct = jnp.concatenate([ct_ref[...]] * (aw // 128), axis=1)
    st = jnp.concatenate([st_ref[...]] * (aw // 128), axis=1)
    lane = lax.broadcasted_iota(jnp.int32, ct.shape, 1) % ATTN_HEAD_DIM
    first_half = lane < (ROT_DIM // 2)

    def rot(t):
        partner = jnp.where(first_half, pltpu.roll(t, aw - ROT_DIM // 2, 1), pltpu.roll(t, ROT_DIM // 2, 1))
        return t * ct + partner * st

    base = 4 * hw
    aq = rot(proj[:, base:base + aw])
    ak = rot(proj[:, base + aw:base + 2 * aw])
    av = proj[:, base + 2 * aw:base + 3 * aw]
    km_ref[...] = jnp.mean(ak, axis=0, keepdims=True)
    lane128 = lax.broadcasted_iota(jnp.int32, (x.shape[0], 128), 1)
    for pair in range(ATTN_HEADS // 2):
        aq_ref[pair] = aq[:, pair * 128:(pair + 1) * 128]
    for hd in range(ATTN_HEADS):
        pair, half = divmod(hd, 2)
        in_head = (lane128 // ATTN_HEAD_DIM) == half
        ak_ref[hd] = jnp.where(in_head, ak[:, pair * 128:(pair + 1) * 128], 0.0).astype(BF16)
        av_ref[hd] = av[:, hd * ATTN_HEAD_DIM:(hd + 1) * ATTN_HEAD_DIM].astype(BF16)


def _proj(x, scale1, shift1, norm_g, w_in_bf16, lb, ctab, stab):
    B, S, D = x.shape
    hw = lb.shape[-1]
    aw = ATTN_HEADS * ATTN_HEAD_DIM
    tm = MOBA_BLOCK
    nb = S // MOBA_BLOCK
    n_proj = w_in_bf16.shape[1]
    row = lambda b, i: (b, i, 0)
    vec = lambda b, i: (b, 0, 0)
    head = lambda b, i: (b, 0, i, 0)
    out_shapes = (
        jax.ShapeDtypeStruct((B, S, hw), F32),
        jax.ShapeDtypeStruct((B, S, hw), F32),
        jax.ShapeDtypeStruct((B, S, hw), F32),
        jax.ShapeDtypeStruct((B, S, hw), F32),
        jax.ShapeDtypeStruct((B, S, hw), F32),
        jax.ShapeDtypeStruct((B, ATTN_HEADS // 2, S, 128), F32),
        jax.ShapeDtypeStruct((B, ATTN_HEADS, S, 128), BF16),
        jax.ShapeDtypeStruct((B, ATTN_HEADS, S, ATTN_HEAD_DIM), BF16),
        jax.ShapeDtypeStruct((B, nb, 1, aw), F32),
    )
    hspec = pl.BlockSpec((None, tm, hw), row)
    aspec = pl.BlockSpec((None, ATTN_HEADS, tm, ATTN_HEAD_DIM), head)
    return pl.pallas_call(
        functools.partial(_proj_kernel, hw=hw, aw=aw),
        out_shape=out_shapes,
        grid=(B, S // tm),
        in_specs=[pl.BlockSpec((None, tm, D), row),
                  pl.BlockSpec((None, 1, D), vec),
                  pl.BlockSpec((None, 1, D), vec),
                  pl.BlockSpec((1, D), lambda b, i: (0, 0)),
                  pl.BlockSpec((D, n_proj), lambda b, i: (0, 0)),
                  pl.BlockSpec((1, hw), lambda b, i: (0, 0)),
                  pl.BlockSpec((None, tm, 128), row),
                  pl.BlockSpec((None, tm, 128), row)],
        out_specs=(hspec, hspec, hspec, hspec, hspec,
                   pl.BlockSpec((None, ATTN_HEADS // 2, tm, 128), head),
                   pl.BlockSpec((None, ATTN_HEADS, tm, 128), head), aspec,
                   pl.BlockSpec((None, None, 1, aw), lambda b, i: (b, i, 0, 0))),
        compiler_params=pltpu.CompilerParams(
            dimension_semantics=("arbitrary", "arbitrary"), vmem_limit_bytes=VMEM_LIMIT),
        name="proj",
    )(x, scale1, shift1, norm_g, w_in_bf16, lb, ctab, stab)


def _hgrn_kernel(q_ref, k_ref, lf_ref, v_ref, gt_ref, gn_ref, o_ref, st_ref, *, n_heads, n_chunks):
    @pl.when(pl.program_id(1) == 0)
    def _():
        st_ref[...] = jnp.zeros_like(st_ref)

    C = HGRN_CHUNK
    r = lax.broadcasted_iota(jnp.int32, (C, C), 0)
    c = lax.broadcasted_iota(jnp.int32, (C, C), 1)
    tril = c <= r
    ltri = tril.astype(F32)
    gn = gn_ref[...]

    def chunk(ci, carry):
        r0 = pl.multiple_of(ci * C, C)
        rows = pl.ds(r0, C)
        b_all = _dot(ltri, lf_ref[rows, :], precision=HIGHEST)
        heads = range(n_heads)
        sls = [slice(hd * HGRN_DK, (hd + 1) * HGRN_DK) for hd in heads]
        bs = [b_all[:, sl] for sl in sls]
        b_lasts = [b[C - 1:C, :] for b in bs]
        qs = [q_ref[rows, sl] for sl in sls]
        ks = [k_ref[rows, sl] for sl in sls]
        vs = [v_ref[rows, sl] for sl in sls]
        states = [st_ref[hd] for hd in heads]
        o_inter = [_dot_nt((qs[hd] * jnp.exp(bs[hd])).astype(BF16), states[hd].astype(BF16)) for hd in heads]
        rhos = [0.5 * bl for bl in b_lasts]
        qas = [(qs[hd] * jnp.exp(jnp.minimum(bs[hd] - rhos[hd], EXP_CLAMP))).astype(BF16) for hd in heads]
        kbs = [(ks[hd] * jnp.exp(jnp.minimum(rhos[hd] - bs[hd], EXP_CLAMP))).astype(BF16) for hd in heads]
        scores = [jnp.where(tril, _dot_nt(qas[hd], kbs[hd]), 0.0).astype(BF16) for hd in heads]
        outs = [o_inter[hd] + _dot(scores[hd], vs[hd].astype(BF16)) for hd in heads]
        kds = [(ks[hd] * jnp.exp(b_lasts[hd] - bs[hd])).astype(BF16) for hd in heads]
        upds = [_dot(vs[hd].T.astype(BF16), kds[hd]) for hd in heads]
        for hd in heads:
            st_ref[hd] = states[hd] * jnp.exp(b_lasts[hd]) + upds[hd]
            o = outs[hd]
            ms = jnp.mean(o * o, axis=-1, keepdims=True)
            o_ref[rows, sls[hd]] = o * lax.rsqrt(ms + RMS_EPS) * gn * gt_ref[rows, sls[hd]]
        return carry

    lax.fori_loop(0, n_chunks, chunk, 0)


def _hgrn(q, k, lf, v, gt, norm_g):
    B, S, hw = q.shape
    n_heads = hw // HGRN_DK
    tc = 512
    spec = pl.BlockSpec((None, tc, hw), lambda b, i: (b, i, 0))
    return pl.pallas_call(
        functools.partial(_hgrn_kernel, n_heads=n_heads, n_chunks=tc // HGRN_CHUNK),
        out_shape=jax.ShapeDtypeStruct((B, S, hw), F32),
        grid=(B, S // tc),
        in_specs=[spec, spec, spec, spec, spec, pl.BlockSpec((1, HGRN_DK), lambda b, i: (0, 0))],
        out_specs=spec,
        scratch_shapes=[pltpu.VMEM((n_heads, HGRN_DK, HGRN_DK), F32)],
        compiler_params=pltpu.CompilerParams(
            dimension_semantics=("arbitrary", "arbitrary"), vmem_limit_bytes=VMEM_LIMIT),
        name="hgrn",
    )(q, k, lf, v, gt, norm_g)


def _sc_move_rows(table, src, dst, n_out, chunk):
    M = src.shape[0]
    D = table.shape[1]
    n_workers = SC_CORES * SC_SUBCORES
    per_worker = M // n_workers
    n_chunks = per_worker // chunk
    assert per_worker * n_workers == M and n_chunks * chunk == per_worker and n_chunks % 2 == 0 and chunk % 8 == 0
    mesh = plsc.VectorSubcoreMesh(core_axis_name="c", subcore_axis_name="s")
    idx_t = pltpu.VMEM((chunk,), jnp.int32)
    row_t = pltpu.VMEM((chunk, D), table.dtype)
    sem_t = pltpu.SemaphoreType.DMA

    def body(table_hbm, src_hbm, dst_hbm, out_hbm, src_v, dst_v, rows_v, g_sem, s_sem):
        wid = lax.axis_index("s") * SC_CORES + lax.axis_index("c")
        base = wid * per_worker

        def offset(j):
            return pl.multiple_of(base + j * chunk, 8)

        def gather(b):
            return pltpu.make_async_copy(table_hbm.at[src_v[b]], rows_v[b], g_sem[b])

        def start_gather(j, b):
            pltpu.sync_copy(src_hbm.at[pl.ds(offset(j), chunk)], src_v[b])
            gather(b).start()

        def write_out(j, b):
            if dst_hbm is None:
                pltpu.sync_copy(rows_v[b], out_hbm.at[pl.ds(offset(j), chunk)])
            else:
                pltpu.sync_copy(dst_hbm.at[pl.ds(offset(j), chunk)], dst_v[b])
                pltpu.async_copy(rows_v[b], out_hbm.at[dst_v[b]], s_sem[b]).wait()

        start_gather(0, 0)

        @pl.loop(0, n_chunks, step=2)
        def _(j):
            for b in (0, 1):
                @pl.when(j + b + 1 < n_chunks)
                def _():
                    start_gather(j + b + 1, 1 - b)
                gather(b).wait()
                write_out(j + b, b)

    if dst is None:
        @functools.partial(pl.kernel, mesh=mesh, out_type=jax.ShapeDtypeStruct((n_out, D), table.dtype),
                           scratch_types=[idx_t, idx_t, row_t, row_t, sem_t, sem_t])
        def gather_kernel(table_hbm, src_hbm, out_hbm, s0, s1, r0, r1, g0, g1):
            body(table_hbm, src_hbm, None, out_hbm, (s0, s1), None, (r0, r1), (g0, g1), None)
        return gather_kernel(table, src)

    @functools.partial(pl.kernel, mesh=mesh, out_type=jax.ShapeDtypeStruct((n_out, D), table.dtype),
                       scratch_types=[idx_t, idx_t, idx_t, idx_t, row_t, row_t, sem_t, sem_t, sem_t, sem_t])
    def permute_kernel(table_hbm, src_hbm, dst_hbm, out_hbm, s0, s1, d0, d1, r0, r1, g0, g1, w0, w1):
        body(table_hbm, src_hbm, dst_hbm, out_hbm, (s0, s1), (d0, d1), (r0, r1), (g0, g1), (w0, w1))
    return permute_kernel(table, src, dst)


def _sc_gather(table, idx, chunk):
    return _sc_move_rows(table, idx, None, idx.shape[0], chunk)


def _sc_permute(table, src, dst, n_out, chunk):
    return _sc_move_rows(table, src, dst, n_out, chunk)


def _tile_layout(counts, bm, n_tiles):
    n_groups = counts.shape[0]
    padded = (counts + bm - 1) // bm * bm
    pad_end = jnp.cumsum(padded)
    tile_start = jnp.arange(n_tiles, dtype=jnp.int32) * bm
    tile_group = jnp.minimum(
        jnp.sum((pad_end[None, :] <= tile_start[:, None]).astype(jnp.int32), axis=1), n_groups - 1)
    n_used = (pad_end[-1] // bm).astype(jnp.int32).reshape(1)
    return pad_end - padded, tile_group.astype(jnp.int32), n_used


def _null_partial(rows):
    lane = lax.broadcasted_iota(jnp.int32, (rows, PART_W), 1)
    return jnp.where(lane < ATTN_HEAD_DIM, 0.0, -jnp.inf).astype(F32)


def _moba_sel_kernel(q_ref, km_ref, k_ref, v_ref, idx_ref, cnt_ref, own_ref, cnt_acc, *, n_blocks):
    j = pl.program_id(1)
    T = MOBA_BLOCK
    heads = range(ATTN_HEADS)
    qs = [q_ref[hd // 2] for hd in heads]
    gates = [_dot_nt(qs[hd], km_ref[hd], precision=HIGHEST) for hd in heads]
    lane = lax.broadcasted_iota(jnp.int32, gates[0].shape, 1)
    neg_inf = jnp.float32(-jnp.inf)
    gates = [jnp.where(lane < j, g, neg_inf) for g in gates]
    picks = [[] for _ in heads]
    for _ in range(MOBA_TOPK):
        ms = [jnp.max(g, axis=1, keepdims=True) for g in gates]
        firsts = [jnp.min(jnp.where(g == m, lane, n_blocks), axis=1, keepdims=True) for g, m in zip(gates, ms)]
        for hd in heads:
            picks[hd].append(jnp.where(ms[hd] > neg_inf, firsts[hd], -1))
        gates = [jnp.where(lane == f, neg_inf, g) for g, f in zip(gates, firsts)]

    @pl.when(j == 0)
    def _():
        cnt_acc[...] = jnp.zeros_like(cnt_acc)

    for hd in heads:
        ranks, total = _group_ranks([(lane == p).astype(F32) for p in picks[hd]], cnt_acc[hd])
        idx_ref[hd] = _columns_to_rows(picks[hd] + ranks)
        cnt_acc[hd] = total
        cnt_ref[hd] = total.astype(jnp.int32)
    causal = lax.broadcasted_iota(jnp.int32, (T, T), 1) <= lax.broadcasted_iota(jnp.int32, (T, T), 0)
    scale = ATTN_HEAD_DIM ** -0.5
    ss = [jnp.where(causal, _dot_nt((qs[hd] * scale).astype(BF16), k_ref[hd]), neg_inf) for hd in heads]
    mx = [jnp.max(s, axis=1, keepdims=True) for s in ss]
    ps = [jnp.exp(s - m) for s, m in zip(ss, mx)]
    ls = [jnp.sum(p, axis=1, keepdims=True) for p in ps]
    accs = [_dot(ps[hd].astype(BF16), v_ref[hd]) for hd in heads]
    for hd in heads:
        lse = jnp.broadcast_to(mx[hd] + jnp.log(ls[hd]), (T, PART_W - ATTN_HEAD_DIM))
        own_ref[hd] = jnp.concatenate([accs[hd] / ls[hd], lse], axis=1)


def _moba_sel(aq, kmean, ak, av):
    B, H, S, hd = av.shape
    nb = S // MOBA_BLOCK
    T = MOBA_BLOCK
    blk = lambda b, j: (b, 0, j, 0)
    return pl.pallas_call(
        functools.partial(_moba_sel_kernel, n_blocks=nb),
        out_shape=(jax.ShapeDtypeStruct((B, H, 8, S), jnp.int32),
                   jax.ShapeDtypeStruct((B, H, 1, nb), jnp.int32),
                   jax.ShapeDtypeStruct((B, H, S, PART_W), F32)),
        grid=(B, nb),
        in_specs=[pl.BlockSpec((None, H // 2, T, 128), blk),
                  pl.BlockSpec((None, H, nb, 128), lambda b, j: (b, 0, 0, 0)),
                  pl.BlockSpec((None, H, T, 128), blk),
                  pl.BlockSpec((None, H, T, hd), blk)],
        out_specs=(pl.BlockSpec((None, H, 8, T), lambda b, j: (b, 0, 0, j)),
                   pl.BlockSpec((None, H, 1, nb), lambda b, j: (b, 0, 0, 0)),
                   pl.BlockSpec((None, H, T, PART_W), blk)),
        scratch_shapes=[pltpu.VMEM((H, 1, nb), F32)],
        compiler_params=pltpu.CompilerParams(
            dimension_semantics=("arbitrary", "arbitrary"), vmem_limit_bytes=VMEM_LIMIT),
        name="moba_sel",
    )(aq, kmean, ak, av)


def _moba_blk_kernel(tg_ref, nu_ref, q_ref, *refs):
    n = MOBA_TILES_PER_STEP
    k_refs, v_refs, o_ref = refs[:n], refs[n:2 * n], refs[2 * n]
    R = MOBA_ROWS
    t0 = pl.program_id(0) * n

    @pl.when(t0 < nu_ref[0])
    def _():
        scale = ATTN_HEAD_DIM ** -0.5
        ss = [_dot_nt((q_ref[j * R:(j + 1) * R, :] * scale).astype(BF16), k_refs[j][...]) for j in range(n)]
        ms = [jnp.max(s, axis=1, keepdims=True) for s in ss]
        ps = [jnp.exp(s - m) for s, m in zip(ss, ms)]
        ls = [jnp.sum(p, axis=1, keepdims=True) for p in ps]
        accs = [_dot(p.astype(BF16), v_refs[j][...]) for j, p in enumerate(ps)]
        null = _null_partial(R)
        for j in range(n):
            lse = jnp.broadcast_to(ms[j] + jnp.log(ls[j]), (R, PART_W - ATTN_HEAD_DIM))
            row = jnp.concatenate([accs[j] / ls[j], lse], axis=1)
            o_ref[j * R:(j + 1) * R, :] = jnp.where(t0 + j < nu_ref[0], row, null)

    @pl.when(t0 >= nu_ref[0])
    def _():
        o_ref[...] = _null_partial(n * R)


def _moba_blk(qs, tile_group, n_used, ak, av):
    B, H, S, hd = av.shape
    nb = S // MOBA_BLOCK
    R = MOBA_ROWS
    n = MOBA_TILES_PER_STEP
    n_tiles = qs.shape[0] // R
    assert n_tiles % n == 0
    kv = lambda j: (lambda i, tg, nu: (tg[i * n + j] // nb, tg[i * n + j] % nb, 0, 0))
    grid_spec = pltpu.PrefetchScalarGridSpec(
        num_scalar_prefetch=2,
        grid=(n_tiles // n,),
        in_specs=[pl.BlockSpec((n * R, 128), lambda i, tg, nu: (i, 0))]
        + [pl.BlockSpec((None, None, MOBA_BLOCK, 128), kv(j)) for j in range(n)]
        + [pl.BlockSpec((None, None, MOBA_BLOCK, hd), kv(j)) for j in range(n)],
        out_specs=pl.BlockSpec((n * R, PART_W), lambda i, tg, nu: (i, 0)),
    )
    k4 = ak.reshape(B * H, nb, MOBA_BLOCK, 128)
    v4 = av.reshape(B * H, nb, MOBA_BLOCK, hd)
    return pl.pallas_call(
        _moba_blk_kernel,
        out_shape=jax.ShapeDtypeStruct((n_tiles * R, PART_W), F32),
        grid_spec=grid_spec,
        compiler_params=pltpu.CompilerParams(
            dimension_semantics=("arbitrary",), vmem_limit_bytes=VMEM_LIMIT),
        name="moba_blk",
    )(tile_group, n_used, qs, *([k4] * n), *([v4] * n))


def _moba_merge_kernel(own_ref, pg_ref, g_ref, o_ref):
    hd = ATTN_HEAD_DIM
    rows = [own_ref[...]] + [pg_ref[s] for s in range(MOBA_TOPK)]
    lses = [pltpu.roll(r, hd, 1) for r in rows]
    top = lses[0]
    for z in lses[1:]:
        top = jnp.maximum(top, z)
    num = jnp.zeros_like(top)
    den = jnp.zeros_like(top)
    for r, z in zip(rows, lses):
        w = jnp.exp(z - top)
        num = num + w * r
        den = den + w
    o = (num / den)[:, :hd]
    ms = jnp.mean(o * o, axis=-1, keepdims=True)
    o_ref[...] = o * lax.rsqrt(ms + RMS_EPS) * g_ref[...]


def _moba_merge(own, pg, norm_g):
    n = own.shape[0]
    T = 512
    row = lambda i: (i, 0)
    return pl.pallas_call(
        _moba_merge_kernel,
        out_shape=jax.ShapeDtypeStruct((n, ATTN_HEAD_DIM), F32),
        grid=(n // T,),
        in_specs=[pl.BlockSpec((T, PART_W), row),
                  pl.BlockSpec((MOBA_TOPK, T, PART_W), lambda i: (0, i, 0)),
                  pl.BlockSpec((1, ATTN_HEAD_DIM), lambda i: (0, 0))],
        out_specs=pl.BlockSpec((T, ATTN_HEAD_DIM), row),
        compiler_params=pltpu.CompilerParams(
            dimension_semantics=("arbitrary",), vmem_limit_bytes=VMEM_LIMIT),
        name="moba_merge",
    )(own, pg, norm_g)


def _moba(aq, km, ak, av, norm_g):
    B, H, S, hd = av.shape
    nb = S // MOBA_BLOCK
    n_q = B * H * S
    kmp = km.reshape(B, nb, H // 2, 128)
    half = jnp.arange(128, dtype=jnp.int32) // hd
    kmean = jnp.stack([jnp.where(half == h % 2, kmp[:, :, h // 2, :], 0.0) for h in range(H)], axis=1)
    idx8, counts, own = _moba_sel(aq, kmean, ak, av)
    sel = idx8[:, :, 0:MOBA_TOPK, :].reshape(B * H, MOBA_TOPK, S)
    rank = idx8[:, :, MOBA_TOPK:2 * MOBA_TOPK, :].reshape(B * H, MOBA_TOPK, S)
    n_groups = B * H * nb
    n_tiles = (n_q * MOBA_TOPK) // MOBA_ROWS + n_groups
    pad_start, tile_group, n_used = _tile_layout(counts.reshape(-1), MOBA_ROWS, n_tiles)
    blocks = jnp.arange(nb, dtype=jnp.int32)[:, None]
    start = jnp.sum(jnp.where(sel[:, :, None, :] == blocks, pad_start.reshape(B * H, 1, nb, 1), 0), axis=2)
    a_ids = jnp.arange(n_q * MOBA_TOPK, dtype=jnp.int32).reshape(B * H, MOBA_TOPK, S)
    assert n_tiles * MOBA_ROWS >= n_q * MOBA_TOPK + MOBA_ROWS
    pos = jnp.where(sel >= 0, start + rank, n_used[0] * MOBA_ROWS + a_ids % MOBA_ROWS)
    bh = jnp.arange(B * H, dtype=jnp.int32)[:, None, None]
    t = jnp.arange(S, dtype=jnp.int32)[None, None, :]
    pair_row = jnp.broadcast_to((bh // H * (H // 2) + bh % H // 2) * S + t, pos.shape)
    qs = _sc_permute(aq.reshape(B * (H // 2) * S, 128), pair_row.reshape(-1), pos.reshape(-1),
                     n_tiles * MOBA_ROWS, 256)
    parts = _moba_blk(qs, tile_group, n_used, ak, av)
    pg = _sc_gather(parts, pos.transpose(1, 0, 2).reshape(-1), 256)
    o = _moba_merge(own.reshape(n_q, PART_W), pg.reshape(MOBA_TOPK, n_q, PART_W), norm_g)
    return o.reshape(B, H, S, hd)


def _mix_kernel(oh_ref, oa_ref, x_ref, g1_ref, sc2_ref, sh2_ref, n2_ref, wo_ref, wr_ref, br_ref,
                x1_ref, h2_ref, gw_ref, idx_ref, cnt_ref, cnt_acc):
    cat = jnp.concatenate([oh_ref[...]] + [oa_ref[hd] for hd in range(ATTN_HEADS)], axis=1)
    mix = _dot(cat.astype(BF16), wo_ref[...])
    x1 = x_ref[...] + g1_ref[...] * mix
    x1_ref[...] = x1
    ms = jnp.mean(x1 * x1, axis=-1, keepdims=True)
    h2 = x1 * lax.rsqrt(ms + RMS_EPS) * n2_ref[...]
    h2 = h2 * (1.0 + sc2_ref[...]) + sh2_ref[...]
    h2_ref[...] = _pack_bf16_pairs(h2)
    E = N_EXPERTS
    h_0 = h2.astype(BF16)
    r_1 = h2 - h_0.astype(F32)
    h_1 = r_1.astype(BF16)
    h_2 = (r_1 - h_1.astype(F32)).astype(BF16)
    wr = wr_ref[...]
    p_0 = _dot(h_0, wr)
    p_1 = _dot(h_1, wr[:, :2 * E])
    p_2 = _dot(h_2, wr[:, :E])
    logits = (p_0[:, :E] + (p_0[:, E:2 * E] + p_1[:, :E])
              + (p_0[:, 2 * E:] + p_1[:, E:] + p_2)) + br_ref[...]
    lane = lax.broadcasted_iota(jnp.int32, logits.shape, 1)
    neg_inf = jnp.float32(-jnp.inf)
    vals, idxs = [], []
    for _ in range(TOP_K):
        m = jnp.max(logits, axis=1, keepdims=True)
        first = jnp.min(jnp.where(logits == m, lane, N_EXPERTS), axis=1, keepdims=True)
        vals.append(m)
        idxs.append(first)
        logits = jnp.where(lane == first, neg_inf, logits)
    e = [jnp.exp(v - vals[0]) for v in vals]
    denom = e[0] + e[1] + e[2] + e[3]
    gw_ref[...] = jnp.concatenate([ei / denom for ei in e], axis=1)

    @pl.when((pl.program_id(0) == 0) & (pl.program_id(1) == 0))
    def _():
        cnt_acc[...] = jnp.zeros_like(cnt_acc)

    ranks, total = _group_ranks([(lane == ix).astype(F32) for ix in idxs], cnt_acc[...])
    idx_ref[...] = _columns_to_rows(idxs + ranks)
    cnt_acc[...] = total
    cnt_ref[...] = total.astype(jnp.int32)


def _mix(oh, oa, x, gate1, scale2, shift2, norm2_g, w_out_bf16, w_router, b_router):
    B, S, D = x.shape
    hw = oh.shape[-1]
    tm = 256
    row = lambda b, i: (b, i, 0)
    vec = lambda b, i: (b, 0, 0)
    const = lambda b, i: (0, 0)
    return pl.pallas_call(
        _mix_kernel,
        out_shape=(jax.ShapeDtypeStruct((B, S, D), F32),
                   jax.ShapeDtypeStruct((B, S, D // 2), jnp.int32),
                   jax.ShapeDtypeStruct((B, S, TOP_K), F32),
                   jax.ShapeDtypeStruct((B, 2 * TOP_K, S), jnp.int32),
                   jax.ShapeDtypeStruct((1, N_EXPERTS), jnp.int32)),
        grid=(B, S // tm),
        in_specs=[pl.BlockSpec((None, tm, hw), row),
                  pl.BlockSpec((None, ATTN_HEADS, tm, ATTN_HEAD_DIM), lambda b, i: (b, 0, i, 0)),
                  pl.BlockSpec((None, tm, D), row),
                  pl.BlockSpec((None, 1, D), vec),
                  pl.BlockSpec((None, 1, D), vec),
                  pl.BlockSpec((None, 1, D), vec),
                  pl.BlockSpec((1, D), const),
                  pl.BlockSpec((D, D), const),
                  pl.BlockSpec((D, 3 * N_EXPERTS), const),
                  pl.BlockSpec((1, N_EXPERTS), const)],
        out_specs=(pl.BlockSpec((None, tm, D), row),
                   pl.BlockSpec((None, tm, D // 2), row),
                   pl.BlockSpec((None, tm, TOP_K), row),
                   pl.BlockSpec((None, 2 * TOP_K, tm), lambda b, i: (b, 0, i)),
                   pl.BlockSpec((1, N_EXPERTS), const)),
        scratch_shapes=[pltpu.VMEM((1, N_EXPERTS), F32)],
        compiler_params=pltpu.CompilerParams(
            dimension_semantics=("arbitrary", "arbitrary"), vmem_limit_bytes=VMEM_LIMIT),
        name="mix",
    )(oh, oa, x, gate1, scale2, shift2, norm2_g, w_out_bf16, w_router, b_router)


def _moe_rows_kernel(be_ref, nu_ref, x_ref, wgu_ref, bgu_ref, wd_ref, bd_ref, y_ref, wgu16, wd16, *, d_ff):
    i = pl.program_id(0)

    @pl.when((i == 0) | (be_ref[i] != be_ref[jnp.maximum(i - 1, 0)]))
    def _():
        wgu16[...] = wgu_ref[...].astype(BF16)
        wd16[...] = wd_ref[...].astype(BF16)

    @pl.when(i < nu_ref[0])
    def _():
        gu = _dot(_unpack_bf16_pairs(x_ref[...]).astype(BF16), wgu16[...]) + bgu_ref[...]
        gate = jnp.minimum(gu[:, :d_ff], SWIGLU_LIMIT)
        up = jnp.clip(gu[:, d_ff:], -SWIGLU_LIMIT, SWIGLU_LIMIT)
        act = (up + 1.0) * gate * _sigmoid(SWIGLU_ALPHA * gate)
        y_ref[...] = _pack_bf16_pairs(_dot(act.astype(BF16), wd16[...]) + bd_ref[...])

    @pl.when(i >= nu_ref[0])
    def _():
        y_ref[...] = jnp.zeros_like(y_ref)


def _moe_rows(xs, blk_expert, n_used, wgu, bgu, wd, bd):
    D = 2 * xs.shape[1]
    bm = MOE_ROWS
    n_blk = xs.shape[0] // bm
    d_ff = wd.shape[1]
    wsel = lambda i, be, nu: (be[i], 0, 0)
    grid_spec = pltpu.PrefetchScalarGridSpec(
        num_scalar_prefetch=2,
        grid=(n_blk,),
        in_specs=[pl.BlockSpec((bm, D // 2), lambda i, be, nu: (i, 0)),
                  pl.BlockSpec((None, D, 2 * d_ff), wsel),
                  pl.BlockSpec((None, 1, 2 * d_ff), wsel),
                  pl.BlockSpec((None, d_ff, D), wsel),
                  pl.BlockSpec((None, 1, D), wsel)],
        out_specs=pl.BlockSpec((bm, D // 2), lambda i, be, nu: (i, 0)),
        scratch_shapes=[pltpu.VMEM((D, 2 * d_ff), BF16), pltpu.VMEM((d_ff, D), BF16)],
    )
    return pl.pallas_call(
        functools.partial(_moe_rows_kernel, d_ff=d_ff),
        out_shape=jax.ShapeDtypeStruct((n_blk * bm, D // 2), jnp.int32),
        grid_spec=grid_spec,
        compiler_params=pltpu.CompilerParams(
            dimension_semantics=("arbitrary",), vmem_limit_bytes=VMEM_LIMIT),
        name="moe_rows",
    )(blk_expert, n_used, xs, wgu, bgu.reshape(N_EXPERTS, 1, 2 * d_ff), wd, bd.reshape(N_EXPERTS, 1, D))


def _combine_rows_kernel(*refs):
    y_refs = refs[:TOP_K]
    gw_ref, x1_ref, g2_ref, fg_ref, o_ref = refs[TOP_K:]
    gw = gw_ref[...]
    y = gw[:, 0:1] * _unpack_bf16_pairs(y_refs[0][...])
    for kk in range(1, TOP_K):
        y = y + gw[:, kk:kk + 1] * _unpack_bf16_pairs(y_refs[kk][...])
    x2 = x1_ref[...] + g2_ref[...] * y
    ms = jnp.mean(x2 * x2, axis=-1, keepdims=True)
    o_ref[...] = x2 * lax.rsqrt(ms + RMS_EPS) * fg_ref[...]


def _combine_rows(yg, gates, x1, gate2, final_g):
    B, S, D = x1.shape
    T = B * S
    tm = 256
    steps = T // tm
    steps_per_batch = S // tm
    slot_spec = lambda kk: pl.BlockSpec((tm, D // 2), lambda i: (kk * steps + i, 0))
    return pl.pallas_call(
        _combine_rows_kernel,
        out_shape=jax.ShapeDtypeStruct((T, D), F32),
        grid=(steps,),
        in_specs=[slot_spec(kk) for kk in range(TOP_K)] + [
            pl.BlockSpec((tm, TOP_K), lambda i: (i, 0)),
            pl.BlockSpec((tm, D), lambda i: (i, 0)),
            pl.BlockSpec((None, 1, D), lambda i: (i // steps_per_batch, 0, 0)),
            pl.BlockSpec((1, D), lambda i: (0, 0))],
        out_specs=pl.BlockSpec((tm, D), lambda i: (i, 0)),
        compiler_params=pltpu.CompilerParams(
            dimension_semantics=("arbitrary",), vmem_limit_bytes=VMEM_LIMIT),
        name="combine_rows",
    )(*([yg] * TOP_K), gates, x1.reshape(T, D), gate2, final_g).reshape(B, S, D)


def _split_bf16x3(w):
    def top(v):
        return lax.bitcast_convert_type(lax.bitcast_convert_type(v, jnp.int32) & jnp.int32(-65536), F32)
    w0 = top(w)
    w1 = top(w - w0)
    w2 = w - w0 - w1
    return jnp.concatenate([w0, w1, w2], axis=1).astype(BF16)


def _rotary_tables(positions):
    half = ROT_DIM // 2
    inv_freq = jnp.exp(-math.log(ROPE_THETA) * jnp.arange(0, ROT_DIM, 2, dtype=F32) / ROT_DIM)
    d = jnp.arange(128, dtype=jnp.int32) % ATTN_HEAD_DIM
    freq = jnp.where(d < ROT_DIM, inv_freq[d % half], 0.0)
    sign = jnp.where(d < half, -1.0, 1.0)
    ang = positions.astype(F32)[:, :, None] * freq
    return jnp.cos(ang), jnp.sin(ang) * sign


def kernel(x, c, positions, w_ada, b_ada, norm1_g, w_in, hgrn_lb_logits, hgrn_norm_g, attn_norm_g,
           w_out, norm2_g, w_router, b_router, w_gate_up, b_gate_up, w_down, b_down, final_norm_g):
    B, S, D = x.shape
    T = B * S
    assert w_in.shape[0] == 1, "single-layer block: the final norm is fused into the combine step"
    l = 0
    ctab, stab = _rotary_tables(positions)
    lower_bounds = jnp.cumsum(jax.nn.softmax(hgrn_lb_logits.astype(F32), axis=0), axis=0)
    mod = _ada(c, w_ada[l], b_ada[l])
    shift1, scale1, gate1, shift2, scale2, gate2 = jnp.split(mod[:, None, :], N_MOD, axis=-1)
    q, k, lf, v, gt, aq, ak, av, km = _proj(
        x, scale1, shift1, norm1_g[l][None], w_in[l].astype(BF16), lower_bounds[l][None], ctab, stab)
    o_a = _moba(aq, km, ak, av, attn_norm_g[l][None])
    o_h = _hgrn(q, k, lf, v, gt, hgrn_norm_g[l][None])
    x1, h2, gates, idx8, counts = _mix(o_h, o_a, x, gate1, scale2, shift2, norm2_g[l][None],
                                       w_out[l].astype(BF16), _split_bf16x3(w_router[l]), b_router[l][None])
    n_blk = (T * TOP_K) // MOE_ROWS + N_EXPERTS
    pad_start, blk_expert, n_used = _tile_layout(counts.reshape(-1), MOE_ROWS, n_blk)
    experts = jnp.arange(N_EXPERTS, dtype=jnp.int32)[:, None]
    chosen = idx8[:, 0:TOP_K, :]
    pos = jnp.sum(jnp.where(chosen[:, :, None, :] == experts, pad_start[:, None], 0), axis=2) + idx8[:, TOP_K:, :]
    tok = jnp.broadcast_to(jnp.arange(B, dtype=jnp.int32)[:, None, None] * S
                           + jnp.arange(S, dtype=jnp.int32)[None, None, :], pos.shape)
    xs = _sc_permute(h2.reshape(T, D // 2), tok.reshape(-1), pos.reshape(-1), n_blk * MOE_ROWS, 64)
    y_sorted = _moe_rows(xs, blk_expert, n_used, w_gate_up[l], b_gate_up[l], w_down[l], b_down[l])
    yg = _sc_gather(y_sorted, pos.transpose(1, 0, 2).reshape(-1), 64)
    return _combine_rows(yg, gates.reshape(T, TOP_K), x1, gate2, final_norm_g[None])
```

```python
import functools
import math

import jax
import jax.numpy as jnp
from jax import lax
from jax.experimental import pallas as pl
from jax.experimental.pallas import tpu as pltpu
from jax.experimental.pallas import tpu_sc as plsc

F32 = jnp.float32
BF16 = jnp.bfloat16
HIGHEST = lax.Precision.HIGHEST

HGRN_DK = 128
HGRN_CHUNK = 64
ATTN_HEADS = 4
ATTN_HEAD_DIM = 64
ROT_DIM = ATTN_HEAD_DIM // 4
ROPE_THETA = 500000.0
MOBA_BLOCK = 256
MOBA_TOPK = 3
N_EXPERTS = 32
TOP_K = 4
SWIGLU_ALPHA = 1.702
SWIGLU_LIMIT = 7.0
N_MOD = 6
RMS_EPS = 1e-6

EXP_CLAMP = 80.0
MOE_ROWS = 512
MOBA_ROWS = 256
MOBA_TILES_PER_STEP = 8
PART_W = 128
VMEM_LIMIT = 56 * 1024 * 1024
SC_CORES = 2
SC_SUBCORES = 16


def _sigmoid(x):
    return 1.0 / (1.0 + jnp.exp(-x))


def _dot(a, b, **kw):
    return jnp.dot(a, b, preferred_element_type=F32, **kw)


def _dot_nt(a, b, **kw):
    return lax.dot_general(a, b, (((1,), (1,)), ((), ())), preferred_element_type=F32, **kw)


def _pack_bf16_pairs(x):
    w = x.shape[1] // 2
    bits = lax.bitcast_convert_type(x.astype(BF16).astype(F32), jnp.int32)
    return bits[:, w:] | lax.shift_right_logical(bits[:, :w], 16)


def _unpack_bf16_pairs(p):
    lo = lax.bitcast_convert_type(lax.shift_left(p, 16), F32)
    hi = lax.bitcast_convert_type(p & jnp.int32(-65536), F32)
    return jnp.concatenate([lo, hi], axis=1)


def _group_ranks(onehots, carry):
    T = onehots[0].shape[0]
    member = onehots[0]
    for oh in onehots[1:]:
        member = member + oh
    earlier = (lax.broadcasted_iota(jnp.int32, (T, T), 1) < lax.broadcasted_iota(jnp.int32, (T, T), 0))
    base = carry + _dot(earlier.astype(BF16), member.astype(BF16))
    ranks = [jnp.sum(oh * base, axis=1, keepdims=True) for oh in onehots]
    return ranks, carry + jnp.sum(member, axis=0, keepdims=True)


def _columns_to_rows(cols):
    T = cols[0].shape[0]
    slab = jnp.concatenate([c.astype(F32) for c in cols] + [jnp.zeros((T, 128 - len(cols)), F32)], axis=1)
    return slab.T[0:8, :].astype(jnp.int32)


def _ada_kernel(c_ref, w_ref, b_ref, o_ref):
    c = c_ref[...]
    o_ref[...] = _dot(c * _sigmoid(c), w_ref[...], precision=HIGHEST) + b_ref[...]


def _ada(c, w_ada, b_ada):
    B, D = c.shape
    N = w_ada.shape[1]
    tn = N // 4
    c8 = jnp.zeros((8, D), F32).at[:B].set(c)
    out = pl.pallas_call(
        _ada_kernel,
        out_shape=jax.ShapeDtypeStruct((8, N), F32),
        grid=(N // tn,),
        in_specs=[pl.BlockSpec((8, D), lambda j: (0, 0)),
                  pl.BlockSpec((D, tn), lambda j: (0, j)),
                  pl.BlockSpec((1, tn), lambda j: (0, j))],
        out_specs=pl.BlockSpec((8, tn), lambda j: (0, j)),
        compiler_params=pltpu.CompilerParams(vmem_limit_bytes=VMEM_LIMIT),
        name="ada",
    )(c8, w_ada, b_ada.reshape(1, N))
    return out[:B]


def _proj_kernel(x_ref, sc_ref, sh_ref, g_ref, w_ref, lb_ref, ct_ref, st_ref,
                 q_ref, k_ref, lf_ref, v_ref, gt_ref, aq_ref, ak_ref, av_ref, km_ref,
                 *, hw, aw):
    x = x_ref[...]
    ms = jnp.mean(x * x, axis=-1, keepdims=True)
    h = x * lax.rsqrt(ms + RMS_EPS) * g_ref[...]
    h = h * (1.0 + sc_ref[...]) + sh_ref[...]
    proj = _dot(h.astype(BF16), w_ref[...])

    hq = proj[:, 0:hw]
    hf = proj[:, hw:2 * hw]
    hg = proj[:, 3 * hw:4 * hw]
    q_ref[...] = hq * _sigmoid(hq) * (HGRN_DK ** -0.5)
    lb = lb_ref[...]
    f = lb + (1.0 - lb) * _sigmoid(hf)
    k_ref[...] = 1.0 - f
    lf_ref[...] = jnp.log(f)
    v_ref[...] = proj[:, 2 * hw:3 * hw]
    gt_ref[...] = hg * _sigmoid(hg)

    ct = jnp.concatenate([ct_ref[...]] * (aw // 128), axis=1)
    st = jnp.concatenate([st_ref[...]] * (aw // 128), axis=1)
    lane = lax.broadcasted_iota(jnp.int32, ct.shape, 1) % ATTN_HEAD_DIM
    first_half = lane < (ROT_DIM // 2)

    def rot(t):
        partner = jnp.where(first_half, pltpu.roll(t, aw - ROT_DIM // 2, 1), pltpu.roll(t, ROT_DIM // 2, 1))
        return t * ct + partner * st

    base = 4 * hw
    aq = rot(proj[:, base:base + aw])
    ak = rot(proj[:, base + aw:base + 2 * aw])
    av = proj[:, base + 2 * aw:base + 3 * aw]
    km_ref[...] = jnp.mean(ak, axis=0, keepdims=True)
    lane128 = lax.broadcasted_iota(jnp.int32, (x.shape[0], 128), 1)
    for pair in range(ATTN_HEADS // 2):
        aq_ref[pair] = aq[:, pair * 128:(pair + 1) * 128]
    for hd in range(ATTN_HEADS):
        pair, half = divmod(hd, 2)
        in_head = (lane128 // ATTN_HEAD_DIM) == half
        ak_ref[hd] = jnp.where(in_head, ak[:, pair * 128:(pair + 1) * 128], 0.0).astype(BF16)
        av_ref[hd] = av[:, hd * ATTN_HEAD_DIM:(hd + 1) * ATTN_HEAD_DIM].astype(BF16)


def _proj(x, scale1, shift1, norm_g, w_in_bf16, lb, ctab, stab):
    B, S, D = x.shape
    hw = lb.shape[-1]
    aw = ATTN_HEADS * ATTN_HEAD_DIM
    tm = MOBA_BLOCK
    nb = S // MOBA_BLOCK
    n_proj = w_in_bf16.shape[1]
    row = lambda b, i: (b, i, 0)
    vec = lambda b, i: (b, 0, 0)
    head = lambda b, i: (b, 0, i, 0)
    out_shapes = (
        jax.ShapeDtypeStruct((B, S, hw), F32),
        jax.ShapeDtypeStruct((B, S, hw), F32),
        jax.ShapeDtypeStruct((B, S, hw), F32),
        jax.ShapeDtypeStruct((B, S, hw), F32),
        jax.ShapeDtypeStruct((B, S, hw), F32),
        jax.ShapeDtypeStruct((B, ATTN_HEADS // 2, S, 128), F32),
        jax.ShapeDtypeStruct((B, ATTN_HEADS, S, 128), BF16),
        jax.ShapeDtypeStruct((B, ATTN_HEADS, S, ATTN_HEAD_DIM), BF16),
        jax.ShapeDtypeStruct((B, nb, 1, aw), F32),
    )
    hspec = pl.BlockSpec((None, tm, hw), row)
    aspec = pl.BlockSpec((None, ATTN_HEADS, tm, ATTN_HEAD_DIM), head)
    return pl.pallas_call(
        functools.partial(_proj_kernel, hw=hw, aw=aw),
        out_shape=out_shapes,
        grid=(B, S // tm),
        in_specs=[pl.BlockSpec((None, tm, D), row),
                  pl.BlockSpec((None, 1, D), vec),
                  pl.BlockSpec((None, 1, D), vec),
                  pl.BlockSpec((1, D), lambda b, i: (0, 0)),
                  pl.BlockSpec((D, n_proj), lambda b, i: (0, 0)),
                  pl.BlockSpec((1, hw), lambda b, i: (0, 0)),
                  pl.BlockSpec((None, tm, 128), row),
                  pl.BlockSpec((None, tm, 128), row)],
        out_specs=(hspec, hspec, hspec, hspec, hspec,
                   pl.BlockSpec((None, ATTN_HEADS // 2, tm, 128), head),
                   pl.BlockSpec((None, ATTN_HEADS, tm, 128), head), aspec,
                   pl.BlockSpec((None, None, 1, aw), lambda b, i: (b, i, 0, 0))),
        compiler_params=pltpu.CompilerParams(
            dimension_semantics=("arbitrary", "arbitrary"), vmem_limit_bytes=VMEM_LIMIT),
        name="proj",
    )(x, scale1, shift1, norm_g, w_in_bf16, lb, ctab, stab)


def _hgrn_kernel(q_ref, k_ref, lf_ref, v_ref, gt_ref, gn_ref, o_ref, st_ref, *, n_heads, n_chunks):
    @pl.when(pl.program_id(1) == 0)
    def _():
        st_ref[...] = jnp.zeros_like(st_ref)

    C = HGRN_CHUNK
    r = lax.broadcasted_iota(jnp.int32, (C, C), 0)
    c = lax.broadcasted_iota(jnp.int32, (C, C), 1)
    tril = c <= r
    ltri = tril.astype(F32)
    gn = gn_ref[...]

    def chunk(ci, carry):
        r0 = pl.multiple_of(ci * C, C)
        rows = pl.ds(r0, C)
        b_all = _dot(ltri, lf_ref[rows, :], precision=HIGHEST)
        heads = range(n_heads)
        sls = [slice(hd * HGRN_DK, (hd + 1) * HGRN_DK) for hd in heads]
        bs = [b_all[:, sl] for sl in sls]
        b_lasts = [b[C - 1:C, :] for b in bs]
        qs = [q_ref[rows, sl] for sl in sls]
        ks = [k_ref[rows, sl] for sl in sls]
        vs = [v_ref[rows, sl] for sl in sls]
        states = [st_ref[hd] for hd in heads]
        o_inter = [_dot_nt((qs[hd] * jnp.exp(bs[hd])).astype(BF16), states[hd].astype(BF16)) for hd in heads]
        rhos = [0.5 * bl for bl in b_lasts]
        qas = [(qs[hd] * jnp.exp(jnp.minimum(bs[hd] - rhos[hd], EXP_CLAMP))).astype(BF16) for hd in heads]
        kbs = [(ks[hd] * jnp.exp(jnp.minimum(rhos[hd] - bs[hd], EXP_CLAMP))).astype(BF16) for hd in heads]
        scores = [jnp.where(tril, _dot_nt(qas[hd], kbs[hd]), 0.0).astype(BF16) for hd in heads]
        outs = [o_inter[hd] + _dot(scores[hd], vs[hd].astype(BF16)) for hd in heads]
        kds = [(ks[hd] * jnp.exp(b_lasts[hd] - bs[hd])).astype(BF16) for hd in heads]
        upds = [_dot(vs[hd].T.astype(BF16), kds[hd]) for hd in heads]
        for hd in heads:
            st_ref[hd] = states[hd] * jnp.exp(b_lasts[hd]) + upds[hd]
            o = outs[hd]
            ms = jnp.mean(o * o, axis=-1, keepdims=True)
            o_ref[rows, sls[hd]] = o * lax.rsqrt(ms + RMS_EPS) * gn * gt_ref[rows, sls[hd]]
        return carry

    lax.fori_loop(0, n_chunks, chunk, 0)


def _hgrn(q, k, lf, v, gt, norm_g):
    B, S, hw = q.shape
    n_heads = hw // HGRN_DK
    tc = 512
    spec = pl.BlockSpec((None, tc, hw), lambda b, i: (b, i, 0))
    return pl.pallas_call(
        functools.partial(_hgrn_kernel, n_heads=n_heads, n_chunks=tc // HGRN_CHUNK),
        out_shape=jax.ShapeDtypeStruct((B, S, hw), F32),
        grid=(B, S // tc),
        in_specs=[spec, spec, spec, spec, spec, pl.BlockSpec((1, HGRN_DK), lambda b, i: (0, 0))],
        out_specs=spec,
        scratch_shapes=[pltpu.VMEM((n_heads, HGRN_DK, HGRN_DK), F32)],
        compiler_params=pltpu.CompilerParams(
            dimension_semantics=("arbitrary", "arbitrary"), vmem_limit_bytes=VMEM_LIMIT),
        name="hgrn",
    )(q, k, lf, v, gt, norm_g)


def _sc_move_rows(table, src, dst, n_out, chunk):
    M = src.shape[0]
    D = table.shape[1]
    n_workers = SC_CORES * SC_SUBCORES
    per_worker = M // n_workers
    n_chunks = per_worker // chunk
    assert per_worker * n_workers == M and n_chunks * chunk == per_worker and n_chunks % 2 == 0 and chunk % 8 == 0
    mesh = plsc.VectorSubcoreMesh(core_axis_name="c", subcore_axis_name="s")
    idx_t = pltpu.VMEM((chunk,), jnp.int32)
    row_t = pltpu.VMEM((chunk, D), table.dtype)
    sem_t = pltpu.SemaphoreType.DMA

    def body(table_hbm, src_hbm, dst_hbm, out_hbm, src_v, dst_v, rows_v, g_sem, s_sem):
        wid = lax.axis_index("s") * SC_CORES + lax.axis_index("c")
        base = wid * per_worker

        def offset(j):
            return pl.multiple_of(base + j * chunk, 8)

        def gather(b):
            return pltpu.make_async_copy(table_hbm.at[src_v[b]], rows_v[b], g_sem[b])

        def start_gather(j, b):
            pltpu.sync_copy(src_hbm.at[pl.ds(offset(j), chunk)], src_v[b])
            gather(b).start()

        def write_out(j, b):
            if dst_hbm is None:
                pltpu.sync_copy(rows_v[b], out_hbm.at[pl.ds(offset(j), chunk)])
            else:
                pltpu.sync_copy(dst_hbm.at[pl.ds(offset(j), chunk)], dst_v[b])
                pltpu.async_copy(rows_v[b], out_hbm.at[dst_v[b]], s_sem[b]).wait()

        start_gather(0, 0)

        @pl.loop(0, n_chunks, step=2)
        def _(j):
            for b in (0, 1):
                @pl.when(j + b + 1 < n_chunks)
                def _():
                    start_gather(j + b + 1, 1 - b)
                gather(b).wait()
                write_out(j + b, b)

    if dst is None:
        @functools.partial(pl.kernel, mesh=mesh, out_type=jax.ShapeDtypeStruct((n_out, D), table.dtype),
                           scratch_types=[idx_t, idx_t, row_t, row_t, sem_t, sem_t])
        def gather_kernel(table_hbm, src_hbm, out_hbm, s0, s1, r0, r1, g0, g1):
            body(table_hbm, src_hbm, None, out_hbm, (s0, s1), None, (r0, r1), (g0, g1), None)
        return gather_kernel(table, src)

    @functools.partial(pl.kernel, mesh=mesh, out_type=jax.ShapeDtypeStruct((n_out, D), table.dtype),
                       scratch_types=[idx_t, idx_t, idx_t, idx_t, row_t, row_t, sem_t, sem_t, sem_t, sem_t])
    def permute_kernel(table_hbm, src_hbm, dst_hbm, out_hbm, s0, s1, d0, d1, r0, r1, g0, g1, w0, w1):
        body(table_hbm, src_hbm, dst_hbm, out_hbm, (s0, s1), (d0, d1), (r0, r1), (g0, g1), (w0, w1))
    return permute_kernel(table, src, dst)


def _sc_gather(table, idx, chunk):
    return _sc_move_rows(table, idx, None, idx.shape[0], chunk)


def _sc_permute(table, src, dst, n_out, chunk):
    return _sc_move_rows(table, src, dst, n_out, chunk)


def _tile_layout(counts, bm, n_tiles):
    n_groups = counts.shape[0]
    padded = (counts + bm - 1) // bm * bm
    pad_end = jnp.cumsum(padded)
    tile_start = jnp.arange(n_tiles, dtype=jnp.int32) * bm
    tile_group = jnp.minimum(
        jnp.sum((pad_end[None, :] <= tile_start[:, None]).astype(jnp.int32), axis=1), n_groups - 1)
    n_used = (pad_end[-1] // bm).astype(jnp.int32).reshape(1)
    return pad_end - padded, tile_group.astype(jnp.int32), n_used


def _null_partial(rows):
    lane = lax.broadcasted_iota(jnp.int32, (rows, PART_W), 1)
    return jnp.where(lane < ATTN_HEAD_DIM, 0.0, -jnp.inf).astype(F32)


def _moba_sel_kernel(q_ref, km_ref, k_ref, v_ref, idx_ref, cnt_ref, own_ref, cnt_acc, *, n_blocks):
    j = pl.program_id(1)
    T = MOBA_BLOCK
    heads = range(ATTN_HEADS)
    qs = [q_ref[hd // 2] for hd in heads]
    gates = [_dot_nt(km_ref[hd], qs[hd], precision=HIGHEST) for hd in heads]
    blk = lax.broadcasted_iota(jnp.int32, gates[0].shape, 0)
    neg_inf = jnp.float32(-jnp.inf)
    gates = [jnp.where(blk < j, g, neg_inf) for g in gates]
    picks = [[] for _ in heads]
    for _ in range(MOBA_TOPK):
        ms = [jnp.max(g, axis=0, keepdims=True) for g in gates]
        firsts = [jnp.min(jnp.where(g == m, blk, n_blocks), axis=0, keepdims=True) for g, m in zip(gates, ms)]
        for hd in heads:
            picks[hd].append(jnp.where(ms[hd] > neg_inf, firsts[hd], -1))
        gates = [jnp.where(blk == f, neg_inf, g) for g, f in zip(gates, firsts)]

    @pl.when(j == 0)
    def _():
        cnt_acc[...] = jnp.zeros_like(cnt_acc)

    earlier = (lax.broadcasted_iota(jnp.int32, (T, T), 0) < lax.broadcasted_iota(jnp.int32, (T, T), 1)).astype(BF16)
    for hd in heads:
        onehots = [(blk == p).astype(F32) for p in picks[hd]]
        member = onehots[0] + onehots[1] + onehots[2]
        base = cnt_acc[hd] + _dot(member.astype(BF16), earlier)
        ranks = [jnp.sum(oh * base, axis=0, keepdims=True).astype(jnp.int32) for oh in onehots]
        idx_ref[hd] = jnp.concatenate(picks[hd] + ranks + [jnp.zeros((2, T), jnp.int32)], axis=0)
        total = cnt_acc[hd] + jnp.sum(member, axis=1, keepdims=True)
        cnt_acc[hd] = total
        cnt_ref[hd] = total.astype(jnp.int32)
    causal = lax.broadcasted_iota(jnp.int32, (T, T), 1) <= lax.broadcasted_iota(jnp.int32, (T, T), 0)
    scale = ATTN_HEAD_DIM ** -0.5
    ss = [jnp.where(causal, _dot_nt((qs[hd] * scale).astype(BF16), k_ref[hd]), neg_inf) for hd in heads]
    mx = [jnp.max(s, axis=1, keepdims=True) for s in ss]
    ps = [jnp.exp(s - m) for s, m in zip(ss, mx)]
    ls = [jnp.sum(p, axis=1, keepdims=True) for p in ps]
    accs = [_dot(ps[hd].astype(BF16), v_ref[hd]) for hd in heads]
    for hd in heads:
        lse = jnp.broadcast_to(mx[hd] + jnp.log(ls[hd]), (T, PART_W - ATTN_HEAD_DIM))
        own_ref[hd] = jnp.concatenate([accs[hd] / ls[hd], lse], axis=1)


def _moba_sel(aq, kmean, ak, av):
    B, H, S, hd = av.shape
    nb = S // MOBA_BLOCK
    T = MOBA_BLOCK
    blk = lambda b, j: (b, 0, j, 0)
    return pl.pallas_call(
        functools.partial(_moba_sel_kernel, n_blocks=nb),
        out_shape=(jax.ShapeDtypeStruct((B, H, 8, S), jnp.int32),
                   jax.ShapeDtypeStruct((B, H, nb, 1), jnp.int32),
                   jax.ShapeDtypeStruct((B, H, S, PART_W), F32)),
        grid=(B, nb),
        in_specs=[pl.BlockSpec((None, H // 2, T, 128), blk),
                  pl.BlockSpec((None, H, nb, 128), lambda b, j: (b, 0, 0, 0)),
                  pl.BlockSpec((None, H, T, 128), blk),
                  pl.BlockSpec((None, H, T, hd), blk)],
        out_specs=(pl.BlockSpec((None, H, 8, T), lambda b, j: (b, 0, 0, j)),
                   pl.BlockSpec((None, H, nb, 1), lambda b, j: (b, 0, 0, 0)),
                   pl.BlockSpec((None, H, T, PART_W), blk)),
        scratch_shapes=[pltpu.VMEM((H, nb, 1), F32)],
        compiler_params=pltpu.CompilerParams(
            dimension_semantics=("arbitrary", "arbitrary"), vmem_limit_bytes=VMEM_LIMIT),
        name="moba_sel",
    )(aq, kmean, ak, av)


def _moba_blk_kernel(tg_ref, nu_ref, q_ref, *refs):
    n = MOBA_TILES_PER_STEP
    k_refs, v_refs, o_ref = refs[:n], refs[n:2 * n], refs[2 * n]
    R = MOBA_ROWS
    t0 = pl.program_id(0) * n

    @pl.when(t0 < nu_ref[0])
    def _():
        scale = ATTN_HEAD_DIM ** -0.5
        ss = [_dot_nt((q_ref[j * R:(j + 1) * R, :] * scale).astype(BF16), k_refs[j][...]) for j in range(n)]
        ms = [jnp.max(s, axis=1, keepdims=True) for s in ss]
        ps = [jnp.exp(s - m) for s, m in zip(ss, ms)]
        ls = [jnp.sum(p, axis=1, keepdims=True) for p in ps]
        accs = [_dot(p.astype(BF16), v_refs[j][...]) for j, p in enumerate(ps)]
        null = _null_partial(R)
        for j in range(n):
            lse = jnp.broadcast_to(ms[j] + jnp.log(ls[j]), (R, PART_W - ATTN_HEAD_DIM))
            row = jnp.concatenate([accs[j] / ls[j], lse], axis=1)
            o_ref[j * R:(j + 1) * R, :] = jnp.where(t0 + j < nu_ref[0], row, null)

    @pl.when(t0 >= nu_ref[0])
    def _():
        o_ref[...] = _null_partial(n * R)


def _moba_blk(qs, tile_group, n_used, ak, av):
    B, H, S, hd = av.shape
    nb = S // MOBA_BLOCK
    R = MOBA_ROWS
    n = MOBA_TILES_PER_STEP
    n_tiles = qs.shape[0] // R
    assert n_tiles % n == 0
    kv = lambda j: (lambda i, tg, nu: (tg[i * n + j] // nb, tg[i * n + j] % nb, 0, 0))
    grid_spec = pltpu.PrefetchScalarGridSpec(
        num_scalar_prefetch=2,
        grid=(n_tiles // n,),
        in_specs=[pl.BlockSpec((n * R, 128), lambda i, tg, nu: (i, 0))]
        + [pl.BlockSpec((None, None, MOBA_BLOCK, 128), kv(j)) for j in range(n)]
        + [pl.BlockSpec((None, None, MOBA_BLOCK, hd), kv(j)) for j in range(n)],
        out_specs=pl.BlockSpec((n * R, PART_W), lambda i, tg, nu: (i, 0)),
    )
    k4 = ak.reshape(B * H, nb, MOBA_BLOCK, 128)
    v4 = av.reshape(B * H, nb, MOBA_BLOCK, hd)
    return pl.pallas_call(
        _moba_blk_kernel,
        out_shape=jax.ShapeDtypeStruct((n_tiles * R, PART_W), F32),
        grid_spec=grid_spec,
        compiler_params=pltpu.CompilerParams(
            dimension_semantics=("arbitrary",), vmem_limit_bytes=VMEM_LIMIT),
        name="moba_blk",
    )(tile_group, n_used, qs, *([k4] * n), *([v4] * n))


def _moba_merge_kernel(own_ref, pg_ref, g_ref, o_ref):
    hd = ATTN_HEAD_DIM
    rows = [own_ref[...]] + [pg_ref[s] for s in range(MOBA_TOPK)]
    lses = [pltpu.roll(r, hd, 1) for r in rows]
    top = lses[0]
    for z in lses[1:]:
        top = jnp.maximum(top, z)
    num = jnp.zeros_like(top)
    den = jnp.zeros_like(top)
    for r, z in zip(rows, lses):
        w = jnp.exp(z - top)
        num = num + w * r
        den = den + w
    o = (num / den)[:, :hd]
    ms = jnp.mean(o * o, axis=-1, keepdims=True)
    o_ref[...] = o * lax.rsqrt(ms + RMS_EPS) * g_ref[...]


def _moba_merge(own, pg, norm_g):
    n = own.shape[0]
    T = 512
    row = lambda i: (i, 0)
    return pl.pallas_call(
        _moba_merge_kernel,
        out_shape=jax.ShapeDtypeStruct((n, ATTN_HEAD_DIM), F32),
        grid=(n // T,),
        in_specs=[pl.BlockSpec((T, PART_W), row),
                  pl.BlockSpec((MOBA_TOPK, T, PART_W), lambda i: (0, i, 0)),
                  pl.BlockSpec((1, ATTN_HEAD_DIM), lambda i: (0, 0))],
        out_specs=pl.BlockSpec((T, ATTN_HEAD_DIM), row),
        compiler_params=pltpu.CompilerParams(
            dimension_semantics=("arbitrary",), vmem_limit_bytes=VMEM_LIMIT),
        name="moba_merge",
    )(own, pg, norm_g)


def _moba(aq, km, ak, av, norm_g):
    B, H, S, hd = av.shape
    nb = S // MOBA_BLOCK
    n_q = B * H * S
    kmp = km.reshape(B, nb, H // 2, 128)
    half = jnp.arange(128, dtype=jnp.int32) // hd
    kmean = jnp.stack([jnp.where(half == h % 2, kmp[:, :, h // 2, :], 0.0) for h in range(H)], axis=1)
    idx8, counts, own = _moba_sel(aq, kmean, ak, av)
    sel = idx8[:, :, 0:MOBA_TOPK, :].reshape(B * H, MOBA_TOPK, S)
    rank = idx8[:, :, MOBA_TOPK:2 * MOBA_TOPK, :].reshape(B * H, MOBA_TOPK, S)
    n_groups = B * H * nb
    n_tiles = (n_q * MOBA_TOPK) // MOBA_ROWS + n_groups
    pad_start, tile_group, n_used = _tile_layout(counts.reshape(-1), MOBA_ROWS, n_tiles)
    blocks = jnp.arange(nb, dtype=jnp.int32)[:, None]
    start = jnp.sum(jnp.where(sel[:, :, None, :] == blocks, pad_start.reshape(B * H, 1, nb, 1), 0), axis=2)
    a_ids = jnp.arange(n_q * MOBA_TOPK, dtype=jnp.int32).reshape(B * H, MOBA_TOPK, S)
    assert n_tiles * MOBA_ROWS >= n_q * MOBA_TOPK + MOBA_ROWS
    pos = jnp.where(sel >= 0, start + rank, n_used[0] * MOBA_ROWS + a_ids % MOBA_ROWS)
    bh = jnp.arange(B * H, dtype=jnp.int32)[:, None, None]
    t = jnp.arange(S, dtype=jnp.int32)[None, None, :]
    pair_row = jnp.broadcast_to((bh // H * (H // 2) + bh % H // 2) * S + t, pos.shape)
    qs = _sc_permute(aq.reshape(B * (H // 2) * S, 128), pair_row.reshape(-1), pos.reshape(-1),
                     n_tiles * MOBA_ROWS, 256)
    parts = _moba_blk(qs, tile_group, n_used, ak, av)
    pg = _sc_gather(parts, pos.transpose(1, 0, 2).reshape(-1), 256)
    o = _moba_merge(own.reshape(n_q, PART_W), pg.reshape(MOBA_TOPK, n_q, PART_W), norm_g)
    return o.reshape(B, H, S, hd)


def _mix_kernel(oh_ref, oa_ref, x_ref, g1_ref, sc2_ref, sh2_ref, n2_ref, wo_ref, wr_ref, br_ref,
                x1_ref, h2_ref, gw_ref, idx_ref, cnt_ref, cnt_acc):
    cat = jnp.concatenate([oh_ref[...]] + [oa_ref[hd] for hd in range(ATTN_HEADS)], axis=1)
    mix = _dot(cat.astype(BF16), wo_ref[...])
    x1 = x_ref[...] + g1_ref[...] * mix
    x1_ref[...] = x1
    ms = jnp.mean(x1 * x1, axis=-1, keepdims=True)
    h2 = x1 * lax.rsqrt(ms + RMS_EPS) * n2_ref[...]
    h2 = h2 * (1.0 + sc2_ref[...]) + sh2_ref[...]
    h2_ref[...] = _pack_bf16_pairs(h2)
    E = N_EXPERTS
    h_0 = h2.astype(BF16)
    r_1 = h2 - h_0.astype(F32)
    h_1 = r_1.astype(BF16)
    h_2 = (r_1 - h_1.astype(F32)).astype(BF16)
    wr = wr_ref[...]
    p_0 = _dot(h_0, wr)
    p_1 = _dot(h_1, wr[:, :2 * E])
    p_2 = _dot(h_2, wr[:, :E])
    logits = (p_0[:, :E] + (p_0[:, E:2 * E] + p_1[:, :E])
              + (p_0[:, 2 * E:] + p_1[:, E:] + p_2)) + br_ref[...]
    lane = lax.broadcasted_iota(jnp.int32, logits.shape, 1)
    neg_inf = jnp.float32(-jnp.inf)
    vals, idxs = [], []
    for _ in range(TOP_K):
        m = jnp.max(logits, axis=1, keepdims=True)
        first = jnp.min(jnp.where(logits == m, lane, N_EXPERTS), axis=1, keepdims=True)
        vals.append(m)
        idxs.append(first)
        logits = jnp.where(lane == first, neg_inf, logits)
    e = [jnp.exp(v - vals[0]) for v in vals]
    denom = e[0] + e[1] + e[2] + e[3]
    gw_ref[...] = jnp.concatenate([ei / denom for ei in e], axis=1)

    @pl.when((pl.program_id(0) == 0) & (pl.program_id(1) == 0))
    def _():
        cnt_acc[...] = jnp.zeros_like(cnt_acc)

    ranks, total = _group_ranks([(lane == ix).astype(F32) for ix in idxs], cnt_acc[...])
    idx_ref[...] = _columns_to_rows(idxs + ranks)
    cnt_acc[...] = total
    cnt_ref[...] = total.astype(jnp.int32)


def _mix(oh, oa, x, gate1, scale2, shift2, norm2_g, w_out_bf16, w_router, b_router):
    B, S, D = x.shape
    hw = oh.shape[-1]
    tm = 256
    row = lambda b, i: (b, i, 0)
    vec = lambda b, i: (b, 0, 0)
    const = lambda b, i: (0, 0)
    return pl.pallas_call(
        _mix_kernel,
        out_shape=(jax.ShapeDtypeStruct((B, S, D), F32),
                   jax.ShapeDtypeStruct((B, S, D // 2), jnp.int32),
                   jax.ShapeDtypeStruct((B, S, TOP_K), F32),
                   jax.ShapeDtypeStruct((B, 2 * TOP_K, S), jnp.int32),
                   jax.ShapeDtypeStruct((1, N_EXPERTS), jnp.int32)),
        grid=(B, S // tm),
        in_specs=[pl.BlockSpec((None, tm, hw), row),
                  pl.BlockSpec((None, ATTN_HEADS, tm, ATTN_HEAD_DIM), lambda b, i: (b, 0, i, 0)),
                  pl.BlockSpec((None, tm, D), row),
                  pl.BlockSpec((None, 1, D), vec),
                  pl.BlockSpec((None, 1, D), vec),
                  pl.BlockSpec((None, 1, D), vec),
                  pl.BlockSpec((1, D), const),
                  pl.BlockSpec((D, D), const),
                  pl.BlockSpec((D, 3 * N_EXPERTS), const),
                  pl.BlockSpec((1, N_EXPERTS), const)],
        out_specs=(pl.BlockSpec((None, tm, D), row),
                   pl.BlockSpec((None, tm, D // 2), row),
                   pl.BlockSpec((None, tm, TOP_K), row),
                   pl.BlockSpec((None, 2 * TOP_K, tm), lambda b, i: (b, 0, i)),
                   pl.BlockSpec((1, N_EXPERTS), const)),
        scratch_shapes=[pltpu.VMEM((1, N_EXPERTS), F32)],
        compiler_params=pltpu.CompilerParams(
            dimension_semantics=("arbitrary", "arbitrary"), vmem_limit_bytes=VMEM_LIMIT),
        name="mix",
    )(oh, oa, x, gate1, scale2, shift2, norm2_g, w_out_bf16, w_router, b_router)


def _moe_rows_kernel(be_ref, nu_ref, x_ref, wgu_ref, bgu_ref, wd_ref, bd_ref, y_ref, wgu16, wd16, *, d_ff):
    i = pl.program_id(0)

    @pl.when((i == 0) | (be_ref[i] != be_ref[jnp.maximum(i - 1, 0)]))
    def _():
        wgu16[...] = wgu_ref[...].astype(BF16)
        wd16[...] = wd_ref[...].astype(BF16)

    @pl.when(i < nu_ref[0])
    def _():
        gu = _dot(_unpack_bf16_pairs(x_ref[...]).astype(BF16), wgu16[...]) + bgu_ref[...]
        gate = jnp.minimum(gu[:, :d_ff], SWIGLU_LIMIT)
        up = jnp.clip(gu[:, d_ff:], -SWIGLU_LIMIT, SWIGLU_LIMIT)
        act = (up + 1.0) * gate * _sigmoid(SWIGLU_ALPHA * gate)
        y_ref[...] = _pack_bf16_pairs(_dot(act.astype(BF16), wd16[...]) + bd_ref[...])

    @pl.when(i >= nu_ref[0])
    def _():
        y_ref[...] = jnp.zeros_like(y_ref)


def _moe_rows(xs, blk_expert, n_used, wgu, bgu, wd, bd):
    D = 2 * xs.shape[1]
    bm = MOE_ROWS
    n_blk = xs.shape[0] // bm
    d_ff = wd.shape[1]
    wsel = lambda i, be, nu: (be[i], 0, 0)
    grid_spec = pltpu.PrefetchScalarGridSpec(
        num_scalar_prefetch=2,
        grid=(n_blk,),
        in_specs=[pl.BlockSpec((bm, D // 2), lambda i, be, nu: (i, 0)),
                  pl.BlockSpec((None, D, 2 * d_ff), wsel),
                  pl.BlockSpec((None, 1, 2 * d_ff), wsel),
                  pl.BlockSpec((None, d_ff, D), wsel),
                  pl.BlockSpec((None, 1, D), wsel)],
        out_specs=pl.BlockSpec((bm, D // 2), lambda i, be, nu: (i, 0)),
        scratch_shapes=[pltpu.VMEM((D, 2 * d_ff), BF16), pltpu.VMEM((d_ff, D), BF16)],
    )
    return pl.pallas_call(
        functools.partial(_moe_rows_kernel, d_ff=d_ff),
        out_shape=jax.ShapeDtypeStruct((n_blk * bm, D // 2), jnp.int32),
        grid_spec=grid_spec,
        compiler_params=pltpu.CompilerParams(
            dimension_semantics=("arbitrary",), vmem_limit_bytes=VMEM_LIMIT),
        name="moe_rows",
    )(blk_expert, n_used, xs, wgu, bgu.reshape(N_EXPERTS, 1, 2 * d_ff), wd, bd.reshape(N_EXPERTS, 1, D))


def _combine_rows_kernel(*refs):
    y_refs = refs[:TOP_K]
    gw_ref, x1_ref, g2_ref, fg_ref, o_ref = refs[TOP_K:]
    gw = gw_ref[...]
    y = gw[:, 0:1] * _unpack_bf16_pairs(y_refs[0][...])
    for kk in range(1, TOP_K):
        y = y + gw[:, kk:kk + 1] * _unpack_bf16_pairs(y_refs[kk][...])
    x2 = x1_ref[...] + g2_ref[...] * y
    ms = jnp.mean(x2 * x2, axis=-1, keepdims=True)
    o_ref[...] = x2 * lax.rsqrt(ms + RMS_EPS) * fg_ref[...]


def _combine_rows(yg, gates, x1, gate2, final_g):
    B, S, D = x1.shape
    T = B * S
    tm = 256
    steps = T // tm
    steps_per_batch = S // tm
    slot_spec = lambda kk: pl.BlockSpec((tm, D // 2), lambda i: (kk * steps + i, 0))
    return pl.pallas_call(
        _combine_rows_kernel,
        out_shape=jax.ShapeDtypeStruct((T, D), F32),
        grid=(steps,),
        in_specs=[slot_spec(kk) for kk in range(TOP_K)] + [
            pl.BlockSpec((tm, TOP_K), lambda i: (i, 0)),
            pl.BlockSpec((tm, D), lambda i: (i, 0)),
            pl.BlockSpec((None, 1, D), lambda i: (i // steps_per_batch, 0, 0)),
            pl.BlockSpec((1, D), lambda i: (0, 0))],
        out_specs=pl.BlockSpec((tm, D), lambda i: (i, 0)),
        compiler_params=pltpu.CompilerParams(
            dimension_semantics=("arbitrary",), vmem_limit_bytes=VMEM_LIMIT),
        name="combine_rows",
    )(*([yg] * TOP_K), gates, x1.reshape(T, D), gate2, final_g).reshape(B, S, D)


def _split_bf16x3(w):
    def top(v):
        return lax.bitcast_convert_type(lax.bitcast_convert_type(v, jnp.int32) & jnp.int32(-65536), F32)
    w0 = top(w)
    w1 = top(w - w0)
    w2 = w - w0 - w1
    return jnp.concatenate([w0, w1, w2], axis=1).astype(BF16)


def _rotary_tables(positions):
    half = ROT_DIM // 2
    inv_freq = jnp.exp(-math.log(ROPE_THETA) * jnp.arange(0, ROT_DIM, 2, dtype=F32) / ROT_DIM)
    d = jnp.arange(128, dtype=jnp.int32) % ATTN_HEAD_DIM
    freq = jnp.where(d < ROT_DIM, inv_freq[d % half], 0.0)
    sign = jnp.where(d < half, -1.0, 1.0)
    ang = positions.astype(F32)[:, :, None] * freq
    return jnp.cos(ang), jnp.sin(ang) * sign


def kernel(x, c, positions, w_ada, b_ada, norm1_g, w_in, hgrn_lb_logits, hgrn_norm_g, attn_norm_g,
           w_out, norm2_g, w_router, b_router, w_gate_up, b_gate_up, w_down, b_down, final_norm_g):
    B, S, D = x.shape
    T = B * S
    assert w_in.shape[0] == 1, "single-layer block: the final norm is fused into the combine step"
    l = 0
    ctab, stab = _rotary_tables(positions)
    lower_bounds = jnp.cumsum(jax.nn.softmax(hgrn_lb_logits.astype(F32), axis=0), axis=0)
    mod = _ada(c, w_ada[l], b_ada[l])
    shift1, scale1, gate1, shift2, scale2, gate2 = jnp.split(mod[:, None, :], N_MOD, axis=-1)
    q, k, lf, v, gt, aq, ak, av, km = _proj(
        x, scale1, shift1, norm1_g[l][None], w_in[l].astype(BF16), lower_bounds[l][None], ctab, stab)
    o_a = _moba(aq, km, ak, av, attn_norm_g[l][None])
    o_h = _hgrn(q, k, lf, v, gt, hgrn_norm_g[l][None])
    x1, h2, gates, idx8, counts = _mix(o_h, o_a, x, gate1, scale2, shift2, norm2_g[l][None],
                                       w_out[l].astype(BF16), _split_bf16x3(w_router[l]), b_router[l][None])
    n_blk = (T * TOP_K) // MOE_ROWS + N_EXPERTS
    pad_start, blk_expert, n_used = _tile_layout(counts.reshape(-1), MOE_ROWS, n_blk)
    experts = jnp.arange(N_EXPERTS, dtype=jnp.int32)[:, None]
    chosen = idx8[:, 0:TOP_K, :]
    pos = jnp.sum(jnp.where(chosen[:, :, None, :] == experts, pad_start[:, None], 0), axis=2) + idx8[:, TOP_K:, :]
    tok = jnp.broadcast_to(jnp.arange(B, dtype=jnp.int32)[:, None, None] * S
                           + jnp.arange(S, dtype=jnp.int32)[None, None, :], pos.shape)
    xs = _sc_permute(h2.reshape(T, D // 2), tok.reshape(-1), pos.reshape(-1), n_blk * MOE_ROWS, 64)
    y_sorted = _moe_rows(xs, blk_expert, n_used, w_gate_up[l], b_gate_up[l], w_down[l], b_down[l])
    yg = _sc_gather(y_sorted, pos.transpose(1, 0, 2).reshape(-1), 64)
    return _combine_rows(yg, gates.reshape(T, TOP_K), x1, gate2, final_norm_g[None])
```

```python
import functools
import math

import jax
import jax.numpy as jnp
from jax import lax
from jax.experimental import pallas as pl
from jax.experimental.pallas import tpu as pltpu
from jax.experimental.pallas import tpu_sc as plsc

F32 = jnp.float32
BF16 = jnp.bfloat16
HIGHEST = lax.Precision.HIGHEST

HGRN_DK = 128
HGRN_CHUNK = 64
ATTN_HEADS = 4
ATTN_HEAD_DIM = 64
ROT_DIM = ATTN_HEAD_DIM // 4
ROPE_THETA = 500000.0
MOBA_BLOCK = 256
MOBA_TOPK = 3
N_EXPERTS = 32
TOP_K = 4
SWIGLU_ALPHA = 1.702
SWIGLU_LIMIT = 7.0
N_MOD = 6
RMS_EPS = 1e-6

EXP_CLAMP = 80.0
MOE_ROWS = 512
MOBA_ROWS = 256
MOBA_TILES_PER_STEP = 8
PART_W = 128
VMEM_LIMIT = 56 * 1024 * 1024
SC_CORES = 2
SC_SUBCORES = 16


def _sigmoid(x):
    return 1.0 / (1.0 + jnp.exp(-x))


def _dot(a, b, **kw):
    return jnp.dot(a, b, preferred_element_type=F32, **kw)


def _dot_nt(a, b, **kw):
    return lax.dot_general(a, b, (((1,), (1,)), ((), ())), preferred_element_type=F32, **kw)


def _pack_bf16_pairs(x):
    w = x.shape[1] // 2
    bits = lax.bitcast_convert_type(x.astype(BF16).astype(F32), jnp.int32)
    return bits[:, w:] | lax.shift_right_logical(bits[:, :w], 16)


def _unpack_bf16_pairs(p):
    lo = lax.bitcast_convert_type(lax.shift_left(p, 16), F32)
    hi = lax.bitcast_convert_type(p & jnp.int32(-65536), F32)
    return jnp.concatenate([lo, hi], axis=1)


def _ada_kernel(c_ref, w_ref, b_ref, o_ref):
    c = c_ref[...]
    o_ref[...] = _dot(c * _sigmoid(c), w_ref[...], precision=HIGHEST) + b_ref[...]


def _ada(c, w_ada, b_ada):
    B, D = c.shape
    N = w_ada.shape[1]
    tn = N // 4
    c8 = jnp.zeros((8, D), F32).at[:B].set(c)
    out = pl.pallas_call(
        _ada_kernel,
        out_shape=jax.ShapeDtypeStruct((8, N), F32),
        grid=(N // tn,),
        in_specs=[pl.BlockSpec((8, D), lambda j: (0, 0)),
                  pl.BlockSpec((D, tn), lambda j: (0, j)),
                  pl.BlockSpec((1, tn), lambda j: (0, j))],
        out_specs=pl.BlockSpec((8, tn), lambda j: (0, j)),
        compiler_params=pltpu.CompilerParams(vmem_limit_bytes=VMEM_LIMIT),
        name="ada",
    )(c8, w_ada, b_ada.reshape(1, N))
    return out[:B]


def _proj_kernel(x_ref, sc_ref, sh_ref, g_ref, w_ref, lb_ref, ct_ref, st_ref,
                 q_ref, k_ref, lf_ref, v_ref, gt_ref, aq_ref, ak_ref, av_ref, km_ref,
                 *, hw, aw):
    x = x_ref[...]
    ms = jnp.mean(x * x, axis=-1, keepdims=True)
    h = x * lax.rsqrt(ms + RMS_EPS) * g_ref[...]
    h = h * (1.0 + sc_ref[...]) + sh_ref[...]
    proj = _dot(h.astype(BF16), w_ref[...])

    hq = proj[:, 0:hw]
    hf = proj[:, hw:2 * hw]
    hg = proj[:, 3 * hw:4 * hw]
    q_ref[...] = hq * _sigmoid(hq) * (HGRN_DK ** -0.5)
    lb = lb_ref[...]
    f = lb + (1.0 - lb) * _sigmoid(hf)
    k_ref[...] = 1.0 - f
    lf_ref[...] = jnp.log(f)
    v_ref[...] = proj[:, 2 * hw:3 * hw]
    gt_ref[...] = hg * _sigmoid(hg)

    ct = jnp.concatenate([ct_ref[...]] * (aw // 128), axis=1)
    st = jnp.concatenate([st_ref[...]] * (aw // 128), axis=1)
    lane = lax.broadcasted_iota(jnp.int32, ct.shape, 1) % ATTN_HEAD_DIM
    first_half = lane < (ROT_DIM // 2)

    def rot(t):
        partner = jnp.where(first_half, pltpu.roll(t, aw - ROT_DIM // 2, 1), pltpu.roll(t, ROT_DIM // 2, 1))
        return t * ct + partner * st

    base = 4 * hw
    aq = rot(proj[:, base:base + aw])
    ak = rot(proj[:, base + aw:base + 2 * aw])
    av = proj[:, base + 2 * aw:base + 3 * aw]
    km_ref[...] = jnp.mean(ak, axis=0, keepdims=True)
    lane128 = lax.broadcasted_iota(jnp.int32, (x.shape[0], 128), 1)
    for pair in range(ATTN_HEADS // 2):
        aq_ref[pair] = aq[:, pair * 128:(pair + 1) * 128]
    for hd in range(ATTN_HEADS):
        pair, half = divmod(hd, 2)
        in_head = (lane128 // ATTN_HEAD_DIM) == half
        ak_ref[hd] = jnp.where(in_head, ak[:, pair * 128:(pair + 1) * 128], 0.0).astype(BF16)
        av_ref[hd] = av[:, hd * ATTN_HEAD_DIM:(hd + 1) * ATTN_HEAD_DIM].astype(BF16)


def _proj(x, scale1, shift1, norm_g, w_in_bf16, lb, ctab, stab):
    B, S, D = x.shape
    hw = lb.shape[-1]
    aw = ATTN_HEADS * ATTN_HEAD_DIM
    tm = MOBA_BLOCK
    nb = S // MOBA_BLOCK
    n_proj = w_in_bf16.shape[1]
    row = lambda b, i: (b, i, 0)
    vec = lambda b, i: (b, 0, 0)
    head = lambda b, i: (b, 0, i, 0)
    out_shapes = (
        jax.ShapeDtypeStruct((B, S, hw), F32),
        jax.ShapeDtypeStruct((B, S, hw), F32),
        jax.ShapeDtypeStruct((B, S, hw), F32),
        jax.ShapeDtypeStruct((B, S, hw), F32),
        jax.ShapeDtypeStruct((B, S, hw), F32),
        jax.ShapeDtypeStruct((B, ATTN_HEADS // 2, S, 128), F32),
        jax.ShapeDtypeStruct((B, ATTN_HEADS, S, 128), BF16),
        jax.ShapeDtypeStruct((B, ATTN_HEADS, S, ATTN_HEAD_DIM), BF16),
        jax.ShapeDtypeStruct((B, nb, 1, aw), F32),
    )
    hspec = pl.BlockSpec((None, tm, hw), row)
    aspec = pl.BlockSpec((None, ATTN_HEADS, tm, ATTN_HEAD_DIM), head)
    return pl.pallas_call(
        functools.partial(_proj_kernel, hw=hw, aw=aw),
        out_shape=out_shapes,
        grid=(B, S // tm),
        in_specs=[pl.BlockSpec((None, tm, D), row),
                  pl.BlockSpec((None, 1, D), vec),
                  pl.BlockSpec((None, 1, D), vec),
                  pl.BlockSpec((1, D), lambda b, i: (0, 0)),
                  pl.BlockSpec((D, n_proj), lambda b, i: (0, 0)),
                  pl.BlockSpec((1, hw), lambda b, i: (0, 0)),
                  pl.BlockSpec((None, tm, 128), row),
                  pl.BlockSpec((None, tm, 128), row)],
        out_specs=(hspec, hspec, hspec, hspec, hspec,
                   pl.BlockSpec((None, ATTN_HEADS // 2, tm, 128), head),
                   pl.BlockSpec((None, ATTN_HEADS, tm, 128), head), aspec,
                   pl.BlockSpec((None, None, 1, aw), lambda b, i: (b, i, 0, 0))),
        compiler_params=pltpu.CompilerParams(
            dimension_semantics=("arbitrary", "arbitrary"), vmem_limit_bytes=VMEM_LIMIT),
        name="proj",
    )(x, scale1, shift1, norm_g, w_in_bf16, lb, ctab, stab)


def _hgrn_kernel(q_ref, k_ref, lf_ref, v_ref, gt_ref, gn_ref, o_ref, st_ref, *, n_heads, n_chunks):
    @pl.when(pl.program_id(1) == 0)
    def _():
        st_ref[...] = jnp.zeros_like(st_ref)

    C = HGRN_CHUNK
    r = lax.broadcasted_iota(jnp.int32, (C, C), 0)
    c = lax.broadcasted_iota(jnp.int32, (C, C), 1)
    tril = c <= r
    ltri = tril.astype(F32)
    gn = gn_ref[...]

    def chunk(ci, carry):
        r0 = pl.multiple_of(ci * C, C)
        rows = pl.ds(r0, C)
        b_all = _dot(ltri, lf_ref[rows, :], precision=HIGHEST)
        heads = range(n_heads)
        sls = [slice(hd * HGRN_DK, (hd + 1) * HGRN_DK) for hd in heads]
        bs = [b_all[:, sl] for sl in sls]
        b_lasts = [b[C - 1:C, :] for b in bs]
        qs = [q_ref[rows, sl] for sl in sls]
        ks = [k_ref[rows, sl] for sl in sls]
        vs = [v_ref[rows, sl] for sl in sls]
        states = [st_ref[hd] for hd in heads]
        o_inter = [_dot_nt((qs[hd] * jnp.exp(bs[hd])).astype(BF16), states[hd].astype(BF16)) for hd in heads]
        rhos = [0.5 * bl for bl in b_lasts]
        qas = [(qs[hd] * jnp.exp(jnp.minimum(bs[hd] - rhos[hd], EXP_CLAMP))).astype(BF16) for hd in heads]
        kbs = [(ks[hd] * jnp.exp(jnp.minimum(rhos[hd] - bs[hd], EXP_CLAMP))).astype(BF16) for hd in heads]
        scores = [jnp.where(tril, _dot_nt(qas[hd], kbs[hd]), 0.0).astype(BF16) for hd in heads]
        outs = [o_inter[hd] + _dot(scores[hd], vs[hd].astype(BF16)) for hd in heads]
        kds = [(ks[hd] * jnp.exp(b_lasts[hd] - bs[hd])).astype(BF16) for hd in heads]
        upds = [_dot(vs[hd].T.astype(BF16), kds[hd]) for hd in heads]
        for hd in heads:
            st_ref[hd] = states[hd] * jnp.exp(b_lasts[hd]) + upds[hd]
            o = outs[hd]
            ms = jnp.mean(o * o, axis=-1, keepdims=True)
            o_ref[rows, sls[hd]] = o * lax.rsqrt(ms + RMS_EPS) * gn * gt_ref[rows, sls[hd]]
        return carry

    lax.fori_loop(0, n_chunks, chunk, 0)


def _hgrn(q, k, lf, v, gt, norm_g):
    B, S, hw = q.shape
    n_heads = hw // HGRN_DK
    tc = 512
    spec = pl.BlockSpec((None, tc, hw), lambda b, i: (b, i, 0))
    return pl.pallas_call(
        functools.partial(_hgrn_kernel, n_heads=n_heads, n_chunks=tc // HGRN_CHUNK),
        out_shape=jax.ShapeDtypeStruct((B, S, hw), F32),
        grid=(B, S // tc),
        in_specs=[spec, spec, spec, spec, spec, pl.BlockSpec((1, HGRN_DK), lambda b, i: (0, 0))],
        out_specs=spec,
        scratch_shapes=[pltpu.VMEM((n_heads, HGRN_DK, HGRN_DK), F32)],
        compiler_params=pltpu.CompilerParams(
            dimension_semantics=("arbitrary", "arbitrary"), vmem_limit_bytes=VMEM_LIMIT),
        name="hgrn",
    )(q, k, lf, v, gt, norm_g)


def _sc_move_rows(table, src, dst, n_out, chunk):
    M = src.shape[0]
    D = table.shape[1]
    n_workers = SC_CORES * SC_SUBCORES
    per_worker = M // n_workers
    n_chunks = per_worker // chunk
    assert per_worker * n_workers == M and n_chunks * chunk == per_worker and n_chunks % 2 == 0 and chunk % 8 == 0
    mesh = plsc.VectorSubcoreMesh(core_axis_name="c", subcore_axis_name="s")
    idx_t = pltpu.VMEM((chunk,), jnp.int32)
    row_t = pltpu.VMEM((chunk, D), table.dtype)
    sem_t = pltpu.SemaphoreType.DMA

    def body(table_hbm, src_hbm, dst_hbm, out_hbm, src_v, dst_v, rows_v, g_sem, s_sem):
        wid = lax.axis_index("s") * SC_CORES + lax.axis_index("c")
        base = wid * per_worker

        def offset(j):
            return pl.multiple_of(base + j * chunk, 8)

        def gather(b):
            return pltpu.make_async_copy(table_hbm.at[src_v[b]], rows_v[b], g_sem[b])

        def start_gather(j, b):
            pltpu.sync_copy(src_hbm.at[pl.ds(offset(j), chunk)], src_v[b])
            gather(b).start()

        def write_out(j, b):
            if dst_hbm is None:
                pltpu.sync_copy(rows_v[b], out_hbm.at[pl.ds(offset(j), chunk)])
            else:
                pltpu.sync_copy(dst_hbm.at[pl.ds(offset(j), chunk)], dst_v[b])
                pltpu.async_copy(rows_v[b], out_hbm.at[dst_v[b]], s_sem[b]).wait()

        start_gather(0, 0)

        @pl.loop(0, n_chunks, step=2)
        def _(j):
            for b in (0, 1):
                @pl.when(j + b + 1 < n_chunks)
                def _():
                    start_gather(j + b + 1, 1 - b)
                gather(b).wait()
                write_out(j + b, b)

    if dst is None:
        @functools.partial(pl.kernel, mesh=mesh, out_type=jax.ShapeDtypeStruct((n_out, D), table.dtype),
                           scratch_types=[idx_t, idx_t, row_t, row_t, sem_t, sem_t])
        def gather_kernel(table_hbm, src_hbm, out_hbm, s0, s1, r0, r1, g0, g1):
            body(table_hbm, src_hbm, None, out_hbm, (s0, s1), None, (r0, r1), (g0, g1), None)
        return gather_kernel(table, src)

    @functools.partial(pl.kernel, mesh=mesh, out_type=jax.ShapeDtypeStruct((n_out, D), table.dtype),
                       scratch_types=[idx_t, idx_t, idx_t, idx_t, row_t, row_t, sem_t, sem_t, sem_t, sem_t])
    def permute_kernel(table_hbm, src_hbm, dst_hbm, out_hbm, s0, s1, d0, d1, r0, r1, g0, g1, w0, w1):
        body(table_hbm, src_hbm, dst_hbm, out_hbm, (s0, s1), (d0, d1), (r0, r1), (g0, g1), (w0, w1))
    return permute_kernel(table, src, dst)


def _sc_gather(table, idx, chunk):
    return _sc_move_rows(table, idx, None, idx.shape[0], chunk)


def _sc_permute(table, src, dst, n_out, chunk):
    return _sc_move_rows(table, src, dst, n_out, chunk)


def _tile_layout(counts, bm, n_tiles):
    n_groups = counts.shape[0]
    padded = (counts + bm - 1) // bm * bm
    pad_end = jnp.cumsum(padded)
    tile_start = jnp.arange(n_tiles, dtype=jnp.int32) * bm
    tile_group = jnp.minimum(
        jnp.sum((pad_end[None, :] <= tile_start[:, None]).astype(jnp.int32), axis=1), n_groups - 1)
    n_used = (pad_end[-1] // bm).astype(jnp.int32).reshape(1)
    return pad_end - padded, tile_group.astype(jnp.int32), n_used


def _null_partial(rows):
    lane = lax.broadcasted_iota(jnp.int32, (rows, PART_W), 1)
    return jnp.where(lane < ATTN_HEAD_DIM, 0.0, -jnp.inf).astype(F32)


def _moba_sel_kernel(q_ref, km_ref, k_ref, v_ref, idx_ref, cnt_ref, own_ref, cnt_acc, *, n_blocks):
    j = pl.program_id(1)
    T = MOBA_BLOCK
    heads = range(ATTN_HEADS)
    qs = [q_ref[hd // 2] for hd in heads]
    gates = [_dot_nt(km_ref[hd], qs[hd], precision=HIGHEST) for hd in heads]
    blk = lax.broadcasted_iota(jnp.int32, gates[0].shape, 0)
    neg_inf = jnp.float32(-jnp.inf)
    gates = [jnp.where(blk < j, g, neg_inf) for g in gates]
    picks = [[] for _ in heads]
    for _ in range(MOBA_TOPK):
        ms = [jnp.max(g, axis=0, keepdims=True) for g in gates]
        firsts = [jnp.min(jnp.where(g == m, blk, n_blocks), axis=0, keepdims=True) for g, m in zip(gates, ms)]
        for hd in heads:
            picks[hd].append(jnp.where(ms[hd] > neg_inf, firsts[hd], -1))
        gates = [jnp.where(blk == f, neg_inf, g) for g, f in zip(gates, firsts)]

    @pl.when(j == 0)
    def _():
        cnt_acc[...] = jnp.zeros_like(cnt_acc)

    earlier = (lax.broadcasted_iota(jnp.int32, (T, T), 0) < lax.broadcasted_iota(jnp.int32, (T, T), 1)).astype(BF16)
    for hd in heads:
        onehots = [(blk == p).astype(F32) for p in picks[hd]]
        member = onehots[0] + onehots[1] + onehots[2]
        base = cnt_acc[hd] + _dot(member.astype(BF16), earlier)
        ranks = [jnp.sum(oh * base, axis=0, keepdims=True).astype(jnp.int32) for oh in onehots]
        idx_ref[hd] = jnp.concatenate(picks[hd] + ranks + [jnp.zeros((2, T), jnp.int32)], axis=0)
        total = cnt_acc[hd] + jnp.sum(member, axis=1, keepdims=True)
        cnt_acc[hd] = total
        cnt_ref[hd] = total.astype(jnp.int32)
    causal = lax.broadcasted_iota(jnp.int32, (T, T), 1) <= lax.broadcasted_iota(jnp.int32, (T, T), 0)
    scale = ATTN_HEAD_DIM ** -0.5
    ss = [jnp.where(causal, _dot_nt((qs[hd] * scale).astype(BF16), k_ref[hd]), neg_inf) for hd in heads]
    mx = [jnp.max(s, axis=1, keepdims=True) for s in ss]
    ps = [jnp.exp(s - m) for s, m in zip(ss, mx)]
    ls = [jnp.sum(p, axis=1, keepdims=True) for p in ps]
    accs = [_dot(ps[hd].astype(BF16), v_ref[hd]) for hd in heads]
    for hd in heads:
        lse = jnp.broadcast_to(mx[hd] + jnp.log(ls[hd]), (T, PART_W - ATTN_HEAD_DIM))
        own_ref[hd] = jnp.concatenate([accs[hd] / ls[hd], lse], axis=1)


def _moba_sel(aq, kmean, ak, av):
    B, H, S, hd = av.shape
    nb = S // MOBA_BLOCK
    T = MOBA_BLOCK
    blk = lambda b, j: (b, 0, j, 0)
    return pl.pallas_call(
        functools.partial(_moba_sel_kernel, n_blocks=nb),
        out_shape=(jax.ShapeDtypeStruct((B, H, 8, S), jnp.int32),
                   jax.ShapeDtypeStruct((B, H, nb, 1), jnp.int32),
                   jax.ShapeDtypeStruct((B, H, S, PART_W), F32)),
        grid=(B, nb),
        in_specs=[pl.BlockSpec((None, H // 2, T, 128), blk),
                  pl.BlockSpec((None, H, nb, 128), lambda b, j: (b, 0, 0, 0)),
                  pl.BlockSpec((None, H, T, 128), blk),
                  pl.BlockSpec((None, H, T, hd), blk)],
        out_specs=(pl.BlockSpec((None, H, 8, T), lambda b, j: (b, 0, 0, j)),
                   pl.BlockSpec((None, H, nb, 1), lambda b, j: (b, 0, 0, 0)),
                   pl.BlockSpec((None, H, T, PART_W), blk)),
        scratch_shapes=[pltpu.VMEM((H, nb, 1), F32)],
        compiler_params=pltpu.CompilerParams(
            dimension_semantics=("arbitrary", "arbitrary"), vmem_limit_bytes=VMEM_LIMIT),
        name="moba_sel",
    )(aq, kmean, ak, av)


def _moba_blk_kernel(tg_ref, nu_ref, q_ref, *refs):
    n = MOBA_TILES_PER_STEP
    k_refs, v_refs, o_ref = refs[:n], refs[n:2 * n], refs[2 * n]
    R = MOBA_ROWS
    t0 = pl.program_id(0) * n

    @pl.when(t0 < nu_ref[0])
    def _():
        scale = ATTN_HEAD_DIM ** -0.5
        ss = [_dot_nt((q_ref[j * R:(j + 1) * R, :] * scale).astype(BF16), k_refs[j][...]) for j in range(n)]
        ms = [jnp.max(s, axis=1, keepdims=True) for s in ss]
        ps = [jnp.exp(s - m) for s, m in zip(ss, ms)]
        ls = [jnp.sum(p, axis=1, keepdims=True) for p in ps]
        accs = [_dot(p.astype(BF16), v_refs[j][...]) for j, p in enumerate(ps)]
        null = _null_partial(R)
        for j in range(n):
            lse = jnp.broadcast_to(ms[j] + jnp.log(ls[j]), (R, PART_W - ATTN_HEAD_DIM))
            row = jnp.concatenate([accs[j] / ls[j], lse], axis=1)
            o_ref[j * R:(j + 1) * R, :] = jnp.where(t0 + j < nu_ref[0], row, null)

    @pl.when(t0 >= nu_ref[0])
    def _():
        o_ref[...] = _null_partial(n * R)


def _moba_blk(qs, tile_group, n_used, ak, av):
    B, H, S, hd = av.shape
    nb = S // MOBA_BLOCK
    R = MOBA_ROWS
    n = MOBA_TILES_PER_STEP
    n_tiles = qs.shape[0] // R
    assert n_tiles % n == 0
    kv = lambda j: (lambda i, tg, nu: (tg[i * n + j] // nb, tg[i * n + j] % nb, 0, 0))
    grid_spec = pltpu.PrefetchScalarGridSpec(
        num_scalar_prefetch=2,
        grid=(n_tiles // n,),
        in_specs=[pl.BlockSpec((n * R, 128), lambda i, tg, nu: (i, 0))]
        + [pl.BlockSpec((None, None, MOBA_BLOCK, 128), kv(j)) for j in range(n)]
        + [pl.BlockSpec((None, None, MOBA_BLOCK, hd), kv(j)) for j in range(n)],
        out_specs=pl.BlockSpec((n * R, PART_W), lambda i, tg, nu: (i, 0)),
    )
    k4 = ak.reshape(B * H, nb, MOBA_BLOCK, 128)
    v4 = av.reshape(B * H, nb, MOBA_BLOCK, hd)
    return pl.pallas_call(
        _moba_blk_kernel,
        out_shape=jax.ShapeDtypeStruct((n_tiles * R, PART_W), F32),
        grid_spec=grid_spec,
        compiler_params=pltpu.CompilerParams(
            dimension_semantics=("arbitrary",), vmem_limit_bytes=VMEM_LIMIT),
        name="moba_blk",
    )(tile_group, n_used, qs, *([k4] * n), *([v4] * n))


def _moba_merge_kernel(own_ref, pg_ref, g_ref, o_ref):
    hd = ATTN_HEAD_DIM
    rows = [own_ref[...]] + [pg_ref[s] for s in range(MOBA_TOPK)]
    lses = [pltpu.roll(r, hd, 1) for r in rows]
    top = lses[0]
    for z in lses[1:]:
        top = jnp.maximum(top, z)
    num = jnp.zeros_like(top)
    den = jnp.zeros_like(top)
    for r, z in zip(rows, lses):
        w = jnp.exp(z - top)
        num = num + w * r
        den = den + w
    o = (num / den)[:, :hd]
    ms = jnp.mean(o * o, axis=-1, keepdims=True)
    o_ref[...] = o * lax.rsqrt(ms + RMS_EPS) * g_ref[...]


def _moba_merge(own, pg, norm_g):
    n = own.shape[0]
    T = 512
    row = lambda i: (i, 0)
    return pl.pallas_call(
        _moba_merge_kernel,
        out_shape=jax.ShapeDtypeStruct((n, ATTN_HEAD_DIM), F32),
        grid=(n // T,),
        in_specs=[pl.BlockSpec((T, PART_W), row),
                  pl.BlockSpec((MOBA_TOPK, T, PART_W), lambda i: (0, i, 0)),
                  pl.BlockSpec((1, ATTN_HEAD_DIM), lambda i: (0, 0))],
        out_specs=pl.BlockSpec((T, ATTN_HEAD_DIM), row),
        compiler_params=pltpu.CompilerParams(
            dimension_semantics=("arbitrary",), vmem_limit_bytes=VMEM_LIMIT),
        name="moba_merge",
    )(own, pg, norm_g)


def _moba(aq, km, ak, av, norm_g):
    B, H, S, hd = av.shape
    nb = S // MOBA_BLOCK
    n_q = B * H * S
    kmp = km.reshape(B, nb, H // 2, 128)
    half = jnp.arange(128, dtype=jnp.int32) // hd
    kmean = jnp.stack([jnp.where(half == h % 2, kmp[:, :, h // 2, :], 0.0) for h in range(H)], axis=1)
    idx8, counts, own = _moba_sel(aq, kmean, ak, av)
    sel = idx8[:, :, 0:MOBA_TOPK, :].reshape(B * H, MOBA_TOPK, S)
    rank = idx8[:, :, MOBA_TOPK:2 * MOBA_TOPK, :].reshape(B * H, MOBA_TOPK, S)
    n_groups = B * H * nb
    n_tiles = (n_q * MOBA_TOPK) // MOBA_ROWS + n_groups
    pad_start, tile_group, n_used = _tile_layout(counts.reshape(-1), MOBA_ROWS, n_tiles)
    blocks = jnp.arange(nb, dtype=jnp.int32)[:, None]
    start = jnp.sum(jnp.where(sel[:, :, None, :] == blocks, pad_start.reshape(B * H, 1, nb, 1), 0), axis=2)
    a_ids = jnp.arange(n_q * MOBA_TOPK, dtype=jnp.int32).reshape(B * H, MOBA_TOPK, S)
    assert n_tiles * MOBA_ROWS >= n_q * MOBA_TOPK + MOBA_ROWS
    pos = jnp.where(sel >= 0, start + rank, n_used[0] * MOBA_ROWS + a_ids % MOBA_ROWS)
    bh = jnp.arange(B * H, dtype=jnp.int32)[:, None, None]
    t = jnp.arange(S, dtype=jnp.int32)[None, None, :]
    pair_row = jnp.broadcast_to((bh // H * (H // 2) + bh % H // 2) * S + t, pos.shape)
    qs = _sc_permute(aq.reshape(B * (H // 2) * S, 128), pair_row.reshape(-1), pos.reshape(-1),
                     n_tiles * MOBA_ROWS, 256)
    parts = _moba_blk(qs, tile_group, n_used, ak, av)
    pg = _sc_gather(parts, pos.transpose(1, 0, 2).reshape(-1), 256)
    o = _moba_merge(own.reshape(n_q, PART_W), pg.reshape(MOBA_TOPK, n_q, PART_W), norm_g)
    return o.reshape(B, H, S, hd)


def _mix_kernel(oh_ref, oa_ref, x_ref, g1_ref, sc2_ref, sh2_ref, n2_ref, wo_ref, wr_ref, br_ref,
                x1_ref, h2_ref, gw_ref, idx_ref, cnt_ref, cnt_acc):
    cat = jnp.concatenate([oh_ref[...]] + [oa_ref[hd] for hd in range(ATTN_HEADS)], axis=1)
    mix = _dot(cat.astype(BF16), wo_ref[...])
    x1 = x_ref[...] + g1_ref[...] * mix
    x1_ref[...] = x1
    ms = jnp.mean(x1 * x1, axis=-1, keepdims=True)
    h2 = x1 * lax.rsqrt(ms + RMS_EPS) * n2_ref[...]
    h2 = h2 * (1.0 + sc2_ref[...]) + sh2_ref[...]
    h2_ref[...] = _pack_bf16_pairs(h2)
    E = N_EXPERTS
    tm = h2.shape[0]
    h_0 = h2.astype(BF16)
    r_1 = h2 - h_0.astype(F32)
    h_1 = r_1.astype(BF16)
    h_2 = (r_1 - h_1.astype(F32)).astype(BF16)
    wt = wr_ref[...]
    p_0 = _dot_nt(wt, h_0)
    p_1 = _dot_nt(wt[:2 * E], h_1)
    p_2 = _dot_nt(wt[:E], h_2)
    logits = (p_0[:E] + (p_0[E:2 * E] + p_1[:E]) + (p_0[2 * E:] + p_1[E:] + p_2)) + br_ref[...]
    ex = lax.broadcasted_iota(jnp.int32, logits.shape, 0)
    neg_inf = jnp.float32(-jnp.inf)
    vals, idxs = [], []
    for _ in range(TOP_K):
        m = jnp.max(logits, axis=0, keepdims=True)
        first = jnp.min(jnp.where(logits == m, ex, E), axis=0, keepdims=True)
        vals.append(m)
        idxs.append(first)
        logits = jnp.where(ex == first, neg_inf, logits)
    e = [jnp.exp(v - vals[0]) for v in vals]
    denom = e[0] + e[1] + e[2] + e[3]
    gate_rows = jnp.concatenate([ei / denom for ei in e] + [jnp.zeros((128 - TOP_K, tm), F32)], axis=0)
    gw_ref[...] = gate_rows.T[:, :TOP_K]

    @pl.when((pl.program_id(0) == 0) & (pl.program_id(1) == 0))
    def _():
        cnt_acc[...] = jnp.zeros_like(cnt_acc)

    earlier = (lax.broadcasted_iota(jnp.int32, (tm, tm), 0) < lax.broadcasted_iota(jnp.int32, (tm, tm), 1)).astype(BF16)
    onehots = [(ex == ix).astype(F32) for ix in idxs]
    member = onehots[0] + onehots[1] + onehots[2] + onehots[3]
    base = cnt_acc[...] + _dot(member.astype(BF16), earlier)
    ranks = [jnp.sum(oh * base, axis=0, keepdims=True).astype(jnp.int32) for oh in onehots]
    idx_ref[...] = jnp.concatenate(idxs + ranks, axis=0)
    total = cnt_acc[...] + jnp.sum(member, axis=1, keepdims=True)
    cnt_acc[...] = total
    cnt_ref[...] = total.astype(jnp.int32)


def _mix(oh, oa, x, gate1, scale2, shift2, norm2_g, w_out_bf16, w_router, b_router):
    B, S, D = x.shape
    hw = oh.shape[-1]
    tm = 256
    row = lambda b, i: (b, i, 0)
    vec = lambda b, i: (b, 0, 0)
    const = lambda b, i: (0, 0)
    return pl.pallas_call(
        _mix_kernel,
        out_shape=(jax.ShapeDtypeStruct((B, S, D), F32),
                   jax.ShapeDtypeStruct((B, S, D // 2), jnp.int32),
                   jax.ShapeDtypeStruct((B, S, TOP_K), F32),
                   jax.ShapeDtypeStruct((B, 2 * TOP_K, S), jnp.int32),
                   jax.ShapeDtypeStruct((N_EXPERTS, 1), jnp.int32)),
        grid=(B, S // tm),
        in_specs=[pl.BlockSpec((None, tm, hw), row),
                  pl.BlockSpec((None, ATTN_HEADS, tm, ATTN_HEAD_DIM), lambda b, i: (b, 0, i, 0)),
                  pl.BlockSpec((None, tm, D), row),
                  pl.BlockSpec((None, 1, D), vec),
                  pl.BlockSpec((None, 1, D), vec),
                  pl.BlockSpec((None, 1, D), vec),
                  pl.BlockSpec((1, D), const),
                  pl.BlockSpec((D, D), const),
                  pl.BlockSpec((3 * N_EXPERTS, D), const),
                  pl.BlockSpec((N_EXPERTS, 1), const)],
        out_specs=(pl.BlockSpec((None, tm, D), row),
                   pl.BlockSpec((None, tm, D // 2), row),
                   pl.BlockSpec((None, tm, TOP_K), row),
                   pl.BlockSpec((None, 2 * TOP_K, tm), lambda b, i: (b, 0, i)),
                   pl.BlockSpec((N_EXPERTS, 1), const)),
        scratch_shapes=[pltpu.VMEM((N_EXPERTS, 1), F32)],
        compiler_params=pltpu.CompilerParams(
            dimension_semantics=("arbitrary", "arbitrary"), vmem_limit_bytes=VMEM_LIMIT),
        name="mix",
    )(oh, oa, x, gate1, scale2, shift2, norm2_g, w_out_bf16, w_router, b_router)


def _moe_rows_kernel(be_ref, nu_ref, x_ref, wgu_ref, bgu_ref, wd_ref, bd_ref, y_ref, wgu16, wd16, *, d_ff):
    i = pl.program_id(0)

    @pl.when((i == 0) | (be_ref[i] != be_ref[jnp.maximum(i - 1, 0)]))
    def _():
        wgu16[...] = wgu_ref[...].astype(BF16)
        wd16[...] = wd_ref[...].astype(BF16)

    @pl.when(i < nu_ref[0])
    def _():
        gu = _dot(_unpack_bf16_pairs(x_ref[...]).astype(BF16), wgu16[...]) + bgu_ref[...]
        gate = jnp.minimum(gu[:, :d_ff], SWIGLU_LIMIT)
        up = jnp.clip(gu[:, d_ff:], -SWIGLU_LIMIT, SWIGLU_LIMIT)
        act = (up + 1.0) * gate * _sigmoid(SWIGLU_ALPHA * gate)
        y_ref[...] = _pack_bf16_pairs(_dot(act.astype(BF16), wd16[...]) + bd_ref[...])

    @pl.when(i >= nu_ref[0])
    def _():
        y_ref[...] = jnp.zeros_like(y_ref)


def _moe_rows(xs, blk_expert, n_used, wgu, bgu, wd, bd):
    D = 2 * xs.shape[1]
    bm = MOE_ROWS
    n_blk = xs.shape[0] // bm
    d_ff = wd.shape[1]
    wsel = lambda i, be, nu: (be[i], 0, 0)
    grid_spec = pltpu.PrefetchScalarGridSpec(
        num_scalar_prefetch=2,
        grid=(n_blk,),
        in_specs=[pl.BlockSpec((bm, D // 2), lambda i, be, nu: (i, 0)),
                  pl.BlockSpec((None, D, 2 * d_ff), wsel),
                  pl.BlockSpec((None, 1, 2 * d_ff), wsel),
                  pl.BlockSpec((None, d_ff, D), wsel),
                  pl.BlockSpec((None, 1, D), wsel)],
        out_specs=pl.BlockSpec((bm, D // 2), lambda i, be, nu: (i, 0)),
        scratch_shapes=[pltpu.VMEM((D, 2 * d_ff), BF16), pltpu.VMEM((d_ff, D), BF16)],
    )
    return pl.pallas_call(
        functools.partial(_moe_rows_kernel, d_ff=d_ff),
        out_shape=jax.ShapeDtypeStruct((n_blk * bm, D // 2), jnp.int32),
        grid_spec=grid_spec,
        compiler_params=pltpu.CompilerParams(
            dimension_semantics=("arbitrary",), vmem_limit_bytes=VMEM_LIMIT),
        name="moe_rows",
    )(blk_expert, n_used, xs, wgu, bgu.reshape(N_EXPERTS, 1, 2 * d_ff), wd, bd.reshape(N_EXPERTS, 1, D))


def _combine_rows_kernel(*refs):
    y_refs = refs[:TOP_K]
    gw_ref, x1_ref, g2_ref, fg_ref, o_ref = refs[TOP_K:]
    gw = gw_ref[...]
    y = gw[:, 0:1] * _unpack_bf16_pairs(y_refs[0][...])
    for kk in range(1, TOP_K):
        y = y + gw[:, kk:kk + 1] * _unpack_bf16_pairs(y_refs[kk][...])
    x2 = x1_ref[...] + g2_ref[...] * y
    ms = jnp.mean(x2 * x2, axis=-1, keepdims=True)
    o_ref[...] = x2 * lax.rsqrt(ms + RMS_EPS) * fg_ref[...]


def _combine_rows(yg, gates, x1, gate2, final_g):
    B, S, D = x1.shape
    T = B * S
    tm = 256
    steps = T // tm
    steps_per_batch = S // tm
    slot_spec = lambda kk: pl.BlockSpec((tm, D // 2), lambda i: (kk * steps + i, 0))
    return pl.pallas_call(
        _combine_rows_kernel,
        out_shape=jax.ShapeDtypeStruct((T, D), F32),
        grid=(steps,),
        in_specs=[slot_spec(kk) for kk in range(TOP_K)] + [
            pl.BlockSpec((tm, TOP_K), lambda i: (i, 0)),
            pl.BlockSpec((tm, D), lambda i: (i, 0)),
            pl.BlockSpec((None, 1, D), lambda i: (i // steps_per_batch, 0, 0)),
            pl.BlockSpec((1, D), lambda i: (0, 0))],
        out_specs=pl.BlockSpec((tm, D), lambda i: (i, 0)),
        compiler_params=pltpu.CompilerParams(
            dimension_semantics=("arbitrary",), vmem_limit_bytes=VMEM_LIMIT),
        name="combine_rows",
    )(*([yg] * TOP_K), gates, x1.reshape(T, D), gate2, final_g).reshape(B, S, D)


def _split_bf16x3(w):
    def top(v):
        return lax.bitcast_convert_type(lax.bitcast_convert_type(v, jnp.int32) & jnp.int32(-65536), F32)
    w0 = top(w)
    w1 = top(w - w0)
    w2 = w - w0 - w1
    return jnp.concatenate([w0, w1, w2], axis=1).astype(BF16).T


def _rotary_tables(positions):
    half = ROT_DIM // 2
    inv_freq = jnp.exp(-math.log(ROPE_THETA) * jnp.arange(0, ROT_DIM, 2, dtype=F32) / ROT_DIM)
    d = jnp.arange(128, dtype=jnp.int32) % ATTN_HEAD_DIM
    freq = jnp.where(d < ROT_DIM, inv_freq[d % half], 0.0)
    sign = jnp.where(d < half, -1.0, 1.0)
    ang = positions.astype(F32)[:, :, None] * freq
    return jnp.cos(ang), jnp.sin(ang) * sign


def kernel(x, c, positions, w_ada, b_ada, norm1_g, w_in, hgrn_lb_logits, hgrn_norm_g, attn_norm_g,
           w_out, norm2_g, w_router, b_router, w_gate_up, b_gate_up, w_down, b_down, final_norm_g):
    B, S, D = x.shape
    T = B * S
    assert w_in.shape[0] == 1, "single-layer block: the final norm is fused into the combine step"
    l = 0
    ctab, stab = _rotary_tables(positions)
    lower_bounds = jnp.cumsum(jax.nn.softmax(hgrn_lb_logits.astype(F32), axis=0), axis=0)
    mod = _ada(c, w_ada[l], b_ada[l])
    shift1, scale1, gate1, shift2, scale2, gate2 = jnp.split(mod[:, None, :], N_MOD, axis=-1)
    q, k, lf, v, gt, aq, ak, av, km = _proj(
        x, scale1, shift1, norm1_g[l][None], w_in[l].astype(BF16), lower_bounds[l][None], ctab, stab)
    o_a = _moba(aq, km, ak, av, attn_norm_g[l][None])
    o_h = _hgrn(q, k, lf, v, gt, hgrn_norm_g[l][None])
    x1, h2, gates, idx8, counts = _mix(o_h, o_a, x, gate1, scale2, shift2, norm2_g[l][None],
                                       w_out[l].astype(BF16), _split_bf16x3(w_router[l]), b_router[l][:, None])
    n_blk = (T * TOP_K) // MOE_ROWS + N_EXPERTS
    pad_start, blk_expert, n_used = _tile_layout(counts.reshape(-1), MOE_ROWS, n_blk)
    experts = jnp.arange(N_EXPERTS, dtype=jnp.int32)[:, None]
    chosen = idx8[:, 0:TOP_K, :]
    pos = jnp.sum(jnp.where(chosen[:, :, None, :] == experts, pad_start[:, None], 0), axis=2) + idx8[:, TOP_K:, :]
    tok = jnp.broadcast_to(jnp.arange(B, dtype=jnp.int32)[:, None, None] * S
                           + jnp.arange(S, dtype=jnp.int32)[None, None, :], pos.shape)
    xs = _sc_permute(h2.reshape(T, D // 2), tok.reshape(-1), pos.reshape(-1), n_blk * MOE_ROWS, 64)
    y_sorted = _moe_rows(xs, blk_expert, n_used, w_gate_up[l], b_gate_up[l], w_down[l], b_down[l])
    yg = _sc_gather(y_sorted, pos.transpose(1, 0, 2).reshape(-1), 64)
    return _combine_rows(yg, gates.reshape(T, TOP_K), x1, gate2, final_norm_g[None])
```

```python
import functools
import math

import jax
import jax.numpy as jnp
from jax import lax
from jax.experimental import pallas as pl
from jax.experimental.pallas import tpu as pltpu
from jax.experimental.pallas import tpu_sc as plsc

F32 = jnp.float32
BF16 = jnp.bfloat16
HIGHEST = lax.Precision.HIGHEST

HGRN_DK = 128
HGRN_CHUNK = 64
ATTN_HEADS = 4
ATTN_HEAD_DIM = 64
ROT_DIM = ATTN_HEAD_DIM // 4
ROPE_THETA = 500000.0
MOBA_BLOCK = 256
MOBA_TOPK = 3
N_EXPERTS = 32
TOP_K = 4
SWIGLU_ALPHA = 1.702
SWIGLU_LIMIT = 7.0
N_MOD = 6
RMS_EPS = 1e-6

EXP_CLAMP = 80.0
MOE_ROWS = 512
MOBA_ROWS = 256
MOBA_TILES_PER_STEP = 8
PART_W = 128
VMEM_LIMIT = 56 * 1024 * 1024
SC_CORES = 2
SC_SUBCORES = 16


def _sigmoid(x):
    return 1.0 / (1.0 + jnp.exp(-x))


def _dot(a, b, **kw):
    return jnp.dot(a, b, preferred_element_type=F32, **kw)


def _dot_nt(a, b, **kw):
    return lax.dot_general(a, b, (((1,), (1,)), ((), ())), preferred_element_type=F32, **kw)


def _pack_bf16_pairs(x):
    w = x.shape[1] // 2
    bits = lax.bitcast_convert_type(x.astype(BF16).astype(F32), jnp.int32)
    return bits[:, w:] | lax.shift_right_logical(bits[:, :w], 16)


def _unpack_bf16_pairs(p):
    lo = lax.bitcast_convert_type(lax.shift_left(p, 16), F32)
    hi = lax.bitcast_convert_type(p & jnp.int32(-65536), F32)
    return jnp.concatenate([lo, hi], axis=1)


def _ada_kernel(c_ref, w_ref, b_ref, o_ref):
    c = c_ref[...]
    o_ref[...] = _dot(c * _sigmoid(c), w_ref[...], precision=HIGHEST) + b_ref[...]


def _ada(c, w_ada, b_ada):
    B, D = c.shape
    N = w_ada.shape[1]
    tn = N // 4
    c8 = jnp.zeros((8, D), F32).at[:B].set(c)
    out = pl.pallas_call(
        _ada_kernel,
        out_shape=jax.ShapeDtypeStruct((8, N), F32),
        grid=(N // tn,),
        in_specs=[pl.BlockSpec((8, D), lambda j: (0, 0)),
                  pl.BlockSpec((D, tn), lambda j: (0, j)),
                  pl.BlockSpec((1, tn), lambda j: (0, j))],
        out_specs=pl.BlockSpec((8, tn), lambda j: (0, j)),
        compiler_params=pltpu.CompilerParams(vmem_limit_bytes=VMEM_LIMIT),
        name="ada",
    )(c8, w_ada, b_ada.reshape(1, N))
    return out[:B]


def _proj_kernel(x_ref, sc_ref, sh_ref, g_ref, w_ref, lb_ref, ct_ref, st_ref,
                 q_ref, k_ref, lf_ref, v_ref, gt_ref, aq_ref, ak_ref, av_ref, km_ref,
                 *, hw, aw):
    x = x_ref[...]
    ms = jnp.mean(x * x, axis=-1, keepdims=True)
    h = x * lax.rsqrt(ms + RMS_EPS) * g_ref[...]
    h = h * (1.0 + sc_ref[...]) + sh_ref[...]
    proj = _dot(h.astype(BF16), w_ref[...])

    hq = proj[:, 0:hw]
    hf = proj[:, hw:2 * hw]
    hg = proj[:, 3 * hw:4 * hw]
    q_ref[...] = hq * _sigmoid(hq) * (HGRN_DK ** -0.5)
    lb = lb_ref[...]
    f = lb + (1.0 - lb) * _sigmoid(hf)
    k_ref[...] = 1.0 - f
    lf_ref[...] = jnp.log(f)
    v_ref[...] = proj[:, 2 * hw:3 * hw]
    gt_ref[...] = hg * _sigmoid(hg)

    ct = jnp.concatenate([ct_ref[...]] * (aw // 128), axis=1)
    st = jnp.concatenate([st_ref[...]] * (aw // 128), axis=1)
    lane = lax.broadcasted_iota(jnp.int32, ct.shape, 1) % ATTN_HEAD_DIM
    first_half = lane < (ROT_DIM // 2)

    def rot(t):
        partner = jnp.where(first_half, pltpu.roll(t, aw - ROT_DIM // 2, 1), pltpu.roll(t, ROT_DIM // 2, 1))
        return t * ct + partner * st

    base = 4 * hw
    aq = rot(proj[:, base:base + aw])
    ak = rot(proj[:, base + aw:base + 2 * aw])
    av = proj[:, base + 2 * aw:base + 3 * aw]
    km_ref[...] = jnp.mean(ak, axis=0, keepdims=True)
    lane128 = lax.broadcasted_iota(jnp.int32, (x.shape[0], 128), 1)
    for pair in range(ATTN_HEADS // 2):
        aq_ref[pair] = aq[:, pair * 128:(pair + 1) * 128]
    for hd in range(ATTN_HEADS):
        pair, half = divmod(hd, 2)
        in_head = (lane128 // ATTN_HEAD_DIM) == half
        ak_ref[hd] = jnp.where(in_head, ak[:, pair * 128:(pair + 1) * 128], 0.0).astype(BF16)
        av_ref[hd] = av[:, hd * ATTN_HEAD_DIM:(hd + 1) * ATTN_HEAD_DIM].astype(BF16)


def _proj(x, scale1, shift1, norm_g, w_in_bf16, lb, ctab, stab, b0):
    _, S, D = x.shape
    B = scale1.shape[0]
    hw = lb.shape[-1]
    aw = ATTN_HEADS * ATTN_HEAD_DIM
    tm = MOBA_BLOCK
    nb = S // MOBA_BLOCK
    n_proj = w_in_bf16.shape[1]
    row = lambda b, i: (b, i, 0)
    xrow = lambda b, i: (b0 + b, i, 0)
    vec = lambda b, i: (b, 0, 0)
    head = lambda b, i: (b, 0, i, 0)
    out_shapes = (
        jax.ShapeDtypeStruct((B, S, hw), F32),
        jax.ShapeDtypeStruct((B, S, hw), F32),
        jax.ShapeDtypeStruct((B, S, hw), F32),
        jax.ShapeDtypeStruct((B, S, hw), F32),
        jax.ShapeDtypeStruct((B, S, hw), F32),
        jax.ShapeDtypeStruct((B, ATTN_HEADS // 2, S, 128), F32),
        jax.ShapeDtypeStruct((B, ATTN_HEADS, S, 128), BF16),
        jax.ShapeDtypeStruct((B, ATTN_HEADS, S, ATTN_HEAD_DIM), BF16),
        jax.ShapeDtypeStruct((B, nb, 1, aw), F32),
    )
    hspec = pl.BlockSpec((None, tm, hw), row)
    aspec = pl.BlockSpec((None, ATTN_HEADS, tm, ATTN_HEAD_DIM), head)
    return pl.pallas_call(
        functools.partial(_proj_kernel, hw=hw, aw=aw),
        out_shape=out_shapes,
        grid=(B, S // tm),
        in_specs=[pl.BlockSpec((None, tm, D), xrow),
                  pl.BlockSpec((None, 1, D), vec),
                  pl.BlockSpec((None, 1, D), vec),
                  pl.BlockSpec((1, D), lambda b, i: (0, 0)),
                  pl.BlockSpec((D, n_proj), lambda b, i: (0, 0)),
                  pl.BlockSpec((1, hw), lambda b, i: (0, 0)),
                  pl.BlockSpec((None, tm, 128), xrow),
                  pl.BlockSpec((None, tm, 128), xrow)],
        out_specs=(hspec, hspec, hspec, hspec, hspec,
                   pl.BlockSpec((None, ATTN_HEADS // 2, tm, 128), head),
                   pl.BlockSpec((None, ATTN_HEADS, tm, 128), head), aspec,
                   pl.BlockSpec((None, None, 1, aw), lambda b, i: (b, i, 0, 0))),
        compiler_params=pltpu.CompilerParams(
            dimension_semantics=("arbitrary", "arbitrary"), vmem_limit_bytes=VMEM_LIMIT),
        name="proj",
    )(x, scale1, shift1, norm_g, w_in_bf16, lb, ctab, stab)


def _hgrn_kernel(q_ref, k_ref, lf_ref, v_ref, gt_ref, gn_ref, o_ref, st_ref, *, n_heads, n_chunks):
    @pl.when(pl.program_id(1) == 0)
    def _():
        st_ref[...] = jnp.zeros_like(st_ref)

    C = HGRN_CHUNK
    r = lax.broadcasted_iota(jnp.int32, (C, C), 0)
    c = lax.broadcasted_iota(jnp.int32, (C, C), 1)
    tril = c <= r
    ltri = tril.astype(F32)
    gn = gn_ref[...]

    def chunk(ci, carry):
        r0 = pl.multiple_of(ci * C, C)
        rows = pl.ds(r0, C)
        b_all = _dot(ltri, lf_ref[rows, :], precision=HIGHEST)
        heads = range(n_heads)
        sls = [slice(hd * HGRN_DK, (hd + 1) * HGRN_DK) for hd in heads]
        bs = [b_all[:, sl] for sl in sls]
        b_lasts = [b[C - 1:C, :] for b in bs]
        qs = [q_ref[rows, sl] for sl in sls]
        ks = [k_ref[rows, sl] for sl in sls]
        vs = [v_ref[rows, sl] for sl in sls]
        states = [st_ref[hd] for hd in heads]
        o_inter = [_dot_nt((qs[hd] * jnp.exp(bs[hd])).astype(BF16), states[hd].astype(BF16)) for hd in heads]
        rhos = [0.5 * bl for bl in b_lasts]
        qas = [(qs[hd] * jnp.exp(jnp.minimum(bs[hd] - rhos[hd], EXP_CLAMP))).astype(BF16) for hd in heads]
        kbs = [(ks[hd] * jnp.exp(jnp.minimum(rhos[hd] - bs[hd], EXP_CLAMP))).astype(BF16) for hd in heads]
        scores = [jnp.where(tril, _dot_nt(qas[hd], kbs[hd]), 0.0).astype(BF16) for hd in heads]
        outs = [o_inter[hd] + _dot(scores[hd], vs[hd].astype(BF16)) for hd in heads]
        kds = [(ks[hd] * jnp.exp(b_lasts[hd] - bs[hd])).astype(BF16) for hd in heads]
        upds = [_dot(vs[hd].T.astype(BF16), kds[hd]) for hd in heads]
        for hd in heads:
            st_ref[hd] = states[hd] * jnp.exp(b_lasts[hd]) + upds[hd]
            o = outs[hd]
            ms = jnp.mean(o * o, axis=-1, keepdims=True)
            o_ref[rows, sls[hd]] = o * lax.rsqrt(ms + RMS_EPS) * gn * gt_ref[rows, sls[hd]]
        return carry

    lax.fori_loop(0, n_chunks, chunk, 0, unroll=True)


def _hgrn(q, k, lf, v, gt, norm_g):
    B, S, hw = q.shape
    n_heads = hw // HGRN_DK
    tc = 512
    spec = pl.BlockSpec((None, tc, hw), lambda b, i: (b, i, 0))
    return pl.pallas_call(
        functools.partial(_hgrn_kernel, n_heads=n_heads, n_chunks=tc // HGRN_CHUNK),
        out_shape=jax.ShapeDtypeStruct((B, S, hw), F32),
        grid=(B, S // tc),
        in_specs=[spec, spec, spec, spec, spec, pl.BlockSpec((1, HGRN_DK), lambda b, i: (0, 0))],
        out_specs=spec,
        scratch_shapes=[pltpu.VMEM((n_heads, HGRN_DK, HGRN_DK), F32)],
        compiler_params=pltpu.CompilerParams(
            dimension_semantics=("arbitrary", "arbitrary"), vmem_limit_bytes=VMEM_LIMIT),
        name="hgrn",
    )(q, k, lf, v, gt, norm_g)


def _sc_move_rows(table, src, dst, n_out, chunk):
    M = src.shape[0]
    D = table.shape[1]
    n_workers = SC_CORES * SC_SUBCORES
    per_worker = M // n_workers
    n_chunks = per_worker // chunk
    assert per_worker * n_workers == M and n_chunks * chunk == per_worker and n_chunks % 2 == 0 and chunk % 8 == 0
    mesh = plsc.VectorSubcoreMesh(core_axis_name="c", subcore_axis_name="s")
    idx_t = pltpu.VMEM((chunk,), jnp.int32)
    row_t = pltpu.VMEM((chunk, D), table.dtype)
    sem_t = pltpu.SemaphoreType.DMA

    def body(table_hbm, src_hbm, dst_hbm, out_hbm, src_v, dst_v, rows_v, g_sem, s_sem):
        wid = lax.axis_index("s") * SC_CORES + lax.axis_index("c")
        base = wid * per_worker

        def offset(j):
            return pl.multiple_of(base + j * chunk, 8)

        def gather(b):
            return pltpu.make_async_copy(table_hbm.at[src_v[b]], rows_v[b], g_sem[b])

        def start_gather(j, b):
            pltpu.sync_copy(src_hbm.at[pl.ds(offset(j), chunk)], src_v[b])
            gather(b).start()

        def write_out(j, b):
            if dst_hbm is None:
                pltpu.sync_copy(rows_v[b], out_hbm.at[pl.ds(offset(j), chunk)])
            else:
                pltpu.sync_copy(dst_hbm.at[pl.ds(offset(j), chunk)], dst_v[b])
                pltpu.async_copy(rows_v[b], out_hbm.at[dst_v[b]], s_sem[b]).wait()

        start_gather(0, 0)

        @pl.loop(0, n_chunks, step=2)
        def _(j):
            for b in (0, 1):
                @pl.when(j + b + 1 < n_chunks)
                def _():
                    start_gather(j + b + 1, 1 - b)
                gather(b).wait()
                write_out(j + b, b)

    if dst is None:
        @functools.partial(pl.kernel, mesh=mesh, out_type=jax.ShapeDtypeStruct((n_out, D), table.dtype),
                           scratch_types=[idx_t, idx_t, row_t, row_t, sem_t, sem_t])
        def gather_kernel(table_hbm, src_hbm, out_hbm, s0, s1, r0, r1, g0, g1):
            body(table_hbm, src_hbm, None, out_hbm, (s0, s1), None, (r0, r1), (g0, g1), None)
        return gather_kernel(table, src)

    @functools.partial(pl.kernel, mesh=mesh, out_type=jax.ShapeDtypeStruct((n_out, D), table.dtype),
                       scratch_types=[idx_t, idx_t, idx_t, idx_t, row_t, row_t, sem_t, sem_t, sem_t, sem_t])
    def permute_kernel(table_hbm, src_hbm, dst_hbm, out_hbm, s0, s1, d0, d1, r0, r1, g0, g1, w0, w1):
        body(table_hbm, src_hbm, dst_hbm, out_hbm, (s0, s1), (d0, d1), (r0, r1), (g0, g1), (w0, w1))
    return permute_kernel(table, src, dst)


def _sc_gather(table, idx, chunk):
    return _sc_move_rows(table, idx, None, idx.shape[0], chunk)


def _sc_permute(table, src, dst, n_out, chunk):
    return _sc_move_rows(table, src, dst, n_out, chunk)


def _tile_layout(counts, bm, n_tiles):
    n_groups = counts.shape[0]
    padded = (counts + bm - 1) // bm * bm
    pad_end = jnp.cumsum(padded)
    tile_start = jnp.arange(n_tiles, dtype=jnp.int32) * bm
    tile_group = jnp.minimum(
        jnp.sum((pad_end[None, :] <= tile_start[:, None]).astype(jnp.int32), axis=1), n_groups - 1)
    n_used = (pad_end[-1] // bm).astype(jnp.int32).reshape(1)
    return pad_end - padded, tile_group.astype(jnp.int32), n_used


def _null_partial(rows):
    lane = lax.broadcasted_iota(jnp.int32, (rows, PART_W), 1)
    return jnp.where(lane < ATTN_HEAD_DIM, 0.0, -jnp.inf).astype(F32)


def _moba_sel_kernel(q_ref, km_ref, k_ref, v_ref, idx_ref, cnt_ref, own_ref, cnt_acc, *, n_blocks):
    j = pl.program_id(1)
    T = MOBA_BLOCK
    heads = range(ATTN_HEADS)
    qs = [q_ref[hd // 2] for hd in heads]
    gates = [_dot_nt(km_ref[hd], qs[hd], precision=HIGHEST) for hd in heads]
    blk = lax.broadcasted_iota(jnp.int32, gates[0].shape, 0)
    neg_inf = jnp.float32(-jnp.inf)
    gates = [jnp.where(blk < j, g, neg_inf) for g in gates]
    picks = [[] for _ in heads]
    for _ in range(MOBA_TOPK):
        ms = [jnp.max(g, axis=0, keepdims=True) for g in gates]
        firsts = [jnp.min(jnp.where(g == m, blk, n_blocks), axis=0, keepdims=True) for g, m in zip(gates, ms)]
        for hd in heads:
            picks[hd].append(jnp.where(ms[hd] > neg_inf, firsts[hd], -1))
        gates = [jnp.where(blk == f, neg_inf, g) for g, f in zip(gates, firsts)]

    @pl.when(j == 0)
    def _():
        cnt_acc[...] = jnp.zeros_like(cnt_acc)

    earlier = (lax.broadcasted_iota(jnp.int32, (T, T), 0) < lax.broadcasted_iota(jnp.int32, (T, T), 1)).astype(BF16)
    for hd in heads:
        onehots = [(blk == p).astype(F32) for p in picks[hd]]
        member = onehots[0] + onehots[1] + onehots[2]
        base = cnt_acc[hd] + _dot(member.astype(BF16), earlier)
        ranks = [jnp.sum(oh * base, axis=0, keepdims=True).astype(jnp.int32) for oh in onehots]
        idx_ref[hd] = jnp.concatenate(picks[hd] + ranks + [jnp.zeros((2, T), jnp.int32)], axis=0)
        total = cnt_acc[hd] + jnp.sum(member, axis=1, keepdims=True)
        cnt_acc[hd] = total
        cnt_ref[hd] = total.astype(jnp.int32)
    causal = lax.broadcasted_iota(jnp.int32, (T, T), 1) <= lax.broadcasted_iota(jnp.int32, (T, T), 0)
    scale = ATTN_HEAD_DIM ** -0.5
    ss = [jnp.where(causal, _dot_nt((qs[hd] * scale).astype(BF16), k_ref[hd]), neg_inf) for hd in heads]
    mx = [jnp.max(s, axis=1, keepdims=True) for s in ss]
    ps = [jnp.exp(s - m) for s, m in zip(ss, mx)]
    ls = [jnp.sum(p, axis=1, keepdims=True) for p in ps]
    accs = [_dot(ps[hd].astype(BF16), v_ref[hd]) for hd in heads]
    for hd in heads:
        lse = jnp.broadcast_to(mx[hd] + jnp.log(ls[hd]), (T, PART_W - ATTN_HEAD_DIM))
        own_ref[hd] = jnp.concatenate([accs[hd] / ls[hd], lse], axis=1)


def _moba_sel(aq, kmean, ak, av):
    B, H, S, hd = av.shape
    nb = S // MOBA_BLOCK
    T = MOBA_BLOCK
    blk = lambda b, j: (b, 0, j, 0)
    return pl.pallas_call(
        functools.partial(_moba_sel_kernel, n_blocks=nb),
        out_shape=(jax.ShapeDtypeStruct((B, H, 8, S), jnp.int32),
                   jax.ShapeDtypeStruct((B, H, nb, 1), jnp.int32),
                   jax.ShapeDtypeStruct((B, H, S, PART_W), F32)),
        grid=(B, nb),
        in_specs=[pl.BlockSpec((None, H // 2, T, 128), blk),
                  pl.BlockSpec((None, H, nb, 128), lambda b, j: (b, 0, 0, 0)),
                  pl.BlockSpec((None, H, T, 128), blk),
                  pl.BlockSpec((None, H, T, hd), blk)],
        out_specs=(pl.BlockSpec((None, H, 8, T), lambda b, j: (b, 0, 0, j)),
                   pl.BlockSpec((None, H, nb, 1), lambda b, j: (b, 0, 0, 0)),
                   pl.BlockSpec((None, H, T, PART_W), blk)),
        scratch_shapes=[pltpu.VMEM((H, nb, 1), F32)],
        compiler_params=pltpu.CompilerParams(
            dimension_semantics=("arbitrary", "arbitrary"), vmem_limit_bytes=VMEM_LIMIT),
        name="moba_sel",
    )(aq, kmean, ak, av)


def _moba_blk_kernel(tg_ref, nu_ref, q_ref, *refs):
    n = MOBA_TILES_PER_STEP
    k_refs, v_refs, o_ref = refs[:n], refs[n:2 * n], refs[2 * n]
    R = MOBA_ROWS
    t0 = pl.program_id(0) * n

    @pl.when(t0 < nu_ref[0])
    def _():
        scale = ATTN_HEAD_DIM ** -0.5
        ss = [_dot_nt((q_ref[j * R:(j + 1) * R, :] * scale).astype(BF16), k_refs[j][...]) for j in range(n)]
        ms = [jnp.max(s, axis=1, keepdims=True) for s in ss]
        ps = [jnp.exp(s - m) for s, m in zip(ss, ms)]
        ls = [jnp.sum(p, axis=1, keepdims=True) for p in ps]
        accs = [_dot(p.astype(BF16), v_refs[j][...]) for j, p in enumerate(ps)]
        null = _null_partial(R)
        for j in range(n):
            lse = jnp.broadcast_to(ms[j] + jnp.log(ls[j]), (R, PART_W - ATTN_HEAD_DIM))
            row = jnp.concatenate([accs[j] / ls[j], lse], axis=1)
            o_ref[j * R:(j + 1) * R, :] = jnp.where(t0 + j < nu_ref[0], row, null)

    @pl.when(t0 >= nu_ref[0])
    def _():
        o_ref[...] = _null_partial(n * R)


def _moba_blk(qs, tile_group, n_used, ak, av):
    B, H, S, hd = av.shape
    nb = S // MOBA_BLOCK
    R = MOBA_ROWS
    n = MOBA_TILES_PER_STEP
    n_tiles = qs.shape[0] // R
    assert n_tiles % n == 0
    kv = lambda j: (lambda i, tg, nu: (tg[i * n + j] // nb, tg[i * n + j] % nb, 0, 0))
    grid_spec = pltpu.PrefetchScalarGridSpec(
        num_scalar_prefetch=2,
        grid=(n_tiles // n,),
        in_specs=[pl.BlockSpec((n * R, 128), lambda i, tg, nu: (i, 0))]
        + [pl.BlockSpec((None, None, MOBA_BLOCK, 128), kv(j)) for j in range(n)]
        + [pl.BlockSpec((None, None, MOBA_BLOCK, hd), kv(j)) for j in range(n)],
        out_specs=pl.BlockSpec((n * R, PART_W), lambda i, tg, nu: (i, 0)),
    )
    k4 = ak.reshape(B * H, nb, MOBA_BLOCK, 128)
    v4 = av.reshape(B * H, nb, MOBA_BLOCK, hd)
    return pl.pallas_call(
        _moba_blk_kernel,
        out_shape=jax.ShapeDtypeStruct((n_tiles * R, PART_W), F32),
        grid_spec=grid_spec,
        compiler_params=pltpu.CompilerParams(
            dimension_semantics=("arbitrary",), vmem_limit_bytes=VMEM_LIMIT),
        name="moba_blk",
    )(tile_group, n_used, qs, *([k4] * n), *([v4] * n))


def _moba_merge_kernel(own_ref, pg_ref, g_ref, o_ref):
    hd = ATTN_HEAD_DIM
    rows = [own_ref[...]] + [pg_ref[s] for s in range(MOBA_TOPK)]
    lses = [pltpu.roll(r, hd, 1) for r in rows]
    top = lses[0]
    for z in lses[1:]:
        top = jnp.maximum(top, z)
    num = jnp.zeros_like(top)
    den = jnp.zeros_like(top)
    for r, z in zip(rows, lses):
        w = jnp.exp(z - top)
        num = num + w * r
        den = den + w
    o = (num / den)[:, :hd]
    ms = jnp.mean(o * o, axis=-1, keepdims=True)
    o_ref[...] = o * lax.rsqrt(ms + RMS_EPS) * g_ref[...]


def _moba_merge(own, pg, norm_g):
    n = own.shape[0]
    T = 512
    row = lambda i: (i, 0)
    return pl.pallas_call(
        _moba_merge_kernel,
        out_shape=jax.ShapeDtypeStruct((n, ATTN_HEAD_DIM), F32),
        grid=(n // T,),
        in_specs=[pl.BlockSpec((T, PART_W), row),
                  pl.BlockSpec((MOBA_TOPK, T, PART_W), lambda i: (0, i, 0)),
                  pl.BlockSpec((1, ATTN_HEAD_DIM), lambda i: (0, 0))],
        out_specs=pl.BlockSpec((T, ATTN_HEAD_DIM), row),
        compiler_params=pltpu.CompilerParams(
            dimension_semantics=("arbitrary",), vmem_limit_bytes=VMEM_LIMIT),
        name="moba_merge",
    )(own, pg, norm_g)


def _moba(aq, km, ak, av, norm_g):
    B, H, S, hd = av.shape
    nb = S // MOBA_BLOCK
    n_q = B * H * S
    kmp = km.reshape(B, nb, H // 2, 128)
    half = jnp.arange(128, dtype=jnp.int32) // hd
    kmean = jnp.stack([jnp.where(half == h % 2, kmp[:, :, h // 2, :], 0.0) for h in range(H)], axis=1)
    idx8, counts, own = _moba_sel(aq, kmean, ak, av)
    sel = idx8[:, :, 0:MOBA_TOPK, :].reshape(B * H, MOBA_TOPK, S)
    rank = idx8[:, :, MOBA_TOPK:2 * MOBA_TOPK, :].reshape(B * H, MOBA_TOPK, S)
    n_groups = B * H * nb
    n_tiles = (n_q * MOBA_TOPK) // MOBA_ROWS + n_groups
    pad_start, tile_group, n_used = _tile_layout(counts.reshape(-1), MOBA_ROWS, n_tiles)
    blocks = jnp.arange(nb, dtype=jnp.int32)[:, None]
    start = jnp.sum(jnp.where(sel[:, :, None, :] == blocks, pad_start.reshape(B * H, 1, nb, 1), 0), axis=2)
    a_ids = jnp.arange(n_q * MOBA_TOPK, dtype=jnp.int32).reshape(B * H, MOBA_TOPK, S)
    assert n_tiles * MOBA_ROWS >= n_q * MOBA_TOPK + MOBA_ROWS
    pos = jnp.where(sel >= 0, start + rank, n_used[0] * MOBA_ROWS + a_ids % MOBA_ROWS)
    bh = jnp.arange(B * H, dtype=jnp.int32)[:, None, None]
    t = jnp.arange(S, dtype=jnp.int32)[None, None, :]
    pair_row = jnp.broadcast_to((bh // H * (H // 2) + bh % H // 2) * S + t, pos.shape)
    qs = _sc_permute(aq.reshape(B * (H // 2) * S, 128), pair_row.reshape(-1), pos.reshape(-1),
                     n_tiles * MOBA_ROWS, 256)
    parts = _moba_blk(qs, tile_group, n_used, ak, av)
    pg = _sc_gather(parts, pos.transpose(1, 0, 2).reshape(-1), 256)
    o = _moba_merge(own.reshape(n_q, PART_W), pg.reshape(MOBA_TOPK, n_q, PART_W), norm_g)
    return o.reshape(B, H, S, hd)


def _mix_kernel(oh_ref, oa_ref, x_ref, g1_ref, sc2_ref, sh2_ref, n2_ref, wo_ref, wr_ref, br_ref,
                x1_ref, h2_ref, gw_ref, idx_ref, cnt_ref, cnt_acc):
    cat = jnp.concatenate([oh_ref[...]] + [oa_ref[hd] for hd in range(ATTN_HEADS)], axis=1)
    mix = _dot(cat.astype(BF16), wo_ref[...])
    x1 = x_ref[...] + g1_ref[...] * mix
    x1_ref[...] = x1
    ms = jnp.mean(x1 * x1, axis=-1, keepdims=True)
    h2 = x1 * lax.rsqrt(ms + RMS_EPS) * n2_ref[...]
    h2 = h2 * (1.0 + sc2_ref[...]) + sh2_ref[...]
    h2_ref[...] = _pack_bf16_pairs(h2)
    E = N_EXPERTS
    tm = h2.shape[0]
    h_0 = h2.astype(BF16)
    r_1 = h2 - h_0.astype(F32)
    h_1 = r_1.astype(BF16)
    h_2 = (r_1 - h_1.astype(F32)).astype(BF16)
    wt = wr_ref[...]
    p_0 = _dot_nt(wt, h_0)
    p_1 = _dot_nt(wt[:2 * E], h_1)
    p_2 = _dot_nt(wt[:E], h_2)
    logits = (p_0[:E] + (p_0[E:2 * E] + p_1[:E]) + (p_0[2 * E:] + p_1[E:] + p_2)) + br_ref[...]
    ex = lax.broadcasted_iota(jnp.int32, logits.shape, 0)
    neg_inf = jnp.float32(-jnp.inf)
    vals, idxs = [], []
    for _ in range(TOP_K):
        m = jnp.max(logits, axis=0, keepdims=True)
        first = jnp.min(jnp.where(logits == m, ex, E), axis=0, keepdims=True)
        vals.append(m)
        idxs.append(first)
        logits = jnp.where(ex == first, neg_inf, logits)
    e = [jnp.exp(v - vals[0]) for v in vals]
    denom = e[0] + e[1] + e[2] + e[3]
    gate_rows = jnp.concatenate([ei / denom for ei in e] + [jnp.zeros((128 - TOP_K, tm), F32)], axis=0)
    gw_ref[...] = gate_rows.T[:, :TOP_K]

    @pl.when((pl.program_id(0) == 0) & (pl.program_id(1) == 0))
    def _():
        cnt_acc[...] = jnp.zeros_like(cnt_acc)

    earlier = (lax.broadcasted_iota(jnp.int32, (tm, tm), 0) < lax.broadcasted_iota(jnp.int32, (tm, tm), 1)).astype(BF16)
    onehots = [(ex == ix).astype(F32) for ix in idxs]
    member = onehots[0] + onehots[1] + onehots[2] + onehots[3]
    base = cnt_acc[...] + _dot(member.astype(BF16), earlier)
    ranks = [jnp.sum(oh * base, axis=0, keepdims=True).astype(jnp.int32) for oh in onehots]
    idx_ref[...] = jnp.concatenate(idxs + ranks, axis=0)
    total = cnt_acc[...] + jnp.sum(member, axis=1, keepdims=True)
    cnt_acc[...] = total
    cnt_ref[...] = total.astype(jnp.int32)


def _mix(oh, oa, x, gate1, scale2, shift2, norm2_g, w_out_bf16, w_router, b_router, b0):
    _, S, D = x.shape
    B = oh.shape[0]
    hw = oh.shape[-1]
    tm = 256
    row = lambda b, i: (b, i, 0)
    xrow = lambda b, i: (b0 + b, i, 0)
    vec = lambda b, i: (b, 0, 0)
    const = lambda b, i: (0, 0)
    return pl.pallas_call(
        _mix_kernel,
        out_shape=(jax.ShapeDtypeStruct((B, S, D), F32),
                   jax.ShapeDtypeStruct((B, S, D // 2), jnp.int32),
                   jax.ShapeDtypeStruct((B, S, TOP_K), F32),
                   jax.ShapeDtypeStruct((B, 2 * TOP_K, S), jnp.int32),
                   jax.ShapeDtypeStruct((N_EXPERTS, 1), jnp.int32)),
        grid=(B, S // tm),
        in_specs=[pl.BlockSpec((None, tm, hw), row),
                  pl.BlockSpec((None, ATTN_HEADS, tm, ATTN_HEAD_DIM), lambda b, i: (b, 0, i, 0)),
                  pl.BlockSpec((None, tm, D), xrow),
                  pl.BlockSpec((None, 1, D), vec),
                  pl.BlockSpec((None, 1, D), vec),
                  pl.BlockSpec((None, 1, D), vec),
                  pl.BlockSpec((1, D), const),
                  pl.BlockSpec((D, D), const),
                  pl.BlockSpec((3 * N_EXPERTS, D), const),
                  pl.BlockSpec((N_EXPERTS, 1), const)],
        out_specs=(pl.BlockSpec((None, tm, D), row),
                   pl.BlockSpec((None, tm, D // 2), row),
                   pl.BlockSpec((None, tm, TOP_K), row),
                   pl.BlockSpec((None, 2 * TOP_K, tm), lambda b, i: (b, 0, i)),
                   pl.BlockSpec((N_EXPERTS, 1), const)),
        scratch_shapes=[pltpu.VMEM((N_EXPERTS, 1), F32)],
        compiler_params=pltpu.CompilerParams(
            dimension_semantics=("arbitrary", "arbitrary"), vmem_limit_bytes=VMEM_LIMIT),
        name="mix",
    )(oh, oa, x, gate1, scale2, shift2, norm2_g, w_out_bf16, w_router, b_router)


def _moe_rows_kernel(be_ref, nu_ref, x_ref, wgu_ref, bgu_ref, wd_ref, bd_ref, y_ref, wgu16, wd16, *, d_ff):
    i = pl.program_id(0)

    @pl.when((i == 0) | (be_ref[i] != be_ref[jnp.maximum(i - 1, 0)]))
    def _():
        wgu16[...] = wgu_ref[...].astype(BF16)
        wd16[...] = wd_ref[...].astype(BF16)

    @pl.when(i < nu_ref[0])
    def _():
        gu = _dot(_unpack_bf16_pairs(x_ref[...]).astype(BF16), wgu16[...]) + bgu_ref[...]
        gate = jnp.minimum(gu[:, :d_ff], SWIGLU_LIMIT)
        up = jnp.clip(gu[:, d_ff:], -SWIGLU_LIMIT, SWIGLU_LIMIT)
        act = (up + 1.0) * gate * _sigmoid(SWIGLU_ALPHA * gate)
        y_ref[...] = _pack_bf16_pairs(_dot(act.astype(BF16), wd16[...]) + bd_ref[...])

    @pl.when(i >= nu_ref[0])
    def _():
        y_ref[...] = jnp.zeros_like(y_ref)


def _moe_rows(xs, blk_expert, n_used, wgu, bgu, wd, bd):
    D = 2 * xs.shape[1]
    bm = MOE_ROWS
    n_blk = xs.shape[0] // bm
    d_ff = wd.shape[1]
    wsel = lambda i, be, nu: (be[i], 0, 0)
    grid_spec = pltpu.PrefetchScalarGridSpec(
        num_scalar_prefetch=2,
        grid=(n_blk,),
        in_specs=[pl.BlockSpec((bm, D // 2), lambda i, be, nu: (i, 0)),
                  pl.BlockSpec((None, D, 2 * d_ff), wsel),
                  pl.BlockSpec((None, 1, 2 * d_ff), wsel),
                  pl.BlockSpec((None, d_ff, D), wsel),
                  pl.BlockSpec((None, 1, D), wsel)],
        out_specs=pl.BlockSpec((bm, D // 2), lambda i, be, nu: (i, 0)),
        scratch_shapes=[pltpu.VMEM((D, 2 * d_ff), BF16), pltpu.VMEM((d_ff, D), BF16)],
    )
    return pl.pallas_call(
        functools.partial(_moe_rows_kernel, d_ff=d_ff),
        out_shape=jax.ShapeDtypeStruct((n_blk * bm, D // 2), jnp.int32),
        grid_spec=grid_spec,
        compiler_params=pltpu.CompilerParams(
            dimension_semantics=("arbitrary",), vmem_limit_bytes=VMEM_LIMIT),
        name="moe_rows",
    )(blk_expert, n_used, xs, wgu, bgu.reshape(N_EXPERTS, 1, 2 * d_ff), wd, bd.reshape(N_EXPERTS, 1, D))


def _combine_rows_kernel(*refs):
    y_refs = refs[:TOP_K]
    gw_ref, x1_ref, g2_ref, fg_ref = refs[TOP_K:TOP_K + 4]
    o_ref = refs[-1]
    gw = gw_ref[...]
    y = gw[:, 0:1] * _unpack_bf16_pairs(y_refs[0][...])
    for kk in range(1, TOP_K):
        y = y + gw[:, kk:kk + 1] * _unpack_bf16_pairs(y_refs[kk][...])
    x2 = x1_ref[...] + g2_ref[...] * y
    ms = jnp.mean(x2 * x2, axis=-1, keepdims=True)
    o_ref[...] = x2 * lax.rsqrt(ms + RMS_EPS) * fg_ref[...]


def _combine_rows(yg, gates, x1, gate2, final_g, out_so_far, b0, n_batches):
    S, D = x1.shape
    tm = 256
    steps = S // tm
    slot_spec = lambda kk: pl.BlockSpec((tm, D // 2), lambda i: (kk * steps + i, 0))
    in_specs = [slot_spec(kk) for kk in range(TOP_K)] + [
        pl.BlockSpec((tm, TOP_K), lambda i: (i, 0)),
        pl.BlockSpec((tm, D), lambda i: (i, 0)),
        pl.BlockSpec((1, D), lambda i: (0, 0)),
        pl.BlockSpec((1, D), lambda i: (0, 0))]
    args = [yg] * TOP_K + [gates, x1, gate2, final_g]
    aliases = {}
    if out_so_far is not None:
        in_specs.append(pl.BlockSpec(memory_space=pl.ANY))
        aliases = {len(args): 0}
        args.append(out_so_far)
    return pl.pallas_call(
        _combine_rows_kernel,
        out_shape=jax.ShapeDtypeStruct((n_batches * S, D), F32),
        grid=(steps,),
        in_specs=in_specs,
        out_specs=pl.BlockSpec((tm, D), lambda i: (b0 * steps + i, 0)),
        input_output_aliases=aliases,
        compiler_params=pltpu.CompilerParams(
            dimension_semantics=("arbitrary",), vmem_limit_bytes=VMEM_LIMIT),
        name="combine_rows",
    )(*args)


def _split_bf16x3(w):
    def top(v):
        return lax.bitcast_convert_type(lax.bitcast_convert_type(v, jnp.int32) & jnp.int32(-65536), F32)
    w0 = top(w)
    w1 = top(w - w0)
    w2 = w - w0 - w1
    return jnp.concatenate([w0, w1, w2], axis=1).astype(BF16).T


def _rotary_tables(positions):
    half = ROT_DIM // 2
    inv_freq = jnp.exp(-math.log(ROPE_THETA) * jnp.arange(0, ROT_DIM, 2, dtype=F32) / ROT_DIM)
    d = jnp.arange(128, dtype=jnp.int32) % ATTN_HEAD_DIM
    freq = jnp.where(d < ROT_DIM, inv_freq[d % half], 0.0)
    sign = jnp.where(d < half, -1.0, 1.0)
    ang = positions.astype(F32)[:, :, None] * freq
    return jnp.cos(ang), jnp.sin(ang) * sign


def kernel(x, c, positions, w_ada, b_ada, norm1_g, w_in, hgrn_lb_logits, hgrn_norm_g, attn_norm_g,
           w_out, norm2_g, w_router, b_router, w_gate_up, b_gate_up, w_down, b_down, final_norm_g):
    B, S, D = x.shape
    assert w_in.shape[0] == 1, "single-layer block: the final norm is fused into the combine step"
    l = 0
    ctab, stab = _rotary_tables(positions)
    lower_bounds = jnp.cumsum(jax.nn.softmax(hgrn_lb_logits.astype(F32), axis=0), axis=0)
    mod = _ada(c, w_ada[l], b_ada[l])
    shift1, scale1, gate1, shift2, scale2, gate2 = jnp.split(mod[:, None, :], N_MOD, axis=-1)
    n_blk = (S * TOP_K) // MOE_ROWS + N_EXPERTS
    experts = jnp.arange(N_EXPERTS, dtype=jnp.int32)[:, None]
    tok = jnp.broadcast_to(jnp.arange(S, dtype=jnp.int32)[None, :], (TOP_K, S)).reshape(-1)
    out = None
    for b in range(B):
        one = slice(b, b + 1)
        q, k, lf, v, gt, aq, ak, av, km = _proj(
            x, scale1[one], shift1[one], norm1_g[l][None], w_in[l].astype(BF16), lower_bounds[l][None],
            ctab, stab, b)
        o_a = _moba(aq, km, ak, av, attn_norm_g[l][None])
        o_h = _hgrn(q, k, lf, v, gt, hgrn_norm_g[l][None])
        x1, h2, gates, idx8, counts = _mix(
            o_h, o_a, x, gate1[one], scale2[one], shift2[one], norm2_g[l][None], w_out[l].astype(BF16),
            _split_bf16x3(w_router[l]), b_router[l][:, None], b)
        pad_start, blk_expert, n_used = _tile_layout(counts.reshape(-1), MOE_ROWS, n_blk)
        chosen = idx8[0, 0:TOP_K, :]
        pos = jnp.sum(jnp.where(chosen[:, None, :] == experts, pad_start[:, None], 0), axis=1) + idx8[0, TOP_K:, :]
        xs = _sc_permute(h2.reshape(S, D // 2), tok, pos.reshape(-1), n_blk * MOE_ROWS, 64)
        y_sorted = _moe_rows(xs, blk_expert, n_used, w_gate_up[l], b_gate_up[l], w_down[l], b_down[l])
        yg = _sc_gather(y_sorted, pos.reshape(-1), 64)
        out = _combine_rows(yg, gates.reshape(S, TOP_K), x1.reshape(S, D), gate2[b], final_norm_g[None], out, b, B)
    return out.reshape(B, S, D)
```

```python
import functools
import math

import jax
import jax.numpy as jnp
from jax import lax
from jax.experimental import pallas as pl
from jax.experimental.pallas import tpu as pltpu
from jax.experimental.pallas import tpu_sc as plsc

F32 = jnp.float32
BF16 = jnp.bfloat16
HIGHEST = lax.Precision.HIGHEST

HGRN_DK = 128
HGRN_CHUNK = 64
ATTN_HEADS = 4
ATTN_HEAD_DIM = 64
ROT_DIM = ATTN_HEAD_DIM // 4
ROPE_THETA = 500000.0
MOBA_BLOCK = 256
MOBA_TOPK = 3
N_EXPERTS = 32
TOP_K = 4
SWIGLU_ALPHA = 1.702
SWIGLU_LIMIT = 7.0
N_MOD = 6
RMS_EPS = 1e-6

EXP_CLAMP = 80.0
MOE_ROWS = 512
MOBA_ROWS = 256
MOBA_TILES_PER_STEP = 8
PART_W = 128
V7X_VMEM_BYTES = 64 * 1024 * 1024
VMEM_LIMIT = V7X_VMEM_BYTES * 7 // 8
SC_CORES = 2
SC_SUBCORES = 16


def _sigmoid(x):
    return 1.0 / (1.0 + jnp.exp(-x))


def _dot(a, b, **kw):
    return jnp.dot(a, b, preferred_element_type=F32, **kw)


def _dot_nt(a, b, **kw):
    return lax.dot_general(a, b, (((1,), (1,)), ((), ())), preferred_element_type=F32, **kw)


def _pack_bf16_pairs(x):
    w = x.shape[1] // 2
    bits = lax.bitcast_convert_type(x.astype(BF16).astype(F32), jnp.int32)
    return bits[:, w:] | lax.shift_right_logical(bits[:, :w], 16)


def _unpack_bf16_pairs(p):
    lo = lax.bitcast_convert_type(lax.shift_left(p, 16), F32)
    hi = lax.bitcast_convert_type(p & jnp.int32(-65536), F32)
    return jnp.concatenate([lo, hi], axis=1)


def _ada_kernel(c_ref, w_ref, b_ref, o_ref):
    c = c_ref[...]
    o_ref[...] = _dot(c * _sigmoid(c), w_ref[...], precision=HIGHEST) + b_ref[...]


def _ada(c, w_ada, b_ada):
    B, D = c.shape
    N = w_ada.shape[1]
    tn = N // 4
    c8 = jnp.zeros((8, D), F32).at[:B].set(c)
    out = pl.pallas_call(
        _ada_kernel,
        out_shape=jax.ShapeDtypeStruct((8, N), F32),
        grid=(N // tn,),
        in_specs=[pl.BlockSpec((8, D), lambda j: (0, 0)),
                  pl.BlockSpec((D, tn), lambda j: (0, j)),
                  pl.BlockSpec((1, tn), lambda j: (0, j))],
        out_specs=pl.BlockSpec((8, tn), lambda j: (0, j)),
        compiler_params=pltpu.CompilerParams(vmem_limit_bytes=VMEM_LIMIT),
        name="ada",
    )(c8, w_ada, b_ada.reshape(1, N))
    return out[:B]


def _proj_kernel(x_ref, sc_ref, sh_ref, g_ref, w_ref, lb_ref, ct_ref, st_ref,
                 q_ref, k_ref, lf_ref, v_ref, gt_ref, aq_ref, ak_ref, av_ref, km_ref,
                 *, hw, aw):
    x = x_ref[...]
    ms = jnp.mean(x * x, axis=-1, keepdims=True)
    h = x * lax.rsqrt(ms + RMS_EPS) * g_ref[...]
    h = h * (1.0 + sc_ref[...]) + sh_ref[...]
    proj = _dot(h.astype(BF16), w_ref[...])

    hq = proj[:, 0:hw]
    hf = proj[:, hw:2 * hw]
    hg = proj[:, 3 * hw:4 * hw]
    q_ref[...] = hq * _sigmoid(hq) * (HGRN_DK ** -0.5)
    lb = lb_ref[...]
    f = lb + (1.0 - lb) * _sigmoid(hf)
    k_ref[...] = 1.0 - f
    lf_ref[...] = jnp.log(f)
    v_ref[...] = proj[:, 2 * hw:3 * hw]
    gt_ref[...] = hg * _sigmoid(hg)

    ct = jnp.concatenate([ct_ref[...]] * (aw // 128), axis=1)
    st = jnp.concatenate([st_ref[...]] * (aw // 128), axis=1)
    lane = lax.broadcasted_iota(jnp.int32, ct.shape, 1) % ATTN_HEAD_DIM
    first_half = lane < (ROT_DIM // 2)

    def rot(t):
        partner = jnp.where(first_half, pltpu.roll(t, aw - ROT_DIM // 2, 1), pltpu.roll(t, ROT_DIM // 2, 1))
        return t * ct + partner * st

    base = 4 * hw
    aq = rot(proj[:, base:base + aw])
    ak = rot(proj[:, base + aw:base + 2 * aw])
    av = proj[:, base + 2 * aw:base + 3 * aw]
    km_ref[...] = jnp.mean(ak, axis=0, keepdims=True)
    lane128 = lax.broadcasted_iota(jnp.int32, (x.shape[0], 128), 1)
    for pair in range(ATTN_HEADS // 2):
        aq_ref[pair] = aq[:, pair * 128:(pair + 1) * 128]
    for hd in range(ATTN_HEADS):
        pair, half = divmod(hd, 2)
        in_head = (lane128 // ATTN_HEAD_DIM) == half
        ak_ref[hd] = jnp.where(in_head, ak[:, pair * 128:(pair + 1) * 128], 0.0).astype(BF16)
        av_ref[hd] = av[:, hd * ATTN_HEAD_DIM:(hd + 1) * ATTN_HEAD_DIM].astype(BF16)


def _proj(x, scale1, shift1, norm_g, w_in_bf16, lb, ctab, stab, b0):
    _, S, D = x.shape
    B = scale1.shape[0]
    hw = lb.shape[-1]
    aw = ATTN_HEADS * ATTN_HEAD_DIM
    tm = MOBA_BLOCK
    nb = S // MOBA_BLOCK
    n_proj = w_in_bf16.shape[1]
    row = lambda b, i: (b, i, 0)
    xrow = lambda b, i: (b0 + b, i, 0)
    vec = lambda b, i: (b, 0, 0)
    head = lambda b, i: (b, 0, i, 0)
    out_shapes = (
        jax.ShapeDtypeStruct((B, S, hw), F32),
        jax.ShapeDtypeStruct((B, S, hw), F32),
        jax.ShapeDtypeStruct((B, S, hw), F32),
        jax.ShapeDtypeStruct((B, S, hw), F32),
        jax.ShapeDtypeStruct((B, S, hw), F32),
        jax.ShapeDtypeStruct((B, ATTN_HEADS // 2, S, 128), F32),
        jax.ShapeDtypeStruct((B, ATTN_HEADS, S, 128), BF16),
        jax.ShapeDtypeStruct((B, ATTN_HEADS, S, ATTN_HEAD_DIM), BF16),
        jax.ShapeDtypeStruct((B, nb, 1, aw), F32),
    )
    hspec = pl.BlockSpec((None, tm, hw), row)
    aspec = pl.BlockSpec((None, ATTN_HEADS, tm, ATTN_HEAD_DIM), head)
    return pl.pallas_call(
        functools.partial(_proj_kernel, hw=hw, aw=aw),
        out_shape=out_shapes,
        grid=(B, S // tm),
        in_specs=[pl.BlockSpec((None, tm, D), xrow),
                  pl.BlockSpec((None, 1, D), vec),
                  pl.BlockSpec((None, 1, D), vec),
                  pl.BlockSpec((1, D), lambda b, i: (0, 0)),
                  pl.BlockSpec((D, n_proj), lambda b, i: (0, 0)),
                  pl.BlockSpec((1, hw), lambda b, i: (0, 0)),
                  pl.BlockSpec((None, tm, 128), xrow),
                  pl.BlockSpec((None, tm, 128), xrow)],
        out_specs=(hspec, hspec, hspec, hspec, hspec,
                   pl.BlockSpec((None, ATTN_HEADS // 2, tm, 128), head),
                   pl.BlockSpec((None, ATTN_HEADS, tm, 128), head), aspec,
                   pl.BlockSpec((None, None, 1, aw), lambda b, i: (b, i, 0, 0))),
        compiler_params=pltpu.CompilerParams(
            dimension_semantics=("arbitrary", "arbitrary"), vmem_limit_bytes=VMEM_LIMIT),
        name="proj",
    )(x, scale1, shift1, norm_g, w_in_bf16, lb, ctab, stab)


def _hgrn_kernel(q_ref, k_ref, lf_ref, v_ref, gt_ref, gn_ref, o_ref, st_ref, *, n_heads, n_chunks):
    @pl.when(pl.program_id(1) == 0)
    def _():
        st_ref[...] = jnp.zeros_like(st_ref)

    C = HGRN_CHUNK
    r = lax.broadcasted_iota(jnp.int32, (C, C), 0)
    c = lax.broadcasted_iota(jnp.int32, (C, C), 1)
    tril = c <= r
    ltri = tril.astype(F32)
    gn = gn_ref[...]

    def chunk(ci, carry):
        r0 = pl.multiple_of(ci * C, C)
        rows = pl.ds(r0, C)
        b_all = _dot(ltri, lf_ref[rows, :], precision=HIGHEST)
        heads = range(n_heads)
        sls = [slice(hd * HGRN_DK, (hd + 1) * HGRN_DK) for hd in heads]
        bs = [b_all[:, sl] for sl in sls]
        b_lasts = [b[C - 1:C, :] for b in bs]
        qs = [q_ref[rows, sl] for sl in sls]
        ks = [k_ref[rows, sl] for sl in sls]
        vs = [v_ref[rows, sl] for sl in sls]
        states = [st_ref[hd] for hd in heads]
        o_inter = [_dot_nt((qs[hd] * jnp.exp(bs[hd])).astype(BF16), states[hd].astype(BF16)) for hd in heads]
        rhos = [0.5 * bl for bl in b_lasts]
        qas = [(qs[hd] * jnp.exp(jnp.minimum(bs[hd] - rhos[hd], EXP_CLAMP))).astype(BF16) for hd in heads]
        kbs = [(ks[hd] * jnp.exp(jnp.minimum(rhos[hd] - bs[hd], EXP_CLAMP))).astype(BF16) for hd in heads]
        scores = [jnp.where(tril, _dot_nt(qas[hd], kbs[hd]), 0.0).astype(BF16) for hd in heads]
        outs = [o_inter[hd] + _dot(scores[hd], vs[hd].astype(BF16)) for hd in heads]
        kds = [(ks[hd] * jnp.exp(b_lasts[hd] - bs[hd])).astype(BF16) for hd in heads]
        upds = [_dot(vs[hd].T.astype(BF16), kds[hd]) for hd in heads]
        for hd in heads:
            st_ref[hd] = states[hd] * jnp.exp(b_lasts[hd]) + upds[hd]
            o = outs[hd]
            ms = jnp.mean(o * o, axis=-1, keepdims=True)
            o_ref[rows, sls[hd]] = o * lax.rsqrt(ms + RMS_EPS) * gn * gt_ref[rows, sls[hd]]
        return carry

    lax.fori_loop(0, n_chunks, chunk, 0, unroll=True)


def _hgrn(q, k, lf, v, gt, norm_g):
    B, S, hw = q.shape
    n_heads = hw // HGRN_DK
    tc = 512
    spec = pl.BlockSpec((None, tc, hw), lambda b, i: (b, i, 0))
    return pl.pallas_call(
        functools.partial(_hgrn_kernel, n_heads=n_heads, n_chunks=tc // HGRN_CHUNK),
        out_shape=jax.ShapeDtypeStruct((B, S, hw), F32),
        grid=(B, S // tc),
        in_specs=[spec, spec, spec, spec, spec, pl.BlockSpec((1, HGRN_DK), lambda b, i: (0, 0))],
        out_specs=spec,
        scratch_shapes=[pltpu.VMEM((n_heads, HGRN_DK, HGRN_DK), F32)],
        compiler_params=pltpu.CompilerParams(
            dimension_semantics=("arbitrary", "arbitrary"), vmem_limit_bytes=VMEM_LIMIT),
        name="hgrn",
    )(q, k, lf, v, gt, norm_g)


def _sc_move_rows(table, src, dst, n_out, chunk):
    M = src.shape[0]
    D = table.shape[1]
    n_workers = SC_CORES * SC_SUBCORES
    per_worker = M // n_workers
    n_chunks = per_worker // chunk
    assert per_worker * n_workers == M and n_chunks * chunk == per_worker and n_chunks % 2 == 0 and chunk % 8 == 0
    mesh = plsc.VectorSubcoreMesh(core_axis_name="c", subcore_axis_name="s")
    idx_t = pltpu.VMEM((chunk,), jnp.int32)
    row_t = pltpu.VMEM((chunk, D), table.dtype)
    sem_t = pltpu.SemaphoreType.DMA

    def body(table_hbm, src_hbm, dst_hbm, out_hbm, src_v, dst_v, rows_v, g_sem, s_sem):
        wid = lax.axis_index("s") * SC_CORES + lax.axis_index("c")
        base = wid * per_worker

        def offset(j):
            return pl.multiple_of(base + j * chunk, 8)

        def gather(b):
            return pltpu.make_async_copy(table_hbm.at[src_v[b]], rows_v[b], g_sem[b])

        def start_gather(j, b):
            pltpu.sync_copy(src_hbm.at[pl.ds(offset(j), chunk)], src_v[b])
            gather(b).start()

        def write_out(j, b):
            if dst_hbm is None:
                pltpu.sync_copy(rows_v[b], out_hbm.at[pl.ds(offset(j), chunk)])
            else:
                pltpu.sync_copy(dst_hbm.at[pl.ds(offset(j), chunk)], dst_v[b])
                pltpu.async_copy(rows_v[b], out_hbm.at[dst_v[b]], s_sem[b]).wait()

        start_gather(0, 0)

        @pl.loop(0, n_chunks, step=2)
        def _(j):
            for b in (0, 1):
                @pl.when(j + b + 1 < n_chunks)
                def _():
                    start_gather(j + b + 1, 1 - b)
                gather(b).wait()
                write_out(j + b, b)

    if dst is None:
        @functools.partial(pl.kernel, mesh=mesh, out_type=jax.ShapeDtypeStruct((n_out, D), table.dtype),
                           scratch_types=[idx_t, idx_t, row_t, row_t, sem_t, sem_t])
        def gather_kernel(table_hbm, src_hbm, out_hbm, s0, s1, r0, r1, g0, g1):
            body(table_hbm, src_hbm, None, out_hbm, (s0, s1), None, (r0, r1), (g0, g1), None)
        return gather_kernel(table, src)

    @functools.partial(pl.kernel, mesh=mesh, out_type=jax.ShapeDtypeStruct((n_out, D), table.dtype),
                       scratch_types=[idx_t, idx_t, idx_t, idx_t, row_t, row_t, sem_t, sem_t, sem_t, sem_t])
    def permute_kernel(table_hbm, src_hbm, dst_hbm, out_hbm, s0, s1, d0, d1, r0, r1, g0, g1, w0, w1):
        body(table_hbm, src_hbm, dst_hbm, out_hbm, (s0, s1), (d0, d1), (r0, r1), (g0, g1), (w0, w1))
    return permute_kernel(table, src, dst)


def _sc_gather(table, idx, chunk):
    return _sc_move_rows(table, idx, None, idx.shape[0], chunk)


def _sc_permute(table, src, dst, n_out, chunk):
    return _sc_move_rows(table, src, dst, n_out, chunk)


def _tile_layout(counts, bm, n_tiles):
    n_groups = counts.shape[0]
    padded = (counts + bm - 1) // bm * bm
    pad_end = jnp.cumsum(padded)
    tile_start = jnp.arange(n_tiles, dtype=jnp.int32) * bm
    tile_group = jnp.minimum(
        jnp.sum((pad_end[None, :] <= tile_start[:, None]).astype(jnp.int32), axis=1), n_groups - 1)
    n_used = (pad_end[-1] // bm).astype(jnp.int32).reshape(1)
    return pad_end - padded, tile_group.astype(jnp.int32), n_used


def _null_partial(rows):
    lane = lax.broadcasted_iota(jnp.int32, (rows, PART_W), 1)
    return jnp.where(lane < ATTN_HEAD_DIM, 0.0, -jnp.inf).astype(F32)


def _moba_sel_kernel(q_ref, km_ref, k_ref, v_ref, idx_ref, cnt_ref, own_ref, cnt_acc, *, n_blocks):
    j = pl.program_id(1)
    T = MOBA_BLOCK
    heads = range(ATTN_HEADS)
    qs = [q_ref[hd // 2] for hd in heads]
    gates = [_dot_nt(km_ref[hd], qs[hd], precision=HIGHEST) for hd in heads]
    blk = lax.broadcasted_iota(jnp.int32, gates[0].shape, 0)
    neg_inf = jnp.float32(-jnp.inf)
    gates = [jnp.where(blk < j, g, neg_inf) for g in gates]
    picks = [[] for _ in heads]
    for _ in range(MOBA_TOPK):
        ms = [jnp.max(g, axis=0, keepdims=True) for g in gates]
        firsts = [jnp.min(jnp.where(g == m, blk, n_blocks), axis=0, keepdims=True) for g, m in zip(gates, ms)]
        for hd in heads:
            picks[hd].append(jnp.where(ms[hd] > neg_inf, firsts[hd], -1))
        gates = [jnp.where(blk == f, neg_inf, g) for g, f in zip(gates, firsts)]

    @pl.when(j == 0)
    def _():
        cnt_acc[...] = jnp.zeros_like(cnt_acc)

    earlier = (lax.broadcasted_iota(jnp.int32, (T, T), 0) < lax.broadcasted_iota(jnp.int32, (T, T), 1)).astype(BF16)
    for hd in heads:
        onehots = [(blk == p).astype(F32) for p in picks[hd]]
        member = onehots[0] + onehots[1] + onehots[2]
        base = cnt_acc[hd] + _dot(member.astype(BF16), earlier)
        ranks = [jnp.sum(oh * base, axis=0, keepdims=True).astype(jnp.int32) for oh in onehots]
        idx_ref[hd] = jnp.concatenate(picks[hd] + ranks + [jnp.zeros((2, T), jnp.int32)], axis=0)
        total = cnt_acc[hd] + jnp.sum(member, axis=1, keepdims=True)
        cnt_acc[hd] = total
        cnt_ref[hd] = total.astype(jnp.int32)
    causal = lax.broadcasted_iota(jnp.int32, (T, T), 1) <= lax.broadcasted_iota(jnp.int32, (T, T), 0)
    scale = ATTN_HEAD_DIM ** -0.5
    ss = [jnp.where(causal, _dot_nt((qs[hd] * scale).astype(BF16), k_ref[hd]), neg_inf) for hd in heads]
    mx = [jnp.max(s, axis=1, keepdims=True) for s in ss]
    ps = [jnp.exp(s - m) for s, m in zip(ss, mx)]
    ls = [jnp.sum(p, axis=1, keepdims=True) for p in ps]
    accs = [_dot(ps[hd].astype(BF16), v_ref[hd]) for hd in heads]
    for hd in heads:
        lse = jnp.broadcast_to(mx[hd] + jnp.log(ls[hd]), (T, PART_W - ATTN_HEAD_DIM))
        own_ref[hd] = jnp.concatenate([accs[hd] / ls[hd], lse], axis=1)


def _moba_sel(aq, kmean, ak, av):
    B, H, S, hd = av.shape
    nb = S // MOBA_BLOCK
    T = MOBA_BLOCK
    blk = lambda b, j: (b, 0, j, 0)
    return pl.pallas_call(
        functools.partial(_moba_sel_kernel, n_blocks=nb),
        out_shape=(jax.ShapeDtypeStruct((B, H, 8, S), jnp.int32),
                   jax.ShapeDtypeStruct((B, H, nb, 1), jnp.int32),
                   jax.ShapeDtypeStruct((B, H, S, PART_W), F32)),
        grid=(B, nb),
        in_specs=[pl.BlockSpec((None, H // 2, T, 128), blk),
                  pl.BlockSpec((None, H, nb, 128), lambda b, j: (b, 0, 0, 0)),
                  pl.BlockSpec((None, H, T, 128), blk),
                  pl.BlockSpec((None, H, T, hd), blk)],
        out_specs=(pl.BlockSpec((None, H, 8, T), lambda b, j: (b, 0, 0, j)),
                   pl.BlockSpec((None, H, nb, 1), lambda b, j: (b, 0, 0, 0)),
                   pl.BlockSpec((None, H, T, PART_W), blk)),
        scratch_shapes=[pltpu.VMEM((H, nb, 1), F32)],
        compiler_params=pltpu.CompilerParams(
            dimension_semantics=("arbitrary", "arbitrary"), vmem_limit_bytes=VMEM_LIMIT),
        name="moba_sel",
    )(aq, kmean, ak, av)


def _moba_blk_kernel(tg_ref, nu_ref, q_ref, *refs):
    n = MOBA_TILES_PER_STEP
    k_refs, v_refs, o_ref = refs[:n], refs[n:2 * n], refs[2 * n]
    R = MOBA_ROWS
    t0 = pl.program_id(0) * n

    @pl.when(t0 < nu_ref[0])
    def _():
        scale = ATTN_HEAD_DIM ** -0.5
        ss = [_dot_nt((q_ref[j * R:(j + 1) * R, :] * scale).astype(BF16), k_refs[j][...]) for j in range(n)]
        ms = [jnp.max(s, axis=1, keepdims=True) for s in ss]
        ps = [jnp.exp(s - m) for s, m in zip(ss, ms)]
        ls = [jnp.sum(p, axis=1, keepdims=True) for p in ps]
        accs = [_dot(p.astype(BF16), v_refs[j][...]) for j, p in enumerate(ps)]
        null = _null_partial(R)
        for j in range(n):
            lse = jnp.broadcast_to(ms[j] + jnp.log(ls[j]), (R, PART_W - ATTN_HEAD_DIM))
            row = jnp.concatenate([accs[j] / ls[j], lse], axis=1)
            o_ref[j * R:(j + 1) * R, :] = jnp.where(t0 + j < nu_ref[0], row, null)

    @pl.when(t0 >= nu_ref[0])
    def _():
        o_ref[...] = _null_partial(n * R)


def _moba_blk(qs, tile_group, n_used, ak, av):
    B, H, S, hd = av.shape
    nb = S // MOBA_BLOCK
    R = MOBA_ROWS
    n = MOBA_TILES_PER_STEP
    n_tiles = qs.shape[0] // R
    assert n_tiles % n == 0
    kv = lambda j: (lambda i, tg, nu: (tg[i * n + j] // nb, tg[i * n + j] % nb, 0, 0))
    grid_spec = pltpu.PrefetchScalarGridSpec(
        num_scalar_prefetch=2,
        grid=(n_tiles // n,),
        in_specs=[pl.BlockSpec((n * R, 128), lambda i, tg, nu: (i, 0))]
        + [pl.BlockSpec((None, None, MOBA_BLOCK, 128), kv(j)) for j in range(n)]
        + [pl.BlockSpec((None, None, MOBA_BLOCK, hd), kv(j)) for j in range(n)],
        out_specs=pl.BlockSpec((n * R, PART_W), lambda i, tg, nu: (i, 0)),
    )
    k4 = ak.reshape(B * H, nb, MOBA_BLOCK, 128)
    v4 = av.reshape(B * H, nb, MOBA_BLOCK, hd)
    return pl.pallas_call(
        _moba_blk_kernel,
        out_shape=jax.ShapeDtypeStruct((n_tiles * R, PART_W), F32),
        grid_spec=grid_spec,
        compiler_params=pltpu.CompilerParams(
            dimension_semantics=("arbitrary",), vmem_limit_bytes=VMEM_LIMIT),
        name="moba_blk",
    )(tile_group, n_used, qs, *([k4] * n), *([v4] * n))


def _moba_merge_kernel(own_ref, pg_ref, g_ref, o_ref):
    hd = ATTN_HEAD_DIM
    rows = [own_ref[...]] + [pg_ref[s] for s in range(MOBA_TOPK)]
    lses = [pltpu.roll(r, hd, 1) for r in rows]
    top = lses[0]
    for z in lses[1:]:
        top = jnp.maximum(top, z)
    num = jnp.zeros_like(top)
    den = jnp.zeros_like(top)
    for r, z in zip(rows, lses):
        w = jnp.exp(z - top)
        num = num + w * r
        den = den + w
    o = (num / den)[:, :hd]
    ms = jnp.mean(o * o, axis=-1, keepdims=True)
    o_ref[...] = o * lax.rsqrt(ms + RMS_EPS) * g_ref[...]


def _moba_merge(own, pg, norm_g):
    n = own.shape[0]
    T = 512
    row = lambda i: (i, 0)
    return pl.pallas_call(
        _moba_merge_kernel,
        out_shape=jax.ShapeDtypeStruct((n, ATTN_HEAD_DIM), F32),
        grid=(n // T,),
        in_specs=[pl.BlockSpec((T, PART_W), row),
                  pl.BlockSpec((MOBA_TOPK, T, PART_W), lambda i: (0, i, 0)),
                  pl.BlockSpec((1, ATTN_HEAD_DIM), lambda i: (0, 0))],
        out_specs=pl.BlockSpec((T, ATTN_HEAD_DIM), row),
        compiler_params=pltpu.CompilerParams(
            dimension_semantics=("arbitrary",), vmem_limit_bytes=VMEM_LIMIT),
        name="moba_merge",
    )(own, pg, norm_g)


def _moba(aq, km, ak, av, norm_g):
    B, H, S, hd = av.shape
    nb = S // MOBA_BLOCK
    n_q = B * H * S
    kmp = km.reshape(B, nb, H // 2, 128)
    half = jnp.arange(128, dtype=jnp.int32) // hd
    kmean = jnp.stack([jnp.where(half == h % 2, kmp[:, :, h // 2, :], 0.0) for h in range(H)], axis=1)
    idx8, counts, own = _moba_sel(aq, kmean, ak, av)
    sel = idx8[:, :, 0:MOBA_TOPK, :].reshape(B * H, MOBA_TOPK, S)
    rank = idx8[:, :, MOBA_TOPK:2 * MOBA_TOPK, :].reshape(B * H, MOBA_TOPK, S)
    n_groups = B * H * nb
    n_tiles = (n_q * MOBA_TOPK) // MOBA_ROWS + n_groups
    pad_start, tile_group, n_used = _tile_layout(counts.reshape(-1), MOBA_ROWS, n_tiles)
    blocks = jnp.arange(nb, dtype=jnp.int32)[:, None]
    start = jnp.sum(jnp.where(sel[:, :, None, :] == blocks, pad_start.reshape(B * H, 1, nb, 1), 0), axis=2)
    a_ids = jnp.arange(n_q * MOBA_TOPK, dtype=jnp.int32).reshape(B * H, MOBA_TOPK, S)
    assert n_tiles * MOBA_ROWS >= n_q * MOBA_TOPK + MOBA_ROWS
    pos = jnp.where(sel >= 0, start + rank, n_used[0] * MOBA_ROWS + a_ids % MOBA_ROWS)
    bh = jnp.arange(B * H, dtype=jnp.int32)[:, None, None]
    t = jnp.arange(S, dtype=jnp.int32)[None, None, :]
    pair_row = jnp.broadcast_to((bh // H * (H // 2) + bh % H // 2) * S + t, pos.shape)
    qs = _sc_permute(aq.reshape(B * (H // 2) * S, 128), pair_row.reshape(-1), pos.reshape(-1),
                     n_tiles * MOBA_ROWS, 256)
    parts = _moba_blk(qs, tile_group, n_used, ak, av)
    pg = _sc_gather(parts, pos.transpose(1, 0, 2).reshape(-1), 256)
    o = _moba_merge(own.reshape(n_q, PART_W), pg.reshape(MOBA_TOPK, n_q, PART_W), norm_g)
    return o.reshape(B, H, S, hd)


def _mix_kernel(oh_ref, oa_ref, x_ref, g1_ref, sc2_ref, sh2_ref, n2_ref, wo_ref, wr_ref, br_ref,
                x1_ref, h2_ref, gw_ref, idx_ref, cnt_ref, cnt_acc):
    cat = jnp.concatenate([oh_ref[...]] + [oa_ref[hd] for hd in range(ATTN_HEADS)], axis=1)
    mix = _dot(cat.astype(BF16), wo_ref[...])
    x1 = x_ref[...] + g1_ref[...] * mix
    x1_ref[...] = x1
    ms = jnp.mean(x1 * x1, axis=-1, keepdims=True)
    h2 = x1 * lax.rsqrt(ms + RMS_EPS) * n2_ref[...]
    h2 = h2 * (1.0 + sc2_ref[...]) + sh2_ref[...]
    h2_ref[...] = _pack_bf16_pairs(h2)
    E = N_EXPERTS
    tm = h2.shape[0]
    h_0 = h2.astype(BF16)
    r_1 = h2 - h_0.astype(F32)
    h_1 = r_1.astype(BF16)
    h_2 = (r_1 - h_1.astype(F32)).astype(BF16)
    wt = wr_ref[...]
    p_0 = _dot_nt(wt, h_0)
    p_1 = _dot_nt(wt[:2 * E], h_1)
    p_2 = _dot_nt(wt[:E], h_2)
    logits = (p_0[:E] + (p_0[E:2 * E] + p_1[:E]) + (p_0[2 * E:] + p_1[E:] + p_2)) + br_ref[...]
    ex = lax.broadcasted_iota(jnp.int32, logits.shape, 0)
    neg_inf = jnp.float32(-jnp.inf)
    vals, idxs = [], []
    for _ in range(TOP_K):
        m = jnp.max(logits, axis=0, keepdims=True)
        first = jnp.min(jnp.where(logits == m, ex, E), axis=0, keepdims=True)
        vals.append(m)
        idxs.append(first)
        logits = jnp.where(ex == first, neg_inf, logits)
    e = [jnp.exp(v - vals[0]) for v in vals]
    denom = e[0] + e[1] + e[2] + e[3]
    gate_rows = jnp.concatenate([ei / denom for ei in e] + [jnp.zeros((128 - TOP_K, tm), F32)], axis=0)
    gw_ref[...] = gate_rows.T[:, :TOP_K]

    @pl.when((pl.program_id(0) == 0) & (pl.program_id(1) == 0))
    def _():
        cnt_acc[...] = jnp.zeros_like(cnt_acc)

    earlier = (lax.broadcasted_iota(jnp.int32, (tm, tm), 0) < lax.broadcasted_iota(jnp.int32, (tm, tm), 1)).astype(BF16)
    onehots = [(ex == ix).astype(F32) for ix in idxs]
    member = onehots[0] + onehots[1] + onehots[2] + onehots[3]
    base = cnt_acc[...] + _dot(member.astype(BF16), earlier)
    ranks = [jnp.sum(oh * base, axis=0, keepdims=True).astype(jnp.int32) for oh in onehots]
    idx_ref[...] = jnp.concatenate(idxs + ranks, axis=0)
    total = cnt_acc[...] + jnp.sum(member, axis=1, keepdims=True)
    cnt_acc[...] = total
    cnt_ref[...] = total.astype(jnp.int32)


def _mix(oh, oa, x, gate1, scale2, shift2, norm2_g, w_out_bf16, w_router, b_router, b0):
    _, S, D = x.shape
    B = oh.shape[0]
    hw = oh.shape[-1]
    tm = 256
    row = lambda b, i: (b, i, 0)
    xrow = lambda b, i: (b0 + b, i, 0)
    vec = lambda b, i: (b, 0, 0)
    const = lambda b, i: (0, 0)
    return pl.pallas_call(
        _mix_kernel,
        out_shape=(jax.ShapeDtypeStruct((B, S, D), F32),
                   jax.ShapeDtypeStruct((B, S, D // 2), jnp.int32),
                   jax.ShapeDtypeStruct((B, S, TOP_K), F32),
                   jax.ShapeDtypeStruct((B, 2 * TOP_K, S), jnp.int32),
                   jax.ShapeDtypeStruct((N_EXPERTS, 1), jnp.int32)),
        grid=(B, S // tm),
        in_specs=[pl.BlockSpec((None, tm, hw), row),
                  pl.BlockSpec((None, ATTN_HEADS, tm, ATTN_HEAD_DIM), lambda b, i: (b, 0, i, 0)),
                  pl.BlockSpec((None, tm, D), xrow),
                  pl.BlockSpec((None, 1, D), vec),
                  pl.BlockSpec((None, 1, D), vec),
                  pl.BlockSpec((None, 1, D), vec),
                  pl.BlockSpec((1, D), const),
                  pl.BlockSpec((D, D), const),
                  pl.BlockSpec((3 * N_EXPERTS, D), const),
                  pl.BlockSpec((N_EXPERTS, 1), const)],
        out_specs=(pl.BlockSpec((None, tm, D), row),
                   pl.BlockSpec((None, tm, D // 2), row),
                   pl.BlockSpec((None, tm, TOP_K), row),
                   pl.BlockSpec((None, 2 * TOP_K, tm), lambda b, i: (b, 0, i)),
                   pl.BlockSpec((N_EXPERTS, 1), const)),
        scratch_shapes=[pltpu.VMEM((N_EXPERTS, 1), F32)],
        compiler_params=pltpu.CompilerParams(
            dimension_semantics=("arbitrary", "arbitrary"), vmem_limit_bytes=VMEM_LIMIT),
        name="mix",
    )(oh, oa, x, gate1, scale2, shift2, norm2_g, w_out_bf16, w_router, b_router)


def _moe_rows_kernel(be_ref, nu_ref, x_ref, wgu_ref, bgu_ref, wd_ref, bd_ref, y_ref, wgu16, wd16, *, d_ff):
    i = pl.program_id(0)

    @pl.when((i == 0) | (be_ref[i] != be_ref[jnp.maximum(i - 1, 0)]))
    def _():
        wgu16[...] = wgu_ref[...].astype(BF16)
        wd16[...] = wd_ref[...].astype(BF16)

    @pl.when(i < nu_ref[0])
    def _():
        gu = _dot(_unpack_bf16_pairs(x_ref[...]).astype(BF16), wgu16[...]) + bgu_ref[...]
        gate = jnp.minimum(gu[:, :d_ff], SWIGLU_LIMIT)
        up = jnp.clip(gu[:, d_ff:], -SWIGLU_LIMIT, SWIGLU_LIMIT)
        act = (up + 1.0) * gate * _sigmoid(SWIGLU_ALPHA * gate)
        y_ref[...] = _pack_bf16_pairs(_dot(act.astype(BF16), wd16[...]) + bd_ref[...])

    @pl.when(i >= nu_ref[0])
    def _():
        y_ref[...] = jnp.zeros_like(y_ref)


def _moe_rows(xs, blk_expert, n_used, wgu, bgu, wd, bd):
    D = 2 * xs.shape[1]
    bm = MOE_ROWS
    n_blk = xs.shape[0] // bm
    d_ff = wd.shape[1]
    wsel = lambda i, be, nu: (be[i], 0, 0)
    grid_spec = pltpu.PrefetchScalarGridSpec(
        num_scalar_prefetch=2,
        grid=(n_blk,),
        in_specs=[pl.BlockSpec((bm, D // 2), lambda i, be, nu: (i, 0)),
                  pl.BlockSpec((None, D, 2 * d_ff), wsel),
                  pl.BlockSpec((None, 1, 2 * d_ff), wsel),
                  pl.BlockSpec((None, d_ff, D), wsel),
                  pl.BlockSpec((None, 1, D), wsel)],
        out_specs=pl.BlockSpec((bm, D // 2), lambda i, be, nu: (i, 0)),
        scratch_shapes=[pltpu.VMEM((D, 2 * d_ff), BF16), pltpu.VMEM((d_ff, D), BF16)],
    )
    return pl.pallas_call(
        functools.partial(_moe_rows_kernel, d_ff=d_ff),
        out_shape=jax.ShapeDtypeStruct((n_blk * bm, D // 2), jnp.int32),
        grid_spec=grid_spec,
        compiler_params=pltpu.CompilerParams(
            dimension_semantics=("arbitrary",), vmem_limit_bytes=VMEM_LIMIT),
        name="moe_rows",
    )(blk_expert, n_used, xs, wgu, bgu.reshape(N_EXPERTS, 1, 2 * d_ff), wd, bd.reshape(N_EXPERTS, 1, D))


def _combine_rows_kernel(*refs):
    y_refs = refs[:TOP_K]
    gw_ref, x1_ref, g2_ref, fg_ref = refs[TOP_K:TOP_K + 4]
    o_ref = refs[-1]
    gw = gw_ref[...]
    y = gw[:, 0:1] * _unpack_bf16_pairs(y_refs[0][...])
    for kk in range(1, TOP_K):
        y = y + gw[:, kk:kk + 1] * _unpack_bf16_pairs(y_refs[kk][...])
    x2 = x1_ref[...] + g2_ref[...] * y
    ms = jnp.mean(x2 * x2, axis=-1, keepdims=True)
    o_ref[...] = x2 * lax.rsqrt(ms + RMS_EPS) * fg_ref[...]


def _combine_rows(yg, gates, x1, gate2, final_g, out_so_far, b0, n_batches):
    S, D = x1.shape
    tm = 256
    steps = S // tm
    slot_spec = lambda kk: pl.BlockSpec((tm, D // 2), lambda i: (kk * steps + i, 0))
    in_specs = [slot_spec(kk) for kk in range(TOP_K)] + [
        pl.BlockSpec((tm, TOP_K), lambda i: (i, 0)),
        pl.BlockSpec((tm, D), lambda i: (i, 0)),
        pl.BlockSpec((1, D), lambda i: (0, 0)),
        pl.BlockSpec((1, D), lambda i: (0, 0))]
    args = [yg] * TOP_K + [gates, x1, gate2, final_g]
    aliases = {}
    if out_so_far is not None:
        in_specs.append(pl.BlockSpec(memory_space=pl.ANY))
        aliases = {len(args): 0}
        args.append(out_so_far)
    return pl.pallas_call(
        _combine_rows_kernel,
        out_shape=jax.ShapeDtypeStruct((n_batches * S, D), F32),
        grid=(steps,),
        in_specs=in_specs,
        out_specs=pl.BlockSpec((tm, D), lambda i: (b0 * steps + i, 0)),
        input_output_aliases=aliases,
        compiler_params=pltpu.CompilerParams(
            dimension_semantics=("arbitrary",), vmem_limit_bytes=VMEM_LIMIT),
        name="combine_rows",
    )(*args)


def _split_bf16x3(w):
    def top(v):
        return lax.bitcast_convert_type(lax.bitcast_convert_type(v, jnp.int32) & jnp.int32(-65536), F32)
    w0 = top(w)
    w1 = top(w - w0)
    w2 = w - w0 - w1
    return jnp.concatenate([w0, w1, w2], axis=1).astype(BF16).T


def _rotary_tables(positions):
    half = ROT_DIM // 2
    inv_freq = jnp.exp(-math.log(ROPE_THETA) * jnp.arange(0, ROT_DIM, 2, dtype=F32) / ROT_DIM)
    d = jnp.arange(128, dtype=jnp.int32) % ATTN_HEAD_DIM
    freq = jnp.where(d < ROT_DIM, inv_freq[d % half], 0.0)
    sign = jnp.where(d < half, -1.0, 1.0)
    ang = positions.astype(F32)[:, :, None] * freq
    return jnp.cos(ang), jnp.sin(ang) * sign


def kernel(x, c, positions, w_ada, b_ada, norm1_g, w_in, hgrn_lb_logits, hgrn_norm_g, attn_norm_g,
           w_out, norm2_g, w_router, b_router, w_gate_up, b_gate_up, w_down, b_down, final_norm_g):
    B, S, D = x.shape
    assert w_in.shape[0] == 1, "single-layer block: the final norm is fused into the combine step"
    l = 0
    ctab, stab = _rotary_tables(positions)
    lower_bounds = jnp.cumsum(jax.nn.softmax(hgrn_lb_logits.astype(F32), axis=0), axis=0)
    mod = _ada(c, w_ada[l], b_ada[l])
    shift1, scale1, gate1, shift2, scale2, gate2 = jnp.split(mod[:, None, :], N_MOD, axis=-1)
    n_blk = (S * TOP_K) // MOE_ROWS + N_EXPERTS
    experts = jnp.arange(N_EXPERTS, dtype=jnp.int32)[:, None]
    tok = jnp.broadcast_to(jnp.arange(S, dtype=jnp.int32)[None, :], (TOP_K, S)).reshape(-1)
    out = None
    for b in range(B):
        one = slice(b, b + 1)
        q, k, lf, v, gt, aq, ak, av, km = _proj(
            x, scale1[one], shift1[one], norm1_g[l][None], w_in[l].astype(BF16), lower_bounds[l][None],
            ctab, stab, b)
        o_a = _moba(aq, km, ak, av, attn_norm_g[l][None])
        o_h = _hgrn(q, k, lf, v, gt, hgrn_norm_g[l][None])
        x1, h2, gates, idx8, counts = _mix(
            o_h, o_a, x, gate1[one], scale2[one], shift2[one], norm2_g[l][None], w_out[l].astype(BF16),
            _split_bf16x3(w_router[l]), b_router[l][:, None], b)
        pad_start, blk_expert, n_used = _tile_layout(counts.reshape(-1), MOE_ROWS, n_blk)
        chosen = idx8[0, 0:TOP_K, :]
        pos = jnp.sum(jnp.where(chosen[:, None, :] == experts, pad_start[:, None], 0), axis=1) + idx8[0, TOP_K:, :]
        xs = _sc_permute(h2.reshape(S, D // 2), tok, pos.reshape(-1), n_blk * MOE_ROWS, 64)
        y_sorted = _moe_rows(xs, blk_expert, n_used, w_gate_up[l], b_gate_up[l], w_down[l], b_down[l])
        yg = _sc_gather(y_sorted, pos.reshape(-1), 64)
        out = _combine_rows(yg, gates.reshape(S, TOP_K), x1.reshape(S, D), gate2[b], final_norm_g[None], out, b, B)
    return out.reshape(B, S, D)
```

```python
import functools
import math

import jax
import jax.numpy as jnp
from jax import lax
from jax.experimental import pallas as pl
from jax.experimental.pallas import tpu as pltpu
from jax.experimental.pallas import tpu_sc as plsc

F32 = jnp.float32
BF16 = jnp.bfloat16
HIGHEST = lax.Precision.HIGHEST

HGRN_DK = 128
HGRN_CHUNK = 64
HGRN_SUB = 16
ATTN_HEADS = 4
ATTN_HEAD_DIM = 64
ROT_DIM = ATTN_HEAD_DIM // 4
ROPE_THETA = 500000.0
MOBA_BLOCK = 256
MOBA_TOPK = 3
N_EXPERTS = 32
TOP_K = 4
SWIGLU_ALPHA = 1.702
SWIGLU_LIMIT = 7.0
N_MOD = 6
RMS_EPS = 1e-6

EXP_CLAMP = 80.0
MOE_ROWS = 512
MOBA_ROWS = 256
MOBA_TILES_PER_STEP = 8
PART_W = 128
V7X_VMEM_BYTES = 64 * 1024 * 1024
VMEM_LIMIT = V7X_VMEM_BYTES * 7 // 8
SC_CORES = 2
SC_SUBCORES = 16


def _sigmoid(x):
    return 1.0 / (1.0 + jnp.exp(-x))


def _dot(a, b, **kw):
    return jnp.dot(a, b, preferred_element_type=F32, **kw)


def _dot_nt(a, b, **kw):
    return lax.dot_general(a, b, (((1,), (1,)), ((), ())), preferred_element_type=F32, **kw)


def _pack_bf16_pairs(x):
    w = x.shape[1] // 2
    bits = lax.bitcast_convert_type(x.astype(BF16).astype(F32), jnp.int32)
    return bits[:, w:] | lax.shift_right_logical(bits[:, :w], 16)


def _unpack_bf16_pairs(p):
    lo = lax.bitcast_convert_type(lax.shift_left(p, 16), F32)
    hi = lax.bitcast_convert_type(p & jnp.int32(-65536), F32)
    return jnp.concatenate([lo, hi], axis=1)


def _ada_kernel(c_ref, w_ref, b_ref, o_ref):
    c = c_ref[...]
    o_ref[...] = _dot(c * _sigmoid(c), w_ref[...], precision=HIGHEST) + b_ref[...]


def _ada(c, w_ada, b_ada):
    B, D = c.shape
    N = w_ada.shape[1]
    tn = N // 4
    c8 = jnp.zeros((8, D), F32).at[:B].set(c)
    out = pl.pallas_call(
        _ada_kernel,
        out_shape=jax.ShapeDtypeStruct((8, N), F32),
        grid=(N // tn,),
        in_specs=[pl.BlockSpec((8, D), lambda j: (0, 0)),
                  pl.BlockSpec((D, tn), lambda j: (0, j)),
                  pl.BlockSpec((1, tn), lambda j: (0, j))],
        out_specs=pl.BlockSpec((8, tn), lambda j: (0, j)),
        compiler_params=pltpu.CompilerParams(vmem_limit_bytes=VMEM_LIMIT),
        name="ada",
    )(c8, w_ada, b_ada.reshape(1, N))
    return out[:B]


def _proj_kernel(x_ref, sc_ref, sh_ref, g_ref, w_ref, lb_ref, ct_ref, st_ref,
                 q_ref, k_ref, lf_ref, v_ref, gt_ref, aq_ref, ak_ref, av_ref, km_ref,
                 *, hw, aw):
    x = x_ref[...]
    ms = jnp.mean(x * x, axis=-1, keepdims=True)
    h = x * lax.rsqrt(ms + RMS_EPS) * g_ref[...]
    h = h * (1.0 + sc_ref[...]) + sh_ref[...]
    proj = _dot(h.astype(BF16), w_ref[...])

    hq = proj[:, 0:hw]
    hf = proj[:, hw:2 * hw]
    hg = proj[:, 3 * hw:4 * hw]
    q_ref[...] = hq * _sigmoid(hq) * (HGRN_DK ** -0.5)
    lb = lb_ref[...]
    f = lb + (1.0 - lb) * _sigmoid(hf)
    k_ref[...] = 1.0 - f
    lf_ref[...] = jnp.log(f)
    v_ref[...] = proj[:, 2 * hw:3 * hw]
    gt_ref[...] = hg * _sigmoid(hg)

    ct = jnp.concatenate([ct_ref[...]] * (aw // 128), axis=1)
    st = jnp.concatenate([st_ref[...]] * (aw // 128), axis=1)
    lane = lax.broadcasted_iota(jnp.int32, ct.shape, 1) % ATTN_HEAD_DIM
    first_half = lane < (ROT_DIM // 2)

    def rot(t):
        partner = jnp.where(first_half, pltpu.roll(t, aw - ROT_DIM // 2, 1), pltpu.roll(t, ROT_DIM // 2, 1))
        return t * ct + partner * st

    base = 4 * hw
    aq = rot(proj[:, base:base + aw])
    ak = rot(proj[:, base + aw:base + 2 * aw])
    av = proj[:, base + 2 * aw:base + 3 * aw]
    km_ref[...] = jnp.mean(ak, axis=0, keepdims=True)
    lane128 = lax.broadcasted_iota(jnp.int32, (x.shape[0], 128), 1)
    for pair in range(ATTN_HEADS // 2):
        aq_ref[pair] = aq[:, pair * 128:(pair + 1) * 128]
    for hd in range(ATTN_HEADS):
        pair, half = divmod(hd, 2)
        in_head = (lane128 // ATTN_HEAD_DIM) == half
        ak_ref[hd] = jnp.where(in_head, ak[:, pair * 128:(pair + 1) * 128], 0.0).astype(BF16)
        av_ref[hd] = av[:, hd * ATTN_HEAD_DIM:(hd + 1) * ATTN_HEAD_DIM].astype(BF16)


def _proj(x, scale1, shift1, norm_g, w_in_bf16, lb, ctab, stab, b0):
    _, S, D = x.shape
    B = scale1.shape[0]
    hw = lb.shape[-1]
    aw = ATTN_HEADS * ATTN_HEAD_DIM
    tm = MOBA_BLOCK
    nb = S // MOBA_BLOCK
    n_proj = w_in_bf16.shape[1]
    row = lambda b, i: (b, i, 0)
    xrow = lambda b, i: (b0 + b, i, 0)
    vec = lambda b, i: (b, 0, 0)
    head = lambda b, i: (b, 0, i, 0)
    out_shapes = (
        jax.ShapeDtypeStruct((B, S, hw), F32),
        jax.ShapeDtypeStruct((B, S, hw), F32),
        jax.ShapeDtypeStruct((B, S, hw), F32),
        jax.ShapeDtypeStruct((B, S, hw), F32),
        jax.ShapeDtypeStruct((B, S, hw), F32),
        jax.ShapeDtypeStruct((B, ATTN_HEADS // 2, S, 128), F32),
        jax.ShapeDtypeStruct((B, ATTN_HEADS, S, 128), BF16),
        jax.ShapeDtypeStruct((B, ATTN_HEADS, S, ATTN_HEAD_DIM), BF16),
        jax.ShapeDtypeStruct((B, nb, 1, aw), F32),
    )
    hspec = pl.BlockSpec((None, tm, hw), row)
    aspec = pl.BlockSpec((None, ATTN_HEADS, tm, ATTN_HEAD_DIM), head)
    return pl.pallas_call(
        functools.partial(_proj_kernel, hw=hw, aw=aw),
        out_shape=out_shapes,
        grid=(B, S // tm),
        in_specs=[pl.BlockSpec((None, tm, D), xrow),
                  pl.BlockSpec((None, 1, D), vec),
                  pl.BlockSpec((None, 1, D), vec),
                  pl.BlockSpec((1, D), lambda b, i: (0, 0)),
                  pl.BlockSpec((D, n_proj), lambda b, i: (0, 0)),
                  pl.BlockSpec((1, hw), lambda b, i: (0, 0)),
                  pl.BlockSpec((None, tm, 128), xrow),
                  pl.BlockSpec((None, tm, 128), xrow)],
        out_specs=(hspec, hspec, hspec, hspec, hspec,
                   pl.BlockSpec((None, ATTN_HEADS // 2, tm, 128), head),
                   pl.BlockSpec((None, ATTN_HEADS, tm, 128), head), aspec,
                   pl.BlockSpec((None, None, 1, aw), lambda b, i: (b, i, 0, 0))),
        compiler_params=pltpu.CompilerParams(
            dimension_semantics=("arbitrary", "arbitrary"), vmem_limit_bytes=VMEM_LIMIT),
        name="proj",
    )(x, scale1, shift1, norm_g, w_in_bf16, lb, ctab, stab)


def _hgrn_kernel(q_ref, k_ref, lf_ref, v_ref, gt_ref, gn_ref, o_ref, st_ref, *, n_heads, n_chunks):
    @pl.when(pl.program_id(1) == 0)
    def _():
        st_ref[...] = jnp.zeros_like(st_ref)

    C = HGRN_CHUNK
    r = lax.broadcasted_iota(jnp.int32, (C, C), 0)
    c = lax.broadcasted_iota(jnp.int32, (C, C), 1)
    tril = c <= r
    ltri = tril.astype(F32)
    gn = gn_ref[...]

    def chunk(ci, carry):
        r0 = pl.multiple_of(ci * C, C)
        rows = pl.ds(r0, C)
        b_all = _dot(ltri, lf_ref[rows, :], precision=HIGHEST)
        heads = range(n_heads)
        sls = [slice(hd * HGRN_DK, (hd + 1) * HGRN_DK) for hd in heads]
        bs = [b_all[:, sl] for sl in sls]
        b_lasts = [b[C - 1:C, :] for b in bs]
        qs = [q_ref[rows, sl] for sl in sls]
        ks = [k_ref[rows, sl] for sl in sls]
        vs = [v_ref[rows, sl] for sl in sls]
        states = [st_ref[hd] for hd in heads]
        o_inter = [_dot_nt((qs[hd] * jnp.exp(bs[hd])).astype(BF16), states[hd].astype(BF16)) for hd in heads]
        scores = []
        for hd in heads:
            blocks = []
            for g0 in range(0, C, HGRN_SUB):
                g1 = g0 + HGRN_SUB
                rho = 0.5 * (bs[hd][g0:g0 + 1, :] + bs[hd][g1 - 1:g1, :])
                qa = qs[hd][g0:g1, :] * jnp.exp(jnp.minimum(bs[hd][g0:g1, :] - rho, EXP_CLAMP))
                kb = ks[hd] * jnp.exp(jnp.minimum(rho - bs[hd], EXP_CLAMP))
                blocks.append(_dot_nt(qa.astype(BF16), kb.astype(BF16)))
            scores.append(jnp.where(tril, jnp.concatenate(blocks, axis=0), 0.0).astype(BF16))
        outs = [o_inter[hd] + _dot(scores[hd], vs[hd].astype(BF16)) for hd in heads]
        kds = [(ks[hd] * jnp.exp(b_lasts[hd] - bs[hd])).astype(BF16) for hd in heads]
        upds = [_dot(vs[hd].T.astype(BF16), kds[hd]) for hd in heads]
        for hd in heads:
            st_ref[hd] = states[hd] * jnp.exp(b_lasts[hd]) + upds[hd]
            o = outs[hd]
            ms = jnp.mean(o * o, axis=-1, keepdims=True)
            o_ref[rows, sls[hd]] = o * lax.rsqrt(ms + RMS_EPS) * gn * gt_ref[rows, sls[hd]]
        return carry

    lax.fori_loop(0, n_chunks, chunk, 0, unroll=True)


def _hgrn(q, k, lf, v, gt, norm_g):
    B, S, hw = q.shape
    n_heads = hw // HGRN_DK
    tc = 512
    spec = pl.BlockSpec((None, tc, hw), lambda b, i: (b, i, 0))
    return pl.pallas_call(
        functools.partial(_hgrn_kernel, n_heads=n_heads, n_chunks=tc // HGRN_CHUNK),
        out_shape=jax.ShapeDtypeStruct((B, S, hw), F32),
        grid=(B, S // tc),
        in_specs=[spec, spec, spec, spec, spec, pl.BlockSpec((1, HGRN_DK), lambda b, i: (0, 0))],
        out_specs=spec,
        scratch_shapes=[pltpu.VMEM((n_heads, HGRN_DK, HGRN_DK), F32)],
        compiler_params=pltpu.CompilerParams(
            dimension_semantics=("arbitrary", "arbitrary"), vmem_limit_bytes=VMEM_LIMIT),
        name="hgrn",
    )(q, k, lf, v, gt, norm_g)


def _sc_move_rows(table, src, dst, n_out, chunk):
    M = src.shape[0]
    D = table.shape[1]
    n_workers = SC_CORES * SC_SUBCORES
    per_worker = M // n_workers
    n_chunks = per_worker // chunk
    assert per_worker * n_workers == M and n_chunks * chunk == per_worker and n_chunks % 2 == 0 and chunk % 8 == 0
    mesh = plsc.VectorSubcoreMesh(core_axis_name="c", subcore_axis_name="s")
    idx_t = pltpu.VMEM((chunk,), jnp.int32)
    row_t = pltpu.VMEM((chunk, D), table.dtype)
    sem_t = pltpu.SemaphoreType.DMA

    def body(table_hbm, src_hbm, dst_hbm, out_hbm, src_v, dst_v, rows_v, g_sem, s_sem):
        wid = lax.axis_index("s") * SC_CORES + lax.axis_index("c")
        base = wid * per_worker

        def offset(j):
            return pl.multiple_of(base + j * chunk, 8)

        def gather(b):
            return pltpu.make_async_copy(table_hbm.at[src_v[b]], rows_v[b], g_sem[b])

        def start_gather(j, b):
            pltpu.sync_copy(src_hbm.at[pl.ds(offset(j), chunk)], src_v[b])
            gather(b).start()

        def write_out(j, b):
            if dst_hbm is None:
                pltpu.sync_copy(rows_v[b], out_hbm.at[pl.ds(offset(j), chunk)])
            else:
                pltpu.sync_copy(dst_hbm.at[pl.ds(offset(j), chunk)], dst_v[b])
                pltpu.async_copy(rows_v[b], out_hbm.at[dst_v[b]], s_sem[b]).wait()

        start_gather(0, 0)

        @pl.loop(0, n_chunks, step=2)
        def _(j):
            for b in (0, 1):
                @pl.when(j + b + 1 < n_chunks)
                def _():
                    start_gather(j + b + 1, 1 - b)
                gather(b).wait()
                write_out(j + b, b)

    if dst is None:
        @functools.partial(pl.kernel, mesh=mesh, out_type=jax.ShapeDtypeStruct((n_out, D), table.dtype),
                           scratch_types=[idx_t, idx_t, row_t, row_t, sem_t, sem_t])
        def gather_kernel(table_hbm, src_hbm, out_hbm, s0, s1, r0, r1, g0, g1):
            body(table_hbm, src_hbm, None, out_hbm, (s0, s1), None, (r0, r1), (g0, g1), None)
        return gather_kernel(table, src)

    @functools.partial(pl.kernel, mesh=mesh, out_type=jax.ShapeDtypeStruct((n_out, D), table.dtype),
                       scratch_types=[idx_t, idx_t, idx_t, idx_t, row_t, row_t, sem_t, sem_t, sem_t, sem_t])
    def permute_kernel(table_hbm, src_hbm, dst_hbm, out_hbm, s0, s1, d0, d1, r0, r1, g0, g1, w0, w1):
        body(table_hbm, src_hbm, dst_hbm, out_hbm, (s0, s1), (d0, d1), (r0, r1), (g0, g1), (w0, w1))
    return permute_kernel(table, src, dst)


def _sc_gather(table, idx, chunk):
    return _sc_move_rows(table, idx, None, idx.shape[0], chunk)


def _sc_permute(table, src, dst, n_out, chunk):
    return _sc_move_rows(table, src, dst, n_out, chunk)


def _tile_layout(counts, bm, n_tiles):
    n_groups = counts.shape[0]
    padded = (counts + bm - 1) // bm * bm
    pad_end = jnp.cumsum(padded)
    tile_start = jnp.arange(n_tiles, dtype=jnp.int32) * bm
    tile_group = jnp.minimum(
        jnp.sum((pad_end[None, :] <= tile_start[:, None]).astype(jnp.int32), axis=1), n_groups - 1)
    n_used = (pad_end[-1] // bm).astype(jnp.int32).reshape(1)
    return pad_end - padded, tile_group.astype(jnp.int32), n_used


def _null_partial(rows):
    lane = lax.broadcasted_iota(jnp.int32, (rows, PART_W), 1)
    return jnp.where(lane < ATTN_HEAD_DIM, 0.0, -jnp.inf).astype(F32)


def _moba_sel_kernel(q_ref, km_ref, k_ref, v_ref, idx_ref, cnt_ref, own_ref, cnt_acc, *, n_blocks):
    j = pl.program_id(1)
    T = MOBA_BLOCK
    heads = range(ATTN_HEADS)
    qs = [q_ref[hd // 2] for hd in heads]
    gates = [_dot_nt(km_ref[hd], qs[hd], precision=HIGHEST) for hd in heads]
    blk = lax.broadcasted_iota(jnp.int32, gates[0].shape, 0)
    neg_inf = jnp.float32(-jnp.inf)
    gates = [jnp.where(blk < j, g, neg_inf) for g in gates]
    picks = [[] for _ in heads]
    for _ in range(MOBA_TOPK):
        ms = [jnp.max(g, axis=0, keepdims=True) for g in gates]
        firsts = [jnp.min(jnp.where(g == m, blk, n_blocks), axis=0, keepdims=True) for g, m in zip(gates, ms)]
        for hd in heads:
            picks[hd].append(jnp.where(ms[hd] > neg_inf, firsts[hd], -1))
        gates = [jnp.where(blk == f, neg_inf, g) for g, f in zip(gates, firsts)]

    @pl.when(j == 0)
    def _():
        cnt_acc[...] = jnp.zeros_like(cnt_acc)

    earlier = (lax.broadcasted_iota(jnp.int32, (T, T), 0) < lax.broadcasted_iota(jnp.int32, (T, T), 1)).astype(BF16)
    for hd in heads:
        onehots = [(blk == p).astype(F32) for p in picks[hd]]
        member = onehots[0] + onehots[1] + onehots[2]
        base = cnt_acc[hd] + _dot(member.astype(BF16), earlier)
        ranks = [jnp.sum(oh * base, axis=0, keepdims=True).astype(jnp.int32) for oh in onehots]
        idx_ref[hd] = jnp.concatenate(picks[hd] + ranks + [jnp.zeros((2, T), jnp.int32)], axis=0)
        total = cnt_acc[hd] + jnp.sum(member, axis=1, keepdims=True)
        cnt_acc[hd] = total
        cnt_ref[hd] = total.astype(jnp.int32)
    causal = lax.broadcasted_iota(jnp.int32, (T, T), 1) <= lax.broadcasted_iota(jnp.int32, (T, T), 0)
    scale = ATTN_HEAD_DIM ** -0.5
    ss = [jnp.where(causal, _dot_nt((qs[hd] * scale).astype(BF16), k_ref[hd]), neg_inf) for hd in heads]
    mx = [jnp.max(s, axis=1, keepdims=True) for s in ss]
    ps = [jnp.exp(s - m) for s, m in zip(ss, mx)]
    ls = [jnp.sum(p, axis=1, keepdims=True) for p in ps]
    accs = [_dot(ps[hd].astype(BF16), v_ref[hd]) for hd in heads]
    for hd in heads:
        lse = jnp.broadcast_to(mx[hd] + jnp.log(ls[hd]), (T, PART_W - ATTN_HEAD_DIM))
        own_ref[hd] = jnp.concatenate([accs[hd] / ls[hd], lse], axis=1)


def _moba_sel(aq, kmean, ak, av):
    B, H, S, hd = av.shape
    nb = S // MOBA_BLOCK
    T = MOBA_BLOCK
    blk = lambda b, j: (b, 0, j, 0)
    return pl.pallas_call(
        functools.partial(_moba_sel_kernel, n_blocks=nb),
        out_shape=(jax.ShapeDtypeStruct((B, H, 8, S), jnp.int32),
                   jax.ShapeDtypeStruct((B, H, nb, 1), jnp.int32),
                   jax.ShapeDtypeStruct((B, H, S, PART_W), F32)),
        grid=(B, nb),
        in_specs=[pl.BlockSpec((None, H // 2, T, 128), blk),
                  pl.BlockSpec((None, H, nb, 128), lambda b, j: (b, 0, 0, 0)),
                  pl.BlockSpec((None, H, T, 128), blk),
                  pl.BlockSpec((None, H, T, hd), blk)],
        out_specs=(pl.BlockSpec((None, H, 8, T), lambda b, j: (b, 0, 0, j)),
                   pl.BlockSpec((None, H, nb, 1), lambda b, j: (b, 0, 0, 0)),
                   pl.BlockSpec((None, H, T, PART_W), blk)),
        scratch_shapes=[pltpu.VMEM((H, nb, 1), F32)],
        compiler_params=pltpu.CompilerParams(
            dimension_semantics=("arbitrary", "arbitrary"), vmem_limit_bytes=VMEM_LIMIT),
        name="moba_sel",
    )(aq, kmean, ak, av)


def _moba_blk_kernel(tg_ref, nu_ref, q_ref, *refs):
    n = MOBA_TILES_PER_STEP
    k_refs, v_refs, o_ref = refs[:n], refs[n:2 * n], refs[2 * n]
    R = MOBA_ROWS
    t0 = pl.program_id(0) * n

    @pl.when(t0 < nu_ref[0])
    def _():
        scale = ATTN_HEAD_DIM ** -0.5
        ss = [_dot_nt((q_ref[j * R:(j + 1) * R, :] * scale).astype(BF16), k_refs[j][...]) for j in range(n)]
        ms = [jnp.max(s, axis=1, keepdims=True) for s in ss]
        ps = [jnp.exp(s - m) for s, m in zip(ss, ms)]
        ls = [jnp.sum(p, axis=1, keepdims=True) for p in ps]
        accs = [_dot(p.astype(BF16), v_refs[j][...]) for j, p in enumerate(ps)]
        null = _null_partial(R)
        for j in range(n):
            lse = jnp.broadcast_to(ms[j] + jnp.log(ls[j]), (R, PART_W - ATTN_HEAD_DIM))
            row = jnp.concatenate([accs[j] / ls[j], lse], axis=1)
            o_ref[j * R:(j + 1) * R, :] = jnp.where(t0 + j < nu_ref[0], row, null)

    @pl.when(t0 >= nu_ref[0])
    def _():
        o_ref[...] = _null_partial(n * R)


def _moba_blk(qs, tile_group, n_used, ak, av):
    B, H, S, hd = av.shape
    nb = S // MOBA_BLOCK
    R = MOBA_ROWS
    n = MOBA_TILES_PER_STEP
    n_tiles = qs.shape[0] // R
    assert n_tiles % n == 0
    kv = lambda j: (lambda i, tg, nu: (tg[i * n + j] // nb, tg[i * n + j] % nb, 0, 0))
    grid_spec = pltpu.PrefetchScalarGridSpec(
        num_scalar_prefetch=2,
        grid=(n_tiles // n,),
        in_specs=[pl.BlockSpec((n * R, 128), lambda i, tg, nu: (i, 0))]
        + [pl.BlockSpec((None, None, MOBA_BLOCK, 128), kv(j)) for j in range(n)]
        + [pl.BlockSpec((None, None, MOBA_BLOCK, hd), kv(j)) for j in range(n)],
        out_specs=pl.BlockSpec((n * R, PART_W), lambda i, tg, nu: (i, 0)),
    )
    k4 = ak.reshape(B * H, nb, MOBA_BLOCK, 128)
    v4 = av.reshape(B * H, nb, MOBA_BLOCK, hd)
    return pl.pallas_call(
        _moba_blk_kernel,
        out_shape=jax.ShapeDtypeStruct((n_tiles * R, PART_W), F32),
        grid_spec=grid_spec,
        compiler_params=pltpu.CompilerParams(
            dimension_semantics=("arbitrary",), vmem_limit_bytes=VMEM_LIMIT),
        name="moba_blk",
    )(tile_group, n_used, qs, *([k4] * n), *([v4] * n))


def _moba_merge_kernel(own_ref, pg_ref, g_ref, o_ref):
    hd = ATTN_HEAD_DIM
    rows = [own_ref[...]] + [pg_ref[s] for s in range(MOBA_TOPK)]
    lses = [pltpu.roll(r, hd, 1) for r in rows]
    top = lses[0]
    for z in lses[1:]:
        top = jnp.maximum(top, z)
    num = jnp.zeros_like(top)
    den = jnp.zeros_like(top)
    for r, z in zip(rows, lses):
        w = jnp.exp(z - top)
        num = num + w * r
        den = den + w
    o = (num / den)[:, :hd]
    ms = jnp.mean(o * o, axis=-1, keepdims=True)
    o_ref[...] = o * lax.rsqrt(ms + RMS_EPS) * g_ref[...]


def _moba_merge(own, pg, norm_g):
    n = own.shape[0]
    T = 512
    row = lambda i: (i, 0)
    return pl.pallas_call(
        _moba_merge_kernel,
        out_shape=jax.ShapeDtypeStruct((n, ATTN_HEAD_DIM), F32),
        grid=(n // T,),
        in_specs=[pl.BlockSpec((T, PART_W), row),
                  pl.BlockSpec((MOBA_TOPK, T, PART_W), lambda i: (0, i, 0)),
                  pl.BlockSpec((1, ATTN_HEAD_DIM), lambda i: (0, 0))],
        out_specs=pl.BlockSpec((T, ATTN_HEAD_DIM), row),
        compiler_params=pltpu.CompilerParams(
            dimension_semantics=("arbitrary",), vmem_limit_bytes=VMEM_LIMIT),
        name="moba_merge",
    )(own, pg, norm_g)


def _moba(aq, km, ak, av, norm_g):
    B, H, S, hd = av.shape
    nb = S // MOBA_BLOCK
    n_q = B * H * S
    kmp = km.reshape(B, nb, H // 2, 128)
    half = jnp.arange(128, dtype=jnp.int32) // hd
    kmean = jnp.stack([jnp.where(half == h % 2, kmp[:, :, h // 2, :], 0.0) for h in range(H)], axis=1)
    idx8, counts, own = _moba_sel(aq, kmean, ak, av)
    sel = idx8[:, :, 0:MOBA_TOPK, :].reshape(B * H, MOBA_TOPK, S)
    rank = idx8[:, :, MOBA_TOPK:2 * MOBA_TOPK, :].reshape(B * H, MOBA_TOPK, S)
    n_groups = B * H * nb
    n_tiles = (n_q * MOBA_TOPK) // MOBA_ROWS + n_groups
    pad_start, tile_group, n_used = _tile_layout(counts.reshape(-1), MOBA_ROWS, n_tiles)
    blocks = jnp.arange(nb, dtype=jnp.int32)[:, None]
    start = jnp.sum(jnp.where(sel[:, :, None, :] == blocks, pad_start.reshape(B * H, 1, nb, 1), 0), axis=2)
    a_ids = jnp.arange(n_q * MOBA_TOPK, dtype=jnp.int32).reshape(B * H, MOBA_TOPK, S)
    assert n_tiles * MOBA_ROWS >= n_q * MOBA_TOPK + MOBA_ROWS
    pos = jnp.where(sel >= 0, start + rank, n_used[0] * MOBA_ROWS + a_ids % MOBA_ROWS)
    bh = jnp.arange(B * H, dtype=jnp.int32)[:, None, None]
    t = jnp.arange(S, dtype=jnp.int32)[None, None, :]
    pair_row = jnp.broadcast_to((bh // H * (H // 2) + bh % H // 2) * S + t, pos.shape)
    qs = _sc_permute(aq.reshape(B * (H // 2) * S, 128), pair_row.reshape(-1), pos.reshape(-1),
                     n_tiles * MOBA_ROWS, 256)
    parts = _moba_blk(qs, tile_group, n_used, ak, av)
    pg = _sc_gather(parts, pos.transpose(1, 0, 2).reshape(-1), 256)
    o = _moba_merge(own.reshape(n_q, PART_W), pg.reshape(MOBA_TOPK, n_q, PART_W), norm_g)
    return o.reshape(B, H, S, hd)


def _mix_kernel(oh_ref, oa_ref, x_ref, g1_ref, sc2_ref, sh2_ref, n2_ref, wo_ref, wr_ref, br_ref,
                x1_ref, h2_ref, gw_ref, idx_ref, cnt_ref, cnt_acc):
    cat = jnp.concatenate([oh_ref[...]] + [oa_ref[hd] for hd in range(ATTN_HEADS)], axis=1)
    mix = _dot(cat.astype(BF16), wo_ref[...])
    x1 = x_ref[...] + g1_ref[...] * mix
    x1_ref[...] = x1
    ms = jnp.mean(x1 * x1, axis=-1, keepdims=True)
    h2 = x1 * lax.rsqrt(ms + RMS_EPS) * n2_ref[...]
    h2 = h2 * (1.0 + sc2_ref[...]) + sh2_ref[...]
    h2_ref[...] = _pack_bf16_pairs(h2)
    E = N_EXPERTS
    tm = h2.shape[0]
    h_0 = h2.astype(BF16)
    r_1 = h2 - h_0.astype(F32)
    h_1 = r_1.astype(BF16)
    h_2 = (r_1 - h_1.astype(F32)).astype(BF16)
    wt = wr_ref[...]
    p_0 = _dot_nt(wt, h_0)
    p_1 = _dot_nt(wt[:2 * E], h_1)
    p_2 = _dot_nt(wt[:E], h_2)
    logits = (p_0[:E] + (p_0[E:2 * E] + p_1[:E]) + (p_0[2 * E:] + p_1[E:] + p_2)) + br_ref[...]
    ex = lax.broadcasted_iota(jnp.int32, logits.shape, 0)
    neg_inf = jnp.float32(-jnp.inf)
    vals, idxs = [], []
    for _ in range(TOP_K):
        m = jnp.max(logits, axis=0, keepdims=True)
        first = jnp.min(jnp.where(logits == m, ex, E), axis=0, keepdims=True)
        vals.append(m)
        idxs.append(first)
        logits = jnp.where(ex == first, neg_inf, logits)
    e = [jnp.exp(v - vals[0]) for v in vals]
    denom = e[0] + e[1] + e[2] + e[3]
    gate_rows = jnp.concatenate([ei / denom for ei in e] + [jnp.zeros((128 - TOP_K, tm), F32)], axis=0)
    gw_ref[...] = gate_rows.T[:, :TOP_K]

    @pl.when((pl.program_id(0) == 0) & (pl.program_id(1) == 0))
    def _():
        cnt_acc[...] = jnp.zeros_like(cnt_acc)

    earlier = (lax.broadcasted_iota(jnp.int32, (tm, tm), 0) < lax.broadcasted_iota(jnp.int32, (tm, tm), 1)).astype(BF16)
    onehots = [(ex == ix).astype(F32) for ix in idxs]
    member = onehots[0] + onehots[1] + onehots[2] + onehots[3]
    base = cnt_acc[...] + _dot(member.astype(BF16), earlier)
    ranks = [jnp.sum(oh * base, axis=0, keepdims=True).astype(jnp.int32) for oh in onehots]
    idx_ref[...] = jnp.concatenate(idxs + ranks, axis=0)
    total = cnt_acc[...] + jnp.sum(member, axis=1, keepdims=True)
    cnt_acc[...] = total
    cnt_ref[...] = total.astype(jnp.int32)


def _mix(oh, oa, x, gate1, scale2, shift2, norm2_g, w_out_bf16, w_router, b_router, b0):
    _, S, D = x.shape
    B = oh.shape[0]
    hw = oh.shape[-1]
    tm = 256
    row = lambda b, i: (b, i, 0)
    xrow = lambda b, i: (b0 + b, i, 0)
    vec = lambda b, i: (b, 0, 0)
    const = lambda b, i: (0, 0)
    return pl.pallas_call(
        _mix_kernel,
        out_shape=(jax.ShapeDtypeStruct((B, S, D), F32),
                   jax.ShapeDtypeStruct((B, S, D // 2), jnp.int32),
                   jax.ShapeDtypeStruct((B, S, TOP_K), F32),
                   jax.ShapeDtypeStruct((B, 2 * TOP_K, S), jnp.int32),
                   jax.ShapeDtypeStruct((N_EXPERTS, 1), jnp.int32)),
        grid=(B, S // tm),
        in_specs=[pl.BlockSpec((None, tm, hw), row),
                  pl.BlockSpec((None, ATTN_HEADS, tm, ATTN_HEAD_DIM), lambda b, i: (b, 0, i, 0)),
                  pl.BlockSpec((None, tm, D), xrow),
                  pl.BlockSpec((None, 1, D), vec),
                  pl.BlockSpec((None, 1, D), vec),
                  pl.BlockSpec((None, 1, D), vec),
                  pl.BlockSpec((1, D), const),
                  pl.BlockSpec((D, D), const),
                  pl.BlockSpec((3 * N_EXPERTS, D), const),
                  pl.BlockSpec((N_EXPERTS, 1), const)],
        out_specs=(pl.BlockSpec((None, tm, D), row),
                   pl.BlockSpec((None, tm, D // 2), row),
                   pl.BlockSpec((None, tm, TOP_K), row),
                   pl.BlockSpec((None, 2 * TOP_K, tm), lambda b, i: (b, 0, i)),
                   pl.BlockSpec((N_EXPERTS, 1), const)),
        scratch_shapes=[pltpu.VMEM((N_EXPERTS, 1), F32)],
        compiler_params=pltpu.CompilerParams(
            dimension_semantics=("arbitrary", "arbitrary"), vmem_limit_bytes=VMEM_LIMIT),
        name="mix",
    )(oh, oa, x, gate1, scale2, shift2, norm2_g, w_out_bf16, w_router, b_router)


def _moe_rows_kernel(be_ref, nu_ref, x_ref, wgu_ref, bgu_ref, wd_ref, bd_ref, y_ref, wgu16, wd16, *, d_ff):
    i = pl.program_id(0)

    @pl.when((i == 0) | (be_ref[i] != be_ref[jnp.maximum(i - 1, 0)]))
    def _():
        wgu16[...] = wgu_ref[...].astype(BF16)
        wd16[...] = wd_ref[...].astype(BF16)

    @pl.when(i < nu_ref[0])
    def _():
        gu = _dot(_unpack_bf16_pairs(x_ref[...]).astype(BF16), wgu16[...]) + bgu_ref[...]
        gate = jnp.minimum(gu[:, :d_ff], SWIGLU_LIMIT)
        up = jnp.clip(gu[:, d_ff:], -SWIGLU_LIMIT, SWIGLU_LIMIT)
        act = (up + 1.0) * gate * _sigmoid(SWIGLU_ALPHA * gate)
        y_ref[...] = _pack_bf16_pairs(_dot(act.astype(BF16), wd16[...]) + bd_ref[...])

    @pl.when(i >= nu_ref[0])
    def _():
        y_ref[...] = jnp.zeros_like(y_ref)


def _moe_rows(xs, blk_expert, n_used, wgu, bgu, wd, bd):
    D = 2 * xs.shape[1]
    bm = MOE_ROWS
    n_blk = xs.shape[0] // bm
    d_ff = wd.shape[1]
    wsel = lambda i, be, nu: (be[i], 0, 0)
    grid_spec = pltpu.PrefetchScalarGridSpec(
        num_scalar_prefetch=2,
        grid=(n_blk,),
        in_specs=[pl.BlockSpec((bm, D // 2), lambda i, be, nu: (i, 0)),
                  pl.BlockSpec((None, D, 2 * d_ff), wsel),
                  pl.BlockSpec((None, 1, 2 * d_ff), wsel),
                  pl.BlockSpec((None, d_ff, D), wsel),
                  pl.BlockSpec((None, 1, D), wsel)],
        out_specs=pl.BlockSpec((bm, D // 2), lambda i, be, nu: (i, 0)),
        scratch_shapes=[pltpu.VMEM((D, 2 * d_ff), BF16), pltpu.VMEM((d_ff, D), BF16)],
    )
    return pl.pallas_call(
        functools.partial(_moe_rows_kernel, d_ff=d_ff),
        out_shape=jax.ShapeDtypeStruct((n_blk * bm, D // 2), jnp.int32),
        grid_spec=grid_spec,
        compiler_params=pltpu.CompilerParams(
            dimension_semantics=("arbitrary",), vmem_limit_bytes=VMEM_LIMIT),
        name="moe_rows",
    )(blk_expert, n_used, xs, wgu, bgu.reshape(N_EXPERTS, 1, 2 * d_ff), wd, bd.reshape(N_EXPERTS, 1, D))


def _combine_rows_kernel(*refs):
    y_refs = refs[:TOP_K]
    gw_ref, x1_ref, g2_ref, fg_ref = refs[TOP_K:TOP_K + 4]
    o_ref = refs[-1]
    gw = gw_ref[...]
    y = gw[:, 0:1] * _unpack_bf16_pairs(y_refs[0][...])
    for kk in range(1, TOP_K):
        y = y + gw[:, kk:kk + 1] * _unpack_bf16_pairs(y_refs[kk][...])
    x2 = x1_ref[...] + g2_ref[...] * y
    ms = jnp.mean(x2 * x2, axis=-1, keepdims=True)
    o_ref[...] = x2 * lax.rsqrt(ms + RMS_EPS) * fg_ref[...]


def _combine_rows(yg, gates, x1, gate2, final_g, out_so_far, b0, n_batches):
    S, D = x1.shape
    tm = 256
    steps = S // tm
    slot_spec = lambda kk: pl.BlockSpec((tm, D // 2), lambda i: (kk * steps + i, 0))
    in_specs = [slot_spec(kk) for kk in range(TOP_K)] + [
        pl.BlockSpec((tm, TOP_K), lambda i: (i, 0)),
        pl.BlockSpec((tm, D), lambda i: (i, 0)),
        pl.BlockSpec((1, D), lambda i: (0, 0)),
        pl.BlockSpec((1, D), lambda i: (0, 0))]
    args = [yg] * TOP_K + [gates, x1, gate2, final_g]
    aliases = {}
    if out_so_far is not None:
        in_specs.append(pl.BlockSpec(memory_space=pl.ANY))
        aliases = {len(args): 0}
        args.append(out_so_far)
    return pl.pallas_call(
        _combine_rows_kernel,
        out_shape=jax.ShapeDtypeStruct((n_batches * S, D), F32),
        grid=(steps,),
        in_specs=in_specs,
        out_specs=pl.BlockSpec((tm, D), lambda i: (b0 * steps + i, 0)),
        input_output_aliases=aliases,
        compiler_params=pltpu.CompilerParams(
            dimension_semantics=("arbitrary",), vmem_limit_bytes=VMEM_LIMIT),
        name="combine_rows",
    )(*args)


def _split_bf16x3(w):
    def top(v):
        return lax.bitcast_convert_type(lax.bitcast_convert_type(v, jnp.int32) & jnp.int32(-65536), F32)
    w0 = top(w)
    w1 = top(w - w0)
    w2 = w - w0 - w1
    return jnp.concatenate([w0, w1, w2], axis=1).astype(BF16).T


def _rotary_tables(positions):
    half = ROT_DIM // 2
    inv_freq = jnp.exp(-math.log(ROPE_THETA) * jnp.arange(0, ROT_DIM, 2, dtype=F32) / ROT_DIM)
    d = jnp.arange(128, dtype=jnp.int32) % ATTN_HEAD_DIM
    freq = jnp.where(d < ROT_DIM, inv_freq[d % half], 0.0)
    sign = jnp.where(d < half, -1.0, 1.0)
    ang = positions.astype(F32)[:, :, None] * freq
    return jnp.cos(ang), jnp.sin(ang) * sign


def kernel(x, c, positions, w_ada, b_ada, norm1_g, w_in, hgrn_lb_logits, hgrn_norm_g, attn_norm_g,
           w_out, norm2_g, w_router, b_router, w_gate_up, b_gate_up, w_down, b_down, final_norm_g):
    B, S, D = x.shape
    assert w_in.shape[0] == 1, "single-layer block: the final norm is fused into the combine step"
    l = 0
    ctab, stab = _rotary_tables(positions)
    lower_bounds = jnp.cumsum(jax.nn.softmax(hgrn_lb_logits.astype(F32), axis=0), axis=0)
    mod = _ada(c, w_ada[l], b_ada[l])
    shift1, scale1, gate1, shift2, scale2, gate2 = jnp.split(mod[:, None, :], N_MOD, axis=-1)
    n_blk = (S * TOP_K) // MOE_ROWS + N_EXPERTS
    experts = jnp.arange(N_EXPERTS, dtype=jnp.int32)[:, None]
    tok = jnp.broadcast_to(jnp.arange(S, dtype=jnp.int32)[None, :], (TOP_K, S)).reshape(-1)
    out = None
    for b in range(B):
        one = slice(b, b + 1)
        q, k, lf, v, gt, aq, ak, av, km = _proj(
            x, scale1[one], shift1[one], norm1_g[l][None], w_in[l].astype(BF16), lower_bounds[l][None],
            ctab, stab, b)
        o_a = _moba(aq, km, ak, av, attn_norm_g[l][None])
        o_h = _hgrn(q, k, lf, v, gt, hgrn_norm_g[l][None])
        x1, h2, gates, idx8, counts = _mix(
            o_h, o_a, x, gate1[one], scale2[one], shift2[one], norm2_g[l][None], w_out[l].astype(BF16),
            _split_bf16x3(w_router[l]), b_router[l][:, None], b)
        pad_start, blk_expert, n_used = _tile_layout(counts.reshape(-1), MOE_ROWS, n_blk)
        chosen = idx8[0, 0:TOP_K, :]
        pos = jnp.sum(jnp.where(chosen[:, None, :] == experts, pad_start[:, None], 0), axis=1) + idx8[0, TOP_K:, :]
        xs = _sc_permute(h2.reshape(S, D // 2), tok, pos.reshape(-1), n_blk * MOE_ROWS, 64)
        y_sorted = _moe_rows(xs, blk_expert, n_used, w_gate_up[l], b_gate_up[l], w_down[l], b_down[l])
        yg = _sc_gather(y_sorted, pos.reshape(-1), 64)
        out = _combine_rows(yg, gates.reshape(S, TOP_K), x1.reshape(S, D), gate2[b], final_norm_g[None], out, b, B)
    return out.reshape(B, S, D)
```

```python
import functools
import math

import jax
import jax.numpy as jnp
from jax import lax
from jax.experimental import pallas as pl
from jax.experimental.pallas import tpu as pltpu
from jax.experimental.pallas import tpu_sc as plsc

F32 = jnp.float32
BF16 = jnp.bfloat16
HIGHEST = lax.Precision.HIGHEST

HGRN_DK = 128
HGRN_CHUNK = 64
ATTN_HEADS = 4
ATTN_HEAD_DIM = 64
ROT_DIM = ATTN_HEAD_DIM // 4
ROPE_THETA = 500000.0
MOBA_BLOCK = 256
MOBA_TOPK = 3
N_EXPERTS = 32
TOP_K = 4
SWIGLU_ALPHA = 1.702
SWIGLU_LIMIT = 7.0
N_MOD = 6
RMS_EPS = 1e-6

HGRN_SUB = 16
EXP_CLAMP = 80.0
MOE_ROWS = 512
MOBA_ROWS = 256
MOBA_TILES_PER_STEP = 8
PART_W = 128
V7X_VMEM_BYTES = 64 * 1024 * 1024
VMEM_LIMIT = V7X_VMEM_BYTES * 7 // 8
SC_CORES = 2
SC_SUBCORES = 16


def _sigmoid(x):
    return 1.0 / (1.0 + jnp.exp(-x))


def _dot(a, b, **kw):
    return jnp.dot(a, b, preferred_element_type=F32, **kw)


def _dot_nt(a, b, **kw):
    return lax.dot_general(a, b, (((1,), (1,)), ((), ())), preferred_element_type=F32, **kw)


def _pack_bf16_pairs(x):
    w = x.shape[1] // 2
    bits = lax.bitcast_convert_type(x.astype(BF16).astype(F32), jnp.int32)
    return bits[:, w:] | lax.shift_right_logical(bits[:, :w], 16)


def _unpack_bf16_pairs(p):
    lo = lax.bitcast_convert_type(lax.shift_left(p, 16), F32)
    hi = lax.bitcast_convert_type(p & jnp.int32(-65536), F32)
    return jnp.concatenate([lo, hi], axis=1)


def _ada_kernel(c_ref, w_ref, b_ref, o_ref):
    c = c_ref[...]
    o_ref[...] = _dot(c * _sigmoid(c), w_ref[...], precision=HIGHEST) + b_ref[...]


def _ada(c, w_ada, b_ada):
    B, D = c.shape
    N = w_ada.shape[1]
    tn = N // 4
    c8 = jnp.zeros((8, D), F32).at[:B].set(c)
    out = pl.pallas_call(
        _ada_kernel,
        out_shape=jax.ShapeDtypeStruct((8, N), F32),
        grid=(N // tn,),
        in_specs=[pl.BlockSpec((8, D), lambda j: (0, 0)),
                  pl.BlockSpec((D, tn), lambda j: (0, j)),
                  pl.BlockSpec((1, tn), lambda j: (0, j))],
        out_specs=pl.BlockSpec((8, tn), lambda j: (0, j)),
        compiler_params=pltpu.CompilerParams(vmem_limit_bytes=VMEM_LIMIT),
        name="ada",
    )(c8, w_ada, b_ada.reshape(1, N))
    return out[:B]


def _proj_kernel(x_ref, sc_ref, sh_ref, g_ref, w_ref, lb_ref, ct_ref, st_ref,
                 q_ref, k_ref, lf_ref, v_ref, gt_ref, aq_ref, ak_ref, av_ref, km_ref,
                 *, hw, aw):
    x = x_ref[...]
    ms = jnp.mean(x * x, axis=-1, keepdims=True)
    h = x * lax.rsqrt(ms + RMS_EPS) * g_ref[...]
    h = h * (1.0 + sc_ref[...]) + sh_ref[...]
    proj = _dot(h.astype(BF16), w_ref[...])

    hq = proj[:, 0:hw]
    hf = proj[:, hw:2 * hw]
    hg = proj[:, 3 * hw:4 * hw]
    q_ref[...] = hq * _sigmoid(hq) * (HGRN_DK ** -0.5)
    lb = lb_ref[...]
    f = lb + (1.0 - lb) * _sigmoid(hf)
    k_ref[...] = 1.0 - f
    lf_ref[...] = jnp.log(f)
    v_ref[...] = proj[:, 2 * hw:3 * hw].astype(BF16)
    gt_ref[...] = hg * _sigmoid(hg)

    ct = jnp.concatenate([ct_ref[...]] * (aw // 128), axis=1)
    st = jnp.concatenate([st_ref[...]] * (aw // 128), axis=1)
    lane = lax.broadcasted_iota(jnp.int32, ct.shape, 1) % ATTN_HEAD_DIM
    first_half = lane < (ROT_DIM // 2)

    def rot(t):
        partner = jnp.where(first_half, pltpu.roll(t, aw - ROT_DIM // 2, 1), pltpu.roll(t, ROT_DIM // 2, 1))
        return t * ct + partner * st

    base = 4 * hw
    aq = rot(proj[:, base:base + aw])
    ak = rot(proj[:, base + aw:base + 2 * aw])
    av = proj[:, base + 2 * aw:base + 3 * aw]
    km_ref[...] = jnp.mean(ak, axis=0, keepdims=True)
    lane128 = lax.broadcasted_iota(jnp.int32, (x.shape[0], 128), 1)
    for pair in range(ATTN_HEADS // 2):
        aq_ref[pair] = aq[:, pair * 128:(pair + 1) * 128]
    for hd in range(ATTN_HEADS):
        pair, half = divmod(hd, 2)
        in_head = (lane128 // ATTN_HEAD_DIM) == half
        ak_ref[hd] = jnp.where(in_head, ak[:, pair * 128:(pair + 1) * 128], 0.0).astype(BF16)
        av_ref[hd] = av[:, hd * ATTN_HEAD_DIM:(hd + 1) * ATTN_HEAD_DIM].astype(BF16)


def _proj(x, scale1, shift1, norm_g, w_in_bf16, lb, ctab, stab, b0):
    _, S, D = x.shape
    B = scale1.shape[0]
    hw = lb.shape[-1]
    aw = ATTN_HEADS * ATTN_HEAD_DIM
    tm = MOBA_BLOCK
    nb = S // MOBA_BLOCK
    n_proj = w_in_bf16.shape[1]
    row = lambda b, i: (b, i, 0)
    xrow = lambda b, i: (b0 + b, i, 0)
    vec = lambda b, i: (b, 0, 0)
    head = lambda b, i: (b, 0, i, 0)
    out_shapes = (
        jax.ShapeDtypeStruct((B, S, hw), F32),
        jax.ShapeDtypeStruct((B, S, hw), F32),
        jax.ShapeDtypeStruct((B, S, hw), F32),
        jax.ShapeDtypeStruct((B, S, hw), BF16),
        jax.ShapeDtypeStruct((B, S, hw), F32),
        jax.ShapeDtypeStruct((B, ATTN_HEADS // 2, S, 128), F32),
        jax.ShapeDtypeStruct((B, ATTN_HEADS, S, 128), BF16),
        jax.ShapeDtypeStruct((B, ATTN_HEADS, S, ATTN_HEAD_DIM), BF16),
        jax.ShapeDtypeStruct((B, nb, 1, aw), F32),
    )
    hspec = pl.BlockSpec((None, tm, hw), row)
    aspec = pl.BlockSpec((None, ATTN_HEADS, tm, ATTN_HEAD_DIM), head)
    return pl.pallas_call(
        functools.partial(_proj_kernel, hw=hw, aw=aw),
        out_shape=out_shapes,
        grid=(B, S // tm),
        in_specs=[pl.BlockSpec((None, tm, D), xrow),
                  pl.BlockSpec((None, 1, D), vec),
                  pl.BlockSpec((None, 1, D), vec),
                  pl.BlockSpec((1, D), lambda b, i: (0, 0)),
                  pl.BlockSpec((D, n_proj), lambda b, i: (0, 0)),
                  pl.BlockSpec((1, hw), lambda b, i: (0, 0)),
                  pl.BlockSpec((None, tm, 128), xrow),
                  pl.BlockSpec((None, tm, 128), xrow)],
        out_specs=(hspec, hspec, hspec, hspec, hspec,
                   pl.BlockSpec((None, ATTN_HEADS // 2, tm, 128), head),
                   pl.BlockSpec((None, ATTN_HEADS, tm, 128), head), aspec,
                   pl.BlockSpec((None, None, 1, aw), lambda b, i: (b, i, 0, 0))),
        compiler_params=pltpu.CompilerParams(
            dimension_semantics=("arbitrary", "arbitrary"), vmem_limit_bytes=VMEM_LIMIT),
        name="proj",
    )(x, scale1, shift1, norm_g, w_in_bf16, lb, ctab, stab)


def _hgrn_kernel(q_ref, k_ref, lf_ref, v_ref, gt_ref, gn_ref, o_ref, st_ref, *, n_heads, n_chunks):
    @pl.when(pl.program_id(1) == 0)
    def _():
        st_ref[...] = jnp.zeros_like(st_ref)

    C = HGRN_CHUNK
    r = lax.broadcasted_iota(jnp.int32, (C, C), 0)
    c = lax.broadcasted_iota(jnp.int32, (C, C), 1)
    tril = c <= r
    ltri = tril.astype(F32)
    gn = gn_ref[...]

    def chunk(ci, carry):
        r0 = pl.multiple_of(ci * C, C)
        rows = pl.ds(r0, C)
        b_all = _dot(ltri, lf_ref[rows, :], precision=HIGHEST)
        heads = range(n_heads)
        sls = [slice(hd * HGRN_DK, (hd + 1) * HGRN_DK) for hd in heads]
        bs = [b_all[:, sl] for sl in sls]
        b_lasts = [b[C - 1:C, :] for b in bs]
        qs = [q_ref[rows, sl] for sl in sls]
        ks = [k_ref[rows, sl] for sl in sls]
        vs = [v_ref[rows, sl] for sl in sls]
        states = [st_ref[hd] for hd in heads]
        o_inter = [_dot_nt((qs[hd] * jnp.exp(bs[hd])).astype(BF16), states[hd].astype(BF16)) for hd in heads]
        scores = []
        for hd in heads:
            blocks = []
            for g0 in range(0, C, HGRN_SUB):
                g1 = g0 + HGRN_SUB
                rho = 0.5 * (bs[hd][g0:g0 + 1, :] + bs[hd][g1 - 1:g1, :])
                qa = qs[hd][g0:g1, :] * jnp.exp(jnp.minimum(bs[hd][g0:g1, :] - rho, EXP_CLAMP))
                kb = ks[hd] * jnp.exp(jnp.minimum(rho - bs[hd], EXP_CLAMP))
                blocks.append(_dot_nt(qa.astype(BF16), kb.astype(BF16)))
            scores.append(jnp.where(tril, jnp.concatenate(blocks, axis=0), 0.0).astype(BF16))
        outs = [o_inter[hd] + _dot(scores[hd], vs[hd]) for hd in heads]
        kds = [(ks[hd] * jnp.exp(b_lasts[hd] - bs[hd])).astype(BF16) for hd in heads]
        upds = [_dot(vs[hd].astype(F32).T.astype(BF16), kds[hd]) for hd in heads]
        for hd in heads:
            st_ref[hd] = states[hd] * jnp.exp(b_lasts[hd]) + upds[hd]
            o = outs[hd]
            ms = jnp.mean(o * o, axis=-1, keepdims=True)
            o_ref[rows, sls[hd]] = (o * lax.rsqrt(ms + RMS_EPS) * gn * gt_ref[rows, sls[hd]]).astype(BF16)
        return carry

    lax.fori_loop(0, n_chunks, chunk, 0, unroll=True)


def _hgrn(q, k, lf, v, gt, norm_g):
    B, S, hw = q.shape
    n_heads = hw // HGRN_DK
    tc = 512
    spec = pl.BlockSpec((None, tc, hw), lambda b, i: (b, i, 0))
    return pl.pallas_call(
        functools.partial(_hgrn_kernel, n_heads=n_heads, n_chunks=tc // HGRN_CHUNK),
        out_shape=jax.ShapeDtypeStruct((B, S, hw), BF16),
        grid=(B, S // tc),
        in_specs=[spec, spec, spec, spec, spec, pl.BlockSpec((1, HGRN_DK), lambda b, i: (0, 0))],
        out_specs=spec,
        scratch_shapes=[pltpu.VMEM((n_heads, HGRN_DK, HGRN_DK), F32)],
        compiler_params=pltpu.CompilerParams(
            dimension_semantics=("arbitrary", "arbitrary"), vmem_limit_bytes=VMEM_LIMIT),
        name="hgrn",
    )(q, k, lf, v, gt, norm_g)


def _sc_move_rows(table, src, dst, n_out, chunk):
    M = src.shape[0]
    D = table.shape[1]
    n_workers = SC_CORES * SC_SUBCORES
    per_worker = M // n_workers
    n_chunks = per_worker // chunk
    assert per_worker * n_workers == M and n_chunks * chunk == per_worker and n_chunks % 2 == 0 and chunk % 8 == 0
    mesh = plsc.VectorSubcoreMesh(core_axis_name="c", subcore_axis_name="s")
    idx_t = pltpu.VMEM((chunk,), jnp.int32)
    row_t = pltpu.VMEM((chunk, D), table.dtype)
    sem_t = pltpu.SemaphoreType.DMA

    def body(table_hbm, src_hbm, dst_hbm, out_hbm, src_v, dst_v, rows_v, g_sem, s_sem):
        wid = lax.axis_index("s") * SC_CORES + lax.axis_index("c")
        base = wid * per_worker

        def offset(j):
            return pl.multiple_of(base + j * chunk, 8)

        def gather(b):
            return pltpu.make_async_copy(table_hbm.at[src_v[b]], rows_v[b], g_sem[b])

        def start_gather(j, b):
            pltpu.sync_copy(src_hbm.at[pl.ds(offset(j), chunk)], src_v[b])
            gather(b).start()

        def write_out(j, b):
            if dst_hbm is None:
                pltpu.sync_copy(rows_v[b], out_hbm.at[pl.ds(offset(j), chunk)])
            else:
                pltpu.sync_copy(dst_hbm.at[pl.ds(offset(j), chunk)], dst_v[b])
                pltpu.async_copy(rows_v[b], out_hbm.at[dst_v[b]], s_sem[b]).wait()

        start_gather(0, 0)

        @pl.loop(0, n_chunks, step=2)
        def _(j):
            for b in (0, 1):
                @pl.when(j + b + 1 < n_chunks)
                def _():
                    start_gather(j + b + 1, 1 - b)
                gather(b).wait()
                write_out(j + b, b)

    if dst is None:
        @functools.partial(pl.kernel, mesh=mesh, out_type=jax.ShapeDtypeStruct((n_out, D), table.dtype),
                           scratch_types=[idx_t, idx_t, row_t, row_t, sem_t, sem_t])
        def gather_kernel(table_hbm, src_hbm, out_hbm, s0, s1, r0, r1, g0, g1):
            body(table_hbm, src_hbm, None, out_hbm, (s0, s1), None, (r0, r1), (g0, g1), None)
        return gather_kernel(table, src)

    @functools.partial(pl.kernel, mesh=mesh, out_type=jax.ShapeDtypeStruct((n_out, D), table.dtype),
                       scratch_types=[idx_t, idx_t, idx_t, idx_t, row_t, row_t, sem_t, sem_t, sem_t, sem_t])
    def permute_kernel(table_hbm, src_hbm, dst_hbm, out_hbm, s0, s1, d0, d1, r0, r1, g0, g1, w0, w1):
        body(table_hbm, src_hbm, dst_hbm, out_hbm, (s0, s1), (d0, d1), (r0, r1), (g0, g1), (w0, w1))
    return permute_kernel(table, src, dst)


def _sc_gather(table, idx, chunk):
    return _sc_move_rows(table, idx, None, idx.shape[0], chunk)


def _sc_permute(table, src, dst, n_out, chunk):
    return _sc_move_rows(table, src, dst, n_out, chunk)


def _tile_layout(counts, bm, n_tiles):
    n_groups = counts.shape[0]
    padded = (counts + bm - 1) // bm * bm
    pad_end = jnp.cumsum(padded)
    tile_start = jnp.arange(n_tiles, dtype=jnp.int32) * bm
    tile_group = jnp.minimum(
        jnp.sum((pad_end[None, :] <= tile_start[:, None]).astype(jnp.int32), axis=1), n_groups - 1)
    n_used = (pad_end[-1] // bm).astype(jnp.int32).reshape(1)
    return pad_end - padded, tile_group.astype(jnp.int32), n_used


def _null_partial(rows):
    lane = lax.broadcasted_iota(jnp.int32, (rows, PART_W), 1)
    return jnp.where(lane < ATTN_HEAD_DIM, 0.0, -jnp.inf).astype(F32)


def _moba_sel_kernel(q_ref, km_ref, k_ref, v_ref, idx_ref, cnt_ref, own_ref, cnt_acc, *, n_blocks):
    j = pl.program_id(1)
    T = MOBA_BLOCK
    heads = range(ATTN_HEADS)
    qs = [q_ref[hd // 2] for hd in heads]
    gates = [_dot_nt(km_ref[hd], qs[hd], precision=HIGHEST) for hd in heads]
    blk = lax.broadcasted_iota(jnp.int32, gates[0].shape, 0)
    neg_inf = jnp.float32(-jnp.inf)
    gates = [jnp.where(blk < j, g, neg_inf) for g in gates]
    picks = [[] for _ in heads]
    for _ in range(MOBA_TOPK):
        ms = [jnp.max(g, axis=0, keepdims=True) for g in gates]
        firsts = [jnp.min(jnp.where(g == m, blk, n_blocks), axis=0, keepdims=True) for g, m in zip(gates, ms)]
        for hd in heads:
            picks[hd].append(jnp.where(ms[hd] > neg_inf, firsts[hd], -1))
        gates = [jnp.where(blk == f, neg_inf, g) for g, f in zip(gates, firsts)]

    @pl.when(j == 0)
    def _():
        cnt_acc[...] = jnp.zeros_like(cnt_acc)

    earlier = (lax.broadcasted_iota(jnp.int32, (T, T), 0) < lax.broadcasted_iota(jnp.int32, (T, T), 1)).astype(BF16)
    for hd in heads:
        onehots = [(blk == p).astype(F32) for p in picks[hd]]
        member = onehots[0] + onehots[1] + onehots[2]
        base = cnt_acc[hd] + _dot(member.astype(BF16), earlier)
        ranks = [jnp.sum(oh * base, axis=0, keepdims=True).astype(jnp.int32) for oh in onehots]
        idx_ref[hd] = jnp.concatenate(picks[hd] + ranks + [jnp.zeros((2, T), jnp.int32)], axis=0)
        total = cnt_acc[hd] + jnp.sum(member, axis=1, keepdims=True)
        cnt_acc[hd] = total
        cnt_ref[hd] = total.astype(jnp.int32)
    causal = lax.broadcasted_iota(jnp.int32, (T, T), 1) <= lax.broadcasted_iota(jnp.int32, (T, T), 0)
    scale = ATTN_HEAD_DIM ** -0.5
    ss = [jnp.where(causal, _dot_nt((qs[hd] * scale).astype(BF16), k_ref[hd]), neg_inf) for hd in heads]
    mx = [jnp.max(s, axis=1, keepdims=True) for s in ss]
    ps = [jnp.exp(s - m) for s, m in zip(ss, mx)]
    ls = [jnp.sum(p, axis=1, keepdims=True) for p in ps]
    accs = [_dot(ps[hd].astype(BF16), v_ref[hd]) for hd in heads]
    for hd in heads:
        lse = jnp.broadcast_to(mx[hd] + jnp.log(ls[hd]), (T, PART_W - ATTN_HEAD_DIM))
        own_ref[hd] = jnp.concatenate([accs[hd] / ls[hd], lse], axis=1)


def _moba_sel(aq, kmean, ak, av):
    B, H, S, hd = av.shape
    nb = S // MOBA_BLOCK
    T = MOBA_BLOCK
    blk = lambda b, j: (b, 0, j, 0)
    return pl.pallas_call(
        functools.partial(_moba_sel_kernel, n_blocks=nb),
        out_shape=(jax.ShapeDtypeStruct((B, H, 8, S), jnp.int32),
                   jax.ShapeDtypeStruct((B, H, nb, 1), jnp.int32),
                   jax.ShapeDtypeStruct((B, H, S, PART_W), F32)),
        grid=(B, nb),
        in_specs=[pl.BlockSpec((None, H // 2, T, 128), blk),
                  pl.BlockSpec((None, H, nb, 128), lambda b, j: (b, 0, 0, 0)),
                  pl.BlockSpec((None, H, T, 128), blk),
                  pl.BlockSpec((None, H, T, hd), blk)],
        out_specs=(pl.BlockSpec((None, H, 8, T), lambda b, j: (b, 0, 0, j)),
                   pl.BlockSpec((None, H, nb, 1), lambda b, j: (b, 0, 0, 0)),
                   pl.BlockSpec((None, H, T, PART_W), blk)),
        scratch_shapes=[pltpu.VMEM((H, nb, 1), F32)],
        compiler_params=pltpu.CompilerParams(
            dimension_semantics=("arbitrary", "arbitrary"), vmem_limit_bytes=VMEM_LIMIT),
        name="moba_sel",
    )(aq, kmean, ak, av)


def _moba_blk_kernel(tg_ref, nu_ref, q_ref, *refs):
    n = MOBA_TILES_PER_STEP
    k_refs, v_refs, o_ref = refs[:n], refs[n:2 * n], refs[2 * n]
    R = MOBA_ROWS
    t0 = pl.program_id(0) * n

    @pl.when(t0 < nu_ref[0])
    def _():
        scale = ATTN_HEAD_DIM ** -0.5
        ss = [_dot_nt((q_ref[j * R:(j + 1) * R, :] * scale).astype(BF16), k_refs[j][...]) for j in range(n)]
        ms = [jnp.max(s, axis=1, keepdims=True) for s in ss]
        ps = [jnp.exp(s - m) for s, m in zip(ss, ms)]
        ls = [jnp.sum(p, axis=1, keepdims=True) for p in ps]
        accs = [_dot(p.astype(BF16), v_refs[j][...]) for j, p in enumerate(ps)]
        null = _null_partial(R)
        for j in range(n):
            lse = jnp.broadcast_to(ms[j] + jnp.log(ls[j]), (R, PART_W - ATTN_HEAD_DIM))
            row = jnp.concatenate([accs[j] / ls[j], lse], axis=1)
            o_ref[j * R:(j + 1) * R, :] = jnp.where(t0 + j < nu_ref[0], row, null)

    @pl.when(t0 >= nu_ref[0])
    def _():
        o_ref[...] = _null_partial(n * R)


def _moba_blk(qs, tile_group, n_used, ak, av):
    B, H, S, hd = av.shape
    nb = S // MOBA_BLOCK
    R = MOBA_ROWS
    n = MOBA_TILES_PER_STEP
    n_tiles = qs.shape[0] // R
    assert n_tiles % n == 0
    kv = lambda j: (lambda i, tg, nu: (tg[i * n + j] // nb, tg[i * n + j] % nb, 0, 0))
    grid_spec = pltpu.PrefetchScalarGridSpec(
        num_scalar_prefetch=2,
        grid=(n_tiles // n,),
        in_specs=[pl.BlockSpec((n * R, 128), lambda i, tg, nu: (i, 0))]
        + [pl.BlockSpec((None, None, MOBA_BLOCK, 128), kv(j)) for j in range(n)]
        + [pl.BlockSpec((None, None, MOBA_BLOCK, hd), kv(j)) for j in range(n)],
        out_specs=pl.BlockSpec((n * R, PART_W), lambda i, tg, nu: (i, 0)),
    )
    k4 = ak.reshape(B * H, nb, MOBA_BLOCK, 128)
    v4 = av.reshape(B * H, nb, MOBA_BLOCK, hd)
    return pl.pallas_call(
        _moba_blk_kernel,
        out_shape=jax.ShapeDtypeStruct((n_tiles * R, PART_W), F32),
        grid_spec=grid_spec,
        compiler_params=pltpu.CompilerParams(
            dimension_semantics=("arbitrary",), vmem_limit_bytes=VMEM_LIMIT),
        name="moba_blk",
    )(tile_group, n_used, qs, *([k4] * n), *([v4] * n))


def _moba_merge_kernel(own_ref, pg_ref, g_ref, o_ref):
    hd = ATTN_HEAD_DIM
    rows = [own_ref[...]] + [pg_ref[s] for s in range(MOBA_TOPK)]
    lses = [pltpu.roll(r, hd, 1) for r in rows]
    top = lses[0]
    for z in lses[1:]:
        top = jnp.maximum(top, z)
    num = jnp.zeros_like(top)
    den = jnp.zeros_like(top)
    for r, z in zip(rows, lses):
        w = jnp.exp(z - top)
        num = num + w * r
        den = den + w
    o = (num / den)[:, :hd]
    ms = jnp.mean(o * o, axis=-1, keepdims=True)
    o_ref[...] = o * lax.rsqrt(ms + RMS_EPS) * g_ref[...]


def _moba_merge(own, pg, norm_g):
    n = own.shape[0]
    T = 512
    row = lambda i: (i, 0)
    return pl.pallas_call(
        _moba_merge_kernel,
        out_shape=jax.ShapeDtypeStruct((n, ATTN_HEAD_DIM), F32),
        grid=(n // T,),
        in_specs=[pl.BlockSpec((T, PART_W), row),
                  pl.BlockSpec((MOBA_TOPK, T, PART_W), lambda i: (0, i, 0)),
                  pl.BlockSpec((1, ATTN_HEAD_DIM), lambda i: (0, 0))],
        out_specs=pl.BlockSpec((T, ATTN_HEAD_DIM), row),
        compiler_params=pltpu.CompilerParams(
            dimension_semantics=("arbitrary",), vmem_limit_bytes=VMEM_LIMIT),
        name="moba_merge",
    )(own, pg, norm_g)


def _moba(aq, km, ak, av, norm_g):
    B, H, S, hd = av.shape
    nb = S // MOBA_BLOCK
    n_q = B * H * S
    kmp = km.reshape(B, nb, H // 2, 128)
    half = jnp.arange(128, dtype=jnp.int32) // hd
    kmean = jnp.stack([jnp.where(half == h % 2, kmp[:, :, h // 2, :], 0.0) for h in range(H)], axis=1)
    idx8, counts, own = _moba_sel(aq, kmean, ak, av)
    sel = idx8[:, :, 0:MOBA_TOPK, :].reshape(B * H, MOBA_TOPK, S)
    rank = idx8[:, :, MOBA_TOPK:2 * MOBA_TOPK, :].reshape(B * H, MOBA_TOPK, S)
    n_groups = B * H * nb
    n_tiles = (n_q * MOBA_TOPK) // MOBA_ROWS + n_groups
    pad_start, tile_group, n_used = _tile_layout(counts.reshape(-1), MOBA_ROWS, n_tiles)
    blocks = jnp.arange(nb, dtype=jnp.int32)[:, None]
    start = jnp.sum(jnp.where(sel[:, :, None, :] == blocks, pad_start.reshape(B * H, 1, nb, 1), 0), axis=2)
    a_ids = jnp.arange(n_q * MOBA_TOPK, dtype=jnp.int32).reshape(B * H, MOBA_TOPK, S)
    assert n_tiles * MOBA_ROWS >= n_q * MOBA_TOPK + MOBA_ROWS
    pos = jnp.where(sel >= 0, start + rank, n_used[0] * MOBA_ROWS + a_ids % MOBA_ROWS)
    bh = jnp.arange(B * H, dtype=jnp.int32)[:, None, None]
    t = jnp.arange(S, dtype=jnp.int32)[None, None, :]
    pair_row = jnp.broadcast_to((bh // H * (H // 2) + bh % H // 2) * S + t, pos.shape)
    qs = _sc_permute(aq.reshape(B * (H // 2) * S, 128), pair_row.reshape(-1), pos.reshape(-1),
                     n_tiles * MOBA_ROWS, 256)
    parts = _moba_blk(qs, tile_group, n_used, ak, av)
    pg = _sc_gather(parts, pos.transpose(1, 0, 2).reshape(-1), 256)
    o = _moba_merge(own.reshape(n_q, PART_W), pg.reshape(MOBA_TOPK, n_q, PART_W), norm_g)
    return o.reshape(B, H, S, hd)


def _mix_kernel(oh_ref, oa_ref, x_ref, g1_ref, sc2_ref, sh2_ref, n2_ref, wo_ref, wr_ref, br_ref,
                x1_ref, h2_ref, gw_ref, idx_ref, cnt_ref, cnt_acc):
    cat = jnp.concatenate([oh_ref[...]] + [oa_ref[hd] for hd in range(ATTN_HEADS)], axis=1)
    mix = _dot(cat.astype(BF16), wo_ref[...])
    x1 = x_ref[...] + g1_ref[...] * mix
    x1_ref[...] = x1
    ms = jnp.mean(x1 * x1, axis=-1, keepdims=True)
    h2 = x1 * lax.rsqrt(ms + RMS_EPS) * n2_ref[...]
    h2 = h2 * (1.0 + sc2_ref[...]) + sh2_ref[...]
    h2_ref[...] = _pack_bf16_pairs(h2)
    E = N_EXPERTS
    tm = h2.shape[0]
    h_0 = h2.astype(BF16)
    r_1 = h2 - h_0.astype(F32)
    h_1 = r_1.astype(BF16)
    h_2 = (r_1 - h_1.astype(F32)).astype(BF16)
    wt = wr_ref[...]
    p_0 = _dot_nt(wt, h_0)
    p_1 = _dot_nt(wt[:2 * E], h_1)
    p_2 = _dot_nt(wt[:E], h_2)
    logits = (p_0[:E] + (p_0[E:2 * E] + p_1[:E]) + (p_0[2 * E:] + p_1[E:] + p_2)) + br_ref[...]
    ex = lax.broadcasted_iota(jnp.int32, logits.shape, 0)
    neg_inf = jnp.float32(-jnp.inf)
    vals, idxs = [], []
    for _ in range(TOP_K):
        m = jnp.max(logits, axis=0, keepdims=True)
        first = jnp.min(jnp.where(logits == m, ex, E), axis=0, keepdims=True)
        vals.append(m)
        idxs.append(first)
        logits = jnp.where(ex == first, neg_inf, logits)
    e = [jnp.exp(v - vals[0]) for v in vals]
    denom = e[0] + e[1] + e[2] + e[3]
    gate_rows = jnp.concatenate([ei / denom for ei in e] + [jnp.zeros((128 - TOP_K, tm), F32)], axis=0)
    gw_ref[...] = gate_rows.T[:, :TOP_K]

    @pl.when((pl.program_id(0) == 0) & (pl.program_id(1) == 0))
    def _():
        cnt_acc[...] = jnp.zeros_like(cnt_acc)

    earlier = (lax.broadcasted_iota(jnp.int32, (tm, tm), 0) < lax.broadcasted_iota(jnp.int32, (tm, tm), 1)).astype(BF16)
    onehots = [(ex == ix).astype(F32) for ix in idxs]
    member = onehots[0] + onehots[1] + onehots[2] + onehots[3]
    base = cnt_acc[...] + _dot(member.astype(BF16), earlier)
    ranks = [jnp.sum(oh * base, axis=0, keepdims=True).astype(jnp.int32) for oh in onehots]
    idx_ref[...] = jnp.concatenate(idxs + ranks, axis=0)
    total = cnt_acc[...] + jnp.sum(member, axis=1, keepdims=True)
    cnt_acc[...] = total
    cnt_ref[...] = total.astype(jnp.int32)


def _mix(oh, oa, x, gate1, scale2, shift2, norm2_g, w_out_bf16, w_router, b_router, b0):
    _, S, D = x.shape
    B = oh.shape[0]
    hw = oh.shape[-1]
    tm = 256
    row = lambda b, i: (b, i, 0)
    xrow = lambda b, i: (b0 + b, i, 0)
    vec = lambda b, i: (b, 0, 0)
    const = lambda b, i: (0, 0)
    return pl.pallas_call(
        _mix_kernel,
        out_shape=(jax.ShapeDtypeStruct((B, S, D), F32),
                   jax.ShapeDtypeStruct((B, S, D // 2), jnp.int32),
                   jax.ShapeDtypeStruct((B, S, TOP_K), F32),
                   jax.ShapeDtypeStruct((B, 2 * TOP_K, S), jnp.int32),
                   jax.ShapeDtypeStruct((N_EXPERTS, 1), jnp.int32)),
        grid=(B, S // tm),
        in_specs=[pl.BlockSpec((None, tm, hw), row),
                  pl.BlockSpec((None, ATTN_HEADS, tm, ATTN_HEAD_DIM), lambda b, i: (b, 0, i, 0)),
                  pl.BlockSpec((None, tm, D), xrow),
                  pl.BlockSpec((None, 1, D), vec),
                  pl.BlockSpec((None, 1, D), vec),
                  pl.BlockSpec((None, 1, D), vec),
                  pl.BlockSpec((1, D), const),
                  pl.BlockSpec((D, D), const),
                  pl.BlockSpec((3 * N_EXPERTS, D), const),
                  pl.BlockSpec((N_EXPERTS, 1), const)],
        out_specs=(pl.BlockSpec((None, tm, D), row),
                   pl.BlockSpec((None, tm, D // 2), row),
                   pl.BlockSpec((None, tm, TOP_K), row),
                   pl.BlockSpec((None, 2 * TOP_K, tm), lambda b, i: (b, 0, i)),
                   pl.BlockSpec((N_EXPERTS, 1), const)),
        scratch_shapes=[pltpu.VMEM((N_EXPERTS, 1), F32)],
        compiler_params=pltpu.CompilerParams(
            dimension_semantics=("arbitrary", "arbitrary"), vmem_limit_bytes=VMEM_LIMIT),
        name="mix",
    )(oh, oa, x, gate1, scale2, shift2, norm2_g, w_out_bf16, w_router, b_router)


def _moe_rows_kernel(be_ref, nu_ref, x_ref, wgu_ref, bgu_ref, wd_ref, bd_ref, y_ref, wgu16, wd16, *, d_ff):
    i = pl.program_id(0)

    @pl.when((i == 0) | (be_ref[i] != be_ref[jnp.maximum(i - 1, 0)]))
    def _():
        wgu16[...] = wgu_ref[...].astype(BF16)
        wd16[...] = wd_ref[...].astype(BF16)

    @pl.when(i < nu_ref[0])
    def _():
        gu = _dot(_unpack_bf16_pairs(x_ref[...]).astype(BF16), wgu16[...]) + bgu_ref[...]
        gate = jnp.minimum(gu[:, :d_ff], SWIGLU_LIMIT)
        up = jnp.clip(gu[:, d_ff:], -SWIGLU_LIMIT, SWIGLU_LIMIT)
        act = (up + 1.0) * gate * _sigmoid(SWIGLU_ALPHA * gate)
        y_ref[...] = _pack_bf16_pairs(_dot(act.astype(BF16), wd16[...]) + bd_ref[...])

    @pl.when(i >= nu_ref[0])
    def _():
        y_ref[...] = jnp.zeros_like(y_ref)


def _moe_rows(xs, blk_expert, n_used, wgu, bgu, wd, bd):
    D = 2 * xs.shape[1]
    bm = MOE_ROWS
    n_blk = xs.shape[0] // bm
    d_ff = wd.shape[1]
    wsel = lambda i, be, nu: (be[i], 0, 0)
    grid_spec = pltpu.PrefetchScalarGridSpec(
        num_scalar_prefetch=2,
        grid=(n_blk,),
        in_specs=[pl.BlockSpec((bm, D // 2), lambda i, be, nu: (i, 0)),
                  pl.BlockSpec((None, D, 2 * d_ff), wsel),
                  pl.BlockSpec((None, 1, 2 * d_ff), wsel),
                  pl.BlockSpec((None, d_ff, D), wsel),
                  pl.BlockSpec((None, 1, D), wsel)],
        out_specs=pl.BlockSpec((bm, D // 2), lambda i, be, nu: (i, 0)),
        scratch_shapes=[pltpu.VMEM((D, 2 * d_ff), BF16), pltpu.VMEM((d_ff, D), BF16)],
    )
    return pl.pallas_call(
        functools.partial(_moe_rows_kernel, d_ff=d_ff),
        out_shape=jax.ShapeDtypeStruct((n_blk * bm, D // 2), jnp.int32),
        grid_spec=grid_spec,
        compiler_params=pltpu.CompilerParams(
            dimension_semantics=("arbitrary",), vmem_limit_bytes=VMEM_LIMIT),
        name="moe_rows",
    )(blk_expert, n_used, xs, wgu, bgu.reshape(N_EXPERTS, 1, 2 * d_ff), wd, bd.reshape(N_EXPERTS, 1, D))


def _combine_rows_kernel(*refs):
    y_refs = refs[:TOP_K]
    gw_ref, x1_ref, g2_ref, fg_ref = refs[TOP_K:TOP_K + 4]
    o_ref = refs[-1]
    gw = gw_ref[...]
    y = gw[:, 0:1] * _unpack_bf16_pairs(y_refs[0][...])
    for kk in range(1, TOP_K):
        y = y + gw[:, kk:kk + 1] * _unpack_bf16_pairs(y_refs[kk][...])
    x2 = x1_ref[...] + g2_ref[...] * y
    ms = jnp.mean(x2 * x2, axis=-1, keepdims=True)
    o_ref[...] = x2 * lax.rsqrt(ms + RMS_EPS) * fg_ref[...]


def _combine_rows(yg, gates, x1, gate2, final_g, out_so_far, b0, n_batches):
    S, D = x1.shape
    tm = 256
    steps = S // tm
    slot_spec = lambda kk: pl.BlockSpec((tm, D // 2), lambda i: (kk * steps + i, 0))
    in_specs = [slot_spec(kk) for kk in range(TOP_K)] + [
        pl.BlockSpec((tm, TOP_K), lambda i: (i, 0)),
        pl.BlockSpec((tm, D), lambda i: (i, 0)),
        pl.BlockSpec((1, D), lambda i: (0, 0)),
        pl.BlockSpec((1, D), lambda i: (0, 0))]
    args = [yg] * TOP_K + [gates, x1, gate2, final_g]
    aliases = {}
    if out_so_far is not None:
        in_specs.append(pl.BlockSpec(memory_space=pl.ANY))
        aliases = {len(args): 0}
        args.append(out_so_far)
    return pl.pallas_call(
        _combine_rows_kernel,
        out_shape=jax.ShapeDtypeStruct((n_batches * S, D), F32),
        grid=(steps,),
        in_specs=in_specs,
        out_specs=pl.BlockSpec((tm, D), lambda i: (b0 * steps + i, 0)),
        input_output_aliases=aliases,
        compiler_params=pltpu.CompilerParams(
            dimension_semantics=("arbitrary",), vmem_limit_bytes=VMEM_LIMIT),
        name="combine_rows",
    )(*args)


def _split_bf16x3(w):
    def top(v):
        return lax.bitcast_convert_type(lax.bitcast_convert_type(v, jnp.int32) & jnp.int32(-65536), F32)
    w0 = top(w)
    w1 = top(w - w0)
    w2 = w - w0 - w1
    return jnp.concatenate([w0, w1, w2], axis=1).astype(BF16).T


def _rotary_tables(positions):
    half = ROT_DIM // 2
    inv_freq = jnp.exp(-math.log(ROPE_THETA) * jnp.arange(0, ROT_DIM, 2, dtype=F32) / ROT_DIM)
    d = jnp.arange(128, dtype=jnp.int32) % ATTN_HEAD_DIM
    freq = jnp.where(d < ROT_DIM, inv_freq[d % half], 0.0)
    sign = jnp.where(d < half, -1.0, 1.0)
    ang = positions.astype(F32)[:, :, None] * freq
    return jnp.cos(ang), jnp.sin(ang) * sign


def kernel(x, c, positions, w_ada, b_ada, norm1_g, w_in, hgrn_lb_logits, hgrn_norm_g, attn_norm_g,
           w_out, norm2_g, w_router, b_router, w_gate_up, b_gate_up, w_down, b_down, final_norm_g):
    B, S, D = x.shape
    assert w_in.shape[0] == 1, "single-layer block: the final norm is fused into the combine step"
    l = 0
    ctab, stab = _rotary_tables(positions)
    lower_bounds = jnp.cumsum(jax.nn.softmax(hgrn_lb_logits.astype(F32), axis=0), axis=0)
    mod = _ada(c, w_ada[l], b_ada[l])
    shift1, scale1, gate1, shift2, scale2, gate2 = jnp.split(mod[:, None, :], N_MOD, axis=-1)
    n_blk = (S * TOP_K) // MOE_ROWS + N_EXPERTS
    experts = jnp.arange(N_EXPERTS, dtype=jnp.int32)[:, None]
    tok = jnp.broadcast_to(jnp.arange(S, dtype=jnp.int32)[None, :], (TOP_K, S)).reshape(-1)
    out = None
    for b in range(B):
        one = slice(b, b + 1)
        q, k, lf, v, gt, aq, ak, av, km = _proj(
            x, scale1[one], shift1[one], norm1_g[l][None], w_in[l].astype(BF16), lower_bounds[l][None],
            ctab, stab, b)
        o_a = _moba(aq, km, ak, av, attn_norm_g[l][None])
        o_h = _hgrn(q, k, lf, v, gt, hgrn_norm_g[l][None])
        x1, h2, gates, idx8, counts = _mix(
            o_h, o_a, x, gate1[one], scale2[one], shift2[one], norm2_g[l][None], w_out[l].astype(BF16),
            _split_bf16x3(w_router[l]), b_router[l][:, None], b)
        pad_start, blk_expert, n_used = _tile_layout(counts.reshape(-1), MOE_ROWS, n_blk)
        chosen = idx8[0, 0:TOP_K, :]
        pos = jnp.sum(jnp.where(chosen[:, None, :] == experts, pad_start[:, None], 0), axis=1) + idx8[0, TOP_K:, :]
        xs = _sc_permute(h2.reshape(S, D // 2), tok, pos.reshape(-1), n_blk * MOE_ROWS, 64)
        y_sorted = _moe_rows(xs, blk_expert, n_used, w_gate_up[l], b_gate_up[l], w_down[l], b_down[l])
        yg = _sc_gather(y_sorted, pos.reshape(-1), 64)
        out = _combine_rows(yg, gates.reshape(S, TOP_K), x1.reshape(S, D), gate2[b], final_norm_g[None], out, b, B)
    return out.reshape(B, S, D)
```

```python
import functools
import math

import jax
import jax.numpy as jnp
from jax import lax
from jax.experimental import pallas as pl
from jax.experimental.pallas import tpu as pltpu
from jax.experimental.pallas import tpu_sc as plsc

F32 = jnp.float32
BF16 = jnp.bfloat16
HIGHEST = lax.Precision.HIGHEST

HGRN_DK = 128
HGRN_CHUNK = 64
ATTN_HEADS = 4
ATTN_HEAD_DIM = 64
ROT_DIM = ATTN_HEAD_DIM // 4
ROPE_THETA = 500000.0
MOBA_BLOCK = 256
MOBA_TOPK = 3
N_EXPERTS = 32
TOP_K = 4
SWIGLU_ALPHA = 1.702
SWIGLU_LIMIT = 7.0
N_MOD = 6
RMS_EPS = 1e-6

HGRN_SUB = 16
EXP_CLAMP = 80.0
MOE_ROWS = 512
MOBA_ROWS = 256
MOBA_TILES_PER_STEP = 8
PART_W = 128
V7X_VMEM_BYTES = 64 * 1024 * 1024
VMEM_LIMIT = V7X_VMEM_BYTES * 7 // 8
SC_CORES = 2
SC_SUBCORES = 16


def _sigmoid(x):
    return 1.0 / (1.0 + jnp.exp(-x))


def _dot(a, b, **kw):
    return jnp.dot(a, b, preferred_element_type=F32, **kw)


def _dot_nt(a, b, **kw):
    return lax.dot_general(a, b, (((1,), (1,)), ((), ())), preferred_element_type=F32, **kw)


def _pack_bf16_pairs(x):
    w = x.shape[1] // 2
    bits = lax.bitcast_convert_type(x.astype(BF16).astype(F32), jnp.int32)
    return bits[:, w:] | lax.shift_right_logical(bits[:, :w], 16)


def _unpack_bf16_pairs(p):
    lo = lax.bitcast_convert_type(lax.shift_left(p, 16), F32)
    hi = lax.bitcast_convert_type(p & jnp.int32(-65536), F32)
    return jnp.concatenate([lo, hi], axis=1)


def _ada_kernel(c_ref, w_ref, b_ref, o_ref):
    c = c_ref[...]
    o_ref[...] = _dot(c * _sigmoid(c), w_ref[...], precision=HIGHEST) + b_ref[...]


def _ada(c, w_ada, b_ada):
    B, D = c.shape
    N = w_ada.shape[1]
    tn = N // 4
    c8 = jnp.zeros((8, D), F32).at[:B].set(c)
    out = pl.pallas_call(
        _ada_kernel,
        out_shape=jax.ShapeDtypeStruct((8, N), F32),
        grid=(N // tn,),
        in_specs=[pl.BlockSpec((8, D), lambda j: (0, 0)),
                  pl.BlockSpec((D, tn), lambda j: (0, j)),
                  pl.BlockSpec((1, tn), lambda j: (0, j))],
        out_specs=pl.BlockSpec((8, tn), lambda j: (0, j)),
        compiler_params=pltpu.CompilerParams(vmem_limit_bytes=VMEM_LIMIT),
        name="ada",
    )(c8, w_ada, b_ada.reshape(1, N))
    return out[:B]


def _proj_kernel(x_ref, sc_ref, sh_ref, g_ref, w_ref, lb_ref, ct_ref, st_ref,
                 q_ref, k_ref, lf_ref, v_ref, gt_ref, aq_ref, ak_ref, av_ref, km_ref,
                 *, hw, aw):
    x = x_ref[...]
    ms = jnp.mean(x * x, axis=-1, keepdims=True)
    h = x * lax.rsqrt(ms + RMS_EPS) * g_ref[...]
    h = h * (1.0 + sc_ref[...]) + sh_ref[...]
    proj = _dot(h.astype(BF16), w_ref[...])

    hq = proj[:, 0:hw]
    hf = proj[:, hw:2 * hw]
    hg = proj[:, 3 * hw:4 * hw]
    q_ref[...] = hq * _sigmoid(hq) * (HGRN_DK ** -0.5)
    lb = lb_ref[...]
    f = lb + (1.0 - lb) * _sigmoid(hf)
    k_ref[...] = 1.0 - f
    lf_ref[...] = jnp.log(f)
    v_ref[...] = proj[:, 2 * hw:3 * hw].astype(BF16)
    gt_ref[...] = hg * _sigmoid(hg)

    ct = jnp.concatenate([ct_ref[...]] * (aw // 128), axis=1)
    st = jnp.concatenate([st_ref[...]] * (aw // 128), axis=1)
    lane = lax.broadcasted_iota(jnp.int32, ct.shape, 1) % ATTN_HEAD_DIM
    first_half = lane < (ROT_DIM // 2)

    def rot(t):
        partner = jnp.where(first_half, pltpu.roll(t, aw - ROT_DIM // 2, 1), pltpu.roll(t, ROT_DIM // 2, 1))
        return t * ct + partner * st

    base = 4 * hw
    aq = rot(proj[:, base:base + aw])
    ak = rot(proj[:, base + aw:base + 2 * aw])
    av = proj[:, base + 2 * aw:base + 3 * aw]
    km_ref[...] = jnp.mean(ak, axis=0, keepdims=True)
    lane128 = lax.broadcasted_iota(jnp.int32, (x.shape[0], 128), 1)
    for pair in range(ATTN_HEADS // 2):
        aq_ref[pair] = aq[:, pair * 128:(pair + 1) * 128]
    for hd in range(ATTN_HEADS):
        pair, half = divmod(hd, 2)
        in_head = (lane128 // ATTN_HEAD_DIM) == half
        ak_ref[hd] = jnp.where(in_head, ak[:, pair * 128:(pair + 1) * 128], 0.0).astype(BF16)
        av_ref[hd] = av[:, hd * ATTN_HEAD_DIM:(hd + 1) * ATTN_HEAD_DIM].astype(BF16)


def _proj(x, scale1, shift1, norm_g, w_in_bf16, lb, ctab, stab, b0):
    _, S, D = x.shape
    B = scale1.shape[0]
    hw = lb.shape[-1]
    aw = ATTN_HEADS * ATTN_HEAD_DIM
    tm = MOBA_BLOCK
    nb = S // MOBA_BLOCK
    n_proj = w_in_bf16.shape[1]
    row = lambda b, i: (b, i, 0)
    xrow = lambda b, i: (b0 + b, i, 0)
    vec = lambda b, i: (b, 0, 0)
    head = lambda b, i: (b, 0, i, 0)
    out_shapes = (
        jax.ShapeDtypeStruct((B, S, hw), F32),
        jax.ShapeDtypeStruct((B, S, hw), F32),
        jax.ShapeDtypeStruct((B, S, hw), F32),
        jax.ShapeDtypeStruct((B, S, hw), BF16),
        jax.ShapeDtypeStruct((B, S, hw), F32),
        jax.ShapeDtypeStruct((B, ATTN_HEADS // 2, S, 128), F32),
        jax.ShapeDtypeStruct((B, ATTN_HEADS, S, 128), BF16),
        jax.ShapeDtypeStruct((B, ATTN_HEADS, S, ATTN_HEAD_DIM), BF16),
        jax.ShapeDtypeStruct((B, nb, 1, aw), F32),
    )
    hspec = pl.BlockSpec((None, tm, hw), row)
    aspec = pl.BlockSpec((None, ATTN_HEADS, tm, ATTN_HEAD_DIM), head)
    return pl.pallas_call(
        functools.partial(_proj_kernel, hw=hw, aw=aw),
        out_shape=out_shapes,
        grid=(B, S // tm),
        in_specs=[pl.BlockSpec((None, tm, D), xrow),
                  pl.BlockSpec((None, 1, D), vec),
                  pl.BlockSpec((None, 1, D), vec),
                  pl.BlockSpec((1, D), lambda b, i: (0, 0)),
                  pl.BlockSpec((D, n_proj), lambda b, i: (0, 0)),
                  pl.BlockSpec((1, hw), lambda b, i: (0, 0)),
                  pl.BlockSpec((None, tm, 128), xrow),
                  pl.BlockSpec((None, tm, 128), xrow)],
        out_specs=(hspec, hspec, hspec, hspec, hspec,
                   pl.BlockSpec((None, ATTN_HEADS // 2, tm, 128), head),
                   pl.BlockSpec((None, ATTN_HEADS, tm, 128), head), aspec,
                   pl.BlockSpec((None, None, 1, aw), lambda b, i: (b, i, 0, 0))),
        compiler_params=pltpu.CompilerParams(
            dimension_semantics=("arbitrary", "arbitrary"), vmem_limit_bytes=VMEM_LIMIT),
        name="proj",
    )(x, scale1, shift1, norm_g, w_in_bf16, lb, ctab, stab)


def _hgrn_kernel(q_ref, k_ref, lf_ref, v_ref, gt_ref, gn_ref, o_ref, st_ref, *, n_heads, n_chunks):
    @pl.when(pl.program_id(1) == 0)
    def _():
        st_ref[...] = jnp.zeros_like(st_ref)

    C = HGRN_CHUNK
    r = lax.broadcasted_iota(jnp.int32, (C, C), 0)
    c = lax.broadcasted_iota(jnp.int32, (C, C), 1)
    tril = c <= r
    ltri = tril.astype(F32)
    gn = gn_ref[...]

    def chunk(ci, carry):
        r0 = pl.multiple_of(ci * C, C)
        rows = pl.ds(r0, C)
        b_all = _dot(ltri, lf_ref[rows, :], precision=HIGHEST)
        heads = range(n_heads)
        sls = [slice(hd * HGRN_DK, (hd + 1) * HGRN_DK) for hd in heads]
        bs = [b_all[:, sl] for sl in sls]
        b_lasts = [b[C - 1:C, :] for b in bs]
        qs = [q_ref[rows, sl] for sl in sls]
        ks = [k_ref[rows, sl] for sl in sls]
        vs = [v_ref[rows, sl] for sl in sls]
        states = [st_ref[hd] for hd in heads]
        o_inter = [_dot_nt((qs[hd] * jnp.exp(bs[hd])).astype(BF16), states[hd].astype(BF16)) for hd in heads]
        scores = []
        for hd in heads:
            blocks = []
            for g0 in range(0, C, HGRN_SUB):
                g1 = g0 + HGRN_SUB
                rho = 0.5 * (bs[hd][g0:g0 + 1, :] + bs[hd][g1 - 1:g1, :])
                qa = qs[hd][g0:g1, :] * jnp.exp(jnp.minimum(bs[hd][g0:g1, :] - rho, EXP_CLAMP))
                kb = ks[hd] * jnp.exp(jnp.minimum(rho - bs[hd], EXP_CLAMP))
                blocks.append(_dot_nt(qa.astype(BF16), kb.astype(BF16)))
            scores.append(jnp.where(tril, jnp.concatenate(blocks, axis=0), 0.0).astype(BF16))
        outs = [o_inter[hd] + _dot(scores[hd], vs[hd]) for hd in heads]
        kds = [(ks[hd] * jnp.exp(b_lasts[hd] - bs[hd])).astype(BF16) for hd in heads]
        upds = [_dot(vs[hd].astype(F32).T.astype(BF16), kds[hd]) for hd in heads]
        for hd in heads:
            st_ref[hd] = states[hd] * jnp.exp(b_lasts[hd]) + upds[hd]
            o = outs[hd]
            ms = jnp.mean(o * o, axis=-1, keepdims=True)
            o_ref[rows, sls[hd]] = (o * lax.rsqrt(ms + RMS_EPS) * gn * gt_ref[rows, sls[hd]]).astype(BF16)
        return carry

    lax.fori_loop(0, n_chunks, chunk, 0, unroll=True)


def _hgrn(q, k, lf, v, gt, norm_g):
    B, S, hw = q.shape
    n_heads = hw // HGRN_DK
    tc = 512
    spec = pl.BlockSpec((None, tc, hw), lambda b, i: (b, i, 0))
    return pl.pallas_call(
        functools.partial(_hgrn_kernel, n_heads=n_heads, n_chunks=tc // HGRN_CHUNK),
        out_shape=jax.ShapeDtypeStruct((B, S, hw), BF16),
        grid=(B, S // tc),
        in_specs=[spec, spec, spec, spec, spec, pl.BlockSpec((1, HGRN_DK), lambda b, i: (0, 0))],
        out_specs=spec,
        scratch_shapes=[pltpu.VMEM((n_heads, HGRN_DK, HGRN_DK), F32)],
        compiler_params=pltpu.CompilerParams(
            dimension_semantics=("arbitrary", "arbitrary"), vmem_limit_bytes=VMEM_LIMIT),
        name="hgrn",
    )(q, k, lf, v, gt, norm_g)


def _sc_move_rows(table, src, dst, n_out, chunk):
    M = src.shape[0]
    D = table.shape[1]
    n_workers = SC_CORES * SC_SUBCORES
    per_worker = M // n_workers
    n_chunks = per_worker // chunk
    assert per_worker * n_workers == M and n_chunks * chunk == per_worker and n_chunks % 2 == 0 and chunk % 8 == 0
    mesh = plsc.VectorSubcoreMesh(core_axis_name="c", subcore_axis_name="s")
    idx_t = pltpu.VMEM((chunk,), jnp.int32)
    row_t = pltpu.VMEM((chunk, D), table.dtype)
    sem_t = pltpu.SemaphoreType.DMA

    def body(table_hbm, src_hbm, dst_hbm, out_hbm, src_v, dst_v, rows_v, g_sem, s_sem):
        wid = lax.axis_index("s") * SC_CORES + lax.axis_index("c")
        base = wid * per_worker

        def offset(j):
            return pl.multiple_of(base + j * chunk, 8)

        def gather(b):
            return pltpu.make_async_copy(table_hbm.at[src_v[b]], rows_v[b], g_sem[b])

        def start_gather(j, b):
            pltpu.sync_copy(src_hbm.at[pl.ds(offset(j), chunk)], src_v[b])
            gather(b).start()

        def write_out(j, b):
            if dst_hbm is None:
                pltpu.sync_copy(rows_v[b], out_hbm.at[pl.ds(offset(j), chunk)])
            else:
                pltpu.sync_copy(dst_hbm.at[pl.ds(offset(j), chunk)], dst_v[b])
                pltpu.async_copy(rows_v[b], out_hbm.at[dst_v[b]], s_sem[b]).wait()

        start_gather(0, 0)

        @pl.loop(0, n_chunks, step=2)
        def _(j):
            for b in (0, 1):
                @pl.when(j + b + 1 < n_chunks)
                def _():
                    start_gather(j + b + 1, 1 - b)
                gather(b).wait()
                write_out(j + b, b)

    if dst is None:
        @functools.partial(pl.kernel, mesh=mesh, out_type=jax.ShapeDtypeStruct((n_out, D), table.dtype),
                           scratch_types=[idx_t, idx_t, row_t, row_t, sem_t, sem_t])
        def gather_kernel(table_hbm, src_hbm, out_hbm, s0, s1, r0, r1, g0, g1):
            body(table_hbm, src_hbm, None, out_hbm, (s0, s1), None, (r0, r1), (g0, g1), None)
        return gather_kernel(table, src)

    @functools.partial(pl.kernel, mesh=mesh, out_type=jax.ShapeDtypeStruct((n_out, D), table.dtype),
                       scratch_types=[idx_t, idx_t, idx_t, idx_t, row_t, row_t, sem_t, sem_t, sem_t, sem_t])
    def permute_kernel(table_hbm, src_hbm, dst_hbm, out_hbm, s0, s1, d0, d1, r0, r1, g0, g1, w0, w1):
        body(table_hbm, src_hbm, dst_hbm, out_hbm, (s0, s1), (d0, d1), (r0, r1), (g0, g1), (w0, w1))
    return permute_kernel(table, src, dst)


def _sc_gather(table, idx, chunk):
    return _sc_move_rows(table, idx, None, idx.shape[0], chunk)


def _sc_permute(table, src, dst, n_out, chunk):
    return _sc_move_rows(table, src, dst, n_out, chunk)


def _tile_layout(counts, bm, n_tiles):
    n_groups = counts.shape[0]
    padded = (counts + bm - 1) // bm * bm
    pad_end = jnp.cumsum(padded)
    tile_start = jnp.arange(n_tiles, dtype=jnp.int32) * bm
    tile_group = jnp.minimum(
        jnp.sum((pad_end[None, :] <= tile_start[:, None]).astype(jnp.int32), axis=1), n_groups - 1)
    n_used = (pad_end[-1] // bm).astype(jnp.int32).reshape(1)
    return pad_end - padded, tile_group.astype(jnp.int32), n_used


def _null_partial(rows):
    lane = lax.broadcasted_iota(jnp.int32, (rows, PART_W), 1)
    return jnp.where(lane < ATTN_HEAD_DIM, 0.0, -jnp.inf).astype(F32)


def _moba_sel_kernel(q_ref, km_ref, k_ref, v_ref, idx_ref, cnt_ref, own_ref, cnt_acc, *, n_blocks):
    j = pl.program_id(1)
    T = MOBA_BLOCK
    heads = range(ATTN_HEADS)
    qs = [q_ref[hd // 2] for hd in heads]
    gates = [_dot_nt(km_ref[hd], qs[hd], precision=HIGHEST) for hd in heads]
    blk = lax.broadcasted_iota(jnp.int32, gates[0].shape, 0)
    neg_inf = jnp.float32(-jnp.inf)
    gates = [jnp.where(blk < j, g, neg_inf) for g in gates]
    picks = [[] for _ in heads]
    for _ in range(MOBA_TOPK):
        ms = [jnp.max(g, axis=0, keepdims=True) for g in gates]
        firsts = [jnp.min(jnp.where(g == m, blk, n_blocks), axis=0, keepdims=True) for g, m in zip(gates, ms)]
        for hd in heads:
            picks[hd].append(jnp.where(ms[hd] > neg_inf, firsts[hd], -1))
        gates = [jnp.where(blk == f, neg_inf, g) for g, f in zip(gates, firsts)]

    @pl.when(j == 0)
    def _():
        cnt_acc[...] = jnp.zeros_like(cnt_acc)

    earlier = (lax.broadcasted_iota(jnp.int32, (T, T), 0) < lax.broadcasted_iota(jnp.int32, (T, T), 1)).astype(BF16)
    for hd in heads:
        onehots = [(blk == p).astype(F32) for p in picks[hd]]
        member = onehots[0] + onehots[1] + onehots[2]
        base = cnt_acc[hd] + _dot(member.astype(BF16), earlier)
        ranks = [jnp.sum(oh * base, axis=0, keepdims=True).astype(jnp.int32) for oh in onehots]
        idx_ref[hd] = jnp.concatenate(picks[hd] + ranks + [jnp.zeros((2, T), jnp.int32)], axis=0)
        total = cnt_acc[hd] + jnp.sum(member, axis=1, keepdims=True)
        cnt_acc[hd] = total
        cnt_ref[hd] = total.astype(jnp.int32)
    causal = lax.broadcasted_iota(jnp.int32, (T, T), 1) <= lax.broadcasted_iota(jnp.int32, (T, T), 0)
    scale = ATTN_HEAD_DIM ** -0.5
    ss = [jnp.where(causal, _dot_nt((qs[hd] * scale).astype(BF16), k_ref[hd]), neg_inf) for hd in heads]
    mx = [jnp.max(s, axis=1, keepdims=True) for s in ss]
    ps = [jnp.exp(s - m) for s, m in zip(ss, mx)]
    ls = [jnp.sum(p, axis=1, keepdims=True) for p in ps]
    accs = [_dot(ps[hd].astype(BF16), v_ref[hd]) for hd in heads]
    for hd in heads:
        lse = jnp.broadcast_to(mx[hd] + jnp.log(ls[hd]), (T, PART_W - ATTN_HEAD_DIM))
        own_ref[hd] = jnp.concatenate([accs[hd] / ls[hd], lse], axis=1)


def _moba_sel(aq, kmean, ak, av):
    B, H, S, hd = av.shape
    nb = S // MOBA_BLOCK
    T = MOBA_BLOCK
    blk = lambda b, j: (b, 0, j, 0)
    return pl.pallas_call(
        functools.partial(_moba_sel_kernel, n_blocks=nb),
        out_shape=(jax.ShapeDtypeStruct((B, H, 8, S), jnp.int32),
                   jax.ShapeDtypeStruct((B, H, nb, 1), jnp.int32),
                   jax.ShapeDtypeStruct((B, H, S, PART_W), F32)),
        grid=(B, nb),
        in_specs=[pl.BlockSpec((None, H // 2, T, 128), blk),
                  pl.BlockSpec((None, H, nb, 128), lambda b, j: (b, 0, 0, 0)),
                  pl.BlockSpec((None, H, T, 128), blk),
                  pl.BlockSpec((None, H, T, hd), blk)],
        out_specs=(pl.BlockSpec((None, H, 8, T), lambda b, j: (b, 0, 0, j)),
                   pl.BlockSpec((None, H, nb, 1), lambda b, j: (b, 0, 0, 0)),
                   pl.BlockSpec((None, H, T, PART_W), blk)),
        scratch_shapes=[pltpu.VMEM((H, nb, 1), F32)],
        compiler_params=pltpu.CompilerParams(
            dimension_semantics=("arbitrary", "arbitrary"), vmem_limit_bytes=VMEM_LIMIT),
        name="moba_sel",
    )(aq, kmean, ak, av)


def _moba_blk_kernel(tg_ref, nu_ref, q_ref, *refs):
    n = MOBA_TILES_PER_STEP
    k_refs, v_refs, o_ref = refs[:n], refs[n:2 * n], refs[2 * n]
    R = MOBA_ROWS
    t0 = pl.program_id(0) * n

    @pl.when(t0 < nu_ref[0])
    def _():
        scale = ATTN_HEAD_DIM ** -0.5
        ss = [_dot_nt((q_ref[j * R:(j + 1) * R, :] * scale).astype(BF16), k_refs[j][...]) for j in range(n)]
        ms = [jnp.max(s, axis=1, keepdims=True) for s in ss]
        ps = [jnp.exp(s - m) for s, m in zip(ss, ms)]
        ls = [jnp.sum(p, axis=1, keepdims=True) for p in ps]
        accs = [_dot(p.astype(BF16), v_refs[j][...]) for j, p in enumerate(ps)]
        null = _null_partial(R)
        for j in range(n):
            lse = jnp.broadcast_to(ms[j] + jnp.log(ls[j]), (R, PART_W - ATTN_HEAD_DIM))
            row = jnp.concatenate([accs[j] / ls[j], lse], axis=1)
            o_ref[j * R:(j + 1) * R, :] = jnp.where(t0 + j < nu_ref[0], row, null)

    @pl.when(t0 >= nu_ref[0])
    def _():
        o_ref[...] = _null_partial(n * R)


def _moba_blk(qs, tile_group, n_used, ak, av):
    B, H, S, hd = av.shape
    nb = S // MOBA_BLOCK
    R = MOBA_ROWS
    n = MOBA_TILES_PER_STEP
    n_tiles = qs.shape[0] // R
    assert n_tiles % n == 0
    kv = lambda j: (lambda i, tg, nu: (tg[i * n + j] // nb, tg[i * n + j] % nb, 0, 0))
    grid_spec = pltpu.PrefetchScalarGridSpec(
        num_scalar_prefetch=2,
        grid=(n_tiles // n,),
        in_specs=[pl.BlockSpec((n * R, 128), lambda i, tg, nu: (i, 0))]
        + [pl.BlockSpec((None, None, MOBA_BLOCK, 128), kv(j)) for j in range(n)]
        + [pl.BlockSpec((None, None, MOBA_BLOCK, hd), kv(j)) for j in range(n)],
        out_specs=pl.BlockSpec((n * R, PART_W), lambda i, tg, nu: (i, 0)),
    )
    k4 = ak.reshape(B * H, nb, MOBA_BLOCK, 128)
    v4 = av.reshape(B * H, nb, MOBA_BLOCK, hd)
    return pl.pallas_call(
        _moba_blk_kernel,
        out_shape=jax.ShapeDtypeStruct((n_tiles * R, PART_W), F32),
        grid_spec=grid_spec,
        compiler_params=pltpu.CompilerParams(
            dimension_semantics=("arbitrary",), vmem_limit_bytes=VMEM_LIMIT),
        name="moba_blk",
    )(tile_group, n_used, qs, *([k4] * n), *([v4] * n))


def _moba_merge_kernel(own_ref, pg_ref, g_ref, o_ref):
    hd = ATTN_HEAD_DIM
    rows = [own_ref[...]] + [pg_ref[s] for s in range(MOBA_TOPK)]
    lses = [pltpu.roll(r, hd, 1) for r in rows]
    top = lses[0]
    for z in lses[1:]:
        top = jnp.maximum(top, z)
    num = jnp.zeros_like(top)
    den = jnp.zeros_like(top)
    for r, z in zip(rows, lses):
        w = jnp.exp(z - top)
        num = num + w * r
        den = den + w
    o = (num / den)[:, :hd]
    ms = jnp.mean(o * o, axis=-1, keepdims=True)
    o_ref[...] = o * lax.rsqrt(ms + RMS_EPS) * g_ref[...]


def _moba_merge(own, pg, norm_g):
    n = own.shape[0]
    T = 512
    row = lambda i: (i, 0)
    return pl.pallas_call(
        _moba_merge_kernel,
        out_shape=jax.ShapeDtypeStruct((n, ATTN_HEAD_DIM), F32),
        grid=(n // T,),
        in_specs=[pl.BlockSpec((T, PART_W), row),
                  pl.BlockSpec((MOBA_TOPK, T, PART_W), lambda i: (0, i, 0)),
                  pl.BlockSpec((1, ATTN_HEAD_DIM), lambda i: (0, 0))],
        out_specs=pl.BlockSpec((T, ATTN_HEAD_DIM), row),
        compiler_params=pltpu.CompilerParams(
            dimension_semantics=("arbitrary",), vmem_limit_bytes=VMEM_LIMIT),
        name="moba_merge",
    )(own, pg, norm_g)


def _moba(aq, km, ak, av, norm_g):
    B, H, S, hd = av.shape
    nb = S // MOBA_BLOCK
    n_q = B * H * S
    kmp = km.reshape(B, nb, H // 2, 128)
    half = jnp.arange(128, dtype=jnp.int32) // hd
    kmean = jnp.stack([jnp.where(half == h % 2, kmp[:, :, h // 2, :], 0.0) for h in range(H)], axis=1)
    idx8, counts, own = _moba_sel(aq, kmean, ak, av)
    sel = idx8[:, :, 0:MOBA_TOPK, :].reshape(B * H, MOBA_TOPK, S)
    rank = idx8[:, :, MOBA_TOPK:2 * MOBA_TOPK, :].reshape(B * H, MOBA_TOPK, S)
    n_groups = B * H * nb
    n_tiles = (n_q * MOBA_TOPK) // MOBA_ROWS + n_groups
    pad_start, tile_group, n_used = _tile_layout(counts.reshape(-1), MOBA_ROWS, n_tiles)
    blocks = jnp.arange(nb, dtype=jnp.int32)[:, None]
    start = jnp.sum(jnp.where(sel[:, :, None, :] == blocks, pad_start.reshape(B * H, 1, nb, 1), 0), axis=2)
    a_ids = jnp.arange(n_q * MOBA_TOPK, dtype=jnp.int32).reshape(B * H, MOBA_TOPK, S)
    assert n_tiles * MOBA_ROWS >= n_q * MOBA_TOPK + MOBA_ROWS
    pos = jnp.where(sel >= 0, start + rank, n_used[0] * MOBA_ROWS + a_ids % MOBA_ROWS)
    bh = jnp.arange(B * H, dtype=jnp.int32)[:, None, None]
    t = jnp.arange(S, dtype=jnp.int32)[None, None, :]
    pair_row = jnp.broadcast_to((bh // H * (H // 2) + bh % H // 2) * S + t, pos.shape)
    qs = _sc_permute(aq.reshape(B * (H // 2) * S, 128), pair_row.reshape(-1), pos.reshape(-1),
                     n_tiles * MOBA_ROWS, 256)
    parts = _moba_blk(qs, tile_group, n_used, ak, av)
    pg = _sc_gather(parts, pos.transpose(1, 0, 2).reshape(-1), 256)
    o = _moba_merge(own.reshape(n_q, PART_W), pg.reshape(MOBA_TOPK, n_q, PART_W), norm_g)
    return o.reshape(B, H, S, hd)


def _mix_kernel(oh_ref, oa_ref, x_ref, g1_ref, sc2_ref, sh2_ref, n2_ref, wo_ref, wr_ref, br_ref, c0_ref, *rest):
    x1_ref, h2_ref, gw_ref, idx_ref, cnt_ref, cnt_acc = rest[-6:]
    cat = jnp.concatenate([oh_ref[...]] + [oa_ref[hd] for hd in range(ATTN_HEADS)], axis=1)
    mix = _dot(cat.astype(BF16), wo_ref[...])
    x1 = x_ref[...] + g1_ref[...] * mix
    x1_ref[...] = x1
    ms = jnp.mean(x1 * x1, axis=-1, keepdims=True)
    h2 = x1 * lax.rsqrt(ms + RMS_EPS) * n2_ref[...]
    h2 = h2 * (1.0 + sc2_ref[...]) + sh2_ref[...]
    h2_ref[...] = _pack_bf16_pairs(h2)
    E = N_EXPERTS
    tm = h2.shape[0]
    h_0 = h2.astype(BF16)
    r_1 = h2 - h_0.astype(F32)
    h_1 = r_1.astype(BF16)
    h_2 = (r_1 - h_1.astype(F32)).astype(BF16)
    wt = wr_ref[...]
    p_0 = _dot_nt(wt, h_0)
    p_1 = _dot_nt(wt[:2 * E], h_1)
    p_2 = _dot_nt(wt[:E], h_2)
    logits = (p_0[:E] + (p_0[E:2 * E] + p_1[:E]) + (p_0[2 * E:] + p_1[E:] + p_2)) + br_ref[...]
    ex = lax.broadcasted_iota(jnp.int32, logits.shape, 0)
    neg_inf = jnp.float32(-jnp.inf)
    vals, idxs = [], []
    for _ in range(TOP_K):
        m = jnp.max(logits, axis=0, keepdims=True)
        first = jnp.min(jnp.where(logits == m, ex, E), axis=0, keepdims=True)
        vals.append(m)
        idxs.append(first)
        logits = jnp.where(ex == first, neg_inf, logits)
    e = [jnp.exp(v - vals[0]) for v in vals]
    denom = e[0] + e[1] + e[2] + e[3]
    gate_rows = jnp.concatenate([ei / denom for ei in e] + [jnp.zeros((128 - TOP_K, tm), F32)], axis=0)
    gw_ref[...] = gate_rows.T[:, :TOP_K]

    @pl.when((pl.program_id(0) == 0) & (pl.program_id(1) == 0))
    def _():
        cnt_acc[...] = c0_ref[...].astype(F32)

    earlier = (lax.broadcasted_iota(jnp.int32, (tm, tm), 0) < lax.broadcasted_iota(jnp.int32, (tm, tm), 1)).astype(BF16)
    onehots = [(ex == ix).astype(F32) for ix in idxs]
    member = onehots[0] + onehots[1] + onehots[2] + onehots[3]
    base = cnt_acc[...] + _dot(member.astype(BF16), earlier)
    ranks = [jnp.sum(oh * base, axis=0, keepdims=True).astype(jnp.int32) for oh in onehots]
    idx_ref[...] = jnp.concatenate(idxs + ranks, axis=0)
    total = cnt_acc[...] + jnp.sum(member, axis=1, keepdims=True)
    cnt_acc[...] = total
    cnt_ref[...] = total.astype(jnp.int32)


def _mix(oh, oa, x, gate1, scale2, shift2, norm2_g, w_out_bf16, w_router, b_router, counts_so_far, h2_so_far, b0):
    n_batches, S, D = x.shape
    hw = oh.shape[-1]
    tm = 256
    row = lambda b, i: (b, i, 0)
    xrow = lambda b, i: (b0, i, 0)
    vec = lambda b, i: (b, 0, 0)
    const = lambda b, i: (0, 0)
    in_specs = [pl.BlockSpec((None, tm, hw), row),
                pl.BlockSpec((None, ATTN_HEADS, tm, ATTN_HEAD_DIM), lambda b, i: (b, 0, i, 0)),
                pl.BlockSpec((None, tm, D), xrow),
                pl.BlockSpec((None, 1, D), vec),
                pl.BlockSpec((None, 1, D), vec),
                pl.BlockSpec((None, 1, D), vec),
                pl.BlockSpec((1, D), const),
                pl.BlockSpec((D, D), const),
                pl.BlockSpec((3 * N_EXPERTS, D), const),
                pl.BlockSpec((N_EXPERTS, 1), const),
                pl.BlockSpec((N_EXPERTS, 1), const)]
    args = [oh, oa, x, gate1, scale2, shift2, norm2_g, w_out_bf16, w_router, b_router, counts_so_far]
    aliases = {}
    if h2_so_far is not None:
        in_specs.append(pl.BlockSpec(memory_space=pl.ANY))
        aliases = {len(args): 1}
        args.append(h2_so_far)
    return pl.pallas_call(
        _mix_kernel,
        out_shape=(jax.ShapeDtypeStruct((1, S, D), F32),
                   jax.ShapeDtypeStruct((n_batches, S, D // 2), jnp.int32),
                   jax.ShapeDtypeStruct((1, S, TOP_K), F32),
                   jax.ShapeDtypeStruct((1, 2 * TOP_K, S), jnp.int32),
                   jax.ShapeDtypeStruct((N_EXPERTS, 1), jnp.int32)),
        grid=(1, S // tm),
        in_specs=in_specs,
        out_specs=(pl.BlockSpec((None, tm, D), row),
                   pl.BlockSpec((None, tm, D // 2), xrow),
                   pl.BlockSpec((None, tm, TOP_K), row),
                   pl.BlockSpec((None, 2 * TOP_K, tm), lambda b, i: (b, 0, i)),
                   pl.BlockSpec((N_EXPERTS, 1), const)),
        scratch_shapes=[pltpu.VMEM((N_EXPERTS, 1), F32)],
        input_output_aliases=aliases,
        compiler_params=pltpu.CompilerParams(
            dimension_semantics=("arbitrary", "arbitrary"), vmem_limit_bytes=VMEM_LIMIT),
        name="mix",
    )(*args)


def _moe_rows_kernel(be_ref, nu_ref, x_ref, wgu_ref, bgu_ref, wd_ref, bd_ref, y_ref, wgu16, wd16, *, d_ff):
    i = pl.program_id(0)

    @pl.when((i == 0) | (be_ref[i] != be_ref[jnp.maximum(i - 1, 0)]))
    def _():
        wgu16[...] = wgu_ref[...].astype(BF16)
        wd16[...] = wd_ref[...].astype(BF16)

    @pl.when(i < nu_ref[0])
    def _():
        gu = _dot(_unpack_bf16_pairs(x_ref[...]).astype(BF16), wgu16[...]) + bgu_ref[...]
        gate = jnp.minimum(gu[:, :d_ff], SWIGLU_LIMIT)
        up = jnp.clip(gu[:, d_ff:], -SWIGLU_LIMIT, SWIGLU_LIMIT)
        act = (up + 1.0) * gate * _sigmoid(SWIGLU_ALPHA * gate)
        y_ref[...] = _pack_bf16_pairs(_dot(act.astype(BF16), wd16[...]) + bd_ref[...])

    @pl.when(i >= nu_ref[0])
    def _():
        y_ref[...] = jnp.zeros_like(y_ref)


def _moe_rows(xs, blk_expert, n_used, wgu, bgu, wd, bd):
    D = 2 * xs.shape[1]
    bm = MOE_ROWS
    n_blk = xs.shape[0] // bm
    d_ff = wd.shape[1]
    wsel = lambda i, be, nu: (be[i], 0, 0)
    grid_spec = pltpu.PrefetchScalarGridSpec(
        num_scalar_prefetch=2,
        grid=(n_blk,),
        in_specs=[pl.BlockSpec((bm, D // 2), lambda i, be, nu: (i, 0)),
                  pl.BlockSpec((None, D, 2 * d_ff), wsel),
                  pl.BlockSpec((None, 1, 2 * d_ff), wsel),
                  pl.BlockSpec((None, d_ff, D), wsel),
                  pl.BlockSpec((None, 1, D), wsel)],
        out_specs=pl.BlockSpec((bm, D // 2), lambda i, be, nu: (i, 0)),
        scratch_shapes=[pltpu.VMEM((D, 2 * d_ff), BF16), pltpu.VMEM((d_ff, D), BF16)],
    )
    return pl.pallas_call(
        functools.partial(_moe_rows_kernel, d_ff=d_ff),
        out_shape=jax.ShapeDtypeStruct((n_blk * bm, D // 2), jnp.int32),
        grid_spec=grid_spec,
        compiler_params=pltpu.CompilerParams(
            dimension_semantics=("arbitrary",), vmem_limit_bytes=VMEM_LIMIT),
        name="moe_rows",
    )(blk_expert, n_used, xs, wgu, bgu.reshape(N_EXPERTS, 1, 2 * d_ff), wd, bd.reshape(N_EXPERTS, 1, D))


def _combine_rows_kernel(*refs):
    y_refs = refs[:TOP_K]
    gw_ref, x1_ref, g2_ref, fg_ref = refs[TOP_K:TOP_K + 4]
    o_ref = refs[-1]
    gw = gw_ref[...]
    y = gw[:, 0:1] * _unpack_bf16_pairs(y_refs[0][...])
    for kk in range(1, TOP_K):
        y = y + gw[:, kk:kk + 1] * _unpack_bf16_pairs(y_refs[kk][...])
    x2 = x1_ref[...] + g2_ref[...] * y
    ms = jnp.mean(x2 * x2, axis=-1, keepdims=True)
    o_ref[...] = x2 * lax.rsqrt(ms + RMS_EPS) * fg_ref[...]


def _combine_rows(yg, gates, x1, gate2, final_g, out_so_far, b0, n_batches):
    S, D = x1.shape
    tm = 256
    steps = S // tm
    slot_spec = lambda kk: pl.BlockSpec((tm, D // 2), lambda i: (kk * steps + i, 0))
    in_specs = [slot_spec(kk) for kk in range(TOP_K)] + [
        pl.BlockSpec((tm, TOP_K), lambda i: (i, 0)),
        pl.BlockSpec((tm, D), lambda i: (i, 0)),
        pl.BlockSpec((1, D), lambda i: (0, 0)),
        pl.BlockSpec((1, D), lambda i: (0, 0))]
    args = [yg] * TOP_K + [gates, x1, gate2, final_g]
    aliases = {}
    if out_so_far is not None:
        in_specs.append(pl.BlockSpec(memory_space=pl.ANY))
        aliases = {len(args): 0}
        args.append(out_so_far)
    return pl.pallas_call(
        _combine_rows_kernel,
        out_shape=jax.ShapeDtypeStruct((n_batches * S, D), F32),
        grid=(steps,),
        in_specs=in_specs,
        out_specs=pl.BlockSpec((tm, D), lambda i: (b0 * steps + i, 0)),
        input_output_aliases=aliases,
        compiler_params=pltpu.CompilerParams(
            dimension_semantics=("arbitrary",), vmem_limit_bytes=VMEM_LIMIT),
        name="combine_rows",
    )(*args)


def _split_bf16x3(w):
    def top(v):
        return lax.bitcast_convert_type(lax.bitcast_convert_type(v, jnp.int32) & jnp.int32(-65536), F32)
    w0 = top(w)
    w1 = top(w - w0)
    w2 = w - w0 - w1
    return jnp.concatenate([w0, w1, w2], axis=1).astype(BF16).T


def _rotary_tables(positions):
    half = ROT_DIM // 2
    inv_freq = jnp.exp(-math.log(ROPE_THETA) * jnp.arange(0, ROT_DIM, 2, dtype=F32) / ROT_DIM)
    d = jnp.arange(128, dtype=jnp.int32) % ATTN_HEAD_DIM
    freq = jnp.where(d < ROT_DIM, inv_freq[d % half], 0.0)
    sign = jnp.where(d < half, -1.0, 1.0)
    ang = positions.astype(F32)[:, :, None] * freq
    return jnp.cos(ang), jnp.sin(ang) * sign


def kernel(x, c, positions, w_ada, b_ada, norm1_g, w_in, hgrn_lb_logits, hgrn_norm_g, attn_norm_g,
           w_out, norm2_g, w_router, b_router, w_gate_up, b_gate_up, w_down, b_down, final_norm_g):
    B, S, D = x.shape
    assert w_in.shape[0] == 1, "single-layer block: the final norm is fused into the combine step"
    l = 0
    ctab, stab = _rotary_tables(positions)
    lower_bounds = jnp.cumsum(jax.nn.softmax(hgrn_lb_logits.astype(F32), axis=0), axis=0)
    mod = _ada(c, w_ada[l], b_ada[l])
    shift1, scale1, gate1, shift2, scale2, gate2 = jnp.split(mod[:, None, :], N_MOD, axis=-1)
    T = B * S
    x1s, gates, idx8s = [], [], []
    counts = jnp.zeros((N_EXPERTS, 1), jnp.int32)
    h2 = None
    for b in range(B):
        one = slice(b, b + 1)
        q, k, lf, v, gt, aq, ak, av, km = _proj(
            x, scale1[one], shift1[one], norm1_g[l][None], w_in[l].astype(BF16), lower_bounds[l][None],
            ctab, stab, b)
        o_a = _moba(aq, km, ak, av, attn_norm_g[l][None])
        o_h = _hgrn(q, k, lf, v, gt, hgrn_norm_g[l][None])
        x1_b, h2, gates_b, idx8_b, counts = _mix(
            o_h, o_a, x, gate1[one], scale2[one], shift2[one], norm2_g[l][None], w_out[l].astype(BF16),
            _split_bf16x3(w_router[l]), b_router[l][:, None], counts, h2, b)
        x1s.append(x1_b)
        gates.append(gates_b)
        idx8s.append(idx8_b)
    n_blk = (T * TOP_K) // MOE_ROWS + N_EXPERTS
    pad_start, blk_expert, n_used = _tile_layout(counts.reshape(-1), MOE_ROWS, n_blk)
    experts = jnp.arange(N_EXPERTS, dtype=jnp.int32)[:, None]
    idx8 = jnp.concatenate(idx8s, axis=0)
    chosen = idx8[:, 0:TOP_K, :]
    pos = jnp.sum(jnp.where(chosen[:, :, None, :] == experts, pad_start[:, None], 0), axis=2) + idx8[:, TOP_K:, :]
    tok = jnp.broadcast_to(jnp.arange(B, dtype=jnp.int32)[:, None, None] * S
                           + jnp.arange(S, dtype=jnp.int32)[None, None, :], pos.shape)
    xs = _sc_permute(h2.reshape(T, D // 2), tok.reshape(-1), pos.reshape(-1), n_blk * MOE_ROWS, 64)
    y_sorted = _moe_rows(xs, blk_expert, n_used, w_gate_up[l], b_gate_up[l], w_down[l], b_down[l])
    out = None
    for b in range(B):
        yg = _sc_gather(y_sorted, pos[b].reshape(-1), 64)
        out = _combine_rows(yg, gates[b].reshape(S, TOP_K), x1s[b].reshape(S, D), gate2[b], final_norm_g[None],
                            out, b, B)
    return out.reshape(B, S, D)
```

```python
import functools
import math

import jax
import jax.numpy as jnp
from jax import lax
from jax.experimental import pallas as pl
from jax.experimental.pallas import tpu as pltpu
from jax.experimental.pallas import tpu_sc as plsc

F32 = jnp.float32
BF16 = jnp.bfloat16
HIGHEST = lax.Precision.HIGHEST

HGRN_DK = 128
HGRN_CHUNK = 64
ATTN_HEADS = 4
ATTN_HEAD_DIM = 64
ROT_DIM = ATTN_HEAD_DIM // 4
ROPE_THETA = 500000.0
MOBA_BLOCK = 256
MOBA_TOPK = 3
N_EXPERTS = 32
TOP_K = 4
SWIGLU_ALPHA = 1.702
SWIGLU_LIMIT = 7.0
N_MOD = 6
RMS_EPS = 1e-6

HGRN_SUB = 16
EXP_CLAMP = 80.0
MOE_ROWS = 512
MOBA_ROWS = 256
MOBA_TILES_PER_STEP = 8
PART_W = 128
V7X_VMEM_BYTES = 64 * 1024 * 1024
VMEM_LIMIT = V7X_VMEM_BYTES * 7 // 8
SC_CORES = 2
SC_SUBCORES = 16


def _sigmoid(x):
    return 1.0 / (1.0 + jnp.exp(-x))


def _dot(a, b, **kw):
    return jnp.dot(a, b, preferred_element_type=F32, **kw)


def _dot_nt(a, b, **kw):
    return lax.dot_general(a, b, (((1,), (1,)), ((), ())), preferred_element_type=F32, **kw)


def _pack_bf16_pairs(x):
    w = x.shape[1] // 2
    bits = lax.bitcast_convert_type(x.astype(BF16).astype(F32), jnp.int32)
    return bits[:, w:] | lax.shift_right_logical(bits[:, :w], 16)


def _unpack_bf16_pairs(p):
    lo = lax.bitcast_convert_type(lax.shift_left(p, 16), F32)
    hi = lax.bitcast_convert_type(p & jnp.int32(-65536), F32)
    return jnp.concatenate([lo, hi], axis=1)


def _ada_kernel(c_ref, w_ref, b_ref, o_ref):
    c = c_ref[...]
    o_ref[...] = _dot(c * _sigmoid(c), w_ref[...], precision=HIGHEST) + b_ref[...]


def _ada(c, w_ada, b_ada):
    B, D = c.shape
    N = w_ada.shape[1]
    tn = N // 4
    c8 = jnp.zeros((8, D), F32).at[:B].set(c)
    out = pl.pallas_call(
        _ada_kernel,
        out_shape=jax.ShapeDtypeStruct((8, N), F32),
        grid=(N // tn,),
        in_specs=[pl.BlockSpec((8, D), lambda j: (0, 0)),
                  pl.BlockSpec((D, tn), lambda j: (0, j)),
                  pl.BlockSpec((1, tn), lambda j: (0, j))],
        out_specs=pl.BlockSpec((8, tn), lambda j: (0, j)),
        compiler_params=pltpu.CompilerParams(vmem_limit_bytes=VMEM_LIMIT),
        name="ada",
    )(c8, w_ada, b_ada.reshape(1, N))
    return out[:B]


def _proj_kernel(x_ref, sc_ref, sh_ref, g_ref, w_ref, lb_ref, ct_ref, st_ref,
                 q_ref, k_ref, lf_ref, v_ref, gt_ref, aq_ref, ak_ref, av_ref, km_ref,
                 *, hw, aw):
    x = x_ref[...]
    ms = jnp.mean(x * x, axis=-1, keepdims=True)
    h = x * lax.rsqrt(ms + RMS_EPS) * g_ref[...]
    h = h * (1.0 + sc_ref[...]) + sh_ref[...]
    proj = _dot(h.astype(BF16), w_ref[...])

    hq = proj[:, 0:hw]
    hf = proj[:, hw:2 * hw]
    hg = proj[:, 3 * hw:4 * hw]
    q_ref[...] = hq * _sigmoid(hq) * (HGRN_DK ** -0.5)
    lb = lb_ref[...]
    f = lb + (1.0 - lb) * _sigmoid(hf)
    k_ref[...] = 1.0 - f
    lf_ref[...] = jnp.log(f)
    v_ref[...] = proj[:, 2 * hw:3 * hw].astype(BF16)
    gt_ref[...] = hg * _sigmoid(hg)

    ct = jnp.concatenate([ct_ref[...]] * (aw // 128), axis=1)
    st = jnp.concatenate([st_ref[...]] * (aw // 128), axis=1)
    lane = lax.broadcasted_iota(jnp.int32, ct.shape, 1) % ATTN_HEAD_DIM
    first_half = lane < (ROT_DIM // 2)

    def rot(t):
        partner = jnp.where(first_half, pltpu.roll(t, aw - ROT_DIM // 2, 1), pltpu.roll(t, ROT_DIM // 2, 1))
        return t * ct + partner * st

    base = 4 * hw
    aq = rot(proj[:, base:base + aw])
    ak = rot(proj[:, base + aw:base + 2 * aw])
    av = proj[:, base + 2 * aw:base + 3 * aw]
    km_ref[...] = jnp.mean(ak, axis=0, keepdims=True)
    lane128 = lax.broadcasted_iota(jnp.int32, (x.shape[0], 128), 1)
    for pair in range(ATTN_HEADS // 2):
        aq_ref[pair] = aq[:, pair * 128:(pair + 1) * 128]
    for hd in range(ATTN_HEADS):
        pair, half = divmod(hd, 2)
        in_head = (lane128 // ATTN_HEAD_DIM) == half
        ak_ref[hd] = jnp.where(in_head, ak[:, pair * 128:(pair + 1) * 128], 0.0).astype(BF16)
        av_ref[hd] = av[:, hd * ATTN_HEAD_DIM:(hd + 1) * ATTN_HEAD_DIM].astype(BF16)


def _proj(x, scale1, shift1, norm_g, w_in_bf16, lb, ctab, stab, b0):
    _, S, D = x.shape
    B = scale1.shape[0]
    hw = lb.shape[-1]
    aw = ATTN_HEADS * ATTN_HEAD_DIM
    tm = MOBA_BLOCK
    nb = S // MOBA_BLOCK
    n_proj = w_in_bf16.shape[1]
    row = lambda b, i: (b, i, 0)
    xrow = lambda b, i: (b0 + b, i, 0)
    vec = lambda b, i: (b, 0, 0)
    head = lambda b, i: (b, 0, i, 0)
    out_shapes = (
        jax.ShapeDtypeStruct((B, S, hw), F32),
        jax.ShapeDtypeStruct((B, S, hw), F32),
        jax.ShapeDtypeStruct((B, S, hw), F32),
        jax.ShapeDtypeStruct((B, S, hw), BF16),
        jax.ShapeDtypeStruct((B, S, hw), F32),
        jax.ShapeDtypeStruct((B, ATTN_HEADS // 2, S, 128), F32),
        jax.ShapeDtypeStruct((B, ATTN_HEADS, S, 128), BF16),
        jax.ShapeDtypeStruct((B, ATTN_HEADS, S, ATTN_HEAD_DIM), BF16),
        jax.ShapeDtypeStruct((B, nb, 1, aw), F32),
    )
    hspec = pl.BlockSpec((None, tm, hw), row)
    aspec = pl.BlockSpec((None, ATTN_HEADS, tm, ATTN_HEAD_DIM), head)
    return pl.pallas_call(
        functools.partial(_proj_kernel, hw=hw, aw=aw),
        out_shape=out_shapes,
        grid=(B, S // tm),
        in_specs=[pl.BlockSpec((None, tm, D), xrow),
                  pl.BlockSpec((None, 1, D), vec),
                  pl.BlockSpec((None, 1, D), vec),
                  pl.BlockSpec((1, D), lambda b, i: (0, 0)),
                  pl.BlockSpec((D, n_proj), lambda b, i: (0, 0)),
                  pl.BlockSpec((1, hw), lambda b, i: (0, 0)),
                  pl.BlockSpec((None, tm, 128), xrow),
                  pl.BlockSpec((None, tm, 128), xrow)],
        out_specs=(hspec, hspec, hspec, hspec, hspec,
                   pl.BlockSpec((None, ATTN_HEADS // 2, tm, 128), head),
                   pl.BlockSpec((None, ATTN_HEADS, tm, 128), head), aspec,
                   pl.BlockSpec((None, None, 1, aw), lambda b, i: (b, i, 0, 0))),
        compiler_params=pltpu.CompilerParams(
            dimension_semantics=("arbitrary", "arbitrary"), vmem_limit_bytes=VMEM_LIMIT),
        name="proj",
    )(x, scale1, shift1, norm_g, w_in_bf16, lb, ctab, stab)


def _hgrn_kernel(q_ref, k_ref, lf_ref, v_ref, gt_ref, gn_ref, o_ref, st_ref, *, n_heads, n_chunks):
    @pl.when(pl.program_id(1) == 0)
    def _():
        st_ref[...] = jnp.zeros_like(st_ref)

    C = HGRN_CHUNK
    r = lax.broadcasted_iota(jnp.int32, (C, C), 0)
    c = lax.broadcasted_iota(jnp.int32, (C, C), 1)
    tril = c <= r
    ltri = tril.astype(F32)
    gn = gn_ref[...]

    def chunk(ci, carry):
        r0 = pl.multiple_of(ci * C, C)
        rows = pl.ds(r0, C)
        b_all = _dot(ltri, lf_ref[rows, :], precision=HIGHEST)
        heads = range(n_heads)
        sls = [slice(hd * HGRN_DK, (hd + 1) * HGRN_DK) for hd in heads]
        bs = [b_all[:, sl] for sl in sls]
        b_lasts = [b[C - 1:C, :] for b in bs]
        qs = [q_ref[rows, sl] for sl in sls]
        ks = [k_ref[rows, sl] for sl in sls]
        vs = [v_ref[rows, sl] for sl in sls]
        states = [st_ref[hd] for hd in heads]
        o_inter = [_dot_nt((qs[hd] * jnp.exp(bs[hd])).astype(BF16), states[hd].astype(BF16)) for hd in heads]
        scores = []
        for hd in heads:
            blocks = []
            for g0 in range(0, C, HGRN_SUB):
                g1 = g0 + HGRN_SUB
                rho = 0.5 * (bs[hd][g0:g0 + 1, :] + bs[hd][g1 - 1:g1, :])
                qa = qs[hd][g0:g1, :] * jnp.exp(jnp.minimum(bs[hd][g0:g1, :] - rho, EXP_CLAMP))
                kb = ks[hd] * jnp.exp(jnp.minimum(rho - bs[hd], EXP_CLAMP))
                blocks.append(_dot_nt(qa.astype(BF16), kb.astype(BF16)))
            scores.append(jnp.where(tril, jnp.concatenate(blocks, axis=0), 0.0).astype(BF16))
        outs = [o_inter[hd] + _dot(scores[hd], vs[hd]) for hd in heads]
        kds = [(ks[hd] * jnp.exp(b_lasts[hd] - bs[hd])).astype(BF16) for hd in heads]
        upds = [_dot(vs[hd].astype(F32).T.astype(BF16), kds[hd]) for hd in heads]
        for hd in heads:
            st_ref[hd] = states[hd] * jnp.exp(b_lasts[hd]) + upds[hd]
            o = outs[hd]
            ms = jnp.mean(o * o, axis=-1, keepdims=True)
            o_ref[rows, sls[hd]] = (o * lax.rsqrt(ms + RMS_EPS) * gn * gt_ref[rows, sls[hd]]).astype(BF16)
        return carry

    lax.fori_loop(0, n_chunks, chunk, 0, unroll=True)


def _hgrn(q, k, lf, v, gt, norm_g):
    B, S, hw = q.shape
    n_heads = hw // HGRN_DK
    tc = 512
    spec = pl.BlockSpec((None, tc, hw), lambda b, i: (b, i, 0))
    return pl.pallas_call(
        functools.partial(_hgrn_kernel, n_heads=n_heads, n_chunks=tc // HGRN_CHUNK),
        out_shape=jax.ShapeDtypeStruct((B, S, hw), BF16),
        grid=(B, S // tc),
        in_specs=[spec, spec, spec, spec, spec, pl.BlockSpec((1, HGRN_DK), lambda b, i: (0, 0))],
        out_specs=spec,
        scratch_shapes=[pltpu.VMEM((n_heads, HGRN_DK, HGRN_DK), F32)],
        compiler_params=pltpu.CompilerParams(
            dimension_semantics=("arbitrary", "arbitrary"), vmem_limit_bytes=VMEM_LIMIT),
        name="hgrn",
    )(q, k, lf, v, gt, norm_g)


def _sc_move_rows(table, src, dst, n_out, chunk):
    M = src.shape[0]
    D = table.shape[1]
    n_workers = SC_CORES * SC_SUBCORES
    per_worker = M // n_workers
    n_chunks = per_worker // chunk
    assert per_worker * n_workers == M and n_chunks * chunk == per_worker and n_chunks % 2 == 0 and chunk % 8 == 0
    mesh = plsc.VectorSubcoreMesh(core_axis_name="c", subcore_axis_name="s")
    idx_t = pltpu.VMEM((chunk,), jnp.int32)
    row_t = pltpu.VMEM((chunk, D), table.dtype)
    sem_t = pltpu.SemaphoreType.DMA

    def body(table_hbm, src_hbm, dst_hbm, out_hbm, src_v, dst_v, rows_v, g_sem, s_sem):
        wid = lax.axis_index("s") * SC_CORES + lax.axis_index("c")
        base = wid * per_worker

        def offset(j):
            return pl.multiple_of(base + j * chunk, 8)

        def gather(b):
            return pltpu.make_async_copy(table_hbm.at[src_v[b]], rows_v[b], g_sem[b])

        def start_gather(j, b):
            pltpu.sync_copy(src_hbm.at[pl.ds(offset(j), chunk)], src_v[b])
            gather(b).start()

        def write_out(j, b):
            if dst_hbm is None:
                pltpu.sync_copy(rows_v[b], out_hbm.at[pl.ds(offset(j), chunk)])
            else:
                pltpu.sync_copy(dst_hbm.at[pl.ds(offset(j), chunk)], dst_v[b])
                pltpu.async_copy(rows_v[b], out_hbm.at[dst_v[b]], s_sem[b]).wait()

        start_gather(0, 0)

        @pl.loop(0, n_chunks, step=2)
        def _(j):
            for b in (0, 1):
                @pl.when(j + b + 1 < n_chunks)
                def _():
                    start_gather(j + b + 1, 1 - b)
                gather(b).wait()
                write_out(j + b, b)

    if dst is None:
        @functools.partial(pl.kernel, mesh=mesh, out_type=jax.ShapeDtypeStruct((n_out, D), table.dtype),
                           scratch_types=[idx_t, idx_t, row_t, row_t, sem_t, sem_t])
        def gather_kernel(table_hbm, src_hbm, out_hbm, s0, s1, r0, r1, g0, g1):
            body(table_hbm, src_hbm, None, out_hbm, (s0, s1), None, (r0, r1), (g0, g1), None)
        return gather_kernel(table, src)

    @functools.partial(pl.kernel, mesh=mesh, out_type=jax.ShapeDtypeStruct((n_out, D), table.dtype),
                       scratch_types=[idx_t, idx_t, idx_t, idx_t, row_t, row_t, sem_t, sem_t, sem_t, sem_t])
    def permute_kernel(table_hbm, src_hbm, dst_hbm, out_hbm, s0, s1, d0, d1, r0, r1, g0, g1, w0, w1):
        body(table_hbm, src_hbm, dst_hbm, out_hbm, (s0, s1), (d0, d1), (r0, r1), (g0, g1), (w0, w1))
    return permute_kernel(table, src, dst)


def _sc_gather(table, idx, chunk):
    return _sc_move_rows(table, idx, None, idx.shape[0], chunk)


def _sc_permute(table, src, dst, n_out, chunk):
    return _sc_move_rows(table, src, dst, n_out, chunk)


def _tile_layout(counts, bm, n_tiles):
    n_groups = counts.shape[0]
    padded = (counts + bm - 1) // bm * bm
    pad_end = jnp.cumsum(padded)
    tile_start = jnp.arange(n_tiles, dtype=jnp.int32) * bm
    tile_group = jnp.minimum(
        jnp.sum((pad_end[None, :] <= tile_start[:, None]).astype(jnp.int32), axis=1), n_groups - 1)
    n_used = (pad_end[-1] // bm).astype(jnp.int32).reshape(1)
    return pad_end - padded, tile_group.astype(jnp.int32), n_used


def _null_partial(rows):
    lane = lax.broadcasted_iota(jnp.int32, (rows, PART_W), 1)
    return jnp.where(lane < ATTN_HEAD_DIM, 0.0, -jnp.inf).astype(F32)


def _moba_sel_kernel(q_ref, km_ref, k_ref, v_ref, idx_ref, cnt_ref, own_ref, cnt_acc, *, n_blocks):
    j = pl.program_id(1)
    T = MOBA_BLOCK
    heads = range(ATTN_HEADS)
    qs = [q_ref[hd // 2] for hd in heads]
    gates = [_dot_nt(km_ref[hd], qs[hd], precision=HIGHEST) for hd in heads]
    blk = lax.broadcasted_iota(jnp.int32, gates[0].shape, 0)
    neg_inf = jnp.float32(-jnp.inf)
    gates = [jnp.where(blk < j, g, neg_inf) for g in gates]
    picks = [[] for _ in heads]
    for _ in range(MOBA_TOPK):
        ms = [jnp.max(g, axis=0, keepdims=True) for g in gates]
        firsts = [jnp.min(jnp.where(g == m, blk, n_blocks), axis=0, keepdims=True) for g, m in zip(gates, ms)]
        for hd in heads:
            picks[hd].append(jnp.where(ms[hd] > neg_inf, firsts[hd], -1))
        gates = [jnp.where(blk == f, neg_inf, g) for g, f in zip(gates, firsts)]

    @pl.when(j == 0)
    def _():
        cnt_acc[...] = jnp.zeros_like(cnt_acc)

    earlier = (lax.broadcasted_iota(jnp.int32, (T, T), 0) < lax.broadcasted_iota(jnp.int32, (T, T), 1)).astype(BF16)
    for hd in heads:
        onehots = [(blk == p).astype(F32) for p in picks[hd]]
        member = onehots[0] + onehots[1] + onehots[2]
        base = cnt_acc[hd] + _dot(member.astype(BF16), earlier)
        ranks = [jnp.sum(oh * base, axis=0, keepdims=True).astype(jnp.int32) for oh in onehots]
        idx_ref[hd] = jnp.concatenate(picks[hd] + ranks + [jnp.zeros((2, T), jnp.int32)], axis=0)
        total = cnt_acc[hd] + jnp.sum(member, axis=1, keepdims=True)
        cnt_acc[hd] = total
        cnt_ref[hd] = total.astype(jnp.int32)
    causal = lax.broadcasted_iota(jnp.int32, (T, T), 1) <= lax.broadcasted_iota(jnp.int32, (T, T), 0)
    scale = ATTN_HEAD_DIM ** -0.5
    ss = [jnp.where(causal, _dot_nt((qs[hd] * scale).astype(BF16), k_ref[hd]), neg_inf) for hd in heads]
    mx = [jnp.max(s, axis=1, keepdims=True) for s in ss]
    ps = [jnp.exp(s - m) for s, m in zip(ss, mx)]
    ls = [jnp.sum(p, axis=1, keepdims=True) for p in ps]
    accs = [_dot(ps[hd].astype(BF16), v_ref[hd]) for hd in heads]
    for hd in heads:
        lse = jnp.broadcast_to(mx[hd] + jnp.log(ls[hd]), (T, PART_W - ATTN_HEAD_DIM))
        own_ref[hd] = jnp.concatenate([accs[hd] / ls[hd], lse], axis=1)


def _moba_sel(aq, kmean, ak, av):
    B, H, S, hd = av.shape
    nb = S // MOBA_BLOCK
    T = MOBA_BLOCK
    blk = lambda b, j: (b, 0, j, 0)
    return pl.pallas_call(
        functools.partial(_moba_sel_kernel, n_blocks=nb),
        out_shape=(jax.ShapeDtypeStruct((B, H, 8, S), jnp.int32),
                   jax.ShapeDtypeStruct((B, H, nb, 1), jnp.int32),
                   jax.ShapeDtypeStruct((B, H, S, PART_W), F32)),
        grid=(B, nb),
        in_specs=[pl.BlockSpec((None, H // 2, T, 128), blk),
                  pl.BlockSpec((None, H, nb, 128), lambda b, j: (b, 0, 0, 0)),
                  pl.BlockSpec((None, H, T, 128), blk),
                  pl.BlockSpec((None, H, T, hd), blk)],
        out_specs=(pl.BlockSpec((None, H, 8, T), lambda b, j: (b, 0, 0, j)),
                   pl.BlockSpec((None, H, nb, 1), lambda b, j: (b, 0, 0, 0)),
                   pl.BlockSpec((None, H, T, PART_W), blk)),
        scratch_shapes=[pltpu.VMEM((H, nb, 1), F32)],
        compiler_params=pltpu.CompilerParams(
            dimension_semantics=("arbitrary", "arbitrary"), vmem_limit_bytes=VMEM_LIMIT),
        name="moba_sel",
    )(aq, kmean, ak, av)


def _moba_blk_kernel(tg_ref, nu_ref, q_ref, *refs):
    n = MOBA_TILES_PER_STEP
    k_refs, v_refs, o_ref = refs[:n], refs[n:2 * n], refs[2 * n]
    R = MOBA_ROWS
    t0 = pl.program_id(0) * n

    @pl.when(t0 < nu_ref[0])
    def _():
        scale = ATTN_HEAD_DIM ** -0.5
        ss = [_dot_nt((q_ref[j * R:(j + 1) * R, :] * scale).astype(BF16), k_refs[j][...]) for j in range(n)]
        ms = [jnp.max(s, axis=1, keepdims=True) for s in ss]
        ps = [jnp.exp(s - m) for s, m in zip(ss, ms)]
        ls = [jnp.sum(p, axis=1, keepdims=True) for p in ps]
        accs = [_dot(p.astype(BF16), v_refs[j][...]) for j, p in enumerate(ps)]
        null = _null_partial(R)
        for j in range(n):
            lse = jnp.broadcast_to(ms[j] + jnp.log(ls[j]), (R, PART_W - ATTN_HEAD_DIM))
            row = jnp.concatenate([accs[j] / ls[j], lse], axis=1)
            o_ref[j * R:(j + 1) * R, :] = jnp.where(t0 + j < nu_ref[0], row, null)

    @pl.when(t0 >= nu_ref[0])
    def _():
        o_ref[...] = _null_partial(n * R)


def _moba_blk(qs, tile_group, n_used, ak, av):
    B, H, S, hd = av.shape
    nb = S // MOBA_BLOCK
    R = MOBA_ROWS
    n = MOBA_TILES_PER_STEP
    n_tiles = qs.shape[0] // R
    assert n_tiles % n == 0
    kv = lambda j: (lambda i, tg, nu: (tg[i * n + j] // nb, tg[i * n + j] % nb, 0, 0))
    grid_spec = pltpu.PrefetchScalarGridSpec(
        num_scalar_prefetch=2,
        grid=(n_tiles // n,),
        in_specs=[pl.BlockSpec((n * R, 128), lambda i, tg, nu: (i, 0))]
        + [pl.BlockSpec((None, None, MOBA_BLOCK, 128), kv(j)) for j in range(n)]
        + [pl.BlockSpec((None, None, MOBA_BLOCK, hd), kv(j)) for j in range(n)],
        out_specs=pl.BlockSpec((n * R, PART_W), lambda i, tg, nu: (i, 0)),
    )
    k4 = ak.reshape(B * H, nb, MOBA_BLOCK, 128)
    v4 = av.reshape(B * H, nb, MOBA_BLOCK, hd)
    return pl.pallas_call(
        _moba_blk_kernel,
        out_shape=jax.ShapeDtypeStruct((n_tiles * R, PART_W), F32),
        grid_spec=grid_spec,
        compiler_params=pltpu.CompilerParams(
            dimension_semantics=("arbitrary",), vmem_limit_bytes=VMEM_LIMIT),
        name="moba_blk",
    )(tile_group, n_used, qs, *([k4] * n), *([v4] * n))


def _moba_merge_kernel(own_ref, pg_ref, g_ref, o_ref):
    hd = ATTN_HEAD_DIM
    rows = [own_ref[...]] + [pg_ref[s] for s in range(MOBA_TOPK)]
    lses = [pltpu.roll(r, hd, 1) for r in rows]
    top = lses[0]
    for z in lses[1:]:
        top = jnp.maximum(top, z)
    num = jnp.zeros_like(top)
    den = jnp.zeros_like(top)
    for r, z in zip(rows, lses):
        w = jnp.exp(z - top)
        num = num + w * r
        den = den + w
    o = (num / den)[:, :hd]
    ms = jnp.mean(o * o, axis=-1, keepdims=True)
    o_ref[...] = o * lax.rsqrt(ms + RMS_EPS) * g_ref[...]


def _moba_merge(own, pg, norm_g):
    n = own.shape[0]
    T = 512
    row = lambda i: (i, 0)
    return pl.pallas_call(
        _moba_merge_kernel,
        out_shape=jax.ShapeDtypeStruct((n, ATTN_HEAD_DIM), F32),
        grid=(n // T,),
        in_specs=[pl.BlockSpec((T, PART_W), row),
                  pl.BlockSpec((MOBA_TOPK, T, PART_W), lambda i: (0, i, 0)),
                  pl.BlockSpec((1, ATTN_HEAD_DIM), lambda i: (0, 0))],
        out_specs=pl.BlockSpec((T, ATTN_HEAD_DIM), row),
        compiler_params=pltpu.CompilerParams(
            dimension_semantics=("arbitrary",), vmem_limit_bytes=VMEM_LIMIT),
        name="moba_merge",
    )(own, pg, norm_g)


def _moba(aq, km, ak, av, norm_g):
    B, H, S, hd = av.shape
    nb = S // MOBA_BLOCK
    n_q = B * H * S
    kmp = km.reshape(B, nb, H // 2, 128)
    half = jnp.arange(128, dtype=jnp.int32) // hd
    kmean = jnp.stack([jnp.where(half == h % 2, kmp[:, :, h // 2, :], 0.0) for h in range(H)], axis=1)
    idx8, counts, own = _moba_sel(aq, kmean, ak, av)
    sel = idx8[:, :, 0:MOBA_TOPK, :].reshape(B * H, MOBA_TOPK, S)
    rank = idx8[:, :, MOBA_TOPK:2 * MOBA_TOPK, :].reshape(B * H, MOBA_TOPK, S)
    n_groups = B * H * nb
    n_tiles = (n_q * MOBA_TOPK) // MOBA_ROWS + n_groups
    pad_start, tile_group, n_used = _tile_layout(counts.reshape(-1), MOBA_ROWS, n_tiles)
    blocks = jnp.arange(nb, dtype=jnp.int32)[:, None]
    start = jnp.sum(jnp.where(sel[:, :, None, :] == blocks, pad_start.reshape(B * H, 1, nb, 1), 0), axis=2)
    a_ids = jnp.arange(n_q * MOBA_TOPK, dtype=jnp.int32).reshape(B * H, MOBA_TOPK, S)
    assert n_tiles * MOBA_ROWS >= n_q * MOBA_TOPK + MOBA_ROWS
    pos = jnp.where(sel >= 0, start + rank, n_used[0] * MOBA_ROWS + a_ids % MOBA_ROWS)
    bh = jnp.arange(B * H, dtype=jnp.int32)[:, None, None]
    t = jnp.arange(S, dtype=jnp.int32)[None, None, :]
    pair_row = jnp.broadcast_to((bh // H * (H // 2) + bh % H // 2) * S + t, pos.shape)
    qs = _sc_permute(aq.reshape(B * (H // 2) * S, 128), pair_row.reshape(-1), pos.reshape(-1),
                     n_tiles * MOBA_ROWS, 256)
    parts = _moba_blk(qs, tile_group, n_used, ak, av)
    pg = _sc_gather(parts, pos.transpose(1, 0, 2).reshape(-1), 256)
    o = _moba_merge(own.reshape(n_q, PART_W), pg.reshape(MOBA_TOPK, n_q, PART_W), norm_g)
    return o.reshape(B, H, S, hd)


def _mix_kernel(oh_ref, oa_ref, x_ref, g1_ref, sc2_ref, sh2_ref, n2_ref, wo_ref, wr_ref, br_ref,
                x1_ref, h2_ref, gw_ref, idx_ref, cnt_ref, cnt_acc):
    cat = jnp.concatenate([oh_ref[...]] + [oa_ref[hd] for hd in range(ATTN_HEADS)], axis=1)
    mix = _dot(cat.astype(BF16), wo_ref[...])
    x1 = x_ref[...] + g1_ref[...] * mix
    x1_ref[...] = x1
    ms = jnp.mean(x1 * x1, axis=-1, keepdims=True)
    h2 = x1 * lax.rsqrt(ms + RMS_EPS) * n2_ref[...]
    h2 = h2 * (1.0 + sc2_ref[...]) + sh2_ref[...]
    h2_ref[...] = _pack_bf16_pairs(h2)
    E = N_EXPERTS
    tm = h2.shape[0]
    h_0 = h2.astype(BF16)
    r_1 = h2 - h_0.astype(F32)
    h_1 = r_1.astype(BF16)
    h_2 = (r_1 - h_1.astype(F32)).astype(BF16)
    wt = wr_ref[...]
    p_0 = _dot_nt(wt, h_0)
    p_1 = _dot_nt(wt[:2 * E], h_1)
    p_2 = _dot_nt(wt[:E], h_2)
    logits = (p_0[:E] + (p_0[E:2 * E] + p_1[:E]) + (p_0[2 * E:] + p_1[E:] + p_2)) + br_ref[...]
    ex = lax.broadcasted_iota(jnp.int32, logits.shape, 0)
    neg_inf = jnp.float32(-jnp.inf)
    vals, idxs = [], []
    for _ in range(TOP_K):
        m = jnp.max(logits, axis=0, keepdims=True)
        first = jnp.min(jnp.where(logits == m, ex, E), axis=0, keepdims=True)
        vals.append(m)
        idxs.append(first)
        logits = jnp.where(ex == first, neg_inf, logits)
    e = [jnp.exp(v - vals[0]) for v in vals]
    denom = e[0] + e[1] + e[2] + e[3]
    gate_rows = jnp.concatenate([ei / denom for ei in e] + [jnp.zeros((128 - TOP_K, tm), F32)], axis=0)
    gw_ref[...] = gate_rows.T[:, :TOP_K]

    @pl.when((pl.program_id(0) == 0) & (pl.program_id(1) == 0))
    def _():
        cnt_acc[...] = jnp.zeros_like(cnt_acc)

    earlier = (lax.broadcasted_iota(jnp.int32, (tm, tm), 0) < lax.broadcasted_iota(jnp.int32, (tm, tm), 1)).astype(BF16)
    onehots = [(ex == ix).astype(F32) for ix in idxs]
    member = onehots[0] + onehots[1] + onehots[2] + onehots[3]
    base = cnt_acc[...] + _dot(member.astype(BF16), earlier)
    ranks = [jnp.sum(oh * base, axis=0, keepdims=True).astype(jnp.int32) for oh in onehots]
    idx_ref[...] = jnp.concatenate(idxs + ranks, axis=0)
    total = cnt_acc[...] + jnp.sum(member, axis=1, keepdims=True)
    cnt_acc[...] = total
    cnt_ref[...] = total.astype(jnp.int32)


def _mix(oh, oa, x, gate1, scale2, shift2, norm2_g, w_out_bf16, w_router, b_router, b0):
    _, S, D = x.shape
    B = oh.shape[0]
    hw = oh.shape[-1]
    tm = 256
    row = lambda b, i: (b, i, 0)
    xrow = lambda b, i: (b0 + b, i, 0)
    vec = lambda b, i: (b, 0, 0)
    const = lambda b, i: (0, 0)
    return pl.pallas_call(
        _mix_kernel,
        out_shape=(jax.ShapeDtypeStruct((B, S, D), F32),
                   jax.ShapeDtypeStruct((B, S, D // 2), jnp.int32),
                   jax.ShapeDtypeStruct((B, S, TOP_K), F32),
                   jax.ShapeDtypeStruct((B, 2 * TOP_K, S), jnp.int32),
                   jax.ShapeDtypeStruct((N_EXPERTS, 1), jnp.int32)),
        grid=(B, S // tm),
        in_specs=[pl.BlockSpec((None, tm, hw), row),
                  pl.BlockSpec((None, ATTN_HEADS, tm, ATTN_HEAD_DIM), lambda b, i: (b, 0, i, 0)),
                  pl.BlockSpec((None, tm, D), xrow),
                  pl.BlockSpec((None, 1, D), vec),
                  pl.BlockSpec((None, 1, D), vec),
                  pl.BlockSpec((None, 1, D), vec),
                  pl.BlockSpec((1, D), const),
                  pl.BlockSpec((D, D), const),
                  pl.BlockSpec((3 * N_EXPERTS, D), const),
                  pl.BlockSpec((N_EXPERTS, 1), const)],
        out_specs=(pl.BlockSpec((None, tm, D), row),
                   pl.BlockSpec((None, tm, D // 2), row),
                   pl.BlockSpec((None, tm, TOP_K), row),
                   pl.BlockSpec((None, 2 * TOP_K, tm), lambda b, i: (b, 0, i)),
                   pl.BlockSpec((N_EXPERTS, 1), const)),
        scratch_shapes=[pltpu.VMEM((N_EXPERTS, 1), F32)],
        compiler_params=pltpu.CompilerParams(
            dimension_semantics=("arbitrary", "arbitrary"), vmem_limit_bytes=VMEM_LIMIT),
        name="mix",
    )(oh, oa, x, gate1, scale2, shift2, norm2_g, w_out_bf16, w_router, b_router)


def _moe_rows_kernel(be_ref, nx_ref, nu_ref, x_ref, wgu_hbm, bgu_ref, wd_hbm, bd_ref, y_ref,
                     wgu32, wd32, wgu16, wd16, sem, *, d_ff):
    i = pl.program_id(0)
    e = be_ref[i]

    def fetch(expert):
        return (pltpu.make_async_copy(wgu_hbm.at[expert], wgu32, sem.at[0]),
                pltpu.make_async_copy(wd_hbm.at[expert], wd32, sem.at[1]))

    @pl.when(i == 0)
    def _():
        for c in fetch(e):
            c.start()

    @pl.when((i == 0) | (e != be_ref[jnp.maximum(i - 1, 0)]))
    def _():
        for c in fetch(e):
            c.wait()
        wgu16[...] = wgu32[...].astype(BF16)
        wd16[...] = wd32[...].astype(BF16)

        @pl.when(nx_ref[i] >= 0)
        def _():
            for c in fetch(nx_ref[i]):
                c.start()

    @pl.when(i < nu_ref[0])
    def _():
        gu = _dot(_unpack_bf16_pairs(x_ref[...]).astype(BF16), wgu16[...]) + bgu_ref[...]
        gate = jnp.minimum(gu[:, :d_ff], SWIGLU_LIMIT)
        up = jnp.clip(gu[:, d_ff:], -SWIGLU_LIMIT, SWIGLU_LIMIT)
        act = (up + 1.0) * gate * _sigmoid(SWIGLU_ALPHA * gate)
        y_ref[...] = _pack_bf16_pairs(_dot(act.astype(BF16), wd16[...]) + bd_ref[...])

    @pl.when(i >= nu_ref[0])
    def _():
        y_ref[...] = jnp.zeros_like(y_ref)


def _moe_rows(xs, blk_expert, n_used, wgu, bgu, wd, bd):
    D = 2 * xs.shape[1]
    bm = MOE_ROWS
    n_blk = xs.shape[0] // bm
    d_ff = wd.shape[1]
    run_end = jnp.sum((blk_expert[None, :] <= blk_expert[:, None]).astype(jnp.int32), axis=1)
    next_expert = jnp.where(run_end < n_blk, blk_expert[jnp.minimum(run_end, n_blk - 1)], -1).astype(jnp.int32)
    bsel = lambda i, be, nx, nu: (be[i], 0, 0)
    rows = lambda i, be, nx, nu: (i, 0)
    grid_spec = pltpu.PrefetchScalarGridSpec(
        num_scalar_prefetch=3,
        grid=(n_blk,),
        in_specs=[pl.BlockSpec((bm, D // 2), rows),
                  pl.BlockSpec(memory_space=pl.ANY),
                  pl.BlockSpec((None, 1, 2 * d_ff), bsel),
                  pl.BlockSpec(memory_space=pl.ANY),
                  pl.BlockSpec((None, 1, D), bsel)],
        out_specs=pl.BlockSpec((bm, D // 2), rows),
        scratch_shapes=[pltpu.VMEM((D, 2 * d_ff), F32), pltpu.VMEM((d_ff, D), F32),
                        pltpu.VMEM((D, 2 * d_ff), BF16), pltpu.VMEM((d_ff, D), BF16),
                        pltpu.SemaphoreType.DMA((2,))],
    )
    return pl.pallas_call(
        functools.partial(_moe_rows_kernel, d_ff=d_ff),
        out_shape=jax.ShapeDtypeStruct((n_blk * bm, D // 2), jnp.int32),
        grid_spec=grid_spec,
        compiler_params=pltpu.CompilerParams(
            dimension_semantics=("arbitrary",), vmem_limit_bytes=VMEM_LIMIT),
        name="moe_rows",
    )(blk_expert, next_expert, n_used, xs, wgu, bgu.reshape(N_EXPERTS, 1, 2 * d_ff), wd, bd.reshape(N_EXPERTS, 1, D))


def _combine_rows_kernel(*refs):
    y_refs = refs[:TOP_K]
    gw_ref, x1_ref, g2_ref, fg_ref = refs[TOP_K:TOP_K + 4]
    o_ref = refs[-1]
    gw = gw_ref[...]
    y = gw[:, 0:1] * _unpack_bf16_pairs(y_refs[0][...])
    for kk in range(1, TOP_K):
        y = y + gw[:, kk:kk + 1] * _unpack_bf16_pairs(y_refs[kk][...])
    x2 = x1_ref[...] + g2_ref[...] * y
    ms = jnp.mean(x2 * x2, axis=-1, keepdims=True)
    o_ref[...] = x2 * lax.rsqrt(ms + RMS_EPS) * fg_ref[...]


def _combine_rows(yg, gates, x1, gate2, final_g, out_so_far, b0, n_batches):
    S, D = x1.shape
    tm = 256
    steps = S // tm
    slot_spec = lambda kk: pl.BlockSpec((tm, D // 2), lambda i: (kk * steps + i, 0))
    in_specs = [slot_spec(kk) for kk in range(TOP_K)] + [
        pl.BlockSpec((tm, TOP_K), lambda i: (i, 0)),
        pl.BlockSpec((tm, D), lambda i: (i, 0)),
        pl.BlockSpec((1, D), lambda i: (0, 0)),
        pl.BlockSpec((1, D), lambda i: (0, 0))]
    args = [yg] * TOP_K + [gates, x1, gate2, final_g]
    aliases = {}
    if out_so_far is not None:
        in_specs.append(pl.BlockSpec(memory_space=pl.ANY))
        aliases = {len(args): 0}
        args.append(out_so_far)
    return pl.pallas_call(
        _combine_rows_kernel,
        out_shape=jax.ShapeDtypeStruct((n_batches * S, D), F32),
        grid=(steps,),
        in_specs=in_specs,
        out_specs=pl.BlockSpec((tm, D), lambda i: (b0 * steps + i, 0)),
        input_output_aliases=aliases,
        compiler_params=pltpu.CompilerParams(
            dimension_semantics=("arbitrary",), vmem_limit_bytes=VMEM_LIMIT),
        name="combine_rows",
    )(*args)


def _split_bf16x3(w):
    def top(v):
        return lax.bitcast_convert_type(lax.bitcast_convert_type(v, jnp.int32) & jnp.int32(-65536), F32)
    w0 = top(w)
    w1 = top(w - w0)
    w2 = w - w0 - w1
    return jnp.concatenate([w0, w1, w2], axis=1).astype(BF16).T


def _rotary_tables(positions):
    half = ROT_DIM // 2
    inv_freq = jnp.exp(-math.log(ROPE_THETA) * jnp.arange(0, ROT_DIM, 2, dtype=F32) / ROT_DIM)
    d = jnp.arange(128, dtype=jnp.int32) % ATTN_HEAD_DIM
    freq = jnp.where(d < ROT_DIM, inv_freq[d % half], 0.0)
    sign = jnp.where(d < half, -1.0, 1.0)
    ang = positions.astype(F32)[:, :, None] * freq
    return jnp.cos(ang), jnp.sin(ang) * sign


def kernel(x, c, positions, w_ada, b_ada, norm1_g, w_in, hgrn_lb_logits, hgrn_norm_g, attn_norm_g,
           w_out, norm2_g, w_router, b_router, w_gate_up, b_gate_up, w_down, b_down, final_norm_g):
    B, S, D = x.shape
    assert w_in.shape[0] == 1, "single-layer block: the final norm is fused into the combine step"
    l = 0
    ctab, stab = _rotary_tables(positions)
    lower_bounds = jnp.cumsum(jax.nn.softmax(hgrn_lb_logits.astype(F32), axis=0), axis=0)
    mod = _ada(c, w_ada[l], b_ada[l])
    shift1, scale1, gate1, shift2, scale2, gate2 = jnp.split(mod[:, None, :], N_MOD, axis=-1)
    n_blk = (S * TOP_K) // MOE_ROWS + N_EXPERTS
    experts = jnp.arange(N_EXPERTS, dtype=jnp.int32)[:, None]
    tok = jnp.broadcast_to(jnp.arange(S, dtype=jnp.int32)[None, :], (TOP_K, S)).reshape(-1)
    out = None
    for b in range(B):
        one = slice(b, b + 1)
        q, k, lf, v, gt, aq, ak, av, km = _proj(
            x, scale1[one], shift1[one], norm1_g[l][None], w_in[l].astype(BF16), lower_bounds[l][None],
            ctab, stab, b)
        o_a = _moba(aq, km, ak, av, attn_norm_g[l][None])
        o_h = _hgrn(q, k, lf, v, gt, hgrn_norm_g[l][None])
        x1, h2, gates, idx8, counts = _mix(
            o_h, o_a, x, gate1[one], scale2[one], shift2[one], norm2_g[l][None], w_out[l].astype(BF16),
            _split_bf16x3(w_router[l]), b_router[l][:, None], b)
        pad_start, blk_expert, n_used = _tile_layout(counts.reshape(-1), MOE_ROWS, n_blk)
        chosen = idx8[0, 0:TOP_K, :]
        pos = jnp.sum(jnp.where(chosen[:, None, :] == experts, pad_start[:, None], 0), axis=1) + idx8[0, TOP_K:, :]
        xs = _sc_permute(h2.reshape(S, D // 2), tok, pos.reshape(-1), n_blk * MOE_ROWS, 64)
        y_sorted = _moe_rows(xs, blk_expert, n_used, w_gate_up[l], b_gate_up[l], w_down[l], b_down[l])
        yg = _sc_gather(y_sorted, pos.reshape(-1), 64)
        out = _combine_rows(yg, gates.reshape(S, TOP_K), x1.reshape(S, D), gate2[b], final_norm_g[None], out, b, B)
    return out.reshape(B, S, D)
```

```python
import functools
import math

import jax
import jax.numpy as jnp
from jax import lax
from jax.experimental import pallas as pl
from jax.experimental.pallas import tpu as pltpu
from jax.experimental.pallas import tpu_sc as plsc

F32 = jnp.float32
BF16 = jnp.bfloat16
HIGHEST = lax.Precision.HIGHEST

HGRN_DK = 128
HGRN_CHUNK = 64
ATTN_HEADS = 4
ATTN_HEAD_DIM = 64
ROT_DIM = ATTN_HEAD_DIM // 4
ROPE_THETA = 500000.0
MOBA_BLOCK = 256
MOBA_TOPK = 3
N_EXPERTS = 32
TOP_K = 4
SWIGLU_ALPHA = 1.702
SWIGLU_LIMIT = 7.0
N_MOD = 6
RMS_EPS = 1e-6

HGRN_SUB = 16
EXP_CLAMP = 80.0
MIX_ROWS = 512
MOE_ROWS = 512
MOBA_ROWS = 256
MOBA_TILES_PER_STEP = 8
PART_W = 128
V7X_VMEM_BYTES = 64 * 1024 * 1024
VMEM_LIMIT = V7X_VMEM_BYTES * 7 // 8
SC_CORES = 2
SC_SUBCORES = 16


def _sigmoid(x):
    return 1.0 / (1.0 + jnp.exp(-x))


def _dot(a, b, **kw):
    return jnp.dot(a, b, preferred_element_type=F32, **kw)


def _dot_nt(a, b, **kw):
    return lax.dot_general(a, b, (((1,), (1,)), ((), ())), preferred_element_type=F32, **kw)


def _pack_bf16_pairs(x):
    w = x.shape[1] // 2
    bits = lax.bitcast_convert_type(x.astype(BF16).astype(F32), jnp.int32)
    return bits[:, w:] | lax.shift_right_logical(bits[:, :w], 16)


def _unpack_bf16_pairs(p):
    lo = lax.bitcast_convert_type(lax.shift_left(p, 16), F32)
    hi = lax.bitcast_convert_type(p & jnp.int32(-65536), F32)
    return jnp.concatenate([lo, hi], axis=1)


def _ada_kernel(c_ref, w_ref, b_ref, o_ref):
    c = c_ref[...]
    o_ref[...] = _dot(c * _sigmoid(c), w_ref[...], precision=HIGHEST) + b_ref[...]


def _ada(c, w_ada, b_ada):
    B, D = c.shape
    N = w_ada.shape[1]
    tn = N // 4
    c8 = jnp.zeros((8, D), F32).at[:B].set(c)
    out = pl.pallas_call(
        _ada_kernel,
        out_shape=jax.ShapeDtypeStruct((8, N), F32),
        grid=(N // tn,),
        in_specs=[pl.BlockSpec((8, D), lambda j: (0, 0)),
                  pl.BlockSpec((D, tn), lambda j: (0, j)),
                  pl.BlockSpec((1, tn), lambda j: (0, j))],
        out_specs=pl.BlockSpec((8, tn), lambda j: (0, j)),
        compiler_params=pltpu.CompilerParams(vmem_limit_bytes=VMEM_LIMIT),
        name="ada",
    )(c8, w_ada, b_ada.reshape(1, N))
    return out[:B]


def _proj_kernel(x_ref, sc_ref, sh_ref, g_ref, w_ref, lb_ref, ct_ref, st_ref,
                 q_ref, k_ref, lf_ref, v_ref, gt_ref, aq_ref, ak_ref, av_ref, km_ref,
                 *, hw, aw):
    x = x_ref[...]
    ms = jnp.mean(x * x, axis=-1, keepdims=True)
    h = x * lax.rsqrt(ms + RMS_EPS) * g_ref[...]
    h = h * (1.0 + sc_ref[...]) + sh_ref[...]
    proj = _dot(h.astype(BF16), w_ref[...])

    hq = proj[:, 0:hw]
    hf = proj[:, hw:2 * hw]
    hg = proj[:, 3 * hw:4 * hw]
    q_ref[...] = hq * _sigmoid(hq) * (HGRN_DK ** -0.5)
    lb = lb_ref[...]
    f = lb + (1.0 - lb) * _sigmoid(hf)
    k_ref[...] = 1.0 - f
    lf_ref[...] = jnp.log(f)
    v_ref[...] = proj[:, 2 * hw:3 * hw].astype(BF16)
    gt_ref[...] = hg * _sigmoid(hg)

    ct = jnp.concatenate([ct_ref[...]] * (aw // 128), axis=1)
    st = jnp.concatenate([st_ref[...]] * (aw // 128), axis=1)
    lane = lax.broadcasted_iota(jnp.int32, ct.shape, 1) % ATTN_HEAD_DIM
    first_half = lane < (ROT_DIM // 2)

    def rot(t):
        partner = jnp.where(first_half, pltpu.roll(t, aw - ROT_DIM // 2, 1), pltpu.roll(t, ROT_DIM // 2, 1))
        return t * ct + partner * st

    base = 4 * hw
    aq = rot(proj[:, base:base + aw])
    ak = rot(proj[:, base + aw:base + 2 * aw])
    av = proj[:, base + 2 * aw:base + 3 * aw]
    km_ref[...] = jnp.mean(ak, axis=0, keepdims=True)
    lane128 = lax.broadcasted_iota(jnp.int32, (x.shape[0], 128), 1)
    for pair in range(ATTN_HEADS // 2):
        aq_ref[pair] = aq[:, pair * 128:(pair + 1) * 128]
    for hd in range(ATTN_HEADS):
        pair, half = divmod(hd, 2)
        in_head = (lane128 // ATTN_HEAD_DIM) == half
        ak_ref[hd] = jnp.where(in_head, ak[:, pair * 128:(pair + 1) * 128], 0.0).astype(BF16)
        av_ref[hd] = av[:, hd * ATTN_HEAD_DIM:(hd + 1) * ATTN_HEAD_DIM].astype(BF16)


def _proj(x, scale1, shift1, norm_g, w_in_bf16, lb, ctab, stab, b0):
    _, S, D = x.shape
    B = scale1.shape[0]
    hw = lb.shape[-1]
    aw = ATTN_HEADS * ATTN_HEAD_DIM
    tm = MOBA_BLOCK
    nb = S // MOBA_BLOCK
    n_proj = w_in_bf16.shape[1]
    row = lambda b, i: (b, i, 0)
    xrow = lambda b, i: (b0 + b, i, 0)
    vec = lambda b, i: (b, 0, 0)
    head = lambda b, i: (b, 0, i, 0)
    out_shapes = (
        jax.ShapeDtypeStruct((B, S, hw), F32),
        jax.ShapeDtypeStruct((B, S, hw), F32),
        jax.ShapeDtypeStruct((B, S, hw), F32),
        jax.ShapeDtypeStruct((B, S, hw), BF16),
        jax.ShapeDtypeStruct((B, S, hw), F32),
        jax.ShapeDtypeStruct((B, ATTN_HEADS // 2, S, 128), F32),
        jax.ShapeDtypeStruct((B, ATTN_HEADS, S, 128), BF16),
        jax.ShapeDtypeStruct((B, ATTN_HEADS, S, ATTN_HEAD_DIM), BF16),
        jax.ShapeDtypeStruct((B, nb, 1, aw), F32),
    )
    hspec = pl.BlockSpec((None, tm, hw), row)
    aspec = pl.BlockSpec((None, ATTN_HEADS, tm, ATTN_HEAD_DIM), head)
    return pl.pallas_call(
        functools.partial(_proj_kernel, hw=hw, aw=aw),
        out_shape=out_shapes,
        grid=(B, S // tm),
        in_specs=[pl.BlockSpec((None, tm, D), xrow),
                  pl.BlockSpec((None, 1, D), vec),
                  pl.BlockSpec((None, 1, D), vec),
                  pl.BlockSpec((1, D), lambda b, i: (0, 0)),
                  pl.BlockSpec((D, n_proj), lambda b, i: (0, 0)),
                  pl.BlockSpec((1, hw), lambda b, i: (0, 0)),
                  pl.BlockSpec((None, tm, 128), xrow),
                  pl.BlockSpec((None, tm, 128), xrow)],
        out_specs=(hspec, hspec, hspec, hspec, hspec,
                   pl.BlockSpec((None, ATTN_HEADS // 2, tm, 128), head),
                   pl.BlockSpec((None, ATTN_HEADS, tm, 128), head), aspec,
                   pl.BlockSpec((None, None, 1, aw), lambda b, i: (b, i, 0, 0))),
        compiler_params=pltpu.CompilerParams(
            dimension_semantics=("arbitrary", "arbitrary"), vmem_limit_bytes=VMEM_LIMIT),
        name="proj",
    )(x, scale1, shift1, norm_g, w_in_bf16, lb, ctab, stab)


def _hgrn_kernel(q_ref, k_ref, lf_ref, v_ref, gt_ref, gn_ref, o_ref, st_ref, *, n_heads, n_chunks):
    @pl.when(pl.program_id(1) == 0)
    def _():
        st_ref[...] = jnp.zeros_like(st_ref)

    C = HGRN_CHUNK
    r = lax.broadcasted_iota(jnp.int32, (C, C), 0)
    c = lax.broadcasted_iota(jnp.int32, (C, C), 1)
    tril = c <= r
    ltri = tril.astype(F32)
    gn = gn_ref[...]

    def chunk(ci, carry):
        r0 = pl.multiple_of(ci * C, C)
        rows = pl.ds(r0, C)
        b_all = _dot(ltri, lf_ref[rows, :], precision=HIGHEST)
        heads = range(n_heads)
        sls = [slice(hd * HGRN_DK, (hd + 1) * HGRN_DK) for hd in heads]
        bs = [b_all[:, sl] for sl in sls]
        b_lasts = [b[C - 1:C, :] for b in bs]
        qs = [q_ref[rows, sl] for sl in sls]
        ks = [k_ref[rows, sl] for sl in sls]
        vs = [v_ref[rows, sl] for sl in sls]
        states = [st_ref[hd] for hd in heads]
        o_inter = [_dot_nt((qs[hd] * jnp.exp(bs[hd])).astype(BF16), states[hd].astype(BF16)) for hd in heads]
        scores = []
        for hd in heads:
            blocks = []
            for g0 in range(0, C, HGRN_SUB):
                g1 = g0 + HGRN_SUB
                rho = 0.5 * (bs[hd][g0:g0 + 1, :] + bs[hd][g1 - 1:g1, :])
                qa = qs[hd][g0:g1, :] * jnp.exp(jnp.minimum(bs[hd][g0:g1, :] - rho, EXP_CLAMP))
                kb = ks[hd] * jnp.exp(jnp.minimum(rho - bs[hd], EXP_CLAMP))
                blocks.append(_dot_nt(qa.astype(BF16), kb.astype(BF16)))
            scores.append(jnp.where(tril, jnp.concatenate(blocks, axis=0), 0.0).astype(BF16))
        outs = [o_inter[hd] + _dot(scores[hd], vs[hd]) for hd in heads]
        kds = [(ks[hd] * jnp.exp(b_lasts[hd] - bs[hd])).astype(BF16) for hd in heads]
        upds = [_dot(vs[hd].astype(F32).T.astype(BF16), kds[hd]) for hd in heads]
        for hd in heads:
            st_ref[hd] = states[hd] * jnp.exp(b_lasts[hd]) + upds[hd]
            o = outs[hd]
            ms = jnp.mean(o * o, axis=-1, keepdims=True)
            o_ref[rows, sls[hd]] = (o * lax.rsqrt(ms + RMS_EPS) * gn * gt_ref[rows, sls[hd]]).astype(BF16)
        return carry

    lax.fori_loop(0, n_chunks, chunk, 0, unroll=True)


def _hgrn(q, k, lf, v, gt, norm_g):
    B, S, hw = q.shape
    n_heads = hw // HGRN_DK
    tc = 512
    spec = pl.BlockSpec((None, tc, hw), lambda b, i: (b, i, 0))
    return pl.pallas_call(
        functools.partial(_hgrn_kernel, n_heads=n_heads, n_chunks=tc // HGRN_CHUNK),
        out_shape=jax.ShapeDtypeStruct((B, S, hw), BF16),
        grid=(B, S // tc),
        in_specs=[spec, spec, spec, spec, spec, pl.BlockSpec((1, HGRN_DK), lambda b, i: (0, 0))],
        out_specs=spec,
        scratch_shapes=[pltpu.VMEM((n_heads, HGRN_DK, HGRN_DK), F32)],
        compiler_params=pltpu.CompilerParams(
            dimension_semantics=("arbitrary", "arbitrary"), vmem_limit_bytes=VMEM_LIMIT),
        name="hgrn",
    )(q, k, lf, v, gt, norm_g)


def _sc_move_rows(table, src, dst, n_out, chunk):
    M = src.shape[0]
    D = table.shape[1]
    n_workers = SC_CORES * SC_SUBCORES
    per_worker = M // n_workers
    n_chunks = per_worker // chunk
    assert per_worker * n_workers == M and n_chunks * chunk == per_worker and n_chunks % 2 == 0 and chunk % 8 == 0
    mesh = plsc.VectorSubcoreMesh(core_axis_name="c", subcore_axis_name="s")
    idx_t = pltpu.VMEM((chunk,), jnp.int32)
    row_t = pltpu.VMEM((chunk, D), table.dtype)
    sem_t = pltpu.SemaphoreType.DMA

    def body(table_hbm, src_hbm, dst_hbm, out_hbm, src_v, dst_v, rows_v, g_sem, s_sem):
        wid = lax.axis_index("s") * SC_CORES + lax.axis_index("c")
        base = wid * per_worker

        def offset(j):
            return pl.multiple_of(base + j * chunk, 8)

        def gather(b):
            return pltpu.make_async_copy(table_hbm.at[src_v[b]], rows_v[b], g_sem[b])

        def start_gather(j, b):
            pltpu.sync_copy(src_hbm.at[pl.ds(offset(j), chunk)], src_v[b])
            gather(b).start()

        def write_out(j, b):
            if dst_hbm is None:
                pltpu.sync_copy(rows_v[b], out_hbm.at[pl.ds(offset(j), chunk)])
            else:
                pltpu.sync_copy(dst_hbm.at[pl.ds(offset(j), chunk)], dst_v[b])
                pltpu.async_copy(rows_v[b], out_hbm.at[dst_v[b]], s_sem[b]).wait()

        start_gather(0, 0)

        @pl.loop(0, n_chunks, step=2)
        def _(j):
            for b in (0, 1):
                @pl.when(j + b + 1 < n_chunks)
                def _():
                    start_gather(j + b + 1, 1 - b)
                gather(b).wait()
                write_out(j + b, b)

    if dst is None:
        @functools.partial(pl.kernel, mesh=mesh, out_type=jax.ShapeDtypeStruct((n_out, D), table.dtype),
                           scratch_types=[idx_t, idx_t, row_t, row_t, sem_t, sem_t])
        def gather_kernel(table_hbm, src_hbm, out_hbm, s0, s1, r0, r1, g0, g1):
            body(table_hbm, src_hbm, None, out_hbm, (s0, s1), None, (r0, r1), (g0, g1), None)
        return gather_kernel(table, src)

    @functools.partial(pl.kernel, mesh=mesh, out_type=jax.ShapeDtypeStruct((n_out, D), table.dtype),
                       scratch_types=[idx_t, idx_t, idx_t, idx_t, row_t, row_t, sem_t, sem_t, sem_t, sem_t])
    def permute_kernel(table_hbm, src_hbm, dst_hbm, out_hbm, s0, s1, d0, d1, r0, r1, g0, g1, w0, w1):
        body(table_hbm, src_hbm, dst_hbm, out_hbm, (s0, s1), (d0, d1), (r0, r1), (g0, g1), (w0, w1))
    return permute_kernel(table, src, dst)


def _sc_gather(table, idx, chunk):
    return _sc_move_rows(table, idx, None, idx.shape[0], chunk)


def _sc_permute(table, src, dst, n_out, chunk):
    return _sc_move_rows(table, src, dst, n_out, chunk)


def _tile_layout(counts, bm, n_tiles):
    n_groups = counts.shape[0]
    padded = (counts + bm - 1) // bm * bm
    pad_end = jnp.cumsum(padded)
    tile_start = jnp.arange(n_tiles, dtype=jnp.int32) * bm
    tile_group = jnp.minimum(
        jnp.sum((pad_end[None, :] <= tile_start[:, None]).astype(jnp.int32), axis=1), n_groups - 1)
    n_used = (pad_end[-1] // bm).astype(jnp.int32).reshape(1)
    return pad_end - padded, tile_group.astype(jnp.int32), n_used


def _null_partial(rows):
    lane = lax.broadcasted_iota(jnp.int32, (rows, PART_W), 1)
    return jnp.where(lane < ATTN_HEAD_DIM, 0.0, -jnp.inf).astype(F32)


def _moba_sel_kernel(q_ref, km_ref, k_ref, v_ref, idx_ref, cnt_ref, own_ref, cnt_acc, *, n_blocks):
    j = pl.program_id(1)
    T = MOBA_BLOCK
    heads = range(ATTN_HEADS)
    qs = [q_ref[hd // 2] for hd in heads]
    gates = [_dot_nt(km_ref[hd], qs[hd], precision=HIGHEST) for hd in heads]
    blk = lax.broadcasted_iota(jnp.int32, gates[0].shape, 0)
    neg_inf = jnp.float32(-jnp.inf)
    gates = [jnp.where(blk < j, g, neg_inf) for g in gates]
    picks = [[] for _ in heads]
    for _ in range(MOBA_TOPK):
        ms = [jnp.max(g, axis=0, keepdims=True) for g in gates]
        firsts = [jnp.min(jnp.where(g == m, blk, n_blocks), axis=0, keepdims=True) for g, m in zip(gates, ms)]
        for hd in heads:
            picks[hd].append(jnp.where(ms[hd] > neg_inf, firsts[hd], -1))
        gates = [jnp.where(blk == f, neg_inf, g) for g, f in zip(gates, firsts)]

    @pl.when(j == 0)
    def _():
        cnt_acc[...] = jnp.zeros_like(cnt_acc)

    earlier = (lax.broadcasted_iota(jnp.int32, (T, T), 0) < lax.broadcasted_iota(jnp.int32, (T, T), 1)).astype(BF16)
    for hd in heads:
        onehots = [(blk == p).astype(F32) for p in picks[hd]]
        member = onehots[0] + onehots[1] + onehots[2]
        base = cnt_acc[hd] + _dot(member.astype(BF16), earlier)
        ranks = [jnp.sum(oh * base, axis=0, keepdims=True).astype(jnp.int32) for oh in onehots]
        idx_ref[hd] = jnp.concatenate(picks[hd] + ranks + [jnp.zeros((2, T), jnp.int32)], axis=0)
        total = cnt_acc[hd] + jnp.sum(member, axis=1, keepdims=True)
        cnt_acc[hd] = total
        cnt_ref[hd] = total.astype(jnp.int32)
    causal = lax.broadcasted_iota(jnp.int32, (T, T), 1) <= lax.broadcasted_iota(jnp.int32, (T, T), 0)
    scale = ATTN_HEAD_DIM ** -0.5
    ss = [jnp.where(causal, _dot_nt((qs[hd] * scale).astype(BF16), k_ref[hd]), neg_inf) for hd in heads]
    mx = [jnp.max(s, axis=1, keepdims=True) for s in ss]
    ps = [jnp.exp(s - m) for s, m in zip(ss, mx)]
    ls = [jnp.sum(p, axis=1, keepdims=True) for p in ps]
    accs = [_dot(ps[hd].astype(BF16), v_ref[hd]) for hd in heads]
    for hd in heads:
        lse = jnp.broadcast_to(mx[hd] + jnp.log(ls[hd]), (T, PART_W - ATTN_HEAD_DIM))
        own_ref[hd] = jnp.concatenate([accs[hd] / ls[hd], lse], axis=1)


def _moba_sel(aq, kmean, ak, av):
    B, H, S, hd = av.shape
    nb = S // MOBA_BLOCK
    T = MOBA_BLOCK
    blk = lambda b, j: (b, 0, j, 0)
    return pl.pallas_call(
        functools.partial(_moba_sel_kernel, n_blocks=nb),
        out_shape=(jax.ShapeDtypeStruct((B, H, 8, S), jnp.int32),
                   jax.ShapeDtypeStruct((B, H, nb, 1), jnp.int32),
                   jax.ShapeDtypeStruct((B, H, S, PART_W), F32)),
        grid=(B, nb),
        in_specs=[pl.BlockSpec((None, H // 2, T, 128), blk),
                  pl.BlockSpec((None, H, nb, 128), lambda b, j: (b, 0, 0, 0)),
                  pl.BlockSpec((None, H, T, 128), blk),
                  pl.BlockSpec((None, H, T, hd), blk)],
        out_specs=(pl.BlockSpec((None, H, 8, T), lambda b, j: (b, 0, 0, j)),
                   pl.BlockSpec((None, H, nb, 1), lambda b, j: (b, 0, 0, 0)),
                   pl.BlockSpec((None, H, T, PART_W), blk)),
        scratch_shapes=[pltpu.VMEM((H, nb, 1), F32)],
        compiler_params=pltpu.CompilerParams(
            dimension_semantics=("arbitrary", "arbitrary"), vmem_limit_bytes=VMEM_LIMIT),
        name="moba_sel",
    )(aq, kmean, ak, av)


def _moba_blk_kernel(tg_ref, nu_ref, q_ref, *refs):
    n = MOBA_TILES_PER_STEP
    k_refs, v_refs, o_ref = refs[:n], refs[n:2 * n], refs[2 * n]
    R = MOBA_ROWS
    t0 = pl.program_id(0) * n

    @pl.when(t0 < nu_ref[0])
    def _():
        scale = ATTN_HEAD_DIM ** -0.5
        ss = [_dot_nt((q_ref[j * R:(j + 1) * R, :] * scale).astype(BF16), k_refs[j][...]) for j in range(n)]
        ms = [jnp.max(s, axis=1, keepdims=True) for s in ss]
        ps = [jnp.exp(s - m) for s, m in zip(ss, ms)]
        ls = [jnp.sum(p, axis=1, keepdims=True) for p in ps]
        accs = [_dot(p.astype(BF16), v_refs[j][...]) for j, p in enumerate(ps)]
        null = _null_partial(R)
        for j in range(n):
            lse = jnp.broadcast_to(ms[j] + jnp.log(ls[j]), (R, PART_W - ATTN_HEAD_DIM))
            row = jnp.concatenate([accs[j] / ls[j], lse], axis=1)
            o_ref[j * R:(j + 1) * R, :] = jnp.where(t0 + j < nu_ref[0], row, null)

    @pl.when(t0 >= nu_ref[0])
    def _():
        o_ref[...] = _null_partial(n * R)


def _moba_blk(qs, tile_group, n_used, ak, av):
    B, H, S, hd = av.shape
    nb = S // MOBA_BLOCK
    R = MOBA_ROWS
    n = MOBA_TILES_PER_STEP
    n_tiles = qs.shape[0] // R
    assert n_tiles % n == 0
    kv = lambda j: (lambda i, tg, nu: (tg[i * n + j] // nb, tg[i * n + j] % nb, 0, 0))
    grid_spec = pltpu.PrefetchScalarGridSpec(
        num_scalar_prefetch=2,
        grid=(n_tiles // n,),
        in_specs=[pl.BlockSpec((n * R, 128), lambda i, tg, nu: (i, 0))]
        + [pl.BlockSpec((None, None, MOBA_BLOCK, 128), kv(j)) for j in range(n)]
        + [pl.BlockSpec((None, None, MOBA_BLOCK, hd), kv(j)) for j in range(n)],
        out_specs=pl.BlockSpec((n * R, PART_W), lambda i, tg, nu: (i, 0)),
    )
    k4 = ak.reshape(B * H, nb, MOBA_BLOCK, 128)
    v4 = av.reshape(B * H, nb, MOBA_BLOCK, hd)
    return pl.pallas_call(
        _moba_blk_kernel,
        out_shape=jax.ShapeDtypeStruct((n_tiles * R, PART_W), F32),
        grid_spec=grid_spec,
        compiler_params=pltpu.CompilerParams(
            dimension_semantics=("arbitrary",), vmem_limit_bytes=VMEM_LIMIT),
        name="moba_blk",
    )(tile_group, n_used, qs, *([k4] * n), *([v4] * n))


def _moba_merge_kernel(own_ref, pg_ref, g_ref, o_ref):
    hd = ATTN_HEAD_DIM
    rows = [own_ref[...]] + [pg_ref[s] for s in range(MOBA_TOPK)]
    lses = [pltpu.roll(r, hd, 1) for r in rows]
    top = lses[0]
    for z in lses[1:]:
        top = jnp.maximum(top, z)
    num = jnp.zeros_like(top)
    den = jnp.zeros_like(top)
    for r, z in zip(rows, lses):
        w = jnp.exp(z - top)
        num = num + w * r
        den = den + w
    o = (num / den)[:, :hd]
    ms = jnp.mean(o * o, axis=-1, keepdims=True)
    o_ref[...] = o * lax.rsqrt(ms + RMS_EPS) * g_ref[...]


def _moba_merge(own, pg, norm_g):
    n = own.shape[0]
    T = 2048
    row = lambda i: (i, 0)
    return pl.pallas_call(
        _moba_merge_kernel,
        out_shape=jax.ShapeDtypeStruct((n, ATTN_HEAD_DIM), F32),
        grid=(n // T,),
        in_specs=[pl.BlockSpec((T, PART_W), row),
                  pl.BlockSpec((MOBA_TOPK, T, PART_W), lambda i: (0, i, 0)),
                  pl.BlockSpec((1, ATTN_HEAD_DIM), lambda i: (0, 0))],
        out_specs=pl.BlockSpec((T, ATTN_HEAD_DIM), row),
        compiler_params=pltpu.CompilerParams(
            dimension_semantics=("arbitrary",), vmem_limit_bytes=VMEM_LIMIT),
        name="moba_merge",
    )(own, pg, norm_g)


def _moba(aq, km, ak, av, norm_g):
    B, H, S, hd = av.shape
    nb = S // MOBA_BLOCK
    n_q = B * H * S
    kmp = km.reshape(B, nb, H // 2, 128)
    half = jnp.arange(128, dtype=jnp.int32) // hd
    kmean = jnp.stack([jnp.where(half == h % 2, kmp[:, :, h // 2, :], 0.0) for h in range(H)], axis=1)
    idx8, counts, own = _moba_sel(aq, kmean, ak, av)
    sel = idx8[:, :, 0:MOBA_TOPK, :].reshape(B * H, MOBA_TOPK, S)
    rank = idx8[:, :, MOBA_TOPK:2 * MOBA_TOPK, :].reshape(B * H, MOBA_TOPK, S)
    n_groups = B * H * nb
    n_tiles = (n_q * MOBA_TOPK) // MOBA_ROWS + n_groups
    pad_start, tile_group, n_used = _tile_layout(counts.reshape(-1), MOBA_ROWS, n_tiles)
    blocks = jnp.arange(nb, dtype=jnp.int32)[:, None]
    start = jnp.sum(jnp.where(sel[:, :, None, :] == blocks, pad_start.reshape(B * H, 1, nb, 1), 0), axis=2)
    a_ids = jnp.arange(n_q * MOBA_TOPK, dtype=jnp.int32).reshape(B * H, MOBA_TOPK, S)
    assert n_tiles * MOBA_ROWS >= n_q * MOBA_TOPK + MOBA_ROWS
    pos = jnp.where(sel >= 0, start + rank, n_used[0] * MOBA_ROWS + a_ids % MOBA_ROWS)
    bh = jnp.arange(B * H, dtype=jnp.int32)[:, None, None]
    t = jnp.arange(S, dtype=jnp.int32)[None, None, :]
    pair_row = jnp.broadcast_to((bh // H * (H // 2) + bh % H // 2) * S + t, pos.shape)
    qs = _sc_permute(aq.reshape(B * (H // 2) * S, 128), pair_row.reshape(-1), pos.reshape(-1),
                     n_tiles * MOBA_ROWS, 256)
    parts = _moba_blk(qs, tile_group, n_used, ak, av)
    pg = _sc_gather(parts, pos.transpose(1, 0, 2).reshape(-1), 256)
    o = _moba_merge(own.reshape(n_q, PART_W), pg.reshape(MOBA_TOPK, n_q, PART_W), norm_g)
    return o.reshape(B, H, S, hd)


def _mix_kernel(oh_ref, oa_ref, x_ref, g1_ref, sc2_ref, sh2_ref, n2_ref, wo_ref, wr_ref, br_ref,
                x1_ref, h2_ref, gw_ref, idx_ref, cnt_ref, cnt_acc):
    cat = jnp.concatenate([oh_ref[...]] + [oa_ref[hd] for hd in range(ATTN_HEADS)], axis=1)
    mix = _dot(cat.astype(BF16), wo_ref[...])
    x1 = x_ref[...] + g1_ref[...] * mix
    x1_ref[...] = x1
    ms = jnp.mean(x1 * x1, axis=-1, keepdims=True)
    h2 = x1 * lax.rsqrt(ms + RMS_EPS) * n2_ref[...]
    h2 = h2 * (1.0 + sc2_ref[...]) + sh2_ref[...]
    h2_ref[...] = _pack_bf16_pairs(h2)
    E = N_EXPERTS
    tm = h2.shape[0]
    h_0 = h2.astype(BF16)
    r_1 = h2 - h_0.astype(F32)
    h_1 = r_1.astype(BF16)
    h_2 = (r_1 - h_1.astype(F32)).astype(BF16)
    wt = wr_ref[...]
    p_0 = _dot_nt(wt, h_0)
    p_1 = _dot_nt(wt[:2 * E], h_1)
    p_2 = _dot_nt(wt[:E], h_2)
    logits = (p_0[:E] + (p_0[E:2 * E] + p_1[:E]) + (p_0[2 * E:] + p_1[E:] + p_2)) + br_ref[...]
    ex = lax.broadcasted_iota(jnp.int32, logits.shape, 0)
    neg_inf = jnp.float32(-jnp.inf)
    vals, idxs = [], []
    for _ in range(TOP_K):
        m = jnp.max(logits, axis=0, keepdims=True)
        first = jnp.min(jnp.where(logits == m, ex, E), axis=0, keepdims=True)
        vals.append(m)
        idxs.append(first)
        logits = jnp.where(ex == first, neg_inf, logits)
    e = [jnp.exp(v - vals[0]) for v in vals]
    denom = e[0] + e[1] + e[2] + e[3]
    gate_rows = jnp.concatenate([ei / denom for ei in e] + [jnp.zeros((128 - TOP_K, tm), F32)], axis=0)
    gw_ref[...] = gate_rows.T[:, :TOP_K]

    @pl.when((pl.program_id(0) == 0) & (pl.program_id(1) == 0))
    def _():
        cnt_acc[...] = jnp.zeros_like(cnt_acc)

    earlier = (lax.broadcasted_iota(jnp.int32, (tm, tm), 0) < lax.broadcasted_iota(jnp.int32, (tm, tm), 1)).astype(BF16)
    onehots = [(ex == ix).astype(F32) for ix in idxs]
    member = onehots[0] + onehots[1] + onehots[2] + onehots[3]
    base = cnt_acc[...] + _dot(member.astype(BF16), earlier)
    ranks = [jnp.sum(oh * base, axis=0, keepdims=True).astype(jnp.int32) for oh in onehots]
    idx_ref[...] = jnp.concatenate(idxs + ranks, axis=0)
    total = cnt_acc[...] + jnp.sum(member, axis=1, keepdims=True)
    cnt_acc[...] = total
    cnt_ref[...] = total.astype(jnp.int32)


def _mix(oh, oa, x, gate1, scale2, shift2, norm2_g, w_out_bf16, w_router, b_router, b0):
    _, S, D = x.shape
    B = oh.shape[0]
    hw = oh.shape[-1]
    tm = MIX_ROWS
    row = lambda b, i: (b, i, 0)
    xrow = lambda b, i: (b0 + b, i, 0)
    vec = lambda b, i: (b, 0, 0)
    const = lambda b, i: (0, 0)
    return pl.pallas_call(
        _mix_kernel,
        out_shape=(jax.ShapeDtypeStruct((B, S, D), F32),
                   jax.ShapeDtypeStruct((B, S, D // 2), jnp.int32),
                   jax.ShapeDtypeStruct((B, S, TOP_K), F32),
                   jax.ShapeDtypeStruct((B, 2 * TOP_K, S), jnp.int32),
                   jax.ShapeDtypeStruct((N_EXPERTS, 1), jnp.int32)),
        grid=(B, S // tm),
        in_specs=[pl.BlockSpec((None, tm, hw), row),
                  pl.BlockSpec((None, ATTN_HEADS, tm, ATTN_HEAD_DIM), lambda b, i: (b, 0, i, 0)),
                  pl.BlockSpec((None, tm, D), xrow),
                  pl.BlockSpec((None, 1, D), vec),
                  pl.BlockSpec((None, 1, D), vec),
                  pl.BlockSpec((None, 1, D), vec),
                  pl.BlockSpec((1, D), const),
                  pl.BlockSpec((D, D), const),
                  pl.BlockSpec((3 * N_EXPERTS, D), const),
                  pl.BlockSpec((N_EXPERTS, 1), const)],
        out_specs=(pl.BlockSpec((None, tm, D), row),
                   pl.BlockSpec((None, tm, D // 2), row),
                   pl.BlockSpec((None, tm, TOP_K), row),
                   pl.BlockSpec((None, 2 * TOP_K, tm), lambda b, i: (b, 0, i)),
                   pl.BlockSpec((N_EXPERTS, 1), const)),
        scratch_shapes=[pltpu.VMEM((N_EXPERTS, 1), F32)],
        compiler_params=pltpu.CompilerParams(
            dimension_semantics=("arbitrary", "arbitrary"), vmem_limit_bytes=VMEM_LIMIT),
        name="mix",
    )(oh, oa, x, gate1, scale2, shift2, norm2_g, w_out_bf16, w_router, b_router)


def _moe_rows_kernel(be_ref, nx_ref, nu_ref, x_ref, wgu_hbm, bgu_ref, wd_hbm, bd_ref, y_ref,
                     wgu32, wd32, wgu16, wd16, sem, *, d_ff):
    i = pl.program_id(0)
    e = be_ref[i]

    def fetch(expert):
        return (pltpu.make_async_copy(wgu_hbm.at[expert], wgu32, sem.at[0]),
                pltpu.make_async_copy(wd_hbm.at[expert], wd32, sem.at[1]))

    @pl.when(i == 0)
    def _():
        for c in fetch(e):
            c.start()

    @pl.when((i == 0) | (e != be_ref[jnp.maximum(i - 1, 0)]))
    def _():
        for c in fetch(e):
            c.wait()
        wgu16[...] = wgu32[...].astype(BF16)
        wd16[...] = wd32[...].astype(BF16)

        @pl.when(nx_ref[i] >= 0)
        def _():
            for c in fetch(nx_ref[i]):
                c.start()

    @pl.when(i < nu_ref[0])
    def _():
        gu = _dot(_unpack_bf16_pairs(x_ref[...]).astype(BF16), wgu16[...]) + bgu_ref[...]
        gate = jnp.minimum(gu[:, :d_ff], SWIGLU_LIMIT)
        up = jnp.clip(gu[:, d_ff:], -SWIGLU_LIMIT, SWIGLU_LIMIT)
        act = (up + 1.0) * gate * _sigmoid(SWIGLU_ALPHA * gate)
        y_ref[...] = _pack_bf16_pairs(_dot(act.astype(BF16), wd16[...]) + bd_ref[...])

    @pl.when(i >= nu_ref[0])
    def _():
        y_ref[...] = jnp.zeros_like(y_ref)


def _moe_rows(xs, blk_expert, n_used, wgu, bgu, wd, bd):
    D = 2 * xs.shape[1]
    bm = MOE_ROWS
    n_blk = xs.shape[0] // bm
    d_ff = wd.shape[1]
    run_end = jnp.sum((blk_expert[None, :] <= blk_expert[:, None]).astype(jnp.int32), axis=1)
    next_expert = jnp.where(run_end < n_blk, blk_expert[jnp.minimum(run_end, n_blk - 1)], -1).astype(jnp.int32)
    bsel = lambda i, be, nx, nu: (be[i], 0, 0)
    rows = lambda i, be, nx, nu: (i, 0)
    grid_spec = pltpu.PrefetchScalarGridSpec(
        num_scalar_prefetch=3,
        grid=(n_blk,),
        in_specs=[pl.BlockSpec((bm, D // 2), rows),
                  pl.BlockSpec(memory_space=pl.ANY),
                  pl.BlockSpec((None, 1, 2 * d_ff), bsel),
                  pl.BlockSpec(memory_space=pl.ANY),
                  pl.BlockSpec((None, 1, D), bsel)],
        out_specs=pl.BlockSpec((bm, D // 2), rows),
        scratch_shapes=[pltpu.VMEM((D, 2 * d_ff), F32), pltpu.VMEM((d_ff, D), F32),
                        pltpu.VMEM((D, 2 * d_ff), BF16), pltpu.VMEM((d_ff, D), BF16),
                        pltpu.SemaphoreType.DMA((2,))],
    )
    return pl.pallas_call(
        functools.partial(_moe_rows_kernel, d_ff=d_ff),
        out_shape=jax.ShapeDtypeStruct((n_blk * bm, D // 2), jnp.int32),
        grid_spec=grid_spec,
        compiler_params=pltpu.CompilerParams(
            dimension_semantics=("arbitrary",), vmem_limit_bytes=VMEM_LIMIT),
        name="moe_rows",
    )(blk_expert, next_expert, n_used, xs, wgu, bgu.reshape(N_EXPERTS, 1, 2 * d_ff), wd, bd.reshape(N_EXPERTS, 1, D))


def _combine_rows_kernel(*refs):
    y_refs = refs[:TOP_K]
    gw_ref, x1_ref, g2_ref, fg_ref = refs[TOP_K:TOP_K + 4]
    o_ref = refs[-1]
    gw = gw_ref[...]
    y = gw[:, 0:1] * _unpack_bf16_pairs(y_refs[0][...])
    for kk in range(1, TOP_K):
        y = y + gw[:, kk:kk + 1] * _unpack_bf16_pairs(y_refs[kk][...])
    x2 = x1_ref[...] + g2_ref[...] * y
    ms = jnp.mean(x2 * x2, axis=-1, keepdims=True)
    o_ref[...] = x2 * lax.rsqrt(ms + RMS_EPS) * fg_ref[...]


def _combine_rows(yg, gates, x1, gate2, final_g, out_so_far, b0, n_batches):
    S, D = x1.shape
    tm = 512
    steps = S // tm
    slot_spec = lambda kk: pl.BlockSpec((tm, D // 2), lambda i: (kk * steps + i, 0))
    in_specs = [slot_spec(kk) for kk in range(TOP_K)] + [
        pl.BlockSpec((tm, TOP_K), lambda i: (i, 0)),
        pl.BlockSpec((tm, D), lambda i: (i, 0)),
        pl.BlockSpec((1, D), lambda i: (0, 0)),
        pl.BlockSpec((1, D), lambda i: (0, 0))]
    args = [yg] * TOP_K + [gates, x1, gate2, final_g]
    aliases = {}
    if out_so_far is not None:
        in_specs.append(pl.BlockSpec(memory_space=pl.ANY))
        aliases = {len(args): 0}
        args.append(out_so_far)
    return pl.pallas_call(
        _combine_rows_kernel,
        out_shape=jax.ShapeDtypeStruct((n_batches * S, D), F32),
        grid=(steps,),
        in_specs=in_specs,
        out_specs=pl.BlockSpec((tm, D), lambda i: (b0 * steps + i, 0)),
        input_output_aliases=aliases,
        compiler_params=pltpu.CompilerParams(
            dimension_semantics=("arbitrary",), vmem_limit_bytes=VMEM_LIMIT),
        name="combine_rows",
    )(*args)


def _split_bf16x3(w):
    def top(v):
        return lax.bitcast_convert_type(lax.bitcast_convert_type(v, jnp.int32) & jnp.int32(-65536), F32)
    w0 = top(w)
    w1 = top(w - w0)
    w2 = w - w0 - w1
    return jnp.concatenate([w0, w1, w2], axis=1).astype(BF16).T


def _rotary_tables(positions):
    half = ROT_DIM // 2
    inv_freq = jnp.exp(-math.log(ROPE_THETA) * jnp.arange(0, ROT_DIM, 2, dtype=F32) / ROT_DIM)
    d = jnp.arange(128, dtype=jnp.int32) % ATTN_HEAD_DIM
    freq = jnp.where(d < ROT_DIM, inv_freq[d % half], 0.0)
    sign = jnp.where(d < half, -1.0, 1.0)
    ang = positions.astype(F32)[:, :, None] * freq
    return jnp.cos(ang), jnp.sin(ang) * sign


def kernel(x, c, positions, w_ada, b_ada, norm1_g, w_in, hgrn_lb_logits, hgrn_norm_g, attn_norm_g,
           w_out, norm2_g, w_router, b_router, w_gate_up, b_gate_up, w_down, b_down, final_norm_g):
    B, S, D = x.shape
    assert w_in.shape[0] == 1, "single-layer block: the final norm is fused into the combine step"
    l = 0
    ctab, stab = _rotary_tables(positions)
    lower_bounds = jnp.cumsum(jax.nn.softmax(hgrn_lb_logits.astype(F32), axis=0), axis=0)
    mod = _ada(c, w_ada[l], b_ada[l])
    shift1, scale1, gate1, shift2, scale2, gate2 = jnp.split(mod[:, None, :], N_MOD, axis=-1)
    n_blk = (S * TOP_K) // MOE_ROWS + N_EXPERTS
    experts = jnp.arange(N_EXPERTS, dtype=jnp.int32)[:, None]
    tok = jnp.broadcast_to(jnp.arange(S, dtype=jnp.int32)[None, :], (TOP_K, S)).reshape(-1)
    out = None
    for b in range(B):
        one = slice(b, b + 1)
        q, k, lf, v, gt, aq, ak, av, km = _proj(
            x, scale1[one], shift1[one], norm1_g[l][None], w_in[l].astype(BF16), lower_bounds[l][None],
            ctab, stab, b)
        o_a = _moba(aq, km, ak, av, attn_norm_g[l][None])
        o_h = _hgrn(q, k, lf, v, gt, hgrn_norm_g[l][None])
        x1, h2, gates, idx8, counts = _mix(
            o_h, o_a, x, gate1[one], scale2[one], shift2[one], norm2_g[l][None], w_out[l].astype(BF16),
            _split_bf16x3(w_router[l]), b_router[l][:, None], b)
        pad_start, blk_expert, n_used = _tile_layout(counts.reshape(-1), MOE_ROWS, n_blk)
        chosen = idx8[0, 0:TOP_K, :]
        pos = jnp.sum(jnp.where(chosen[:, None, :] == experts, pad_start[:, None], 0), axis=1) + idx8[0, TOP_K:, :]
        xs = _sc_permute(h2.reshape(S, D // 2), tok, pos.reshape(-1), n_blk * MOE_ROWS, 64)
        y_sorted = _moe_rows(xs, blk_expert, n_used, w_gate_up[l], b_gate_up[l], w_down[l], b_down[l])
        yg = _sc_gather(y_sorted, pos.reshape(-1), 64)
        out = _combine_rows(yg, gates.reshape(S, TOP_K), x1.reshape(S, D), gate2[b], final_norm_g[None], out, b, B)
    return out.reshape(B, S, D)
```

```python
import functools
import math

import jax
import jax.numpy as jnp
from jax import lax
from jax.experimental import pallas as pl
from jax.experimental.pallas import tpu as pltpu
from jax.experimental.pallas import tpu_sc as plsc

F32 = jnp.float32
BF16 = jnp.bfloat16
HIGHEST = lax.Precision.HIGHEST

HGRN_DK = 128
HGRN_CHUNK = 64
ATTN_HEADS = 4
ATTN_HEAD_DIM = 64
ROT_DIM = ATTN_HEAD_DIM // 4
ROPE_THETA = 500000.0
MOBA_BLOCK = 256
MOBA_TOPK = 3
N_EXPERTS = 32
TOP_K = 4
SWIGLU_ALPHA = 1.702
SWIGLU_LIMIT = 7.0
N_MOD = 6
RMS_EPS = 1e-6

HGRN_SUB = 16
EXP_CLAMP = 80.0
MIX_ROWS = 512
MOE_ROWS = 512
MOBA_ROWS = 256
MOBA_TILES_PER_STEP = 8
PART_W = 128
V7X_VMEM_BYTES = 64 * 1024 * 1024
VMEM_LIMIT = V7X_VMEM_BYTES * 7 // 8
SC_CORES = 2
SC_SUBCORES = 16


def _sigmoid(x):
    return 1.0 / (1.0 + jnp.exp(-x))


def _dot(a, b, **kw):
    return jnp.dot(a, b, preferred_element_type=F32, **kw)


def _dot_nt(a, b, **kw):
    return lax.dot_general(a, b, (((1,), (1,)), ((), ())), preferred_element_type=F32, **kw)


def _pack_bf16_pairs(x):
    w = x.shape[1] // 2
    bits = lax.bitcast_convert_type(x.astype(BF16).astype(F32), jnp.int32)
    return bits[:, w:] | lax.shift_right_logical(bits[:, :w], 16)


def _unpack_bf16_pairs(p):
    lo = lax.bitcast_convert_type(lax.shift_left(p, 16), F32)
    hi = lax.bitcast_convert_type(p & jnp.int32(-65536), F32)
    return jnp.concatenate([lo, hi], axis=1)


def _ada_kernel(c_ref, w_ref, b_ref, o_ref):
    c = c_ref[...]
    o_ref[...] = _dot(c * _sigmoid(c), w_ref[...], precision=HIGHEST) + b_ref[...]


def _ada(c, w_ada, b_ada):
    B, D = c.shape
    N = w_ada.shape[1]
    tn = N // 4
    c8 = jnp.zeros((8, D), F32).at[:B].set(c)
    out = pl.pallas_call(
        _ada_kernel,
        out_shape=jax.ShapeDtypeStruct((8, N), F32),
        grid=(N // tn,),
        in_specs=[pl.BlockSpec((8, D), lambda j: (0, 0)),
                  pl.BlockSpec((D, tn), lambda j: (0, j)),
                  pl.BlockSpec((1, tn), lambda j: (0, j))],
        out_specs=pl.BlockSpec((8, tn), lambda j: (0, j)),
        compiler_params=pltpu.CompilerParams(vmem_limit_bytes=VMEM_LIMIT),
        name="ada",
    )(c8, w_ada, b_ada.reshape(1, N))
    return out[:B]


def _proj_kernel(x_ref, sc_ref, sh_ref, g_ref, w_ref, lb_ref, ct_ref, st_ref,
                 q_ref, k_ref, lf_ref, v_ref, gt_ref, aq_ref, ak_ref, av_ref, km_ref,
                 *, hw, aw):
    x = x_ref[...]
    ms = jnp.mean(x * x, axis=-1, keepdims=True)
    h = x * lax.rsqrt(ms + RMS_EPS) * g_ref[...]
    h = h * (1.0 + sc_ref[...]) + sh_ref[...]
    proj = _dot(h.astype(BF16), w_ref[...])

    hq = proj[:, 0:hw]
    hf = proj[:, hw:2 * hw]
    hg = proj[:, 3 * hw:4 * hw]
    q_ref[...] = hq * _sigmoid(hq) * (HGRN_DK ** -0.5)
    lb = lb_ref[...]
    f = lb + (1.0 - lb) * _sigmoid(hf)
    k_ref[...] = 1.0 - f
    lf_ref[...] = jnp.log(f)
    v_ref[...] = proj[:, 2 * hw:3 * hw].astype(BF16)
    gt_ref[...] = hg * _sigmoid(hg)

    ct = jnp.concatenate([ct_ref[...]] * (aw // 128), axis=1)
    st = jnp.concatenate([st_ref[...]] * (aw // 128), axis=1)
    lane = lax.broadcasted_iota(jnp.int32, ct.shape, 1) % ATTN_HEAD_DIM
    first_half = lane < (ROT_DIM // 2)

    def rot(t):
        partner = jnp.where(first_half, pltpu.roll(t, aw - ROT_DIM // 2, 1), pltpu.roll(t, ROT_DIM // 2, 1))
        return t * ct + partner * st

    base = 4 * hw
    aq = rot(proj[:, base:base + aw])
    ak = rot(proj[:, base + aw:base + 2 * aw])
    av = proj[:, base + 2 * aw:base + 3 * aw]
    km_ref[...] = jnp.mean(ak, axis=0, keepdims=True)
    lane128 = lax.broadcasted_iota(jnp.int32, (x.shape[0], 128), 1)
    for pair in range(ATTN_HEADS // 2):
        aq_ref[pair] = aq[:, pair * 128:(pair + 1) * 128]
    for hd in range(ATTN_HEADS):
        pair, half = divmod(hd, 2)
        in_head = (lane128 // ATTN_HEAD_DIM) == half
        ak_ref[hd] = jnp.where(in_head, ak[:, pair * 128:(pair + 1) * 128], 0.0).astype(BF16)
        av_ref[hd] = av[:, hd * ATTN_HEAD_DIM:(hd + 1) * ATTN_HEAD_DIM].astype(BF16)


def _proj(x, scale1, shift1, norm_g, w_in_bf16, lb, ctab, stab, b0):
    _, S, D = x.shape
    B = scale1.shape[0]
    hw = lb.shape[-1]
    aw = ATTN_HEADS * ATTN_HEAD_DIM
    tm = MOBA_BLOCK
    nb = S // MOBA_BLOCK
    n_proj = w_in_bf16.shape[1]
    row = lambda b, i: (b, i, 0)
    xrow = lambda b, i: (b0 + b, i, 0)
    vec = lambda b, i: (b, 0, 0)
    head = lambda b, i: (b, 0, i, 0)
    out_shapes = (
        jax.ShapeDtypeStruct((B, S, hw), F32),
        jax.ShapeDtypeStruct((B, S, hw), F32),
        jax.ShapeDtypeStruct((B, S, hw), F32),
        jax.ShapeDtypeStruct((B, S, hw), BF16),
        jax.ShapeDtypeStruct((B, S, hw), F32),
        jax.ShapeDtypeStruct((B, ATTN_HEADS // 2, S, 128), F32),
        jax.ShapeDtypeStruct((B, ATTN_HEADS, S, 128), BF16),
        jax.ShapeDtypeStruct((B, ATTN_HEADS, S, ATTN_HEAD_DIM), BF16),
        jax.ShapeDtypeStruct((B, nb, 1, aw), F32),
    )
    hspec = pl.BlockSpec((None, tm, hw), row)
    aspec = pl.BlockSpec((None, ATTN_HEADS, tm, ATTN_HEAD_DIM), head)
    return pl.pallas_call(
        functools.partial(_proj_kernel, hw=hw, aw=aw),
        out_shape=out_shapes,
        grid=(B, S // tm),
        in_specs=[pl.BlockSpec((None, tm, D), xrow),
                  pl.BlockSpec((None, 1, D), vec),
                  pl.BlockSpec((None, 1, D), vec),
                  pl.BlockSpec((1, D), lambda b, i: (0, 0)),
                  pl.BlockSpec((D, n_proj), lambda b, i: (0, 0)),
                  pl.BlockSpec((1, hw), lambda b, i: (0, 0)),
                  pl.BlockSpec((None, tm, 128), xrow),
                  pl.BlockSpec((None, tm, 128), xrow)],
        out_specs=(hspec, hspec, hspec, hspec, hspec,
                   pl.BlockSpec((None, ATTN_HEADS // 2, tm, 128), head),
                   pl.BlockSpec((None, ATTN_HEADS, tm, 128), head), aspec,
                   pl.BlockSpec((None, None, 1, aw), lambda b, i: (b, i, 0, 0))),
        compiler_params=pltpu.CompilerParams(
            dimension_semantics=("arbitrary", "arbitrary"), vmem_limit_bytes=VMEM_LIMIT),
        name="proj",
    )(x, scale1, shift1, norm_g, w_in_bf16, lb, ctab, stab)


def _hgrn_kernel(q_ref, k_ref, lf_ref, v_ref, gt_ref, gn_ref, o_ref, st_ref, *, n_heads, n_chunks):
    @pl.when(pl.program_id(1) == 0)
    def _():
        st_ref[...] = jnp.zeros_like(st_ref)

    C = HGRN_CHUNK
    r = lax.broadcasted_iota(jnp.int32, (C, C), 0)
    c = lax.broadcasted_iota(jnp.int32, (C, C), 1)
    tril = c <= r
    ltri = tril.astype(F32)
    gn = gn_ref[...]

    def chunk(ci, carry):
        r0 = pl.multiple_of(ci * C, C)
        rows = pl.ds(r0, C)
        b_all = _dot(ltri, lf_ref[rows, :], precision=HIGHEST)
        heads = range(n_heads)
        sls = [slice(hd * HGRN_DK, (hd + 1) * HGRN_DK) for hd in heads]
        bs = [b_all[:, sl] for sl in sls]
        b_lasts = [b[C - 1:C, :] for b in bs]
        qs = [q_ref[rows, sl] for sl in sls]
        ks = [k_ref[rows, sl] for sl in sls]
        vs = [v_ref[rows, sl] for sl in sls]
        states = [st_ref[hd] for hd in heads]
        o_inter = [_dot_nt((qs[hd] * jnp.exp(bs[hd])).astype(BF16), states[hd].astype(BF16)) for hd in heads]
        scores = []
        for hd in heads:
            blocks = []
            for g0 in range(0, C, HGRN_SUB):
                g1 = g0 + HGRN_SUB
                rho = 0.5 * (bs[hd][g0:g0 + 1, :] + bs[hd][g1 - 1:g1, :])
                qa = qs[hd][g0:g1, :] * jnp.exp(jnp.minimum(bs[hd][g0:g1, :] - rho, EXP_CLAMP))
                kb = ks[hd] * jnp.exp(jnp.minimum(rho - bs[hd], EXP_CLAMP))
                blocks.append(_dot_nt(qa.astype(BF16), kb.astype(BF16)))
            scores.append(jnp.where(tril, jnp.concatenate(blocks, axis=0), 0.0).astype(BF16))
        outs = [o_inter[hd] + _dot(scores[hd], vs[hd]) for hd in heads]
        kds = [(ks[hd] * jnp.exp(b_lasts[hd] - bs[hd])).astype(BF16) for hd in heads]
        upds = [_dot(vs[hd].astype(F32).T.astype(BF16), kds[hd]) for hd in heads]
        for hd in heads:
            st_ref[hd] = states[hd] * jnp.exp(b_lasts[hd]) + upds[hd]
            o = outs[hd]
            ms = jnp.mean(o * o, axis=-1, keepdims=True)
            o_ref[rows, sls[hd]] = (o * lax.rsqrt(ms + RMS_EPS) * gn * gt_ref[rows, sls[hd]]).astype(BF16)
        return carry

    lax.fori_loop(0, n_chunks, chunk, 0, unroll=True)


def _hgrn(q, k, lf, v, gt, norm_g):
    B, S, hw = q.shape
    n_heads = hw // HGRN_DK
    tc = 512
    spec = pl.BlockSpec((None, tc, hw), lambda b, i: (b, i, 0))
    return pl.pallas_call(
        functools.partial(_hgrn_kernel, n_heads=n_heads, n_chunks=tc // HGRN_CHUNK),
        out_shape=jax.ShapeDtypeStruct((B, S, hw), BF16),
        grid=(B, S // tc),
        in_specs=[spec, spec, spec, spec, spec, pl.BlockSpec((1, HGRN_DK), lambda b, i: (0, 0))],
        out_specs=spec,
        scratch_shapes=[pltpu.VMEM((n_heads, HGRN_DK, HGRN_DK), F32)],
        compiler_params=pltpu.CompilerParams(
            dimension_semantics=("arbitrary", "arbitrary"), vmem_limit_bytes=VMEM_LIMIT),
        name="hgrn",
    )(q, k, lf, v, gt, norm_g)


def _sc_move_rows(table, src, dst, n_out, chunk):
    M = src.shape[0]
    D = table.shape[1]
    n_workers = SC_CORES * SC_SUBCORES
    per_worker = M // n_workers
    n_chunks = per_worker // chunk
    assert per_worker * n_workers == M and n_chunks * chunk == per_worker and n_chunks % 2 == 0 and chunk % 8 == 0
    mesh = plsc.VectorSubcoreMesh(core_axis_name="c", subcore_axis_name="s")
    idx_t = pltpu.VMEM((chunk,), jnp.int32)
    row_t = pltpu.VMEM((chunk, D), table.dtype)
    sem_t = pltpu.SemaphoreType.DMA

    def body(table_hbm, src_hbm, dst_hbm, out_hbm, src_v, dst_v, rows_v, g_sem, s_sem):
        wid = lax.axis_index("s") * SC_CORES + lax.axis_index("c")
        base = wid * per_worker

        def offset(j):
            return pl.multiple_of(base + j * chunk, 8)

        def gather(b):
            return pltpu.make_async_copy(table_hbm.at[src_v[b]], rows_v[b], g_sem[b])

        def start_gather(j, b):
            pltpu.sync_copy(src_hbm.at[pl.ds(offset(j), chunk)], src_v[b])
            gather(b).start()

        def write_out(j, b):
            if dst_hbm is None:
                pltpu.sync_copy(rows_v[b], out_hbm.at[pl.ds(offset(j), chunk)])
            else:
                pltpu.sync_copy(dst_hbm.at[pl.ds(offset(j), chunk)], dst_v[b])
                pltpu.async_copy(rows_v[b], out_hbm.at[dst_v[b]], s_sem[b]).wait()

        start_gather(0, 0)

        @pl.loop(0, n_chunks, step=2)
        def _(j):
            for b in (0, 1):
                @pl.when(j + b + 1 < n_chunks)
                def _():
                    start_gather(j + b + 1, 1 - b)
                gather(b).wait()
                write_out(j + b, b)

    if dst is None:
        @functools.partial(pl.kernel, mesh=mesh, out_type=jax.ShapeDtypeStruct((n_out, D), table.dtype),
                           scratch_types=[idx_t, idx_t, row_t, row_t, sem_t, sem_t])
        def gather_kernel(table_hbm, src_hbm, out_hbm, s0, s1, r0, r1, g0, g1):
            body(table_hbm, src_hbm, None, out_hbm, (s0, s1), None, (r0, r1), (g0, g1), None)
        return gather_kernel(table, src)

    @functools.partial(pl.kernel, mesh=mesh, out_type=jax.ShapeDtypeStruct((n_out, D), table.dtype),
                       scratch_types=[idx_t, idx_t, idx_t, idx_t, row_t, row_t, sem_t, sem_t, sem_t, sem_t])
    def permute_kernel(table_hbm, src_hbm, dst_hbm, out_hbm, s0, s1, d0, d1, r0, r1, g0, g1, w0, w1):
        body(table_hbm, src_hbm, dst_hbm, out_hbm, (s0, s1), (d0, d1), (r0, r1), (g0, g1), (w0, w1))
    return permute_kernel(table, src, dst)


def _sc_gather(table, idx, chunk):
    return _sc_move_rows(table, idx, None, idx.shape[0], chunk)


def _sc_permute(table, src, dst, n_out, chunk):
    return _sc_move_rows(table, src, dst, n_out, chunk)


def _tile_layout(counts, bm, n_tiles):
    n_groups = counts.shape[0]
    padded = (counts + bm - 1) // bm * bm
    pad_end = jnp.cumsum(padded)
    tile_start = jnp.arange(n_tiles, dtype=jnp.int32) * bm
    tile_group = jnp.minimum(
        jnp.sum((pad_end[None, :] <= tile_start[:, None]).astype(jnp.int32), axis=1), n_groups - 1)
    n_used = (pad_end[-1] // bm).astype(jnp.int32).reshape(1)
    return pad_end - padded, tile_group.astype(jnp.int32), n_used


def _null_partial(rows):
    lane = lax.broadcasted_iota(jnp.int32, (rows, PART_W), 1)
    return jnp.where(lane < ATTN_HEAD_DIM, 0.0, -jnp.inf).astype(F32)


def _moba_sel_kernel(q_ref, km_ref, k_ref, v_ref, idx_ref, cnt_ref, own_ref, cnt_acc, *, n_blocks):
    j = pl.program_id(1)
    T = MOBA_BLOCK
    heads = range(ATTN_HEADS)
    qs = [q_ref[hd // 2] for hd in heads]
    gates = [_dot_nt(km_ref[hd], qs[hd], precision=HIGHEST) for hd in heads]
    blk = lax.broadcasted_iota(jnp.int32, gates[0].shape, 0)
    neg_inf = jnp.float32(-jnp.inf)
    gates = [jnp.where(blk < j, g, neg_inf) for g in gates]
    picks = [[] for _ in heads]
    for _ in range(MOBA_TOPK):
        ms = [jnp.max(g, axis=0, keepdims=True) for g in gates]
        firsts = [jnp.min(jnp.where(g == m, blk, n_blocks), axis=0, keepdims=True) for g, m in zip(gates, ms)]
        for hd in heads:
            picks[hd].append(jnp.where(ms[hd] > neg_inf, firsts[hd], -1))
        gates = [jnp.where(blk == f, neg_inf, g) for g, f in zip(gates, firsts)]

    @pl.when(j == 0)
    def _():
        cnt_acc[...] = jnp.zeros_like(cnt_acc)

    earlier = (lax.broadcasted_iota(jnp.int32, (T, T), 0) < lax.broadcasted_iota(jnp.int32, (T, T), 1)).astype(BF16)
    for hd in heads:
        onehots = [(blk == p).astype(F32) for p in picks[hd]]
        member = onehots[0] + onehots[1] + onehots[2]
        base = cnt_acc[hd] + _dot(member.astype(BF16), earlier)
        ranks = [jnp.sum(oh * base, axis=0, keepdims=True).astype(jnp.int32) for oh in onehots]
        idx_ref[hd] = jnp.concatenate(picks[hd] + ranks + [jnp.zeros((2, T), jnp.int32)], axis=0)
        total = cnt_acc[hd] + jnp.sum(member, axis=1, keepdims=True)
        cnt_acc[hd] = total
        cnt_ref[hd] = total.astype(jnp.int32)
    causal = lax.broadcasted_iota(jnp.int32, (T, T), 1) <= lax.broadcasted_iota(jnp.int32, (T, T), 0)
    scale = ATTN_HEAD_DIM ** -0.5
    ss = [jnp.where(causal, _dot_nt((qs[hd] * scale).astype(BF16), k_ref[hd]), neg_inf) for hd in heads]
    mx = [jnp.max(s, axis=1, keepdims=True) for s in ss]
    ps = [jnp.exp(s - m) for s, m in zip(ss, mx)]
    ls = [jnp.sum(p, axis=1, keepdims=True) for p in ps]
    accs = [_dot(ps[hd].astype(BF16), v_ref[hd]) for hd in heads]
    for hd in heads:
        lse = jnp.broadcast_to(mx[hd] + jnp.log(ls[hd]), (T, PART_W - ATTN_HEAD_DIM))
        own_ref[hd] = jnp.concatenate([accs[hd] / ls[hd], lse], axis=1)


def _moba_sel(aq, kmean, ak, av):
    B, H, S, hd = av.shape
    nb = S // MOBA_BLOCK
    T = MOBA_BLOCK
    blk = lambda b, j: (b, 0, j, 0)
    return pl.pallas_call(
        functools.partial(_moba_sel_kernel, n_blocks=nb),
        out_shape=(jax.ShapeDtypeStruct((B, H, 8, S), jnp.int32),
                   jax.ShapeDtypeStruct((B, H, nb, 1), jnp.int32),
                   jax.ShapeDtypeStruct((B, H, S, PART_W), F32)),
        grid=(B, nb),
        in_specs=[pl.BlockSpec((None, H // 2, T, 128), blk),
                  pl.BlockSpec((None, H, nb, 128), lambda b, j: (b, 0, 0, 0)),
                  pl.BlockSpec((None, H, T, 128), blk),
                  pl.BlockSpec((None, H, T, hd), blk)],
        out_specs=(pl.BlockSpec((None, H, 8, T), lambda b, j: (b, 0, 0, j)),
                   pl.BlockSpec((None, H, nb, 1), lambda b, j: (b, 0, 0, 0)),
                   pl.BlockSpec((None, H, T, PART_W), blk)),
        scratch_shapes=[pltpu.VMEM((H, nb, 1), F32)],
        compiler_params=pltpu.CompilerParams(
            dimension_semantics=("arbitrary", "arbitrary"), vmem_limit_bytes=VMEM_LIMIT),
        name="moba_sel",
    )(aq, kmean, ak, av)


def _moba_blk_kernel(tg_ref, nu_ref, q_ref, k_ref, v_ref, o_ref, *, n_blocks):
    n = MOBA_TILES_PER_STEP
    R = MOBA_ROWS
    t0 = pl.program_id(0) * n

    @pl.when(t0 < nu_ref[0])
    def _():
        scale = ATTN_HEAD_DIM ** -0.5
        groups = [tg_ref[t0 + j] for j in range(n)]
        kv_rows = [(g // n_blocks, pl.ds(pl.multiple_of((g % n_blocks) * MOBA_BLOCK, MOBA_BLOCK), MOBA_BLOCK))
                   for g in groups]
        ss = [_dot_nt((q_ref[j * R:(j + 1) * R, :] * scale).astype(BF16), k_ref[kv_rows[j][0], kv_rows[j][1], :])
              for j in range(n)]
        ms = [jnp.max(s, axis=1, keepdims=True) for s in ss]
        ps = [jnp.exp(s - m) for s, m in zip(ss, ms)]
        ls = [jnp.sum(p, axis=1, keepdims=True) for p in ps]
        accs = [_dot(p.astype(BF16), v_ref[kv_rows[j][0], kv_rows[j][1], :]) for j, p in enumerate(ps)]
        null = _null_partial(R)
        for j in range(n):
            lse = jnp.broadcast_to(ms[j] + jnp.log(ls[j]), (R, PART_W - ATTN_HEAD_DIM))
            row = jnp.concatenate([accs[j] / ls[j], lse], axis=1)
            o_ref[j * R:(j + 1) * R, :] = jnp.where(t0 + j < nu_ref[0], row, null)

    @pl.when(t0 >= nu_ref[0])
    def _():
        o_ref[...] = _null_partial(n * R)


def _moba_blk(qs, tile_group, n_used, ak, av):
    B, H, S, hd = av.shape
    nb = S // MOBA_BLOCK
    R = MOBA_ROWS
    n = MOBA_TILES_PER_STEP
    n_tiles = qs.shape[0] // R
    assert n_tiles % n == 0
    whole = lambda i, tg, nu: (0, 0, 0)
    grid_spec = pltpu.PrefetchScalarGridSpec(
        num_scalar_prefetch=2,
        grid=(n_tiles // n,),
        in_specs=[pl.BlockSpec((n * R, 128), lambda i, tg, nu: (i, 0)),
                  pl.BlockSpec((B * H, S, 128), whole, pipeline_mode=pl.Buffered(1)),
                  pl.BlockSpec((B * H, S, hd), whole, pipeline_mode=pl.Buffered(1))],
        out_specs=pl.BlockSpec((n * R, PART_W), lambda i, tg, nu: (i, 0)),
    )
    return pl.pallas_call(
        functools.partial(_moba_blk_kernel, n_blocks=nb),
        out_shape=jax.ShapeDtypeStruct((n_tiles * R, PART_W), F32),
        grid_spec=grid_spec,
        compiler_params=pltpu.CompilerParams(
            dimension_semantics=("arbitrary",), vmem_limit_bytes=VMEM_LIMIT),
        name="moba_blk",
    )(tile_group, n_used, qs, ak.reshape(B * H, S, 128), av.reshape(B * H, S, hd))


def _moba_merge_kernel(own_ref, pg_ref, g_ref, o_ref):
    hd = ATTN_HEAD_DIM
    rows = [own_ref[...]] + [pg_ref[s] for s in range(MOBA_TOPK)]
    lses = [pltpu.roll(r, hd, 1) for r in rows]
    top = lses[0]
    for z in lses[1:]:
        top = jnp.maximum(top, z)
    num = jnp.zeros_like(top)
    den = jnp.zeros_like(top)
    for r, z in zip(rows, lses):
        w = jnp.exp(z - top)
        num = num + w * r
        den = den + w
    o = (num / den)[:, :hd]
    ms = jnp.mean(o * o, axis=-1, keepdims=True)
    o_ref[...] = o * lax.rsqrt(ms + RMS_EPS) * g_ref[...]


def _moba_merge(own, pg, norm_g):
    n = own.shape[0]
    T = 2048
    row = lambda i: (i, 0)
    return pl.pallas_call(
        _moba_merge_kernel,
        out_shape=jax.ShapeDtypeStruct((n, ATTN_HEAD_DIM), F32),
        grid=(n // T,),
        in_specs=[pl.BlockSpec((T, PART_W), row),
                  pl.BlockSpec((MOBA_TOPK, T, PART_W), lambda i: (0, i, 0)),
                  pl.BlockSpec((1, ATTN_HEAD_DIM), lambda i: (0, 0))],
        out_specs=pl.BlockSpec((T, ATTN_HEAD_DIM), row),
        compiler_params=pltpu.CompilerParams(
            dimension_semantics=("arbitrary",), vmem_limit_bytes=VMEM_LIMIT),
        name="moba_merge",
    )(own, pg, norm_g)


def _moba(aq, km, ak, av, norm_g):
    B, H, S, hd = av.shape
    nb = S // MOBA_BLOCK
    n_q = B * H * S
    kmp = km.reshape(B, nb, H // 2, 128)
    half = jnp.arange(128, dtype=jnp.int32) // hd
    kmean = jnp.stack([jnp.where(half == h % 2, kmp[:, :, h // 2, :], 0.0) for h in range(H)], axis=1)
    idx8, counts, own = _moba_sel(aq, kmean, ak, av)
    sel = idx8[:, :, 0:MOBA_TOPK, :].reshape(B * H, MOBA_TOPK, S)
    rank = idx8[:, :, MOBA_TOPK:2 * MOBA_TOPK, :].reshape(B * H, MOBA_TOPK, S)
    n_groups = B * H * nb
    n_tiles = (n_q * MOBA_TOPK) // MOBA_ROWS + n_groups
    pad_start, tile_group, n_used = _tile_layout(counts.reshape(-1), MOBA_ROWS, n_tiles)
    blocks = jnp.arange(nb, dtype=jnp.int32)[:, None]
    start = jnp.sum(jnp.where(sel[:, :, None, :] == blocks, pad_start.reshape(B * H, 1, nb, 1), 0), axis=2)
    a_ids = jnp.arange(n_q * MOBA_TOPK, dtype=jnp.int32).reshape(B * H, MOBA_TOPK, S)
    assert n_tiles * MOBA_ROWS >= n_q * MOBA_TOPK + MOBA_ROWS
    pos = jnp.where(sel >= 0, start + rank, n_used[0] * MOBA_ROWS + a_ids % MOBA_ROWS)
    bh = jnp.arange(B * H, dtype=jnp.int32)[:, None, None]
    t = jnp.arange(S, dtype=jnp.int32)[None, None, :]
    pair_row = jnp.broadcast_to((bh // H * (H // 2) + bh % H // 2) * S + t, pos.shape)
    qs = _sc_permute(aq.reshape(B * (H // 2) * S, 128), pair_row.reshape(-1), pos.reshape(-1),
                     n_tiles * MOBA_ROWS, 256)
    parts = _moba_blk(qs, tile_group, n_used, ak, av)
    pg = _sc_gather(parts, pos.transpose(1, 0, 2).reshape(-1), 256)
    o = _moba_merge(own.reshape(n_q, PART_W), pg.reshape(MOBA_TOPK, n_q, PART_W), norm_g)
    return o.reshape(B, H, S, hd)


def _mix_kernel(oh_ref, oa_ref, x_ref, g1_ref, sc2_ref, sh2_ref, n2_ref, wo_ref, wr_ref, br_ref,
                x1_ref, h2_ref, gw_ref, idx_ref, cnt_ref, cnt_acc):
    cat = jnp.concatenate([oh_ref[...]] + [oa_ref[hd] for hd in range(ATTN_HEADS)], axis=1)
    mix = _dot(cat.astype(BF16), wo_ref[...])
    x1 = x_ref[...] + g1_ref[...] * mix
    x1_ref[...] = x1
    ms = jnp.mean(x1 * x1, axis=-1, keepdims=True)
    h2 = x1 * lax.rsqrt(ms + RMS_EPS) * n2_ref[...]
    h2 = h2 * (1.0 + sc2_ref[...]) + sh2_ref[...]
    h2_ref[...] = _pack_bf16_pairs(h2)
    E = N_EXPERTS
    tm = h2.shape[0]
    h_0 = h2.astype(BF16)
    r_1 = h2 - h_0.astype(F32)
    h_1 = r_1.astype(BF16)
    h_2 = (r_1 - h_1.astype(F32)).astype(BF16)
    wt = wr_ref[...]
    p_0 = _dot_nt(wt, h_0)
    p_1 = _dot_nt(wt[:2 * E], h_1)
    p_2 = _dot_nt(wt[:E], h_2)
    logits = (p_0[:E] + (p_0[E:2 * E] + p_1[:E]) + (p_0[2 * E:] + p_1[E:] + p_2)) + br_ref[...]
    ex = lax.broadcasted_iota(jnp.int32, logits.shape, 0)
    neg_inf = jnp.float32(-jnp.inf)
    vals, idxs = [], []
    for _ in range(TOP_K):
        m = jnp.max(logits, axis=0, keepdims=True)
        first = jnp.min(jnp.where(logits == m, ex, E), axis=0, keepdims=True)
        vals.append(m)
        idxs.append(first)
        logits = jnp.where(ex == first, neg_inf, logits)
    e = [jnp.exp(v - vals[0]) for v in vals]
    denom = e[0] + e[1] + e[2] + e[3]
    gate_rows = jnp.concatenate([ei / denom for ei in e] + [jnp.zeros((128 - TOP_K, tm), F32)], axis=0)
    gw_ref[...] = gate_rows.T[:, :TOP_K]

    @pl.when((pl.program_id(0) == 0) & (pl.program_id(1) == 0))
    def _():
        cnt_acc[...] = jnp.zeros_like(cnt_acc)

    earlier = (lax.broadcasted_iota(jnp.int32, (tm, tm), 0) < lax.broadcasted_iota(jnp.int32, (tm, tm), 1)).astype(BF16)
    onehots = [(ex == ix).astype(F32) for ix in idxs]
    member = onehots[0] + onehots[1] + onehots[2] + onehots[3]
    base = cnt_acc[...] + _dot(member.astype(BF16), earlier)
    ranks = [jnp.sum(oh * base, axis=0, keepdims=True).astype(jnp.int32) for oh in onehots]
    idx_ref[...] = jnp.concatenate(idxs + ranks, axis=0)
    total = cnt_acc[...] + jnp.sum(member, axis=1, keepdims=True)
    cnt_acc[...] = total
    cnt_ref[...] = total.astype(jnp.int32)


def _mix(oh, oa, x, gate1, scale2, shift2, norm2_g, w_out_bf16, w_router, b_router, b0):
    _, S, D = x.shape
    B = oh.shape[0]
    hw = oh.shape[-1]
    tm = MIX_ROWS
    row = lambda b, i: (b, i, 0)
    xrow = lambda b, i: (b0 + b, i, 0)
    vec = lambda b, i: (b, 0, 0)
    const = lambda b, i: (0, 0)
    return pl.pallas_call(
        _mix_kernel,
        out_shape=(jax.ShapeDtypeStruct((B, S, D), F32),
                   jax.ShapeDtypeStruct((B, S, D // 2), jnp.int32),
                   jax.ShapeDtypeStruct((B, S, TOP_K), F32),
                   jax.ShapeDtypeStruct((B, 2 * TOP_K, S), jnp.int32),
                   jax.ShapeDtypeStruct((N_EXPERTS, 1), jnp.int32)),
        grid=(B, S // tm),
        in_specs=[pl.BlockSpec((None, tm, hw), row),
                  pl.BlockSpec((None, ATTN_HEADS, tm, ATTN_HEAD_DIM), lambda b, i: (b, 0, i, 0)),
                  pl.BlockSpec((None, tm, D), xrow),
                  pl.BlockSpec((None, 1, D), vec),
                  pl.BlockSpec((None, 1, D), vec),
                  pl.BlockSpec((None, 1, D), vec),
                  pl.BlockSpec((1, D), const),
                  pl.BlockSpec((D, D), const),
                  pl.BlockSpec((3 * N_EXPERTS, D), const),
                  pl.BlockSpec((N_EXPERTS, 1), const)],
        out_specs=(pl.BlockSpec((None, tm, D), row),
                   pl.BlockSpec((None, tm, D // 2), row),
                   pl.BlockSpec((None, tm, TOP_K), row),
                   pl.BlockSpec((None, 2 * TOP_K, tm), lambda b, i: (b, 0, i)),
                   pl.BlockSpec((N_EXPERTS, 1), const)),
        scratch_shapes=[pltpu.VMEM((N_EXPERTS, 1), F32)],
        compiler_params=pltpu.CompilerParams(
            dimension_semantics=("arbitrary", "arbitrary"), vmem_limit_bytes=VMEM_LIMIT),
        name="mix",
    )(oh, oa, x, gate1, scale2, shift2, norm2_g, w_out_bf16, w_router, b_router)


def _moe_rows_kernel(be_ref, nx_ref, nu_ref, x_ref, wgu_hbm, bgu_ref, wd_hbm, bd_ref, y_ref,
                     wgu32, wd32, wgu16, wd16, sem, *, d_ff):
    i = pl.program_id(0)
    e = be_ref[i]

    def fetch(expert):
        return (pltpu.make_async_copy(wgu_hbm.at[expert], wgu32, sem.at[0]),
                pltpu.make_async_copy(wd_hbm.at[expert], wd32, sem.at[1]))

    @pl.when(i == 0)
    def _():
        for c in fetch(e):
            c.start()

    @pl.when((i == 0) | (e != be_ref[jnp.maximum(i - 1, 0)]))
    def _():
        for c in fetch(e):
            c.wait()
        wgu16[...] = wgu32[...].astype(BF16)
        wd16[...] = wd32[...].astype(BF16)

        @pl.when(nx_ref[i] >= 0)
        def _():
            for c in fetch(nx_ref[i]):
                c.start()

    @pl.when(i < nu_ref[0])
    def _():
        gu = _dot(_unpack_bf16_pairs(x_ref[...]).astype(BF16), wgu16[...]) + bgu_ref[...]
        gate = jnp.minimum(gu[:, :d_ff], SWIGLU_LIMIT)
        up = jnp.clip(gu[:, d_ff:], -SWIGLU_LIMIT, SWIGLU_LIMIT)
        act = (up + 1.0) * gate * _sigmoid(SWIGLU_ALPHA * gate)
        y_ref[...] = _pack_bf16_pairs(_dot(act.astype(BF16), wd16[...]) + bd_ref[...])

    @pl.when(i >= nu_ref[0])
    def _():
        y_ref[...] = jnp.zeros_like(y_ref)


def _moe_rows(xs, blk_expert, n_used, wgu, bgu, wd, bd):
    D = 2 * xs.shape[1]
    bm = MOE_ROWS
    n_blk = xs.shape[0] // bm
    d_ff = wd.shape[1]
    run_end = jnp.sum((blk_expert[None, :] <= blk_expert[:, None]).astype(jnp.int32), axis=1)
    next_expert = jnp.where(run_end < n_blk, blk_expert[jnp.minimum(run_end, n_blk - 1)], -1).astype(jnp.int32)
    bsel = lambda i, be, nx, nu: (be[i], 0, 0)
    rows = lambda i, be, nx, nu: (i, 0)
    grid_spec = pltpu.PrefetchScalarGridSpec(
        num_scalar_prefetch=3,
        grid=(n_blk,),
        in_specs=[pl.BlockSpec((bm, D // 2), rows),
                  pl.BlockSpec(memory_space=pl.ANY),
                  pl.BlockSpec((None, 1, 2 * d_ff), bsel),
                  pl.BlockSpec(memory_space=pl.ANY),
                  pl.BlockSpec((None, 1, D), bsel)],
        out_specs=pl.BlockSpec((bm, D // 2), rows),
        scratch_shapes=[pltpu.VMEM((D, 2 * d_ff), F32), pltpu.VMEM((d_ff, D), F32),
                        pltpu.VMEM((D, 2 * d_ff), BF16), pltpu.VMEM((d_ff, D), BF16),
                        pltpu.SemaphoreType.DMA((2,))],
    )
    return pl.pallas_call(
        functools.partial(_moe_rows_kernel, d_ff=d_ff),
        out_shape=jax.ShapeDtypeStruct((n_blk * bm, D // 2), jnp.int32),
        grid_spec=grid_spec,
        compiler_params=pltpu.CompilerParams(
            dimension_semantics=("arbitrary",), vmem_limit_bytes=VMEM_LIMIT),
        name="moe_rows",
    )(blk_expert, next_expert, n_used, xs, wgu, bgu.reshape(N_EXPERTS, 1, 2 * d_ff), wd, bd.reshape(N_EXPERTS, 1, D))


def _combine_rows_kernel(*refs):
    y_refs = refs[:TOP_K]
    gw_ref, x1_ref, g2_ref, fg_ref = refs[TOP_K:TOP_K + 4]
    o_ref = refs[-1]
    gw = gw_ref[...]
    y = gw[:, 0:1] * _unpack_bf16_pairs(y_refs[0][...])
    for kk in range(1, TOP_K):
        y = y + gw[:, kk:kk + 1] * _unpack_bf16_pairs(y_refs[kk][...])
    x2 = x1_ref[...] + g2_ref[...] * y
    ms = jnp.mean(x2 * x2, axis=-1, keepdims=True)
    o_ref[...] = x2 * lax.rsqrt(ms + RMS_EPS) * fg_ref[...]


def _combine_rows(yg, gates, x1, gate2, final_g, out_so_far, b0, n_batches):
    S, D = x1.shape
    tm = 512
    steps = S // tm
    slot_spec = lambda kk: pl.BlockSpec((tm, D // 2), lambda i: (kk * steps + i, 0))
    in_specs = [slot_spec(kk) for kk in range(TOP_K)] + [
        pl.BlockSpec((tm, TOP_K), lambda i: (i, 0)),
        pl.BlockSpec((tm, D), lambda i: (i, 0)),
        pl.BlockSpec((1, D), lambda i: (0, 0)),
        pl.BlockSpec((1, D), lambda i: (0, 0))]
    args = [yg] * TOP_K + [gates, x1, gate2, final_g]
    aliases = {}
    if out_so_far is not None:
        in_specs.append(pl.BlockSpec(memory_space=pl.ANY))
        aliases = {len(args): 0}
        args.append(out_so_far)
    return pl.pallas_call(
        _combine_rows_kernel,
        out_shape=jax.ShapeDtypeStruct((n_batches * S, D), F32),
        grid=(steps,),
        in_specs=in_specs,
        out_specs=pl.BlockSpec((tm, D), lambda i: (b0 * steps + i, 0)),
        input_output_aliases=aliases,
        compiler_params=pltpu.CompilerParams(
            dimension_semantics=("arbitrary",), vmem_limit_bytes=VMEM_LIMIT),
        name="combine_rows",
    )(*args)


def _split_bf16x3(w):
    def top(v):
        return lax.bitcast_convert_type(lax.bitcast_convert_type(v, jnp.int32) & jnp.int32(-65536), F32)
    w0 = top(w)
    w1 = top(w - w0)
    w2 = w - w0 - w1
    return jnp.concatenate([w0, w1, w2], axis=1).astype(BF16).T


def _rotary_tables(positions):
    half = ROT_DIM // 2
    inv_freq = jnp.exp(-math.log(ROPE_THETA) * jnp.arange(0, ROT_DIM, 2, dtype=F32) / ROT_DIM)
    d = jnp.arange(128, dtype=jnp.int32) % ATTN_HEAD_DIM
    freq = jnp.where(d < ROT_DIM, inv_freq[d % half], 0.0)
    sign = jnp.where(d < half, -1.0, 1.0)
    ang = positions.astype(F32)[:, :, None] * freq
    return jnp.cos(ang), jnp.sin(ang) * sign


def kernel(x, c, positions, w_ada, b_ada, norm1_g, w_in, hgrn_lb_logits, hgrn_norm_g, attn_norm_g,
           w_out, norm2_g, w_router, b_router, w_gate_up, b_gate_up, w_down, b_down, final_norm_g):
    B, S, D = x.shape
    assert w_in.shape[0] == 1, "single-layer block: the final norm is fused into the combine step"
    l = 0
    ctab, stab = _rotary_tables(positions)
    lower_bounds = jnp.cumsum(jax.nn.softmax(hgrn_lb_logits.astype(F32), axis=0), axis=0)
    mod = _ada(c, w_ada[l], b_ada[l])
    shift1, scale1, gate1, shift2, scale2, gate2 = jnp.split(mod[:, None, :], N_MOD, axis=-1)
    n_blk = (S * TOP_K) // MOE_ROWS + N_EXPERTS
    experts = jnp.arange(N_EXPERTS, dtype=jnp.int32)[:, None]
    tok = jnp.broadcast_to(jnp.arange(S, dtype=jnp.int32)[None, :], (TOP_K, S)).reshape(-1)
    out = None
    for b in range(B):
        one = slice(b, b + 1)
        q, k, lf, v, gt, aq, ak, av, km = _proj(
            x, scale1[one], shift1[one], norm1_g[l][None], w_in[l].astype(BF16), lower_bounds[l][None],
            ctab, stab, b)
        o_a = _moba(aq, km, ak, av, attn_norm_g[l][None])
        o_h = _hgrn(q, k, lf, v, gt, hgrn_norm_g[l][None])
        x1, h2, gates, idx8, counts = _mix(
            o_h, o_a, x, gate1[one], scale2[one], shift2[one], norm2_g[l][None], w_out[l].astype(BF16),
            _split_bf16x3(w_router[l]), b_router[l][:, None], b)
        pad_start, blk_expert, n_used = _tile_layout(counts.reshape(-1), MOE_ROWS, n_blk)
        chosen = idx8[0, 0:TOP_K, :]
        pos = jnp.sum(jnp.where(chosen[:, None, :] == experts, pad_start[:, None], 0), axis=1) + idx8[0, TOP_K:, :]
        xs = _sc_permute(h2.reshape(S, D // 2), tok, pos.reshape(-1), n_blk * MOE_ROWS, 64)
        y_sorted = _moe_rows(xs, blk_expert, n_used, w_gate_up[l], b_gate_up[l], w_down[l], b_down[l])
        yg = _sc_gather(y_sorted, pos.reshape(-1), 64)
        out = _combine_rows(yg, gates.reshape(S, TOP_K), x1.reshape(S, D), gate2[b], final_norm_g[None], out, b, B)
    return out.reshape(B, S, D)
```

```python
import functools
import math

import jax
import jax.numpy as jnp
from jax import lax
from jax.experimental import pallas as pl
from jax.experimental.pallas import tpu as pltpu
from jax.experimental.pallas import tpu_sc as plsc

F32 = jnp.float32
BF16 = jnp.bfloat16
HIGHEST = lax.Precision.HIGHEST

HGRN_DK = 128
HGRN_CHUNK = 64
ATTN_HEADS = 4
ATTN_HEAD_DIM = 64
ROT_DIM = ATTN_HEAD_DIM // 4
ROPE_THETA = 500000.0
MOBA_BLOCK = 256
MOBA_TOPK = 3
N_EXPERTS = 32
TOP_K = 4
SWIGLU_ALPHA = 1.702
SWIGLU_LIMIT = 7.0
N_MOD = 6
RMS_EPS = 1e-6

HGRN_SUB = 16
EXP_CLAMP = 80.0
MIX_ROWS = 512
MOE_ROWS = 512
MOBA_ROWS = 256
MOBA_TILES_PER_STEP = 8
PART_W = 128
V7X_VMEM_BYTES = 64 * 1024 * 1024
VMEM_LIMIT = V7X_VMEM_BYTES * 7 // 8
SC_CORES = 2
SC_SUBCORES = 16


def _sigmoid(x):
    return 1.0 / (1.0 + jnp.exp(-x))


def _dot(a, b, **kw):
    return jnp.dot(a, b, preferred_element_type=F32, **kw)


def _dot_nt(a, b, **kw):
    return lax.dot_general(a, b, (((1,), (1,)), ((), ())), preferred_element_type=F32, **kw)


def _pack_bf16_pairs(x):
    w = x.shape[1] // 2
    bits = lax.bitcast_convert_type(x.astype(BF16).astype(F32), jnp.int32)
    return bits[:, w:] | lax.shift_right_logical(bits[:, :w], 16)


def _unpack_bf16_pairs(p):
    lo = lax.bitcast_convert_type(lax.shift_left(p, 16), F32)
    hi = lax.bitcast_convert_type(p & jnp.int32(-65536), F32)
    return jnp.concatenate([lo, hi], axis=1)


def _ada_kernel(c_ref, w_ref, b_ref, o_ref):
    c = c_ref[...]
    o_ref[...] = _dot(c * _sigmoid(c), w_ref[...], precision=HIGHEST) + b_ref[...]


def _ada(c, w_ada, b_ada):
    B, D = c.shape
    N = w_ada.shape[1]
    tn = N // 4
    c8 = jnp.zeros((8, D), F32).at[:B].set(c)
    out = pl.pallas_call(
        _ada_kernel,
        out_shape=jax.ShapeDtypeStruct((8, N), F32),
        grid=(N // tn,),
        in_specs=[pl.BlockSpec((8, D), lambda j: (0, 0)),
                  pl.BlockSpec((D, tn), lambda j: (0, j)),
                  pl.BlockSpec((1, tn), lambda j: (0, j))],
        out_specs=pl.BlockSpec((8, tn), lambda j: (0, j)),
        compiler_params=pltpu.CompilerParams(vmem_limit_bytes=VMEM_LIMIT),
        name="ada",
    )(c8, w_ada, b_ada.reshape(1, N))
    return out[:B]


def _proj_kernel(x_ref, sc_ref, sh_ref, g_ref, w_ref, lb_ref, ct_ref, st_ref,
                 q_ref, k_ref, lf_ref, v_ref, gt_ref, aq_ref, ak_ref, av_ref, km_ref,
                 *, hw, aw):
    x = x_ref[...]
    ms = jnp.mean(x * x, axis=-1, keepdims=True)
    h = x * lax.rsqrt(ms + RMS_EPS) * g_ref[...]
    h = h * (1.0 + sc_ref[...]) + sh_ref[...]
    proj = _dot(h.astype(BF16), w_ref[...])

    hq = proj[:, 0:hw]
    hf = proj[:, hw:2 * hw]
    hg = proj[:, 3 * hw:4 * hw]
    q_ref[...] = (hq * _sigmoid(hq) * (HGRN_DK ** -0.5)).astype(BF16)
    lb = lb_ref[...]
    f = lb + (1.0 - lb) * _sigmoid(hf)
    k_ref[...] = (1.0 - f).astype(BF16)
    lf_ref[...] = jnp.log(f)
    v_ref[...] = proj[:, 2 * hw:3 * hw].astype(BF16)
    gt_ref[...] = (hg * _sigmoid(hg)).astype(BF16)

    ct = jnp.concatenate([ct_ref[...]] * (aw // 128), axis=1)
    st = jnp.concatenate([st_ref[...]] * (aw // 128), axis=1)
    lane = lax.broadcasted_iota(jnp.int32, ct.shape, 1) % ATTN_HEAD_DIM
    first_half = lane < (ROT_DIM // 2)

    def rot(t):
        partner = jnp.where(first_half, pltpu.roll(t, aw - ROT_DIM // 2, 1), pltpu.roll(t, ROT_DIM // 2, 1))
        return t * ct + partner * st

    base = 4 * hw
    aq = rot(proj[:, base:base + aw])
    ak = rot(proj[:, base + aw:base + 2 * aw])
    av = proj[:, base + 2 * aw:base + 3 * aw]
    km_ref[...] = jnp.mean(ak, axis=0, keepdims=True)
    lane128 = lax.broadcasted_iota(jnp.int32, (x.shape[0], 128), 1)
    for pair in range(ATTN_HEADS // 2):
        aq_ref[pair] = aq[:, pair * 128:(pair + 1) * 128]
    for hd in range(ATTN_HEADS):
        pair, half = divmod(hd, 2)
        in_head = (lane128 // ATTN_HEAD_DIM) == half
        ak_ref[hd] = jnp.where(in_head, ak[:, pair * 128:(pair + 1) * 128], 0.0).astype(BF16)
        av_ref[hd] = av[:, hd * ATTN_HEAD_DIM:(hd + 1) * ATTN_HEAD_DIM].astype(BF16)


def _proj(x, scale1, shift1, norm_g, w_in_bf16, lb, ctab, stab, b0):
    _, S, D = x.shape
    B = scale1.shape[0]
    hw = lb.shape[-1]
    aw = ATTN_HEADS * ATTN_HEAD_DIM
    tm = MOBA_BLOCK
    nb = S // MOBA_BLOCK
    n_proj = w_in_bf16.shape[1]
    row = lambda b, i: (b, i, 0)
    xrow = lambda b, i: (b0 + b, i, 0)
    vec = lambda b, i: (b, 0, 0)
    head = lambda b, i: (b, 0, i, 0)
    out_shapes = (
        jax.ShapeDtypeStruct((B, S, hw), BF16),
        jax.ShapeDtypeStruct((B, S, hw), BF16),
        jax.ShapeDtypeStruct((B, S, hw), F32),
        jax.ShapeDtypeStruct((B, S, hw), BF16),
        jax.ShapeDtypeStruct((B, S, hw), BF16),
        jax.ShapeDtypeStruct((B, ATTN_HEADS // 2, S, 128), F32),
        jax.ShapeDtypeStruct((B, ATTN_HEADS, S, 128), BF16),
        jax.ShapeDtypeStruct((B, ATTN_HEADS, S, ATTN_HEAD_DIM), BF16),
        jax.ShapeDtypeStruct((B, nb, 1, aw), F32),
    )
    hspec = pl.BlockSpec((None, tm, hw), row)
    aspec = pl.BlockSpec((None, ATTN_HEADS, tm, ATTN_HEAD_DIM), head)
    return pl.pallas_call(
        functools.partial(_proj_kernel, hw=hw, aw=aw),
        out_shape=out_shapes,
        grid=(B, S // tm),
        in_specs=[pl.BlockSpec((None, tm, D), xrow),
                  pl.BlockSpec((None, 1, D), vec),
                  pl.BlockSpec((None, 1, D), vec),
                  pl.BlockSpec((1, D), lambda b, i: (0, 0)),
                  pl.BlockSpec((D, n_proj), lambda b, i: (0, 0)),
                  pl.BlockSpec((1, hw), lambda b, i: (0, 0)),
                  pl.BlockSpec((None, tm, 128), xrow),
                  pl.BlockSpec((None, tm, 128), xrow)],
        out_specs=(hspec, hspec, hspec, hspec, hspec,
                   pl.BlockSpec((None, ATTN_HEADS // 2, tm, 128), head),
                   pl.BlockSpec((None, ATTN_HEADS, tm, 128), head), aspec,
                   pl.BlockSpec((None, None, 1, aw), lambda b, i: (b, i, 0, 0))),
        compiler_params=pltpu.CompilerParams(
            dimension_semantics=("arbitrary", "arbitrary"), vmem_limit_bytes=VMEM_LIMIT),
        name="proj",
    )(x, scale1, shift1, norm_g, w_in_bf16, lb, ctab, stab)


def _hgrn_kernel(q_ref, k_ref, lf_ref, v_ref, gt_ref, gn_ref, o_ref, st_ref, *, n_heads, n_chunks):
    @pl.when(pl.program_id(1) == 0)
    def _():
        st_ref[...] = jnp.zeros_like(st_ref)

    C = HGRN_CHUNK
    r = lax.broadcasted_iota(jnp.int32, (C, C), 0)
    c = lax.broadcasted_iota(jnp.int32, (C, C), 1)
    tril = c <= r
    ltri = tril.astype(F32)
    gn = gn_ref[...]

    def chunk(ci, carry):
        r0 = pl.multiple_of(ci * C, C)
        rows = pl.ds(r0, C)
        b_all = _dot(ltri, lf_ref[rows, :], precision=HIGHEST)
        heads = range(n_heads)
        sls = [slice(hd * HGRN_DK, (hd + 1) * HGRN_DK) for hd in heads]
        bs = [b_all[:, sl] for sl in sls]
        b_lasts = [b[C - 1:C, :] for b in bs]
        qs = [q_ref[rows, sl].astype(F32) for sl in sls]
        ks = [k_ref[rows, sl].astype(F32) for sl in sls]
        vs = [v_ref[rows, sl] for sl in sls]
        states = [st_ref[hd] for hd in heads]
        o_inter = [_dot_nt((qs[hd] * jnp.exp(bs[hd])).astype(BF16), states[hd].astype(BF16)) for hd in heads]
        scores = []
        for hd in heads:
            blocks = []
            for g0 in range(0, C, HGRN_SUB):
                g1 = g0 + HGRN_SUB
                rho = 0.5 * (bs[hd][g0:g0 + 1, :] + bs[hd][g1 - 1:g1, :])
                qa = qs[hd][g0:g1, :] * jnp.exp(jnp.minimum(bs[hd][g0:g1, :] - rho, EXP_CLAMP))
                kb = ks[hd] * jnp.exp(jnp.minimum(rho - bs[hd], EXP_CLAMP))
                blocks.append(_dot_nt(qa.astype(BF16), kb.astype(BF16)))
            scores.append(jnp.where(tril, jnp.concatenate(blocks, axis=0), 0.0).astype(BF16))
        outs = [o_inter[hd] + _dot(scores[hd], vs[hd]) for hd in heads]
        kds = [(ks[hd] * jnp.exp(b_lasts[hd] - bs[hd])).astype(BF16) for hd in heads]
        upds = [_dot(vs[hd].astype(F32).T.astype(BF16), kds[hd]) for hd in heads]
        for hd in heads:
            st_ref[hd] = states[hd] * jnp.exp(b_lasts[hd]) + upds[hd]
            o = outs[hd]
            ms = jnp.mean(o * o, axis=-1, keepdims=True)
            o_ref[rows, sls[hd]] = (o * lax.rsqrt(ms + RMS_EPS) * gn * gt_ref[rows, sls[hd]].astype(F32)).astype(BF16)
        return carry

    lax.fori_loop(0, n_chunks, chunk, 0, unroll=True)


def _hgrn(q, k, lf, v, gt, norm_g):
    B, S, hw = q.shape
    n_heads = hw // HGRN_DK
    tc = 512
    spec = pl.BlockSpec((None, tc, hw), lambda b, i: (b, i, 0))
    return pl.pallas_call(
        functools.partial(_hgrn_kernel, n_heads=n_heads, n_chunks=tc // HGRN_CHUNK),
        out_shape=jax.ShapeDtypeStruct((B, S, hw), BF16),
        grid=(B, S // tc),
        in_specs=[spec, spec, spec, spec, spec, pl.BlockSpec((1, HGRN_DK), lambda b, i: (0, 0))],
        out_specs=spec,
        scratch_shapes=[pltpu.VMEM((n_heads, HGRN_DK, HGRN_DK), F32)],
        compiler_params=pltpu.CompilerParams(
            dimension_semantics=("arbitrary", "arbitrary"), vmem_limit_bytes=VMEM_LIMIT),
        name="hgrn",
    )(q, k, lf, v, gt, norm_g)


def _sc_move_rows(table, src, dst, n_out, chunk):
    M = src.shape[0]
    D = table.shape[1]
    n_workers = SC_CORES * SC_SUBCORES
    per_worker = M // n_workers
    n_chunks = per_worker // chunk
    assert per_worker * n_workers == M and n_chunks * chunk == per_worker and n_chunks % 2 == 0 and chunk % 8 == 0
    mesh = plsc.VectorSubcoreMesh(core_axis_name="c", subcore_axis_name="s")
    idx_t = pltpu.VMEM((chunk,), jnp.int32)
    row_t = pltpu.VMEM((chunk, D), table.dtype)
    sem_t = pltpu.SemaphoreType.DMA

    def body(table_hbm, src_hbm, dst_hbm, out_hbm, src_v, dst_v, rows_v, g_sem, s_sem):
        wid = lax.axis_index("s") * SC_CORES + lax.axis_index("c")
        base = wid * per_worker

        def offset(j):
            return pl.multiple_of(base + j * chunk, 8)

        def gather(b):
            return pltpu.make_async_copy(table_hbm.at[src_v[b]], rows_v[b], g_sem[b])

        def start_gather(j, b):
            pltpu.sync_copy(src_hbm.at[pl.ds(offset(j), chunk)], src_v[b])
            gather(b).start()

        def write_out(j, b):
            if dst_hbm is None:
                pltpu.sync_copy(rows_v[b], out_hbm.at[pl.ds(offset(j), chunk)])
            else:
                pltpu.sync_copy(dst_hbm.at[pl.ds(offset(j), chunk)], dst_v[b])
                pltpu.async_copy(rows_v[b], out_hbm.at[dst_v[b]], s_sem[b]).wait()

        start_gather(0, 0)

        @pl.loop(0, n_chunks, step=2)
        def _(j):
            for b in (0, 1):
                @pl.when(j + b + 1 < n_chunks)
                def _():
                    start_gather(j + b + 1, 1 - b)
                gather(b).wait()
                write_out(j + b, b)

    if dst is None:
        @functools.partial(pl.kernel, mesh=mesh, out_type=jax.ShapeDtypeStruct((n_out, D), table.dtype),
                           scratch_types=[idx_t, idx_t, row_t, row_t, sem_t, sem_t])
        def gather_kernel(table_hbm, src_hbm, out_hbm, s0, s1, r0, r1, g0, g1):
            body(table_hbm, src_hbm, None, out_hbm, (s0, s1), None, (r0, r1), (g0, g1), None)
        return gather_kernel(table, src)

    @functools.partial(pl.kernel, mesh=mesh, out_type=jax.ShapeDtypeStruct((n_out, D), table.dtype),
                       scratch_types=[idx_t, idx_t, idx_t, idx_t, row_t, row_t, sem_t, sem_t, sem_t, sem_t])
    def permute_kernel(table_hbm, src_hbm, dst_hbm, out_hbm, s0, s1, d0, d1, r0, r1, g0, g1, w0, w1):
        body(table_hbm, src_hbm, dst_hbm, out_hbm, (s0, s1), (d0, d1), (r0, r1), (g0, g1), (w0, w1))
    return permute_kernel(table, src, dst)


def _sc_gather(table, idx, chunk):
    return _sc_move_rows(table, idx, None, idx.shape[0], chunk)


def _sc_permute(table, src, dst, n_out, chunk):
    return _sc_move_rows(table, src, dst, n_out, chunk)


def _tile_layout(counts, bm, n_tiles):
    n_groups = counts.shape[0]
    padded = (counts + bm - 1) // bm * bm
    pad_end = jnp.cumsum(padded)
    tile_start = jnp.arange(n_tiles, dtype=jnp.int32) * bm
    tile_group = jnp.minimum(
        jnp.sum((pad_end[None, :] <= tile_start[:, None]).astype(jnp.int32), axis=1), n_groups - 1)
    n_used = (pad_end[-1] // bm).astype(jnp.int32).reshape(1)
    return pad_end - padded, tile_group.astype(jnp.int32), n_used


def _null_partial(rows):
    lane = lax.broadcasted_iota(jnp.int32, (rows, PART_W), 1)
    return jnp.where(lane < ATTN_HEAD_DIM, 0.0, -jnp.inf).astype(F32)


def _moba_sel_kernel(q_ref, km_ref, k_ref, v_ref, idx_ref, cnt_ref, own_ref, cnt_acc, *, n_blocks):
    j = pl.program_id(1)
    T = MOBA_BLOCK
    heads = range(ATTN_HEADS)
    qs = [q_ref[hd // 2] for hd in heads]
    gates = [_dot_nt(km_ref[hd], qs[hd], precision=HIGHEST) for hd in heads]
    blk = lax.broadcasted_iota(jnp.int32, gates[0].shape, 0)
    neg_inf = jnp.float32(-jnp.inf)
    gates = [jnp.where(blk < j, g, neg_inf) for g in gates]
    picks = [[] for _ in heads]
    for _ in range(MOBA_TOPK):
        ms = [jnp.max(g, axis=0, keepdims=True) for g in gates]
        firsts = [jnp.min(jnp.where(g == m, blk, n_blocks), axis=0, keepdims=True) for g, m in zip(gates, ms)]
        for hd in heads:
            picks[hd].append(jnp.where(ms[hd] > neg_inf, firsts[hd], -1))
        gates = [jnp.where(blk == f, neg_inf, g) for g, f in zip(gates, firsts)]

    @pl.when(j == 0)
    def _():
        cnt_acc[...] = jnp.zeros_like(cnt_acc)

    earlier = (lax.broadcasted_iota(jnp.int32, (T, T), 0) < lax.broadcasted_iota(jnp.int32, (T, T), 1)).astype(BF16)
    for hd in heads:
        onehots = [(blk == p).astype(F32) for p in picks[hd]]
        member = onehots[0] + onehots[1] + onehots[2]
        base = cnt_acc[hd] + _dot(member.astype(BF16), earlier)
        ranks = [jnp.sum(oh * base, axis=0, keepdims=True).astype(jnp.int32) for oh in onehots]
        idx_ref[hd] = jnp.concatenate(picks[hd] + ranks + [jnp.zeros((2, T), jnp.int32)], axis=0)
        total = cnt_acc[hd] + jnp.sum(member, axis=1, keepdims=True)
        cnt_acc[hd] = total
        cnt_ref[hd] = total.astype(jnp.int32)
    causal = lax.broadcasted_iota(jnp.int32, (T, T), 1) <= lax.broadcasted_iota(jnp.int32, (T, T), 0)
    scale = ATTN_HEAD_DIM ** -0.5
    ss = [jnp.where(causal, _dot_nt((qs[hd] * scale).astype(BF16), k_ref[hd]), neg_inf) for hd in heads]
    mx = [jnp.max(s, axis=1, keepdims=True) for s in ss]
    ps = [jnp.exp(s - m) for s, m in zip(ss, mx)]
    ls = [jnp.sum(p, axis=1, keepdims=True) for p in ps]
    accs = [_dot(ps[hd].astype(BF16), v_ref[hd]) for hd in heads]
    for hd in heads:
        lse = jnp.broadcast_to(mx[hd] + jnp.log(ls[hd]), (T, PART_W - ATTN_HEAD_DIM))
        own_ref[hd] = jnp.concatenate([accs[hd] / ls[hd], lse], axis=1)


def _moba_sel(aq, kmean, ak, av):
    B, H, S, hd = av.shape
    nb = S // MOBA_BLOCK
    T = MOBA_BLOCK
    blk = lambda b, j: (b, 0, j, 0)
    return pl.pallas_call(
        functools.partial(_moba_sel_kernel, n_blocks=nb),
        out_shape=(jax.ShapeDtypeStruct((B, H, 8, S), jnp.int32),
                   jax.ShapeDtypeStruct((B, H, nb, 1), jnp.int32),
                   jax.ShapeDtypeStruct((B, H, S, PART_W), F32)),
        grid=(B, nb),
        in_specs=[pl.BlockSpec((None, H // 2, T, 128), blk),
                  pl.BlockSpec((None, H, nb, 128), lambda b, j: (b, 0, 0, 0)),
                  pl.BlockSpec((None, H, T, 128), blk),
                  pl.BlockSpec((None, H, T, hd), blk)],
        out_specs=(pl.BlockSpec((None, H, 8, T), lambda b, j: (b, 0, 0, j)),
                   pl.BlockSpec((None, H, nb, 1), lambda b, j: (b, 0, 0, 0)),
                   pl.BlockSpec((None, H, T, PART_W), blk)),
        scratch_shapes=[pltpu.VMEM((H, nb, 1), F32)],
        compiler_params=pltpu.CompilerParams(
            dimension_semantics=("arbitrary", "arbitrary"), vmem_limit_bytes=VMEM_LIMIT),
        name="moba_sel",
    )(aq, kmean, ak, av)


def _moba_blk_kernel(tg_ref, nu_ref, q_ref, k_ref, v_ref, o_ref, *, n_blocks):
    n = MOBA_TILES_PER_STEP
    R = MOBA_ROWS
    t0 = pl.program_id(0) * n

    @pl.when(t0 < nu_ref[0])
    def _():
        scale = ATTN_HEAD_DIM ** -0.5
        groups = [tg_ref[t0 + j] for j in range(n)]
        kv_rows = [(g // n_blocks, pl.ds(pl.multiple_of((g % n_blocks) * MOBA_BLOCK, MOBA_BLOCK), MOBA_BLOCK))
                   for g in groups]
        ss = [_dot_nt((q_ref[j * R:(j + 1) * R, :] * scale).astype(BF16), k_ref[kv_rows[j][0], kv_rows[j][1], :])
              for j in range(n)]
        ms = [jnp.max(s, axis=1, keepdims=True) for s in ss]
        ps = [jnp.exp(s - m) for s, m in zip(ss, ms)]
        ls = [jnp.sum(p, axis=1, keepdims=True) for p in ps]
        accs = [_dot(p.astype(BF16), v_ref[kv_rows[j][0], kv_rows[j][1], :]) for j, p in enumerate(ps)]
        null = _null_partial(R)
        for j in range(n):
            lse = jnp.broadcast_to(ms[j] + jnp.log(ls[j]), (R, PART_W - ATTN_HEAD_DIM))
            row = jnp.concatenate([accs[j] / ls[j], lse], axis=1)
            o_ref[j * R:(j + 1) * R, :] = jnp.where(t0 + j < nu_ref[0], row, null)

    @pl.when(t0 >= nu_ref[0])
    def _():
        o_ref[...] = _null_partial(n * R)


def _moba_blk(qs, tile_group, n_used, ak, av):
    B, H, S, hd = av.shape
    nb = S // MOBA_BLOCK
    R = MOBA_ROWS
    n = MOBA_TILES_PER_STEP
    n_tiles = qs.shape[0] // R
    assert n_tiles % n == 0
    whole = lambda i, tg, nu: (0, 0, 0)
    grid_spec = pltpu.PrefetchScalarGridSpec(
        num_scalar_prefetch=2,
        grid=(n_tiles // n,),
        in_specs=[pl.BlockSpec((n * R, 128), lambda i, tg, nu: (i, 0)),
                  pl.BlockSpec((B * H, S, 128), whole, pipeline_mode=pl.Buffered(1)),
                  pl.BlockSpec((B * H, S, hd), whole, pipeline_mode=pl.Buffered(1))],
        out_specs=pl.BlockSpec((n * R, PART_W), lambda i, tg, nu: (i, 0)),
    )
    return pl.pallas_call(
        functools.partial(_moba_blk_kernel, n_blocks=nb),
        out_shape=jax.ShapeDtypeStruct((n_tiles * R, PART_W), F32),
        grid_spec=grid_spec,
        compiler_params=pltpu.CompilerParams(
            dimension_semantics=("arbitrary",), vmem_limit_bytes=VMEM_LIMIT),
        name="moba_blk",
    )(tile_group, n_used, qs, ak.reshape(B * H, S, 128), av.reshape(B * H, S, hd))


def _moba_merge_kernel(own_ref, pg_ref, g_ref, o_ref):
    hd = ATTN_HEAD_DIM
    rows = [own_ref[...]] + [pg_ref[s] for s in range(MOBA_TOPK)]
    lses = [pltpu.roll(r, hd, 1) for r in rows]
    top = lses[0]
    for z in lses[1:]:
        top = jnp.maximum(top, z)
    num = jnp.zeros_like(top)
    den = jnp.zeros_like(top)
    for r, z in zip(rows, lses):
        w = jnp.exp(z - top)
        num = num + w * r
        den = den + w
    o = (num / den)[:, :hd]
    ms = jnp.mean(o * o, axis=-1, keepdims=True)
    o_ref[...] = o * lax.rsqrt(ms + RMS_EPS) * g_ref[...]


def _moba_merge(own, pg, norm_g):
    n = own.shape[0]
    T = 2048
    row = lambda i: (i, 0)
    return pl.pallas_call(
        _moba_merge_kernel,
        out_shape=jax.ShapeDtypeStruct((n, ATTN_HEAD_DIM), F32),
        grid=(n // T,),
        in_specs=[pl.BlockSpec((T, PART_W), row),
                  pl.BlockSpec((MOBA_TOPK, T, PART_W), lambda i: (0, i, 0)),
                  pl.BlockSpec((1, ATTN_HEAD_DIM), lambda i: (0, 0))],
        out_specs=pl.BlockSpec((T, ATTN_HEAD_DIM), row),
        compiler_params=pltpu.CompilerParams(
            dimension_semantics=("arbitrary",), vmem_limit_bytes=VMEM_LIMIT),
        name="moba_merge",
    )(own, pg, norm_g)


def _moba(aq, km, ak, av, norm_g):
    B, H, S, hd = av.shape
    nb = S // MOBA_BLOCK
    n_q = B * H * S
    kmp = km.reshape(B, nb, H // 2, 128)
    half = jnp.arange(128, dtype=jnp.int32) // hd
    kmean = jnp.stack([jnp.where(half == h % 2, kmp[:, :, h // 2, :], 0.0) for h in range(H)], axis=1)
    idx8, counts, own = _moba_sel(aq, kmean, ak, av)
    sel = idx8[:, :, 0:MOBA_TOPK, :].reshape(B * H, MOBA_TOPK, S)
    rank = idx8[:, :, MOBA_TOPK:2 * MOBA_TOPK, :].reshape(B * H, MOBA_TOPK, S)
    n_groups = B * H * nb
    n_tiles = (n_q * MOBA_TOPK) // MOBA_ROWS + n_groups
    pad_start, tile_group, n_used = _tile_layout(counts.reshape(-1), MOBA_ROWS, n_tiles)
    blocks = jnp.arange(nb, dtype=jnp.int32)[:, None]
    start = jnp.sum(jnp.where(sel[:, :, None, :] == blocks, pad_start.reshape(B * H, 1, nb, 1), 0), axis=2)
    a_ids = jnp.arange(n_q * MOBA_TOPK, dtype=jnp.int32).reshape(B * H, MOBA_TOPK, S)
    assert n_tiles * MOBA_ROWS >= n_q * MOBA_TOPK + MOBA_ROWS
    pos = jnp.where(sel >= 0, start + rank, n_used[0] * MOBA_ROWS + a_ids % MOBA_ROWS)
    bh = jnp.arange(B * H, dtype=jnp.int32)[:, None, None]
    t = jnp.arange(S, dtype=jnp.int32)[None, None, :]
    pair_row = jnp.broadcast_to((bh // H * (H // 2) + bh % H // 2) * S + t, pos.shape)
    qs = _sc_permute(aq.reshape(B * (H // 2) * S, 128), pair_row.reshape(-1), pos.reshape(-1),
                     n_tiles * MOBA_ROWS, 256)
    parts = _moba_blk(qs, tile_group, n_used, ak, av)
    pg = _sc_gather(parts, pos.transpose(1, 0, 2).reshape(-1), 256)
    o = _moba_merge(own.reshape(n_q, PART_W), pg.reshape(MOBA_TOPK, n_q, PART_W), norm_g)
    return o.reshape(B, H, S, hd)


def _mix_kernel(oh_ref, oa_ref, x_ref, g1_ref, sc2_ref, sh2_ref, n2_ref, wo_ref, wr_ref, br_ref,
                x1_ref, h2_ref, gw_ref, idx_ref, cnt_ref, cnt_acc):
    cat = jnp.concatenate([oh_ref[...]] + [oa_ref[hd] for hd in range(ATTN_HEADS)], axis=1)
    mix = _dot(cat.astype(BF16), wo_ref[...])
    x1 = x_ref[...] + g1_ref[...] * mix
    x1_ref[...] = x1
    ms = jnp.mean(x1 * x1, axis=-1, keepdims=True)
    h2 = x1 * lax.rsqrt(ms + RMS_EPS) * n2_ref[...]
    h2 = h2 * (1.0 + sc2_ref[...]) + sh2_ref[...]
    h2_ref[...] = _pack_bf16_pairs(h2)
    E = N_EXPERTS
    tm = h2.shape[0]
    h_0 = h2.astype(BF16)
    r_1 = h2 - h_0.astype(F32)
    h_1 = r_1.astype(BF16)
    h_2 = (r_1 - h_1.astype(F32)).astype(BF16)
    wt = wr_ref[...]
    p_0 = _dot_nt(wt, h_0)
    p_1 = _dot_nt(wt[:2 * E], h_1)
    p_2 = _dot_nt(wt[:E], h_2)
    logits = (p_0[:E] + (p_0[E:2 * E] + p_1[:E]) + (p_0[2 * E:] + p_1[E:] + p_2)) + br_ref[...]
    ex = lax.broadcasted_iota(jnp.int32, logits.shape, 0)
    neg_inf = jnp.float32(-jnp.inf)
    vals, idxs = [], []
    for _ in range(TOP_K):
        m = jnp.max(logits, axis=0, keepdims=True)
        first = jnp.min(jnp.where(logits == m, ex, E), axis=0, keepdims=True)
        vals.append(m)
        idxs.append(first)
        logits = jnp.where(ex == first, neg_inf, logits)
    e = [jnp.exp(v - vals[0]) for v in vals]
    denom = e[0] + e[1] + e[2] + e[3]
    gate_rows = jnp.concatenate([ei / denom for ei in e] + [jnp.zeros((128 - TOP_K, tm), F32)], axis=0)
    gw_ref[...] = gate_rows.T[:, :TOP_K]

    @pl.when((pl.program_id(0) == 0) & (pl.program_id(1) == 0))
    def _():
        cnt_acc[...] = jnp.zeros_like(cnt_acc)

    earlier = (lax.broadcasted_iota(jnp.int32, (tm, tm), 0) < lax.broadcasted_iota(jnp.int32, (tm, tm), 1)).astype(BF16)
    onehots = [(ex == ix).astype(F32) for ix in idxs]
    member = onehots[0] + onehots[1] + onehots[2] + onehots[3]
    base = cnt_acc[...] + _dot(member.astype(BF16), earlier)
    ranks = [jnp.sum(oh * base, axis=0, keepdims=True).astype(jnp.int32) for oh in onehots]
    idx_ref[...] = jnp.concatenate(idxs + ranks, axis=0)
    total = cnt_acc[...] + jnp.sum(member, axis=1, keepdims=True)
    cnt_acc[...] = total
    cnt_ref[...] = total.astype(jnp.int32)


def _mix(oh, oa, x, gate1, scale2, shift2, norm2_g, w_out_bf16, w_router, b_router, b0):
    _, S, D = x.shape
    B = oh.shape[0]
    hw = oh.shape[-1]
    tm = MIX_ROWS
    row = lambda b, i: (b, i, 0)
    xrow = lambda b, i: (b0 + b, i, 0)
    vec = lambda b, i: (b, 0, 0)
    const = lambda b, i: (0, 0)
    return pl.pallas_call(
        _mix_kernel,
        out_shape=(jax.ShapeDtypeStruct((B, S, D), F32),
                   jax.ShapeDtypeStruct((B, S, D // 2), jnp.int32),
                   jax.ShapeDtypeStruct((B, S, TOP_K), F32),
                   jax.ShapeDtypeStruct((B, 2 * TOP_K, S), jnp.int32),
                   jax.ShapeDtypeStruct((N_EXPERTS, 1), jnp.int32)),
        grid=(B, S // tm),
        in_specs=[pl.BlockSpec((None, tm, hw), row),
                  pl.BlockSpec((None, ATTN_HEADS, tm, ATTN_HEAD_DIM), lambda b, i: (b, 0, i, 0)),
                  pl.BlockSpec((None, tm, D), xrow),
                  pl.BlockSpec((None, 1, D), vec),
                  pl.BlockSpec((None, 1, D), vec),
                  pl.BlockSpec((None, 1, D), vec),
                  pl.BlockSpec((1, D), const),
                  pl.BlockSpec((D, D), const),
                  pl.BlockSpec((3 * N_EXPERTS, D), const),
                  pl.BlockSpec((N_EXPERTS, 1), const)],
        out_specs=(pl.BlockSpec((None, tm, D), row),
                   pl.BlockSpec((None, tm, D // 2), row),
                   pl.BlockSpec((None, tm, TOP_K), row),
                   pl.BlockSpec((None, 2 * TOP_K, tm), lambda b, i: (b, 0, i)),
                   pl.BlockSpec((N_EXPERTS, 1), const)),
        scratch_shapes=[pltpu.VMEM((N_EXPERTS, 1), F32)],
        compiler_params=pltpu.CompilerParams(
            dimension_semantics=("arbitrary", "arbitrary"), vmem_limit_bytes=VMEM_LIMIT),
        name="mix",
    )(oh, oa, x, gate1, scale2, shift2, norm2_g, w_out_bf16, w_router, b_router)


def _moe_rows_kernel(be_ref, nx_ref, nu_ref, x_ref, wgu_hbm, bgu_ref, wd_hbm, bd_ref, y_ref,
                     wgu32, wd32, wgu16, wd16, sem, *, d_ff):
    i = pl.program_id(0)
    e = be_ref[i]

    def fetch(expert):
        return (pltpu.make_async_copy(wgu_hbm.at[expert], wgu32, sem.at[0]),
                pltpu.make_async_copy(wd_hbm.at[expert], wd32, sem.at[1]))

    @pl.when(i == 0)
    def _():
        for c in fetch(e):
            c.start()

    @pl.when((i == 0) | (e != be_ref[jnp.maximum(i - 1, 0)]))
    def _():
        for c in fetch(e):
            c.wait()
        wgu16[...] = wgu32[...].astype(BF16)
        wd16[...] = wd32[...].astype(BF16)

        @pl.when(nx_ref[i] >= 0)
        def _():
            for c in fetch(nx_ref[i]):
                c.start()

    @pl.when(i < nu_ref[0])
    def _():
        gu = _dot(_unpack_bf16_pairs(x_ref[...]).astype(BF16), wgu16[...]) + bgu_ref[...]
        gate = jnp.minimum(gu[:, :d_ff], SWIGLU_LIMIT)
        up = jnp.clip(gu[:, d_ff:], -SWIGLU_LIMIT, SWIGLU_LIMIT)
        act = (up + 1.0) * gate * _sigmoid(SWIGLU_ALPHA * gate)
        y_ref[...] = _pack_bf16_pairs(_dot(act.astype(BF16), wd16[...]) + bd_ref[...])

    @pl.when(i >= nu_ref[0])
    def _():
        y_ref[...] = jnp.zeros_like(y_ref)


def _moe_rows(xs, blk_expert, n_used, wgu, bgu, wd, bd):
    D = 2 * xs.shape[1]
    bm = MOE_ROWS
    n_blk = xs.shape[0] // bm
    d_ff = wd.shape[1]
    run_end = jnp.sum((blk_expert[None, :] <= blk_expert[:, None]).astype(jnp.int32), axis=1)
    next_expert = jnp.where(run_end < n_blk, blk_expert[jnp.minimum(run_end, n_blk - 1)], -1).astype(jnp.int32)
    bsel = lambda i, be, nx, nu: (be[i], 0, 0)
    rows = lambda i, be, nx, nu: (i, 0)
    grid_spec = pltpu.PrefetchScalarGridSpec(
        num_scalar_prefetch=3,
        grid=(n_blk,),
        in_specs=[pl.BlockSpec((bm, D // 2), rows),
                  pl.BlockSpec(memory_space=pl.ANY),
                  pl.BlockSpec((None, 1, 2 * d_ff), bsel),
                  pl.BlockSpec(memory_space=pl.ANY),
                  pl.BlockSpec((None, 1, D), bsel)],
        out_specs=pl.BlockSpec((bm, D // 2), rows),
        scratch_shapes=[pltpu.VMEM((D, 2 * d_ff), F32), pltpu.VMEM((d_ff, D), F32),
                        pltpu.VMEM((D, 2 * d_ff), BF16), pltpu.VMEM((d_ff, D), BF16),
                        pltpu.SemaphoreType.DMA((2,))],
    )
    return pl.pallas_call(
        functools.partial(_moe_rows_kernel, d_ff=d_ff),
        out_shape=jax.ShapeDtypeStruct((n_blk * bm, D // 2), jnp.int32),
        grid_spec=grid_spec,
        compiler_params=pltpu.CompilerParams(
            dimension_semantics=("arbitrary",), vmem_limit_bytes=VMEM_LIMIT),
        name="moe_rows",
    )(blk_expert, next_expert, n_used, xs, wgu, bgu.reshape(N_EXPERTS, 1, 2 * d_ff), wd, bd.reshape(N_EXPERTS, 1, D))


def _combine_rows_kernel(*refs):
    y_refs = refs[:TOP_K]
    gw_ref, x1_ref, g2_ref, fg_ref = refs[TOP_K:TOP_K + 4]
    o_ref = refs[-1]
    gw = gw_ref[...]
    y = gw[:, 0:1] * _unpack_bf16_pairs(y_refs[0][...])
    for kk in range(1, TOP_K):
        y = y + gw[:, kk:kk + 1] * _unpack_bf16_pairs(y_refs[kk][...])
    x2 = x1_ref[...] + g2_ref[...] * y
    ms = jnp.mean(x2 * x2, axis=-1, keepdims=True)
    o_ref[...] = x2 * lax.rsqrt(ms + RMS_EPS) * fg_ref[...]


def _combine_rows(yg, gates, x1, gate2, final_g, out_so_far, b0, n_batches):
    S, D = x1.shape
    tm = 512
    steps = S // tm
    slot_spec = lambda kk: pl.BlockSpec((tm, D // 2), lambda i: (kk * steps + i, 0))
    in_specs = [slot_spec(kk) for kk in range(TOP_K)] + [
        pl.BlockSpec((tm, TOP_K), lambda i: (i, 0)),
        pl.BlockSpec((tm, D), lambda i: (i, 0)),
        pl.BlockSpec((1, D), lambda i: (0, 0)),
        pl.BlockSpec((1, D), lambda i: (0, 0))]
    args = [yg] * TOP_K + [gates, x1, gate2, final_g]
    aliases = {}
    if out_so_far is not None:
        in_specs.append(pl.BlockSpec(memory_space=pl.ANY))
        aliases = {len(args): 0}
        args.append(out_so_far)
    return pl.pallas_call(
        _combine_rows_kernel,
        out_shape=jax.ShapeDtypeStruct((n_batches * S, D), F32),
        grid=(steps,),
        in_specs=in_specs,
        out_specs=pl.BlockSpec((tm, D), lambda i: (b0 * steps + i, 0)),
        input_output_aliases=aliases,
        compiler_params=pltpu.CompilerParams(
            dimension_semantics=("arbitrary",), vmem_limit_bytes=VMEM_LIMIT),
        name="combine_rows",
    )(*args)


def _split_bf16x3(w):
    def top(v):
        return lax.bitcast_convert_type(lax.bitcast_convert_type(v, jnp.int32) & jnp.int32(-65536), F32)
    w0 = top(w)
    w1 = top(w - w0)
    w2 = w - w0 - w1
    return jnp.concatenate([w0, w1, w2], axis=1).astype(BF16).T


def _rotary_tables(positions):
    half = ROT_DIM // 2
    inv_freq = jnp.exp(-math.log(ROPE_THETA) * jnp.arange(0, ROT_DIM, 2, dtype=F32) / ROT_DIM)
    d = jnp.arange(128, dtype=jnp.int32) % ATTN_HEAD_DIM
    freq = jnp.where(d < ROT_DIM, inv_freq[d % half], 0.0)
    sign = jnp.where(d < half, -1.0, 1.0)
    ang = positions.astype(F32)[:, :, None] * freq
    return jnp.cos(ang), jnp.sin(ang) * sign


def kernel(x, c, positions, w_ada, b_ada, norm1_g, w_in, hgrn_lb_logits, hgrn_norm_g, attn_norm_g,
           w_out, norm2_g, w_router, b_router, w_gate_up, b_gate_up, w_down, b_down, final_norm_g):
    B, S, D = x.shape
    assert w_in.shape[0] == 1, "single-layer block: the final norm is fused into the combine step"
    l = 0
    ctab, stab = _rotary_tables(positions)
    lower_bounds = jnp.cumsum(jax.nn.softmax(hgrn_lb_logits.astype(F32), axis=0), axis=0)
    mod = _ada(c, w_ada[l], b_ada[l])
    shift1, scale1, gate1, shift2, scale2, gate2 = jnp.split(mod[:, None, :], N_MOD, axis=-1)
    n_blk = (S * TOP_K) // MOE_ROWS + N_EXPERTS
    experts = jnp.arange(N_EXPERTS, dtype=jnp.int32)[:, None]
    tok = jnp.broadcast_to(jnp.arange(S, dtype=jnp.int32)[None, :], (TOP_K, S)).reshape(-1)
    out = None
    for b in range(B):
        one = slice(b, b + 1)
        q, k, lf, v, gt, aq, ak, av, km = _proj(
            x, scale1[one], shift1[one], norm1_g[l][None], w_in[l].astype(BF16), lower_bounds[l][None],
            ctab, stab, b)
        o_a = _moba(aq, km, ak, av, attn_norm_g[l][None])
        o_h = _hgrn(q, k, lf, v, gt, hgrn_norm_g[l][None])
        x1, h2, gates, idx8, counts = _mix(
            o_h, o_a, x, gate1[one], scale2[one], shift2[one], norm2_g[l][None], w_out[l].astype(BF16),
            _split_bf16x3(w_router[l]), b_router[l][:, None], b)
        pad_start, blk_expert, n_used = _tile_layout(counts.reshape(-1), MOE_ROWS, n_blk)
        chosen = idx8[0, 0:TOP_K, :]
        pos = jnp.sum(jnp.where(chosen[:, None, :] == experts, pad_start[:, None], 0), axis=1) + idx8[0, TOP_K:, :]
        xs = _sc_permute(h2.reshape(S, D // 2), tok, pos.reshape(-1), n_blk * MOE_ROWS, 64)
        y_sorted = _moe_rows(xs, blk_expert, n_used, w_gate_up[l], b_gate_up[l], w_down[l], b_down[l])
        yg = _sc_gather(y_sorted, pos.reshape(-1), 64)
        out = _combine_rows(yg, gates.reshape(S, TOP_K), x1.reshape(S, D), gate2[b], final_norm_g[None], out, b, B)
    return out.reshape(B, S, D)
```

```python
import functools
import math

import jax
import jax.numpy as jnp
from jax import lax
from jax.experimental import pallas as pl
from jax.experimental.pallas import tpu as pltpu
from jax.experimental.pallas import tpu_sc as plsc

F32 = jnp.float32
BF16 = jnp.bfloat16
HIGHEST = lax.Precision.HIGHEST

HGRN_DK = 128
HGRN_CHUNK = 64
ATTN_HEADS = 4
ATTN_HEAD_DIM = 64
ROT_DIM = ATTN_HEAD_DIM // 4
ROPE_THETA = 500000.0
MOBA_BLOCK = 256
MOBA_TOPK = 3
N_EXPERTS = 32
TOP_K = 4
SWIGLU_ALPHA = 1.702
SWIGLU_LIMIT = 7.0
N_MOD = 6
RMS_EPS = 1e-6

HGRN_SUB = 16
EXP_CLAMP = 80.0
MERGE_ROWS = 4096
COMBINE_ROWS = 1024
MIX_ROWS = 1024
MOE_ROWS = 512
MOBA_ROWS = 256
MOBA_TILES_PER_STEP = 8
PART_W = 128
V7X_VMEM_BYTES = 64 * 1024 * 1024
VMEM_LIMIT = V7X_VMEM_BYTES * 7 // 8
SC_CORES = 2
SC_SUBCORES = 16


def _sigmoid(x):
    return 1.0 / (1.0 + jnp.exp(-x))


def _dot(a, b, **kw):
    return jnp.dot(a, b, preferred_element_type=F32, **kw)


def _dot_nt(a, b, **kw):
    return lax.dot_general(a, b, (((1,), (1,)), ((), ())), preferred_element_type=F32, **kw)


def _pack_bf16_pairs(x):
    w = x.shape[1] // 2
    bits = lax.bitcast_convert_type(x.astype(BF16).astype(F32), jnp.int32)
    return bits[:, w:] | lax.shift_right_logical(bits[:, :w], 16)


def _unpack_bf16_pairs(p):
    lo = lax.bitcast_convert_type(lax.shift_left(p, 16), F32)
    hi = lax.bitcast_convert_type(p & jnp.int32(-65536), F32)
    return jnp.concatenate([lo, hi], axis=1)


def _ada_kernel(c_ref, w_ref, b_ref, o_ref):
    c = c_ref[...]
    o_ref[...] = _dot(c * _sigmoid(c), w_ref[...], precision=HIGHEST) + b_ref[...]


def _ada(c, w_ada, b_ada):
    B, D = c.shape
    N = w_ada.shape[1]
    tn = N // 4
    c8 = jnp.zeros((8, D), F32).at[:B].set(c)
    out = pl.pallas_call(
        _ada_kernel,
        out_shape=jax.ShapeDtypeStruct((8, N), F32),
        grid=(N // tn,),
        in_specs=[pl.BlockSpec((8, D), lambda j: (0, 0)),
                  pl.BlockSpec((D, tn), lambda j: (0, j)),
                  pl.BlockSpec((1, tn), lambda j: (0, j))],
        out_specs=pl.BlockSpec((8, tn), lambda j: (0, j)),
        compiler_params=pltpu.CompilerParams(vmem_limit_bytes=VMEM_LIMIT),
        name="ada",
    )(c8, w_ada, b_ada.reshape(1, N))
    return out[:B]


def _proj_kernel(x_ref, sc_ref, sh_ref, g_ref, w_ref, lb_ref, ct_ref, st_ref,
                 q_ref, k_ref, lf_ref, v_ref, gt_ref, aq_ref, ak_ref, av_ref, km_ref,
                 *, hw, aw):
    x = x_ref[...]
    ms = jnp.mean(x * x, axis=-1, keepdims=True)
    h = x * lax.rsqrt(ms + RMS_EPS) * g_ref[...]
    h = h * (1.0 + sc_ref[...]) + sh_ref[...]
    proj = _dot(h.astype(BF16), w_ref[...])

    hq = proj[:, 0:hw]
    hf = proj[:, hw:2 * hw]
    hg = proj[:, 3 * hw:4 * hw]
    q_ref[...] = (hq * _sigmoid(hq) * (HGRN_DK ** -0.5)).astype(BF16)
    lb = lb_ref[...]
    f = lb + (1.0 - lb) * _sigmoid(hf)
    k_ref[...] = (1.0 - f).astype(BF16)
    lf_ref[...] = jnp.log(f)
    v_ref[...] = proj[:, 2 * hw:3 * hw].astype(BF16)
    gt_ref[...] = (hg * _sigmoid(hg)).astype(BF16)

    ct = jnp.concatenate([ct_ref[...]] * (aw // 128), axis=1)
    st = jnp.concatenate([st_ref[...]] * (aw // 128), axis=1)
    lane = lax.broadcasted_iota(jnp.int32, ct.shape, 1) % ATTN_HEAD_DIM
    first_half = lane < (ROT_DIM // 2)

    def rot(t):
        partner = jnp.where(first_half, pltpu.roll(t, aw - ROT_DIM // 2, 1), pltpu.roll(t, ROT_DIM // 2, 1))
        return t * ct + partner * st

    base = 4 * hw
    aq = rot(proj[:, base:base + aw])
    ak = rot(proj[:, base + aw:base + 2 * aw])
    av = proj[:, base + 2 * aw:base + 3 * aw]
    km_ref[...] = jnp.mean(ak, axis=0, keepdims=True)
    lane128 = lax.broadcasted_iota(jnp.int32, (x.shape[0], 128), 1)
    for pair in range(ATTN_HEADS // 2):
        aq_ref[pair] = aq[:, pair * 128:(pair + 1) * 128]
    for hd in range(ATTN_HEADS):
        pair, half = divmod(hd, 2)
        in_head = (lane128 // ATTN_HEAD_DIM) == half
        ak_ref[hd] = jnp.where(in_head, ak[:, pair * 128:(pair + 1) * 128], 0.0).astype(BF16)
        av_ref[hd] = av[:, hd * ATTN_HEAD_DIM:(hd + 1) * ATTN_HEAD_DIM].astype(BF16)


def _proj(x, scale1, shift1, norm_g, w_in_bf16, lb, ctab, stab, b0):
    _, S, D = x.shape
    B = scale1.shape[0]
    hw = lb.shape[-1]
    aw = ATTN_HEADS * ATTN_HEAD_DIM
    tm = MOBA_BLOCK
    nb = S // MOBA_BLOCK
    n_proj = w_in_bf16.shape[1]
    row = lambda b, i: (b, i, 0)
    xrow = lambda b, i: (b0 + b, i, 0)
    vec = lambda b, i: (b, 0, 0)
    head = lambda b, i: (b, 0, i, 0)
    out_shapes = (
        jax.ShapeDtypeStruct((B, S, hw), BF16),
        jax.ShapeDtypeStruct((B, S, hw), BF16),
        jax.ShapeDtypeStruct((B, S, hw), F32),
        jax.ShapeDtypeStruct((B, S, hw), BF16),
        jax.ShapeDtypeStruct((B, S, hw), BF16),
        jax.ShapeDtypeStruct((B, ATTN_HEADS // 2, S, 128), F32),
        jax.ShapeDtypeStruct((B, ATTN_HEADS, S, 128), BF16),
        jax.ShapeDtypeStruct((B, ATTN_HEADS, S, ATTN_HEAD_DIM), BF16),
        jax.ShapeDtypeStruct((B, nb, 1, aw), F32),
    )
    hspec = pl.BlockSpec((None, tm, hw), row)
    aspec = pl.BlockSpec((None, ATTN_HEADS, tm, ATTN_HEAD_DIM), head)
    return pl.pallas_call(
        functools.partial(_proj_kernel, hw=hw, aw=aw),
        out_shape=out_shapes,
        grid=(B, S // tm),
        in_specs=[pl.BlockSpec((None, tm, D), xrow),
                  pl.BlockSpec((None, 1, D), vec),
                  pl.BlockSpec((None, 1, D), vec),
                  pl.BlockSpec((1, D), lambda b, i: (0, 0)),
                  pl.BlockSpec((D, n_proj), lambda b, i: (0, 0)),
                  pl.BlockSpec((1, hw), lambda b, i: (0, 0)),
                  pl.BlockSpec((None, tm, 128), xrow),
                  pl.BlockSpec((None, tm, 128), xrow)],
        out_specs=(hspec, hspec, hspec, hspec, hspec,
                   pl.BlockSpec((None, ATTN_HEADS // 2, tm, 128), head),
                   pl.BlockSpec((None, ATTN_HEADS, tm, 128), head), aspec,
                   pl.BlockSpec((None, None, 1, aw), lambda b, i: (b, i, 0, 0))),
        compiler_params=pltpu.CompilerParams(
            dimension_semantics=("arbitrary", "arbitrary"), vmem_limit_bytes=VMEM_LIMIT),
        name="proj",
    )(x, scale1, shift1, norm_g, w_in_bf16, lb, ctab, stab)


def _hgrn_kernel(q_ref, k_ref, lf_ref, v_ref, gt_ref, gn_ref, o_ref, st_ref, *, n_heads, n_chunks):
    @pl.when(pl.program_id(1) == 0)
    def _():
        st_ref[...] = jnp.zeros_like(st_ref)

    C = HGRN_CHUNK
    r = lax.broadcasted_iota(jnp.int32, (C, C), 0)
    c = lax.broadcasted_iota(jnp.int32, (C, C), 1)
    tril = c <= r
    ltri = tril.astype(F32)
    gn = gn_ref[...]

    def chunk(ci, carry):
        r0 = pl.multiple_of(ci * C, C)
        rows = pl.ds(r0, C)
        b_all = _dot(ltri, lf_ref[rows, :], precision=HIGHEST)
        heads = range(n_heads)
        sls = [slice(hd * HGRN_DK, (hd + 1) * HGRN_DK) for hd in heads]
        bs = [b_all[:, sl] for sl in sls]
        b_lasts = [b[C - 1:C, :] for b in bs]
        qs = [q_ref[rows, sl].astype(F32) for sl in sls]
        ks = [k_ref[rows, sl].astype(F32) for sl in sls]
        vs = [v_ref[rows, sl] for sl in sls]
        states = [st_ref[hd] for hd in heads]
        o_inter = [_dot_nt((qs[hd] * jnp.exp(bs[hd])).astype(BF16), states[hd].astype(BF16)) for hd in heads]
        scores = []
        for hd in heads:
            blocks = []
            for g0 in range(0, C, HGRN_SUB):
                g1 = g0 + HGRN_SUB
                rho = 0.5 * (bs[hd][g0:g0 + 1, :] + bs[hd][g1 - 1:g1, :])
                qa = qs[hd][g0:g1, :] * jnp.exp(jnp.minimum(bs[hd][g0:g1, :] - rho, EXP_CLAMP))
                kb = ks[hd] * jnp.exp(jnp.minimum(rho - bs[hd], EXP_CLAMP))
                blocks.append(_dot_nt(qa.astype(BF16), kb.astype(BF16)))
            scores.append(jnp.where(tril, jnp.concatenate(blocks, axis=0), 0.0).astype(BF16))
        outs = [o_inter[hd] + _dot(scores[hd], vs[hd]) for hd in heads]
        kds = [(ks[hd] * jnp.exp(b_lasts[hd] - bs[hd])).astype(BF16) for hd in heads]
        upds = [_dot(vs[hd].astype(F32).T.astype(BF16), kds[hd]) for hd in heads]
        for hd in heads:
            st_ref[hd] = states[hd] * jnp.exp(b_lasts[hd]) + upds[hd]
            o = outs[hd]
            ms = jnp.mean(o * o, axis=-1, keepdims=True)
            o_ref[rows, sls[hd]] = (o * lax.rsqrt(ms + RMS_EPS) * gn * gt_ref[rows, sls[hd]].astype(F32)).astype(BF16)
        return carry

    lax.fori_loop(0, n_chunks, chunk, 0, unroll=True)


def _hgrn(q, k, lf, v, gt, norm_g):
    B, S, hw = q.shape
    n_heads = hw // HGRN_DK
    tc = 512
    spec = pl.BlockSpec((None, tc, hw), lambda b, i: (b, i, 0))
    return pl.pallas_call(
        functools.partial(_hgrn_kernel, n_heads=n_heads, n_chunks=tc // HGRN_CHUNK),
        out_shape=jax.ShapeDtypeStruct((B, S, hw), BF16),
        grid=(B, S // tc),
        in_specs=[spec, spec, spec, spec, spec, pl.BlockSpec((1, HGRN_DK), lambda b, i: (0, 0))],
        out_specs=spec,
        scratch_shapes=[pltpu.VMEM((n_heads, HGRN_DK, HGRN_DK), F32)],
        compiler_params=pltpu.CompilerParams(
            dimension_semantics=("arbitrary", "arbitrary"), vmem_limit_bytes=VMEM_LIMIT),
        name="hgrn",
    )(q, k, lf, v, gt, norm_g)


def _sc_move_rows(table, src, dst, n_out, chunk):
    M = src.shape[0]
    D = table.shape[1]
    n_workers = SC_CORES * SC_SUBCORES
    per_worker = M // n_workers
    n_chunks = per_worker // chunk
    assert per_worker * n_workers == M and n_chunks * chunk == per_worker and n_chunks % 2 == 0 and chunk % 8 == 0
    mesh = plsc.VectorSubcoreMesh(core_axis_name="c", subcore_axis_name="s")
    idx_t = pltpu.VMEM((chunk,), jnp.int32)
    row_t = pltpu.VMEM((chunk, D), table.dtype)
    sem_t = pltpu.SemaphoreType.DMA

    def body(table_hbm, src_hbm, dst_hbm, out_hbm, src_v, dst_v, rows_v, g_sem, s_sem):
        wid = lax.axis_index("s") * SC_CORES + lax.axis_index("c")
        base = wid * per_worker

        def offset(j):
            return pl.multiple_of(base + j * chunk, 8)

        def gather(b):
            return pltpu.make_async_copy(table_hbm.at[src_v[b]], rows_v[b], g_sem[b])

        def start_gather(j, b):
            pltpu.sync_copy(src_hbm.at[pl.ds(offset(j), chunk)], src_v[b])
            gather(b).start()

        def write_out(j, b):
            if dst_hbm is None:
                pltpu.sync_copy(rows_v[b], out_hbm.at[pl.ds(offset(j), chunk)])
            else:
                pltpu.sync_copy(dst_hbm.at[pl.ds(offset(j), chunk)], dst_v[b])
                pltpu.async_copy(rows_v[b], out_hbm.at[dst_v[b]], s_sem[b]).wait()

        start_gather(0, 0)

        @pl.loop(0, n_chunks, step=2)
        def _(j):
            for b in (0, 1):
                @pl.when(j + b + 1 < n_chunks)
                def _():
                    start_gather(j + b + 1, 1 - b)
                gather(b).wait()
                write_out(j + b, b)

    if dst is None:
        @functools.partial(pl.kernel, mesh=mesh, out_type=jax.ShapeDtypeStruct((n_out, D), table.dtype),
                           scratch_types=[idx_t, idx_t, row_t, row_t, sem_t, sem_t])
        def gather_kernel(table_hbm, src_hbm, out_hbm, s0, s1, r0, r1, g0, g1):
            body(table_hbm, src_hbm, None, out_hbm, (s0, s1), None, (r0, r1), (g0, g1), None)
        return gather_kernel(table, src)

    @functools.partial(pl.kernel, mesh=mesh, out_type=jax.ShapeDtypeStruct((n_out, D), table.dtype),
                       scratch_types=[idx_t, idx_t, idx_t, idx_t, row_t, row_t, sem_t, sem_t, sem_t, sem_t])
    def permute_kernel(table_hbm, src_hbm, dst_hbm, out_hbm, s0, s1, d0, d1, r0, r1, g0, g1, w0, w1):
        body(table_hbm, src_hbm, dst_hbm, out_hbm, (s0, s1), (d0, d1), (r0, r1), (g0, g1), (w0, w1))
    return permute_kernel(table, src, dst)


def _sc_gather(table, idx, chunk):
    return _sc_move_rows(table, idx, None, idx.shape[0], chunk)


def _sc_permute(table, src, dst, n_out, chunk):
    return _sc_move_rows(table, src, dst, n_out, chunk)


def _tile_layout(counts, bm, n_tiles):
    n_groups = counts.shape[0]
    padded = (counts + bm - 1) // bm * bm
    pad_end = jnp.cumsum(padded)
    tile_start = jnp.arange(n_tiles, dtype=jnp.int32) * bm
    tile_group = jnp.minimum(
        jnp.sum((pad_end[None, :] <= tile_start[:, None]).astype(jnp.int32), axis=1), n_groups - 1)
    n_used = (pad_end[-1] // bm).astype(jnp.int32).reshape(1)
    return pad_end - padded, tile_group.astype(jnp.int32), n_used


def _null_partial(rows):
    lane = lax.broadcasted_iota(jnp.int32, (rows, PART_W), 1)
    return jnp.where(lane < ATTN_HEAD_DIM, 0.0, -jnp.inf).astype(F32)


def _moba_sel_kernel(q_ref, km_ref, k_ref, v_ref, idx_ref, cnt_ref, own_ref, cnt_acc, *, n_blocks):
    j = pl.program_id(1)
    T = MOBA_BLOCK
    heads = range(ATTN_HEADS)
    qs = [q_ref[hd // 2] for hd in heads]
    gates = [_dot_nt(km_ref[hd], qs[hd], precision=HIGHEST) for hd in heads]
    blk = lax.broadcasted_iota(jnp.int32, gates[0].shape, 0)
    neg_inf = jnp.float32(-jnp.inf)
    gates = [jnp.where(blk < j, g, neg_inf) for g in gates]
    picks = [[] for _ in heads]
    for _ in range(MOBA_TOPK):
        ms = [jnp.max(g, axis=0, keepdims=True) for g in gates]
        firsts = [jnp.min(jnp.where(g == m, blk, n_blocks), axis=0, keepdims=True) for g, m in zip(gates, ms)]
        for hd in heads:
            picks[hd].append(jnp.where(ms[hd] > neg_inf, firsts[hd], -1))
        gates = [jnp.where(blk == f, neg_inf, g) for g, f in zip(gates, firsts)]

    @pl.when(j == 0)
    def _():
        cnt_acc[...] = jnp.zeros_like(cnt_acc)

    earlier = (lax.broadcasted_iota(jnp.int32, (T, T), 0) < lax.broadcasted_iota(jnp.int32, (T, T), 1)).astype(BF16)
    for hd in heads:
        onehots = [(blk == p).astype(F32) for p in picks[hd]]
        member = onehots[0] + onehots[1] + onehots[2]
        base = cnt_acc[hd] + _dot(member.astype(BF16), earlier)
        ranks = [jnp.sum(oh * base, axis=0, keepdims=True).astype(jnp.int32) for oh in onehots]
        idx_ref[hd] = jnp.concatenate(picks[hd] + ranks + [jnp.zeros((2, T), jnp.int32)], axis=0)
        total = cnt_acc[hd] + jnp.sum(member, axis=1, keepdims=True)
        cnt_acc[hd] = total
        cnt_ref[hd] = total.astype(jnp.int32)
    causal = lax.broadcasted_iota(jnp.int32, (T, T), 1) <= lax.broadcasted_iota(jnp.int32, (T, T), 0)
    scale = ATTN_HEAD_DIM ** -0.5
    ss = [jnp.where(causal, _dot_nt((qs[hd] * scale).astype(BF16), k_ref[hd]), neg_inf) for hd in heads]
    mx = [jnp.max(s, axis=1, keepdims=True) for s in ss]
    ps = [jnp.exp(s - m) for s, m in zip(ss, mx)]
    ls = [jnp.sum(p, axis=1, keepdims=True) for p in ps]
    accs = [_dot(ps[hd].astype(BF16), v_ref[hd]) for hd in heads]
    for hd in heads:
        lse = jnp.broadcast_to(mx[hd] + jnp.log(ls[hd]), (T, PART_W - ATTN_HEAD_DIM))
        own_ref[hd] = jnp.concatenate([accs[hd] / ls[hd], lse], axis=1)


def _moba_sel(aq, kmean, ak, av):
    B, H, S, hd = av.shape
    nb = S // MOBA_BLOCK
    T = MOBA_BLOCK
    blk = lambda b, j: (b, 0, j, 0)
    return pl.pallas_call(
        functools.partial(_moba_sel_kernel, n_blocks=nb),
        out_shape=(jax.ShapeDtypeStruct((B, H, 8, S), jnp.int32),
                   jax.ShapeDtypeStruct((B, H, nb, 1), jnp.int32),
                   jax.ShapeDtypeStruct((B, H, S, PART_W), F32)),
        grid=(B, nb),
        in_specs=[pl.BlockSpec((None, H // 2, T, 128), blk),
                  pl.BlockSpec((None, H, nb, 128), lambda b, j: (b, 0, 0, 0)),
                  pl.BlockSpec((None, H, T, 128), blk),
                  pl.BlockSpec((None, H, T, hd), blk)],
        out_specs=(pl.BlockSpec((None, H, 8, T), lambda b, j: (b, 0, 0, j)),
                   pl.BlockSpec((None, H, nb, 1), lambda b, j: (b, 0, 0, 0)),
                   pl.BlockSpec((None, H, T, PART_W), blk)),
        scratch_shapes=[pltpu.VMEM((H, nb, 1), F32)],
        compiler_params=pltpu.CompilerParams(
            dimension_semantics=("arbitrary", "arbitrary"), vmem_limit_bytes=VMEM_LIMIT),
        name="moba_sel",
    )(aq, kmean, ak, av)


def _moba_blk_kernel(tg_ref, nu_ref, q_ref, k_ref, v_ref, o_ref, *, n_blocks):
    n = MOBA_TILES_PER_STEP
    R = MOBA_ROWS
    t0 = pl.program_id(0) * n

    @pl.when(t0 < nu_ref[0])
    def _():
        scale = ATTN_HEAD_DIM ** -0.5
        groups = [tg_ref[t0 + j] for j in range(n)]
        kv_rows = [(g // n_blocks, pl.ds(pl.multiple_of((g % n_blocks) * MOBA_BLOCK, MOBA_BLOCK), MOBA_BLOCK))
                   for g in groups]
        ss = [_dot_nt((q_ref[j * R:(j + 1) * R, :] * scale).astype(BF16), k_ref[kv_rows[j][0], kv_rows[j][1], :])
              for j in range(n)]
        ms = [jnp.max(s, axis=1, keepdims=True) for s in ss]
        ps = [jnp.exp(s - m) for s, m in zip(ss, ms)]
        ls = [jnp.sum(p, axis=1, keepdims=True) for p in ps]
        accs = [_dot(p.astype(BF16), v_ref[kv_rows[j][0], kv_rows[j][1], :]) for j, p in enumerate(ps)]
        null = _null_partial(R)
        for j in range(n):
            lse = jnp.broadcast_to(ms[j] + jnp.log(ls[j]), (R, PART_W - ATTN_HEAD_DIM))
            row = jnp.concatenate([accs[j] / ls[j], lse], axis=1)
            o_ref[j * R:(j + 1) * R, :] = jnp.where(t0 + j < nu_ref[0], row, null)

    @pl.when(t0 >= nu_ref[0])
    def _():
        o_ref[...] = _null_partial(n * R)


def _moba_blk(qs, tile_group, n_used, ak, av):
    B, H, S, hd = av.shape
    nb = S // MOBA_BLOCK
    R = MOBA_ROWS
    n = MOBA_TILES_PER_STEP
    n_tiles = qs.shape[0] // R
    assert n_tiles % n == 0
    whole = lambda i, tg, nu: (0, 0, 0)
    grid_spec = pltpu.PrefetchScalarGridSpec(
        num_scalar_prefetch=2,
        grid=(n_tiles // n,),
        in_specs=[pl.BlockSpec((n * R, 128), lambda i, tg, nu: (i, 0)),
                  pl.BlockSpec((B * H, S, 128), whole, pipeline_mode=pl.Buffered(1)),
                  pl.BlockSpec((B * H, S, hd), whole, pipeline_mode=pl.Buffered(1))],
        out_specs=pl.BlockSpec((n * R, PART_W), lambda i, tg, nu: (i, 0)),
    )
    return pl.pallas_call(
        functools.partial(_moba_blk_kernel, n_blocks=nb),
        out_shape=jax.ShapeDtypeStruct((n_tiles * R, PART_W), F32),
        grid_spec=grid_spec,
        compiler_params=pltpu.CompilerParams(
            dimension_semantics=("arbitrary",), vmem_limit_bytes=VMEM_LIMIT),
        name="moba_blk",
    )(tile_group, n_used, qs, ak.reshape(B * H, S, 128), av.reshape(B * H, S, hd))


def _moba_merge_kernel(own_ref, pg_ref, g_ref, o_ref):
    hd = ATTN_HEAD_DIM
    rows = [own_ref[...]] + [pg_ref[s] for s in range(MOBA_TOPK)]
    lses = [pltpu.roll(r, hd, 1) for r in rows]
    top = lses[0]
    for z in lses[1:]:
        top = jnp.maximum(top, z)
    num = jnp.zeros_like(top)
    den = jnp.zeros_like(top)
    for r, z in zip(rows, lses):
        w = jnp.exp(z - top)
        num = num + w * r
        den = den + w
    o = (num / den)[:, :hd]
    ms = jnp.mean(o * o, axis=-1, keepdims=True)
    o_ref[...] = o * lax.rsqrt(ms + RMS_EPS) * g_ref[...]


def _moba_merge(own, pg, norm_g):
    n = own.shape[0]
    T = MERGE_ROWS
    row = lambda i: (i, 0)
    return pl.pallas_call(
        _moba_merge_kernel,
        out_shape=jax.ShapeDtypeStruct((n, ATTN_HEAD_DIM), F32),
        grid=(n // T,),
        in_specs=[pl.BlockSpec((T, PART_W), row),
                  pl.BlockSpec((MOBA_TOPK, T, PART_W), lambda i: (0, i, 0)),
                  pl.BlockSpec((1, ATTN_HEAD_DIM), lambda i: (0, 0))],
        out_specs=pl.BlockSpec((T, ATTN_HEAD_DIM), row),
        compiler_params=pltpu.CompilerParams(
            dimension_semantics=("arbitrary",), vmem_limit_bytes=VMEM_LIMIT),
        name="moba_merge",
    )(own, pg, norm_g)


def _moba(aq, km, ak, av, norm_g):
    B, H, S, hd = av.shape
    nb = S // MOBA_BLOCK
    n_q = B * H * S
    kmp = km.reshape(B, nb, H // 2, 128)
    half = jnp.arange(128, dtype=jnp.int32) // hd
    kmean = jnp.stack([jnp.where(half == h % 2, kmp[:, :, h // 2, :], 0.0) for h in range(H)], axis=1)
    idx8, counts, own = _moba_sel(aq, kmean, ak, av)
    sel = idx8[:, :, 0:MOBA_TOPK, :].reshape(B * H, MOBA_TOPK, S)
    rank = idx8[:, :, MOBA_TOPK:2 * MOBA_TOPK, :].reshape(B * H, MOBA_TOPK, S)
    n_groups = B * H * nb
    n_tiles = (n_q * MOBA_TOPK) // MOBA_ROWS + n_groups
    pad_start, tile_group, n_used = _tile_layout(counts.reshape(-1), MOBA_ROWS, n_tiles)
    blocks = jnp.arange(nb, dtype=jnp.int32)[:, None]
    start = jnp.sum(jnp.where(sel[:, :, None, :] == blocks, pad_start.reshape(B * H, 1, nb, 1), 0), axis=2)
    a_ids = jnp.arange(n_q * MOBA_TOPK, dtype=jnp.int32).reshape(B * H, MOBA_TOPK, S)
    assert n_tiles * MOBA_ROWS >= n_q * MOBA_TOPK + MOBA_ROWS
    pos = jnp.where(sel >= 0, start + rank, n_used[0] * MOBA_ROWS + a_ids % MOBA_ROWS)
    bh = jnp.arange(B * H, dtype=jnp.int32)[:, None, None]
    t = jnp.arange(S, dtype=jnp.int32)[None, None, :]
    pair_row = jnp.broadcast_to((bh // H * (H // 2) + bh % H // 2) * S + t, pos.shape)
    qs = _sc_permute(aq.reshape(B * (H // 2) * S, 128), pair_row.reshape(-1), pos.reshape(-1),
                     n_tiles * MOBA_ROWS, 256)
    parts = _moba_blk(qs, tile_group, n_used, ak, av)
    pg = _sc_gather(parts, pos.transpose(1, 0, 2).reshape(-1), 256)
    o = _moba_merge(own.reshape(n_q, PART_W), pg.reshape(MOBA_TOPK, n_q, PART_W), norm_g)
    return o.reshape(B, H, S, hd)


def _mix_kernel(oh_ref, oa_ref, x_ref, g1_ref, sc2_ref, sh2_ref, n2_ref, wo_ref, wr_ref, br_ref,
                x1_ref, h2_ref, gw_ref, idx_ref, cnt_ref, cnt_acc):
    cat = jnp.concatenate([oh_ref[...]] + [oa_ref[hd] for hd in range(ATTN_HEADS)], axis=1)
    mix = _dot(cat.astype(BF16), wo_ref[...])
    x1 = x_ref[...] + g1_ref[...] * mix
    x1_ref[...] = x1
    ms = jnp.mean(x1 * x1, axis=-1, keepdims=True)
    h2 = x1 * lax.rsqrt(ms + RMS_EPS) * n2_ref[...]
    h2 = h2 * (1.0 + sc2_ref[...]) + sh2_ref[...]
    h2_ref[...] = _pack_bf16_pairs(h2)
    E = N_EXPERTS
    tm = h2.shape[0]
    h_0 = h2.astype(BF16)
    r_1 = h2 - h_0.astype(F32)
    h_1 = r_1.astype(BF16)
    h_2 = (r_1 - h_1.astype(F32)).astype(BF16)
    wt = wr_ref[...]
    p_0 = _dot_nt(wt, h_0)
    p_1 = _dot_nt(wt[:2 * E], h_1)
    p_2 = _dot_nt(wt[:E], h_2)
    logits = (p_0[:E] + (p_0[E:2 * E] + p_1[:E]) + (p_0[2 * E:] + p_1[E:] + p_2)) + br_ref[...]
    ex = lax.broadcasted_iota(jnp.int32, logits.shape, 0)
    neg_inf = jnp.float32(-jnp.inf)
    vals, idxs = [], []
    for _ in range(TOP_K):
        m = jnp.max(logits, axis=0, keepdims=True)
        first = jnp.min(jnp.where(logits == m, ex, E), axis=0, keepdims=True)
        vals.append(m)
        idxs.append(first)
        logits = jnp.where(ex == first, neg_inf, logits)
    e = [jnp.exp(v - vals[0]) for v in vals]
    denom = e[0] + e[1] + e[2] + e[3]
    gate_rows = jnp.concatenate([ei / denom for ei in e] + [jnp.zeros((128 - TOP_K, tm), F32)], axis=0)
    gw_ref[...] = gate_rows.T[:, :TOP_K]

    @pl.when((pl.program_id(0) == 0) & (pl.program_id(1) == 0))
    def _():
        cnt_acc[...] = jnp.zeros_like(cnt_acc)

    earlier = (lax.broadcasted_iota(jnp.int32, (tm, tm), 0) < lax.broadcasted_iota(jnp.int32, (tm, tm), 1)).astype(BF16)
    onehots = [(ex == ix).astype(F32) for ix in idxs]
    member = onehots[0] + onehots[1] + onehots[2] + onehots[3]
    base = cnt_acc[...] + _dot(member.astype(BF16), earlier)
    ranks = [jnp.sum(oh * base, axis=0, keepdims=True).astype(jnp.int32) for oh in onehots]
    idx_ref[...] = jnp.concatenate(idxs + ranks, axis=0)
    total = cnt_acc[...] + jnp.sum(member, axis=1, keepdims=True)
    cnt_acc[...] = total
    cnt_ref[...] = total.astype(jnp.int32)


def _mix(oh, oa, x, gate1, scale2, shift2, norm2_g, w_out_bf16, w_router, b_router, b0):
    _, S, D = x.shape
    B = oh.shape[0]
    hw = oh.shape[-1]
    tm = MIX_ROWS
    row = lambda b, i: (b, i, 0)
    xrow = lambda b, i: (b0 + b, i, 0)
    vec = lambda b, i: (b, 0, 0)
    const = lambda b, i: (0, 0)
    return pl.pallas_call(
        _mix_kernel,
        out_shape=(jax.ShapeDtypeStruct((B, S, D), F32),
                   jax.ShapeDtypeStruct((B, S, D // 2), jnp.int32),
                   jax.ShapeDtypeStruct((B, S, TOP_K), F32),
                   jax.ShapeDtypeStruct((B, 2 * TOP_K, S), jnp.int32),
                   jax.ShapeDtypeStruct((N_EXPERTS, 1), jnp.int32)),
        grid=(B, S // tm),
        in_specs=[pl.BlockSpec((None, tm, hw), row),
                  pl.BlockSpec((None, ATTN_HEADS, tm, ATTN_HEAD_DIM), lambda b, i: (b, 0, i, 0)),
                  pl.BlockSpec((None, tm, D), xrow),
                  pl.BlockSpec((None, 1, D), vec),
                  pl.BlockSpec((None, 1, D), vec),
                  pl.BlockSpec((None, 1, D), vec),
                  pl.BlockSpec((1, D), const),
                  pl.BlockSpec((D, D), const),
                  pl.BlockSpec((3 * N_EXPERTS, D), const),
                  pl.BlockSpec((N_EXPERTS, 1), const)],
        out_specs=(pl.BlockSpec((None, tm, D), row),
                   pl.BlockSpec((None, tm, D // 2), row),
                   pl.BlockSpec((None, tm, TOP_K), row),
                   pl.BlockSpec((None, 2 * TOP_K, tm), lambda b, i: (b, 0, i)),
                   pl.BlockSpec((N_EXPERTS, 1), const)),
        scratch_shapes=[pltpu.VMEM((N_EXPERTS, 1), F32)],
        compiler_params=pltpu.CompilerParams(
            dimension_semantics=("arbitrary", "arbitrary"), vmem_limit_bytes=VMEM_LIMIT),
        name="mix",
    )(oh, oa, x, gate1, scale2, shift2, norm2_g, w_out_bf16, w_router, b_router)


def _moe_rows_kernel(be_ref, nx_ref, nu_ref, x_ref, wgu_hbm, bgu_ref, wd_hbm, bd_ref, y_ref,
                     wgu32, wd32, wgu16, wd16, sem, *, d_ff):
    i = pl.program_id(0)
    e = be_ref[i]

    def fetch(expert):
        return (pltpu.make_async_copy(wgu_hbm.at[expert], wgu32, sem.at[0]),
                pltpu.make_async_copy(wd_hbm.at[expert], wd32, sem.at[1]))

    @pl.when(i == 0)
    def _():
        for c in fetch(e):
            c.start()

    @pl.when((i == 0) | (e != be_ref[jnp.maximum(i - 1, 0)]))
    def _():
        for c in fetch(e):
            c.wait()
        wgu16[...] = wgu32[...].astype(BF16)
        wd16[...] = wd32[...].astype(BF16)

        @pl.when(nx_ref[i] >= 0)
        def _():
            for c in fetch(nx_ref[i]):
                c.start()

    @pl.when(i < nu_ref[0])
    def _():
        gu = _dot(_unpack_bf16_pairs(x_ref[...]).astype(BF16), wgu16[...]) + bgu_ref[...]
        gate = jnp.minimum(gu[:, :d_ff], SWIGLU_LIMIT)
        up = jnp.clip(gu[:, d_ff:], -SWIGLU_LIMIT, SWIGLU_LIMIT)
        act = (up + 1.0) * gate * _sigmoid(SWIGLU_ALPHA * gate)
        y_ref[...] = _pack_bf16_pairs(_dot(act.astype(BF16), wd16[...]) + bd_ref[...])

    @pl.when(i >= nu_ref[0])
    def _():
        y_ref[...] = jnp.zeros_like(y_ref)


def _moe_rows(xs, blk_expert, n_used, wgu, bgu, wd, bd):
    D = 2 * xs.shape[1]
    bm = MOE_ROWS
    n_blk = xs.shape[0] // bm
    d_ff = wd.shape[1]
    run_end = jnp.sum((blk_expert[None, :] <= blk_expert[:, None]).astype(jnp.int32), axis=1)
    next_expert = jnp.where(run_end < n_blk, blk_expert[jnp.minimum(run_end, n_blk - 1)], -1).astype(jnp.int32)
    bsel = lambda i, be, nx, nu: (be[i], 0, 0)
    rows = lambda i, be, nx, nu: (i, 0)
    grid_spec = pltpu.PrefetchScalarGridSpec(
        num_scalar_prefetch=3,
        grid=(n_blk,),
        in_specs=[pl.BlockSpec((bm, D // 2), rows),
                  pl.BlockSpec(memory_space=pl.ANY),
                  pl.BlockSpec((None, 1, 2 * d_ff), bsel),
                  pl.BlockSpec(memory_space=pl.ANY),
                  pl.BlockSpec((None, 1, D), bsel)],
        out_specs=pl.BlockSpec((bm, D // 2), rows),
        scratch_shapes=[pltpu.VMEM((D, 2 * d_ff), F32), pltpu.VMEM((d_ff, D), F32),
                        pltpu.VMEM((D, 2 * d_ff), BF16), pltpu.VMEM((d_ff, D), BF16),
                        pltpu.SemaphoreType.DMA((2,))],
    )
    return pl.pallas_call(
        functools.partial(_moe_rows_kernel, d_ff=d_ff),
        out_shape=jax.ShapeDtypeStruct((n_blk * bm, D // 2), jnp.int32),
        grid_spec=grid_spec,
        compiler_params=pltpu.CompilerParams(
            dimension_semantics=("arbitrary",), vmem_limit_bytes=VMEM_LIMIT),
        name="moe_rows",
    )(blk_expert, next_expert, n_used, xs, wgu, bgu.reshape(N_EXPERTS, 1, 2 * d_ff), wd, bd.reshape(N_EXPERTS, 1, D))


def _combine_rows_kernel(*refs):
    y_refs = refs[:TOP_K]
    gw_ref, x1_ref, g2_ref, fg_ref = refs[TOP_K:TOP_K + 4]
    o_ref = refs[-1]
    gw = gw_ref[...]
    y = gw[:, 0:1] * _unpack_bf16_pairs(y_refs[0][...])
    for kk in range(1, TOP_K):
        y = y + gw[:, kk:kk + 1] * _unpack_bf16_pairs(y_refs[kk][...])
    x2 = x1_ref[...] + g2_ref[...] * y
    ms = jnp.mean(x2 * x2, axis=-1, keepdims=True)
    o_ref[...] = x2 * lax.rsqrt(ms + RMS_EPS) * fg_ref[...]


def _combine_rows(yg, gates, x1, gate2, final_g, out_so_far, b0, n_batches):
    S, D = x1.shape
    tm = COMBINE_ROWS
    steps = S // tm
    slot_spec = lambda kk: pl.BlockSpec((tm, D // 2), lambda i: (kk * steps + i, 0))
    in_specs = [slot_spec(kk) for kk in range(TOP_K)] + [
        pl.BlockSpec((tm, TOP_K), lambda i: (i, 0)),
        pl.BlockSpec((tm, D), lambda i: (i, 0)),
        pl.BlockSpec((1, D), lambda i: (0, 0)),
        pl.BlockSpec((1, D), lambda i: (0, 0))]
    args = [yg] * TOP_K + [gates, x1, gate2, final_g]
    aliases = {}
    if out_so_far is not None:
        in_specs.append(pl.BlockSpec(memory_space=pl.ANY))
        aliases = {len(args): 0}
        args.append(out_so_far)
    return pl.pallas_call(
        _combine_rows_kernel,
        out_shape=jax.ShapeDtypeStruct((n_batches * S, D), F32),
        grid=(steps,),
        in_specs=in_specs,
        out_specs=pl.BlockSpec((tm, D), lambda i: (b0 * steps + i, 0)),
        input_output_aliases=aliases,
        compiler_params=pltpu.CompilerParams(
            dimension_semantics=("arbitrary",), vmem_limit_bytes=VMEM_LIMIT),
        name="combine_rows",
    )(*args)


def _split_bf16x3(w):
    def top(v):
        return lax.bitcast_convert_type(lax.bitcast_convert_type(v, jnp.int32) & jnp.int32(-65536), F32)
    w0 = top(w)
    w1 = top(w - w0)
    w2 = w - w0 - w1
    return jnp.concatenate([w0, w1, w2], axis=1).astype(BF16).T


def _rotary_tables(positions):
    half = ROT_DIM // 2
    inv_freq = jnp.exp(-math.log(ROPE_THETA) * jnp.arange(0, ROT_DIM, 2, dtype=F32) / ROT_DIM)
    d = jnp.arange(128, dtype=jnp.int32) % ATTN_HEAD_DIM
    freq = jnp.where(d < ROT_DIM, inv_freq[d % half], 0.0)
    sign = jnp.where(d < half, -1.0, 1.0)
    ang = positions.astype(F32)[:, :, None] * freq
    return jnp.cos(ang), jnp.sin(ang) * sign


def kernel(x, c, positions, w_ada, b_ada, norm1_g, w_in, hgrn_lb_logits, hgrn_norm_g, attn_norm_g,
           w_out, norm2_g, w_router, b_router, w_gate_up, b_gate_up, w_down, b_down, final_norm_g):
    B, S, D = x.shape
    assert w_in.shape[0] == 1, "single-layer block: the final norm is fused into the combine step"
    l = 0
    ctab, stab = _rotary_tables(positions)
    lower_bounds = jnp.cumsum(jax.nn.softmax(hgrn_lb_logits.astype(F32), axis=0), axis=0)
    mod = _ada(c, w_ada[l], b_ada[l])
    shift1, scale1, gate1, shift2, scale2, gate2 = jnp.split(mod[:, None, :], N_MOD, axis=-1)
    n_blk = (S * TOP_K) // MOE_ROWS + N_EXPERTS
    experts = jnp.arange(N_EXPERTS, dtype=jnp.int32)[:, None]
    tok = jnp.broadcast_to(jnp.arange(S, dtype=jnp.int32)[None, :], (TOP_K, S)).reshape(-1)
    out = None
    for b in range(B):
        one = slice(b, b + 1)
        q, k, lf, v, gt, aq, ak, av, km = _proj(
            x, scale1[one], shift1[one], norm1_g[l][None], w_in[l].astype(BF16), lower_bounds[l][None],
            ctab, stab, b)
        o_a = _moba(aq, km, ak, av, attn_norm_g[l][None])
        o_h = _hgrn(q, k, lf, v, gt, hgrn_norm_g[l][None])
        x1, h2, gates, idx8, counts = _mix(
            o_h, o_a, x, gate1[one], scale2[one], shift2[one], norm2_g[l][None], w_out[l].astype(BF16),
            _split_bf16x3(w_router[l]), b_router[l][:, None], b)
        pad_start, blk_expert, n_used = _tile_layout(counts.reshape(-1), MOE_ROWS, n_blk)
        chosen = idx8[0, 0:TOP_K, :]
        pos = jnp.sum(jnp.where(chosen[:, None, :] == experts, pad_start[:, None], 0), axis=1) + idx8[0, TOP_K:, :]
        xs = _sc_permute(h2.reshape(S, D // 2), tok, pos.reshape(-1), n_blk * MOE_ROWS, 64)
        y_sorted = _moe_rows(xs, blk_expert, n_used, w_gate_up[l], b_gate_up[l], w_down[l], b_down[l])
        yg = _sc_gather(y_sorted, pos.reshape(-1), 64)
        out = _combine_rows(yg, gates.reshape(S, TOP_K), x1.reshape(S, D), gate2[b], final_norm_g[None], out, b, B)
    return out.reshape(B, S, D)
```

```python
import functools
import math

import jax
import jax.numpy as jnp
from jax import lax
from jax.experimental import pallas as pl
from jax.experimental.pallas import tpu as pltpu
from jax.experimental.pallas import tpu_sc as plsc

F32 = jnp.float32
BF16 = jnp.bfloat16
HIGHEST = lax.Precision.HIGHEST

HGRN_DK = 128
HGRN_CHUNK = 64
ATTN_HEADS = 4
ATTN_HEAD_DIM = 64
ROT_DIM = ATTN_HEAD_DIM // 4
ROPE_THETA = 500000.0
MOBA_BLOCK = 256
MOBA_TOPK = 3
N_EXPERTS = 32
TOP_K = 4
SWIGLU_ALPHA = 1.702
SWIGLU_LIMIT = 7.0
N_MOD = 6
RMS_EPS = 1e-6

HGRN_SUB = 16
EXP_CLAMP = 80.0
PROJ_ROWS = 512
HGRN_ROWS = 1024
MERGE_ROWS = 4096
COMBINE_ROWS = 1024
MIX_ROWS = 1024
MOE_ROWS = 512
MOBA_ROWS = 256
MOBA_TILES_PER_STEP = 8
PART_W = 128
V7X_VMEM_BYTES = 64 * 1024 * 1024
VMEM_LIMIT = V7X_VMEM_BYTES * 7 // 8
SC_CORES = 2
SC_SUBCORES = 16


def _sigmoid(x):
    return 1.0 / (1.0 + jnp.exp(-x))


def _dot(a, b, **kw):
    return jnp.dot(a, b, preferred_element_type=F32, **kw)


def _dot_nt(a, b, **kw):
    return lax.dot_general(a, b, (((1,), (1,)), ((), ())), preferred_element_type=F32, **kw)


def _pack_bf16_pairs(x):
    w = x.shape[1] // 2
    bits = lax.bitcast_convert_type(x.astype(BF16).astype(F32), jnp.int32)
    return bits[:, w:] | lax.shift_right_logical(bits[:, :w], 16)


def _unpack_bf16_pairs(p):
    lo = lax.bitcast_convert_type(lax.shift_left(p, 16), F32)
    hi = lax.bitcast_convert_type(p & jnp.int32(-65536), F32)
    return jnp.concatenate([lo, hi], axis=1)


def _ada_kernel(c_ref, w_ref, b_ref, o_ref):
    c = c_ref[...]
    o_ref[...] = _dot(c * _sigmoid(c), w_ref[...], precision=HIGHEST) + b_ref[...]


def _ada(c, w_ada, b_ada):
    B, D = c.shape
    N = w_ada.shape[1]
    tn = N // 4
    c8 = jnp.zeros((8, D), F32).at[:B].set(c)
    out = pl.pallas_call(
        _ada_kernel,
        out_shape=jax.ShapeDtypeStruct((8, N), F32),
        grid=(N // tn,),
        in_specs=[pl.BlockSpec((8, D), lambda j: (0, 0)),
                  pl.BlockSpec((D, tn), lambda j: (0, j)),
                  pl.BlockSpec((1, tn), lambda j: (0, j))],
        out_specs=pl.BlockSpec((8, tn), lambda j: (0, j)),
        compiler_params=pltpu.CompilerParams(vmem_limit_bytes=VMEM_LIMIT),
        name="ada",
    )(c8, w_ada, b_ada.reshape(1, N))
    return out[:B]


def _proj_kernel(x_ref, sc_ref, sh_ref, g_ref, w_ref, lb_ref, ct_ref, st_ref,
                 q_ref, k_ref, lf_ref, v_ref, gt_ref, aq_ref, ak_ref, av_ref, km_ref,
                 *, hw, aw):
    x = x_ref[...]
    ms = jnp.mean(x * x, axis=-1, keepdims=True)
    h = x * lax.rsqrt(ms + RMS_EPS) * g_ref[...]
    h = h * (1.0 + sc_ref[...]) + sh_ref[...]
    proj = _dot(h.astype(BF16), w_ref[...])

    hq = proj[:, 0:hw]
    hf = proj[:, hw:2 * hw]
    hg = proj[:, 3 * hw:4 * hw]
    q_ref[...] = (hq * _sigmoid(hq) * (HGRN_DK ** -0.5)).astype(BF16)
    lb = lb_ref[...]
    f = lb + (1.0 - lb) * _sigmoid(hf)
    k_ref[...] = (1.0 - f).astype(BF16)
    lf_ref[...] = jnp.log(f)
    v_ref[...] = proj[:, 2 * hw:3 * hw].astype(BF16)
    gt_ref[...] = (hg * _sigmoid(hg)).astype(BF16)

    ct = jnp.concatenate([ct_ref[...]] * (aw // 128), axis=1)
    st = jnp.concatenate([st_ref[...]] * (aw // 128), axis=1)
    lane = lax.broadcasted_iota(jnp.int32, ct.shape, 1) % ATTN_HEAD_DIM
    first_half = lane < (ROT_DIM // 2)

    def rot(t):
        partner = jnp.where(first_half, pltpu.roll(t, aw - ROT_DIM // 2, 1), pltpu.roll(t, ROT_DIM // 2, 1))
        return t * ct + partner * st

    base = 4 * hw
    aq = rot(proj[:, base:base + aw])
    ak = rot(proj[:, base + aw:base + 2 * aw])
    av = proj[:, base + 2 * aw:base + 3 * aw]
    for blk in range(ak.shape[0] // MOBA_BLOCK):
        km_ref[blk] = jnp.mean(ak[blk * MOBA_BLOCK:(blk + 1) * MOBA_BLOCK], axis=0, keepdims=True)
    lane128 = lax.broadcasted_iota(jnp.int32, (x.shape[0], 128), 1)
    for pair in range(ATTN_HEADS // 2):
        aq_ref[pair] = aq[:, pair * 128:(pair + 1) * 128]
    for hd in range(ATTN_HEADS):
        pair, half = divmod(hd, 2)
        in_head = (lane128 // ATTN_HEAD_DIM) == half
        ak_ref[hd] = jnp.where(in_head, ak[:, pair * 128:(pair + 1) * 128], 0.0).astype(BF16)
        av_ref[hd] = av[:, hd * ATTN_HEAD_DIM:(hd + 1) * ATTN_HEAD_DIM].astype(BF16)


def _proj(x, scale1, shift1, norm_g, w_in_bf16, lb, ctab, stab, b0):
    _, S, D = x.shape
    B = scale1.shape[0]
    hw = lb.shape[-1]
    aw = ATTN_HEADS * ATTN_HEAD_DIM
    tm = PROJ_ROWS
    nb = S // MOBA_BLOCK
    n_proj = w_in_bf16.shape[1]
    row = lambda b, i: (b, i, 0)
    xrow = lambda b, i: (b0 + b, i, 0)
    vec = lambda b, i: (b, 0, 0)
    head = lambda b, i: (b, 0, i, 0)
    out_shapes = (
        jax.ShapeDtypeStruct((B, S, hw), BF16),
        jax.ShapeDtypeStruct((B, S, hw), BF16),
        jax.ShapeDtypeStruct((B, S, hw), F32),
        jax.ShapeDtypeStruct((B, S, hw), BF16),
        jax.ShapeDtypeStruct((B, S, hw), BF16),
        jax.ShapeDtypeStruct((B, ATTN_HEADS // 2, S, 128), F32),
        jax.ShapeDtypeStruct((B, ATTN_HEADS, S, 128), BF16),
        jax.ShapeDtypeStruct((B, ATTN_HEADS, S, ATTN_HEAD_DIM), BF16),
        jax.ShapeDtypeStruct((B, nb, 1, aw), F32),
    )
    hspec = pl.BlockSpec((None, tm, hw), row)
    aspec = pl.BlockSpec((None, ATTN_HEADS, tm, ATTN_HEAD_DIM), head)
    return pl.pallas_call(
        functools.partial(_proj_kernel, hw=hw, aw=aw),
        out_shape=out_shapes,
        grid=(B, S // tm),
        in_specs=[pl.BlockSpec((None, tm, D), xrow),
                  pl.BlockSpec((None, 1, D), vec),
                  pl.BlockSpec((None, 1, D), vec),
                  pl.BlockSpec((1, D), lambda b, i: (0, 0)),
                  pl.BlockSpec((D, n_proj), lambda b, i: (0, 0)),
                  pl.BlockSpec((1, hw), lambda b, i: (0, 0)),
                  pl.BlockSpec((None, tm, 128), xrow),
                  pl.BlockSpec((None, tm, 128), xrow)],
        out_specs=(hspec, hspec, hspec, hspec, hspec,
                   pl.BlockSpec((None, ATTN_HEADS // 2, tm, 128), head),
                   pl.BlockSpec((None, ATTN_HEADS, tm, 128), head), aspec,
                   pl.BlockSpec((None, tm // MOBA_BLOCK, 1, aw), lambda b, i: (b, i, 0, 0))),
        compiler_params=pltpu.CompilerParams(
            dimension_semantics=("arbitrary", "arbitrary"), vmem_limit_bytes=VMEM_LIMIT),
        name="proj",
    )(x, scale1, shift1, norm_g, w_in_bf16, lb, ctab, stab)


def _hgrn_kernel(q_ref, k_ref, lf_ref, v_ref, gt_ref, gn_ref, o_ref, st_ref, *, n_heads, n_chunks):
    @pl.when(pl.program_id(1) == 0)
    def _():
        st_ref[...] = jnp.zeros_like(st_ref)

    C = HGRN_CHUNK
    r = lax.broadcasted_iota(jnp.int32, (C, C), 0)
    c = lax.broadcasted_iota(jnp.int32, (C, C), 1)
    tril = c <= r
    ltri = tril.astype(F32)
    gn = gn_ref[...]

    def chunk(ci, carry):
        r0 = pl.multiple_of(ci * C, C)
        rows = pl.ds(r0, C)
        b_all = _dot(ltri, lf_ref[rows, :], precision=HIGHEST)
        heads = range(n_heads)
        sls = [slice(hd * HGRN_DK, (hd + 1) * HGRN_DK) for hd in heads]
        bs = [b_all[:, sl] for sl in sls]
        b_lasts = [b[C - 1:C, :] for b in bs]
        qs = [q_ref[rows, sl].astype(F32) for sl in sls]
        ks = [k_ref[rows, sl].astype(F32) for sl in sls]
        vs = [v_ref[rows, sl] for sl in sls]
        states = [st_ref[hd] for hd in heads]
        o_inter = [_dot_nt((qs[hd] * jnp.exp(bs[hd])).astype(BF16), states[hd].astype(BF16)) for hd in heads]
        scores = []
        for hd in heads:
            blocks = []
            for g0 in range(0, C, HGRN_SUB):
                g1 = g0 + HGRN_SUB
                rho = 0.5 * (bs[hd][g0:g0 + 1, :] + bs[hd][g1 - 1:g1, :])
                qa = qs[hd][g0:g1, :] * jnp.exp(jnp.minimum(bs[hd][g0:g1, :] - rho, EXP_CLAMP))
                kb = ks[hd] * jnp.exp(jnp.minimum(rho - bs[hd], EXP_CLAMP))
                blocks.append(_dot_nt(qa.astype(BF16), kb.astype(BF16)))
            scores.append(jnp.where(tril, jnp.concatenate(blocks, axis=0), 0.0).astype(BF16))
        outs = [o_inter[hd] + _dot(scores[hd], vs[hd]) for hd in heads]
        kds = [(ks[hd] * jnp.exp(b_lasts[hd] - bs[hd])).astype(BF16) for hd in heads]
        upds = [_dot(vs[hd].astype(F32).T.astype(BF16), kds[hd]) for hd in heads]
        for hd in heads:
            st_ref[hd] = states[hd] * jnp.exp(b_lasts[hd]) + upds[hd]
            o = outs[hd]
            ms = jnp.mean(o * o, axis=-1, keepdims=True)
            o_ref[rows, sls[hd]] = (o * lax.rsqrt(ms + RMS_EPS) * gn * gt_ref[rows, sls[hd]].astype(F32)).astype(BF16)
        return carry

    lax.fori_loop(0, n_chunks, chunk, 0, unroll=True)


def _hgrn(q, k, lf, v, gt, norm_g):
    B, S, hw = q.shape
    n_heads = hw // HGRN_DK
    tc = HGRN_ROWS
    spec = pl.BlockSpec((None, tc, hw), lambda b, i: (b, i, 0))
    return pl.pallas_call(
        functools.partial(_hgrn_kernel, n_heads=n_heads, n_chunks=tc // HGRN_CHUNK),
        out_shape=jax.ShapeDtypeStruct((B, S, hw), BF16),
        grid=(B, S // tc),
        in_specs=[spec, spec, spec, spec, spec, pl.BlockSpec((1, HGRN_DK), lambda b, i: (0, 0))],
        out_specs=spec,
        scratch_shapes=[pltpu.VMEM((n_heads, HGRN_DK, HGRN_DK), F32)],
        compiler_params=pltpu.CompilerParams(
            dimension_semantics=("arbitrary", "arbitrary"), vmem_limit_bytes=VMEM_LIMIT),
        name="hgrn",
    )(q, k, lf, v, gt, norm_g)


def _sc_move_rows(table, src, dst, n_out, chunk):
    M = src.shape[0]
    D = table.shape[1]
    n_workers = SC_CORES * SC_SUBCORES
    per_worker = M // n_workers
    n_chunks = per_worker // chunk
    assert per_worker * n_workers == M and n_chunks * chunk == per_worker and n_chunks % 2 == 0 and chunk % 8 == 0
    mesh = plsc.VectorSubcoreMesh(core_axis_name="c", subcore_axis_name="s")
    idx_t = pltpu.VMEM((chunk,), jnp.int32)
    row_t = pltpu.VMEM((chunk, D), table.dtype)
    sem_t = pltpu.SemaphoreType.DMA

    def body(table_hbm, src_hbm, dst_hbm, out_hbm, src_v, dst_v, rows_v, g_sem, s_sem):
        wid = lax.axis_index("s") * SC_CORES + lax.axis_index("c")
        base = wid * per_worker

        def offset(j):
            return pl.multiple_of(base + j * chunk, 8)

        def gather(b):
            return pltpu.make_async_copy(table_hbm.at[src_v[b]], rows_v[b], g_sem[b])

        def start_gather(j, b):
            pltpu.sync_copy(src_hbm.at[pl.ds(offset(j), chunk)], src_v[b])
            gather(b).start()

        def write_out(j, b):
            if dst_hbm is None:
                pltpu.sync_copy(rows_v[b], out_hbm.at[pl.ds(offset(j), chunk)])
            else:
                pltpu.sync_copy(dst_hbm.at[pl.ds(offset(j), chunk)], dst_v[b])
                pltpu.async_copy(rows_v[b], out_hbm.at[dst_v[b]], s_sem[b]).wait()

        start_gather(0, 0)

        @pl.loop(0, n_chunks, step=2)
        def _(j):
            for b in (0, 1):
                @pl.when(j + b + 1 < n_chunks)
                def _():
                    start_gather(j + b + 1, 1 - b)
                gather(b).wait()
                write_out(j + b, b)

    if dst is None:
        @functools.partial(pl.kernel, mesh=mesh, out_type=jax.ShapeDtypeStruct((n_out, D), table.dtype),
                           scratch_types=[idx_t, idx_t, row_t, row_t, sem_t, sem_t])
        def gather_kernel(table_hbm, src_hbm, out_hbm, s0, s1, r0, r1, g0, g1):
            body(table_hbm, src_hbm, None, out_hbm, (s0, s1), None, (r0, r1), (g0, g1), None)
        return gather_kernel(table, src)

    @functools.partial(pl.kernel, mesh=mesh, out_type=jax.ShapeDtypeStruct((n_out, D), table.dtype),
                       scratch_types=[idx_t, idx_t, idx_t, idx_t, row_t, row_t, sem_t, sem_t, sem_t, sem_t])
    def permute_kernel(table_hbm, src_hbm, dst_hbm, out_hbm, s0, s1, d0, d1, r0, r1, g0, g1, w0, w1):
        body(table_hbm, src_hbm, dst_hbm, out_hbm, (s0, s1), (d0, d1), (r0, r1), (g0, g1), (w0, w1))
    return permute_kernel(table, src, dst)


def _sc_gather(table, idx, chunk):
    return _sc_move_rows(table, idx, None, idx.shape[0], chunk)


def _sc_permute(table, src, dst, n_out, chunk):
    return _sc_move_rows(table, src, dst, n_out, chunk)


def _tile_layout(counts, bm, n_tiles):
    n_groups = counts.shape[0]
    padded = (counts + bm - 1) // bm * bm
    pad_end = jnp.cumsum(padded)
    tile_start = jnp.arange(n_tiles, dtype=jnp.int32) * bm
    tile_group = jnp.minimum(
        jnp.sum((pad_end[None, :] <= tile_start[:, None]).astype(jnp.int32), axis=1), n_groups - 1)
    n_used = (pad_end[-1] // bm).astype(jnp.int32).reshape(1)
    return pad_end - padded, tile_group.astype(jnp.int32), n_used


def _null_partial(rows):
    lane = lax.broadcasted_iota(jnp.int32, (rows, PART_W), 1)
    return jnp.where(lane < ATTN_HEAD_DIM, 0.0, -jnp.inf).astype(F32)


def _moba_sel_kernel(q_ref, km_ref, k_ref, v_ref, idx_ref, cnt_ref, own_ref, cnt_acc, *, n_blocks):
    j = pl.program_id(1)
    T = MOBA_BLOCK
    heads = range(ATTN_HEADS)
    qs = [q_ref[hd // 2] for hd in heads]
    gates = [_dot_nt(km_ref[hd], qs[hd], precision=HIGHEST) for hd in heads]
    blk = lax.broadcasted_iota(jnp.int32, gates[0].shape, 0)
    neg_inf = jnp.float32(-jnp.inf)
    gates = [jnp.where(blk < j, g, neg_inf) for g in gates]
    picks = [[] for _ in heads]
    for _ in range(MOBA_TOPK):
        ms = [jnp.max(g, axis=0, keepdims=True) for g in gates]
        firsts = [jnp.min(jnp.where(g == m, blk, n_blocks), axis=0, keepdims=True) for g, m in zip(gates, ms)]
        for hd in heads:
            picks[hd].append(jnp.where(ms[hd] > neg_inf, firsts[hd], -1))
        gates = [jnp.where(blk == f, neg_inf, g) for g, f in zip(gates, firsts)]

    @pl.when(j == 0)
    def _():
        cnt_acc[...] = jnp.zeros_like(cnt_acc)

    earlier = (lax.broadcasted_iota(jnp.int32, (T, T), 0) < lax.broadcasted_iota(jnp.int32, (T, T), 1)).astype(BF16)
    for hd in heads:
        onehots = [(blk == p).astype(F32) for p in picks[hd]]
        member = onehots[0] + onehots[1] + onehots[2]
        base = cnt_acc[hd] + _dot(member.astype(BF16), earlier)
        ranks = [jnp.sum(oh * base, axis=0, keepdims=True).astype(jnp.int32) for oh in onehots]
        idx_ref[hd] = jnp.concatenate(picks[hd] + ranks + [jnp.zeros((2, T), jnp.int32)], axis=0)
        total = cnt_acc[hd] + jnp.sum(member, axis=1, keepdims=True)
        cnt_acc[hd] = total
        cnt_ref[hd] = total.astype(jnp.int32)
    causal = lax.broadcasted_iota(jnp.int32, (T, T), 1) <= lax.broadcasted_iota(jnp.int32, (T, T), 0)
    scale = ATTN_HEAD_DIM ** -0.5
    ss = [jnp.where(causal, _dot_nt((qs[hd] * scale).astype(BF16), k_ref[hd]), neg_inf) for hd in heads]
    mx = [jnp.max(s, axis=1, keepdims=True) for s in ss]
    ps = [jnp.exp(s - m) for s, m in zip(ss, mx)]
    ls = [jnp.sum(p, axis=1, keepdims=True) for p in ps]
    accs = [_dot(ps[hd].astype(BF16), v_ref[hd]) for hd in heads]
    for hd in heads:
        lse = jnp.broadcast_to(mx[hd] + jnp.log(ls[hd]), (T, PART_W - ATTN_HEAD_DIM))
        own_ref[hd] = jnp.concatenate([accs[hd] / ls[hd], lse], axis=1)


def _moba_sel(aq, kmean, ak, av):
    B, H, S, hd = av.shape
    nb = S // MOBA_BLOCK
    T = MOBA_BLOCK
    blk = lambda b, j: (b, 0, j, 0)
    return pl.pallas_call(
        functools.partial(_moba_sel_kernel, n_blocks=nb),
        out_shape=(jax.ShapeDtypeStruct((B, H, 8, S), jnp.int32),
                   jax.ShapeDtypeStruct((B, H, nb, 1), jnp.int32),
                   jax.ShapeDtypeStruct((B, H, S, PART_W), F32)),
        grid=(B, nb),
        in_specs=[pl.BlockSpec((None, H // 2, T, 128), blk),
                  pl.BlockSpec((None, H, nb, 128), lambda b, j: (b, 0, 0, 0)),
                  pl.BlockSpec((None, H, T, 128), blk),
                  pl.BlockSpec((None, H, T, hd), blk)],
        out_specs=(pl.BlockSpec((None, H, 8, T), lambda b, j: (b, 0, 0, j)),
                   pl.BlockSpec((None, H, nb, 1), lambda b, j: (b, 0, 0, 0)),
                   pl.BlockSpec((None, H, T, PART_W), blk)),
        scratch_shapes=[pltpu.VMEM((H, nb, 1), F32)],
        compiler_params=pltpu.CompilerParams(
            dimension_semantics=("arbitrary", "arbitrary"), vmem_limit_bytes=VMEM_LIMIT),
        name="moba_sel",
    )(aq, kmean, ak, av)


def _moba_blk_kernel(tg_ref, nu_ref, q_ref, k_ref, v_ref, o_ref, *, n_blocks):
    n = MOBA_TILES_PER_STEP
    R = MOBA_ROWS
    t0 = pl.program_id(0) * n

    @pl.when(t0 < nu_ref[0])
    def _():
        scale = ATTN_HEAD_DIM ** -0.5
        groups = [tg_ref[t0 + j] for j in range(n)]
        kv_rows = [(g // n_blocks, pl.ds(pl.multiple_of((g % n_blocks) * MOBA_BLOCK, MOBA_BLOCK), MOBA_BLOCK))
                   for g in groups]
        ss = [_dot_nt((q_ref[j * R:(j + 1) * R, :] * scale).astype(BF16), k_ref[kv_rows[j][0], kv_rows[j][1], :])
              for j in range(n)]
        ms = [jnp.max(s, axis=1, keepdims=True) for s in ss]
        ps = [jnp.exp(s - m) for s, m in zip(ss, ms)]
        ls = [jnp.sum(p, axis=1, keepdims=True) for p in ps]
        accs = [_dot(p.astype(BF16), v_ref[kv_rows[j][0], kv_rows[j][1], :]) for j, p in enumerate(ps)]
        null = _null_partial(R)
        for j in range(n):
            lse = jnp.broadcast_to(ms[j] + jnp.log(ls[j]), (R, PART_W - ATTN_HEAD_DIM))
            row = jnp.concatenate([accs[j] / ls[j], lse], axis=1)
            o_ref[j * R:(j + 1) * R, :] = jnp.where(t0 + j < nu_ref[0], row, null)

    @pl.when(t0 >= nu_ref[0])
    def _():
        o_ref[...] = _null_partial(n * R)


def _moba_blk(qs, tile_group, n_used, ak, av):
    B, H, S, hd = av.shape
    nb = S // MOBA_BLOCK
    R = MOBA_ROWS
    n = MOBA_TILES_PER_STEP
    n_tiles = qs.shape[0] // R
    assert n_tiles % n == 0
    whole = lambda i, tg, nu: (0, 0, 0)
    grid_spec = pltpu.PrefetchScalarGridSpec(
        num_scalar_prefetch=2,
        grid=(n_tiles // n,),
        in_specs=[pl.BlockSpec((n * R, 128), lambda i, tg, nu: (i, 0)),
                  pl.BlockSpec((B * H, S, 128), whole, pipeline_mode=pl.Buffered(1)),
                  pl.BlockSpec((B * H, S, hd), whole, pipeline_mode=pl.Buffered(1))],
        out_specs=pl.BlockSpec((n * R, PART_W), lambda i, tg, nu: (i, 0)),
    )
    return pl.pallas_call(
        functools.partial(_moba_blk_kernel, n_blocks=nb),
        out_shape=jax.ShapeDtypeStruct((n_tiles * R, PART_W), F32),
        grid_spec=grid_spec,
        compiler_params=pltpu.CompilerParams(
            dimension_semantics=("arbitrary",), vmem_limit_bytes=VMEM_LIMIT),
        name="moba_blk",
    )(tile_group, n_used, qs, ak.reshape(B * H, S, 128), av.reshape(B * H, S, hd))


def _moba_merge_kernel(own_ref, pg_ref, g_ref, o_ref):
    hd = ATTN_HEAD_DIM
    rows = [own_ref[...]] + [pg_ref[s] for s in range(MOBA_TOPK)]
    lses = [pltpu.roll(r, hd, 1) for r in rows]
    top = lses[0]
    for z in lses[1:]:
        top = jnp.maximum(top, z)
    num = jnp.zeros_like(top)
    den = jnp.zeros_like(top)
    for r, z in zip(rows, lses):
        w = jnp.exp(z - top)
        num = num + w * r
        den = den + w
    o = (num / den)[:, :hd]
    ms = jnp.mean(o * o, axis=-1, keepdims=True)
    o_ref[...] = o * lax.rsqrt(ms + RMS_EPS) * g_ref[...]


def _moba_merge(own, pg, norm_g):
    n = own.shape[0]
    T = MERGE_ROWS
    row = lambda i: (i, 0)
    return pl.pallas_call(
        _moba_merge_kernel,
        out_shape=jax.ShapeDtypeStruct((n, ATTN_HEAD_DIM), F32),
        grid=(n // T,),
        in_specs=[pl.BlockSpec((T, PART_W), row),
                  pl.BlockSpec((MOBA_TOPK, T, PART_W), lambda i: (0, i, 0)),
                  pl.BlockSpec((1, ATTN_HEAD_DIM), lambda i: (0, 0))],
        out_specs=pl.BlockSpec((T, ATTN_HEAD_DIM), row),
        compiler_params=pltpu.CompilerParams(
            dimension_semantics=("arbitrary",), vmem_limit_bytes=VMEM_LIMIT),
        name="moba_merge",
    )(own, pg, norm_g)


def _moba(aq, km, ak, av, norm_g):
    B, H, S, hd = av.shape
    nb = S // MOBA_BLOCK
    n_q = B * H * S
    kmp = km.reshape(B, nb, H // 2, 128)
    half = jnp.arange(128, dtype=jnp.int32) // hd
    kmean = jnp.stack([jnp.where(half == h % 2, kmp[:, :, h // 2, :], 0.0) for h in range(H)], axis=1)
    idx8, counts, own = _moba_sel(aq, kmean, ak, av)
    sel = idx8[:, :, 0:MOBA_TOPK, :].reshape(B * H, MOBA_TOPK, S)
    rank = idx8[:, :, MOBA_TOPK:2 * MOBA_TOPK, :].reshape(B * H, MOBA_TOPK, S)
    n_groups = B * H * nb
    n_tiles = (n_q * MOBA_TOPK) // MOBA_ROWS + n_groups
    pad_start, tile_group, n_used = _tile_layout(counts.reshape(-1), MOBA_ROWS, n_tiles)
    blocks = jnp.arange(nb, dtype=jnp.int32)[:, None]
    start = jnp.sum(jnp.where(sel[:, :, None, :] == blocks, pad_start.reshape(B * H, 1, nb, 1), 0), axis=2)
    a_ids = jnp.arange(n_q * MOBA_TOPK, dtype=jnp.int32).reshape(B * H, MOBA_TOPK, S)
    assert n_tiles * MOBA_ROWS >= n_q * MOBA_TOPK + MOBA_ROWS
    pos = jnp.where(sel >= 0, start + rank, n_used[0] * MOBA_ROWS + a_ids % MOBA_ROWS)
    bh = jnp.arange(B * H, dtype=jnp.int32)[:, None, None]
    t = jnp.arange(S, dtype=jnp.int32)[None, None, :]
    pair_row = jnp.broadcast_to((bh // H * (H // 2) + bh % H // 2) * S + t, pos.shape)
    qs = _sc_permute(aq.reshape(B * (H // 2) * S, 128), pair_row.reshape(-1), pos.reshape(-1),
                     n_tiles * MOBA_ROWS, 256)
    parts = _moba_blk(qs, tile_group, n_used, ak, av)
    pg = _sc_gather(parts, pos.transpose(1, 0, 2).reshape(-1), 256)
    o = _moba_merge(own.reshape(n_q, PART_W), pg.reshape(MOBA_TOPK, n_q, PART_W), norm_g)
    return o.reshape(B, H, S, hd)


def _mix_kernel(oh_ref, oa_ref, x_ref, g1_ref, sc2_ref, sh2_ref, n2_ref, wo_ref, wr_ref, br_ref,
                x1_ref, h2_ref, gw_ref, idx_ref, cnt_ref, cnt_acc):
    cat = jnp.concatenate([oh_ref[...]] + [oa_ref[hd] for hd in range(ATTN_HEADS)], axis=1)
    mix = _dot(cat.astype(BF16), wo_ref[...])
    x1 = x_ref[...] + g1_ref[...] * mix
    x1_ref[...] = x1
    ms = jnp.mean(x1 * x1, axis=-1, keepdims=True)
    h2 = x1 * lax.rsqrt(ms + RMS_EPS) * n2_ref[...]
    h2 = h2 * (1.0 + sc2_ref[...]) + sh2_ref[...]
    h2_ref[...] = _pack_bf16_pairs(h2)
    E = N_EXPERTS
    tm = h2.shape[0]
    h_0 = h2.astype(BF16)
    r_1 = h2 - h_0.astype(F32)
    h_1 = r_1.astype(BF16)
    h_2 = (r_1 - h_1.astype(F32)).astype(BF16)
    wt = wr_ref[...]
    p_0 = _dot_nt(wt, h_0)
    p_1 = _dot_nt(wt[:2 * E], h_1)
    p_2 = _dot_nt(wt[:E], h_2)
    logits = (p_0[:E] + (p_0[E:2 * E] + p_1[:E]) + (p_0[2 * E:] + p_1[E:] + p_2)) + br_ref[...]
    ex = lax.broadcasted_iota(jnp.int32, logits.shape, 0)
    neg_inf = jnp.float32(-jnp.inf)
    vals, idxs = [], []
    for _ in range(TOP_K):
        m = jnp.max(logits, axis=0, keepdims=True)
        first = jnp.min(jnp.where(logits == m, ex, E), axis=0, keepdims=True)
        vals.append(m)
        idxs.append(first)
        logits = jnp.where(ex == first, neg_inf, logits)
    e = [jnp.exp(v - vals[0]) for v in vals]
    denom = e[0] + e[1] + e[2] + e[3]
    gate_rows = jnp.concatenate([ei / denom for ei in e] + [jnp.zeros((128 - TOP_K, tm), F32)], axis=0)
    gw_ref[...] = gate_rows.T[:, :TOP_K]

    @pl.when((pl.program_id(0) == 0) & (pl.program_id(1) == 0))
    def _():
        cnt_acc[...] = jnp.zeros_like(cnt_acc)

    earlier = (lax.broadcasted_iota(jnp.int32, (tm, tm), 0) < lax.broadcasted_iota(jnp.int32, (tm, tm), 1)).astype(BF16)
    onehots = [(ex == ix).astype(F32) for ix in idxs]
    member = onehots[0] + onehots[1] + onehots[2] + onehots[3]
    base = cnt_acc[...] + _dot(member.astype(BF16), earlier)
    ranks = [jnp.sum(oh * base, axis=0, keepdims=True).astype(jnp.int32) for oh in onehots]
    idx_ref[...] = jnp.concatenate(idxs + ranks, axis=0)
    total = cnt_acc[...] + jnp.sum(member, axis=1, keepdims=True)
    cnt_acc[...] = total
    cnt_ref[...] = total.astype(jnp.int32)


def _mix(oh, oa, x, gate1, scale2, shift2, norm2_g, w_out_bf16, w_router, b_router, b0):
    _, S, D = x.shape
    B = oh.shape[0]
    hw = oh.shape[-1]
    tm = MIX_ROWS
    row = lambda b, i: (b, i, 0)
    xrow = lambda b, i: (b0 + b, i, 0)
    vec = lambda b, i: (b, 0, 0)
    const = lambda b, i: (0, 0)
    return pl.pallas_call(
        _mix_kernel,
        out_shape=(jax.ShapeDtypeStruct((B, S, D), F32),
                   jax.ShapeDtypeStruct((B, S, D // 2), jnp.int32),
                   jax.ShapeDtypeStruct((B, S, TOP_K), F32),
                   jax.ShapeDtypeStruct((B, 2 * TOP_K, S), jnp.int32),
                   jax.ShapeDtypeStruct((N_EXPERTS, 1), jnp.int32)),
        grid=(B, S // tm),
        in_specs=[pl.BlockSpec((None, tm, hw), row),
                  pl.BlockSpec((None, ATTN_HEADS, tm, ATTN_HEAD_DIM), lambda b, i: (b, 0, i, 0)),
                  pl.BlockSpec((None, tm, D), xrow),
                  pl.BlockSpec((None, 1, D), vec),
                  pl.BlockSpec((None, 1, D), vec),
                  pl.BlockSpec((None, 1, D), vec),
                  pl.BlockSpec((1, D), const),
                  pl.BlockSpec((D, D), const),
                  pl.BlockSpec((3 * N_EXPERTS, D), const),
                  pl.BlockSpec((N_EXPERTS, 1), const)],
        out_specs=(pl.BlockSpec((None, tm, D), row),
                   pl.BlockSpec((None, tm, D // 2), row),
                   pl.BlockSpec((None, tm, TOP_K), row),
                   pl.BlockSpec((None, 2 * TOP_K, tm), lambda b, i: (b, 0, i)),
                   pl.BlockSpec((N_EXPERTS, 1), const)),
        scratch_shapes=[pltpu.VMEM((N_EXPERTS, 1), F32)],
        compiler_params=pltpu.CompilerParams(
            dimension_semantics=("arbitrary", "arbitrary"), vmem_limit_bytes=VMEM_LIMIT),
        name="mix",
    )(oh, oa, x, gate1, scale2, shift2, norm2_g, w_out_bf16, w_router, b_router)


def _moe_rows_kernel(be_ref, nx_ref, nu_ref, x_ref, wgu_hbm, bgu_ref, wd_hbm, bd_ref, y_ref,
                     wgu32, wd32, wgu16, wd16, sem, *, d_ff):
    i = pl.program_id(0)
    e = be_ref[i]

    def fetch(expert):
        return (pltpu.make_async_copy(wgu_hbm.at[expert], wgu32, sem.at[0]),
                pltpu.make_async_copy(wd_hbm.at[expert], wd32, sem.at[1]))

    @pl.when(i == 0)
    def _():
        for c in fetch(e):
            c.start()

    @pl.when((i == 0) | (e != be_ref[jnp.maximum(i - 1, 0)]))
    def _():
        for c in fetch(e):
            c.wait()
        wgu16[...] = wgu32[...].astype(BF16)
        wd16[...] = wd32[...].astype(BF16)

        @pl.when(nx_ref[i] >= 0)
        def _():
            for c in fetch(nx_ref[i]):
                c.start()

    @pl.when(i < nu_ref[0])
    def _():
        gu = _dot(_unpack_bf16_pairs(x_ref[...]).astype(BF16), wgu16[...]) + bgu_ref[...]
        gate = jnp.minimum(gu[:, :d_ff], SWIGLU_LIMIT)
        up = jnp.clip(gu[:, d_ff:], -SWIGLU_LIMIT, SWIGLU_LIMIT)
        act = (up + 1.0) * gate * _sigmoid(SWIGLU_ALPHA * gate)
        y_ref[...] = _pack_bf16_pairs(_dot(act.astype(BF16), wd16[...]) + bd_ref[...])

    @pl.when(i >= nu_ref[0])
    def _():
        y_ref[...] = jnp.zeros_like(y_ref)


def _moe_rows(xs, blk_expert, n_used, wgu, bgu, wd, bd):
    D = 2 * xs.shape[1]
    bm = MOE_ROWS
    n_blk = xs.shape[0] // bm
    d_ff = wd.shape[1]
    run_end = jnp.sum((blk_expert[None, :] <= blk_expert[:, None]).astype(jnp.int32), axis=1)
    next_expert = jnp.where(run_end < n_blk, blk_expert[jnp.minimum(run_end, n_blk - 1)], -1).astype(jnp.int32)
    bsel = lambda i, be, nx, nu: (be[i], 0, 0)
    rows = lambda i, be, nx, nu: (i, 0)
    grid_spec = pltpu.PrefetchScalarGridSpec(
        num_scalar_prefetch=3,
        grid=(n_blk,),
        in_specs=[pl.BlockSpec((bm, D // 2), rows),
                  pl.BlockSpec(memory_space=pl.ANY),
                  pl.BlockSpec((None, 1, 2 * d_ff), bsel),
                  pl.BlockSpec(memory_space=pl.ANY),
                  pl.BlockSpec((None, 1, D), bsel)],
        out_specs=pl.BlockSpec((bm, D // 2), rows),
        scratch_shapes=[pltpu.VMEM((D, 2 * d_ff), F32), pltpu.VMEM((d_ff, D), F32),
                        pltpu.VMEM((D, 2 * d_ff), BF16), pltpu.VMEM((d_ff, D), BF16),
                        pltpu.SemaphoreType.DMA((2,))],
    )
    return pl.pallas_call(
        functools.partial(_moe_rows_kernel, d_ff=d_ff),
        out_shape=jax.ShapeDtypeStruct((n_blk * bm, D // 2), jnp.int32),
        grid_spec=grid_spec,
        compiler_params=pltpu.CompilerParams(
            dimension_semantics=("arbitrary",), vmem_limit_bytes=VMEM_LIMIT),
        name="moe_rows",
    )(blk_expert, next_expert, n_used, xs, wgu, bgu.reshape(N_EXPERTS, 1, 2 * d_ff), wd, bd.reshape(N_EXPERTS, 1, D))


def _combine_rows_kernel(*refs):
    y_refs = refs[:TOP_K]
    gw_ref, x1_ref, g2_ref, fg_ref = refs[TOP_K:TOP_K + 4]
    o_ref = refs[-1]
    gw = gw_ref[...]
    y = gw[:, 0:1] * _unpack_bf16_pairs(y_refs[0][...])
    for kk in range(1, TOP_K):
        y = y + gw[:, kk:kk + 1] * _unpack_bf16_pairs(y_refs[kk][...])
    x2 = x1_ref[...] + g2_ref[...] * y
    ms = jnp.mean(x2 * x2, axis=-1, keepdims=True)
    o_ref[...] = x2 * lax.rsqrt(ms + RMS_EPS) * fg_ref[...]


def _combine_rows(yg, gates, x1, gate2, final_g, out_so_far, b0, n_batches):
    S, D = x1.shape
    tm = COMBINE_ROWS
    steps = S // tm
    slot_spec = lambda kk: pl.BlockSpec((tm, D // 2), lambda i: (kk * steps + i, 0))
    in_specs = [slot_spec(kk) for kk in range(TOP_K)] + [
        pl.BlockSpec((tm, TOP_K), lambda i: (i, 0)),
        pl.BlockSpec((tm, D), lambda i: (i, 0)),
        pl.BlockSpec((1, D), lambda i: (0, 0)),
        pl.BlockSpec((1, D), lambda i: (0, 0))]
    args = [yg] * TOP_K + [gates, x1, gate2, final_g]
    aliases = {}
    if out_so_far is not None:
        in_specs.append(pl.BlockSpec(memory_space=pl.ANY))
        aliases = {len(args): 0}
        args.append(out_so_far)
    return pl.pallas_call(
        _combine_rows_kernel,
        out_shape=jax.ShapeDtypeStruct((n_batches * S, D), F32),
        grid=(steps,),
        in_specs=in_specs,
        out_specs=pl.BlockSpec((tm, D), lambda i: (b0 * steps + i, 0)),
        input_output_aliases=aliases,
        compiler_params=pltpu.CompilerParams(
            dimension_semantics=("arbitrary",), vmem_limit_bytes=VMEM_LIMIT),
        name="combine_rows",
    )(*args)


def _split_bf16x3(w):
    def top(v):
        return lax.bitcast_convert_type(lax.bitcast_convert_type(v, jnp.int32) & jnp.int32(-65536), F32)
    w0 = top(w)
    w1 = top(w - w0)
    w2 = w - w0 - w1
    return jnp.concatenate([w0, w1, w2], axis=1).astype(BF16).T


def _rotary_tables(positions):
    B, S = positions.shape
    half = ROT_DIM // 2
    inv_freq = jnp.exp(-math.log(ROPE_THETA) * jnp.arange(0, ROT_DIM, 2, dtype=F32) / ROT_DIM)
    ang = inv_freq[:, None] * positions.astype(F32).reshape(1, B * S)
    d = jnp.arange(128, dtype=jnp.int32) % ATTN_HEAD_DIM
    rotary = d < ROT_DIM
    sel = ((d[None, :] % half == jnp.arange(half, dtype=jnp.int32)[:, None]) & rotary[None, :]).astype(F32)
    sign = jnp.where(d < half, -1.0, 1.0)
    spread = lambda t, w: lax.dot_general(t, w, (((0,), (0,)), ((), ())), precision=HIGHEST)
    ct = spread(jnp.cos(ang), sel) + jnp.where(rotary, 0.0, 1.0)
    st = spread(jnp.sin(ang), sel * sign)
    return ct.reshape(B, S, 128), st.reshape(B, S, 128)


def kernel(x, c, positions, w_ada, b_ada, norm1_g, w_in, hgrn_lb_logits, hgrn_norm_g, attn_norm_g,
           w_out, norm2_g, w_router, b_router, w_gate_up, b_gate_up, w_down, b_down, final_norm_g):
    B, S, D = x.shape
    assert w_in.shape[0] == 1, "single-layer block: the final norm is fused into the combine step"
    l = 0
    ctab, stab = _rotary_tables(positions)
    lower_bounds = jnp.cumsum(jax.nn.softmax(hgrn_lb_logits.astype(F32), axis=0), axis=0)
    mod = _ada(c, w_ada[l], b_ada[l])
    shift1, scale1, gate1, shift2, scale2, gate2 = jnp.split(mod[:, None, :], N_MOD, axis=-1)
    n_blk = (S * TOP_K) // MOE_ROWS + N_EXPERTS
    experts = jnp.arange(N_EXPERTS, dtype=jnp.int32)[:, None]
    tok = jnp.broadcast_to(jnp.arange(S, dtype=jnp.int32)[None, :], (TOP_K, S)).reshape(-1)
    out = None
    for b in range(B):
        one = slice(b, b + 1)
        q, k, lf, v, gt, aq, ak, av, km = _proj(
            x, scale1[one], shift1[one], norm1_g[l][None], w_in[l].astype(BF16), lower_bounds[l][None],
            ctab, stab, b)
        o_a = _moba(aq, km, ak, av, attn_norm_g[l][None])
        o_h = _hgrn(q, k, lf, v, gt, hgrn_norm_g[l][None])
        x1, h2, gates, idx8, counts = _mix(
            o_h, o_a, x, gate1[one], scale2[one], shift2[one], norm2_g[l][None], w_out[l].astype(BF16),
            _split_bf16x3(w_router[l]), b_router[l][:, None], b)
        pad_start, blk_expert, n_used = _tile_layout(counts.reshape(-1), MOE_ROWS, n_blk)
        chosen = idx8[0, 0:TOP_K, :]
        pos = jnp.sum(jnp.where(chosen[:, None, :] == experts, pad_start[:, None], 0), axis=1) + idx8[0, TOP_K:, :]
        xs = _sc_permute(h2.reshape(S, D // 2), tok, pos.reshape(-1), n_blk * MOE_ROWS, 64)
        y_sorted = _moe_rows(xs, blk_expert, n_used, w_gate_up[l], b_gate_up[l], w_down[l], b_down[l])
        yg = _sc_gather(y_sorted, pos.reshape(-1), 64)
        out = _combine_rows(yg, gates.reshape(S, TOP_K), x1.reshape(S, D), gate2[b], final_norm_g[None], out, b, B)
    return out.reshape(B, S, D)
```

```python
import functools
import math

import jax
import jax.numpy as jnp
from jax import lax
from jax.experimental import pallas as pl
from jax.experimental.pallas import tpu as pltpu
from jax.experimental.pallas import tpu_sc as plsc

F32 = jnp.float32
BF16 = jnp.bfloat16
HIGHEST = lax.Precision.HIGHEST

HGRN_DK = 128
HGRN_CHUNK = 64
ATTN_HEADS = 4
ATTN_HEAD_DIM = 64
ROT_DIM = ATTN_HEAD_DIM // 4
ROPE_THETA = 500000.0
MOBA_BLOCK = 256
MOBA_TOPK = 3
N_EXPERTS = 32
TOP_K = 4
SWIGLU_ALPHA = 1.702
SWIGLU_LIMIT = 7.0
N_MOD = 6
RMS_EPS = 1e-6

HGRN_SUB = 16
EXP_CLAMP = 80.0
PROJ_ROWS = 512
HGRN_ROWS = 1024
MERGE_ROWS = 4096
RETURN_PARTS = 2
COMBINE_ROWS = 1024
MIX_ROWS = 1024
MOE_ROWS = 512
MOBA_ROWS = 256
MOBA_TILES_PER_STEP = 8
PART_W = 128
V7X_VMEM_BYTES = 64 * 1024 * 1024
VMEM_LIMIT = V7X_VMEM_BYTES * 7 // 8
SC_CORES = 2
SC_SUBCORES = 16


def _sigmoid(x):
    return 1.0 / (1.0 + jnp.exp(-x))


def _dot(a, b, **kw):
    return jnp.dot(a, b, preferred_element_type=F32, **kw)


def _dot_nt(a, b, **kw):
    return lax.dot_general(a, b, (((1,), (1,)), ((), ())), preferred_element_type=F32, **kw)


def _pack_bf16_pairs(x):
    w = x.shape[1] // 2
    bits = lax.bitcast_convert_type(x.astype(BF16).astype(F32), jnp.int32)
    return bits[:, w:] | lax.shift_right_logical(bits[:, :w], 16)


def _unpack_bf16_pairs(p):
    lo = lax.bitcast_convert_type(lax.shift_left(p, 16), F32)
    hi = lax.bitcast_convert_type(p & jnp.int32(-65536), F32)
    return jnp.concatenate([lo, hi], axis=1)


def _ada_kernel(c_ref, w_ref, b_ref, o_ref):
    c = c_ref[...]
    o_ref[...] = _dot(c * _sigmoid(c), w_ref[...], precision=HIGHEST) + b_ref[...]


def _ada(c, w_ada, b_ada):
    B, D = c.shape
    N = w_ada.shape[1]
    tn = N // 4
    c8 = jnp.zeros((8, D), F32).at[:B].set(c)
    out = pl.pallas_call(
        _ada_kernel,
        out_shape=jax.ShapeDtypeStruct((8, N), F32),
        grid=(N // tn,),
        in_specs=[pl.BlockSpec((8, D), lambda j: (0, 0)),
                  pl.BlockSpec((D, tn), lambda j: (0, j)),
                  pl.BlockSpec((1, tn), lambda j: (0, j))],
        out_specs=pl.BlockSpec((8, tn), lambda j: (0, j)),
        compiler_params=pltpu.CompilerParams(vmem_limit_bytes=VMEM_LIMIT),
        name="ada",
    )(c8, w_ada, b_ada.reshape(1, N))
    return out[:B]


def _proj_kernel(x_ref, sc_ref, sh_ref, g_ref, w_ref, lb_ref, ct_ref, st_ref,
                 q_ref, k_ref, lf_ref, v_ref, gt_ref, aq_ref, ak_ref, av_ref, km_ref,
                 *, hw, aw):
    x = x_ref[...]
    ms = jnp.mean(x * x, axis=-1, keepdims=True)
    h = x * lax.rsqrt(ms + RMS_EPS) * g_ref[...]
    h = h * (1.0 + sc_ref[...]) + sh_ref[...]
    proj = _dot(h.astype(BF16), w_ref[...])

    hq = proj[:, 0:hw]
    hf = proj[:, hw:2 * hw]
    hg = proj[:, 3 * hw:4 * hw]
    q_ref[...] = (hq * _sigmoid(hq) * (HGRN_DK ** -0.5)).astype(BF16)
    lb = lb_ref[...]
    f = lb + (1.0 - lb) * _sigmoid(hf)
    k_ref[...] = (1.0 - f).astype(BF16)
    lf_ref[...] = jnp.log(f)
    v_ref[...] = proj[:, 2 * hw:3 * hw].astype(BF16)
    gt_ref[...] = (hg * _sigmoid(hg)).astype(BF16)

    ct = jnp.concatenate([ct_ref[...]] * (aw // 128), axis=1)
    st = jnp.concatenate([st_ref[...]] * (aw // 128), axis=1)
    lane = lax.broadcasted_iota(jnp.int32, ct.shape, 1) % ATTN_HEAD_DIM
    first_half = lane < (ROT_DIM // 2)

    def rot(t):
        partner = jnp.where(first_half, pltpu.roll(t, aw - ROT_DIM // 2, 1), pltpu.roll(t, ROT_DIM // 2, 1))
        return t * ct + partner * st

    base = 4 * hw
    aq = rot(proj[:, base:base + aw])
    ak = rot(proj[:, base + aw:base + 2 * aw])
    av = proj[:, base + 2 * aw:base + 3 * aw]
    for blk in range(ak.shape[0] // MOBA_BLOCK):
        km_ref[blk] = jnp.mean(ak[blk * MOBA_BLOCK:(blk + 1) * MOBA_BLOCK], axis=0, keepdims=True)
    lane128 = lax.broadcasted_iota(jnp.int32, (x.shape[0], 128), 1)
    for pair in range(ATTN_HEADS // 2):
        aq_ref[pair] = aq[:, pair * 128:(pair + 1) * 128]
    for hd in range(ATTN_HEADS):
        pair, half = divmod(hd, 2)
        in_head = (lane128 // ATTN_HEAD_DIM) == half
        ak_ref[hd] = jnp.where(in_head, ak[:, pair * 128:(pair + 1) * 128], 0.0).astype(BF16)
        av_ref[hd] = av[:, hd * ATTN_HEAD_DIM:(hd + 1) * ATTN_HEAD_DIM].astype(BF16)


def _proj(x, scale1, shift1, norm_g, w_in_bf16, lb, ctab, stab, b0):
    _, S, D = x.shape
    B = scale1.shape[0]
    hw = lb.shape[-1]
    aw = ATTN_HEADS * ATTN_HEAD_DIM
    tm = PROJ_ROWS
    nb = S // MOBA_BLOCK
    n_proj = w_in_bf16.shape[1]
    row = lambda b, i: (b, i, 0)
    xrow = lambda b, i: (b0 + b, i, 0)
    vec = lambda b, i: (b, 0, 0)
    head = lambda b, i: (b, 0, i, 0)
    out_shapes = (
        jax.ShapeDtypeStruct((B, S, hw), BF16),
        jax.ShapeDtypeStruct((B, S, hw), BF16),
        jax.ShapeDtypeStruct((B, S, hw), F32),
        jax.ShapeDtypeStruct((B, S, hw), BF16),
        jax.ShapeDtypeStruct((B, S, hw), BF16),
        jax.ShapeDtypeStruct((B, ATTN_HEADS // 2, S, 128), F32),
        jax.ShapeDtypeStruct((B, ATTN_HEADS, S, 128), BF16),
        jax.ShapeDtypeStruct((B, ATTN_HEADS, S, ATTN_HEAD_DIM), BF16),
        jax.ShapeDtypeStruct((B, nb, 1, aw), F32),
    )
    hspec = pl.BlockSpec((None, tm, hw), row)
    aspec = pl.BlockSpec((None, ATTN_HEADS, tm, ATTN_HEAD_DIM), head)
    return pl.pallas_call(
        functools.partial(_proj_kernel, hw=hw, aw=aw),
        out_shape=out_shapes,
        grid=(B, S // tm),
        in_specs=[pl.BlockSpec((None, tm, D), xrow),
                  pl.BlockSpec((None, 1, D), vec),
                  pl.BlockSpec((None, 1, D), vec),
                  pl.BlockSpec((1, D), lambda b, i: (0, 0)),
                  pl.BlockSpec((D, n_proj), lambda b, i: (0, 0)),
                  pl.BlockSpec((1, hw), lambda b, i: (0, 0)),
                  pl.BlockSpec((None, tm, 128), xrow),
                  pl.BlockSpec((None, tm, 128), xrow)],
        out_specs=(hspec, hspec, hspec, hspec, hspec,
                   pl.BlockSpec((None, ATTN_HEADS // 2, tm, 128), head),
                   pl.BlockSpec((None, ATTN_HEADS, tm, 128), head), aspec,
                   pl.BlockSpec((None, tm // MOBA_BLOCK, 1, aw), lambda b, i: (b, i, 0, 0))),
        compiler_params=pltpu.CompilerParams(
            dimension_semantics=("arbitrary", "arbitrary"), vmem_limit_bytes=VMEM_LIMIT),
        name="proj",
    )(x, scale1, shift1, norm_g, w_in_bf16, lb, ctab, stab)


def _hgrn_kernel(q_ref, k_ref, lf_ref, v_ref, gt_ref, gn_ref, o_ref, st_ref, *, n_heads, n_chunks):
    @pl.when(pl.program_id(1) == 0)
    def _():
        st_ref[...] = jnp.zeros_like(st_ref)

    C = HGRN_CHUNK
    r = lax.broadcasted_iota(jnp.int32, (C, C), 0)
    c = lax.broadcasted_iota(jnp.int32, (C, C), 1)
    tril = c <= r
    ltri = tril.astype(F32)
    gn = gn_ref[...]

    def chunk(ci, carry):
        r0 = pl.multiple_of(ci * C, C)
        rows = pl.ds(r0, C)
        b_all = _dot(ltri, lf_ref[rows, :], precision=HIGHEST)
        heads = range(n_heads)
        sls = [slice(hd * HGRN_DK, (hd + 1) * HGRN_DK) for hd in heads]
        bs = [b_all[:, sl] for sl in sls]
        b_lasts = [b[C - 1:C, :] for b in bs]
        qs = [q_ref[rows, sl].astype(F32) for sl in sls]
        ks = [k_ref[rows, sl].astype(F32) for sl in sls]
        vs = [v_ref[rows, sl] for sl in sls]
        states = [st_ref[hd] for hd in heads]
        o_inter = [_dot_nt((qs[hd] * jnp.exp(bs[hd])).astype(BF16), states[hd].astype(BF16)) for hd in heads]
        scores = []
        for hd in heads:
            blocks = []
            for g0 in range(0, C, HGRN_SUB):
                g1 = g0 + HGRN_SUB
                rho = 0.5 * (bs[hd][g0:g0 + 1, :] + bs[hd][g1 - 1:g1, :])
                qa = qs[hd][g0:g1, :] * jnp.exp(jnp.minimum(bs[hd][g0:g1, :] - rho, EXP_CLAMP))
                kb = ks[hd] * jnp.exp(jnp.minimum(rho - bs[hd], EXP_CLAMP))
                blocks.append(_dot_nt(qa.astype(BF16), kb.astype(BF16)))
            scores.append(jnp.where(tril, jnp.concatenate(blocks, axis=0), 0.0).astype(BF16))
        outs = [o_inter[hd] + _dot(scores[hd], vs[hd]) for hd in heads]
        kds = [(ks[hd] * jnp.exp(b_lasts[hd] - bs[hd])).astype(BF16) for hd in heads]
        upds = [_dot(vs[hd].astype(F32).T.astype(BF16), kds[hd]) for hd in heads]
        for hd in heads:
            st_ref[hd] = states[hd] * jnp.exp(b_lasts[hd]) + upds[hd]
            o = outs[hd]
            ms = jnp.mean(o * o, axis=-1, keepdims=True)
            o_ref[rows, sls[hd]] = (o * lax.rsqrt(ms + RMS_EPS) * gn * gt_ref[rows, sls[hd]].astype(F32)).astype(BF16)
        return carry

    lax.fori_loop(0, n_chunks, chunk, 0, unroll=True)


def _hgrn(q, k, lf, v, gt, norm_g):
    B, S, hw = q.shape
    n_heads = hw // HGRN_DK
    tc = HGRN_ROWS
    spec = pl.BlockSpec((None, tc, hw), lambda b, i: (b, i, 0))
    return pl.pallas_call(
        functools.partial(_hgrn_kernel, n_heads=n_heads, n_chunks=tc // HGRN_CHUNK),
        out_shape=jax.ShapeDtypeStruct((B, S, hw), BF16),
        grid=(B, S // tc),
        in_specs=[spec, spec, spec, spec, spec, pl.BlockSpec((1, HGRN_DK), lambda b, i: (0, 0))],
        out_specs=spec,
        scratch_shapes=[pltpu.VMEM((n_heads, HGRN_DK, HGRN_DK), F32)],
        compiler_params=pltpu.CompilerParams(
            dimension_semantics=("arbitrary", "arbitrary"), vmem_limit_bytes=VMEM_LIMIT),
        name="hgrn",
    )(q, k, lf, v, gt, norm_g)


def _sc_move_rows(table, src, dst, n_out, chunk):
    M = src.shape[0]
    D = table.shape[1]
    n_workers = SC_CORES * SC_SUBCORES
    per_worker = M // n_workers
    n_chunks = per_worker // chunk
    assert per_worker * n_workers == M and n_chunks * chunk == per_worker and n_chunks % 2 == 0 and chunk % 8 == 0
    mesh = plsc.VectorSubcoreMesh(core_axis_name="c", subcore_axis_name="s")
    idx_t = pltpu.VMEM((chunk,), jnp.int32)
    row_t = pltpu.VMEM((chunk, D), table.dtype)
    sem_t = pltpu.SemaphoreType.DMA

    def body(table_hbm, src_hbm, dst_hbm, out_hbm, src_v, dst_v, rows_v, g_sem, s_sem):
        wid = lax.axis_index("s") * SC_CORES + lax.axis_index("c")
        base = wid * per_worker

        def offset(j):
            return pl.multiple_of(base + j * chunk, 8)

        def gather(b):
            return pltpu.make_async_copy(table_hbm.at[src_v[b]], rows_v[b], g_sem[b])

        def start_gather(j, b):
            pltpu.sync_copy(src_hbm.at[pl.ds(offset(j), chunk)], src_v[b])
            gather(b).start()

        def write_out(j, b):
            if dst_hbm is None:
                pltpu.sync_copy(rows_v[b], out_hbm.at[pl.ds(offset(j), chunk)])
            else:
                pltpu.sync_copy(dst_hbm.at[pl.ds(offset(j), chunk)], dst_v[b])
                pltpu.async_copy(rows_v[b], out_hbm.at[dst_v[b]], s_sem[b]).wait()

        start_gather(0, 0)

        @pl.loop(0, n_chunks, step=2)
        def _(j):
            for b in (0, 1):
                @pl.when(j + b + 1 < n_chunks)
                def _():
                    start_gather(j + b + 1, 1 - b)
                gather(b).wait()
                write_out(j + b, b)

    if dst is None:
        @functools.partial(pl.kernel, mesh=mesh, out_type=jax.ShapeDtypeStruct((n_out, D), table.dtype),
                           scratch_types=[idx_t, idx_t, row_t, row_t, sem_t, sem_t])
        def gather_kernel(table_hbm, src_hbm, out_hbm, s0, s1, r0, r1, g0, g1):
            body(table_hbm, src_hbm, None, out_hbm, (s0, s1), None, (r0, r1), (g0, g1), None)
        return gather_kernel(table, src)

    @functools.partial(pl.kernel, mesh=mesh, out_type=jax.ShapeDtypeStruct((n_out, D), table.dtype),
                       scratch_types=[idx_t, idx_t, idx_t, idx_t, row_t, row_t, sem_t, sem_t, sem_t, sem_t])
    def permute_kernel(table_hbm, src_hbm, dst_hbm, out_hbm, s0, s1, d0, d1, r0, r1, g0, g1, w0, w1):
        body(table_hbm, src_hbm, dst_hbm, out_hbm, (s0, s1), (d0, d1), (r0, r1), (g0, g1), (w0, w1))
    return permute_kernel(table, src, dst)


def _sc_gather(table, idx, chunk):
    return _sc_move_rows(table, idx, None, idx.shape[0], chunk)


def _sc_permute(table, src, dst, n_out, chunk):
    return _sc_move_rows(table, src, dst, n_out, chunk)


def _tile_layout(counts, bm, n_tiles):
    n_groups = counts.shape[0]
    padded = (counts + bm - 1) // bm * bm
    pad_end = jnp.cumsum(padded)
    tile_start = jnp.arange(n_tiles, dtype=jnp.int32) * bm
    tile_group = jnp.minimum(
        jnp.sum((pad_end[None, :] <= tile_start[:, None]).astype(jnp.int32), axis=1), n_groups - 1)
    n_used = (pad_end[-1] // bm).astype(jnp.int32).reshape(1)
    return pad_end - padded, tile_group.astype(jnp.int32), n_used


def _null_partial(rows):
    lane = lax.broadcasted_iota(jnp.int32, (rows, PART_W), 1)
    return jnp.where(lane < ATTN_HEAD_DIM, 0.0, -jnp.inf).astype(F32)


def _moba_sel_kernel(q_ref, km_ref, k_ref, v_ref, idx_ref, cnt_ref, own_ref, cnt_acc, *, n_blocks):
    j = pl.program_id(1)
    T = MOBA_BLOCK
    heads = range(ATTN_HEADS)
    qs = [q_ref[hd // 2] for hd in heads]
    gates = [_dot_nt(km_ref[hd], qs[hd], precision=HIGHEST) for hd in heads]
    blk = lax.broadcasted_iota(jnp.int32, gates[0].shape, 0)
    neg_inf = jnp.float32(-jnp.inf)
    gates = [jnp.where(blk < j, g, neg_inf) for g in gates]
    picks = [[] for _ in heads]
    for _ in range(MOBA_TOPK):
        ms = [jnp.max(g, axis=0, keepdims=True) for g in gates]
        firsts = [jnp.min(jnp.where(g == m, blk, n_blocks), axis=0, keepdims=True) for g, m in zip(gates, ms)]
        for hd in heads:
            picks[hd].append(jnp.where(ms[hd] > neg_inf, firsts[hd], -1))
        gates = [jnp.where(blk == f, neg_inf, g) for g, f in zip(gates, firsts)]

    @pl.when(j == 0)
    def _():
        cnt_acc[...] = jnp.zeros_like(cnt_acc)

    earlier = (lax.broadcasted_iota(jnp.int32, (T, T), 0) < lax.broadcasted_iota(jnp.int32, (T, T), 1)).astype(BF16)
    for hd in heads:
        onehots = [(blk == p).astype(F32) for p in picks[hd]]
        member = onehots[0] + onehots[1] + onehots[2]
        base = cnt_acc[hd] + _dot(member.astype(BF16), earlier)
        ranks = [jnp.sum(oh * base, axis=0, keepdims=True).astype(jnp.int32) for oh in onehots]
        idx_ref[hd] = jnp.concatenate(picks[hd] + ranks + [jnp.zeros((2, T), jnp.int32)], axis=0)
        total = cnt_acc[hd] + jnp.sum(member, axis=1, keepdims=True)
        cnt_acc[hd] = total
        cnt_ref[hd] = total.astype(jnp.int32)
    causal = lax.broadcasted_iota(jnp.int32, (T, T), 1) <= lax.broadcasted_iota(jnp.int32, (T, T), 0)
    scale = ATTN_HEAD_DIM ** -0.5
    ss = [jnp.where(causal, _dot_nt((qs[hd] * scale).astype(BF16), k_ref[hd]), neg_inf) for hd in heads]
    mx = [jnp.max(s, axis=1, keepdims=True) for s in ss]
    ps = [jnp.exp(s - m) for s, m in zip(ss, mx)]
    ls = [jnp.sum(p, axis=1, keepdims=True) for p in ps]
    accs = [_dot(ps[hd].astype(BF16), v_ref[hd]) for hd in heads]
    for hd in heads:
        lse = jnp.broadcast_to(mx[hd] + jnp.log(ls[hd]), (T, PART_W - ATTN_HEAD_DIM))
        own_ref[hd] = jnp.concatenate([accs[hd] / ls[hd], lse], axis=1)


def _moba_sel(aq, kmean, ak, av):
    B, H, S, hd = av.shape
    nb = S // MOBA_BLOCK
    T = MOBA_BLOCK
    blk = lambda b, j: (b, 0, j, 0)
    return pl.pallas_call(
        functools.partial(_moba_sel_kernel, n_blocks=nb),
        out_shape=(jax.ShapeDtypeStruct((B, H, 8, S), jnp.int32),
                   jax.ShapeDtypeStruct((B, H, nb, 1), jnp.int32),
                   jax.ShapeDtypeStruct((B, H, S, PART_W), F32)),
        grid=(B, nb),
        in_specs=[pl.BlockSpec((None, H // 2, T, 128), blk),
                  pl.BlockSpec((None, H, nb, 128), lambda b, j: (b, 0, 0, 0)),
                  pl.BlockSpec((None, H, T, 128), blk),
                  pl.BlockSpec((None, H, T, hd), blk)],
        out_specs=(pl.BlockSpec((None, H, 8, T), lambda b, j: (b, 0, 0, j)),
                   pl.BlockSpec((None, H, nb, 1), lambda b, j: (b, 0, 0, 0)),
                   pl.BlockSpec((None, H, T, PART_W), blk)),
        scratch_shapes=[pltpu.VMEM((H, nb, 1), F32)],
        compiler_params=pltpu.CompilerParams(
            dimension_semantics=("arbitrary", "arbitrary"), vmem_limit_bytes=VMEM_LIMIT),
        name="moba_sel",
    )(aq, kmean, ak, av)


def _moba_blk_kernel(tg_ref, nu_ref, q_ref, k_ref, v_ref, o_ref, *, n_blocks):
    n = MOBA_TILES_PER_STEP
    R = MOBA_ROWS
    t0 = pl.program_id(0) * n

    @pl.when(t0 < nu_ref[0])
    def _():
        scale = ATTN_HEAD_DIM ** -0.5
        groups = [tg_ref[t0 + j] for j in range(n)]
        kv_rows = [(g // n_blocks, pl.ds(pl.multiple_of((g % n_blocks) * MOBA_BLOCK, MOBA_BLOCK), MOBA_BLOCK))
                   for g in groups]
        ss = [_dot_nt((q_ref[j * R:(j + 1) * R, :] * scale).astype(BF16), k_ref[kv_rows[j][0], kv_rows[j][1], :])
              for j in range(n)]
        ms = [jnp.max(s, axis=1, keepdims=True) for s in ss]
        ps = [jnp.exp(s - m) for s, m in zip(ss, ms)]
        ls = [jnp.sum(p, axis=1, keepdims=True) for p in ps]
        accs = [_dot(p.astype(BF16), v_ref[kv_rows[j][0], kv_rows[j][1], :]) for j, p in enumerate(ps)]
        null = _null_partial(R)
        for j in range(n):
            lse = jnp.broadcast_to(ms[j] + jnp.log(ls[j]), (R, PART_W - ATTN_HEAD_DIM))
            row = jnp.concatenate([accs[j] / ls[j], lse], axis=1)
            o_ref[j * R:(j + 1) * R, :] = jnp.where(t0 + j < nu_ref[0], row, null)

    @pl.when(t0 >= nu_ref[0])
    def _():
        o_ref[...] = _null_partial(n * R)


def _moba_blk(qs, tile_group, n_used, ak, av):
    B, H, S, hd = av.shape
    nb = S // MOBA_BLOCK
    R = MOBA_ROWS
    n = MOBA_TILES_PER_STEP
    n_tiles = qs.shape[0] // R
    assert n_tiles % n == 0
    whole = lambda i, tg, nu: (0, 0, 0)
    grid_spec = pltpu.PrefetchScalarGridSpec(
        num_scalar_prefetch=2,
        grid=(n_tiles // n,),
        in_specs=[pl.BlockSpec((n * R, 128), lambda i, tg, nu: (i, 0)),
                  pl.BlockSpec((B * H, S, 128), whole, pipeline_mode=pl.Buffered(1)),
                  pl.BlockSpec((B * H, S, hd), whole, pipeline_mode=pl.Buffered(1))],
        out_specs=pl.BlockSpec((n * R, PART_W), lambda i, tg, nu: (i, 0)),
    )
    return pl.pallas_call(
        functools.partial(_moba_blk_kernel, n_blocks=nb),
        out_shape=jax.ShapeDtypeStruct((n_tiles * R, PART_W), F32),
        grid_spec=grid_spec,
        compiler_params=pltpu.CompilerParams(
            dimension_semantics=("arbitrary",), vmem_limit_bytes=VMEM_LIMIT),
        name="moba_blk",
    )(tile_group, n_used, qs, ak.reshape(B * H, S, 128), av.reshape(B * H, S, hd))


def _moba_merge_kernel(own_ref, pg_ref, g_ref, o_ref):
    hd = ATTN_HEAD_DIM
    rows = [own_ref[...]] + [pg_ref[s] for s in range(MOBA_TOPK)]
    lses = [pltpu.roll(r, hd, 1) for r in rows]
    top = lses[0]
    for z in lses[1:]:
        top = jnp.maximum(top, z)
    num = jnp.zeros_like(top)
    den = jnp.zeros_like(top)
    for r, z in zip(rows, lses):
        w = jnp.exp(z - top)
        num = num + w * r
        den = den + w
    o = (num / den)[:, :hd]
    ms = jnp.mean(o * o, axis=-1, keepdims=True)
    o_ref[...] = o * lax.rsqrt(ms + RMS_EPS) * g_ref[...]


def _moba_merge(own, pg, norm_g):
    n = own.shape[0]
    T = MERGE_ROWS
    row = lambda i: (i, 0)
    return pl.pallas_call(
        _moba_merge_kernel,
        out_shape=jax.ShapeDtypeStruct((n, ATTN_HEAD_DIM), F32),
        grid=(n // T,),
        in_specs=[pl.BlockSpec((T, PART_W), row),
                  pl.BlockSpec((MOBA_TOPK, T, PART_W), lambda i: (0, i, 0)),
                  pl.BlockSpec((1, ATTN_HEAD_DIM), lambda i: (0, 0))],
        out_specs=pl.BlockSpec((T, ATTN_HEAD_DIM), row),
        compiler_params=pltpu.CompilerParams(
            dimension_semantics=("arbitrary",), vmem_limit_bytes=VMEM_LIMIT),
        name="moba_merge",
    )(own, pg, norm_g)


def _moba(aq, km, ak, av, norm_g):
    B, H, S, hd = av.shape
    nb = S // MOBA_BLOCK
    n_q = B * H * S
    kmp = km.reshape(B, nb, H // 2, 128)
    half = jnp.arange(128, dtype=jnp.int32) // hd
    kmean = jnp.stack([jnp.where(half == h % 2, kmp[:, :, h // 2, :], 0.0) for h in range(H)], axis=1)
    idx8, counts, own = _moba_sel(aq, kmean, ak, av)
    sel = idx8[:, :, 0:MOBA_TOPK, :].reshape(B * H, MOBA_TOPK, S)
    rank = idx8[:, :, MOBA_TOPK:2 * MOBA_TOPK, :].reshape(B * H, MOBA_TOPK, S)
    n_groups = B * H * nb
    n_tiles = (n_q * MOBA_TOPK) // MOBA_ROWS + n_groups
    pad_start, tile_group, n_used = _tile_layout(counts.reshape(-1), MOBA_ROWS, n_tiles)
    blocks = jnp.arange(nb, dtype=jnp.int32)[:, None]
    start = jnp.sum(jnp.where(sel[:, :, None, :] == blocks, pad_start.reshape(B * H, 1, nb, 1), 0), axis=2)
    a_ids = jnp.arange(n_q * MOBA_TOPK, dtype=jnp.int32).reshape(B * H, MOBA_TOPK, S)
    assert n_tiles * MOBA_ROWS >= n_q * MOBA_TOPK + MOBA_ROWS
    pos = jnp.where(sel >= 0, start + rank, n_used[0] * MOBA_ROWS + a_ids % MOBA_ROWS)
    bh = jnp.arange(B * H, dtype=jnp.int32)[:, None, None]
    t = jnp.arange(S, dtype=jnp.int32)[None, None, :]
    pair_row = jnp.broadcast_to((bh // H * (H // 2) + bh % H // 2) * S + t, pos.shape)
    qs = _sc_permute(aq.reshape(B * (H // 2) * S, 128), pair_row.reshape(-1), pos.reshape(-1),
                     n_tiles * MOBA_ROWS, 256)
    parts = _moba_blk(qs, tile_group, n_used, ak, av)
    pg = _sc_gather(parts, pos.transpose(1, 0, 2).reshape(-1), 256)
    o = _moba_merge(own.reshape(n_q, PART_W), pg.reshape(MOBA_TOPK, n_q, PART_W), norm_g)
    return o.reshape(B, H, S, hd)


def _mix_kernel(oh_ref, oa_ref, x_ref, g1_ref, sc2_ref, sh2_ref, n2_ref, wo_ref, wr_ref, br_ref,
                x1_ref, h2_ref, gw_ref, idx_ref, cnt_ref, cnt_acc):
    cat = jnp.concatenate([oh_ref[...]] + [oa_ref[hd] for hd in range(ATTN_HEADS)], axis=1)
    mix = _dot(cat.astype(BF16), wo_ref[...])
    x1 = x_ref[...] + g1_ref[...] * mix
    x1_ref[...] = x1
    ms = jnp.mean(x1 * x1, axis=-1, keepdims=True)
    h2 = x1 * lax.rsqrt(ms + RMS_EPS) * n2_ref[...]
    h2 = h2 * (1.0 + sc2_ref[...]) + sh2_ref[...]
    h2_ref[...] = _pack_bf16_pairs(h2)
    E = N_EXPERTS
    tm = h2.shape[0]
    h_0 = h2.astype(BF16)
    r_1 = h2 - h_0.astype(F32)
    h_1 = r_1.astype(BF16)
    h_2 = (r_1 - h_1.astype(F32)).astype(BF16)
    wt = wr_ref[...]
    p_0 = _dot_nt(wt, h_0)
    p_1 = _dot_nt(wt[:2 * E], h_1)
    p_2 = _dot_nt(wt[:E], h_2)
    logits = (p_0[:E] + (p_0[E:2 * E] + p_1[:E]) + (p_0[2 * E:] + p_1[E:] + p_2)) + br_ref[...]
    ex = lax.broadcasted_iota(jnp.int32, logits.shape, 0)
    neg_inf = jnp.float32(-jnp.inf)
    vals, idxs = [], []
    for _ in range(TOP_K):
        m = jnp.max(logits, axis=0, keepdims=True)
        first = jnp.min(jnp.where(logits == m, ex, E), axis=0, keepdims=True)
        vals.append(m)
        idxs.append(first)
        logits = jnp.where(ex == first, neg_inf, logits)
    e = [jnp.exp(v - vals[0]) for v in vals]
    denom = e[0] + e[1] + e[2] + e[3]
    gate_rows = jnp.concatenate([ei / denom for ei in e] + [jnp.zeros((128 - TOP_K, tm), F32)], axis=0)
    gw_ref[...] = gate_rows.T[:, :TOP_K]

    @pl.when((pl.program_id(0) == 0) & (pl.program_id(1) == 0))
    def _():
        cnt_acc[...] = jnp.zeros_like(cnt_acc)

    earlier = (lax.broadcasted_iota(jnp.int32, (tm, tm), 0) < lax.broadcasted_iota(jnp.int32, (tm, tm), 1)).astype(BF16)
    onehots = [(ex == ix).astype(F32) for ix in idxs]
    member = onehots[0] + onehots[1] + onehots[2] + onehots[3]
    base = cnt_acc[...] + _dot(member.astype(BF16), earlier)
    ranks = [jnp.sum(oh * base, axis=0, keepdims=True).astype(jnp.int32) for oh in onehots]
    idx_ref[...] = jnp.concatenate(idxs + ranks, axis=0)
    total = cnt_acc[...] + jnp.sum(member, axis=1, keepdims=True)
    cnt_acc[...] = total
    cnt_ref[...] = total.astype(jnp.int32)


def _mix(oh, oa, x, gate1, scale2, shift2, norm2_g, w_out_bf16, w_router, b_router, b0):
    _, S, D = x.shape
    B = oh.shape[0]
    hw = oh.shape[-1]
    tm = MIX_ROWS
    row = lambda b, i: (b, i, 0)
    xrow = lambda b, i: (b0 + b, i, 0)
    vec = lambda b, i: (b, 0, 0)
    const = lambda b, i: (0, 0)
    return pl.pallas_call(
        _mix_kernel,
        out_shape=(jax.ShapeDtypeStruct((B, S, D), F32),
                   jax.ShapeDtypeStruct((B, S, D // 2), jnp.int32),
                   jax.ShapeDtypeStruct((B, S, TOP_K), F32),
                   jax.ShapeDtypeStruct((B, 2 * TOP_K, S), jnp.int32),
                   jax.ShapeDtypeStruct((N_EXPERTS, 1), jnp.int32)),
        grid=(B, S // tm),
        in_specs=[pl.BlockSpec((None, tm, hw), row),
                  pl.BlockSpec((None, ATTN_HEADS, tm, ATTN_HEAD_DIM), lambda b, i: (b, 0, i, 0)),
                  pl.BlockSpec((None, tm, D), xrow),
                  pl.BlockSpec((None, 1, D), vec),
                  pl.BlockSpec((None, 1, D), vec),
                  pl.BlockSpec((None, 1, D), vec),
                  pl.BlockSpec((1, D), const),
                  pl.BlockSpec((D, D), const),
                  pl.BlockSpec((3 * N_EXPERTS, D), const),
                  pl.BlockSpec((N_EXPERTS, 1), const)],
        out_specs=(pl.BlockSpec((None, tm, D), row),
                   pl.BlockSpec((None, tm, D // 2), row),
                   pl.BlockSpec((None, tm, TOP_K), row),
                   pl.BlockSpec((None, 2 * TOP_K, tm), lambda b, i: (b, 0, i)),
                   pl.BlockSpec((N_EXPERTS, 1), const)),
        scratch_shapes=[pltpu.VMEM((N_EXPERTS, 1), F32)],
        compiler_params=pltpu.CompilerParams(
            dimension_semantics=("arbitrary", "arbitrary"), vmem_limit_bytes=VMEM_LIMIT),
        name="mix",
    )(oh, oa, x, gate1, scale2, shift2, norm2_g, w_out_bf16, w_router, b_router)


def _moe_rows_kernel(be_ref, nx_ref, nu_ref, x_ref, wgu_hbm, bgu_ref, wd_hbm, bd_ref, y_ref,
                     wgu32, wd32, wgu16, wd16, sem, *, d_ff):
    i = pl.program_id(0)
    e = be_ref[i]

    def fetch(expert):
        return (pltpu.make_async_copy(wgu_hbm.at[expert], wgu32, sem.at[0]),
                pltpu.make_async_copy(wd_hbm.at[expert], wd32, sem.at[1]))

    @pl.when(i == 0)
    def _():
        for c in fetch(e):
            c.start()

    @pl.when((i == 0) | (e != be_ref[jnp.maximum(i - 1, 0)]))
    def _():
        for c in fetch(e):
            c.wait()
        wgu16[...] = wgu32[...].astype(BF16)
        wd16[...] = wd32[...].astype(BF16)

        @pl.when(nx_ref[i] >= 0)
        def _():
            for c in fetch(nx_ref[i]):
                c.start()

    @pl.when(i < nu_ref[0])
    def _():
        gu = _dot(_unpack_bf16_pairs(x_ref[...]).astype(BF16), wgu16[...]) + bgu_ref[...]
        gate = jnp.minimum(gu[:, :d_ff], SWIGLU_LIMIT)
        up = jnp.clip(gu[:, d_ff:], -SWIGLU_LIMIT, SWIGLU_LIMIT)
        act = (up + 1.0) * gate * _sigmoid(SWIGLU_ALPHA * gate)
        y_ref[...] = _pack_bf16_pairs(_dot(act.astype(BF16), wd16[...]) + bd_ref[...])

    @pl.when(i >= nu_ref[0])
    def _():
        y_ref[...] = jnp.zeros_like(y_ref)


def _moe_rows(xs, blk_expert, n_used, wgu, bgu, wd, bd):
    D = 2 * xs.shape[1]
    bm = MOE_ROWS
    n_blk = xs.shape[0] // bm
    d_ff = wd.shape[1]
    run_end = jnp.sum((blk_expert[None, :] <= blk_expert[:, None]).astype(jnp.int32), axis=1)
    next_expert = jnp.where(run_end < n_blk, blk_expert[jnp.minimum(run_end, n_blk - 1)], -1).astype(jnp.int32)
    bsel = lambda i, be, nx, nu: (be[i], 0, 0)
    rows = lambda i, be, nx, nu: (i, 0)
    grid_spec = pltpu.PrefetchScalarGridSpec(
        num_scalar_prefetch=3,
        grid=(n_blk,),
        in_specs=[pl.BlockSpec((bm, D // 2), rows),
                  pl.BlockSpec(memory_space=pl.ANY),
                  pl.BlockSpec((None, 1, 2 * d_ff), bsel),
                  pl.BlockSpec(memory_space=pl.ANY),
                  pl.BlockSpec((None, 1, D), bsel)],
        out_specs=pl.BlockSpec((bm, D // 2), rows),
        scratch_shapes=[pltpu.VMEM((D, 2 * d_ff), F32), pltpu.VMEM((d_ff, D), F32),
                        pltpu.VMEM((D, 2 * d_ff), BF16), pltpu.VMEM((d_ff, D), BF16),
                        pltpu.SemaphoreType.DMA((2,))],
    )
    return pl.pallas_call(
        functools.partial(_moe_rows_kernel, d_ff=d_ff),
        out_shape=jax.ShapeDtypeStruct((n_blk * bm, D // 2), jnp.int32),
        grid_spec=grid_spec,
        compiler_params=pltpu.CompilerParams(
            dimension_semantics=("arbitrary",), vmem_limit_bytes=VMEM_LIMIT),
        name="moe_rows",
    )(blk_expert, next_expert, n_used, xs, wgu, bgu.reshape(N_EXPERTS, 1, 2 * d_ff), wd, bd.reshape(N_EXPERTS, 1, D))


def _combine_rows_kernel(*refs):
    y_refs = refs[:TOP_K]
    gw_ref, x1_ref, g2_ref, fg_ref = refs[TOP_K:TOP_K + 4]
    o_ref = refs[-1]
    gw = gw_ref[...]
    y = gw[:, 0:1] * _unpack_bf16_pairs(y_refs[0][...])
    for kk in range(1, TOP_K):
        y = y + gw[:, kk:kk + 1] * _unpack_bf16_pairs(y_refs[kk][...])
    x2 = x1_ref[...] + g2_ref[...] * y
    ms = jnp.mean(x2 * x2, axis=-1, keepdims=True)
    o_ref[...] = x2 * lax.rsqrt(ms + RMS_EPS) * fg_ref[...]


def _combine_rows(yg, gates, x1, gate2, final_g, out_so_far, row0, n_out):
    S, D = x1.shape
    R = yg.shape[0] // TOP_K
    tm = COMBINE_ROWS
    steps = R // tm
    first = (row0 % S) // tm
    slot_spec = lambda kk: pl.BlockSpec((tm, D // 2), lambda i: (kk * steps + i, 0))
    in_specs = [slot_spec(kk) for kk in range(TOP_K)] + [
        pl.BlockSpec((tm, TOP_K), lambda i: (first + i, 0)),
        pl.BlockSpec((tm, D), lambda i: (first + i, 0)),
        pl.BlockSpec((1, D), lambda i: (0, 0)),
        pl.BlockSpec((1, D), lambda i: (0, 0))]
    args = [yg] * TOP_K + [gates, x1, gate2, final_g]
    aliases = {}
    if out_so_far is not None:
        in_specs.append(pl.BlockSpec(memory_space=pl.ANY))
        aliases = {len(args): 0}
        args.append(out_so_far)
    return pl.pallas_call(
        _combine_rows_kernel,
        out_shape=jax.ShapeDtypeStruct((n_out, D), F32),
        grid=(steps,),
        in_specs=in_specs,
        out_specs=pl.BlockSpec((tm, D), lambda i: (row0 // tm + i, 0)),
        input_output_aliases=aliases,
        compiler_params=pltpu.CompilerParams(
            dimension_semantics=("arbitrary",), vmem_limit_bytes=VMEM_LIMIT),
        name="combine_rows",
    )(*args)


def _split_bf16x3(w):
    def top(v):
        return lax.bitcast_convert_type(lax.bitcast_convert_type(v, jnp.int32) & jnp.int32(-65536), F32)
    w0 = top(w)
    w1 = top(w - w0)
    w2 = w - w0 - w1
    return jnp.concatenate([w0, w1, w2], axis=1).astype(BF16).T


def _rotary_tables(positions):
    B, S = positions.shape
    half = ROT_DIM // 2
    inv_freq = jnp.exp(-math.log(ROPE_THETA) * jnp.arange(0, ROT_DIM, 2, dtype=F32) / ROT_DIM)
    ang = inv_freq[:, None] * positions.astype(F32).reshape(1, B * S)
    d = jnp.arange(128, dtype=jnp.int32) % ATTN_HEAD_DIM
    rotary = d < ROT_DIM
    sel = ((d[None, :] % half == jnp.arange(half, dtype=jnp.int32)[:, None]) & rotary[None, :]).astype(F32)
    sign = jnp.where(d < half, -1.0, 1.0)
    spread = lambda t, w: lax.dot_general(t, w, (((0,), (0,)), ((), ())), precision=HIGHEST)
    ct = spread(jnp.cos(ang), sel) + jnp.where(rotary, 0.0, 1.0)
    st = spread(jnp.sin(ang), sel * sign)
    return ct.reshape(B, S, 128), st.reshape(B, S, 128)


def kernel(x, c, positions, w_ada, b_ada, norm1_g, w_in, hgrn_lb_logits, hgrn_norm_g, attn_norm_g,
           w_out, norm2_g, w_router, b_router, w_gate_up, b_gate_up, w_down, b_down, final_norm_g):
    B, S, D = x.shape
    assert w_in.shape[0] == 1, "single-layer block: the final norm is fused into the combine step"
    l = 0
    ctab, stab = _rotary_tables(positions)
    lower_bounds = jnp.cumsum(jax.nn.softmax(hgrn_lb_logits.astype(F32), axis=0), axis=0)
    mod = _ada(c, w_ada[l], b_ada[l])
    shift1, scale1, gate1, shift2, scale2, gate2 = jnp.split(mod[:, None, :], N_MOD, axis=-1)
    n_blk = (S * TOP_K) // MOE_ROWS + N_EXPERTS
    experts = jnp.arange(N_EXPERTS, dtype=jnp.int32)[:, None]
    tok = jnp.broadcast_to(jnp.arange(S, dtype=jnp.int32)[None, :], (TOP_K, S)).reshape(-1)
    out = None
    for b in range(B):
        one = slice(b, b + 1)
        q, k, lf, v, gt, aq, ak, av, km = _proj(
            x, scale1[one], shift1[one], norm1_g[l][None], w_in[l].astype(BF16), lower_bounds[l][None],
            ctab, stab, b)
        o_a = _moba(aq, km, ak, av, attn_norm_g[l][None])
        o_h = _hgrn(q, k, lf, v, gt, hgrn_norm_g[l][None])
        x1, h2, gates, idx8, counts = _mix(
            o_h, o_a, x, gate1[one], scale2[one], shift2[one], norm2_g[l][None], w_out[l].astype(BF16),
            _split_bf16x3(w_router[l]), b_router[l][:, None], b)
        pad_start, blk_expert, n_used = _tile_layout(counts.reshape(-1), MOE_ROWS, n_blk)
        chosen = idx8[0, 0:TOP_K, :]
        pos = jnp.sum(jnp.where(chosen[:, None, :] == experts, pad_start[:, None], 0), axis=1) + idx8[0, TOP_K:, :]
        xs = _sc_permute(h2.reshape(S, D // 2), tok, pos.reshape(-1), n_blk * MOE_ROWS, 64)
        y_sorted = _moe_rows(xs, blk_expert, n_used, w_gate_up[l], b_gate_up[l], w_down[l], b_down[l])
        half = S // RETURN_PARTS
        for part in range(RETURN_PARTS):
            yg = _sc_gather(y_sorted, pos[:, part * half:(part + 1) * half].reshape(-1), 64)
            out = _combine_rows(yg, gates.reshape(S, TOP_K), x1.reshape(S, D), gate2[b], final_norm_g[None], out,
                                b * S + part * half, B * S)
    return out.reshape(B, S, D)
```

```python
import functools
import math

import jax
import jax.numpy as jnp
from jax import lax
from jax.experimental import pallas as pl
from jax.experimental.pallas import tpu as pltpu
from jax.experimental.pallas import tpu_sc as plsc

F32 = jnp.float32
BF16 = jnp.bfloat16
HIGHEST = lax.Precision.HIGHEST

HGRN_DK = 128
HGRN_CHUNK = 64
ATTN_HEADS = 4
ATTN_HEAD_DIM = 64
ROT_DIM = ATTN_HEAD_DIM // 4
ROPE_THETA = 500000.0
MOBA_BLOCK = 256
MOBA_TOPK = 3
N_EXPERTS = 32
TOP_K = 4
SWIGLU_ALPHA = 1.702
SWIGLU_LIMIT = 7.0
N_MOD = 6
RMS_EPS = 1e-6

HGRN_SUB = 16
EXP_CLAMP = 80.0
PROJ_ROWS = 512
HGRN_ROWS = 1024
MERGE_ROWS = 4096
COMBINE_ROWS = 1024
MIX_ROWS = 1024
MOE_ROWS = 512
MOBA_ROWS = 256
MOBA_TILES_PER_STEP = 16
PART_W = 128
V7X_VMEM_BYTES = 64 * 1024 * 1024
VMEM_LIMIT = V7X_VMEM_BYTES * 7 // 8
SC_CORES = 2
SC_SUBCORES = 16


def _sigmoid(x):
    return 1.0 / (1.0 + jnp.exp(-x))


def _dot(a, b, **kw):
    return jnp.dot(a, b, preferred_element_type=F32, **kw)


def _dot_nt(a, b, **kw):
    return lax.dot_general(a, b, (((1,), (1,)), ((), ())), preferred_element_type=F32, **kw)


def _pack_bf16_pairs(x):
    w = x.shape[1] // 2
    bits = lax.bitcast_convert_type(x.astype(BF16).astype(F32), jnp.int32)
    return bits[:, w:] | lax.shift_right_logical(bits[:, :w], 16)


def _unpack_bf16_pairs(p):
    lo = lax.bitcast_convert_type(lax.shift_left(p, 16), F32)
    hi = lax.bitcast_convert_type(p & jnp.int32(-65536), F32)
    return jnp.concatenate([lo, hi], axis=1)


def _ada_kernel(c_ref, w_ref, b_ref, o_ref):
    c = c_ref[...]
    o_ref[...] = _dot(c * _sigmoid(c), w_ref[...], precision=HIGHEST) + b_ref[...]


def _ada(c, w_ada, b_ada):
    B, D = c.shape
    N = w_ada.shape[1]
    tn = N // 4
    c8 = jnp.zeros((8, D), F32).at[:B].set(c)
    out = pl.pallas_call(
        _ada_kernel,
        out_shape=jax.ShapeDtypeStruct((8, N), F32),
        grid=(N // tn,),
        in_specs=[pl.BlockSpec((8, D), lambda j: (0, 0)),
                  pl.BlockSpec((D, tn), lambda j: (0, j)),
                  pl.BlockSpec((1, tn), lambda j: (0, j))],
        out_specs=pl.BlockSpec((8, tn), lambda j: (0, j)),
        compiler_params=pltpu.CompilerParams(vmem_limit_bytes=VMEM_LIMIT),
        name="ada",
    )(c8, w_ada, b_ada.reshape(1, N))
    return out[:B]


def _proj_kernel(x_ref, sc_ref, sh_ref, g_ref, w_ref, lb_ref, ct_ref, st_ref,
                 q_ref, k_ref, lf_ref, v_ref, gt_ref, aq_ref, ak_ref, av_ref, km_ref,
                 *, hw, aw):
    x = x_ref[...]
    ms = jnp.mean(x * x, axis=-1, keepdims=True)
    h = x * lax.rsqrt(ms + RMS_EPS) * g_ref[...]
    h = h * (1.0 + sc_ref[...]) + sh_ref[...]
    proj = _dot(h.astype(BF16), w_ref[...])

    hq = proj[:, 0:hw]
    hf = proj[:, hw:2 * hw]
    hg = proj[:, 3 * hw:4 * hw]
    q_ref[...] = (hq * _sigmoid(hq) * (HGRN_DK ** -0.5)).astype(BF16)
    lb = lb_ref[...]
    f = lb + (1.0 - lb) * _sigmoid(hf)
    k_ref[...] = (1.0 - f).astype(BF16)
    lf_ref[...] = jnp.log(f)
    v_ref[...] = proj[:, 2 * hw:3 * hw].astype(BF16)
    gt_ref[...] = (hg * _sigmoid(hg)).astype(BF16)

    ct = jnp.concatenate([ct_ref[...]] * (aw // 128), axis=1)
    st = jnp.concatenate([st_ref[...]] * (aw // 128), axis=1)
    lane = lax.broadcasted_iota(jnp.int32, ct.shape, 1) % ATTN_HEAD_DIM
    first_half = lane < (ROT_DIM // 2)

    def rot(t):
        partner = jnp.where(first_half, pltpu.roll(t, aw - ROT_DIM // 2, 1), pltpu.roll(t, ROT_DIM // 2, 1))
        return t * ct + partner * st

    base = 4 * hw
    aq = rot(proj[:, base:base + aw])
    ak = rot(proj[:, base + aw:base + 2 * aw])
    av = proj[:, base + 2 * aw:base + 3 * aw]
    for blk in range(ak.shape[0] // MOBA_BLOCK):
        km_ref[blk] = jnp.mean(ak[blk * MOBA_BLOCK:(blk + 1) * MOBA_BLOCK], axis=0, keepdims=True)
    lane128 = lax.broadcasted_iota(jnp.int32, (x.shape[0], 128), 1)
    for pair in range(ATTN_HEADS // 2):
        aq_ref[pair] = aq[:, pair * 128:(pair + 1) * 128]
    for hd in range(ATTN_HEADS):
        pair, half = divmod(hd, 2)
        in_head = (lane128 // ATTN_HEAD_DIM) == half
        ak_ref[hd] = jnp.where(in_head, ak[:, pair * 128:(pair + 1) * 128], 0.0).astype(BF16)
        av_ref[hd] = av[:, hd * ATTN_HEAD_DIM:(hd + 1) * ATTN_HEAD_DIM].astype(BF16)


def _proj(x, scale1, shift1, norm_g, w_in_bf16, lb, ctab, stab, b0):
    _, S, D = x.shape
    B = scale1.shape[0]
    hw = lb.shape[-1]
    aw = ATTN_HEADS * ATTN_HEAD_DIM
    tm = PROJ_ROWS
    nb = S // MOBA_BLOCK
    n_proj = w_in_bf16.shape[1]
    row = lambda b, i: (b, i, 0)
    xrow = lambda b, i: (b0 + b, i, 0)
    vec = lambda b, i: (b, 0, 0)
    head = lambda b, i: (b, 0, i, 0)
    out_shapes = (
        jax.ShapeDtypeStruct((B, S, hw), BF16),
        jax.ShapeDtypeStruct((B, S, hw), BF16),
        jax.ShapeDtypeStruct((B, S, hw), F32),
        jax.ShapeDtypeStruct((B, S, hw), BF16),
        jax.ShapeDtypeStruct((B, S, hw), BF16),
        jax.ShapeDtypeStruct((B, ATTN_HEADS // 2, S, 128), F32),
        jax.ShapeDtypeStruct((B, ATTN_HEADS, S, 128), BF16),
        jax.ShapeDtypeStruct((B, ATTN_HEADS, S, ATTN_HEAD_DIM), BF16),
        jax.ShapeDtypeStruct((B, nb, 1, aw), F32),
    )
    hspec = pl.BlockSpec((None, tm, hw), row)
    aspec = pl.BlockSpec((None, ATTN_HEADS, tm, ATTN_HEAD_DIM), head)
    return pl.pallas_call(
        functools.partial(_proj_kernel, hw=hw, aw=aw),
        out_shape=out_shapes,
        grid=(B, S // tm),
        in_specs=[pl.BlockSpec((None, tm, D), xrow),
                  pl.BlockSpec((None, 1, D), vec),
                  pl.BlockSpec((None, 1, D), vec),
                  pl.BlockSpec((1, D), lambda b, i: (0, 0)),
                  pl.BlockSpec((D, n_proj), lambda b, i: (0, 0)),
                  pl.BlockSpec((1, hw), lambda b, i: (0, 0)),
                  pl.BlockSpec((None, tm, 128), xrow),
                  pl.BlockSpec((None, tm, 128), xrow)],
        out_specs=(hspec, hspec, hspec, hspec, hspec,
                   pl.BlockSpec((None, ATTN_HEADS // 2, tm, 128), head),
                   pl.BlockSpec((None, ATTN_HEADS, tm, 128), head), aspec,
                   pl.BlockSpec((None, tm // MOBA_BLOCK, 1, aw), lambda b, i: (b, i, 0, 0))),
        compiler_params=pltpu.CompilerParams(
            dimension_semantics=("arbitrary", "arbitrary"), vmem_limit_bytes=VMEM_LIMIT),
        name="proj",
    )(x, scale1, shift1, norm_g, w_in_bf16, lb, ctab, stab)


def _hgrn_kernel(q_ref, k_ref, lf_ref, v_ref, gt_ref, gn_ref, o_ref, st_ref, *, n_heads, n_chunks):
    @pl.when(pl.program_id(1) == 0)
    def _():
        st_ref[...] = jnp.zeros_like(st_ref)

    C = HGRN_CHUNK
    r = lax.broadcasted_iota(jnp.int32, (C, C), 0)
    c = lax.broadcasted_iota(jnp.int32, (C, C), 1)
    tril = c <= r
    ltri = tril.astype(F32)
    gn = gn_ref[...]

    def chunk(ci, carry):
        r0 = pl.multiple_of(ci * C, C)
        rows = pl.ds(r0, C)
        b_all = _dot(ltri, lf_ref[rows, :], precision=HIGHEST)
        heads = range(n_heads)
        sls = [slice(hd * HGRN_DK, (hd + 1) * HGRN_DK) for hd in heads]
        bs = [b_all[:, sl] for sl in sls]
        b_lasts = [b[C - 1:C, :] for b in bs]
        qs = [q_ref[rows, sl].astype(F32) for sl in sls]
        ks = [k_ref[rows, sl].astype(F32) for sl in sls]
        vs = [v_ref[rows, sl] for sl in sls]
        states = [st_ref[hd] for hd in heads]
        o_inter = [_dot_nt((qs[hd] * jnp.exp(bs[hd])).astype(BF16), states[hd].astype(BF16)) for hd in heads]
        scores = []
        for hd in heads:
            blocks = []
            for g0 in range(0, C, HGRN_SUB):
                g1 = g0 + HGRN_SUB
                rho = 0.5 * (bs[hd][g0:g0 + 1, :] + bs[hd][g1 - 1:g1, :])
                qa = qs[hd][g0:g1, :] * jnp.exp(jnp.minimum(bs[hd][g0:g1, :] - rho, EXP_CLAMP))
                kb = ks[hd] * jnp.exp(jnp.minimum(rho - bs[hd], EXP_CLAMP))
                blocks.append(_dot_nt(qa.astype(BF16), kb.astype(BF16)))
            scores.append(jnp.where(tril, jnp.concatenate(blocks, axis=0), 0.0).astype(BF16))
        outs = [o_inter[hd] + _dot(scores[hd], vs[hd]) for hd in heads]
        kds = [(ks[hd] * jnp.exp(b_lasts[hd] - bs[hd])).astype(BF16) for hd in heads]
        upds = [_dot(vs[hd].astype(F32).T.astype(BF16), kds[hd]) for hd in heads]
        for hd in heads:
            st_ref[hd] = states[hd] * jnp.exp(b_lasts[hd]) + upds[hd]
            o = outs[hd]
            ms = jnp.mean(o * o, axis=-1, keepdims=True)
            o_ref[rows, sls[hd]] = (o * lax.rsqrt(ms + RMS_EPS) * gn * gt_ref[rows, sls[hd]].astype(F32)).astype(BF16)
        return carry

    lax.fori_loop(0, n_chunks, chunk, 0, unroll=True)


def _hgrn(q, k, lf, v, gt, norm_g):
    B, S, hw = q.shape
    n_heads = hw // HGRN_DK
    tc = HGRN_ROWS
    spec = pl.BlockSpec((None, tc, hw), lambda b, i: (b, i, 0))
    return pl.pallas_call(
        functools.partial(_hgrn_kernel, n_heads=n_heads, n_chunks=tc // HGRN_CHUNK),
        out_shape=jax.ShapeDtypeStruct((B, S, hw), BF16),
        grid=(B, S // tc),
        in_specs=[spec, spec, spec, spec, spec, pl.BlockSpec((1, HGRN_DK), lambda b, i: (0, 0))],
        out_specs=spec,
        scratch_shapes=[pltpu.VMEM((n_heads, HGRN_DK, HGRN_DK), F32)],
        compiler_params=pltpu.CompilerParams(
            dimension_semantics=("arbitrary", "arbitrary"), vmem_limit_bytes=VMEM_LIMIT),
        name="hgrn",
    )(q, k, lf, v, gt, norm_g)


def _sc_move_rows(table, src, dst, n_out, chunk):
    M = src.shape[0]
    D = table.shape[1]
    n_workers = SC_CORES * SC_SUBCORES
    per_worker = M // n_workers
    n_chunks = per_worker // chunk
    assert per_worker * n_workers == M and n_chunks * chunk == per_worker and n_chunks % 2 == 0 and chunk % 8 == 0
    mesh = plsc.VectorSubcoreMesh(core_axis_name="c", subcore_axis_name="s")
    idx_t = pltpu.VMEM((chunk,), jnp.int32)
    row_t = pltpu.VMEM((chunk, D), table.dtype)
    sem_t = pltpu.SemaphoreType.DMA

    def body(table_hbm, src_hbm, dst_hbm, out_hbm, src_v, dst_v, rows_v, g_sem, s_sem):
        wid = lax.axis_index("s") * SC_CORES + lax.axis_index("c")
        base = wid * per_worker

        def offset(j):
            return pl.multiple_of(base + j * chunk, 8)

        def gather(b):
            return pltpu.make_async_copy(table_hbm.at[src_v[b]], rows_v[b], g_sem[b])

        def start_gather(j, b):
            pltpu.sync_copy(src_hbm.at[pl.ds(offset(j), chunk)], src_v[b])
            gather(b).start()

        def write_out(j, b):
            if dst_hbm is None:
                pltpu.sync_copy(rows_v[b], out_hbm.at[pl.ds(offset(j), chunk)])
            else:
                pltpu.sync_copy(dst_hbm.at[pl.ds(offset(j), chunk)], dst_v[b])
                pltpu.async_copy(rows_v[b], out_hbm.at[dst_v[b]], s_sem[b]).wait()

        start_gather(0, 0)

        @pl.loop(0, n_chunks, step=2)
        def _(j):
            for b in (0, 1):
                @pl.when(j + b + 1 < n_chunks)
                def _():
                    start_gather(j + b + 1, 1 - b)
                gather(b).wait()
                write_out(j + b, b)

    if dst is None:
        @functools.partial(pl.kernel, mesh=mesh, out_type=jax.ShapeDtypeStruct((n_out, D), table.dtype),
                           scratch_types=[idx_t, idx_t, row_t, row_t, sem_t, sem_t])
        def gather_kernel(table_hbm, src_hbm, out_hbm, s0, s1, r0, r1, g0, g1):
            body(table_hbm, src_hbm, None, out_hbm, (s0, s1), None, (r0, r1), (g0, g1), None)
        return gather_kernel(table, src)

    @functools.partial(pl.kernel, mesh=mesh, out_type=jax.ShapeDtypeStruct((n_out, D), table.dtype),
                       scratch_types=[idx_t, idx_t, idx_t, idx_t, row_t, row_t, sem_t, sem_t, sem_t, sem_t])
    def permute_kernel(table_hbm, src_hbm, dst_hbm, out_hbm, s0, s1, d0, d1, r0, r1, g0, g1, w0, w1):
        body(table_hbm, src_hbm, dst_hbm, out_hbm, (s0, s1), (d0, d1), (r0, r1), (g0, g1), (w0, w1))
    return permute_kernel(table, src, dst)


def _sc_gather(table, idx, chunk):
    return _sc_move_rows(table, idx, None, idx.shape[0], chunk)


def _sc_permute(table, src, dst, n_out, chunk):
    return _sc_move_rows(table, src, dst, n_out, chunk)


def _tile_layout(counts, bm, n_tiles):
    n_groups = counts.shape[0]
    padded = (counts + bm - 1) // bm * bm
    pad_end = jnp.cumsum(padded)
    tile_start = jnp.arange(n_tiles, dtype=jnp.int32) * bm
    tile_group = jnp.minimum(
        jnp.sum((pad_end[None, :] <= tile_start[:, None]).astype(jnp.int32), axis=1), n_groups - 1)
    n_used = (pad_end[-1] // bm).astype(jnp.int32).reshape(1)
    return pad_end - padded, tile_group.astype(jnp.int32), n_used


def _null_partial(rows):
    lane = lax.broadcasted_iota(jnp.int32, (rows, PART_W), 1)
    return jnp.where(lane < ATTN_HEAD_DIM, 0.0, -jnp.inf).astype(F32)


def _moba_sel_kernel(q_ref, km_ref, k_ref, v_ref, idx_ref, cnt_ref, own_ref, cnt_acc, *, n_blocks):
    j = pl.program_id(1)
    T = MOBA_BLOCK
    heads = range(ATTN_HEADS)
    qs = [q_ref[hd // 2] for hd in heads]
    gates = [_dot_nt(km_ref[hd], qs[hd], precision=HIGHEST) for hd in heads]
    blk = lax.broadcasted_iota(jnp.int32, gates[0].shape, 0)
    neg_inf = jnp.float32(-jnp.inf)
    gates = [jnp.where(blk < j, g, neg_inf) for g in gates]
    picks = [[] for _ in heads]
    for _ in range(MOBA_TOPK):
        ms = [jnp.max(g, axis=0, keepdims=True) for g in gates]
        firsts = [jnp.min(jnp.where(g == m, blk, n_blocks), axis=0, keepdims=True) for g, m in zip(gates, ms)]
        for hd in heads:
            picks[hd].append(jnp.where(ms[hd] > neg_inf, firsts[hd], -1))
        gates = [jnp.where(blk == f, neg_inf, g) for g, f in zip(gates, firsts)]

    @pl.when(j == 0)
    def _():
        cnt_acc[...] = jnp.zeros_like(cnt_acc)

    earlier = (lax.broadcasted_iota(jnp.int32, (T, T), 0) < lax.broadcasted_iota(jnp.int32, (T, T), 1)).astype(BF16)
    for hd in heads:
        onehots = [(blk == p).astype(F32) for p in picks[hd]]
        member = onehots[0] + onehots[1] + onehots[2]
        base = cnt_acc[hd] + _dot(member.astype(BF16), earlier)
        ranks = [jnp.sum(oh * base, axis=0, keepdims=True).astype(jnp.int32) for oh in onehots]
        idx_ref[hd] = jnp.concatenate(picks[hd] + ranks + [jnp.zeros((2, T), jnp.int32)], axis=0)
        total = cnt_acc[hd] + jnp.sum(member, axis=1, keepdims=True)
        cnt_acc[hd] = total
        cnt_ref[hd] = total.astype(jnp.int32)
    causal = lax.broadcasted_iota(jnp.int32, (T, T), 1) <= lax.broadcasted_iota(jnp.int32, (T, T), 0)
    scale = ATTN_HEAD_DIM ** -0.5
    ss = [jnp.where(causal, _dot_nt((qs[hd] * scale).astype(BF16), k_ref[hd]), neg_inf) for hd in heads]
    mx = [jnp.max(s, axis=1, keepdims=True) for s in ss]
    ps = [jnp.exp(s - m) for s, m in zip(ss, mx)]
    ls = [jnp.sum(p, axis=1, keepdims=True) for p in ps]
    accs = [_dot(ps[hd].astype(BF16), v_ref[hd]) for hd in heads]
    for hd in heads:
        lse = jnp.broadcast_to(mx[hd] + jnp.log(ls[hd]), (T, PART_W - ATTN_HEAD_DIM))
        own_ref[hd] = jnp.concatenate([accs[hd] / ls[hd], lse], axis=1)


def _moba_sel(aq, kmean, ak, av):
    B, H, S, hd = av.shape
    nb = S // MOBA_BLOCK
    T = MOBA_BLOCK
    blk = lambda b, j: (b, 0, j, 0)
    return pl.pallas_call(
        functools.partial(_moba_sel_kernel, n_blocks=nb),
        out_shape=(jax.ShapeDtypeStruct((B, H, 8, S), jnp.int32),
                   jax.ShapeDtypeStruct((B, H, nb, 1), jnp.int32),
                   jax.ShapeDtypeStruct((B, H, S, PART_W), F32)),
        grid=(B, nb),
        in_specs=[pl.BlockSpec((None, H // 2, T, 128), blk),
                  pl.BlockSpec((None, H, nb, 128), lambda b, j: (b, 0, 0, 0)),
                  pl.BlockSpec((None, H, T, 128), blk),
                  pl.BlockSpec((None, H, T, hd), blk)],
        out_specs=(pl.BlockSpec((None, H, 8, T), lambda b, j: (b, 0, 0, j)),
                   pl.BlockSpec((None, H, nb, 1), lambda b, j: (b, 0, 0, 0)),
                   pl.BlockSpec((None, H, T, PART_W), blk)),
        scratch_shapes=[pltpu.VMEM((H, nb, 1), F32)],
        compiler_params=pltpu.CompilerParams(
            dimension_semantics=("arbitrary", "arbitrary"), vmem_limit_bytes=VMEM_LIMIT),
        name="moba_sel",
    )(aq, kmean, ak, av)


def _moba_blk_kernel(tg_ref, nu_ref, q_ref, k_ref, v_ref, o_ref, *, n_blocks):
    n = MOBA_TILES_PER_STEP
    R = MOBA_ROWS
    t0 = pl.program_id(0) * n

    @pl.when(t0 < nu_ref[0])
    def _():
        scale = ATTN_HEAD_DIM ** -0.5
        groups = [tg_ref[t0 + j] for j in range(n)]
        kv_rows = [(g // n_blocks, pl.ds(pl.multiple_of((g % n_blocks) * MOBA_BLOCK, MOBA_BLOCK), MOBA_BLOCK))
                   for g in groups]
        ss = [_dot_nt((q_ref[j * R:(j + 1) * R, :] * scale).astype(BF16), k_ref[kv_rows[j][0], kv_rows[j][1], :])
              for j in range(n)]
        ms = [jnp.max(s, axis=1, keepdims=True) for s in ss]
        ps = [jnp.exp(s - m) for s, m in zip(ss, ms)]
        ls = [jnp.sum(p, axis=1, keepdims=True) for p in ps]
        accs = [_dot(p.astype(BF16), v_ref[kv_rows[j][0], kv_rows[j][1], :]) for j, p in enumerate(ps)]
        null = _null_partial(R)
        for j in range(n):
            lse = jnp.broadcast_to(ms[j] + jnp.log(ls[j]), (R, PART_W - ATTN_HEAD_DIM))
            row = jnp.concatenate([accs[j] / ls[j], lse], axis=1)
            o_ref[j * R:(j + 1) * R, :] = jnp.where(t0 + j < nu_ref[0], row, null)

    @pl.when(t0 >= nu_ref[0])
    def _():
        o_ref[...] = _null_partial(n * R)


def _moba_blk(qs, tile_group, n_used, ak, av):
    B, H, S, hd = av.shape
    nb = S // MOBA_BLOCK
    R = MOBA_ROWS
    n = MOBA_TILES_PER_STEP
    n_tiles = qs.shape[0] // R
    assert n_tiles % n == 0
    whole = lambda i, tg, nu: (0, 0, 0)
    grid_spec = pltpu.PrefetchScalarGridSpec(
        num_scalar_prefetch=2,
        grid=(n_tiles // n,),
        in_specs=[pl.BlockSpec((n * R, 128), lambda i, tg, nu: (i, 0)),
                  pl.BlockSpec((B * H, S, 128), whole, pipeline_mode=pl.Buffered(1)),
                  pl.BlockSpec((B * H, S, hd), whole, pipeline_mode=pl.Buffered(1))],
        out_specs=pl.BlockSpec((n * R, PART_W), lambda i, tg, nu: (i, 0)),
    )
    return pl.pallas_call(
        functools.partial(_moba_blk_kernel, n_blocks=nb),
        out_shape=jax.ShapeDtypeStruct((n_tiles * R, PART_W), F32),
        grid_spec=grid_spec,
        compiler_params=pltpu.CompilerParams(
            dimension_semantics=("arbitrary",), vmem_limit_bytes=VMEM_LIMIT),
        name="moba_blk",
    )(tile_group, n_used, qs, ak.reshape(B * H, S, 128), av.reshape(B * H, S, hd))


def _moba_merge_kernel(own_ref, pg_ref, g_ref, o_ref):
    hd = ATTN_HEAD_DIM
    rows = [own_ref[...]] + [pg_ref[s] for s in range(MOBA_TOPK)]
    lses = [pltpu.roll(r, hd, 1) for r in rows]
    top = lses[0]
    for z in lses[1:]:
        top = jnp.maximum(top, z)
    num = jnp.zeros_like(top)
    den = jnp.zeros_like(top)
    for r, z in zip(rows, lses):
        w = jnp.exp(z - top)
        num = num + w * r
        den = den + w
    o = (num / den)[:, :hd]
    ms = jnp.mean(o * o, axis=-1, keepdims=True)
    o_ref[...] = o * lax.rsqrt(ms + RMS_EPS) * g_ref[...]


def _moba_merge(own, pg, norm_g):
    n = own.shape[0]
    T = MERGE_ROWS
    row = lambda i: (i, 0)
    return pl.pallas_call(
        _moba_merge_kernel,
        out_shape=jax.ShapeDtypeStruct((n, ATTN_HEAD_DIM), F32),
        grid=(n // T,),
        in_specs=[pl.BlockSpec((T, PART_W), row),
                  pl.BlockSpec((MOBA_TOPK, T, PART_W), lambda i: (0, i, 0)),
                  pl.BlockSpec((1, ATTN_HEAD_DIM), lambda i: (0, 0))],
        out_specs=pl.BlockSpec((T, ATTN_HEAD_DIM), row),
        compiler_params=pltpu.CompilerParams(
            dimension_semantics=("arbitrary",), vmem_limit_bytes=VMEM_LIMIT),
        name="moba_merge",
    )(own, pg, norm_g)


def _moba(aq, km, ak, av, norm_g):
    B, H, S, hd = av.shape
    nb = S // MOBA_BLOCK
    n_q = B * H * S
    kmp = km.reshape(B, nb, H // 2, 128)
    half = jnp.arange(128, dtype=jnp.int32) // hd
    kmean = jnp.stack([jnp.where(half == h % 2, kmp[:, :, h // 2, :], 0.0) for h in range(H)], axis=1)
    idx8, counts, own = _moba_sel(aq, kmean, ak, av)
    sel = idx8[:, :, 0:MOBA_TOPK, :].reshape(B * H, MOBA_TOPK, S)
    rank = idx8[:, :, MOBA_TOPK:2 * MOBA_TOPK, :].reshape(B * H, MOBA_TOPK, S)
    n_groups = B * H * nb
    n_tiles = (n_q * MOBA_TOPK) // MOBA_ROWS + n_groups
    pad_start, tile_group, n_used = _tile_layout(counts.reshape(-1), MOBA_ROWS, n_tiles)
    blocks = jnp.arange(nb, dtype=jnp.int32)[:, None]
    start = jnp.sum(jnp.where(sel[:, :, None, :] == blocks, pad_start.reshape(B * H, 1, nb, 1), 0), axis=2)
    a_ids = jnp.arange(n_q * MOBA_TOPK, dtype=jnp.int32).reshape(B * H, MOBA_TOPK, S)
    assert n_tiles * MOBA_ROWS >= n_q * MOBA_TOPK + MOBA_ROWS
    pos = jnp.where(sel >= 0, start + rank, n_used[0] * MOBA_ROWS + a_ids % MOBA_ROWS)
    bh = jnp.arange(B * H, dtype=jnp.int32)[:, None, None]
    t = jnp.arange(S, dtype=jnp.int32)[None, None, :]
    pair_row = jnp.broadcast_to((bh // H * (H // 2) + bh % H // 2) * S + t, pos.shape)
    qs = _sc_permute(aq.reshape(B * (H // 2) * S, 128), pair_row.reshape(-1), pos.reshape(-1),
                     n_tiles * MOBA_ROWS, 256)
    parts = _moba_blk(qs, tile_group, n_used, ak, av)
    pg = _sc_gather(parts, pos.transpose(1, 0, 2).reshape(-1), 256)
    o = _moba_merge(own.reshape(n_q, PART_W), pg.reshape(MOBA_TOPK, n_q, PART_W), norm_g)
    return o.reshape(B, H, S, hd)


def _mix_kernel(oh_ref, oa_ref, x_ref, g1_ref, sc2_ref, sh2_ref, n2_ref, wo_ref, wr_ref, br_ref,
                x1_ref, h2_ref, gw_ref, idx_ref, cnt_ref, cnt_acc):
    cat = jnp.concatenate([oh_ref[...]] + [oa_ref[hd] for hd in range(ATTN_HEADS)], axis=1)
    mix = _dot(cat.astype(BF16), wo_ref[...])
    x1 = x_ref[...] + g1_ref[...] * mix
    x1_ref[...] = x1
    ms = jnp.mean(x1 * x1, axis=-1, keepdims=True)
    h2 = x1 * lax.rsqrt(ms + RMS_EPS) * n2_ref[...]
    h2 = h2 * (1.0 + sc2_ref[...]) + sh2_ref[...]
    h2_ref[...] = _pack_bf16_pairs(h2)
    E = N_EXPERTS
    tm = h2.shape[0]
    h_0 = h2.astype(BF16)
    r_1 = h2 - h_0.astype(F32)
    h_1 = r_1.astype(BF16)
    h_2 = (r_1 - h_1.astype(F32)).astype(BF16)
    wt = wr_ref[...]
    p_0 = _dot_nt(wt, h_0)
    p_1 = _dot_nt(wt[:2 * E], h_1)
    p_2 = _dot_nt(wt[:E], h_2)
    logits = (p_0[:E] + (p_0[E:2 * E] + p_1[:E]) + (p_0[2 * E:] + p_1[E:] + p_2)) + br_ref[...]
    ex = lax.broadcasted_iota(jnp.int32, logits.shape, 0)
    neg_inf = jnp.float32(-jnp.inf)
    vals, idxs = [], []
    for _ in range(TOP_K):
        m = jnp.max(logits, axis=0, keepdims=True)
        first = jnp.min(jnp.where(logits == m, ex, E), axis=0, keepdims=True)
        vals.append(m)
        idxs.append(first)
        logits = jnp.where(ex == first, neg_inf, logits)
    e = [jnp.exp(v - vals[0]) for v in vals]
    denom = e[0] + e[1] + e[2] + e[3]
    gate_rows = jnp.concatenate([ei / denom for ei in e] + [jnp.zeros((128 - TOP_K, tm), F32)], axis=0)
    gw_ref[...] = gate_rows.T[:, :TOP_K]

    @pl.when((pl.program_id(0) == 0) & (pl.program_id(1) == 0))
    def _():
        cnt_acc[...] = jnp.zeros_like(cnt_acc)

    earlier = (lax.broadcasted_iota(jnp.int32, (tm, tm), 0) < lax.broadcasted_iota(jnp.int32, (tm, tm), 1)).astype(BF16)
    onehots = [(ex == ix).astype(F32) for ix in idxs]
    member = onehots[0] + onehots[1] + onehots[2] + onehots[3]
    base = cnt_acc[...] + _dot(member.astype(BF16), earlier)
    ranks = [jnp.sum(oh * base, axis=0, keepdims=True).astype(jnp.int32) for oh in onehots]
    idx_ref[...] = jnp.concatenate(idxs + ranks, axis=0)
    total = cnt_acc[...] + jnp.sum(member, axis=1, keepdims=True)
    cnt_acc[...] = total
    cnt_ref[...] = total.astype(jnp.int32)


def _mix(oh, oa, x, gate1, scale2, shift2, norm2_g, w_out_bf16, w_router, b_router, b0):
    _, S, D = x.shape
    B = oh.shape[0]
    hw = oh.shape[-1]
    tm = MIX_ROWS
    row = lambda b, i: (b, i, 0)
    xrow = lambda b, i: (b0 + b, i, 0)
    vec = lambda b, i: (b, 0, 0)
    const = lambda b, i: (0, 0)
    return pl.pallas_call(
        _mix_kernel,
        out_shape=(jax.ShapeDtypeStruct((B, S, D), F32),
                   jax.ShapeDtypeStruct((B, S, D // 2), jnp.int32),
                   jax.ShapeDtypeStruct((B, S, TOP_K), F32),
                   jax.ShapeDtypeStruct((B, 2 * TOP_K, S), jnp.int32),
                   jax.ShapeDtypeStruct((N_EXPERTS, 1), jnp.int32)),
        grid=(B, S // tm),
        in_specs=[pl.BlockSpec((None, tm, hw), row),
                  pl.BlockSpec((None, ATTN_HEADS, tm, ATTN_HEAD_DIM), lambda b, i: (b, 0, i, 0)),
                  pl.BlockSpec((None, tm, D), xrow),
                  pl.BlockSpec((None, 1, D), vec),
                  pl.BlockSpec((None, 1, D), vec),
                  pl.BlockSpec((None, 1, D), vec),
                  pl.BlockSpec((1, D), const),
                  pl.BlockSpec((D, D), const),
                  pl.BlockSpec((3 * N_EXPERTS, D), const),
                  pl.BlockSpec((N_EXPERTS, 1), const)],
        out_specs=(pl.BlockSpec((None, tm, D), row),
                   pl.BlockSpec((None, tm, D // 2), row),
                   pl.BlockSpec((None, tm, TOP_K), row),
                   pl.BlockSpec((None, 2 * TOP_K, tm), lambda b, i: (b, 0, i)),
                   pl.BlockSpec((N_EXPERTS, 1), const)),
        scratch_shapes=[pltpu.VMEM((N_EXPERTS, 1), F32)],
        compiler_params=pltpu.CompilerParams(
            dimension_semantics=("arbitrary", "arbitrary"), vmem_limit_bytes=VMEM_LIMIT),
        name="mix",
    )(oh, oa, x, gate1, scale2, shift2, norm2_g, w_out_bf16, w_router, b_router)


def _moe_rows_kernel(be_ref, nx_ref, nu_ref, x_ref, wgu_hbm, bgu_ref, wd_hbm, bd_ref, y_ref,
                     wgu32, wd32, wgu16, wd16, sem, *, d_ff):
    i = pl.program_id(0)
    e = be_ref[i]

    def fetch(expert):
        return (pltpu.make_async_copy(wgu_hbm.at[expert], wgu32, sem.at[0]),
                pltpu.make_async_copy(wd_hbm.at[expert], wd32, sem.at[1]))

    @pl.when(i == 0)
    def _():
        for c in fetch(e):
            c.start()

    @pl.when((i == 0) | (e != be_ref[jnp.maximum(i - 1, 0)]))
    def _():
        for c in fetch(e):
            c.wait()
        wgu16[...] = wgu32[...].astype(BF16)
        wd16[...] = wd32[...].astype(BF16)

        @pl.when(nx_ref[i] >= 0)
        def _():
            for c in fetch(nx_ref[i]):
                c.start()

    @pl.when(i < nu_ref[0])
    def _():
        gu = _dot(_unpack_bf16_pairs(x_ref[...]).astype(BF16), wgu16[...]) + bgu_ref[...]
        gate = jnp.minimum(gu[:, :d_ff], SWIGLU_LIMIT)
        up = jnp.clip(gu[:, d_ff:], -SWIGLU_LIMIT, SWIGLU_LIMIT)
        act = (up + 1.0) * gate * _sigmoid(SWIGLU_ALPHA * gate)
        y_ref[...] = _pack_bf16_pairs(_dot(act.astype(BF16), wd16[...]) + bd_ref[...])

    @pl.when(i >= nu_ref[0])
    def _():
        y_ref[...] = jnp.zeros_like(y_ref)


def _moe_rows(xs, blk_expert, n_used, wgu, bgu, wd, bd):
    D = 2 * xs.shape[1]
    bm = MOE_ROWS
    n_blk = xs.shape[0] // bm
    d_ff = wd.shape[1]
    run_end = jnp.sum((blk_expert[None, :] <= blk_expert[:, None]).astype(jnp.int32), axis=1)
    next_expert = jnp.where(run_end < n_blk, blk_expert[jnp.minimum(run_end, n_blk - 1)], -1).astype(jnp.int32)
    bsel = lambda i, be, nx, nu: (be[i], 0, 0)
    rows = lambda i, be, nx, nu: (i, 0)
    grid_spec = pltpu.PrefetchScalarGridSpec(
        num_scalar_prefetch=3,
        grid=(n_blk,),
        in_specs=[pl.BlockSpec((bm, D // 2), rows),
                  pl.BlockSpec(memory_space=pl.ANY),
                  pl.BlockSpec((None, 1, 2 * d_ff), bsel),
                  pl.BlockSpec(memory_space=pl.ANY),
                  pl.BlockSpec((None, 1, D), bsel)],
        out_specs=pl.BlockSpec((bm, D // 2), rows),
        scratch_shapes=[pltpu.VMEM((D, 2 * d_ff), F32), pltpu.VMEM((d_ff, D), F32),
                        pltpu.VMEM((D, 2 * d_ff), BF16), pltpu.VMEM((d_ff, D), BF16),
                        pltpu.SemaphoreType.DMA((2,))],
    )
    return pl.pallas_call(
        functools.partial(_moe_rows_kernel, d_ff=d_ff),
        out_shape=jax.ShapeDtypeStruct((n_blk * bm, D // 2), jnp.int32),
        grid_spec=grid_spec,
        compiler_params=pltpu.CompilerParams(
            dimension_semantics=("arbitrary",), vmem_limit_bytes=VMEM_LIMIT),
        name="moe_rows",
    )(blk_expert, next_expert, n_used, xs, wgu, bgu.reshape(N_EXPERTS, 1, 2 * d_ff), wd, bd.reshape(N_EXPERTS, 1, D))


def _combine_rows_kernel(*refs):
    y_refs = refs[:TOP_K]
    gw_ref, x1_ref, g2_ref, fg_ref = refs[TOP_K:TOP_K + 4]
    o_ref = refs[-1]
    gw = gw_ref[...]
    y = gw[:, 0:1] * _unpack_bf16_pairs(y_refs[0][...])
    for kk in range(1, TOP_K):
        y = y + gw[:, kk:kk + 1] * _unpack_bf16_pairs(y_refs[kk][...])
    x2 = x1_ref[...] + g2_ref[...] * y
    ms = jnp.mean(x2 * x2, axis=-1, keepdims=True)
    o_ref[...] = x2 * lax.rsqrt(ms + RMS_EPS) * fg_ref[...]


def _combine_rows(yg, gates, x1, gate2, final_g, out_so_far, b0, n_batches):
    S, D = x1.shape
    tm = COMBINE_ROWS
    steps = S // tm
    slot_spec = lambda kk: pl.BlockSpec((tm, D // 2), lambda i: (kk * steps + i, 0))
    in_specs = [slot_spec(kk) for kk in range(TOP_K)] + [
        pl.BlockSpec((tm, TOP_K), lambda i: (i, 0)),
        pl.BlockSpec((tm, D), lambda i: (i, 0)),
        pl.BlockSpec((1, D), lambda i: (0, 0)),
        pl.BlockSpec((1, D), lambda i: (0, 0))]
    args = [yg] * TOP_K + [gates, x1, gate2, final_g]
    aliases = {}
    if out_so_far is not None:
        in_specs.append(pl.BlockSpec(memory_space=pl.ANY))
        aliases = {len(args): 0}
        args.append(out_so_far)
    return pl.pallas_call(
        _combine_rows_kernel,
        out_shape=jax.ShapeDtypeStruct((n_batches * S, D), F32),
        grid=(steps,),
        in_specs=in_specs,
        out_specs=pl.BlockSpec((tm, D), lambda i: (b0 * steps + i, 0)),
        input_output_aliases=aliases,
        compiler_params=pltpu.CompilerParams(
            dimension_semantics=("arbitrary",), vmem_limit_bytes=VMEM_LIMIT),
        name="combine_rows",
    )(*args)


def _split_bf16x3(w):
    def top(v):
        return lax.bitcast_convert_type(lax.bitcast_convert_type(v, jnp.int32) & jnp.int32(-65536), F32)
    w0 = top(w)
    w1 = top(w - w0)
    w2 = w - w0 - w1
    return jnp.concatenate([w0, w1, w2], axis=1).astype(BF16).T


def _rotary_tables(positions):
    B, S = positions.shape
    half = ROT_DIM // 2
    inv_freq = jnp.exp(-math.log(ROPE_THETA) * jnp.arange(0, ROT_DIM, 2, dtype=F32) / ROT_DIM)
    ang = inv_freq[:, None] * positions.astype(F32).reshape(1, B * S)
    d = jnp.arange(128, dtype=jnp.int32) % ATTN_HEAD_DIM
    rotary = d < ROT_DIM
    sel = ((d[None, :] % half == jnp.arange(half, dtype=jnp.int32)[:, None]) & rotary[None, :]).astype(F32)
    sign = jnp.where(d < half, -1.0, 1.0)
    spread = lambda t, w: lax.dot_general(t, w, (((0,), (0,)), ((), ())), precision=HIGHEST)
    ct = spread(jnp.cos(ang), sel) + jnp.where(rotary, 0.0, 1.0)
    st = spread(jnp.sin(ang), sel * sign)
    return ct.reshape(B, S, 128), st.reshape(B, S, 128)


def kernel(x, c, positions, w_ada, b_ada, norm1_g, w_in, hgrn_lb_logits, hgrn_norm_g, attn_norm_g,
           w_out, norm2_g, w_router, b_router, w_gate_up, b_gate_up, w_down, b_down, final_norm_g):
    B, S, D = x.shape
    assert w_in.shape[0] == 1, "single-layer block: the final norm is fused into the combine step"
    l = 0
    ctab, stab = _rotary_tables(positions)
    lower_bounds = jnp.cumsum(jax.nn.softmax(hgrn_lb_logits.astype(F32), axis=0), axis=0)
    mod = _ada(c, w_ada[l], b_ada[l])
    shift1, scale1, gate1, shift2, scale2, gate2 = jnp.split(mod[:, None, :], N_MOD, axis=-1)
    n_blk = (S * TOP_K) // MOE_ROWS + N_EXPERTS
    experts = jnp.arange(N_EXPERTS, dtype=jnp.int32)[:, None]
    tok = jnp.broadcast_to(jnp.arange(S, dtype=jnp.int32)[None, :], (TOP_K, S)).reshape(-1)
    out = None
    for b in range(B):
        one = slice(b, b + 1)
        q, k, lf, v, gt, aq, ak, av, km = _proj(
            x, scale1[one], shift1[one], norm1_g[l][None], w_in[l].astype(BF16), lower_bounds[l][None],
            ctab, stab, b)
        o_a = _moba(aq, km, ak, av, attn_norm_g[l][None])
        o_h = _hgrn(q, k, lf, v, gt, hgrn_norm_g[l][None])
        x1, h2, gates, idx8, counts = _mix(
            o_h, o_a, x, gate1[one], scale2[one], shift2[one], norm2_g[l][None], w_out[l].astype(BF16),
            _split_bf16x3(w_router[l]), b_router[l][:, None], b)
        pad_start, blk_expert, n_used = _tile_layout(counts.reshape(-1), MOE_ROWS, n_blk)
        chosen = idx8[0, 0:TOP_K, :]
        pos = jnp.sum(jnp.where(chosen[:, None, :] == experts, pad_start[:, None], 0), axis=1) + idx8[0, TOP_K:, :]
        xs = _sc_permute(h2.reshape(S, D // 2), tok, pos.reshape(-1), n_blk * MOE_ROWS, 64)
        y_sorted = _moe_rows(xs, blk_expert, n_used, w_gate_up[l], b_gate_up[l], w_down[l], b_down[l])
        yg = _sc_gather(y_sorted, pos.reshape(-1), 64)
        out = _combine_rows(yg, gates.reshape(S, TOP_K), x1.reshape(S, D), gate2[b], final_norm_g[None], out, b, B)
    return out.reshape(B, S, D)
```

```python
import functools
import math

import jax
import jax.numpy as jnp
from jax import lax
from jax.experimental import pallas as pl
from jax.experimental.pallas import tpu as pltpu
from jax.experimental.pallas import tpu_sc as plsc

F32 = jnp.float32
BF16 = jnp.bfloat16
HIGHEST = lax.Precision.HIGHEST

HGRN_DK = 128
HGRN_CHUNK = 64
ATTN_HEADS = 4
ATTN_HEAD_DIM = 64
ROT_DIM = ATTN_HEAD_DIM // 4
ROPE_THETA = 500000.0
MOBA_BLOCK = 256
MOBA_TOPK = 3
N_EXPERTS = 32
TOP_K = 4
SWIGLU_ALPHA = 1.702
SWIGLU_LIMIT = 7.0
N_MOD = 6
RMS_EPS = 1e-6

HGRN_SUB = 16
EXP_CLAMP = 80.0
PROJ_ROWS = 512
HGRN_ROWS = 1024
MERGE_ROWS = 4096
COMBINE_ROWS = 1024
MIX_ROWS = 1024
MOE_ROWS = 512
MOBA_ROWS = 256
MOBA_TILES_PER_STEP = 16
PART_W = 128
V7X_VMEM_BYTES = 64 * 1024 * 1024
VMEM_LIMIT = V7X_VMEM_BYTES * 7 // 8
SC_CORES = 2
SC_SUBCORES = 16


def _sigmoid(x):
    return 1.0 / (1.0 + jnp.exp(-x))


def _dot(a, b, **kw):
    return jnp.dot(a, b, preferred_element_type=F32, **kw)


def _dot_nt(a, b, **kw):
    return lax.dot_general(a, b, (((1,), (1,)), ((), ())), preferred_element_type=F32, **kw)


def _pack_bf16_pairs(x):
    w = x.shape[1] // 2
    bits = lax.bitcast_convert_type(x.astype(BF16).astype(F32), jnp.int32)
    return bits[:, w:] | lax.shift_right_logical(bits[:, :w], 16)


def _unpack_bf16_pairs(p):
    lo = lax.bitcast_convert_type(lax.shift_left(p, 16), F32)
    hi = lax.bitcast_convert_type(p & jnp.int32(-65536), F32)
    return jnp.concatenate([lo, hi], axis=1)


def _ada_kernel(c_ref, w_ref, b_ref, o_ref):
    c = c_ref[...]
    o_ref[...] = _dot(c * _sigmoid(c), w_ref[...], precision=HIGHEST) + b_ref[...]


def _ada(c, w_ada, b_ada):
    B, D = c.shape
    N = w_ada.shape[1]
    tn = N // 4
    c8 = jnp.zeros((8, D), F32).at[:B].set(c)
    out = pl.pallas_call(
        _ada_kernel,
        out_shape=jax.ShapeDtypeStruct((8, N), F32),
        grid=(N // tn,),
        in_specs=[pl.BlockSpec((8, D), lambda j: (0, 0)),
                  pl.BlockSpec((D, tn), lambda j: (0, j)),
                  pl.BlockSpec((1, tn), lambda j: (0, j))],
        out_specs=pl.BlockSpec((8, tn), lambda j: (0, j)),
        compiler_params=pltpu.CompilerParams(vmem_limit_bytes=VMEM_LIMIT),
        name="ada",
    )(c8, w_ada, b_ada.reshape(1, N))
    return out[:B]


def _proj_kernel(x_ref, sc_ref, sh_ref, g_ref, w_ref, lb_ref, ct_ref, st_ref,
                 q_ref, k_ref, lf_ref, v_ref, gt_ref, aq_ref, ak_ref, av_ref, km_ref,
                 *, hw, aw):
    x = x_ref[...]
    ms = jnp.mean(x * x, axis=-1, keepdims=True)
    h = x * lax.rsqrt(ms + RMS_EPS) * g_ref[...]
    h = h * (1.0 + sc_ref[...]) + sh_ref[...]
    proj = _dot(h.astype(BF16), w_ref[...])

    hq = proj[:, 0:hw]
    hf = proj[:, hw:2 * hw]
    hg = proj[:, 3 * hw:4 * hw]
    q_ref[...] = (hq * _sigmoid(hq) * (HGRN_DK ** -0.5)).astype(BF16)
    lb = lb_ref[...]
    f = lb + (1.0 - lb) * _sigmoid(hf)
    k_ref[...] = (1.0 - f).astype(BF16)
    lf_ref[...] = jnp.log(f)
    v_ref[...] = proj[:, 2 * hw:3 * hw].astype(BF16)
    gt_ref[...] = (hg * _sigmoid(hg)).astype(BF16)

    ct = jnp.concatenate([ct_ref[...]] * (aw // 128), axis=1)
    st = jnp.concatenate([st_ref[...]] * (aw // 128), axis=1)
    lane = lax.broadcasted_iota(jnp.int32, ct.shape, 1) % ATTN_HEAD_DIM
    first_half = lane < (ROT_DIM // 2)

    def rot(t):
        partner = jnp.where(first_half, pltpu.roll(t, aw - ROT_DIM // 2, 1), pltpu.roll(t, ROT_DIM // 2, 1))
        return t * ct + partner * st

    base = 4 * hw
    aq = rot(proj[:, base:base + aw])
    ak = rot(proj[:, base + aw:base + 2 * aw])
    av = proj[:, base + 2 * aw:base + 3 * aw]
    for blk in range(ak.shape[0] // MOBA_BLOCK):
        km_ref[blk] = jnp.mean(ak[blk * MOBA_BLOCK:(blk + 1) * MOBA_BLOCK], axis=0, keepdims=True)
    lane128 = lax.broadcasted_iota(jnp.int32, (x.shape[0], 128), 1)
    for pair in range(ATTN_HEADS // 2):
        aq_ref[pair] = aq[:, pair * 128:(pair + 1) * 128]
    for hd in range(ATTN_HEADS):
        pair, half = divmod(hd, 2)
        in_head = (lane128 // ATTN_HEAD_DIM) == half
        ak_ref[hd] = jnp.where(in_head, ak[:, pair * 128:(pair + 1) * 128], 0.0).astype(BF16)
        av_ref[hd] = av[:, hd * ATTN_HEAD_DIM:(hd + 1) * ATTN_HEAD_DIM].astype(BF16)


def _proj(x, scale1, shift1, norm_g, w_in_bf16, lb, ctab, stab, b0):
    _, S, D = x.shape
    B = scale1.shape[0]
    hw = lb.shape[-1]
    aw = ATTN_HEADS * ATTN_HEAD_DIM
    tm = PROJ_ROWS
    nb = S // MOBA_BLOCK
    n_proj = w_in_bf16.shape[1]
    row = lambda b, i: (b, i, 0)
    xrow = lambda b, i: (b0 + b, i, 0)
    vec = lambda b, i: (b, 0, 0)
    head = lambda b, i: (b, 0, i, 0)
    out_shapes = (
        jax.ShapeDtypeStruct((B, S, hw), BF16),
        jax.ShapeDtypeStruct((B, S, hw), BF16),
        jax.ShapeDtypeStruct((B, S, hw), F32),
        jax.ShapeDtypeStruct((B, S, hw), BF16),
        jax.ShapeDtypeStruct((B, S, hw), BF16),
        jax.ShapeDtypeStruct((B, ATTN_HEADS // 2, S, 128), F32),
        jax.ShapeDtypeStruct((B, ATTN_HEADS, S, 128), BF16),
        jax.ShapeDtypeStruct((B, ATTN_HEADS, S, ATTN_HEAD_DIM), BF16),
        jax.ShapeDtypeStruct((B, nb, 1, aw), F32),
    )
    hspec = pl.BlockSpec((None, tm, hw), row)
    aspec = pl.BlockSpec((None, ATTN_HEADS, tm, ATTN_HEAD_DIM), head)
    return pl.pallas_call(
        functools.partial(_proj_kernel, hw=hw, aw=aw),
        out_shape=out_shapes,
        grid=(B, S // tm),
        in_specs=[pl.BlockSpec((None, tm, D), xrow),
                  pl.BlockSpec((None, 1, D), vec),
                  pl.BlockSpec((None, 1, D), vec),
                  pl.BlockSpec((1, D), lambda b, i: (0, 0)),
                  pl.BlockSpec((D, n_proj), lambda b, i: (0, 0)),
                  pl.BlockSpec((1, hw), lambda b, i: (0, 0)),
                  pl.BlockSpec((None, tm, 128), xrow),
                  pl.BlockSpec((None, tm, 128), xrow)],
        out_specs=(hspec, hspec, hspec, hspec, hspec,
                   pl.BlockSpec((None, ATTN_HEADS // 2, tm, 128), head),
                   pl.BlockSpec((None, ATTN_HEADS, tm, 128), head), aspec,
                   pl.BlockSpec((None, tm // MOBA_BLOCK, 1, aw), lambda b, i: (b, i, 0, 0))),
        compiler_params=pltpu.CompilerParams(
            dimension_semantics=("arbitrary", "arbitrary"), vmem_limit_bytes=VMEM_LIMIT),
        name="proj",
    )(x, scale1, shift1, norm_g, w_in_bf16, lb, ctab, stab)


def _hgrn_kernel(q_ref, k_ref, lf_ref, v_ref, gt_ref, gn_ref, o_ref, st_ref, *, n_heads, n_chunks):
    @pl.when(pl.program_id(1) == 0)
    def _():
        st_ref[...] = jnp.zeros_like(st_ref)

    C = HGRN_CHUNK
    r = lax.broadcasted_iota(jnp.int32, (C, C), 0)
    c = lax.broadcasted_iota(jnp.int32, (C, C), 1)
    tril = c <= r
    ltri = tril.astype(F32)
    gn = gn_ref[...]

    def chunk(ci, carry):
        r0 = pl.multiple_of(ci * C, C)
        rows = pl.ds(r0, C)
        b_all = _dot(ltri, lf_ref[rows, :], precision=HIGHEST)
        heads = range(n_heads)
        sls = [slice(hd * HGRN_DK, (hd + 1) * HGRN_DK) for hd in heads]
        bs = [b_all[:, sl] for sl in sls]
        b_lasts = [b[C - 1:C, :] for b in bs]
        qs = [q_ref[rows, sl].astype(F32) for sl in sls]
        ks = [k_ref[rows, sl].astype(F32) for sl in sls]
        vs = [v_ref[rows, sl] for sl in sls]
        states = [st_ref[hd] for hd in heads]
        o_inter = [_dot_nt((qs[hd] * jnp.exp(bs[hd])).astype(BF16), states[hd].astype(BF16)) for hd in heads]
        scores = []
        for hd in heads:
            blocks = []
            for g0 in range(0, C, HGRN_SUB):
                g1 = g0 + HGRN_SUB
                rho = 0.5 * (bs[hd][g0:g0 + 1, :] + bs[hd][g1 - 1:g1, :])
                qa = qs[hd][g0:g1, :] * jnp.exp(jnp.minimum(bs[hd][g0:g1, :] - rho, EXP_CLAMP))
                kb = ks[hd] * jnp.exp(jnp.minimum(rho - bs[hd], EXP_CLAMP))
                blocks.append(_dot_nt(qa.astype(BF16), kb.astype(BF16)))
            scores.append(jnp.where(tril, jnp.concatenate(blocks, axis=0), 0.0).astype(BF16))
        outs = [o_inter[hd] + _dot(scores[hd], vs[hd]) for hd in heads]
        kds = [(ks[hd] * jnp.exp(b_lasts[hd] - bs[hd])).astype(BF16) for hd in heads]
        upds = [_dot(vs[hd].astype(F32).T.astype(BF16), kds[hd]) for hd in heads]
        for hd in heads:
            st_ref[hd] = states[hd] * jnp.exp(b_lasts[hd]) + upds[hd]
            o = outs[hd]
            ms = jnp.mean(o * o, axis=-1, keepdims=True)
            o_ref[rows, sls[hd]] = (o * lax.rsqrt(ms + RMS_EPS) * gn * gt_ref[rows, sls[hd]].astype(F32)).astype(BF16)
        return carry

    lax.fori_loop(0, n_chunks, chunk, 0, unroll=True)


def _hgrn(q, k, lf, v, gt, norm_g):
    B, S, hw = q.shape
    n_heads = hw // HGRN_DK
    tc = HGRN_ROWS
    spec = pl.BlockSpec((None, tc, hw), lambda b, i: (b, i, 0))
    return pl.pallas_call(
        functools.partial(_hgrn_kernel, n_heads=n_heads, n_chunks=tc // HGRN_CHUNK),
        out_shape=jax.ShapeDtypeStruct((B, S, hw), BF16),
        grid=(B, S // tc),
        in_specs=[spec, spec, spec, spec, spec, pl.BlockSpec((1, HGRN_DK), lambda b, i: (0, 0))],
        out_specs=spec,
        scratch_shapes=[pltpu.VMEM((n_heads, HGRN_DK, HGRN_DK), F32)],
        compiler_params=pltpu.CompilerParams(
            dimension_semantics=("arbitrary", "arbitrary"), vmem_limit_bytes=VMEM_LIMIT),
        name="hgrn",
    )(q, k, lf, v, gt, norm_g)


def _sc_move_rows(table, src, dst, n_out, chunk):
    M = src.shape[0]
    D = table.shape[1]
    n_workers = SC_CORES * SC_SUBCORES
    per_worker = M // n_workers
    n_chunks = per_worker // chunk
    assert per_worker * n_workers == M and n_chunks * chunk == per_worker and n_chunks % 2 == 0 and chunk % 8 == 0
    mesh = plsc.VectorSubcoreMesh(core_axis_name="c", subcore_axis_name="s")
    idx_t = pltpu.VMEM((chunk,), jnp.int32)
    row_t = pltpu.VMEM((chunk, D), table.dtype)
    sem_t = pltpu.SemaphoreType.DMA

    def body(table_hbm, src_hbm, dst_hbm, out_hbm, src_v, dst_v, rows_v, g_sem, s_sem):
        wid = lax.axis_index("s") * SC_CORES + lax.axis_index("c")
        base = wid * per_worker

        def offset(j):
            return pl.multiple_of(base + j * chunk, 8)

        def gather(b):
            return pltpu.make_async_copy(table_hbm.at[src_v[b]], rows_v[b], g_sem[b])

        def start_gather(j, b):
            pltpu.sync_copy(src_hbm.at[pl.ds(offset(j), chunk)], src_v[b])
            gather(b).start()

        def write_out(j, b):
            if dst_hbm is None:
                pltpu.sync_copy(rows_v[b], out_hbm.at[pl.ds(offset(j), chunk)])
            else:
                pltpu.sync_copy(dst_hbm.at[pl.ds(offset(j), chunk)], dst_v[b])
                pltpu.async_copy(rows_v[b], out_hbm.at[dst_v[b]], s_sem[b]).wait()

        start_gather(0, 0)

        @pl.loop(0, n_chunks, step=2)
        def _(j):
            for b in (0, 1):
                @pl.when(j + b + 1 < n_chunks)
                def _():
                    start_gather(j + b + 1, 1 - b)
                gather(b).wait()
                write_out(j + b, b)

    if dst is None:
        @functools.partial(pl.kernel, mesh=mesh, out_type=jax.ShapeDtypeStruct((n_out, D), table.dtype),
                           scratch_types=[idx_t, idx_t, row_t, row_t, sem_t, sem_t])
        def gather_kernel(table_hbm, src_hbm, out_hbm, s0, s1, r0, r1, g0, g1):
            body(table_hbm, src_hbm, None, out_hbm, (s0, s1), None, (r0, r1), (g0, g1), None)
        return gather_kernel(table, src)

    @functools.partial(pl.kernel, mesh=mesh, out_type=jax.ShapeDtypeStruct((n_out, D), table.dtype),
                       scratch_types=[idx_t, idx_t, idx_t, idx_t, row_t, row_t, sem_t, sem_t, sem_t, sem_t])
    def permute_kernel(table_hbm, src_hbm, dst_hbm, out_hbm, s0, s1, d0, d1, r0, r1, g0, g1, w0, w1):
        body(table_hbm, src_hbm, dst_hbm, out_hbm, (s0, s1), (d0, d1), (r0, r1), (g0, g1), (w0, w1))
    return permute_kernel(table, src, dst)


def _sc_gather(table, idx, chunk):
    return _sc_move_rows(table, idx, None, idx.shape[0], chunk)


def _sc_permute(table, src, dst, n_out, chunk):
    return _sc_move_rows(table, src, dst, n_out, chunk)


def _tile_layout(counts, bm, n_tiles):
    n_groups = counts.shape[0]
    padded = (counts + bm - 1) // bm * bm
    pad_end = jnp.cumsum(padded)
    tile_start = jnp.arange(n_tiles, dtype=jnp.int32) * bm
    tile_group = jnp.minimum(
        jnp.sum((pad_end[None, :] <= tile_start[:, None]).astype(jnp.int32), axis=1), n_groups - 1)
    n_used = (pad_end[-1] // bm).astype(jnp.int32).reshape(1)
    return pad_end - padded, tile_group.astype(jnp.int32), n_used


def _null_partial(rows):
    lane = lax.broadcasted_iota(jnp.int32, (rows, PART_W), 1)
    return jnp.where(lane < ATTN_HEAD_DIM, 0.0, -jnp.inf).astype(F32)


def _moba_sel_kernel(q_ref, km_ref, k_ref, v_ref, idx_ref, cnt_ref, own_ref, cnt_acc, *, n_blocks):
    j = pl.program_id(1)
    T = MOBA_BLOCK
    heads = range(ATTN_HEADS)
    qs = [q_ref[hd // 2] for hd in heads]
    gates = [_dot_nt(km_ref[hd], qs[hd], precision=HIGHEST) for hd in heads]
    blk = lax.broadcasted_iota(jnp.int32, gates[0].shape, 0)
    neg_inf = jnp.float32(-jnp.inf)
    gates = [jnp.where(blk < j, g, neg_inf) for g in gates]
    picks = [[] for _ in heads]
    for _ in range(MOBA_TOPK):
        ms = [jnp.max(g, axis=0, keepdims=True) for g in gates]
        firsts = [jnp.min(jnp.where(g == m, blk, n_blocks), axis=0, keepdims=True) for g, m in zip(gates, ms)]
        for hd in heads:
            picks[hd].append(jnp.where(ms[hd] > neg_inf, firsts[hd], -1))
        gates = [jnp.where(blk == f, neg_inf, g) for g, f in zip(gates, firsts)]

    @pl.when(j == 0)
    def _():
        cnt_acc[...] = jnp.zeros_like(cnt_acc)

    earlier = (lax.broadcasted_iota(jnp.int32, (T, T), 0) < lax.broadcasted_iota(jnp.int32, (T, T), 1)).astype(BF16)
    for hd in heads:
        onehots = [(blk == p).astype(F32) for p in picks[hd]]
        member = onehots[0] + onehots[1] + onehots[2]
        base = cnt_acc[hd] + _dot(member.astype(BF16), earlier)
        ranks = [jnp.sum(oh * base, axis=0, keepdims=True).astype(jnp.int32) for oh in onehots]
        idx_ref[hd] = jnp.concatenate(picks[hd] + ranks + [jnp.zeros((2, T), jnp.int32)], axis=0)
        total = cnt_acc[hd] + jnp.sum(member, axis=1, keepdims=True)
        cnt_acc[hd] = total
        cnt_ref[hd] = total.astype(jnp.int32)
    causal = lax.broadcasted_iota(jnp.int32, (T, T), 1) <= lax.broadcasted_iota(jnp.int32, (T, T), 0)
    scale = ATTN_HEAD_DIM ** -0.5
    ss = [jnp.where(causal, _dot_nt((qs[hd] * scale).astype(BF16), k_ref[hd]), neg_inf) for hd in heads]
    mx = [jnp.max(s, axis=1, keepdims=True) for s in ss]
    ps = [jnp.exp(s - m) for s, m in zip(ss, mx)]
    ls = [jnp.sum(p, axis=1, keepdims=True) for p in ps]
    accs = [_dot(ps[hd].astype(BF16), v_ref[hd]) for hd in heads]
    for hd in heads:
        lse = jnp.broadcast_to(mx[hd] + jnp.log(ls[hd]), (T, PART_W - ATTN_HEAD_DIM))
        own_ref[hd] = jnp.concatenate([accs[hd] / ls[hd], lse], axis=1)


def _moba_sel(aq, kmean, ak, av):
    B, H, S, hd = av.shape
    nb = S // MOBA_BLOCK
    T = MOBA_BLOCK
    blk = lambda b, j: (b, 0, j, 0)
    return pl.pallas_call(
        functools.partial(_moba_sel_kernel, n_blocks=nb),
        out_shape=(jax.ShapeDtypeStruct((B, H, 8, S), jnp.int32),
                   jax.ShapeDtypeStruct((B, H, nb, 1), jnp.int32),
                   jax.ShapeDtypeStruct((B, H, S, PART_W), F32)),
        grid=(B, nb),
        in_specs=[pl.BlockSpec((None, H // 2, T, 128), blk),
                  pl.BlockSpec((None, H, nb, 128), lambda b, j: (b, 0, 0, 0)),
                  pl.BlockSpec((None, H, T, 128), blk),
                  pl.BlockSpec((None, H, T, hd), blk)],
        out_specs=(pl.BlockSpec((None, H, 8, T), lambda b, j: (b, 0, 0, j)),
                   pl.BlockSpec((None, H, nb, 1), lambda b, j: (b, 0, 0, 0)),
                   pl.BlockSpec((None, H, T, PART_W), blk)),
        scratch_shapes=[pltpu.VMEM((H, nb, 1), F32)],
        compiler_params=pltpu.CompilerParams(
            dimension_semantics=("arbitrary", "arbitrary"), vmem_limit_bytes=VMEM_LIMIT),
        name="moba_sel",
    )(aq, kmean, ak, av)


def _moba_blk_kernel(tg_ref, nu_ref, q_ref, k_ref, v_ref, o_ref, *, n_blocks):
    n = MOBA_TILES_PER_STEP
    R = MOBA_ROWS
    t0 = pl.program_id(0) * n

    @pl.when(t0 < nu_ref[0])
    def _():
        scale = ATTN_HEAD_DIM ** -0.5
        groups = [tg_ref[t0 + j] for j in range(n)]
        kv_rows = [(g // n_blocks, pl.ds(pl.multiple_of((g % n_blocks) * MOBA_BLOCK, MOBA_BLOCK), MOBA_BLOCK))
                   for g in groups]
        ss = [_dot_nt((q_ref[j * R:(j + 1) * R, :] * scale).astype(BF16), k_ref[kv_rows[j][0], kv_rows[j][1], :])
              for j in range(n)]
        ms = [jnp.max(s, axis=1, keepdims=True) for s in ss]
        ps = [jnp.exp(s - m) for s, m in zip(ss, ms)]
        ls = [jnp.sum(p, axis=1, keepdims=True) for p in ps]
        accs = [_dot(p.astype(BF16), v_ref[kv_rows[j][0], kv_rows[j][1], :]) for j, p in enumerate(ps)]
        null = _null_partial(R)
        for j in range(n):
            lse = jnp.broadcast_to(ms[j] + jnp.log(ls[j]), (R, PART_W - ATTN_HEAD_DIM))
            row = jnp.concatenate([accs[j] / ls[j], lse], axis=1)
            o_ref[j * R:(j + 1) * R, :] = jnp.where(t0 + j < nu_ref[0], row, null)

    @pl.when(t0 >= nu_ref[0])
    def _():
        o_ref[...] = _null_partial(n * R)


def _moba_blk(qs, tile_group, n_used, ak, av):
    B, H, S, hd = av.shape
    nb = S // MOBA_BLOCK
    R = MOBA_ROWS
    n = MOBA_TILES_PER_STEP
    n_tiles = qs.shape[0] // R
    assert n_tiles % n == 0
    whole = lambda i, tg, nu: (0, 0, 0)
    grid_spec = pltpu.PrefetchScalarGridSpec(
        num_scalar_prefetch=2,
        grid=(n_tiles // n,),
        in_specs=[pl.BlockSpec((n * R, 128), lambda i, tg, nu: (i, 0)),
                  pl.BlockSpec((B * H, S, 128), whole, pipeline_mode=pl.Buffered(1)),
                  pl.BlockSpec((B * H, S, hd), whole, pipeline_mode=pl.Buffered(1))],
        out_specs=pl.BlockSpec((n * R, PART_W), lambda i, tg, nu: (i, 0)),
    )
    return pl.pallas_call(
        functools.partial(_moba_blk_kernel, n_blocks=nb),
        out_shape=jax.ShapeDtypeStruct((n_tiles * R, PART_W), F32),
        grid_spec=grid_spec,
        compiler_params=pltpu.CompilerParams(
            dimension_semantics=("arbitrary",), vmem_limit_bytes=VMEM_LIMIT),
        name="moba_blk",
    )(tile_group, n_used, qs, ak.reshape(B * H, S, 128), av.reshape(B * H, S, hd))


def _moba_merge_kernel(own_ref, pg_ref, g_ref, o_ref):
    hd = ATTN_HEAD_DIM
    rows = [own_ref[...]] + [pg_ref[s] for s in range(MOBA_TOPK)]
    lses = [pltpu.roll(r, hd, 1) for r in rows]
    top = lses[0]
    for z in lses[1:]:
        top = jnp.maximum(top, z)
    num = jnp.zeros_like(top)
    den = jnp.zeros_like(top)
    for r, z in zip(rows, lses):
        w = jnp.exp(z - top)
        num = num + w * r
        den = den + w
    o = (num / den)[:, :hd]
    ms = jnp.mean(o * o, axis=-1, keepdims=True)
    o_ref[...] = o * lax.rsqrt(ms + RMS_EPS) * g_ref[...]


def _moba_merge(own, pg, norm_g):
    n = own.shape[0]
    T = MERGE_ROWS
    row = lambda i: (i, 0)
    return pl.pallas_call(
        _moba_merge_kernel,
        out_shape=jax.ShapeDtypeStruct((n, ATTN_HEAD_DIM), F32),
        grid=(n // T,),
        in_specs=[pl.BlockSpec((T, PART_W), row),
                  pl.BlockSpec((MOBA_TOPK, T, PART_W), lambda i: (0, i, 0)),
                  pl.BlockSpec((1, ATTN_HEAD_DIM), lambda i: (0, 0))],
        out_specs=pl.BlockSpec((T, ATTN_HEAD_DIM), row),
        compiler_params=pltpu.CompilerParams(
            dimension_semantics=("arbitrary",), vmem_limit_bytes=VMEM_LIMIT),
        name="moba_merge",
    )(own, pg, norm_g)


def _moba(aq, km, ak, av, norm_g):
    B, H, S, hd = av.shape
    nb = S // MOBA_BLOCK
    n_q = B * H * S
    kmp = km.reshape(B, nb, H // 2, 128)
    half = jnp.arange(128, dtype=jnp.int32) // hd
    kmean = jnp.stack([jnp.where(half == h % 2, kmp[:, :, h // 2, :], 0.0) for h in range(H)], axis=1)
    idx8, counts, own = _moba_sel(aq, kmean, ak, av)
    sel = idx8[:, :, 0:MOBA_TOPK, :].reshape(B * H, MOBA_TOPK, S)
    rank = idx8[:, :, MOBA_TOPK:2 * MOBA_TOPK, :].reshape(B * H, MOBA_TOPK, S)
    n_groups = B * H * nb
    n_tiles = (n_q * MOBA_TOPK) // MOBA_ROWS + n_groups
    pad_start, tile_group, n_used = _tile_layout(counts.reshape(-1), MOBA_ROWS, n_tiles)
    blocks = jnp.arange(nb, dtype=jnp.int32)[:, None]
    start = jnp.sum(jnp.where(sel[:, :, None, :] == blocks, pad_start.reshape(B * H, 1, nb, 1), 0), axis=2)
    a_ids = jnp.arange(n_q * MOBA_TOPK, dtype=jnp.int32).reshape(B * H, MOBA_TOPK, S)
    assert n_tiles * MOBA_ROWS >= n_q * MOBA_TOPK + MOBA_ROWS
    pos = jnp.where(sel >= 0, start + rank, n_used[0] * MOBA_ROWS + a_ids % MOBA_ROWS)
    bh = jnp.arange(B * H, dtype=jnp.int32)[:, None, None]
    t = jnp.arange(S, dtype=jnp.int32)[None, None, :]
    pair_row = jnp.broadcast_to((bh // H * (H // 2) + bh % H // 2) * S + t, pos.shape)
    qs = _sc_permute(aq.reshape(B * (H // 2) * S, 128), pair_row.reshape(-1), pos.reshape(-1),
                     n_tiles * MOBA_ROWS, 256)
    parts = _moba_blk(qs, tile_group, n_used, ak, av)
    pg = _sc_gather(parts, pos.transpose(1, 0, 2).reshape(-1), 256)
    o = _moba_merge(own.reshape(n_q, PART_W), pg.reshape(MOBA_TOPK, n_q, PART_W), norm_g)
    return o.reshape(B, H, S, hd)


def _mix_kernel(oh_ref, oa_ref, x_ref, g1_ref, sc2_ref, sh2_ref, n2_ref, wo_ref, wr_ref, br_ref,
                x1_ref, h2_ref, gw_ref, idx_ref, cnt_ref, cnt_acc):
    cat = jnp.concatenate([oh_ref[...]] + [oa_ref[hd] for hd in range(ATTN_HEADS)], axis=1)
    mix = _dot(cat.astype(BF16), wo_ref[...])
    x1 = x_ref[...] + g1_ref[...] * mix
    x1_ref[...] = x1
    ms = jnp.mean(x1 * x1, axis=-1, keepdims=True)
    h2 = x1 * lax.rsqrt(ms + RMS_EPS) * n2_ref[...]
    h2 = h2 * (1.0 + sc2_ref[...]) + sh2_ref[...]
    h2_ref[...] = _pack_bf16_pairs(h2)
    E = N_EXPERTS
    tm = h2.shape[0]
    h_0 = h2.astype(BF16)
    r_1 = h2 - h_0.astype(F32)
    h_1 = r_1.astype(BF16)
    h_2 = (r_1 - h_1.astype(F32)).astype(BF16)
    wt = wr_ref[...]
    p_0 = _dot_nt(wt, h_0)
    p_1 = _dot_nt(wt[:2 * E], h_1)
    p_2 = _dot_nt(wt[:E], h_2)
    logits = (p_0[:E] + (p_0[E:2 * E] + p_1[:E]) + (p_0[2 * E:] + p_1[E:] + p_2)) + br_ref[...]
    ex = lax.broadcasted_iota(jnp.int32, logits.shape, 0)
    neg_inf = jnp.float32(-jnp.inf)
    vals, idxs = [], []
    for _ in range(TOP_K):
        m = jnp.max(logits, axis=0, keepdims=True)
        first = jnp.min(jnp.where(logits == m, ex, E), axis=0, keepdims=True)
        vals.append(m)
        idxs.append(first)
        logits = jnp.where(ex == first, neg_inf, logits)
    e = [jnp.exp(v - vals[0]) for v in vals]
    denom = e[0] + e[1] + e[2] + e[3]
    gate_rows = jnp.concatenate([ei / denom for ei in e] + [jnp.zeros((128 - TOP_K, tm), F32)], axis=0)
    gw_ref[...] = gate_rows.T[:, :TOP_K]

    @pl.when((pl.program_id(0) == 0) & (pl.program_id(1) == 0))
    def _():
        cnt_acc[...] = jnp.zeros_like(cnt_acc)

    earlier = (lax.broadcasted_iota(jnp.int32, (tm, tm), 0) < lax.broadcasted_iota(jnp.int32, (tm, tm), 1)).astype(BF16)
    onehots = [(ex == ix).astype(F32) for ix in idxs]
    member = onehots[0] + onehots[1] + onehots[2] + onehots[3]
    base = cnt_acc[...] + _dot(member.astype(BF16), earlier)
    ranks = [jnp.sum(oh * base, axis=0, keepdims=True).astype(jnp.int32) for oh in onehots]
    idx_ref[...] = jnp.concatenate(idxs + ranks, axis=0)
    total = cnt_acc[...] + jnp.sum(member, axis=1, keepdims=True)
    cnt_acc[...] = total
    cnt_ref[...] = total.astype(jnp.int32)


def _mix(oh, oa, x, gate1, scale2, shift2, norm2_g, w_out_bf16, w_router, b_router, b0):
    _, S, D = x.shape
    B = oh.shape[0]
    hw = oh.shape[-1]
    tm = MIX_ROWS
    row = lambda b, i: (b, i, 0)
    xrow = lambda b, i: (b0 + b, i, 0)
    vec = lambda b, i: (b, 0, 0)
    const = lambda b, i: (0, 0)
    return pl.pallas_call(
        _mix_kernel,
        out_shape=(jax.ShapeDtypeStruct((B, S, D), F32),
                   jax.ShapeDtypeStruct((B, S, D // 2), jnp.int32),
                   jax.ShapeDtypeStruct((B, S, TOP_K), F32),
                   jax.ShapeDtypeStruct((B, 2 * TOP_K, S), jnp.int32),
                   jax.ShapeDtypeStruct((N_EXPERTS, 1), jnp.int32)),
        grid=(B, S // tm),
        in_specs=[pl.BlockSpec((None, tm, hw), row),
                  pl.BlockSpec((None, ATTN_HEADS, tm, ATTN_HEAD_DIM), lambda b, i: (b, 0, i, 0)),
                  pl.BlockSpec((None, tm, D), xrow),
                  pl.BlockSpec((None, 1, D), vec),
                  pl.BlockSpec((None, 1, D), vec),
                  pl.BlockSpec((None, 1, D), vec),
                  pl.BlockSpec((1, D), const),
                  pl.BlockSpec((D, D), const),
                  pl.BlockSpec((3 * N_EXPERTS, D), const),
                  pl.BlockSpec((N_EXPERTS, 1), const)],
        out_specs=(pl.BlockSpec((None, tm, D), row),
                   pl.BlockSpec((None, tm, D // 2), row),
                   pl.BlockSpec((None, tm, TOP_K), row),
                   pl.BlockSpec((None, 2 * TOP_K, tm), lambda b, i: (b, 0, i)),
                   pl.BlockSpec((N_EXPERTS, 1), const)),
        scratch_shapes=[pltpu.VMEM((N_EXPERTS, 1), F32)],
        compiler_params=pltpu.CompilerParams(
            dimension_semantics=("arbitrary", "arbitrary"), vmem_limit_bytes=VMEM_LIMIT),
        name="mix",
    )(oh, oa, x, gate1, scale2, shift2, norm2_g, w_out_bf16, w_router, b_router)


def _moe_rows_kernel(be_ref, nx_ref, nu_ref, x_ref, wgu_hbm, bgu_ref, wd_hbm, bd_ref, y_ref,
                     wgu32, wd32, wgu16, wd16, sem, *, d_ff):
    i = pl.program_id(0)
    e = be_ref[i]

    def fetch(expert):
        return (pltpu.make_async_copy(wgu_hbm.at[expert], wgu32, sem.at[0]),
                pltpu.make_async_copy(wd_hbm.at[expert], wd32, sem.at[1]))

    @pl.when(i == 0)
    def _():
        for c in fetch(e):
            c.start()

    @pl.when((i == 0) | (e != be_ref[jnp.maximum(i - 1, 0)]))
    def _():
        for c in fetch(e):
            c.wait()
        wgu16[...] = wgu32[...].astype(BF16)
        wd16[...] = wd32[...].astype(BF16)

        @pl.when(nx_ref[i] >= 0)
        def _():
            for c in fetch(nx_ref[i]):
                c.start()

    @pl.when(i < nu_ref[0])
    def _():
        gu = _dot(_unpack_bf16_pairs(x_ref[...]).astype(BF16), wgu16[...]) + bgu_ref[...]
        gate = jnp.minimum(gu[:, :d_ff], SWIGLU_LIMIT)
        up = jnp.clip(gu[:, d_ff:], -SWIGLU_LIMIT, SWIGLU_LIMIT)
        act = (up + 1.0) * gate * _sigmoid(SWIGLU_ALPHA * gate)
        y_ref[...] = _pack_bf16_pairs(_dot(act.astype(BF16), wd16[...]) + bd_ref[...])

    @pl.when(i >= nu_ref[0])
    def _():
        y_ref[...] = jnp.zeros_like(y_ref)


def _moe_rows(xs, blk_expert, n_used, wgu, bgu, wd, bd):
    D = 2 * xs.shape[1]
    bm = MOE_ROWS
    n_blk = xs.shape[0] // bm
    d_ff = wd.shape[1]
    run_end = jnp.sum((blk_expert[None, :] <= blk_expert[:, None]).astype(jnp.int32), axis=1)
    next_expert = jnp.where(run_end < n_blk, blk_expert[jnp.minimum(run_end, n_blk - 1)], -1).astype(jnp.int32)
    bsel = lambda i, be, nx, nu: (be[i], 0, 0)
    rows = lambda i, be, nx, nu: (i, 0)
    grid_spec = pltpu.PrefetchScalarGridSpec(
        num_scalar_prefetch=3,
        grid=(n_blk,),
        in_specs=[pl.BlockSpec((bm, D // 2), rows),
                  pl.BlockSpec(memory_space=pl.ANY),
                  pl.BlockSpec((None, 1, 2 * d_ff), bsel),
                  pl.BlockSpec(memory_space=pl.ANY),
                  pl.BlockSpec((None, 1, D), bsel)],
        out_specs=pl.BlockSpec((bm, D // 2), rows),
        scratch_shapes=[pltpu.VMEM((D, 2 * d_ff), F32), pltpu.VMEM((d_ff, D), F32),
                        pltpu.VMEM((D, 2 * d_ff), BF16), pltpu.VMEM((d_ff, D), BF16),
                        pltpu.SemaphoreType.DMA((2,))],
    )
    return pl.pallas_call(
        functools.partial(_moe_rows_kernel, d_ff=d_ff),
        out_shape=jax.ShapeDtypeStruct((n_blk * bm, D // 2), jnp.int32),
        grid_spec=grid_spec,
        compiler_params=pltpu.CompilerParams(
            dimension_semantics=("arbitrary",), vmem_limit_bytes=VMEM_LIMIT),
        name="moe_rows",
    )(blk_expert, next_expert, n_used, xs, wgu, bgu.reshape(N_EXPERTS, 1, 2 * d_ff), wd, bd.reshape(N_EXPERTS, 1, D))


def _combine_rows_kernel(*refs):
    y_refs = refs[:TOP_K]
    gw_ref, x1_ref, g2_ref, fg_ref = refs[TOP_K:TOP_K + 4]
    o_ref = refs[-1]
    gw = gw_ref[...]
    y = gw[:, 0:1] * _unpack_bf16_pairs(y_refs[0][...])
    for kk in range(1, TOP_K):
        y = y + gw[:, kk:kk + 1] * _unpack_bf16_pairs(y_refs[kk][...])
    x2 = x1_ref[...] + g2_ref[...] * y
    ms = jnp.mean(x2 * x2, axis=-1, keepdims=True)
    o_ref[...] = x2 * lax.rsqrt(ms + RMS_EPS) * fg_ref[...]


def _combine_rows(yg, gates, x1, gate2, final_g, out_so_far, b0, n_batches):
    S, D = x1.shape
    tm = COMBINE_ROWS
    steps = S // tm
    slot_spec = lambda kk: pl.BlockSpec((tm, D // 2), lambda i: (kk * steps + i, 0))
    in_specs = [slot_spec(kk) for kk in range(TOP_K)] + [
        pl.BlockSpec((tm, TOP_K), lambda i: (i, 0)),
        pl.BlockSpec((tm, D), lambda i: (i, 0)),
        pl.BlockSpec((1, D), lambda i: (0, 0)),
        pl.BlockSpec((1, D), lambda i: (0, 0))]
    args = [yg] * TOP_K + [gates, x1, gate2, final_g]
    aliases = {}
    if out_so_far is not None:
        in_specs.append(pl.BlockSpec(memory_space=pl.ANY))
        aliases = {len(args): 0}
        args.append(out_so_far)
    return pl.pallas_call(
        _combine_rows_kernel,
        out_shape=jax.ShapeDtypeStruct((n_batches * S, D), F32),
        grid=(steps,),
        in_specs=in_specs,
        out_specs=pl.BlockSpec((tm, D), lambda i: (b0 * steps + i, 0)),
        input_output_aliases=aliases,
        compiler_params=pltpu.CompilerParams(
            dimension_semantics=("arbitrary",), vmem_limit_bytes=VMEM_LIMIT),
        name="combine_rows",
    )(*args)


def _split_bf16x3(w):
    def top(v):
        return lax.bitcast_convert_type(lax.bitcast_convert_type(v, jnp.int32) & jnp.int32(-65536), F32)
    w0 = top(w)
    w1 = top(w - w0)
    w2 = w - w0 - w1
    return jnp.concatenate([w0, w1, w2], axis=1).astype(BF16).T


def _rotary_tables(positions):
    B, S = positions.shape
    half = ROT_DIM // 2
    inv_freq = jnp.exp(-math.log(ROPE_THETA) * jnp.arange(0, ROT_DIM, 2, dtype=F32) / ROT_DIM)
    ang = inv_freq[:, None] * positions.astype(F32).reshape(1, B * S)
    d = jnp.arange(128, dtype=jnp.int32) % ATTN_HEAD_DIM
    rotary = d < ROT_DIM
    sel = ((d[None, :] % half == jnp.arange(half, dtype=jnp.int32)[:, None]) & rotary[None, :]).astype(F32)
    sign = jnp.where(d < half, -1.0, 1.0)
    spread = lambda t, w: lax.dot_general(t, w, (((0,), (0,)), ((), ())), precision=HIGHEST)
    ct = spread(jnp.cos(ang), sel) + jnp.where(rotary, 0.0, 1.0)
    st = spread(jnp.sin(ang), sel * sign)
    return ct.reshape(B, S, 128), st.reshape(B, S, 128)


def kernel(x, c, positions, w_ada, b_ada, norm1_g, w_in, hgrn_lb_logits, hgrn_norm_g, attn_norm_g,
           w_out, norm2_g, w_router, b_router, w_gate_up, b_gate_up, w_down, b_down, final_norm_g):
    B, S, D = x.shape
    assert w_in.shape[0] == 1, "single-layer block: the final norm is fused into the combine step"
    l = 0
    ctab, stab = _rotary_tables(positions)
    lower_bounds = jnp.cumsum(jax.nn.softmax(hgrn_lb_logits.astype(F32), axis=0), axis=0)
    mod = _ada(c, w_ada[l], b_ada[l])
    shift1, scale1, gate1, shift2, scale2, gate2 = jnp.split(mod[:, None, :], N_MOD, axis=-1)
    n_blk = (S * TOP_K) // MOE_ROWS + N_EXPERTS
    experts = jnp.arange(N_EXPERTS, dtype=jnp.int32)[:, None]
    tok = jnp.broadcast_to(jnp.arange(S, dtype=jnp.int32)[None, :], (TOP_K, S)).reshape(-1)
    out = None
    mixed = []
    for b in range(B):
        one = slice(b, b + 1)
        q, k, lf, v, gt, aq, ak, av, km = _proj(
            x, scale1[one], shift1[one], norm1_g[l][None], w_in[l].astype(BF16), lower_bounds[l][None],
            ctab, stab, b)
        o_a = _moba(aq, km, ak, av, attn_norm_g[l][None])
        o_h = _hgrn(q, k, lf, v, gt, hgrn_norm_g[l][None])
        mixed.append(_mix(
            o_h, o_a, x, gate1[one], scale2[one], shift2[one], norm2_g[l][None], w_out[l].astype(BF16),
            _split_bf16x3(w_router[l]), b_router[l][:, None], b))
    for b in range(B):
        x1, h2, gates, idx8, counts = mixed[b]
        pad_start, blk_expert, n_used = _tile_layout(counts.reshape(-1), MOE_ROWS, n_blk)
        chosen = idx8[0, 0:TOP_K, :]
        pos = jnp.sum(jnp.where(chosen[:, None, :] == experts, pad_start[:, None], 0), axis=1) + idx8[0, TOP_K:, :]
        xs = _sc_permute(h2.reshape(S, D // 2), tok, pos.reshape(-1), n_blk * MOE_ROWS, 64)
        y_sorted = _moe_rows(xs, blk_expert, n_used, w_gate_up[l], b_gate_up[l], w_down[l], b_down[l])
        yg = _sc_gather(y_sorted, pos.reshape(-1), 64)
        out = _combine_rows(yg, gates.reshape(S, TOP_K), x1.reshape(S, D), gate2[b], final_norm_g[None], out, b, B)
    return out.reshape(B, S, D)
```

```python
import functools
import math

import jax
import jax.numpy as jnp
from jax import lax
from jax.experimental import pallas as pl
from jax.experimental.pallas import tpu as pltpu
from jax.experimental.pallas import tpu_sc as plsc

F32 = jnp.float32
BF16 = jnp.bfloat16
HIGHEST = lax.Precision.HIGHEST

HGRN_DK = 128
HGRN_CHUNK = 64
ATTN_HEADS = 4
ATTN_HEAD_DIM = 64
ROT_DIM = ATTN_HEAD_DIM // 4
ROPE_THETA = 500000.0
MOBA_BLOCK = 256
MOBA_TOPK = 3
N_EXPERTS = 32
TOP_K = 4
SWIGLU_ALPHA = 1.702
SWIGLU_LIMIT = 7.0
N_MOD = 6
RMS_EPS = 1e-6

HGRN_SUB = 16
EXP_CLAMP = 80.0
PROJ_ROWS = 512
HGRN_ROWS = 1024
MERGE_ROWS = 4096
COMBINE_ROWS = 1024
MIX_ROWS = 1024
MOE_ROWS = 512
MOBA_ROWS = 256
MOBA_TILES_PER_STEP = 16
PART_W = 128
V7X_VMEM_BYTES = 64 * 1024 * 1024
VMEM_LIMIT = V7X_VMEM_BYTES * 7 // 8
SC_CORES = 2
SC_SUBCORES = 16


def _sigmoid(x):
    return 1.0 / (1.0 + jnp.exp(-x))


def _dot(a, b, **kw):
    return jnp.dot(a, b, preferred_element_type=F32, **kw)


def _dot_nt(a, b, **kw):
    return lax.dot_general(a, b, (((1,), (1,)), ((), ())), preferred_element_type=F32, **kw)


def _pack_bf16_pairs(x):
    w = x.shape[1] // 2
    bits = lax.bitcast_convert_type(x.astype(BF16).astype(F32), jnp.int32)
    return bits[:, w:] | lax.shift_right_logical(bits[:, :w], 16)


def _unpack_bf16_pairs(p):
    lo = lax.bitcast_convert_type(lax.shift_left(p, 16), F32)
    hi = lax.bitcast_convert_type(p & jnp.int32(-65536), F32)
    return jnp.concatenate([lo, hi], axis=1)


def _ada_kernel(c_ref, w_ref, b_ref, o_ref):
    c = c_ref[...]
    o_ref[...] = _dot(c * _sigmoid(c), w_ref[...], precision=HIGHEST) + b_ref[...]


def _ada(c, w_ada, b_ada):
    B, D = c.shape
    N = w_ada.shape[1]
    tn = N // 4
    c8 = jnp.zeros((8, D), F32).at[:B].set(c)
    out = pl.pallas_call(
        _ada_kernel,
        out_shape=jax.ShapeDtypeStruct((8, N), F32),
        grid=(N // tn,),
        in_specs=[pl.BlockSpec((8, D), lambda j: (0, 0)),
                  pl.BlockSpec((D, tn), lambda j: (0, j)),
                  pl.BlockSpec((1, tn), lambda j: (0, j))],
        out_specs=pl.BlockSpec((8, tn), lambda j: (0, j)),
        compiler_params=pltpu.CompilerParams(vmem_limit_bytes=VMEM_LIMIT),
        name="ada",
    )(c8, w_ada, b_ada.reshape(1, N))
    return out[:B]


def _proj_kernel(x_ref, sc_ref, sh_ref, g_ref, w_ref, lb_ref, ct_ref, st_ref,
                 q_ref, k_ref, lf_ref, v_ref, gt_ref, aq_ref, ak_ref, av_ref, km_ref,
                 *, hw, aw):
    x = x_ref[...]
    ms = jnp.mean(x * x, axis=-1, keepdims=True)
    h = x * lax.rsqrt(ms + RMS_EPS) * g_ref[...]
    h = h * (1.0 + sc_ref[...]) + sh_ref[...]
    proj = _dot(h.astype(BF16), w_ref[...])

    hq = proj[:, 0:hw]
    hf = proj[:, hw:2 * hw]
    hg = proj[:, 3 * hw:4 * hw]
    q_ref[...] = (hq * _sigmoid(hq) * (HGRN_DK ** -0.5)).astype(BF16)
    lb = lb_ref[...]
    f = lb + (1.0 - lb) * _sigmoid(hf)
    k_ref[...] = (1.0 - f).astype(BF16)
    lf_ref[...] = jnp.log(f)
    v_ref[...] = proj[:, 2 * hw:3 * hw].astype(BF16)
    gt_ref[...] = (hg * _sigmoid(hg)).astype(BF16)

    ct = jnp.concatenate([ct_ref[...]] * (aw // 128), axis=1)
    st = jnp.concatenate([st_ref[...]] * (aw // 128), axis=1)
    lane = lax.broadcasted_iota(jnp.int32, ct.shape, 1) % ATTN_HEAD_DIM
    first_half = lane < (ROT_DIM // 2)

    def rot(t):
        partner = jnp.where(first_half, pltpu.roll(t, aw - ROT_DIM // 2, 1), pltpu.roll(t, ROT_DIM // 2, 1))
        return t * ct + partner * st

    base = 4 * hw
    aq = rot(proj[:, base:base + aw])
    ak = rot(proj[:, base + aw:base + 2 * aw])
    av = proj[:, base + 2 * aw:base + 3 * aw]
    for blk in range(ak.shape[0] // MOBA_BLOCK):
        km_ref[blk] = jnp.mean(ak[blk * MOBA_BLOCK:(blk + 1) * MOBA_BLOCK], axis=0, keepdims=True)
    lane128 = lax.broadcasted_iota(jnp.int32, (x.shape[0], 128), 1)
    for pair in range(ATTN_HEADS // 2):
        aq_ref[pair] = aq[:, pair * 128:(pair + 1) * 128]
    for hd in range(ATTN_HEADS):
        pair, half = divmod(hd, 2)
        in_head = (lane128 // ATTN_HEAD_DIM) == half
        ak_ref[hd] = jnp.where(in_head, ak[:, pair * 128:(pair + 1) * 128], 0.0).astype(BF16)
        av_ref[hd] = av[:, hd * ATTN_HEAD_DIM:(hd + 1) * ATTN_HEAD_DIM].astype(BF16)


def _proj(x, scale1, shift1, norm_g, w_in_bf16, lb, ctab, stab, b0):
    _, S, D = x.shape
    B = scale1.shape[0]
    hw = lb.shape[-1]
    aw = ATTN_HEADS * ATTN_HEAD_DIM
    tm = PROJ_ROWS
    nb = S // MOBA_BLOCK
    n_proj = w_in_bf16.shape[1]
    row = lambda b, i: (b, i, 0)
    xrow = lambda b, i: (b0 + b, i, 0)
    vec = lambda b, i: (b, 0, 0)
    head = lambda b, i: (b, 0, i, 0)
    out_shapes = (
        jax.ShapeDtypeStruct((B, S, hw), BF16),
        jax.ShapeDtypeStruct((B, S, hw), BF16),
        jax.ShapeDtypeStruct((B, S, hw), F32),
        jax.ShapeDtypeStruct((B, S, hw), BF16),
        jax.ShapeDtypeStruct((B, S, hw), BF16),
        jax.ShapeDtypeStruct((B, ATTN_HEADS // 2, S, 128), F32),
        jax.ShapeDtypeStruct((B, ATTN_HEADS, S, 128), BF16),
        jax.ShapeDtypeStruct((B, ATTN_HEADS, S, ATTN_HEAD_DIM), BF16),
        jax.ShapeDtypeStruct((B, nb, 1, aw), F32),
    )
    hspec = pl.BlockSpec((None, tm, hw), row)
    aspec = pl.BlockSpec((None, ATTN_HEADS, tm, ATTN_HEAD_DIM), head)
    return pl.pallas_call(
        functools.partial(_proj_kernel, hw=hw, aw=aw),
        out_shape=out_shapes,
        grid=(B, S // tm),
        in_specs=[pl.BlockSpec((None, tm, D), xrow),
                  pl.BlockSpec((None, 1, D), vec),
                  pl.BlockSpec((None, 1, D), vec),
                  pl.BlockSpec((1, D), lambda b, i: (0, 0)),
                  pl.BlockSpec((D, n_proj), lambda b, i: (0, 0)),
                  pl.BlockSpec((1, hw), lambda b, i: (0, 0)),
                  pl.BlockSpec((None, tm, 128), xrow),
                  pl.BlockSpec((None, tm, 128), xrow)],
        out_specs=(hspec, hspec, hspec, hspec, hspec,
                   pl.BlockSpec((None, ATTN_HEADS // 2, tm, 128), head),
                   pl.BlockSpec((None, ATTN_HEADS, tm, 128), head), aspec,
                   pl.BlockSpec((None, tm // MOBA_BLOCK, 1, aw), lambda b, i: (b, i, 0, 0))),
        compiler_params=pltpu.CompilerParams(
            dimension_semantics=("arbitrary", "arbitrary"), vmem_limit_bytes=VMEM_LIMIT),
        name="proj",
    )(x, scale1, shift1, norm_g, w_in_bf16, lb, ctab, stab)


def _hgrn_kernel(q_ref, k_ref, lf_ref, v_ref, gt_ref, gn_ref, o_ref, st_ref, *, n_heads, n_chunks):
    @pl.when(pl.program_id(1) == 0)
    def _():
        st_ref[...] = jnp.zeros_like(st_ref)

    C = HGRN_CHUNK
    r = lax.broadcasted_iota(jnp.int32, (C, C), 0)
    c = lax.broadcasted_iota(jnp.int32, (C, C), 1)
    tril = c <= r
    ltri = tril.astype(F32)
    gn = gn_ref[...]

    def chunk(ci, carry):
        r0 = pl.multiple_of(ci * C, C)
        rows = pl.ds(r0, C)
        b_all = _dot(ltri, lf_ref[rows, :], precision=HIGHEST)
        heads = range(n_heads)
        sls = [slice(hd * HGRN_DK, (hd + 1) * HGRN_DK) for hd in heads]
        bs = [b_all[:, sl] for sl in sls]
        b_lasts = [b[C - 1:C, :] for b in bs]
        qs = [q_ref[rows, sl].astype(F32) for sl in sls]
        ks = [k_ref[rows, sl].astype(F32) for sl in sls]
        vs = [v_ref[rows, sl] for sl in sls]
        states = [st_ref[hd] for hd in heads]
        o_inter = [_dot_nt((qs[hd] * jnp.exp(bs[hd])).astype(BF16), states[hd].astype(BF16)) for hd in heads]
        scores = []
        for hd in heads:
            blocks = []
            for g0 in range(0, C, HGRN_SUB):
                g1 = g0 + HGRN_SUB
                rho = 0.5 * (bs[hd][g0:g0 + 1, :] + bs[hd][g1 - 1:g1, :])
                qa = qs[hd][g0:g1, :] * jnp.exp(jnp.minimum(bs[hd][g0:g1, :] - rho, EXP_CLAMP))
                kb = ks[hd] * jnp.exp(jnp.minimum(rho - bs[hd], EXP_CLAMP))
                blocks.append(_dot_nt(qa.astype(BF16), kb.astype(BF16)))
            scores.append(jnp.where(tril, jnp.concatenate(blocks, axis=0), 0.0).astype(BF16))
        outs = [o_inter[hd] + _dot(scores[hd], vs[hd]) for hd in heads]
        kds = [(ks[hd] * jnp.exp(b_lasts[hd] - bs[hd])).astype(BF16) for hd in heads]
        upds = [_dot(vs[hd].astype(F32).T.astype(BF16), kds[hd]) for hd in heads]
        for hd in heads:
            st_ref[hd] = states[hd] * jnp.exp(b_lasts[hd]) + upds[hd]
            o = outs[hd]
            ms = jnp.mean(o * o, axis=-1, keepdims=True)
            o_ref[rows, sls[hd]] = (o * lax.rsqrt(ms + RMS_EPS) * gn * gt_ref[rows, sls[hd]].astype(F32)).astype(BF16)
        return carry

    lax.fori_loop(0, n_chunks, chunk, 0, unroll=True)


def _hgrn(q, k, lf, v, gt, norm_g):
    B, S, hw = q.shape
    n_heads = hw // HGRN_DK
    tc = HGRN_ROWS
    spec = pl.BlockSpec((None, tc, hw), lambda b, i: (b, i, 0))
    return pl.pallas_call(
        functools.partial(_hgrn_kernel, n_heads=n_heads, n_chunks=tc // HGRN_CHUNK),
        out_shape=jax.ShapeDtypeStruct((B, S, hw), BF16),
        grid=(B, S // tc),
        in_specs=[spec, spec, spec, spec, spec, pl.BlockSpec((1, HGRN_DK), lambda b, i: (0, 0))],
        out_specs=spec,
        scratch_shapes=[pltpu.VMEM((n_heads, HGRN_DK, HGRN_DK), F32)],
        compiler_params=pltpu.CompilerParams(
            dimension_semantics=("arbitrary", "arbitrary"), vmem_limit_bytes=VMEM_LIMIT),
        name="hgrn",
    )(q, k, lf, v, gt, norm_g)


def _sc_move_rows(table, src, dst, n_out, chunk):
    M = src.shape[0]
    D = table.shape[1]
    n_workers = SC_CORES * SC_SUBCORES
    per_worker = M // n_workers
    n_chunks = per_worker // chunk
    assert per_worker * n_workers == M and n_chunks * chunk == per_worker and n_chunks % 2 == 0 and chunk % 8 == 0
    mesh = plsc.VectorSubcoreMesh(core_axis_name="c", subcore_axis_name="s")
    idx_t = pltpu.VMEM((chunk,), jnp.int32)
    row_t = pltpu.VMEM((chunk, D), table.dtype)
    sem_t = pltpu.SemaphoreType.DMA

    def body(table_hbm, src_hbm, dst_hbm, out_hbm, src_v, dst_v, rows_v, g_sem, s_sem):
        wid = lax.axis_index("s") * SC_CORES + lax.axis_index("c")
        base = wid * per_worker

        def offset(j):
            return pl.multiple_of(base + j * chunk, 8)

        def gather(b):
            return pltpu.make_async_copy(table_hbm.at[src_v[b]], rows_v[b], g_sem[b])

        def start_gather(j, b):
            pltpu.sync_copy(src_hbm.at[pl.ds(offset(j), chunk)], src_v[b])
            gather(b).start()

        def write_out(j, b):
            if dst_hbm is None:
                pltpu.sync_copy(rows_v[b], out_hbm.at[pl.ds(offset(j), chunk)])
            else:
                pltpu.sync_copy(dst_hbm.at[pl.ds(offset(j), chunk)], dst_v[b])
                pltpu.async_copy(rows_v[b], out_hbm.at[dst_v[b]], s_sem[b]).wait()

        start_gather(0, 0)

        @pl.loop(0, n_chunks, step=2)
        def _(j):
            for b in (0, 1):
                @pl.when(j + b + 1 < n_chunks)
                def _():
                    start_gather(j + b + 1, 1 - b)
                gather(b).wait()
                write_out(j + b, b)

    if dst is None:
        @functools.partial(pl.kernel, mesh=mesh, out_type=jax.ShapeDtypeStruct((n_out, D), table.dtype),
                           scratch_types=[idx_t, idx_t, row_t, row_t, sem_t, sem_t])
        def gather_kernel(table_hbm, src_hbm, out_hbm, s0, s1, r0, r1, g0, g1):
            body(table_hbm, src_hbm, None, out_hbm, (s0, s1), None, (r0, r1), (g0, g1), None)
        return gather_kernel(table, src)

    @functools.partial(pl.kernel, mesh=mesh, out_type=jax.ShapeDtypeStruct((n_out, D), table.dtype),
                       scratch_types=[idx_t, idx_t, idx_t, idx_t, row_t, row_t, sem_t, sem_t, sem_t, sem_t])
    def permute_kernel(table_hbm, src_hbm, dst_hbm, out_hbm, s0, s1, d0, d1, r0, r1, g0, g1, w0, w1):
        body(table_hbm, src_hbm, dst_hbm, out_hbm, (s0, s1), (d0, d1), (r0, r1), (g0, g1), (w0, w1))
    return permute_kernel(table, src, dst)


def _sc_gather(table, idx, chunk):
    return _sc_move_rows(table, idx, None, idx.shape[0], chunk)


def _sc_permute(table, src, dst, n_out, chunk):
    return _sc_move_rows(table, src, dst, n_out, chunk)


def _tile_layout(counts, bm, n_tiles):
    n_groups = counts.shape[0]
    padded = (counts + bm - 1) // bm * bm
    pad_end = jnp.cumsum(padded)
    tile_start = jnp.arange(n_tiles, dtype=jnp.int32) * bm
    tile_group = jnp.minimum(
        jnp.sum((pad_end[None, :] <= tile_start[:, None]).astype(jnp.int32), axis=1), n_groups - 1)
    n_used = (pad_end[-1] // bm).astype(jnp.int32).reshape(1)
    return pad_end - padded, tile_group.astype(jnp.int32), n_used


def _null_partial(rows):
    lane = lax.broadcasted_iota(jnp.int32, (rows, PART_W), 1)
    return jnp.where(lane < ATTN_HEAD_DIM, 0.0, -jnp.inf).astype(F32)


def _moba_sel_kernel(q_ref, km_ref, k_ref, v_ref, idx_ref, cnt_ref, own_ref, cnt_acc, *, n_blocks):
    j = pl.program_id(1)
    T = MOBA_BLOCK
    heads = range(ATTN_HEADS)
    qs = [q_ref[hd // 2] for hd in heads]
    gates = [_dot_nt(km_ref[hd], qs[hd], precision=HIGHEST) for hd in heads]
    blk = lax.broadcasted_iota(jnp.int32, gates[0].shape, 0)
    neg_inf = jnp.float32(-jnp.inf)
    gates = [jnp.where(blk < j, g, neg_inf) for g in gates]
    picks = [[] for _ in heads]
    for _ in range(MOBA_TOPK):
        ms = [jnp.max(g, axis=0, keepdims=True) for g in gates]
        firsts = [jnp.min(jnp.where(g == m, blk, n_blocks), axis=0, keepdims=True) for g, m in zip(gates, ms)]
        for hd in heads:
            picks[hd].append(jnp.where(ms[hd] > neg_inf, firsts[hd], -1))
        gates = [jnp.where(blk == f, neg_inf, g) for g, f in zip(gates, firsts)]

    @pl.when(j == 0)
    def _():
        cnt_acc[...] = jnp.zeros_like(cnt_acc)

    earlier = (lax.broadcasted_iota(jnp.int32, (T, T), 0) < lax.broadcasted_iota(jnp.int32, (T, T), 1)).astype(BF16)
    for hd in heads:
        onehots = [(blk == p).astype(F32) for p in picks[hd]]
        member = onehots[0] + onehots[1] + onehots[2]
        base = cnt_acc[hd] + _dot(member.astype(BF16), earlier)
        ranks = [jnp.sum(oh * base, axis=0, keepdims=True).astype(jnp.int32) for oh in onehots]
        idx_ref[hd] = jnp.concatenate(picks[hd] + ranks + [jnp.zeros((2, T), jnp.int32)], axis=0)
        total = cnt_acc[hd] + jnp.sum(member, axis=1, keepdims=True)
        cnt_acc[hd] = total
        cnt_ref[hd] = total.astype(jnp.int32)
    causal = lax.broadcasted_iota(jnp.int32, (T, T), 1) <= lax.broadcasted_iota(jnp.int32, (T, T), 0)
    scale = ATTN_HEAD_DIM ** -0.5
    ss = [jnp.where(causal, _dot_nt((qs[hd] * scale).astype(BF16), k_ref[hd]), neg_inf) for hd in heads]
    mx = [jnp.max(s, axis=1, keepdims=True) for s in ss]
    ps = [jnp.exp(s - m) for s, m in zip(ss, mx)]
    ls = [jnp.sum(p, axis=1, keepdims=True) for p in ps]
    accs = [_dot(ps[hd].astype(BF16), v_ref[hd]) for hd in heads]
    for hd in heads:
        lse = jnp.broadcast_to(mx[hd] + jnp.log(ls[hd]), (T, PART_W - ATTN_HEAD_DIM))
        own_ref[hd] = jnp.concatenate([accs[hd] / ls[hd], lse], axis=1)


def _moba_sel(aq, kmean, ak, av):
    B, H, S, hd = av.shape
    nb = S // MOBA_BLOCK
    T = MOBA_BLOCK
    blk = lambda b, j: (b, 0, j, 0)
    return pl.pallas_call(
        functools.partial(_moba_sel_kernel, n_blocks=nb),
        out_shape=(jax.ShapeDtypeStruct((B, H, 8, S), jnp.int32),
                   jax.ShapeDtypeStruct((B, H, nb, 1), jnp.int32),
                   jax.ShapeDtypeStruct((B, H, S, PART_W), F32)),
        grid=(B, nb),
        in_specs=[pl.BlockSpec((None, H // 2, T, 128), blk),
                  pl.BlockSpec((None, H, nb, 128), lambda b, j: (b, 0, 0, 0)),
                  pl.BlockSpec((None, H, T, 128), blk),
                  pl.BlockSpec((None, H, T, hd), blk)],
        out_specs=(pl.BlockSpec((None, H, 8, T), lambda b, j: (b, 0, 0, j)),
                   pl.BlockSpec((None, H, nb, 1), lambda b, j: (b, 0, 0, 0)),
                   pl.BlockSpec((None, H, T, PART_W), blk)),
        scratch_shapes=[pltpu.VMEM((H, nb, 1), F32)],
        compiler_params=pltpu.CompilerParams(
            dimension_semantics=("arbitrary", "arbitrary"), vmem_limit_bytes=VMEM_LIMIT),
        name="moba_sel",
    )(aq, kmean, ak, av)


def _moba_blk_kernel(tg_ref, nu_ref, q_ref, k_ref, v_ref, o_ref, *, n_blocks):
    n = MOBA_TILES_PER_STEP
    R = MOBA_ROWS
    t0 = pl.program_id(0) * n

    @pl.when(t0 < nu_ref[0])
    def _():
        scale = ATTN_HEAD_DIM ** -0.5
        groups = [tg_ref[t0 + j] for j in range(n)]
        kv_rows = [(g // n_blocks, pl.ds(pl.multiple_of((g % n_blocks) * MOBA_BLOCK, MOBA_BLOCK), MOBA_BLOCK))
                   for g in groups]
        ss = [_dot_nt((q_ref[j * R:(j + 1) * R, :] * scale).astype(BF16), k_ref[kv_rows[j][0], kv_rows[j][1], :])
              for j in range(n)]
        ms = [jnp.max(s, axis=1, keepdims=True) for s in ss]
        ps = [jnp.exp(s - m) for s, m in zip(ss, ms)]
        ls = [jnp.sum(p, axis=1, keepdims=True) for p in ps]
        accs = [_dot(p.astype(BF16), v_ref[kv_rows[j][0], kv_rows[j][1], :]) for j, p in enumerate(ps)]
        null = _null_partial(R)
        for j in range(n):
            lse = jnp.broadcast_to(ms[j] + jnp.log(ls[j]), (R, PART_W - ATTN_HEAD_DIM))
            row = jnp.concatenate([accs[j] / ls[j], lse], axis=1)
            o_ref[j * R:(j + 1) * R, :] = jnp.where(t0 + j < nu_ref[0], row, null)

    @pl.when(t0 >= nu_ref[0])
    def _():
        o_ref[...] = _null_partial(n * R)


def _moba_blk(qs, tile_group, n_used, ak, av):
    B, H, S, hd = av.shape
    nb = S // MOBA_BLOCK
    R = MOBA_ROWS
    n = MOBA_TILES_PER_STEP
    n_tiles = qs.shape[0] // R
    assert n_tiles % n == 0
    whole = lambda i, tg, nu: (0, 0, 0)
    grid_spec = pltpu.PrefetchScalarGridSpec(
        num_scalar_prefetch=2,
        grid=(n_tiles // n,),
        in_specs=[pl.BlockSpec((n * R, 128), lambda i, tg, nu: (i, 0)),
                  pl.BlockSpec((B * H, S, 128), whole, pipeline_mode=pl.Buffered(1)),
                  pl.BlockSpec((B * H, S, hd), whole, pipeline_mode=pl.Buffered(1))],
        out_specs=pl.BlockSpec((n * R, PART_W), lambda i, tg, nu: (i, 0)),
    )
    return pl.pallas_call(
        functools.partial(_moba_blk_kernel, n_blocks=nb),
        out_shape=jax.ShapeDtypeStruct((n_tiles * R, PART_W), F32),
        grid_spec=grid_spec,
        compiler_params=pltpu.CompilerParams(
            dimension_semantics=("arbitrary",), vmem_limit_bytes=VMEM_LIMIT),
        name="moba_blk",
    )(tile_group, n_used, qs, ak.reshape(B * H, S, 128), av.reshape(B * H, S, hd))


def _moba_merge_kernel(own_ref, pg_ref, g_ref, o_ref):
    hd = ATTN_HEAD_DIM
    rows = [own_ref[...]] + [pg_ref[s] for s in range(MOBA_TOPK)]
    lses = [pltpu.roll(r, hd, 1) for r in rows]
    top = lses[0]
    for z in lses[1:]:
        top = jnp.maximum(top, z)
    num = jnp.zeros_like(top)
    den = jnp.zeros_like(top)
    for r, z in zip(rows, lses):
        w = jnp.exp(z - top)
        num = num + w * r
        den = den + w
    o = (num / den)[:, :hd]
    ms = jnp.mean(o * o, axis=-1, keepdims=True)
    o_ref[...] = o * lax.rsqrt(ms + RMS_EPS) * g_ref[...]


def _moba_merge(own, pg, norm_g):
    n = own.shape[0]
    T = MERGE_ROWS
    row = lambda i: (i, 0)
    return pl.pallas_call(
        _moba_merge_kernel,
        out_shape=jax.ShapeDtypeStruct((n, ATTN_HEAD_DIM), F32),
        grid=(n // T,),
        in_specs=[pl.BlockSpec((T, PART_W), row),
                  pl.BlockSpec((MOBA_TOPK, T, PART_W), lambda i: (0, i, 0)),
                  pl.BlockSpec((1, ATTN_HEAD_DIM), lambda i: (0, 0))],
        out_specs=pl.BlockSpec((T, ATTN_HEAD_DIM), row),
        compiler_params=pltpu.CompilerParams(
            dimension_semantics=("arbitrary",), vmem_limit_bytes=VMEM_LIMIT),
        name="moba_merge",
    )(own, pg, norm_g)


def _moba(aq, km, ak, av, norm_g):
    B, H, S, hd = av.shape
    nb = S // MOBA_BLOCK
    n_q = B * H * S
    kmp = km.reshape(B, nb, H // 2, 128)
    half = jnp.arange(128, dtype=jnp.int32) // hd
    kmean = jnp.stack([jnp.where(half == h % 2, kmp[:, :, h // 2, :], 0.0) for h in range(H)], axis=1)
    idx8, counts, own = _moba_sel(aq, kmean, ak, av)
    sel = idx8[:, :, 0:MOBA_TOPK, :].reshape(B * H, MOBA_TOPK, S)
    rank = idx8[:, :, MOBA_TOPK:2 * MOBA_TOPK, :].reshape(B * H, MOBA_TOPK, S)
    n_groups = B * H * nb
    n_tiles = (n_q * MOBA_TOPK) // MOBA_ROWS + n_groups
    pad_start, tile_group, n_used = _tile_layout(counts.reshape(-1), MOBA_ROWS, n_tiles)
    blocks = jnp.arange(nb, dtype=jnp.int32)[:, None]
    start = jnp.sum(jnp.where(sel[:, :, None, :] == blocks, pad_start.reshape(B * H, 1, nb, 1), 0), axis=2)
    a_ids = jnp.arange(n_q * MOBA_TOPK, dtype=jnp.int32).reshape(B * H, MOBA_TOPK, S)
    assert n_tiles * MOBA_ROWS >= n_q * MOBA_TOPK + MOBA_ROWS
    pos = jnp.where(sel >= 0, start + rank, n_used[0] * MOBA_ROWS + a_ids % MOBA_ROWS)
    bh = jnp.arange(B * H, dtype=jnp.int32)[:, None, None]
    t = jnp.arange(S, dtype=jnp.int32)[None, None, :]
    pair_row = jnp.broadcast_to((bh // H * (H // 2) + bh % H // 2) * S + t, pos.shape)
    qs = _sc_permute(aq.reshape(B * (H // 2) * S, 128), pair_row.reshape(-1), pos.reshape(-1),
                     n_tiles * MOBA_ROWS, 256)
    parts = _moba_blk(qs, tile_group, n_used, ak, av)
    pg = _sc_gather(parts, pos.transpose(1, 0, 2).reshape(-1), 256)
    o = _moba_merge(own.reshape(n_q, PART_W), pg.reshape(MOBA_TOPK, n_q, PART_W), norm_g)
    return o.reshape(B, H, S, hd)


def _mix_kernel(oh_ref, oa_ref, x_ref, g1_ref, sc2_ref, sh2_ref, n2_ref, wo_ref, wr_ref, br_ref,
                x1_ref, h2_ref, gw_ref, idx_ref, cnt_ref, cnt_acc):
    cat = jnp.concatenate([oh_ref[...]] + [oa_ref[hd] for hd in range(ATTN_HEADS)], axis=1)
    mix = _dot(cat.astype(BF16), wo_ref[...])
    x1 = x_ref[...] + g1_ref[...] * mix
    x1_ref[...] = x1
    ms = jnp.mean(x1 * x1, axis=-1, keepdims=True)
    h2 = x1 * lax.rsqrt(ms + RMS_EPS) * n2_ref[...]
    h2 = h2 * (1.0 + sc2_ref[...]) + sh2_ref[...]
    h2_ref[...] = _pack_bf16_pairs(h2)
    E = N_EXPERTS
    tm = h2.shape[0]
    h_0 = h2.astype(BF16)
    r_1 = h2 - h_0.astype(F32)
    h_1 = r_1.astype(BF16)
    h_2 = (r_1 - h_1.astype(F32)).astype(BF16)
    wt = wr_ref[...]
    p_0 = _dot_nt(wt, h_0)
    p_1 = _dot_nt(wt[:2 * E], h_1)
    p_2 = _dot_nt(wt[:E], h_2)
    logits = (p_0[:E] + (p_0[E:2 * E] + p_1[:E]) + (p_0[2 * E:] + p_1[E:] + p_2)) + br_ref[...]
    ex = lax.broadcasted_iota(jnp.int32, logits.shape, 0)
    neg_inf = jnp.float32(-jnp.inf)
    vals, idxs = [], []
    for _ in range(TOP_K):
        m = jnp.max(logits, axis=0, keepdims=True)
        first = jnp.min(jnp.where(logits == m, ex, E), axis=0, keepdims=True)
        vals.append(m)
        idxs.append(first)
        logits = jnp.where(ex == first, neg_inf, logits)
    e = [jnp.exp(v - vals[0]) for v in vals]
    denom = e[0] + e[1] + e[2] + e[3]
    gate_rows = jnp.concatenate([ei / denom for ei in e] + [jnp.zeros((128 - TOP_K, tm), F32)], axis=0)
    gw_ref[...] = gate_rows.T[:, :TOP_K]

    @pl.when((pl.program_id(0) == 0) & (pl.program_id(1) == 0))
    def _():
        cnt_acc[...] = jnp.zeros_like(cnt_acc)

    earlier = (lax.broadcasted_iota(jnp.int32, (tm, tm), 0) < lax.broadcasted_iota(jnp.int32, (tm, tm), 1)).astype(BF16)
    onehots = [(ex == ix).astype(F32) for ix in idxs]
    member = onehots[0] + onehots[1] + onehots[2] + onehots[3]
    base = cnt_acc[...] + _dot(member.astype(BF16), earlier)
    ranks = [jnp.sum(oh * base, axis=0, keepdims=True).astype(jnp.int32) for oh in onehots]
    idx_ref[...] = jnp.concatenate(idxs + ranks, axis=0)
    total = cnt_acc[...] + jnp.sum(member, axis=1, keepdims=True)
    cnt_acc[...] = total
    cnt_ref[...] = total.astype(jnp.int32)


def _mix(oh, oa, x, gate1, scale2, shift2, norm2_g, w_out_bf16, w_router, b_router, b0):
    _, S, D = x.shape
    B = oh.shape[0]
    hw = oh.shape[-1]
    tm = MIX_ROWS
    row = lambda b, i: (b, i, 0)
    xrow = lambda b, i: (b0 + b, i, 0)
    vec = lambda b, i: (b, 0, 0)
    const = lambda b, i: (0, 0)
    return pl.pallas_call(
        _mix_kernel,
        out_shape=(jax.ShapeDtypeStruct((B, S, D), F32),
                   jax.ShapeDtypeStruct((B, S, D // 2), jnp.int32),
                   jax.ShapeDtypeStruct((B, S, TOP_K), F32),
                   jax.ShapeDtypeStruct((B, 2 * TOP_K, S), jnp.int32),
                   jax.ShapeDtypeStruct((N_EXPERTS, 1), jnp.int32)),
        grid=(B, S // tm),
        in_specs=[pl.BlockSpec((None, tm, hw), row),
                  pl.BlockSpec((None, ATTN_HEADS, tm, ATTN_HEAD_DIM), lambda b, i: (b, 0, i, 0)),
                  pl.BlockSpec((None, tm, D), xrow),
                  pl.BlockSpec((None, 1, D), vec),
                  pl.BlockSpec((None, 1, D), vec),
                  pl.BlockSpec((None, 1, D), vec),
                  pl.BlockSpec((1, D), const),
                  pl.BlockSpec((D, D), const),
                  pl.BlockSpec((3 * N_EXPERTS, D), const),
                  pl.BlockSpec((N_EXPERTS, 1), const)],
        out_specs=(pl.BlockSpec((None, tm, D), row),
                   pl.BlockSpec((None, tm, D // 2), row),
                   pl.BlockSpec((None, tm, TOP_K), row),
                   pl.BlockSpec((None, 2 * TOP_K, tm), lambda b, i: (b, 0, i)),
                   pl.BlockSpec((N_EXPERTS, 1), const)),
        scratch_shapes=[pltpu.VMEM((N_EXPERTS, 1), F32)],
        compiler_params=pltpu.CompilerParams(
            dimension_semantics=("arbitrary", "arbitrary"), vmem_limit_bytes=VMEM_LIMIT),
        name="mix",
    )(oh, oa, x, gate1, scale2, shift2, norm2_g, w_out_bf16, w_router, b_router)


def _moe_rows_kernel(be_ref, nx_ref, sl_ref, nu_ref, x_ref, wgu_hbm, bgu_ref, wd_hbm, bd_ref, y_ref,
                     wgu32, wd32, sem, *, d_ff):
    i = pl.program_id(0)
    e = be_ref[i]
    s = sl_ref[i]

    def fetch(expert, slot):
        return (pltpu.make_async_copy(wgu_hbm.at[expert], wgu32.at[slot], sem.at[slot, 0]),
                pltpu.make_async_copy(wd_hbm.at[expert], wd32.at[slot], sem.at[slot, 1]))

    @pl.when(i == 0)
    def _():
        for c in fetch(e, s):
            c.start()

    @pl.when((i == 0) | (e != be_ref[jnp.maximum(i - 1, 0)]))
    def _():
        for c in fetch(e, s):
            c.wait()

        @pl.when(nx_ref[i] >= 0)
        def _():
            for c in fetch(nx_ref[i], 1 - s):
                c.start()

    @pl.when(i < nu_ref[0])
    def _():
        gu = _dot(_unpack_bf16_pairs(x_ref[...]).astype(BF16), wgu32[s].astype(BF16)) + bgu_ref[...]
        gate = jnp.minimum(gu[:, :d_ff], SWIGLU_LIMIT)
        up = jnp.clip(gu[:, d_ff:], -SWIGLU_LIMIT, SWIGLU_LIMIT)
        act = (up + 1.0) * gate * _sigmoid(SWIGLU_ALPHA * gate)
        y_ref[...] = _pack_bf16_pairs(_dot(act.astype(BF16), wd32[s].astype(BF16)) + bd_ref[...])

    @pl.when(i >= nu_ref[0])
    def _():
        y_ref[...] = jnp.zeros_like(y_ref)


def _moe_rows(xs, blk_expert, n_used, wgu, bgu, wd, bd):
    D = 2 * xs.shape[1]
    bm = MOE_ROWS
    n_blk = xs.shape[0] // bm
    d_ff = wd.shape[1]
    run_end = jnp.sum((blk_expert[None, :] <= blk_expert[:, None]).astype(jnp.int32), axis=1)
    next_expert = jnp.where(run_end < n_blk, blk_expert[jnp.minimum(run_end, n_blk - 1)], -1).astype(jnp.int32)
    changes = jnp.concatenate([jnp.zeros((1,), jnp.int32), (blk_expert[1:] != blk_expert[:-1]).astype(jnp.int32)])
    slot = (jnp.cumsum(changes) % 2).astype(jnp.int32)
    bsel = lambda i, be, nx, sl, nu: (be[i], 0, 0)
    rows = lambda i, be, nx, sl, nu: (i, 0)
    grid_spec = pltpu.PrefetchScalarGridSpec(
        num_scalar_prefetch=4,
        grid=(n_blk,),
        in_specs=[pl.BlockSpec((bm, D // 2), rows),
                  pl.BlockSpec(memory_space=pl.ANY),
                  pl.BlockSpec((None, 1, 2 * d_ff), bsel),
                  pl.BlockSpec(memory_space=pl.ANY),
                  pl.BlockSpec((None, 1, D), bsel)],
        out_specs=pl.BlockSpec((bm, D // 2), rows),
        scratch_shapes=[pltpu.VMEM((2, D, 2 * d_ff), F32), pltpu.VMEM((2, d_ff, D), F32),
                        pltpu.SemaphoreType.DMA((2, 2))],
    )
    return pl.pallas_call(
        functools.partial(_moe_rows_kernel, d_ff=d_ff),
        out_shape=jax.ShapeDtypeStruct((n_blk * bm, D // 2), jnp.int32),
        grid_spec=grid_spec,
        compiler_params=pltpu.CompilerParams(
            dimension_semantics=("arbitrary",), vmem_limit_bytes=VMEM_LIMIT),
        name="moe_rows",
    )(blk_expert, next_expert, slot, n_used, xs, wgu, bgu.reshape(N_EXPERTS, 1, 2 * d_ff), wd, bd.reshape(N_EXPERTS, 1, D))


def _combine_rows_kernel(*refs):
    y_refs = refs[:TOP_K]
    gw_ref, x1_ref, g2_ref, fg_ref = refs[TOP_K:TOP_K + 4]
    o_ref = refs[-1]
    gw = gw_ref[...]
    y = gw[:, 0:1] * _unpack_bf16_pairs(y_refs[0][...])
    for kk in range(1, TOP_K):
        y = y + gw[:, kk:kk + 1] * _unpack_bf16_pairs(y_refs[kk][...])
    x2 = x1_ref[...] + g2_ref[...] * y
    ms = jnp.mean(x2 * x2, axis=-1, keepdims=True)
    o_ref[...] = x2 * lax.rsqrt(ms + RMS_EPS) * fg_ref[...]


def _combine_rows(yg, gates, x1, gate2, final_g, out_so_far, b0, n_batches):
    S, D = x1.shape
    tm = COMBINE_ROWS
    steps = S // tm
    slot_spec = lambda kk: pl.BlockSpec((tm, D // 2), lambda i: (kk * steps + i, 0))
    in_specs = [slot_spec(kk) for kk in range(TOP_K)] + [
        pl.BlockSpec((tm, TOP_K), lambda i: (i, 0)),
        pl.BlockSpec((tm, D), lambda i: (i, 0)),
        pl.BlockSpec((1, D), lambda i: (0, 0)),
        pl.BlockSpec((1, D), lambda i: (0, 0))]
    args = [yg] * TOP_K + [gates, x1, gate2, final_g]
    aliases = {}
    if out_so_far is not None:
        in_specs.append(pl.BlockSpec(memory_space=pl.ANY))
        aliases = {len(args): 0}
        args.append(out_so_far)
    return pl.pallas_call(
        _combine_rows_kernel,
        out_shape=jax.ShapeDtypeStruct((n_batches * S, D), F32),
        grid=(steps,),
        in_specs=in_specs,
        out_specs=pl.BlockSpec((tm, D), lambda i: (b0 * steps + i, 0)),
        input_output_aliases=aliases,
        compiler_params=pltpu.CompilerParams(
            dimension_semantics=("arbitrary",), vmem_limit_bytes=VMEM_LIMIT),
        name="combine_rows",
    )(*args)


def _split_bf16x3(w):
    def top(v):
        return lax.bitcast_convert_type(lax.bitcast_convert_type(v, jnp.int32) & jnp.int32(-65536), F32)
    w0 = top(w)
    w1 = top(w - w0)
    w2 = w - w0 - w1
    return jnp.concatenate([w0, w1, w2], axis=1).astype(BF16).T


def _rotary_tables(positions):
    B, S = positions.shape
    half = ROT_DIM // 2
    inv_freq = jnp.exp(-math.log(ROPE_THETA) * jnp.arange(0, ROT_DIM, 2, dtype=F32) / ROT_DIM)
    ang = inv_freq[:, None] * positions.astype(F32).reshape(1, B * S)
    d = jnp.arange(128, dtype=jnp.int32) % ATTN_HEAD_DIM
    rotary = d < ROT_DIM
    sel = ((d[None, :] % half == jnp.arange(half, dtype=jnp.int32)[:, None]) & rotary[None, :]).astype(F32)
    sign = jnp.where(d < half, -1.0, 1.0)
    spread = lambda t, w: lax.dot_general(t, w, (((0,), (0,)), ((), ())), precision=HIGHEST)
    ct = spread(jnp.cos(ang), sel) + jnp.where(rotary, 0.0, 1.0)
    st = spread(jnp.sin(ang), sel * sign)
    return ct.reshape(B, S, 128), st.reshape(B, S, 128)


def kernel(x, c, positions, w_ada, b_ada, norm1_g, w_in, hgrn_lb_logits, hgrn_norm_g, attn_norm_g,
           w_out, norm2_g, w_router, b_router, w_gate_up, b_gate_up, w_down, b_down, final_norm_g):
    B, S, D = x.shape
    assert w_in.shape[0] == 1, "single-layer block: the final norm is fused into the combine step"
    l = 0
    ctab, stab = _rotary_tables(positions)
    lower_bounds = jnp.cumsum(jax.nn.softmax(hgrn_lb_logits.astype(F32), axis=0), axis=0)
    mod = _ada(c, w_ada[l], b_ada[l])
    shift1, scale1, gate1, shift2, scale2, gate2 = jnp.split(mod[:, None, :], N_MOD, axis=-1)
    n_blk = (S * TOP_K) // MOE_ROWS + N_EXPERTS
    experts = jnp.arange(N_EXPERTS, dtype=jnp.int32)[:, None]
    tok = jnp.broadcast_to(jnp.arange(S, dtype=jnp.int32)[None, :], (TOP_K, S)).reshape(-1)
    out = None
    for b in range(B):
        one = slice(b, b + 1)
        q, k, lf, v, gt, aq, ak, av, km = _proj(
            x, scale1[one], shift1[one], norm1_g[l][None], w_in[l].astype(BF16), lower_bounds[l][None],
            ctab, stab, b)
        o_a = _moba(aq, km, ak, av, attn_norm_g[l][None])
        o_h = _hgrn(q, k, lf, v, gt, hgrn_norm_g[l][None])
        x1, h2, gates, idx8, counts = _mix(
            o_h, o_a, x, gate1[one], scale2[one], shift2[one], norm2_g[l][None], w_out[l].astype(BF16),
            _split_bf16x3(w_router[l]), b_router[l][:, None], b)
        pad_start, blk_expert, n_used = _tile_layout(counts.reshape(-1), MOE_ROWS, n_blk)
        chosen = idx8[0, 0:TOP_K, :]
        pos = jnp.sum(jnp.where(chosen[:, None, :] == experts, pad_start[:, None], 0), axis=1) + idx8[0, TOP_K:, :]
        xs = _sc_permute(h2.reshape(S, D // 2), tok, pos.reshape(-1), n_blk * MOE_ROWS, 64)
        y_sorted = _moe_rows(xs, blk_expert, n_used, w_gate_up[l], b_gate_up[l], w_down[l], b_down[l])
        yg = _sc_gather(y_sorted, pos.reshape(-1), 64)
        out = _combine_rows(yg, gates.reshape(S, TOP_K), x1.reshape(S, D), gate2[b], final_norm_g[None], out, b, B)
    return out.reshape(B, S, D)
```

```python
import functools
import math

import jax
import jax.numpy as jnp
from jax import lax
from jax.experimental import pallas as pl
from jax.experimental.pallas import tpu as pltpu
from jax.experimental.pallas import tpu_sc as plsc

F32 = jnp.float32
BF16 = jnp.bfloat16
HIGHEST = lax.Precision.HIGHEST

HGRN_DK = 128
HGRN_CHUNK = 64
ATTN_HEADS = 4
ATTN_HEAD_DIM = 64
ROT_DIM = ATTN_HEAD_DIM // 4
ROPE_THETA = 500000.0
MOBA_BLOCK = 256
MOBA_TOPK = 3
N_EXPERTS = 32
TOP_K = 4
SWIGLU_ALPHA = 1.702
SWIGLU_LIMIT = 7.0
N_MOD = 6
RMS_EPS = 1e-6

HGRN_SUB = 16
EXP_CLAMP = 80.0
PROJ_ROWS = 512
HGRN_ROWS = 1024
MERGE_ROWS = 4096
COMBINE_ROWS = 1024
MIX_ROWS = 1024
MOE_ROWS = 512
MOBA_ROWS = 256
MOBA_TILES_PER_STEP = 16
PART_W = 128
V7X_VMEM_BYTES = 64 * 1024 * 1024
VMEM_LIMIT = V7X_VMEM_BYTES * 7 // 8
SC_CORES = 2
SC_SUBCORES = 16


def _sigmoid(x):
    return 1.0 / (1.0 + jnp.exp(-x))


def _dot(a, b, **kw):
    return jnp.dot(a, b, preferred_element_type=F32, **kw)


def _dot_nt(a, b, **kw):
    return lax.dot_general(a, b, (((1,), (1,)), ((), ())), preferred_element_type=F32, **kw)


def _pack_bf16_pairs(x):
    w = x.shape[1] // 2
    bits = lax.bitcast_convert_type(x.astype(BF16).astype(F32), jnp.int32)
    return bits[:, w:] | lax.shift_right_logical(bits[:, :w], 16)


def _unpack_bf16_pairs(p):
    lo = lax.bitcast_convert_type(lax.shift_left(p, 16), F32)
    hi = lax.bitcast_convert_type(p & jnp.int32(-65536), F32)
    return jnp.concatenate([lo, hi], axis=1)


def _ada_kernel(c_ref, w_ref, b_ref, o_ref):
    c = c_ref[...]
    o_ref[...] = _dot(c * _sigmoid(c), w_ref[...], precision=HIGHEST) + b_ref[...]


def _ada(c, w_ada, b_ada):
    B, D = c.shape
    N = w_ada.shape[1]
    tn = N // 4
    c8 = jnp.zeros((8, D), F32).at[:B].set(c)
    out = pl.pallas_call(
        _ada_kernel,
        out_shape=jax.ShapeDtypeStruct((8, N), F32),
        grid=(N // tn,),
        in_specs=[pl.BlockSpec((8, D), lambda j: (0, 0)),
                  pl.BlockSpec((D, tn), lambda j: (0, j)),
                  pl.BlockSpec((1, tn), lambda j: (0, j))],
        out_specs=pl.BlockSpec((8, tn), lambda j: (0, j)),
        compiler_params=pltpu.CompilerParams(vmem_limit_bytes=VMEM_LIMIT),
        name="ada",
    )(c8, w_ada, b_ada.reshape(1, N))
    return out[:B]


def _proj_kernel(x_ref, sc_ref, sh_ref, g_ref, w_ref, lb_ref, ct_ref, st_ref,
                 q_ref, k_ref, lf_ref, v_ref, gt_ref, aq_ref, ak_ref, av_ref, km_ref,
                 *, hw, aw):
    x = x_ref[...]
    ms = jnp.mean(x * x, axis=-1, keepdims=True)
    h = x * lax.rsqrt(ms + RMS_EPS) * g_ref[...]
    h = h * (1.0 + sc_ref[...]) + sh_ref[...]
    proj = _dot(h.astype(BF16), w_ref[...])

    hq = proj[:, 0:hw]
    hf = proj[:, hw:2 * hw]
    hg = proj[:, 3 * hw:4 * hw]
    q_ref[...] = (hq * _sigmoid(hq) * (HGRN_DK ** -0.5)).astype(BF16)
    lb = lb_ref[...]
    f = lb + (1.0 - lb) * _sigmoid(hf)
    k_ref[...] = (1.0 - f).astype(BF16)
    lf_ref[...] = jnp.log(f)
    v_ref[...] = proj[:, 2 * hw:3 * hw].astype(BF16)
    gt_ref[...] = (hg * _sigmoid(hg)).astype(BF16)

    ct = jnp.concatenate([ct_ref[...]] * (aw // 128), axis=1)
    st = jnp.concatenate([st_ref[...]] * (aw // 128), axis=1)
    lane = lax.broadcasted_iota(jnp.int32, ct.shape, 1) % ATTN_HEAD_DIM
    first_half = lane < (ROT_DIM // 2)

    def rot(t):
        partner = jnp.where(first_half, pltpu.roll(t, aw - ROT_DIM // 2, 1), pltpu.roll(t, ROT_DIM // 2, 1))
        return t * ct + partner * st

    base = 4 * hw
    aq = rot(proj[:, base:base + aw])
    ak = rot(proj[:, base + aw:base + 2 * aw])
    av = proj[:, base + 2 * aw:base + 3 * aw]
    for blk in range(ak.shape[0] // MOBA_BLOCK):
        km_ref[blk] = jnp.mean(ak[blk * MOBA_BLOCK:(blk + 1) * MOBA_BLOCK], axis=0, keepdims=True)
    lane128 = lax.broadcasted_iota(jnp.int32, (x.shape[0], 128), 1)
    for pair in range(ATTN_HEADS // 2):
        aq_ref[pair] = aq[:, pair * 128:(pair + 1) * 128]
    for hd in range(ATTN_HEADS):
        pair, half = divmod(hd, 2)
        in_head = (lane128 // ATTN_HEAD_DIM) == half
        ak_ref[hd] = jnp.where(in_head, ak[:, pair * 128:(pair + 1) * 128], 0.0).astype(BF16)
        av_ref[hd] = av[:, hd * ATTN_HEAD_DIM:(hd + 1) * ATTN_HEAD_DIM].astype(BF16)


def _proj(x, scale1, shift1, norm_g, w_in_bf16, lb, ctab, stab, b0):
    _, S, D = x.shape
    B = scale1.shape[0]
    hw = lb.shape[-1]
    aw = ATTN_HEADS * ATTN_HEAD_DIM
    tm = PROJ_ROWS
    nb = S // MOBA_BLOCK
    n_proj = w_in_bf16.shape[1]
    row = lambda b, i: (b, i, 0)
    xrow = lambda b, i: (b0 + b, i, 0)
    vec = lambda b, i: (b, 0, 0)
    head = lambda b, i: (b, 0, i, 0)
    out_shapes = (
        jax.ShapeDtypeStruct((B, S, hw), BF16),
        jax.ShapeDtypeStruct((B, S, hw), BF16),
        jax.ShapeDtypeStruct((B, S, hw), F32),
        jax.ShapeDtypeStruct((B, S, hw), BF16),
        jax.ShapeDtypeStruct((B, S, hw), BF16),
        jax.ShapeDtypeStruct((B, ATTN_HEADS // 2, S, 128), F32),
        jax.ShapeDtypeStruct((B, ATTN_HEADS, S, 128), BF16),
        jax.ShapeDtypeStruct((B, ATTN_HEADS, S, ATTN_HEAD_DIM), BF16),
        jax.ShapeDtypeStruct((B, nb, 1, aw), F32),
    )
    hspec = pl.BlockSpec((None, tm, hw), row)
    aspec = pl.BlockSpec((None, ATTN_HEADS, tm, ATTN_HEAD_DIM), head)
    return pl.pallas_call(
        functools.partial(_proj_kernel, hw=hw, aw=aw),
        out_shape=out_shapes,
        grid=(B, S // tm),
        in_specs=[pl.BlockSpec((None, tm, D), xrow),
                  pl.BlockSpec((None, 1, D), vec),
                  pl.BlockSpec((None, 1, D), vec),
                  pl.BlockSpec((1, D), lambda b, i: (0, 0)),
                  pl.BlockSpec((D, n_proj), lambda b, i: (0, 0)),
                  pl.BlockSpec((1, hw), lambda b, i: (0, 0)),
                  pl.BlockSpec((None, tm, 128), xrow),
                  pl.BlockSpec((None, tm, 128), xrow)],
        out_specs=(hspec, hspec, hspec, hspec, hspec,
                   pl.BlockSpec((None, ATTN_HEADS // 2, tm, 128), head),
                   pl.BlockSpec((None, ATTN_HEADS, tm, 128), head), aspec,
                   pl.BlockSpec((None, tm // MOBA_BLOCK, 1, aw), lambda b, i: (b, i, 0, 0))),
        compiler_params=pltpu.CompilerParams(
            dimension_semantics=("arbitrary", "arbitrary"), vmem_limit_bytes=VMEM_LIMIT),
        name="proj",
    )(x, scale1, shift1, norm_g, w_in_bf16, lb, ctab, stab)


def _hgrn_kernel(q_ref, k_ref, lf_ref, v_ref, gt_ref, gn_ref, o_ref, st_ref, *, n_heads, n_chunks):
    @pl.when(pl.program_id(1) == 0)
    def _():
        st_ref[...] = jnp.zeros_like(st_ref)

    C = HGRN_CHUNK
    r = lax.broadcasted_iota(jnp.int32, (C, C), 0)
    c = lax.broadcasted_iota(jnp.int32, (C, C), 1)
    tril = c <= r
    ltri = tril.astype(F32)
    gn = gn_ref[...]

    def chunk(ci, carry):
        r0 = pl.multiple_of(ci * C, C)
        rows = pl.ds(r0, C)
        b_all = _dot(ltri, lf_ref[rows, :], precision=HIGHEST)
        heads = range(n_heads)
        sls = [slice(hd * HGRN_DK, (hd + 1) * HGRN_DK) for hd in heads]
        bs = [b_all[:, sl] for sl in sls]
        b_lasts = [b[C - 1:C, :] for b in bs]
        qs = [q_ref[rows, sl].astype(F32) for sl in sls]
        ks = [k_ref[rows, sl].astype(F32) for sl in sls]
        vs = [v_ref[rows, sl] for sl in sls]
        states = [st_ref[hd] for hd in heads]
        o_inter = [_dot_nt((qs[hd] * jnp.exp(bs[hd])).astype(BF16), states[hd].astype(BF16)) for hd in heads]
        scores = []
        for hd in heads:
            blocks = []
            for g0 in range(0, C, HGRN_SUB):
                g1 = g0 + HGRN_SUB
                rho = 0.5 * (bs[hd][g0:g0 + 1, :] + bs[hd][g1 - 1:g1, :])
                qa = qs[hd][g0:g1, :] * jnp.exp(jnp.minimum(bs[hd][g0:g1, :] - rho, EXP_CLAMP))
                kb = ks[hd] * jnp.exp(jnp.minimum(rho - bs[hd], EXP_CLAMP))
                blocks.append(_dot_nt(qa.astype(BF16), kb.astype(BF16)))
            scores.append(jnp.where(tril, jnp.concatenate(blocks, axis=0), 0.0).astype(BF16))
        outs = [o_inter[hd] + _dot(scores[hd], vs[hd]) for hd in heads]
        kds = [(ks[hd] * jnp.exp(b_lasts[hd] - bs[hd])).astype(BF16) for hd in heads]
        upds = [_dot(vs[hd].astype(F32).T.astype(BF16), kds[hd]) for hd in heads]
        for hd in heads:
            st_ref[hd] = states[hd] * jnp.exp(b_lasts[hd]) + upds[hd]
            o = outs[hd]
            ms = jnp.mean(o * o, axis=-1, keepdims=True)
            o_ref[rows, sls[hd]] = (o * lax.rsqrt(ms + RMS_EPS) * gn * gt_ref[rows, sls[hd]].astype(F32)).astype(BF16)
        return carry

    lax.fori_loop(0, n_chunks, chunk, 0, unroll=True)


def _hgrn(q, k, lf, v, gt, norm_g):
    B, S, hw = q.shape
    n_heads = hw // HGRN_DK
    tc = HGRN_ROWS
    spec = pl.BlockSpec((None, tc, hw), lambda b, i: (b, i, 0))
    return pl.pallas_call(
        functools.partial(_hgrn_kernel, n_heads=n_heads, n_chunks=tc // HGRN_CHUNK),
        out_shape=jax.ShapeDtypeStruct((B, S, hw), BF16),
        grid=(B, S // tc),
        in_specs=[spec, spec, spec, spec, spec, pl.BlockSpec((1, HGRN_DK), lambda b, i: (0, 0))],
        out_specs=spec,
        scratch_shapes=[pltpu.VMEM((n_heads, HGRN_DK, HGRN_DK), F32)],
        compiler_params=pltpu.CompilerParams(
            dimension_semantics=("arbitrary", "arbitrary"), vmem_limit_bytes=VMEM_LIMIT),
        name="hgrn",
    )(q, k, lf, v, gt, norm_g)


def _sc_move_rows(table, src, dst, n_out, chunk):
    M = src.shape[0]
    D = table.shape[1]
    n_workers = SC_CORES * SC_SUBCORES
    per_worker = M // n_workers
    n_chunks = per_worker // chunk
    assert per_worker * n_workers == M and n_chunks * chunk == per_worker and n_chunks % 2 == 0 and chunk % 8 == 0
    mesh = plsc.VectorSubcoreMesh(core_axis_name="c", subcore_axis_name="s")
    idx_t = pltpu.VMEM((chunk,), jnp.int32)
    row_t = pltpu.VMEM((chunk, D), table.dtype)
    sem_t = pltpu.SemaphoreType.DMA

    def body(table_hbm, src_hbm, dst_hbm, out_hbm, src_v, dst_v, rows_v, g_sem, s_sem):
        wid = lax.axis_index("s") * SC_CORES + lax.axis_index("c")
        base = wid * per_worker

        def offset(j):
            return pl.multiple_of(base + j * chunk, 8)

        def gather(b):
            return pltpu.make_async_copy(table_hbm.at[src_v[b]], rows_v[b], g_sem[b])

        def start_gather(j, b):
            pltpu.sync_copy(src_hbm.at[pl.ds(offset(j), chunk)], src_v[b])
            gather(b).start()

        def write_out(j, b):
            if dst_hbm is None:
                pltpu.sync_copy(rows_v[b], out_hbm.at[pl.ds(offset(j), chunk)])
            else:
                pltpu.sync_copy(dst_hbm.at[pl.ds(offset(j), chunk)], dst_v[b])
                pltpu.async_copy(rows_v[b], out_hbm.at[dst_v[b]], s_sem[b]).wait()

        start_gather(0, 0)

        @pl.loop(0, n_chunks, step=2)
        def _(j):
            for b in (0, 1):
                @pl.when(j + b + 1 < n_chunks)
                def _():
                    start_gather(j + b + 1, 1 - b)
                gather(b).wait()
                write_out(j + b, b)

    if dst is None:
        @functools.partial(pl.kernel, mesh=mesh, out_type=jax.ShapeDtypeStruct((n_out, D), table.dtype),
                           scratch_types=[idx_t, idx_t, row_t, row_t, sem_t, sem_t])
        def gather_kernel(table_hbm, src_hbm, out_hbm, s0, s1, r0, r1, g0, g1):
            body(table_hbm, src_hbm, None, out_hbm, (s0, s1), None, (r0, r1), (g0, g1), None)
        return gather_kernel(table, src)

    @functools.partial(pl.kernel, mesh=mesh, out_type=jax.ShapeDtypeStruct((n_out, D), table.dtype),
                       scratch_types=[idx_t, idx_t, idx_t, idx_t, row_t, row_t, sem_t, sem_t, sem_t, sem_t])
    def permute_kernel(table_hbm, src_hbm, dst_hbm, out_hbm, s0, s1, d0, d1, r0, r1, g0, g1, w0, w1):
        body(table_hbm, src_hbm, dst_hbm, out_hbm, (s0, s1), (d0, d1), (r0, r1), (g0, g1), (w0, w1))
    return permute_kernel(table, src, dst)


def _sc_gather(table, idx, chunk):
    return _sc_move_rows(table, idx, None, idx.shape[0], chunk)


def _sc_permute(table, src, dst, n_out, chunk):
    return _sc_move_rows(table, src, dst, n_out, chunk)


def _tile_layout(counts, bm, n_tiles):
    n_groups = counts.shape[0]
    padded = (counts + bm - 1) // bm * bm
    pad_end = jnp.cumsum(padded)
    tile_start = jnp.arange(n_tiles, dtype=jnp.int32) * bm
    tile_group = jnp.minimum(
        jnp.sum((pad_end[None, :] <= tile_start[:, None]).astype(jnp.int32), axis=1), n_groups - 1)
    n_used = (pad_end[-1] // bm).astype(jnp.int32).reshape(1)
    return pad_end - padded, tile_group.astype(jnp.int32), n_used


def _null_partial(rows):
    lane = lax.broadcasted_iota(jnp.int32, (rows, PART_W), 1)
    return jnp.where(lane < ATTN_HEAD_DIM, 0.0, -jnp.inf).astype(F32)


def _moba_sel_kernel(q_ref, km_ref, k_ref, v_ref, idx_ref, cnt_ref, own_ref, cnt_acc, *, n_blocks):
    j = pl.program_id(1)
    T = MOBA_BLOCK
    heads = range(ATTN_HEADS)
    qs = [q_ref[hd // 2] for hd in heads]
    gates = [_dot_nt(km_ref[hd], qs[hd], precision=HIGHEST) for hd in heads]
    blk = lax.broadcasted_iota(jnp.int32, gates[0].shape, 0)
    neg_inf = jnp.float32(-jnp.inf)
    gates = [jnp.where(blk < j, g, neg_inf) for g in gates]
    picks = [[] for _ in heads]
    for _ in range(MOBA_TOPK):
        ms = [jnp.max(g, axis=0, keepdims=True) for g in gates]
        firsts = [jnp.min(jnp.where(g == m, blk, n_blocks), axis=0, keepdims=True) for g, m in zip(gates, ms)]
        for hd in heads:
            picks[hd].append(jnp.where(ms[hd] > neg_inf, firsts[hd], -1))
        gates = [jnp.where(blk == f, neg_inf, g) for g, f in zip(gates, firsts)]

    @pl.when(j == 0)
    def _():
        cnt_acc[...] = jnp.zeros_like(cnt_acc)

    earlier = (lax.broadcasted_iota(jnp.int32, (T, T), 0) < lax.broadcasted_iota(jnp.int32, (T, T), 1)).astype(BF16)
    for hd in heads:
        onehots = [(blk == p).astype(F32) for p in picks[hd]]
        member = onehots[0] + onehots[1] + onehots[2]
        base = cnt_acc[hd] + _dot(member.astype(BF16), earlier)
        ranks = [jnp.sum(oh * base, axis=0, keepdims=True).astype(jnp.int32) for oh in onehots]
        idx_ref[hd] = jnp.concatenate(picks[hd] + ranks + [jnp.zeros((2, T), jnp.int32)], axis=0)
        total = cnt_acc[hd] + jnp.sum(member, axis=1, keepdims=True)
        cnt_acc[hd] = total
        cnt_ref[hd] = total.astype(jnp.int32)
    causal = lax.broadcasted_iota(jnp.int32, (T, T), 1) <= lax.broadcasted_iota(jnp.int32, (T, T), 0)
    scale = ATTN_HEAD_DIM ** -0.5
    ss = [jnp.where(causal, _dot_nt((qs[hd] * scale).astype(BF16), k_ref[hd]), neg_inf) for hd in heads]
    mx = [jnp.max(s, axis=1, keepdims=True) for s in ss]
    ps = [jnp.exp(s - m) for s, m in zip(ss, mx)]
    ls = [jnp.sum(p, axis=1, keepdims=True) for p in ps]
    accs = [_dot(ps[hd].astype(BF16), v_ref[hd]) for hd in heads]
    for hd in heads:
        lse = jnp.broadcast_to(mx[hd] + jnp.log(ls[hd]), (T, PART_W - ATTN_HEAD_DIM))
        own_ref[hd] = jnp.concatenate([accs[hd] / ls[hd], lse], axis=1)


def _moba_sel(aq, kmean, ak, av):
    B, H, S, hd = av.shape
    nb = S // MOBA_BLOCK
    T = MOBA_BLOCK
    blk = lambda b, j: (b, 0, j, 0)
    return pl.pallas_call(
        functools.partial(_moba_sel_kernel, n_blocks=nb),
        out_shape=(jax.ShapeDtypeStruct((B, H, 8, S), jnp.int32),
                   jax.ShapeDtypeStruct((B, H, nb, 1), jnp.int32),
                   jax.ShapeDtypeStruct((B, H, S, PART_W), F32)),
        grid=(B, nb),
        in_specs=[pl.BlockSpec((None, H // 2, T, 128), blk),
                  pl.BlockSpec((None, H, nb, 128), lambda b, j: (b, 0, 0, 0)),
                  pl.BlockSpec((None, H, T, 128), blk),
                  pl.BlockSpec((None, H, T, hd), blk)],
        out_specs=(pl.BlockSpec((None, H, 8, T), lambda b, j: (b, 0, 0, j)),
                   pl.BlockSpec((None, H, nb, 1), lambda b, j: (b, 0, 0, 0)),
                   pl.BlockSpec((None, H, T, PART_W), blk)),
        scratch_shapes=[pltpu.VMEM((H, nb, 1), F32)],
        compiler_params=pltpu.CompilerParams(
            dimension_semantics=("arbitrary", "arbitrary"), vmem_limit_bytes=VMEM_LIMIT),
        name="moba_sel",
    )(aq, kmean, ak, av)


def _moba_blk_kernel(tg_ref, nu_ref, q_ref, k_ref, v_ref, o_ref, *, n_blocks):
    n = MOBA_TILES_PER_STEP
    R = MOBA_ROWS
    t0 = pl.program_id(0) * n

    @pl.when(t0 < nu_ref[0])
    def _():
        scale = ATTN_HEAD_DIM ** -0.5
        groups = [tg_ref[t0 + j] for j in range(n)]
        kv_rows = [(g // n_blocks, pl.ds(pl.multiple_of((g % n_blocks) * MOBA_BLOCK, MOBA_BLOCK), MOBA_BLOCK))
                   for g in groups]
        ss = [_dot_nt((q_ref[j * R:(j + 1) * R, :] * scale).astype(BF16), k_ref[kv_rows[j][0], kv_rows[j][1], :])
              for j in range(n)]
        ms = [jnp.max(s, axis=1, keepdims=True) for s in ss]
        ps = [jnp.exp(s - m) for s, m in zip(ss, ms)]
        ls = [jnp.sum(p, axis=1, keepdims=True) for p in ps]
        accs = [_dot(p.astype(BF16), v_ref[kv_rows[j][0], kv_rows[j][1], :]) for j, p in enumerate(ps)]
        null = _null_partial(R)
        for j in range(n):
            lse = jnp.broadcast_to(ms[j] + jnp.log(ls[j]), (R, PART_W - ATTN_HEAD_DIM))
            row = jnp.concatenate([accs[j] / ls[j], lse], axis=1)
            o_ref[j * R:(j + 1) * R, :] = jnp.where(t0 + j < nu_ref[0], row, null)

    @pl.when(t0 >= nu_ref[0])
    def _():
        o_ref[...] = _null_partial(n * R)


def _moba_blk(qs, tile_group, n_used, ak, av):
    B, H, S, hd = av.shape
    nb = S // MOBA_BLOCK
    R = MOBA_ROWS
    n = MOBA_TILES_PER_STEP
    n_tiles = qs.shape[0] // R
    assert n_tiles % n == 0
    whole = lambda i, tg, nu: (0, 0, 0)
    grid_spec = pltpu.PrefetchScalarGridSpec(
        num_scalar_prefetch=2,
        grid=(n_tiles // n,),
        in_specs=[pl.BlockSpec((n * R, 128), lambda i, tg, nu: (i, 0)),
                  pl.BlockSpec((B * H, S, 128), whole, pipeline_mode=pl.Buffered(1)),
                  pl.BlockSpec((B * H, S, hd), whole, pipeline_mode=pl.Buffered(1))],
        out_specs=pl.BlockSpec((n * R, PART_W), lambda i, tg, nu: (i, 0)),
    )
    return pl.pallas_call(
        functools.partial(_moba_blk_kernel, n_blocks=nb),
        out_shape=jax.ShapeDtypeStruct((n_tiles * R, PART_W), F32),
        grid_spec=grid_spec,
        compiler_params=pltpu.CompilerParams(
            dimension_semantics=("arbitrary",), vmem_limit_bytes=VMEM_LIMIT),
        name="moba_blk",
    )(tile_group, n_used, qs, ak.reshape(B * H, S, 128), av.reshape(B * H, S, hd))


def _moba_merge_kernel(own_ref, pg_ref, g_ref, o_ref):
    hd = ATTN_HEAD_DIM
    rows = [own_ref[...]] + [pg_ref[s] for s in range(MOBA_TOPK)]
    lses = [pltpu.roll(r, hd, 1) for r in rows]
    top = lses[0]
    for z in lses[1:]:
        top = jnp.maximum(top, z)
    num = jnp.zeros_like(top)
    den = jnp.zeros_like(top)
    for r, z in zip(rows, lses):
        w = jnp.exp(z - top)
        num = num + w * r
        den = den + w
    o = (num / den)[:, :hd]
    ms = jnp.mean(o * o, axis=-1, keepdims=True)
    o_ref[...] = o * lax.rsqrt(ms + RMS_EPS) * g_ref[...]


def _moba_merge(own, pg, norm_g):
    n = own.shape[0]
    T = MERGE_ROWS
    row = lambda i: (i, 0)
    return pl.pallas_call(
        _moba_merge_kernel,
        out_shape=jax.ShapeDtypeStruct((n, ATTN_HEAD_DIM), F32),
        grid=(n // T,),
        in_specs=[pl.BlockSpec((T, PART_W), row),
                  pl.BlockSpec((MOBA_TOPK, T, PART_W), lambda i: (0, i, 0)),
                  pl.BlockSpec((1, ATTN_HEAD_DIM), lambda i: (0, 0))],
        out_specs=pl.BlockSpec((T, ATTN_HEAD_DIM), row),
        compiler_params=pltpu.CompilerParams(
            dimension_semantics=("arbitrary",), vmem_limit_bytes=VMEM_LIMIT),
        name="moba_merge",
    )(own, pg, norm_g)


def _moba(aq, km, ak, av, norm_g):
    B, H, S, hd = av.shape
    nb = S // MOBA_BLOCK
    n_q = B * H * S
    kmp = km.reshape(B, nb, H // 2, 128)
    half = jnp.arange(128, dtype=jnp.int32) // hd
    kmean = jnp.stack([jnp.where(half == h % 2, kmp[:, :, h // 2, :], 0.0) for h in range(H)], axis=1)
    idx8, counts, own = _moba_sel(aq, kmean, ak, av)
    sel = idx8[:, :, 0:MOBA_TOPK, :].reshape(B * H, MOBA_TOPK, S)
    rank = idx8[:, :, MOBA_TOPK:2 * MOBA_TOPK, :].reshape(B * H, MOBA_TOPK, S)
    n_groups = B * H * nb
    n_tiles = (n_q * MOBA_TOPK) // MOBA_ROWS + n_groups
    pad_start, tile_group, n_used = _tile_layout(counts.reshape(-1), MOBA_ROWS, n_tiles)
    blocks = jnp.arange(nb, dtype=jnp.int32)[:, None]
    start = jnp.sum(jnp.where(sel[:, :, None, :] == blocks, pad_start.reshape(B * H, 1, nb, 1), 0), axis=2)
    a_ids = jnp.arange(n_q * MOBA_TOPK, dtype=jnp.int32).reshape(B * H, MOBA_TOPK, S)
    assert n_tiles * MOBA_ROWS >= n_q * MOBA_TOPK + MOBA_ROWS
    pos = jnp.where(sel >= 0, start + rank, n_used[0] * MOBA_ROWS + a_ids % MOBA_ROWS)
    bh = jnp.arange(B * H, dtype=jnp.int32)[:, None, None]
    t = jnp.arange(S, dtype=jnp.int32)[None, None, :]
    pair_row = jnp.broadcast_to((bh // H * (H // 2) + bh % H // 2) * S + t, pos.shape)
    qs = _sc_permute(aq.reshape(B * (H // 2) * S, 128), pair_row.reshape(-1), pos.reshape(-1),
                     n_tiles * MOBA_ROWS, 256)
    parts = _moba_blk(qs, tile_group, n_used, ak, av)
    pg = _sc_gather(parts, pos.transpose(1, 0, 2).reshape(-1), 256)
    o = _moba_merge(own.reshape(n_q, PART_W), pg.reshape(MOBA_TOPK, n_q, PART_W), norm_g)
    return o.reshape(B, H, S, hd)


def _mix_kernel(oh_ref, oa_ref, x_ref, g1_ref, sc2_ref, sh2_ref, n2_ref, wo_ref, wr_ref, br_ref,
                x1_ref, h2_ref, gw_ref, idx_ref, cnt_ref, cnt_acc):
    cat = jnp.concatenate([oh_ref[...]] + [oa_ref[hd] for hd in range(ATTN_HEADS)], axis=1)
    mix = _dot(cat.astype(BF16), wo_ref[...])
    x1 = x_ref[...] + g1_ref[...] * mix
    x1_ref[...] = x1
    ms = jnp.mean(x1 * x1, axis=-1, keepdims=True)
    h2 = x1 * lax.rsqrt(ms + RMS_EPS) * n2_ref[...]
    h2 = h2 * (1.0 + sc2_ref[...]) + sh2_ref[...]
    h2_ref[...] = _pack_bf16_pairs(h2)
    E = N_EXPERTS
    tm = h2.shape[0]
    h_0 = h2.astype(BF16)
    r_1 = h2 - h_0.astype(F32)
    h_1 = r_1.astype(BF16)
    h_2 = (r_1 - h_1.astype(F32)).astype(BF16)
    wt = wr_ref[...]
    p_0 = _dot_nt(wt, h_0)
    p_1 = _dot_nt(wt[:2 * E], h_1)
    p_2 = _dot_nt(wt[:E], h_2)
    logits = (p_0[:E] + (p_0[E:2 * E] + p_1[:E]) + (p_0[2 * E:] + p_1[E:] + p_2)) + br_ref[...]
    ex = lax.broadcasted_iota(jnp.int32, logits.shape, 0)
    neg_inf = jnp.float32(-jnp.inf)
    vals, idxs = [], []
    for _ in range(TOP_K):
        m = jnp.max(logits, axis=0, keepdims=True)
        first = jnp.min(jnp.where(logits == m, ex, E), axis=0, keepdims=True)
        vals.append(m)
        idxs.append(first)
        logits = jnp.where(ex == first, neg_inf, logits)
    e = [jnp.exp(v - vals[0]) for v in vals]
    denom = e[0] + e[1] + e[2] + e[3]
    gate_rows = jnp.concatenate([ei / denom for ei in e] + [jnp.zeros((128 - TOP_K, tm), F32)], axis=0)
    gw_ref[...] = gate_rows.T[:, :TOP_K]

    @pl.when((pl.program_id(0) == 0) & (pl.program_id(1) == 0))
    def _():
        cnt_acc[...] = jnp.zeros_like(cnt_acc)

    earlier = (lax.broadcasted_iota(jnp.int32, (tm, tm), 0) < lax.broadcasted_iota(jnp.int32, (tm, tm), 1)).astype(BF16)
    onehots = [(ex == ix).astype(F32) for ix in idxs]
    member = onehots[0] + onehots[1] + onehots[2] + onehots[3]
    base = cnt_acc[...] + _dot(member.astype(BF16), earlier)
    ranks = [jnp.sum(oh * base, axis=0, keepdims=True).astype(jnp.int32) for oh in onehots]
    idx_ref[...] = jnp.concatenate(idxs + ranks, axis=0)
    total = cnt_acc[...] + jnp.sum(member, axis=1, keepdims=True)
    cnt_acc[...] = total
    cnt_ref[...] = total.astype(jnp.int32)


def _mix(oh, oa, x, gate1, scale2, shift2, norm2_g, w_out_bf16, w_router, b_router, b0):
    _, S, D = x.shape
    B = oh.shape[0]
    hw = oh.shape[-1]
    tm = MIX_ROWS
    row = lambda b, i: (b, i, 0)
    xrow = lambda b, i: (b0 + b, i, 0)
    vec = lambda b, i: (b, 0, 0)
    const = lambda b, i: (0, 0)
    return pl.pallas_call(
        _mix_kernel,
        out_shape=(jax.ShapeDtypeStruct((B, S, D), F32),
                   jax.ShapeDtypeStruct((B, S, D // 2), jnp.int32),
                   jax.ShapeDtypeStruct((B, S, TOP_K), F32),
                   jax.ShapeDtypeStruct((B, 2 * TOP_K, S), jnp.int32),
                   jax.ShapeDtypeStruct((N_EXPERTS, 1), jnp.int32)),
        grid=(B, S // tm),
        in_specs=[pl.BlockSpec((None, tm, hw), row),
                  pl.BlockSpec((None, ATTN_HEADS, tm, ATTN_HEAD_DIM), lambda b, i: (b, 0, i, 0)),
                  pl.BlockSpec((None, tm, D), xrow),
                  pl.BlockSpec((None, 1, D), vec),
                  pl.BlockSpec((None, 1, D), vec),
                  pl.BlockSpec((None, 1, D), vec),
                  pl.BlockSpec((1, D), const),
                  pl.BlockSpec((D, D), const),
                  pl.BlockSpec((3 * N_EXPERTS, D), const),
                  pl.BlockSpec((N_EXPERTS, 1), const)],
        out_specs=(pl.BlockSpec((None, tm, D), row),
                   pl.BlockSpec((None, tm, D // 2), row),
                   pl.BlockSpec((None, tm, TOP_K), row),
                   pl.BlockSpec((None, 2 * TOP_K, tm), lambda b, i: (b, 0, i)),
                   pl.BlockSpec((N_EXPERTS, 1), const)),
        scratch_shapes=[pltpu.VMEM((N_EXPERTS, 1), F32)],
        compiler_params=pltpu.CompilerParams(
            dimension_semantics=("arbitrary", "arbitrary"), vmem_limit_bytes=VMEM_LIMIT),
        name="mix",
    )(oh, oa, x, gate1, scale2, shift2, norm2_g, w_out_bf16, w_router, b_router)


def _moe_rows_kernel(be_ref, nx_ref, sl_ref, nu_ref, x_ref, wgu_hbm, bgu_ref, wd_hbm, bd_ref, y_ref,
                     wgu32, wd32, sem, *, d_ff):
    i = pl.program_id(0)
    e = be_ref[i]
    s = sl_ref[i]

    def fetch(expert, slot):
        return (pltpu.make_async_copy(wgu_hbm.at[expert], wgu32.at[slot], sem.at[slot, 0]),
                pltpu.make_async_copy(wd_hbm.at[expert], wd32.at[slot], sem.at[slot, 1]))

    @pl.when(i == 0)
    def _():
        for c in fetch(e, s):
            c.start()

    @pl.when((i == 0) | (e != be_ref[jnp.maximum(i - 1, 0)]))
    def _():
        for c in fetch(e, s):
            c.wait()

        @pl.when(nx_ref[i] >= 0)
        def _():
            for c in fetch(nx_ref[i], 1 - s):
                c.start()

    @pl.when(i < nu_ref[0])
    def _():
        xb = _unpack_bf16_pairs(x_ref[...]).astype(BF16)
        w = d_ff // 2
        acc = bd_ref[...]
        for h in range(2):
            g = _dot(xb, wgu32[s, :, h * w:(h + 1) * w].astype(BF16)) + bgu_ref[:, h * w:(h + 1) * w]
            u = (_dot(xb, wgu32[s, :, d_ff + h * w:d_ff + (h + 1) * w].astype(BF16))
                 + bgu_ref[:, d_ff + h * w:d_ff + (h + 1) * w])
            gate = jnp.minimum(g, SWIGLU_LIMIT)
            up = jnp.clip(u, -SWIGLU_LIMIT, SWIGLU_LIMIT)
            act = (up + 1.0) * gate * _sigmoid(SWIGLU_ALPHA * gate)
            acc = acc + _dot(act.astype(BF16), wd32[s, h * w:(h + 1) * w, :].astype(BF16))
        y_ref[...] = _pack_bf16_pairs(acc)

    @pl.when(i >= nu_ref[0])
    def _():
        y_ref[...] = jnp.zeros_like(y_ref)


def _moe_rows(xs, blk_expert, n_used, wgu, bgu, wd, bd):
    D = 2 * xs.shape[1]
    bm = MOE_ROWS
    n_blk = xs.shape[0] // bm
    d_ff = wd.shape[1]
    run_end = jnp.sum((blk_expert[None, :] <= blk_expert[:, None]).astype(jnp.int32), axis=1)
    next_expert = jnp.where(run_end < n_blk, blk_expert[jnp.minimum(run_end, n_blk - 1)], -1).astype(jnp.int32)
    changes = jnp.concatenate([jnp.zeros((1,), jnp.int32), (blk_expert[1:] != blk_expert[:-1]).astype(jnp.int32)])
    slot = (jnp.cumsum(changes) % 2).astype(jnp.int32)
    bsel = lambda i, be, nx, sl, nu: (be[i], 0, 0)
    rows = lambda i, be, nx, sl, nu: (i, 0)
    grid_spec = pltpu.PrefetchScalarGridSpec(
        num_scalar_prefetch=4,
        grid=(n_blk,),
        in_specs=[pl.BlockSpec((bm, D // 2), rows),
                  pl.BlockSpec(memory_space=pl.ANY),
                  pl.BlockSpec((None, 1, 2 * d_ff), bsel),
                  pl.BlockSpec(memory_space=pl.ANY),
                  pl.BlockSpec((None, 1, D), bsel)],
        out_specs=pl.BlockSpec((bm, D // 2), rows),
        scratch_shapes=[pltpu.VMEM((2, D, 2 * d_ff), F32), pltpu.VMEM((2, d_ff, D), F32),
                        pltpu.SemaphoreType.DMA((2, 2))],
    )
    return pl.pallas_call(
        functools.partial(_moe_rows_kernel, d_ff=d_ff),
        out_shape=jax.ShapeDtypeStruct((n_blk * bm, D // 2), jnp.int32),
        grid_spec=grid_spec,
        compiler_params=pltpu.CompilerParams(
            dimension_semantics=("arbitrary",), vmem_limit_bytes=VMEM_LIMIT),
        name="moe_rows",
    )(blk_expert, next_expert, slot, n_used, xs, wgu, bgu.reshape(N_EXPERTS, 1, 2 * d_ff), wd, bd.reshape(N_EXPERTS, 1, D))


def _combine_rows_kernel(*refs):
    y_refs = refs[:TOP_K]
    gw_ref, x1_ref, g2_ref, fg_ref = refs[TOP_K:TOP_K + 4]
    o_ref = refs[-1]
    gw = gw_ref[...]
    y = gw[:, 0:1] * _unpack_bf16_pairs(y_refs[0][...])
    for kk in range(1, TOP_K):
        y = y + gw[:, kk:kk + 1] * _unpack_bf16_pairs(y_refs[kk][...])
    x2 = x1_ref[...] + g2_ref[...] * y
    ms = jnp.mean(x2 * x2, axis=-1, keepdims=True)
    o_ref[...] = x2 * lax.rsqrt(ms + RMS_EPS) * fg_ref[...]


def _combine_rows(yg, gates, x1, gate2, final_g, out_so_far, b0, n_batches):
    S, D = x1.shape
    tm = COMBINE_ROWS
    steps = S // tm
    slot_spec = lambda kk: pl.BlockSpec((tm, D // 2), lambda i: (kk * steps + i, 0))
    in_specs = [slot_spec(kk) for kk in range(TOP_K)] + [
        pl.BlockSpec((tm, TOP_K), lambda i: (i, 0)),
        pl.BlockSpec((tm, D), lambda i: (i, 0)),
        pl.BlockSpec((1, D), lambda i: (0, 0)),
        pl.BlockSpec((1, D), lambda i: (0, 0))]
    args = [yg] * TOP_K + [gates, x1, gate2, final_g]
    aliases = {}
    if out_so_far is not None:
        in_specs.append(pl.BlockSpec(memory_space=pl.ANY))
        aliases = {len(args): 0}
        args.append(out_so_far)
    return pl.pallas_call(
        _combine_rows_kernel,
        out_shape=jax.ShapeDtypeStruct((n_batches * S, D), F32),
        grid=(steps,),
        in_specs=in_specs,
        out_specs=pl.BlockSpec((tm, D), lambda i: (b0 * steps + i, 0)),
        input_output_aliases=aliases,
        compiler_params=pltpu.CompilerParams(
            dimension_semantics=("arbitrary",), vmem_limit_bytes=VMEM_LIMIT),
        name="combine_rows",
    )(*args)


def _split_bf16x3(w):
    def top(v):
        return lax.bitcast_convert_type(lax.bitcast_convert_type(v, jnp.int32) & jnp.int32(-65536), F32)
    w0 = top(w)
    w1 = top(w - w0)
    w2 = w - w0 - w1
    return jnp.concatenate([w0, w1, w2], axis=1).astype(BF16).T


def _rotary_tables(positions):
    B, S = positions.shape
    half = ROT_DIM // 2
    inv_freq = jnp.exp(-math.log(ROPE_THETA) * jnp.arange(0, ROT_DIM, 2, dtype=F32) / ROT_DIM)
    ang = inv_freq[:, None] * positions.astype(F32).reshape(1, B * S)
    d = jnp.arange(128, dtype=jnp.int32) % ATTN_HEAD_DIM
    rotary = d < ROT_DIM
    sel = ((d[None, :] % half == jnp.arange(half, dtype=jnp.int32)[:, None]) & rotary[None, :]).astype(F32)
    sign = jnp.where(d < half, -1.0, 1.0)
    spread = lambda t, w: lax.dot_general(t, w, (((0,), (0,)), ((), ())), precision=HIGHEST)
    ct = spread(jnp.cos(ang), sel) + jnp.where(rotary, 0.0, 1.0)
    st = spread(jnp.sin(ang), sel * sign)
    return ct.reshape(B, S, 128), st.reshape(B, S, 128)


def kernel(x, c, positions, w_ada, b_ada, norm1_g, w_in, hgrn_lb_logits, hgrn_norm_g, attn_norm_g,
           w_out, norm2_g, w_router, b_router, w_gate_up, b_gate_up, w_down, b_down, final_norm_g):
    B, S, D = x.shape
    assert w_in.shape[0] == 1, "single-layer block: the final norm is fused into the combine step"
    l = 0
    ctab, stab = _rotary_tables(positions)
    lower_bounds = jnp.cumsum(jax.nn.softmax(hgrn_lb_logits.astype(F32), axis=0), axis=0)
    mod = _ada(c, w_ada[l], b_ada[l])
    shift1, scale1, gate1, shift2, scale2, gate2 = jnp.split(mod[:, None, :], N_MOD, axis=-1)
    n_blk = (S * TOP_K) // MOE_ROWS + N_EXPERTS
    experts = jnp.arange(N_EXPERTS, dtype=jnp.int32)[:, None]
    tok = jnp.broadcast_to(jnp.arange(S, dtype=jnp.int32)[None, :], (TOP_K, S)).reshape(-1)
    out = None
    for b in range(B):
        one = slice(b, b + 1)
        q, k, lf, v, gt, aq, ak, av, km = _proj(
            x, scale1[one], shift1[one], norm1_g[l][None], w_in[l].astype(BF16), lower_bounds[l][None],
            ctab, stab, b)
        o_a = _moba(aq, km, ak, av, attn_norm_g[l][None])
        o_h = _hgrn(q, k, lf, v, gt, hgrn_norm_g[l][None])
        x1, h2, gates, idx8, counts = _mix(
            o_h, o_a, x, gate1[one], scale2[one], shift2[one], norm2_g[l][None], w_out[l].astype(BF16),
            _split_bf16x3(w_router[l]), b_router[l][:, None], b)
        pad_start, blk_expert, n_used = _tile_layout(counts.reshape(-1), MOE_ROWS, n_blk)
        chosen = idx8[0, 0:TOP_K, :]
        pos = jnp.sum(jnp.where(chosen[:, None, :] == experts, pad_start[:, None], 0), axis=1) + idx8[0, TOP_K:, :]
        xs = _sc_permute(h2.reshape(S, D // 2), tok, pos.reshape(-1), n_blk * MOE_ROWS, 64)
        y_sorted = _moe_rows(xs, blk_expert, n_used, w_gate_up[l], b_gate_up[l], w_down[l], b_down[l])
        yg = _sc_gather(y_sorted, pos.reshape(-1), 64)
        out = _combine_rows(yg, gates.reshape(S, TOP_K), x1.reshape(S, D), gate2[b], final_norm_g[None], out, b, B)
    return out.reshape(B, S, D)
```

```python
import functools
import math

import jax
import jax.numpy as jnp
from jax import lax
from jax.experimental import pallas as pl
from jax.experimental.pallas import tpu as pltpu
from jax.experimental.pallas import tpu_sc as plsc

F32 = jnp.float32
BF16 = jnp.bfloat16
HIGHEST = lax.Precision.HIGHEST

HGRN_DK = 128
HGRN_CHUNK = 64
ATTN_HEADS = 4
ATTN_HEAD_DIM = 64
ROT_DIM = ATTN_HEAD_DIM // 4
ROPE_THETA = 500000.0
MOBA_BLOCK = 256
MOBA_TOPK = 3
N_EXPERTS = 32
TOP_K = 4
SWIGLU_ALPHA = 1.702
SWIGLU_LIMIT = 7.0
N_MOD = 6
RMS_EPS = 1e-6

HGRN_SUB = 16
EXP_CLAMP = 80.0
PROJ_ROWS = 512
HGRN_ROWS = 1024
MERGE_ROWS = 4096
COMBINE_ROWS = 1024
MIX_ROWS = 1024
MOE_ROWS = 512
MOBA_ROWS = 256
MOBA_TILES_PER_STEP = 16
PART_W = 128
V7X_VMEM_BYTES = 64 * 1024 * 1024
VMEM_LIMIT = V7X_VMEM_BYTES * 7 // 8
SC_CORES = 2
SC_SUBCORES = 16


def _sigmoid(x):
    return 1.0 / (1.0 + jnp.exp(-x))


def _dot(a, b, **kw):
    return jnp.dot(a, b, preferred_element_type=F32, **kw)


def _dot_nt(a, b, **kw):
    return lax.dot_general(a, b, (((1,), (1,)), ((), ())), preferred_element_type=F32, **kw)


def _pack_bf16_pairs(x):
    w = x.shape[1] // 2
    bits = lax.bitcast_convert_type(x.astype(BF16).astype(F32), jnp.int32)
    return bits[:, w:] | lax.shift_right_logical(bits[:, :w], 16)


def _unpack_bf16_pairs(p):
    lo = lax.bitcast_convert_type(lax.shift_left(p, 16), F32)
    hi = lax.bitcast_convert_type(p & jnp.int32(-65536), F32)
    return jnp.concatenate([lo, hi], axis=1)


def _ada_kernel(c_ref, w_ref, b_ref, o_ref):
    c = c_ref[...]
    o_ref[...] = _dot(c * _sigmoid(c), w_ref[...], precision=HIGHEST) + b_ref[...]


def _ada(c, w_ada, b_ada):
    B, D = c.shape
    N = w_ada.shape[1]
    tn = N // 4
    c8 = jnp.zeros((8, D), F32).at[:B].set(c)
    out = pl.pallas_call(
        _ada_kernel,
        out_shape=jax.ShapeDtypeStruct((8, N), F32),
        grid=(N // tn,),
        in_specs=[pl.BlockSpec((8, D), lambda j: (0, 0)),
                  pl.BlockSpec((D, tn), lambda j: (0, j)),
                  pl.BlockSpec((1, tn), lambda j: (0, j))],
        out_specs=pl.BlockSpec((8, tn), lambda j: (0, j)),
        compiler_params=pltpu.CompilerParams(vmem_limit_bytes=VMEM_LIMIT),
        name="ada",
    )(c8, w_ada, b_ada.reshape(1, N))
    return out[:B]


def _proj_kernel(x_ref, sc_ref, sh_ref, g_ref, w_ref, lb_ref, ct_ref, st_ref,
                 q_ref, k_ref, lf_ref, v_ref, gt_ref, aq_ref, ak_ref, av_ref, km_ref,
                 *, hw, aw):
    x = x_ref[...]
    ms = jnp.mean(x * x, axis=-1, keepdims=True)
    h = x * lax.rsqrt(ms + RMS_EPS) * g_ref[...]
    h = h * (1.0 + sc_ref[...]) + sh_ref[...]
    proj = _dot(h.astype(BF16), w_ref[...])

    hq = proj[:, 0:hw]
    hf = proj[:, hw:2 * hw]
    hg = proj[:, 3 * hw:4 * hw]
    q_ref[...] = (hq * _sigmoid(hq) * (HGRN_DK ** -0.5)).astype(BF16)
    lb = lb_ref[...]
    f = lb + (1.0 - lb) * _sigmoid(hf)
    k_ref[...] = (1.0 - f).astype(BF16)
    lf_ref[...] = jnp.log(f)
    v_ref[...] = proj[:, 2 * hw:3 * hw].astype(BF16)
    gt_ref[...] = (hg * _sigmoid(hg)).astype(BF16)

    ct = jnp.concatenate([ct_ref[...]] * (aw // 128), axis=1)
    st = jnp.concatenate([st_ref[...]] * (aw // 128), axis=1)
    lane = lax.broadcasted_iota(jnp.int32, ct.shape, 1) % ATTN_HEAD_DIM
    first_half = lane < (ROT_DIM // 2)

    def rot(t):
        partner = jnp.where(first_half, pltpu.roll(t, aw - ROT_DIM // 2, 1), pltpu.roll(t, ROT_DIM // 2, 1))
        return t * ct + partner * st

    base = 4 * hw
    aq = rot(proj[:, base:base + aw])
    ak = rot(proj[:, base + aw:base + 2 * aw])
    av = proj[:, base + 2 * aw:base + 3 * aw]
    for blk in range(ak.shape[0] // MOBA_BLOCK):
        km_ref[blk] = jnp.mean(ak[blk * MOBA_BLOCK:(blk + 1) * MOBA_BLOCK], axis=0, keepdims=True)
    lane128 = lax.broadcasted_iota(jnp.int32, (x.shape[0], 128), 1)
    for pair in range(ATTN_HEADS // 2):
        aq_ref[pair] = aq[:, pair * 128:(pair + 1) * 128]
    for hd in range(ATTN_HEADS):
        pair, half = divmod(hd, 2)
        in_head = (lane128 // ATTN_HEAD_DIM) == half
        ak_ref[hd] = jnp.where(in_head, ak[:, pair * 128:(pair + 1) * 128], 0.0).astype(BF16)
        av_ref[hd] = av[:, hd * ATTN_HEAD_DIM:(hd + 1) * ATTN_HEAD_DIM].astype(BF16)


def _proj(x, scale1, shift1, norm_g, w_in_bf16, lb, ctab, stab, b0):
    _, S, D = x.shape
    B = scale1.shape[0]
    hw = lb.shape[-1]
    aw = ATTN_HEADS * ATTN_HEAD_DIM
    tm = PROJ_ROWS
    nb = S // MOBA_BLOCK
    n_proj = w_in_bf16.shape[1]
    row = lambda b, i: (b, i, 0)
    xrow = lambda b, i: (b0 + b, i, 0)
    vec = lambda b, i: (b, 0, 0)
    head = lambda b, i: (b, 0, i, 0)
    out_shapes = (
        jax.ShapeDtypeStruct((B, S, hw), BF16),
        jax.ShapeDtypeStruct((B, S, hw), BF16),
        jax.ShapeDtypeStruct((B, S, hw), F32),
        jax.ShapeDtypeStruct((B, S, hw), BF16),
        jax.ShapeDtypeStruct((B, S, hw), BF16),
        jax.ShapeDtypeStruct((B, ATTN_HEADS // 2, S, 128), F32),
        jax.ShapeDtypeStruct((B, ATTN_HEADS, S, 128), BF16),
        jax.ShapeDtypeStruct((B, ATTN_HEADS, S, ATTN_HEAD_DIM), BF16),
        jax.ShapeDtypeStruct((B, nb, 1, aw), F32),
    )
    hspec = pl.BlockSpec((None, tm, hw), row)
    aspec = pl.BlockSpec((None, ATTN_HEADS, tm, ATTN_HEAD_DIM), head)
    return pl.pallas_call(
        functools.partial(_proj_kernel, hw=hw, aw=aw),
        out_shape=out_shapes,
        grid=(B, S // tm),
        in_specs=[pl.BlockSpec((None, tm, D), xrow),
                  pl.BlockSpec((None, 1, D), vec),
                  pl.BlockSpec((None, 1, D), vec),
                  pl.BlockSpec((1, D), lambda b, i: (0, 0)),
                  pl.BlockSpec((D, n_proj), lambda b, i: (0, 0)),
                  pl.BlockSpec((1, hw), lambda b, i: (0, 0)),
                  pl.BlockSpec((None, tm, 128), xrow),
                  pl.BlockSpec((None, tm, 128), xrow)],
        out_specs=(hspec, hspec, hspec, hspec, hspec,
                   pl.BlockSpec((None, ATTN_HEADS // 2, tm, 128), head),
                   pl.BlockSpec((None, ATTN_HEADS, tm, 128), head), aspec,
                   pl.BlockSpec((None, tm // MOBA_BLOCK, 1, aw), lambda b, i: (b, i, 0, 0))),
        compiler_params=pltpu.CompilerParams(
            dimension_semantics=("arbitrary", "arbitrary"), vmem_limit_bytes=VMEM_LIMIT),
        name="proj",
    )(x, scale1, shift1, norm_g, w_in_bf16, lb, ctab, stab)


def _hgrn_kernel(q_ref, k_ref, lf_ref, v_ref, gt_ref, gn_ref, o_ref, st_ref, *, n_heads, n_chunks):
    @pl.when(pl.program_id(1) == 0)
    def _():
        st_ref[...] = jnp.zeros_like(st_ref)

    C = HGRN_CHUNK
    r = lax.broadcasted_iota(jnp.int32, (C, C), 0)
    c = lax.broadcasted_iota(jnp.int32, (C, C), 1)
    tril = c <= r
    ltri = tril.astype(F32)
    gn = gn_ref[...]

    def chunk(ci, carry):
        r0 = pl.multiple_of(ci * C, C)
        rows = pl.ds(r0, C)
        b_all = _dot(ltri, lf_ref[rows, :], precision=HIGHEST)
        heads = range(n_heads)
        sls = [slice(hd * HGRN_DK, (hd + 1) * HGRN_DK) for hd in heads]
        bs = [b_all[:, sl] for sl in sls]
        b_lasts = [b[C - 1:C, :] for b in bs]
        qs = [q_ref[rows, sl].astype(F32) for sl in sls]
        ks = [k_ref[rows, sl].astype(F32) for sl in sls]
        vs = [v_ref[rows, sl] for sl in sls]
        states = [st_ref[hd] for hd in heads]
        o_inter = [_dot_nt((qs[hd] * jnp.exp(bs[hd])).astype(BF16), states[hd].astype(BF16)) for hd in heads]
        scores = []
        for hd in heads:
            blocks = []
            for g0 in range(0, C, HGRN_SUB):
                g1 = g0 + HGRN_SUB
                rho = 0.5 * (bs[hd][g0:g0 + 1, :] + bs[hd][g1 - 1:g1, :])
                qa = qs[hd][g0:g1, :] * jnp.exp(jnp.minimum(bs[hd][g0:g1, :] - rho, EXP_CLAMP))
                kb = ks[hd] * jnp.exp(jnp.minimum(rho - bs[hd], EXP_CLAMP))
                blocks.append(_dot_nt(qa.astype(BF16), kb.astype(BF16)))
            scores.append(jnp.where(tril, jnp.concatenate(blocks, axis=0), 0.0).astype(BF16))
        outs = [o_inter[hd] + _dot(scores[hd], vs[hd]) for hd in heads]
        kds = [(ks[hd] * jnp.exp(b_lasts[hd] - bs[hd])).astype(BF16) for hd in heads]
        upds = [_dot(vs[hd].astype(F32).T.astype(BF16), kds[hd]) for hd in heads]
        for hd in heads:
            st_ref[hd] = states[hd] * jnp.exp(b_lasts[hd]) + upds[hd]
            o = outs[hd]
            ms = jnp.mean(o * o, axis=-1, keepdims=True)
            o_ref[rows, sls[hd]] = (o * lax.rsqrt(ms + RMS_EPS) * gn * gt_ref[rows, sls[hd]].astype(F32)).astype(BF16)
        return carry

    lax.fori_loop(0, n_chunks, chunk, 0, unroll=True)


def _hgrn(q, k, lf, v, gt, norm_g):
    B, S, hw = q.shape
    n_heads = hw // HGRN_DK
    tc = HGRN_ROWS
    spec = pl.BlockSpec((None, tc, hw), lambda b, i: (b, i, 0))
    return pl.pallas_call(
        functools.partial(_hgrn_kernel, n_heads=n_heads, n_chunks=tc // HGRN_CHUNK),
        out_shape=jax.ShapeDtypeStruct((B, S, hw), BF16),
        grid=(B, S // tc),
        in_specs=[spec, spec, spec, spec, spec, pl.BlockSpec((1, HGRN_DK), lambda b, i: (0, 0))],
        out_specs=spec,
        scratch_shapes=[pltpu.VMEM((n_heads, HGRN_DK, HGRN_DK), F32)],
        compiler_params=pltpu.CompilerParams(
            dimension_semantics=("arbitrary", "arbitrary"), vmem_limit_bytes=VMEM_LIMIT),
        name="hgrn",
    )(q, k, lf, v, gt, norm_g)


def _sc_move_rows(table, src, dst, n_out, chunk):
    M = src.shape[0]
    D = table.shape[1]
    n_workers = SC_CORES * SC_SUBCORES
    per_worker = M // n_workers
    n_chunks = per_worker // chunk
    assert per_worker * n_workers == M and n_chunks * chunk == per_worker and n_chunks % 2 == 0 and chunk % 8 == 0
    mesh = plsc.VectorSubcoreMesh(core_axis_name="c", subcore_axis_name="s")
    idx_t = pltpu.VMEM((chunk,), jnp.int32)
    row_t = pltpu.VMEM((chunk, D), table.dtype)
    sem_t = pltpu.SemaphoreType.DMA

    def body(table_hbm, src_hbm, dst_hbm, out_hbm, src_v, dst_v, rows_v, g_sem, s_sem):
        wid = lax.axis_index("s") * SC_CORES + lax.axis_index("c")
        base = wid * per_worker

        def offset(j):
            return pl.multiple_of(base + j * chunk, 8)

        def gather(b):
            return pltpu.make_async_copy(table_hbm.at[src_v[b]], rows_v[b], g_sem[b])

        def start_gather(j, b):
            pltpu.sync_copy(src_hbm.at[pl.ds(offset(j), chunk)], src_v[b])
            gather(b).start()

        def write_out(j, b):
            if dst_hbm is None:
                pltpu.sync_copy(rows_v[b], out_hbm.at[pl.ds(offset(j), chunk)])
            else:
                pltpu.sync_copy(dst_hbm.at[pl.ds(offset(j), chunk)], dst_v[b])
                pltpu.async_copy(rows_v[b], out_hbm.at[dst_v[b]], s_sem[b]).wait()

        start_gather(0, 0)

        @pl.loop(0, n_chunks, step=2)
        def _(j):
            for b in (0, 1):
                @pl.when(j + b + 1 < n_chunks)
                def _():
                    start_gather(j + b + 1, 1 - b)
                gather(b).wait()
                write_out(j + b, b)

    if dst is None:
        @functools.partial(pl.kernel, mesh=mesh, out_type=jax.ShapeDtypeStruct((n_out, D), table.dtype),
                           scratch_types=[idx_t, idx_t, row_t, row_t, sem_t, sem_t])
        def gather_kernel(table_hbm, src_hbm, out_hbm, s0, s1, r0, r1, g0, g1):
            body(table_hbm, src_hbm, None, out_hbm, (s0, s1), None, (r0, r1), (g0, g1), None)
        return gather_kernel(table, src)

    @functools.partial(pl.kernel, mesh=mesh, out_type=jax.ShapeDtypeStruct((n_out, D), table.dtype),
                       scratch_types=[idx_t, idx_t, idx_t, idx_t, row_t, row_t, sem_t, sem_t, sem_t, sem_t])
    def permute_kernel(table_hbm, src_hbm, dst_hbm, out_hbm, s0, s1, d0, d1, r0, r1, g0, g1, w0, w1):
        body(table_hbm, src_hbm, dst_hbm, out_hbm, (s0, s1), (d0, d1), (r0, r1), (g0, g1), (w0, w1))
    return permute_kernel(table, src, dst)


def _sc_gather(table, idx, chunk):
    return _sc_move_rows(table, idx, None, idx.shape[0], chunk)


def _sc_permute(table, src, dst, n_out, chunk):
    return _sc_move_rows(table, src, dst, n_out, chunk)


def _tile_layout(counts, bm, n_tiles):
    n_groups = counts.shape[0]
    padded = (counts + bm - 1) // bm * bm
    pad_end = jnp.cumsum(padded)
    tile_start = jnp.arange(n_tiles, dtype=jnp.int32) * bm
    tile_group = jnp.minimum(
        jnp.sum((pad_end[None, :] <= tile_start[:, None]).astype(jnp.int32), axis=1), n_groups - 1)
    n_used = (pad_end[-1] // bm).astype(jnp.int32).reshape(1)
    return pad_end - padded, tile_group.astype(jnp.int32), n_used


def _null_partial(rows):
    lane = lax.broadcasted_iota(jnp.int32, (rows, PART_W), 1)
    return jnp.where(lane < ATTN_HEAD_DIM, 0.0, -jnp.inf).astype(F32)


def _moba_sel_kernel(q_ref, km_ref, k_ref, v_ref, idx_ref, cnt_ref, own_ref, cnt_acc, *, n_blocks):
    j = pl.program_id(1)
    T = MOBA_BLOCK
    heads = range(ATTN_HEADS)
    qs = [q_ref[hd // 2] for hd in heads]
    gates = [_dot_nt(km_ref[hd], qs[hd], precision=HIGHEST) for hd in heads]
    blk = lax.broadcasted_iota(jnp.int32, gates[0].shape, 0)
    neg_inf = jnp.float32(-jnp.inf)
    gates = [jnp.where(blk < j, g, neg_inf) for g in gates]
    picks = [[] for _ in heads]
    for _ in range(MOBA_TOPK):
        ms = [jnp.max(g, axis=0, keepdims=True) for g in gates]
        firsts = [jnp.min(jnp.where(g == m, blk, n_blocks), axis=0, keepdims=True) for g, m in zip(gates, ms)]
        for hd in heads:
            picks[hd].append(jnp.where(ms[hd] > neg_inf, firsts[hd], -1))
        gates = [jnp.where(blk == f, neg_inf, g) for g, f in zip(gates, firsts)]

    @pl.when(j == 0)
    def _():
        cnt_acc[...] = jnp.zeros_like(cnt_acc)

    earlier = (lax.broadcasted_iota(jnp.int32, (T, T), 0) < lax.broadcasted_iota(jnp.int32, (T, T), 1)).astype(BF16)
    for hd in heads:
        onehots = [(blk == p).astype(F32) for p in picks[hd]]
        member = onehots[0] + onehots[1] + onehots[2]
        base = cnt_acc[hd] + _dot(member.astype(BF16), earlier)
        ranks = [jnp.sum(oh * base, axis=0, keepdims=True).astype(jnp.int32) for oh in onehots]
        idx_ref[hd] = jnp.concatenate(picks[hd] + ranks + [jnp.zeros((2, T), jnp.int32)], axis=0)
        total = cnt_acc[hd] + jnp.sum(member, axis=1, keepdims=True)
        cnt_acc[hd] = total
        cnt_ref[hd] = total.astype(jnp.int32)
    causal = lax.broadcasted_iota(jnp.int32, (T, T), 1) <= lax.broadcasted_iota(jnp.int32, (T, T), 0)
    scale = ATTN_HEAD_DIM ** -0.5
    ss = [jnp.where(causal, _dot_nt((qs[hd] * scale).astype(BF16), k_ref[hd]), neg_inf) for hd in heads]
    mx = [jnp.max(s, axis=1, keepdims=True) for s in ss]
    ps = [jnp.exp(s - m) for s, m in zip(ss, mx)]
    ls = [jnp.sum(p, axis=1, keepdims=True) for p in ps]
    accs = [_dot(ps[hd].astype(BF16), v_ref[hd]) for hd in heads]
    for hd in heads:
        lse = jnp.broadcast_to(mx[hd] + jnp.log(ls[hd]), (T, PART_W - ATTN_HEAD_DIM))
        own_ref[hd] = jnp.concatenate([accs[hd] / ls[hd], lse], axis=1)


def _moba_sel(aq, kmean, ak, av):
    B, H, S, hd = av.shape
    nb = S // MOBA_BLOCK
    T = MOBA_BLOCK
    blk = lambda b, j: (b, 0, j, 0)
    return pl.pallas_call(
        functools.partial(_moba_sel_kernel, n_blocks=nb),
        out_shape=(jax.ShapeDtypeStruct((B, H, 8, S), jnp.int32),
                   jax.ShapeDtypeStruct((B, H, nb, 1), jnp.int32),
                   jax.ShapeDtypeStruct((B, H, S, PART_W), F32)),
        grid=(B, nb),
        in_specs=[pl.BlockSpec((None, H // 2, T, 128), blk),
                  pl.BlockSpec((None, H, nb, 128), lambda b, j: (b, 0, 0, 0)),
                  pl.BlockSpec((None, H, T, 128), blk),
                  pl.BlockSpec((None, H, T, hd), blk)],
        out_specs=(pl.BlockSpec((None, H, 8, T), lambda b, j: (b, 0, 0, j)),
                   pl.BlockSpec((None, H, nb, 1), lambda b, j: (b, 0, 0, 0)),
                   pl.BlockSpec((None, H, T, PART_W), blk)),
        scratch_shapes=[pltpu.VMEM((H, nb, 1), F32)],
        compiler_params=pltpu.CompilerParams(
            dimension_semantics=("arbitrary", "arbitrary"), vmem_limit_bytes=VMEM_LIMIT),
        name="moba_sel",
    )(aq, kmean, ak, av)


def _moba_blk_kernel(tg_ref, nu_ref, q_ref, k_ref, v_ref, o_ref, *, n_blocks):
    n = MOBA_TILES_PER_STEP
    R = MOBA_ROWS
    t0 = pl.program_id(0) * n

    @pl.when(t0 < nu_ref[0])
    def _():
        scale = ATTN_HEAD_DIM ** -0.5
        groups = [tg_ref[t0 + j] for j in range(n)]
        kv_rows = [(g // n_blocks, pl.ds(pl.multiple_of((g % n_blocks) * MOBA_BLOCK, MOBA_BLOCK), MOBA_BLOCK))
                   for g in groups]
        ss = [_dot_nt((q_ref[j * R:(j + 1) * R, :] * scale).astype(BF16), k_ref[kv_rows[j][0], kv_rows[j][1], :])
              for j in range(n)]
        ms = [jnp.max(s, axis=1, keepdims=True) for s in ss]
        ps = [jnp.exp(s - m) for s, m in zip(ss, ms)]
        ls = [jnp.sum(p, axis=1, keepdims=True) for p in ps]
        accs = [_dot(p.astype(BF16), v_ref[kv_rows[j][0], kv_rows[j][1], :]) for j, p in enumerate(ps)]
        null = _null_partial(R)
        for j in range(n):
            lse = jnp.broadcast_to(ms[j] + jnp.log(ls[j]), (R, PART_W - ATTN_HEAD_DIM))
            row = jnp.concatenate([accs[j] / ls[j], lse], axis=1)
            o_ref[j * R:(j + 1) * R, :] = jnp.where(t0 + j < nu_ref[0], row, null)

    @pl.when(t0 >= nu_ref[0])
    def _():
        o_ref[...] = _null_partial(n * R)


def _moba_blk(qs, tile_group, n_used, ak, av):
    B, H, S, hd = av.shape
    nb = S // MOBA_BLOCK
    R = MOBA_ROWS
    n = MOBA_TILES_PER_STEP
    n_tiles = qs.shape[0] // R
    assert n_tiles % n == 0
    whole = lambda i, tg, nu: (0, 0, 0)
    grid_spec = pltpu.PrefetchScalarGridSpec(
        num_scalar_prefetch=2,
        grid=(n_tiles // n,),
        in_specs=[pl.BlockSpec((n * R, 128), lambda i, tg, nu: (i, 0)),
                  pl.BlockSpec((B * H, S, 128), whole, pipeline_mode=pl.Buffered(1)),
                  pl.BlockSpec((B * H, S, hd), whole, pipeline_mode=pl.Buffered(1))],
        out_specs=pl.BlockSpec((n * R, PART_W), lambda i, tg, nu: (i, 0)),
    )
    return pl.pallas_call(
        functools.partial(_moba_blk_kernel, n_blocks=nb),
        out_shape=jax.ShapeDtypeStruct((n_tiles * R, PART_W), F32),
        grid_spec=grid_spec,
        compiler_params=pltpu.CompilerParams(
            dimension_semantics=("arbitrary",), vmem_limit_bytes=VMEM_LIMIT),
        name="moba_blk",
    )(tile_group, n_used, qs, ak.reshape(B * H, S, 128), av.reshape(B * H, S, hd))


def _moba_merge_kernel(own_ref, pg_ref, g_ref, o_ref):
    hd = ATTN_HEAD_DIM
    rows = [own_ref[...]] + [pg_ref[s] for s in range(MOBA_TOPK)]
    lses = [pltpu.roll(r, hd, 1) for r in rows]
    top = lses[0]
    for z in lses[1:]:
        top = jnp.maximum(top, z)
    num = jnp.zeros_like(top)
    den = jnp.zeros_like(top)
    for r, z in zip(rows, lses):
        w = jnp.exp(z - top)
        num = num + w * r
        den = den + w
    o = (num / den)[:, :hd]
    ms = jnp.mean(o * o, axis=-1, keepdims=True)
    o_ref[...] = o * lax.rsqrt(ms + RMS_EPS) * g_ref[...]


def _moba_merge(own, pg, norm_g):
    n = own.shape[0]
    T = MERGE_ROWS
    row = lambda i: (i, 0)
    return pl.pallas_call(
        _moba_merge_kernel,
        out_shape=jax.ShapeDtypeStruct((n, ATTN_HEAD_DIM), F32),
        grid=(n // T,),
        in_specs=[pl.BlockSpec((T, PART_W), row),
                  pl.BlockSpec((MOBA_TOPK, T, PART_W), lambda i: (0, i, 0)),
                  pl.BlockSpec((1, ATTN_HEAD_DIM), lambda i: (0, 0))],
        out_specs=pl.BlockSpec((T, ATTN_HEAD_DIM), row),
        compiler_params=pltpu.CompilerParams(
            dimension_semantics=("arbitrary",), vmem_limit_bytes=VMEM_LIMIT),
        name="moba_merge",
    )(own, pg, norm_g)


def _moba(aq, km, ak, av, norm_g):
    B, H, S, hd = av.shape
    nb = S // MOBA_BLOCK
    n_q = B * H * S
    kmp = km.reshape(B, nb, H // 2, 128)
    half = jnp.arange(128, dtype=jnp.int32) // hd
    kmean = jnp.stack([jnp.where(half == h % 2, kmp[:, :, h // 2, :], 0.0) for h in range(H)], axis=1)
    idx8, counts, own = _moba_sel(aq, kmean, ak, av)
    sel = idx8[:, :, 0:MOBA_TOPK, :].reshape(B * H, MOBA_TOPK, S)
    rank = idx8[:, :, MOBA_TOPK:2 * MOBA_TOPK, :].reshape(B * H, MOBA_TOPK, S)
    n_groups = B * H * nb
    n_tiles = (n_q * MOBA_TOPK) // MOBA_ROWS + n_groups
    pad_start, tile_group, n_used = _tile_layout(counts.reshape(-1), MOBA_ROWS, n_tiles)
    blocks = jnp.arange(nb, dtype=jnp.int32)[:, None]
    start = jnp.sum(jnp.where(sel[:, :, None, :] == blocks, pad_start.reshape(B * H, 1, nb, 1), 0), axis=2)
    a_ids = jnp.arange(n_q * MOBA_TOPK, dtype=jnp.int32).reshape(B * H, MOBA_TOPK, S)
    assert n_tiles * MOBA_ROWS >= n_q * MOBA_TOPK + MOBA_ROWS
    pos = jnp.where(sel >= 0, start + rank, n_used[0] * MOBA_ROWS + a_ids % MOBA_ROWS)
    bh = jnp.arange(B * H, dtype=jnp.int32)[:, None, None]
    t = jnp.arange(S, dtype=jnp.int32)[None, None, :]
    pair_row = jnp.broadcast_to((bh // H * (H // 2) + bh % H // 2) * S + t, pos.shape)
    qs = _sc_permute(aq.reshape(B * (H // 2) * S, 128), pair_row.reshape(-1), pos.reshape(-1),
                     n_tiles * MOBA_ROWS, 256)
    parts = _moba_blk(qs, tile_group, n_used, ak, av)
    pg = _sc_gather(parts, pos.transpose(1, 0, 2).reshape(-1), 256)
    o = _moba_merge(own.reshape(n_q, PART_W), pg.reshape(MOBA_TOPK, n_q, PART_W), norm_g)
    return o.reshape(B, H, S, hd)


def _mix_kernel(oh_ref, oa_ref, x_ref, g1_ref, sc2_ref, sh2_ref, n2_ref, wo_ref, wr_ref, br_ref,
                x1_ref, h2_ref, gw_ref, idx_ref, cnt_ref, cnt_acc):
    cat = jnp.concatenate([oh_ref[...]] + [oa_ref[hd] for hd in range(ATTN_HEADS)], axis=1)
    mix = _dot(cat.astype(BF16), wo_ref[...])
    x1 = x_ref[...] + g1_ref[...] * mix
    x1_ref[...] = x1
    ms = jnp.mean(x1 * x1, axis=-1, keepdims=True)
    h2 = x1 * lax.rsqrt(ms + RMS_EPS) * n2_ref[...]
    h2 = h2 * (1.0 + sc2_ref[...]) + sh2_ref[...]
    h2_ref[...] = _pack_bf16_pairs(h2)
    E = N_EXPERTS
    tm = h2.shape[0]
    h_0 = h2.astype(BF16)
    r_1 = h2 - h_0.astype(F32)
    h_1 = r_1.astype(BF16)
    h_2 = (r_1 - h_1.astype(F32)).astype(BF16)
    wt = wr_ref[...]
    p_0 = _dot_nt(wt, h_0)
    p_1 = _dot_nt(wt[:2 * E], h_1)
    p_2 = _dot_nt(wt[:E], h_2)
    logits = (p_0[:E] + (p_0[E:2 * E] + p_1[:E]) + (p_0[2 * E:] + p_1[E:] + p_2)) + br_ref[...]
    ex = lax.broadcasted_iota(jnp.int32, logits.shape, 0)
    neg_inf = jnp.float32(-jnp.inf)
    vals, idxs = [], []
    for _ in range(TOP_K):
        m = jnp.max(logits, axis=0, keepdims=True)
        first = jnp.min(jnp.where(logits == m, ex, E), axis=0, keepdims=True)
        vals.append(m)
        idxs.append(first)
        logits = jnp.where(ex == first, neg_inf, logits)
    e = [jnp.exp(v - vals[0]) for v in vals]
    denom = e[0] + e[1] + e[2] + e[3]
    gate_rows = jnp.concatenate([ei / denom for ei in e] + [jnp.zeros((128 - TOP_K, tm), F32)], axis=0)
    gw_ref[...] = gate_rows.T[:, :TOP_K]

    @pl.when((pl.program_id(0) == 0) & (pl.program_id(1) == 0))
    def _():
        cnt_acc[...] = jnp.zeros_like(cnt_acc)

    earlier = (lax.broadcasted_iota(jnp.int32, (tm, tm), 0) < lax.broadcasted_iota(jnp.int32, (tm, tm), 1)).astype(BF16)
    onehots = [(ex == ix).astype(F32) for ix in idxs]
    member = onehots[0] + onehots[1] + onehots[2] + onehots[3]
    base = cnt_acc[...] + _dot(member.astype(BF16), earlier)
    ranks = [jnp.sum(oh * base, axis=0, keepdims=True).astype(jnp.int32) for oh in onehots]
    idx_ref[...] = jnp.concatenate(idxs + ranks, axis=0)
    total = cnt_acc[...] + jnp.sum(member, axis=1, keepdims=True)
    cnt_acc[...] = total
    cnt_ref[...] = total.astype(jnp.int32)


def _mix(oh, oa, x, gate1, scale2, shift2, norm2_g, w_out_bf16, w_router, b_router, b0):
    _, S, D = x.shape
    B = oh.shape[0]
    hw = oh.shape[-1]
    tm = MIX_ROWS
    row = lambda b, i: (b, i, 0)
    xrow = lambda b, i: (b0 + b, i, 0)
    vec = lambda b, i: (b, 0, 0)
    const = lambda b, i: (0, 0)
    return pl.pallas_call(
        _mix_kernel,
        out_shape=(jax.ShapeDtypeStruct((B, S, D), F32),
                   jax.ShapeDtypeStruct((B, S, D // 2), jnp.int32),
                   jax.ShapeDtypeStruct((B, S, TOP_K), F32),
                   jax.ShapeDtypeStruct((B, 2 * TOP_K, S), jnp.int32),
                   jax.ShapeDtypeStruct((N_EXPERTS, 1), jnp.int32)),
        grid=(B, S // tm),
        in_specs=[pl.BlockSpec((None, tm, hw), row),
                  pl.BlockSpec((None, ATTN_HEADS, tm, ATTN_HEAD_DIM), lambda b, i: (b, 0, i, 0)),
                  pl.BlockSpec((None, tm, D), xrow),
                  pl.BlockSpec((None, 1, D), vec),
                  pl.BlockSpec((None, 1, D), vec),
                  pl.BlockSpec((None, 1, D), vec),
                  pl.BlockSpec((1, D), const),
                  pl.BlockSpec((D, D), const),
                  pl.BlockSpec((3 * N_EXPERTS, D), const),
                  pl.BlockSpec((N_EXPERTS, 1), const)],
        out_specs=(pl.BlockSpec((None, tm, D), row),
                   pl.BlockSpec((None, tm, D // 2), row),
                   pl.BlockSpec((None, tm, TOP_K), row),
                   pl.BlockSpec((None, 2 * TOP_K, tm), lambda b, i: (b, 0, i)),
                   pl.BlockSpec((N_EXPERTS, 1), const)),
        scratch_shapes=[pltpu.VMEM((N_EXPERTS, 1), F32)],
        compiler_params=pltpu.CompilerParams(
            dimension_semantics=("arbitrary", "arbitrary"), vmem_limit_bytes=VMEM_LIMIT),
        name="mix",
    )(oh, oa, x, gate1, scale2, shift2, norm2_g, w_out_bf16, w_router, b_router)


def _moe_rows_kernel(be_ref, nx_ref, sl_ref, nu_ref, x_ref, wgu_hbm, bgu_ref, wd_hbm, bd_ref, y_ref,
                     wgu32, wd32, sem, *, d_ff):
    i = pl.program_id(0)
    e = be_ref[i]
    s = sl_ref[i]

    def fetch(expert, slot):
        return (pltpu.make_async_copy(wgu_hbm.at[expert], wgu32.at[slot], sem.at[slot, 0]),
                pltpu.make_async_copy(wd_hbm.at[expert], wd32.at[slot], sem.at[slot, 1]))

    @pl.when(i == 0)
    def _():
        for c in fetch(e, s):
            c.start()

    @pl.when((i == 0) | (e != be_ref[jnp.maximum(i - 1, 0)]))
    def _():
        for c in fetch(e, s):
            c.wait()

        @pl.when(nx_ref[i] >= 0)
        def _():
            for c in fetch(nx_ref[i], 1 - s):
                c.start()

    @pl.when(i < nu_ref[0])
    def _():
        xb = _unpack_bf16_pairs(x_ref[...]).astype(BF16)
        w = d_ff // 4
        acc = bd_ref[...]
        for h in range(4):
            g = _dot(xb, wgu32[s, :, h * w:(h + 1) * w].astype(BF16)) + bgu_ref[:, h * w:(h + 1) * w]
            u = (_dot(xb, wgu32[s, :, d_ff + h * w:d_ff + (h + 1) * w].astype(BF16))
                 + bgu_ref[:, d_ff + h * w:d_ff + (h + 1) * w])
            gate = jnp.minimum(g, SWIGLU_LIMIT)
            up = jnp.clip(u, -SWIGLU_LIMIT, SWIGLU_LIMIT)
            act = (up + 1.0) * gate * _sigmoid(SWIGLU_ALPHA * gate)
            acc = acc + _dot(act.astype(BF16), wd32[s, h * w:(h + 1) * w, :].astype(BF16))
        y_ref[...] = _pack_bf16_pairs(acc)

    @pl.when(i >= nu_ref[0])
    def _():
        y_ref[...] = jnp.zeros_like(y_ref)


def _moe_rows(xs, blk_expert, n_used, wgu, bgu, wd, bd):
    D = 2 * xs.shape[1]
    bm = MOE_ROWS
    n_blk = xs.shape[0] // bm
    d_ff = wd.shape[1]
    run_end = jnp.sum((blk_expert[None, :] <= blk_expert[:, None]).astype(jnp.int32), axis=1)
    next_expert = jnp.where(run_end < n_blk, blk_expert[jnp.minimum(run_end, n_blk - 1)], -1).astype(jnp.int32)
    changes = jnp.concatenate([jnp.zeros((1,), jnp.int32), (blk_expert[1:] != blk_expert[:-1]).astype(jnp.int32)])
    slot = (jnp.cumsum(changes) % 2).astype(jnp.int32)
    bsel = lambda i, be, nx, sl, nu: (be[i], 0, 0)
    rows = lambda i, be, nx, sl, nu: (i, 0)
    grid_spec = pltpu.PrefetchScalarGridSpec(
        num_scalar_prefetch=4,
        grid=(n_blk,),
        in_specs=[pl.BlockSpec((bm, D // 2), rows),
                  pl.BlockSpec(memory_space=pl.ANY),
                  pl.BlockSpec((None, 1, 2 * d_ff), bsel),
                  pl.BlockSpec(memory_space=pl.ANY),
                  pl.BlockSpec((None, 1, D), bsel)],
        out_specs=pl.BlockSpec((bm, D // 2), rows),
        scratch_shapes=[pltpu.VMEM((2, D, 2 * d_ff), F32), pltpu.VMEM((2, d_ff, D), F32),
                        pltpu.SemaphoreType.DMA((2, 2))],
    )
    return pl.pallas_call(
        functools.partial(_moe_rows_kernel, d_ff=d_ff),
        out_shape=jax.ShapeDtypeStruct((n_blk * bm, D // 2), jnp.int32),
        grid_spec=grid_spec,
        compiler_params=pltpu.CompilerParams(
            dimension_semantics=("arbitrary",), vmem_limit_bytes=VMEM_LIMIT),
        name="moe_rows",
    )(blk_expert, next_expert, slot, n_used, xs, wgu, bgu.reshape(N_EXPERTS, 1, 2 * d_ff), wd, bd.reshape(N_EXPERTS, 1, D))


def _combine_rows_kernel(*refs):
    y_refs = refs[:TOP_K]
    gw_ref, x1_ref, g2_ref, fg_ref = refs[TOP_K:TOP_K + 4]
    o_ref = refs[-1]
    gw = gw_ref[...]
    y = gw[:, 0:1] * _unpack_bf16_pairs(y_refs[0][...])
    for kk in range(1, TOP_K):
        y = y + gw[:, kk:kk + 1] * _unpack_bf16_pairs(y_refs[kk][...])
    x2 = x1_ref[...] + g2_ref[...] * y
    ms = jnp.mean(x2 * x2, axis=-1, keepdims=True)
    o_ref[...] = x2 * lax.rsqrt(ms + RMS_EPS) * fg_ref[...]


def _combine_rows(yg, gates, x1, gate2, final_g, out_so_far, b0, n_batches):
    S, D = x1.shape
    tm = COMBINE_ROWS
    steps = S // tm
    slot_spec = lambda kk: pl.BlockSpec((tm, D // 2), lambda i: (kk * steps + i, 0))
    in_specs = [slot_spec(kk) for kk in range(TOP_K)] + [
        pl.BlockSpec((tm, TOP_K), lambda i: (i, 0)),
        pl.BlockSpec((tm, D), lambda i: (i, 0)),
        pl.BlockSpec((1, D), lambda i: (0, 0)),
        pl.BlockSpec((1, D), lambda i: (0, 0))]
    args = [yg] * TOP_K + [gates, x1, gate2, final_g]
    aliases = {}
    if out_so_far is not None:
        in_specs.append(pl.BlockSpec(memory_space=pl.ANY))
        aliases = {len(args): 0}
        args.append(out_so_far)
    return pl.pallas_call(
        _combine_rows_kernel,
        out_shape=jax.ShapeDtypeStruct((n_batches * S, D), F32),
        grid=(steps,),
        in_specs=in_specs,
        out_specs=pl.BlockSpec((tm, D), lambda i: (b0 * steps + i, 0)),
        input_output_aliases=aliases,
        compiler_params=pltpu.CompilerParams(
            dimension_semantics=("arbitrary",), vmem_limit_bytes=VMEM_LIMIT),
        name="combine_rows",
    )(*args)


def _split_bf16x3(w):
    def top(v):
        return lax.bitcast_convert_type(lax.bitcast_convert_type(v, jnp.int32) & jnp.int32(-65536), F32)
    w0 = top(w)
    w1 = top(w - w0)
    w2 = w - w0 - w1
    return jnp.concatenate([w0, w1, w2], axis=1).astype(BF16).T


def _rotary_tables(positions):
    B, S = positions.shape
    half = ROT_DIM // 2
    inv_freq = jnp.exp(-math.log(ROPE_THETA) * jnp.arange(0, ROT_DIM, 2, dtype=F32) / ROT_DIM)
    ang = inv_freq[:, None] * positions.astype(F32).reshape(1, B * S)
    d = jnp.arange(128, dtype=jnp.int32) % ATTN_HEAD_DIM
    rotary = d < ROT_DIM
    sel = ((d[None, :] % half == jnp.arange(half, dtype=jnp.int32)[:, None]) & rotary[None, :]).astype(F32)
    sign = jnp.where(d < half, -1.0, 1.0)
    spread = lambda t, w: lax.dot_general(t, w, (((0,), (0,)), ((), ())), precision=HIGHEST)
    ct = spread(jnp.cos(ang), sel) + jnp.where(rotary, 0.0, 1.0)
    st = spread(jnp.sin(ang), sel * sign)
    return ct.reshape(B, S, 128), st.reshape(B, S, 128)


def kernel(x, c, positions, w_ada, b_ada, norm1_g, w_in, hgrn_lb_logits, hgrn_norm_g, attn_norm_g,
           w_out, norm2_g, w_router, b_router, w_gate_up, b_gate_up, w_down, b_down, final_norm_g):
    B, S, D = x.shape
    assert w_in.shape[0] == 1, "single-layer block: the final norm is fused into the combine step"
    l = 0
    ctab, stab = _rotary_tables(positions)
    lower_bounds = jnp.cumsum(jax.nn.softmax(hgrn_lb_logits.astype(F32), axis=0), axis=0)
    mod = _ada(c, w_ada[l], b_ada[l])
    shift1, scale1, gate1, shift2, scale2, gate2 = jnp.split(mod[:, None, :], N_MOD, axis=-1)
    n_blk = (S * TOP_K) // MOE_ROWS + N_EXPERTS
    experts = jnp.arange(N_EXPERTS, dtype=jnp.int32)[:, None]
    tok = jnp.broadcast_to(jnp.arange(S, dtype=jnp.int32)[None, :], (TOP_K, S)).reshape(-1)
    out = None
    for b in range(B):
        one = slice(b, b + 1)
        q, k, lf, v, gt, aq, ak, av, km = _proj(
            x, scale1[one], shift1[one], norm1_g[l][None], w_in[l].astype(BF16), lower_bounds[l][None],
            ctab, stab, b)
        o_a = _moba(aq, km, ak, av, attn_norm_g[l][None])
        o_h = _hgrn(q, k, lf, v, gt, hgrn_norm_g[l][None])
        x1, h2, gates, idx8, counts = _mix(
            o_h, o_a, x, gate1[one], scale2[one], shift2[one], norm2_g[l][None], w_out[l].astype(BF16),
            _split_bf16x3(w_router[l]), b_router[l][:, None], b)
        pad_start, blk_expert, n_used = _tile_layout(counts.reshape(-1), MOE_ROWS, n_blk)
        chosen = idx8[0, 0:TOP_K, :]
        pos = jnp.sum(jnp.where(chosen[:, None, :] == experts, pad_start[:, None], 0), axis=1) + idx8[0, TOP_K:, :]
        xs = _sc_permute(h2.reshape(S, D // 2), tok, pos.reshape(-1), n_blk * MOE_ROWS, 64)
        y_sorted = _moe_rows(xs, blk_expert, n_used, w_gate_up[l], b_gate_up[l], w_down[l], b_down[l])
        yg = _sc_gather(y_sorted, pos.reshape(-1), 64)
        out = _combine_rows(yg, gates.reshape(S, TOP_K), x1.reshape(S, D), gate2[b], final_norm_g[None], out, b, B)
    return out.reshape(B, S, D)
```
